```python
import math
import jax, jax.numpy as jnp
from jax import lax
import numpy as np

D_MODEL = 1024
BATCH = 16
SEQ = 4096
DEPTH = 1

MEM_LEN = 256
EPS = 1e-6
GM_WIDTH = D_MODEL
GM_CHUNK = 128
GM_GROUP_CH = 128
GM_GROUPS = GM_WIDTH // GM_GROUP_CH
S5_WIDTH = D_MODEL // 2
S5_GROUP_CH = 16
S5_GROUPS = S5_WIDTH // S5_GROUP_CH
S5_STATE = 64
CA_HEADS = 4
CA_HEAD_DIM = D_MODEL // CA_HEADS
FFN_HIDDEN = ((8 * D_MODEL + 3 * 256 - 1) // (3 * 256)) * 256
IN_COLS = 2 * GM_WIDTH + S5_WIDTH + 2 * D_MODEL
SPLIT_POINTS = (2 * GM_WIDTH, 2 * GM_WIDTH + S5_WIDTH, 2 * GM_WIDTH + S5_WIDTH + D_MODEL)

kernel_name = 'hybrid_gmlp_s5_memxattn_block'


def rms_norm(x, g):
    xf = x.astype(jnp.float32)
    y = xf * lax.rsqrt(jnp.mean(xf * xf, axis=-1, keepdims=True) + EPS)
    return (y * g.astype(jnp.float32)).astype(x.dtype)


def layer_norm(x, g, b):
    xf = x.astype(jnp.float32)
    mu = jnp.mean(xf, axis=-1, keepdims=True)
    xc = xf - mu
    y = xc * lax.rsqrt(jnp.mean(xc * xc, axis=-1, keepdims=True) + EPS)
    return (y * g.astype(jnp.float32) + b.astype(jnp.float32)).astype(x.dtype)


def gmlp_branch(z, ln_g, ln_b, w_s, b_s):
    bsz, seq, _ = z.shape
    z = jax.nn.gelu(z)
    u, v = jnp.split(z, 2, axis=-1)
    v = layer_norm(v, ln_g, ln_b)
    v = v.reshape(bsz, seq // GM_CHUNK, GM_CHUNK, GM_GROUPS, GM_GROUP_CH)
    mask = jnp.tril(jnp.ones((GM_CHUNK, GM_CHUNK), dtype=bool))
    w = jnp.where(mask[None], w_s, jnp.zeros((), w_s.dtype))
    sv = jnp.einsum('gts,bnsgc->bntgc', w, v) + b_s.T[:, :, None]
    return u * sv.reshape(bsz, seq, GM_WIDTH)


def _ssm_combine(e1, e2):
    a1r, a1i, b1r, b1i = e1
    a2r, a2i, b2r, b2i = e2
    ar = a2r * a1r - a2i * a1i
    ai = a2r * a1i + a2i * a1r
    br = a2r * b1r - a2i * b1i + b2r
    bi = a2r * b1i + a2i * b1r + b2i
    return (ar, ai, br, bi)


def s5_branch(u, lam_re, lam_im, log_step, b_re, b_im, c_re, c_im, d, w_glu):
    f32 = jnp.float32
    dt = u.dtype
    bsz, seq, _ = u.shape
    uf = u.astype(f32).reshape(bsz, seq, S5_GROUPS, S5_GROUP_CH)
    lr = lam_re.astype(f32)
    li = lam_im.astype(f32)
    step = jnp.exp(log_step.astype(f32))[:, None]
    mag = jnp.exp(lr * step)
    ab_re = mag * jnp.cos(li * step)
    ab_im = mag * jnp.sin(li * step)
    den = lr * lr + li * li
    nr = ab_re - 1.0
    co_re = (nr * lr + ab_im * li) / den
    co_im = (ab_im * lr - nr * li) / den
    br_ = b_re.astype(f32)
    bi_ = b_im.astype(f32)
    bb_re = co_re[..., None] * br_ - co_im[..., None] * bi_
    bb_im = co_re[..., None] * bi_ + co_im[..., None] * br_
    bu_re = jnp.einsum('bsgh,gph->bsgp', uf, bb_re)
    bu_im = jnp.einsum('bsgh,gph->bsgp', uf, bb_im)
    a_re = jnp.broadcast_to(ab_re, (seq, S5_GROUPS, S5_STATE))
    a_im = jnp.broadcast_to(ab_im, (seq, S5_GROUPS, S5_STATE))

    def scan_one(br, bi):
        _, _, sr, si = lax.associative_scan(_ssm_combine, (a_re, a_im, br, bi), axis=0)
        return sr, si

    s_re, s_im = jax.vmap(scan_one)(bu_re, bu_im)
    y = (jnp.einsum('bsgp,ghp->bsgh', s_re, c_re.astype(f32))
         - jnp.einsum('bsgp,ghp->bsgh', s_im, c_im.astype(f32))
         + d.astype(f32) * uf)
    y = jax.nn.gelu(y.reshape(bsz, seq, S5_WIDTH))
    y = y * jax.nn.sigmoid(y @ w_glu.astype(f32))
    return y.astype(dt)


def cross_attention(h, mem_n, w_q, w_kv, w_o):
    bsz, seq, _ = h.shape
    q = (h @ w_q).reshape(bsz, seq, CA_HEADS, CA_HEAD_DIM)
    k, v = jnp.split(mem_n @ w_kv, 2, axis=-1)
    k = k.reshape(bsz, -1, CA_HEADS, CA_HEAD_DIM)
    v = v.reshape(bsz, -1, CA_HEADS, CA_HEAD_DIM)
    s = jnp.einsum('bshd,bmhd->bhsm', q, k).astype(jnp.float32) * (CA_HEAD_DIM ** -0.5)
    p = jax.nn.softmax(s, axis=-1).astype(h.dtype)
    o = jnp.einsum('bhsm,bmhd->bshd', p, v).reshape(bsz, seq, D_MODEL)
    return o @ w_o


def swiglu(h, w_gu, w_down):
    g, u = jnp.split(h @ w_gu, 2, axis=-1)
    return (jax.nn.silu(g) * u) @ w_down


def _fwd_setup_inputs(seed: int = 0) -> dict:
    key = jax.random.key(seed)
    ks = jax.random.split(key, 32)
    L = DEPTH
    f32 = jnp.float32

    def nrm(k, shape, scale):
        return jax.random.normal(k, shape, f32) * scale

    def gain(k, shape):
        return 1.0 + 0.01 * jax.random.normal(k, shape, f32)

    lam_re = -0.5 * jnp.exp(0.05 * jax.random.normal(ks[8], (L, S5_GROUPS, S5_STATE), f32))
    lam_im = (math.pi * jnp.arange(S5_STATE, dtype=f32))[None, None, :] + 0.01 * jax.random.normal(ks[9], (L, S5_GROUPS, S5_STATE), f32)
    log_step = jax.random.uniform(ks[10], (L, S5_GROUPS), f32, math.log(1e-3), math.log(1e-1))
    return {
        'x': jax.random.normal(ks[0], (BATCH, SEQ, D_MODEL), f32),
        'mem': jax.random.normal(ks[1], (BATCH, MEM_LEN, D_MODEL), f32),
        'g_mix_pre': gain(ks[2], (L, D_MODEL)),
        'w_in': nrm(ks[3], (L, D_MODEL, IN_COLS), D_MODEL ** -0.5),
        'gm_ln_g': gain(ks[4], (L, GM_WIDTH)),
        'gm_ln_b': nrm(ks[5], (L, GM_WIDTH), 0.01),
        'gm_w_s': nrm(ks[6], (L, GM_GROUPS, GM_CHUNK, GM_CHUNK), GM_CHUNK ** -0.5),
        'gm_b_s': gain(ks[7], (L, GM_GROUPS, GM_CHUNK)),
        's5_lam_re': lam_re,
        's5_lam_im': lam_im,
        's5_log_step': log_step,
        's5_b_re': nrm(ks[11], (L, S5_GROUPS, S5_STATE, S5_GROUP_CH), (2 * S5_GROUP_CH) ** -0.5),
        's5_b_im': nrm(ks[12], (L, S5_GROUPS, S5_STATE, S5_GROUP_CH), (2 * S5_GROUP_CH) ** -0.5),
        's5_c_re': nrm(ks[13], (L, S5_GROUPS, S5_GROUP_CH, S5_STATE), (2 * S5_STATE) ** -0.5),
        's5_c_im': nrm(ks[14], (L, S5_GROUPS, S5_GROUP_CH, S5_STATE), (2 * S5_STATE) ** -0.5),
        's5_d': nrm(ks[15], (L, S5_GROUPS, S5_GROUP_CH), 1.0),
        's5_w_glu': nrm(ks[16], (L, S5_WIDTH, S5_WIDTH), S5_WIDTH ** -0.5),
        'w_br_gm': nrm(ks[17], (L, GM_WIDTH, D_MODEL), GM_WIDTH ** -0.5),
        'w_br_s5': nrm(ks[18], (L, S5_WIDTH, D_MODEL), S5_WIDTH ** -0.5),
        'w_mix_out': nrm(ks[19], (L, D_MODEL, D_MODEL), D_MODEL ** -0.5),
        'g_mix_post': gain(ks[20], (L, D_MODEL)),
        'g_ca_pre': gain(ks[21], (L, D_MODEL)),
        'g_mem': gain(ks[22], (L, D_MODEL)),
        'ca_w_q': nrm(ks[23], (L, D_MODEL, D_MODEL), D_MODEL ** -0.5),
        'ca_w_kv': nrm(ks[24], (L, D_MODEL, 2 * D_MODEL), D_MODEL ** -0.5),
        'ca_w_o': nrm(ks[25], (L, D_MODEL, D_MODEL), D_MODEL ** -0.5),
        'g_ca_post': gain(ks[26], (L, D_MODEL)),
        'g_ffn_pre': gain(ks[27], (L, D_MODEL)),
        'ffn_w_gu': nrm(ks[28], (L, D_MODEL, 2 * FFN_HIDDEN), D_MODEL ** -0.5),
        'ffn_w_down': nrm(ks[29], (L, FFN_HIDDEN, D_MODEL), FFN_HIDDEN ** -0.5),
        'g_ffn_post': gain(ks[30], (L, D_MODEL)),
    }


def _fwd_reference(x, mem, g_mix_pre, w_in, gm_ln_g, gm_ln_b, gm_w_s, gm_b_s,
              s5_lam_re, s5_lam_im, s5_log_step, s5_b_re, s5_b_im, s5_c_re, s5_c_im,
              s5_d, s5_w_glu, w_br_gm, w_br_s5, w_mix_out, g_mix_post,
              g_ca_pre, g_mem, ca_w_q, ca_w_kv, ca_w_o, g_ca_post,
              g_ffn_pre, ffn_w_gu, ffn_w_down, g_ffn_post):
    for l in range(DEPTH):
        h = rms_norm(x, g_mix_pre[l])
        z = h @ w_in[l]
        z_gm, z_s5, z_ga, z_gb = jnp.split(z, SPLIT_POINTS, axis=-1)
        y_gm = gmlp_branch(z_gm, gm_ln_g[l], gm_ln_b[l], gm_w_s[l], gm_b_s[l])
        y_s5 = s5_branch(z_s5, s5_lam_re[l], s5_lam_im[l], s5_log_step[l],
                         s5_b_re[l], s5_b_im[l], s5_c_re[l], s5_c_im[l], s5_d[l], s5_w_glu[l])
        merged = (jax.nn.sigmoid(z_ga) * (y_gm @ w_br_gm[l])
                  + jax.nn.sigmoid(z_gb) * (y_s5 @ w_br_s5[l]))
        x = x + rms_norm(merged @ w_mix_out[l], g_mix_post[l])
        hc = rms_norm(x, g_ca_pre[l])
        mem_n = rms_norm(mem, g_mem[l])
        x = x + rms_norm(cross_attention(hc, mem_n, ca_w_q[l], ca_w_kv[l], ca_w_o[l]), g_ca_post[l])
        hf = rms_norm(x, g_ffn_pre[l])
        x = x + rms_norm(swiglu(hf, ffn_w_gu[l], ffn_w_down[l]), g_ffn_post[l])
    return x


import jax as _jax
import jax.numpy as _jnp

TWIN_FORMAT = 'train_step'
FWD_PARAMS = ['x', 'mem', 'g_mix_pre', 'w_in', 'gm_ln_g', 'gm_ln_b', 'gm_w_s', 'gm_b_s', 's5_lam_re', 's5_lam_im', 's5_log_step', 's5_b_re', 's5_b_im', 's5_c_re', 's5_c_im', 's5_d', 's5_w_glu', 'w_br_gm', 'w_br_s5', 'w_mix_out', 'g_mix_post', 'g_ca_pre', 'g_mem', 'ca_w_q', 'ca_w_kv', 'ca_w_o', 'g_ca_post', 'g_ffn_pre', 'ffn_w_gu', 'ffn_w_down', 'g_ffn_post']
TWIN_WEIGHTS = ['g_mix_pre', 'w_in', 'gm_ln_g', 'gm_ln_b', 'gm_w_s', 'gm_b_s', 's5_lam_re', 's5_lam_im', 's5_log_step', 's5_b_re', 's5_b_im', 's5_c_re', 's5_c_im', 's5_d', 's5_w_glu', 'w_br_gm', 'w_br_s5', 'w_mix_out', 'g_mix_post', 'g_ca_pre', 'g_mem', 'ca_w_q', 'ca_w_kv', 'ca_w_o', 'g_ca_post', 'g_ffn_pre', 'ffn_w_gu', 'ffn_w_down', 'g_ffn_post']
TWIN_DIFF_INPUT = 'x'
TWIN_INPUTS = ['x', 'mem', 'g_mix_pre', 'w_in', 'gm_ln_g', 'gm_ln_b', 'gm_w_s', 'gm_b_s', 's5_lam_re', 's5_lam_im', 's5_log_step', 's5_b_re', 's5_b_im', 's5_c_re', 's5_c_im', 's5_d', 's5_w_glu', 'w_br_gm', 'w_br_s5', 'w_mix_out', 'g_mix_post', 'g_ca_pre', 'g_mem', 'ca_w_q', 'ca_w_kv', 'ca_w_o', 'g_ca_post', 'g_ffn_pre', 'ffn_w_gu', 'ffn_w_down', 'g_ffn_post', 'loss_target', 'm_g_mix_pre', 'm_w_in', 'm_gm_ln_g', 'm_gm_ln_b', 'm_gm_w_s', 'm_gm_b_s', 'm_s5_lam_re', 'm_s5_lam_im', 'm_s5_log_step', 'm_s5_b_re', 'm_s5_b_im', 'm_s5_c_re', 'm_s5_c_im', 'm_s5_d', 'm_s5_w_glu', 'm_w_br_gm', 'm_w_br_s5', 'm_w_mix_out', 'm_g_mix_post', 'm_g_ca_pre', 'm_g_mem', 'm_ca_w_q', 'm_ca_w_kv', 'm_ca_w_o', 'm_g_ca_post', 'm_g_ffn_pre', 'm_ffn_w_gu', 'm_ffn_w_down', 'm_g_ffn_post', 'v_g_mix_pre', 'v_w_in', 'v_gm_ln_g', 'v_gm_ln_b', 'v_gm_w_s', 'v_gm_b_s', 'v_s5_lam_re', 'v_s5_lam_im', 'v_s5_log_step', 'v_s5_b_re', 'v_s5_b_im', 'v_s5_c_re', 'v_s5_c_im', 'v_s5_d', 'v_s5_w_glu', 'v_w_br_gm', 'v_w_br_s5', 'v_w_mix_out', 'v_g_mix_post', 'v_g_ca_pre', 'v_g_mem', 'v_ca_w_q', 'v_ca_w_kv', 'v_ca_w_o', 'v_g_ca_post', 'v_g_ffn_pre', 'v_ffn_w_gu', 'v_ffn_w_down', 'v_g_ffn_post']
TWIN_OUTPUTS = ['loss', 'grad_x', 'grad_g_mix_pre', 'grad_w_in', 'grad_gm_ln_g', 'grad_gm_ln_b', 'grad_gm_w_s', 'grad_gm_b_s', 'grad_s5_lam_re', 'grad_s5_lam_im', 'grad_s5_log_step', 'grad_s5_b_re', 'grad_s5_b_im', 'grad_s5_c_re', 'grad_s5_c_im', 'grad_s5_d', 'grad_s5_w_glu', 'grad_w_br_gm', 'grad_w_br_s5', 'grad_w_mix_out', 'grad_g_mix_post', 'grad_g_ca_pre', 'grad_g_mem', 'grad_ca_w_q', 'grad_ca_w_kv', 'grad_ca_w_o', 'grad_g_ca_post', 'grad_g_ffn_pre', 'grad_ffn_w_gu', 'grad_ffn_w_down', 'grad_g_ffn_post', 'delta_g_mix_pre', 'delta_w_in', 'delta_gm_ln_g', 'delta_gm_ln_b', 'delta_gm_w_s', 'delta_gm_b_s', 'delta_s5_lam_re', 'delta_s5_lam_im', 'delta_s5_log_step', 'delta_s5_b_re', 'delta_s5_b_im', 'delta_s5_c_re', 'delta_s5_c_im', 'delta_s5_d', 'delta_s5_w_glu', 'delta_w_br_gm', 'delta_w_br_s5', 'delta_w_mix_out', 'delta_g_mix_post', 'delta_g_ca_pre', 'delta_g_mem', 'delta_ca_w_q', 'delta_ca_w_kv', 'delta_ca_w_o', 'delta_g_ca_post', 'delta_g_ffn_pre', 'delta_ffn_w_gu', 'delta_ffn_w_down', 'delta_g_ffn_post', 'new_m_g_mix_pre', 'new_m_w_in', 'new_m_gm_ln_g', 'new_m_gm_ln_b', 'new_m_gm_w_s', 'new_m_gm_b_s', 'new_m_s5_lam_re', 'new_m_s5_lam_im', 'new_m_s5_log_step', 'new_m_s5_b_re', 'new_m_s5_b_im', 'new_m_s5_c_re', 'new_m_s5_c_im', 'new_m_s5_d', 'new_m_s5_w_glu', 'new_m_w_br_gm', 'new_m_w_br_s5', 'new_m_w_mix_out', 'new_m_g_mix_post', 'new_m_g_ca_pre', 'new_m_g_mem', 'new_m_ca_w_q', 'new_m_ca_w_kv', 'new_m_ca_w_o', 'new_m_g_ca_post', 'new_m_g_ffn_pre', 'new_m_ffn_w_gu', 'new_m_ffn_w_down', 'new_m_g_ffn_post', 'new_v_g_mix_pre', 'new_v_w_in', 'new_v_gm_ln_g', 'new_v_gm_ln_b', 'new_v_gm_w_s', 'new_v_gm_b_s', 'new_v_s5_lam_re', 'new_v_s5_lam_im', 'new_v_s5_log_step', 'new_v_s5_b_re', 'new_v_s5_b_im', 'new_v_s5_c_re', 'new_v_s5_c_im', 'new_v_s5_d', 'new_v_s5_w_glu', 'new_v_w_br_gm', 'new_v_w_br_s5', 'new_v_w_mix_out', 'new_v_g_mix_post', 'new_v_g_ca_pre', 'new_v_g_mem', 'new_v_ca_w_q', 'new_v_ca_w_kv', 'new_v_ca_w_o', 'new_v_g_ca_post', 'new_v_g_ffn_pre', 'new_v_ffn_w_gu', 'new_v_ffn_w_down', 'new_v_g_ffn_post']
TWIN_LEAF_KINDS = {'loss': 'loss', 'grad_x': 'grad_x', 'grad_g_mix_pre': 'grad_w', 'grad_w_in': 'grad_w', 'grad_gm_ln_g': 'grad_w', 'grad_gm_ln_b': 'grad_w', 'grad_gm_w_s': 'grad_w', 'grad_gm_b_s': 'grad_w', 'grad_s5_lam_re': 'grad_w', 'grad_s5_lam_im': 'grad_w', 'grad_s5_log_step': 'grad_w', 'grad_s5_b_re': 'grad_w', 'grad_s5_b_im': 'grad_w', 'grad_s5_c_re': 'grad_w', 'grad_s5_c_im': 'grad_w', 'grad_s5_d': 'grad_w', 'grad_s5_w_glu': 'grad_w', 'grad_w_br_gm': 'grad_w', 'grad_w_br_s5': 'grad_w', 'grad_w_mix_out': 'grad_w', 'grad_g_mix_post': 'grad_w', 'grad_g_ca_pre': 'grad_w', 'grad_g_mem': 'grad_w', 'grad_ca_w_q': 'grad_w', 'grad_ca_w_kv': 'grad_w', 'grad_ca_w_o': 'grad_w', 'grad_g_ca_post': 'grad_w', 'grad_g_ffn_pre': 'grad_w', 'grad_ffn_w_gu': 'grad_w', 'grad_ffn_w_down': 'grad_w', 'grad_g_ffn_post': 'grad_w', 'delta_g_mix_pre': 'delta_w', 'delta_w_in': 'delta_w', 'delta_gm_ln_g': 'delta_w', 'delta_gm_ln_b': 'delta_w', 'delta_gm_w_s': 'delta_w', 'delta_gm_b_s': 'delta_w', 'delta_s5_lam_re': 'delta_w', 'delta_s5_lam_im': 'delta_w', 'delta_s5_log_step': 'delta_w', 'delta_s5_b_re': 'delta_w', 'delta_s5_b_im': 'delta_w', 'delta_s5_c_re': 'delta_w', 'delta_s5_c_im': 'delta_w', 'delta_s5_d': 'delta_w', 'delta_s5_w_glu': 'delta_w', 'delta_w_br_gm': 'delta_w', 'delta_w_br_s5': 'delta_w', 'delta_w_mix_out': 'delta_w', 'delta_g_mix_post': 'delta_w', 'delta_g_ca_pre': 'delta_w', 'delta_g_mem': 'delta_w', 'delta_ca_w_q': 'delta_w', 'delta_ca_w_kv': 'delta_w', 'delta_ca_w_o': 'delta_w', 'delta_g_ca_post': 'delta_w', 'delta_g_ffn_pre': 'delta_w', 'delta_ffn_w_gu': 'delta_w', 'delta_ffn_w_down': 'delta_w', 'delta_g_ffn_post': 'delta_w', 'new_m_g_mix_pre': 'new_m', 'new_m_w_in': 'new_m', 'new_m_gm_ln_g': 'new_m', 'new_m_gm_ln_b': 'new_m', 'new_m_gm_w_s': 'new_m', 'new_m_gm_b_s': 'new_m', 'new_m_s5_lam_re': 'new_m', 'new_m_s5_lam_im': 'new_m', 'new_m_s5_log_step': 'new_m', 'new_m_s5_b_re': 'new_m', 'new_m_s5_b_im': 'new_m', 'new_m_s5_c_re': 'new_m', 'new_m_s5_c_im': 'new_m', 'new_m_s5_d': 'new_m', 'new_m_s5_w_glu': 'new_m', 'new_m_w_br_gm': 'new_m', 'new_m_w_br_s5': 'new_m', 'new_m_w_mix_out': 'new_m', 'new_m_g_mix_post': 'new_m', 'new_m_g_ca_pre': 'new_m', 'new_m_g_mem': 'new_m', 'new_m_ca_w_q': 'new_m', 'new_m_ca_w_kv': 'new_m', 'new_m_ca_w_o': 'new_m', 'new_m_g_ca_post': 'new_m', 'new_m_g_ffn_pre': 'new_m', 'new_m_ffn_w_gu': 'new_m', 'new_m_ffn_w_down': 'new_m', 'new_m_g_ffn_post': 'new_m', 'new_v_g_mix_pre': 'new_v', 'new_v_w_in': 'new_v', 'new_v_gm_ln_g': 'new_v', 'new_v_gm_ln_b': 'new_v', 'new_v_gm_w_s': 'new_v', 'new_v_gm_b_s': 'new_v', 'new_v_s5_lam_re': 'new_v', 'new_v_s5_lam_im': 'new_v', 'new_v_s5_log_step': 'new_v', 'new_v_s5_b_re': 'new_v', 'new_v_s5_b_im': 'new_v', 'new_v_s5_c_re': 'new_v', 'new_v_s5_c_im': 'new_v', 'new_v_s5_d': 'new_v', 'new_v_s5_w_glu': 'new_v', 'new_v_w_br_gm': 'new_v', 'new_v_w_br_s5': 'new_v', 'new_v_w_mix_out': 'new_v', 'new_v_g_mix_post': 'new_v', 'new_v_g_ca_pre': 'new_v', 'new_v_g_mem': 'new_v', 'new_v_ca_w_q': 'new_v', 'new_v_ca_w_kv': 'new_v', 'new_v_ca_w_o': 'new_v', 'new_v_g_ca_post': 'new_v', 'new_v_g_ffn_pre': 'new_v', 'new_v_ffn_w_gu': 'new_v', 'new_v_ffn_w_down': 'new_v', 'new_v_g_ffn_post': 'new_v'}


def _forward(args):
    return _fwd_reference(*[args[k] for k in FWD_PARAMS])


def _output_shape():
    out = _jax.eval_shape(lambda: _forward(_fwd_setup_inputs(0)))
    return out.shape, out.dtype

N_MICROBATCH = 1
ADAM_LR = 0.001
ADAM_B1 = 0.9
ADAM_B2 = 0.999
ADAM_EPS = 1e-08
ADAM_WD = 0.01
ADAM_STEP = 10
PER_EXAMPLE_BATCH_AXIS = {'x': 0, 'mem': 0, 'loss_target': 0}
SHARED_INPUTS = []
_WEIGHT_DTYPES = {'g_mix_pre': _jnp.float32, 'w_in': _jnp.float32, 'gm_ln_g': _jnp.float32, 'gm_ln_b': _jnp.float32, 'gm_w_s': _jnp.float32, 'gm_b_s': _jnp.float32, 's5_lam_re': _jnp.float32, 's5_lam_im': _jnp.float32, 's5_log_step': _jnp.float32, 's5_b_re': _jnp.float32, 's5_b_im': _jnp.float32, 's5_c_re': _jnp.float32, 's5_c_im': _jnp.float32, 's5_d': _jnp.float32, 's5_w_glu': _jnp.float32, 'w_br_gm': _jnp.float32, 'w_br_s5': _jnp.float32, 'w_mix_out': _jnp.float32, 'g_mix_post': _jnp.float32, 'g_ca_pre': _jnp.float32, 'g_mem': _jnp.float32, 'ca_w_q': _jnp.float32, 'ca_w_kv': _jnp.float32, 'ca_w_o': _jnp.float32, 'g_ca_post': _jnp.float32, 'g_ffn_pre': _jnp.float32, 'ffn_w_gu': _jnp.float32, 'ffn_w_down': _jnp.float32, 'g_ffn_post': _jnp.float32}
MOMENT_SCALE = {'g_mix_pre': 1.278445e+00, 'w_in': 5.738945e-01, 'gm_ln_g': 4.471767e-01, 'gm_ln_b': 4.451642e-01, 'gm_w_s': 4.081685e-01, 'gm_b_s': 6.742392e-01, 's5_lam_re': 2.908196e-02, 's5_lam_im': 3.410256e-02, 's5_log_step': 3.960345e+01, 's5_b_re': 1.893229e-02, 's5_b_im': 1.844117e-02, 's5_c_re': 3.725619e-02, 's5_c_im': 3.745932e-02, 's5_d': 5.936214e+00, 's5_w_glu': 7.582630e-01, 'w_br_gm': 5.676642e+00, 'w_br_s5': 3.649005e+00, 'w_mix_out': 6.958144e+00, 'g_mix_post': 6.532310e+01, 'g_ca_pre': 2.314061e+00, 'g_mem': 8.388954e+00, 'ca_w_q': 2.220410e+00, 'ca_w_kv': 5.749499e+00, 'ca_w_o': 7.951128e+00, 'g_ca_post': 6.575373e+01, 'g_ffn_pre': 4.339856e+00, 'ffn_w_gu': 1.873235e+00, 'ffn_w_down': 3.719318e+00, 'g_ffn_post': 6.361082e+01}


def _to_microbatches(a, axis):
    t = _jnp.moveaxis(a, axis, 0)
    t = t.reshape((N_MICROBATCH, t.shape[0] // N_MICROBATCH) + t.shape[1:])
    return _jnp.moveaxis(t, 1, axis + 1)


def setup_inputs(seed: int = 0) -> dict:
    inp = _fwd_setup_inputs(seed)
    key = _jax.random.fold_in(_jax.random.key(seed), 7919)
    shape, _ = _output_shape()
    out = dict(inp)
    out["loss_target"] = _jax.random.normal(_jax.random.fold_in(key, 0), shape, _jnp.float32)
    for i, name in enumerate(TWIN_WEIGHTS):
        w = inp[name].astype(_jnp.float32)
        if MOMENT_SCALE is None:
            s = _jnp.sqrt(_jnp.mean(_jnp.square(w)) + 1e-30)
        else:
            s = MOMENT_SCALE[name]
        km, kv = _jax.random.split(_jax.random.fold_in(key, i + 1))
        out[name] = w
        out["m_" + name] = s * _jax.random.normal(km, w.shape, _jnp.float32)
        out["v_" + name] = (s * s) * _jax.random.uniform(kv, w.shape, _jnp.float32, 0.5, 1.5)
    if N_MICROBATCH > 1:
        for name, axis in PER_EXAMPLE_BATCH_AXIS.items():
            out[name] = _to_microbatches(out[name], axis)
    return {'x': out['x'], 'mem': out['mem'], 'g_mix_pre': out['g_mix_pre'], 'w_in': out['w_in'], 'gm_ln_g': out['gm_ln_g'], 'gm_ln_b': out['gm_ln_b'], 'gm_w_s': out['gm_w_s'], 'gm_b_s': out['gm_b_s'], 's5_lam_re': out['s5_lam_re'], 's5_lam_im': out['s5_lam_im'], 's5_log_step': out['s5_log_step'], 's5_b_re': out['s5_b_re'], 's5_b_im': out['s5_b_im'], 's5_c_re': out['s5_c_re'], 's5_c_im': out['s5_c_im'], 's5_d': out['s5_d'], 's5_w_glu': out['s5_w_glu'], 'w_br_gm': out['w_br_gm'], 'w_br_s5': out['w_br_s5'], 'w_mix_out': out['w_mix_out'], 'g_mix_post': out['g_mix_post'], 'g_ca_pre': out['g_ca_pre'], 'g_mem': out['g_mem'], 'ca_w_q': out['ca_w_q'], 'ca_w_kv': out['ca_w_kv'], 'ca_w_o': out['ca_w_o'], 'g_ca_post': out['g_ca_post'], 'g_ffn_pre': out['g_ffn_pre'], 'ffn_w_gu': out['ffn_w_gu'], 'ffn_w_down': out['ffn_w_down'], 'g_ffn_post': out['g_ffn_post'], 'loss_target': out['loss_target'], 'm_g_mix_pre': out['m_g_mix_pre'], 'm_w_in': out['m_w_in'], 'm_gm_ln_g': out['m_gm_ln_g'], 'm_gm_ln_b': out['m_gm_ln_b'], 'm_gm_w_s': out['m_gm_w_s'], 'm_gm_b_s': out['m_gm_b_s'], 'm_s5_lam_re': out['m_s5_lam_re'], 'm_s5_lam_im': out['m_s5_lam_im'], 'm_s5_log_step': out['m_s5_log_step'], 'm_s5_b_re': out['m_s5_b_re'], 'm_s5_b_im': out['m_s5_b_im'], 'm_s5_c_re': out['m_s5_c_re'], 'm_s5_c_im': out['m_s5_c_im'], 'm_s5_d': out['m_s5_d'], 'm_s5_w_glu': out['m_s5_w_glu'], 'm_w_br_gm': out['m_w_br_gm'], 'm_w_br_s5': out['m_w_br_s5'], 'm_w_mix_out': out['m_w_mix_out'], 'm_g_mix_post': out['m_g_mix_post'], 'm_g_ca_pre': out['m_g_ca_pre'], 'm_g_mem': out['m_g_mem'], 'm_ca_w_q': out['m_ca_w_q'], 'm_ca_w_kv': out['m_ca_w_kv'], 'm_ca_w_o': out['m_ca_w_o'], 'm_g_ca_post': out['m_g_ca_post'], 'm_g_ffn_pre': out['m_g_ffn_pre'], 'm_ffn_w_gu': out['m_ffn_w_gu'], 'm_ffn_w_down': out['m_ffn_w_down'], 'm_g_ffn_post': out['m_g_ffn_post'], 'v_g_mix_pre': out['v_g_mix_pre'], 'v_w_in': out['v_w_in'], 'v_gm_ln_g': out['v_gm_ln_g'], 'v_gm_ln_b': out['v_gm_ln_b'], 'v_gm_w_s': out['v_gm_w_s'], 'v_gm_b_s': out['v_gm_b_s'], 'v_s5_lam_re': out['v_s5_lam_re'], 'v_s5_lam_im': out['v_s5_lam_im'], 'v_s5_log_step': out['v_s5_log_step'], 'v_s5_b_re': out['v_s5_b_re'], 'v_s5_b_im': out['v_s5_b_im'], 'v_s5_c_re': out['v_s5_c_re'], 'v_s5_c_im': out['v_s5_c_im'], 'v_s5_d': out['v_s5_d'], 'v_s5_w_glu': out['v_s5_w_glu'], 'v_w_br_gm': out['v_w_br_gm'], 'v_w_br_s5': out['v_w_br_s5'], 'v_w_mix_out': out['v_w_mix_out'], 'v_g_mix_post': out['v_g_mix_post'], 'v_g_ca_pre': out['v_g_ca_pre'], 'v_g_mem': out['v_g_mem'], 'v_ca_w_q': out['v_ca_w_q'], 'v_ca_w_kv': out['v_ca_w_kv'], 'v_ca_w_o': out['v_ca_w_o'], 'v_g_ca_post': out['v_g_ca_post'], 'v_g_ffn_pre': out['v_g_ffn_pre'], 'v_ffn_w_gu': out['v_ffn_w_gu'], 'v_ffn_w_down': out['v_ffn_w_down'], 'v_g_ffn_post': out['v_g_ffn_post']}


def _loss(weights, diff, rest, loss_target):
    with _jax.named_scope("forward"):
        args = {**rest, TWIN_DIFF_INPUT: diff, **{k: w.astype(_WEIGHT_DTYPES[k]) for k, w in weights.items()}}
        y = _forward(args)
    with _jax.named_scope("loss_head"):
        err = _jnp.square(y.astype(_jnp.float32) - loss_target)
        return 0.5 * _jnp.sum(_jnp.mean(err, axis=-1)) if err.ndim else 0.5 * err


def _adamw(w, g, m, v):
    m = ADAM_B1 * m + (1.0 - ADAM_B1) * g
    v = ADAM_B2 * v + (1.0 - ADAM_B2) * _jnp.square(g)
    m_hat = m / (1.0 - ADAM_B1 ** ADAM_STEP)
    v_hat = v / (1.0 - ADAM_B2 ** ADAM_STEP)
    delta = -ADAM_LR * (m_hat / (_jnp.sqrt(v_hat) + ADAM_EPS) + ADAM_WD * w)
    return delta, m, v


def reference(x, mem, g_mix_pre, w_in, gm_ln_g, gm_ln_b, gm_w_s, gm_b_s, s5_lam_re, s5_lam_im, s5_log_step, s5_b_re, s5_b_im, s5_c_re, s5_c_im, s5_d, s5_w_glu, w_br_gm, w_br_s5, w_mix_out, g_mix_post, g_ca_pre, g_mem, ca_w_q, ca_w_kv, ca_w_o, g_ca_post, g_ffn_pre, ffn_w_gu, ffn_w_down, g_ffn_post, loss_target, m_g_mix_pre, m_w_in, m_gm_ln_g, m_gm_ln_b, m_gm_w_s, m_gm_b_s, m_s5_lam_re, m_s5_lam_im, m_s5_log_step, m_s5_b_re, m_s5_b_im, m_s5_c_re, m_s5_c_im, m_s5_d, m_s5_w_glu, m_w_br_gm, m_w_br_s5, m_w_mix_out, m_g_mix_post, m_g_ca_pre, m_g_mem, m_ca_w_q, m_ca_w_kv, m_ca_w_o, m_g_ca_post, m_g_ffn_pre, m_ffn_w_gu, m_ffn_w_down, m_g_ffn_post, v_g_mix_pre, v_w_in, v_gm_ln_g, v_gm_ln_b, v_gm_w_s, v_gm_b_s, v_s5_lam_re, v_s5_lam_im, v_s5_log_step, v_s5_b_re, v_s5_b_im, v_s5_c_re, v_s5_c_im, v_s5_d, v_s5_w_glu, v_w_br_gm, v_w_br_s5, v_w_mix_out, v_g_mix_post, v_g_ca_pre, v_g_mem, v_ca_w_q, v_ca_w_kv, v_ca_w_o, v_g_ca_post, v_g_ffn_pre, v_ffn_w_gu, v_ffn_w_down, v_g_ffn_post):
    given = dict(x=x, mem=mem, g_mix_pre=g_mix_pre, w_in=w_in, gm_ln_g=gm_ln_g, gm_ln_b=gm_ln_b, gm_w_s=gm_w_s, gm_b_s=gm_b_s, s5_lam_re=s5_lam_re, s5_lam_im=s5_lam_im, s5_log_step=s5_log_step, s5_b_re=s5_b_re, s5_b_im=s5_b_im, s5_c_re=s5_c_re, s5_c_im=s5_c_im, s5_d=s5_d, s5_w_glu=s5_w_glu, w_br_gm=w_br_gm, w_br_s5=w_br_s5, w_mix_out=w_mix_out, g_mix_post=g_mix_post, g_ca_pre=g_ca_pre, g_mem=g_mem, ca_w_q=ca_w_q, ca_w_kv=ca_w_kv, ca_w_o=ca_w_o, g_ca_post=g_ca_post, g_ffn_pre=g_ffn_pre, ffn_w_gu=ffn_w_gu, ffn_w_down=ffn_w_down, g_ffn_post=g_ffn_post, loss_target=loss_target, m_g_mix_pre=m_g_mix_pre, m_w_in=m_w_in, m_gm_ln_g=m_gm_ln_g, m_gm_ln_b=m_gm_ln_b, m_gm_w_s=m_gm_w_s, m_gm_b_s=m_gm_b_s, m_s5_lam_re=m_s5_lam_re, m_s5_lam_im=m_s5_lam_im, m_s5_log_step=m_s5_log_step, m_s5_b_re=m_s5_b_re, m_s5_b_im=m_s5_b_im, m_s5_c_re=m_s5_c_re, m_s5_c_im=m_s5_c_im, m_s5_d=m_s5_d, m_s5_w_glu=m_s5_w_glu, m_w_br_gm=m_w_br_gm, m_w_br_s5=m_w_br_s5, m_w_mix_out=m_w_mix_out, m_g_mix_post=m_g_mix_post, m_g_ca_pre=m_g_ca_pre, m_g_mem=m_g_mem, m_ca_w_q=m_ca_w_q, m_ca_w_kv=m_ca_w_kv, m_ca_w_o=m_ca_w_o, m_g_ca_post=m_g_ca_post, m_g_ffn_pre=m_g_ffn_pre, m_ffn_w_gu=m_ffn_w_gu, m_ffn_w_down=m_ffn_w_down, m_g_ffn_post=m_g_ffn_post, v_g_mix_pre=v_g_mix_pre, v_w_in=v_w_in, v_gm_ln_g=v_gm_ln_g, v_gm_ln_b=v_gm_ln_b, v_gm_w_s=v_gm_w_s, v_gm_b_s=v_gm_b_s, v_s5_lam_re=v_s5_lam_re, v_s5_lam_im=v_s5_lam_im, v_s5_log_step=v_s5_log_step, v_s5_b_re=v_s5_b_re, v_s5_b_im=v_s5_b_im, v_s5_c_re=v_s5_c_re, v_s5_c_im=v_s5_c_im, v_s5_d=v_s5_d, v_s5_w_glu=v_s5_w_glu, v_w_br_gm=v_w_br_gm, v_w_br_s5=v_w_br_s5, v_w_mix_out=v_w_mix_out, v_g_mix_post=v_g_mix_post, v_g_ca_pre=v_g_ca_pre, v_g_mem=v_g_mem, v_ca_w_q=v_ca_w_q, v_ca_w_kv=v_ca_w_kv, v_ca_w_o=v_ca_w_o, v_g_ca_post=v_g_ca_post, v_g_ffn_pre=v_g_ffn_pre, v_ffn_w_gu=v_ffn_w_gu, v_ffn_w_down=v_ffn_w_down, v_g_ffn_post=v_g_ffn_post)
    weights = {n: given[n] for n in TWIN_WEIGHTS}
    shared = {n: given[n] for n in SHARED_INPUTS}
    per_example = {n: given[n] for n in ['x', 'mem']}
    grad_fn = _jax.value_and_grad(_loss, argnums=(0, 1))

    def one_microbatch(ex, loss_target):
        ex = dict(ex)
        diff = ex.pop(TWIN_DIFF_INPUT)
        return grad_fn(weights, diff, {**shared, **ex}, loss_target)

    if N_MICROBATCH == 1:
        loss, (grad_w, grad_x) = one_microbatch(per_example, given["loss_target"])
    else:
        def body(carry, xs):
            loss_sum, grad_sum = carry
            l_k, (gw_k, gx_k) = one_microbatch(xs[0], xs[1])
            with _jax.named_scope("update"):
                return (loss_sum + l_k, _jax.tree.map(_jnp.add, grad_sum, gw_k)), gx_k

        init = (_jnp.zeros((), _jnp.float32), _jax.tree.map(_jnp.zeros_like, weights))
        (loss, grad_w), grad_x = _jax.lax.scan(body, init, (per_example, given["loss_target"]))
    with _jax.named_scope("update"):
        delta_w, new_m, new_v = {}, {}, {}
        for n in TWIN_WEIGHTS:
            delta_w[n], new_m[n], new_v[n] = _adamw(weights[n], grad_w[n], given["m_" + n], given["v_" + n])
    return (loss, grad_x, *[grad_w[n] for n in TWIN_WEIGHTS], *[delta_w[n] for n in TWIN_WEIGHTS],
            *[new_m[n] for n in TWIN_WEIGHTS], *[new_v[n] for n in TWIN_WEIGHTS])
```

```python
import functools
import math

import jax
import jax.numpy as jnp
from jax import lax
from jax.experimental import pallas as pl
from jax.experimental.pallas import tpu as pltpu

F32 = jnp.float32
BF16 = jnp.bfloat16
EPS = 1e-6
D = 1024
GM_CHUNK = 128
GM_GROUPS = 8
S5_W = 512
S5_SUPER = 4
S5_T = 128
HEADS = 4
HEAD_DIM = 256
FFN_H = 2816
LANE = 128
VMEM_LIMIT = 56 * 1024 * 1024
MESH = pl.DeviceIdType.MESH

ADAM_LR, ADAM_B1, ADAM_B2, ADAM_EPS, ADAM_WD, ADAM_STEP = 0.001, 0.9, 0.999, 1e-08, 0.01, 10

WNAMES = ['g_mix_pre', 'w_in', 'gm_ln_g', 'gm_ln_b', 'gm_w_s', 'gm_b_s', 's5_lam_re', 's5_lam_im', 's5_log_step',
          's5_b_re', 's5_b_im', 's5_c_re', 's5_c_im', 's5_d', 's5_w_glu', 'w_br_gm', 'w_br_s5', 'w_mix_out',
          'g_mix_post', 'g_ca_pre', 'g_mem', 'ca_w_q', 'ca_w_kv', 'ca_w_o', 'g_ca_post', 'g_ffn_pre', 'ffn_w_gu',
          'ffn_w_down', 'g_ffn_post']
BIG = {'w_in': 1, 's5_w_glu': 0, 'w_br_gm': 0, 'w_br_s5': 1, 'w_mix_out': 0, 'ca_w_q': 0, 'ca_w_kv': 1,
       'ca_w_o': 0, 'ffn_w_gu': 1, 'ffn_w_down': 0}
SMALL = [n for n in WNAMES if n not in BIG]


def _dot(a, b):
    return jnp.dot(a, b, preferred_element_type=F32)


def _dot_nt(a, b):
    return lax.dot_general(a, b, (((1,), (1,)), ((), ())), preferred_element_type=F32)


def _dot_tn(a, b):
    return lax.dot_general(a, b, (((0,), (0,)), ((), ())), preferred_element_type=F32)


def _rms_fwd(x, g):
    r = lax.rsqrt(jnp.mean(x * x, axis=-1, keepdims=True) + EPS)
    return x * r * g


def _rms_bwd(x, g, dy):
    r = lax.rsqrt(jnp.mean(x * x, axis=-1, keepdims=True) + EPS)
    xh = x * r
    gdy = dy * g
    dx = r * (gdy - xh * jnp.mean(gdy * xh, axis=-1, keepdims=True))
    dg = jnp.sum(dy * xh, axis=0, keepdims=True)
    return dx, dg


_GC = math.sqrt(2.0 / math.pi)


def _gelu(x):
    return 0.5 * x * (1.0 + jnp.tanh(_GC * (x + 0.044715 * x * x * x)))


def _gelu_grad(x):
    t = jnp.tanh(_GC * (x + 0.044715 * x * x * x))
    return 0.5 * (1.0 + t) + 0.5 * x * (1.0 - t * t) * _GC * (1.0 + 3 * 0.044715 * x * x)


def _sig(x):
    return 1.0 / (1.0 + jnp.exp(-x))


def _cscan(br, bi, ar, ai, reverse):
    t = br.shape[0]
    row = lax.broadcasted_iota(jnp.int32, br.shape, 0)
    pr, pi = ar, ai
    sh = 1
    while sh < t:
        if reverse:
            keep = row < t - sh
            rr, ri = pltpu.roll(br, t - sh, 0), pltpu.roll(bi, t - sh, 0)
        else:
            keep = row >= sh
            rr, ri = pltpu.roll(br, sh, 0), pltpu.roll(bi, sh, 0)
        rr = jnp.where(keep, rr, 0.0)
        ri = jnp.where(keep, ri, 0.0)
        br, bi = br + pr * rr - pi * ri, bi + pr * ri + pi * rr
        pr, pi = pr * pr - pi * pi, 2.0 * pr * pi
        sh *= 2
    return br, bi


def _s5_disc(lr, li, ls, br, bi):
    step = jnp.exp(ls)
    mag = jnp.exp(lr * step)
    ab_re = mag * jnp.cos(li * step)
    ab_im = mag * jnp.sin(li * step)
    den = lr * lr + li * li
    nr = ab_re - 1.0
    co_re = (nr * lr + ab_im * li) / den
    co_im = (ab_im * lr - nr * li) / den
    return ab_re, ab_im, co_re * br - co_im * bi, co_re * bi + co_im * br


def _params(sem=None):
    return pltpu.CompilerParams(dimension_semantics=sem, vmem_limit_bytes=VMEM_LIMIT)


def _rowcall(name, body, n_rows, tm, ins, outs, scratch=()):
    def spec(kind, shape):
        if kind == 'row':
            return pl.BlockSpec((tm, shape[1]), lambda i: (i, 0))
        if kind in ('full', 'acc'):
            nd = len(shape)
            return pl.BlockSpec(tuple(shape), lambda i: (0,) * nd)
        return kind

    in_specs = [spec(k, a.shape) for a, k in ins]
    out_shape, out_specs = [], []
    for k, s, dt in outs:
        shape = (n_rows, s) if k == 'row' else tuple(s)
        out_shape.append(jax.ShapeDtypeStruct(shape, dt))
        out_specs.append(spec(k, shape))
    return pl.pallas_call(
        body, name=name, grid=(n_rows // tm,), in_specs=in_specs, out_specs=out_specs, out_shape=out_shape,
        scratch_shapes=list(scratch), compiler_params=_params(("arbitrary",)),
    )(*[a for a, _ in ins])


def _tile(n, cap):
    if n <= cap:
        return n
    best = LANE
    for k in range(1, n // LANE + 1):
        t = k * LANE
        if n % t == 0 and t <= cap:
            best = t
    return best


def _mm_tn(name, a, b):
    n, k = a.shape
    m = b.shape[1]
    tk, tn, tr = _tile(k, 1536), _tile(m, 1536), min(n, 512)

    def body(a_ref, b_ref, o_ref):
        @pl.when(pl.program_id(2) == 0)
        def _():
            o_ref[...] = jnp.zeros_like(o_ref)
        o_ref[...] += _dot_tn(a_ref[...], b_ref[...])

    return pl.pallas_call(
        body, name=name, grid=(k // tk, m // tn, n // tr),
        in_specs=[pl.BlockSpec((tr, tk), lambda i, j, r: (r, i)), pl.BlockSpec((tr, tn), lambda i, j, r: (r, j))],
        out_specs=pl.BlockSpec((tk, tn), lambda i, j, r: (i, j)),
        out_shape=jax.ShapeDtypeStruct((k, m), F32),
        compiler_params=_params(("parallel", "parallel", "arbitrary")),
    )(a, b)


def _exchange(name, src, out_shape, dtype, sends, local=None):
    n = len(sends)

    def body(src_ref, out_ref, send_sems, recv_sems, local_sem):
        me = (lax.axis_index("x"), lax.axis_index("y"), lax.axis_index("c"))

        def at(ref, idx):
            return ref.at[idx] if idx is not None else ref

        if local is not None:
            mine = pltpu.make_async_copy(at(src_ref, local[0](*me)), at(out_ref, local[1](*me)), local_sem)
            mine.start()
        copies = []
        for k, (flips, sidx, didx) in enumerate(sends):
            peer = tuple(1 - p if f else p for p, f in zip(me, flips))
            cp = pltpu.make_async_remote_copy(
                src_ref=at(src_ref, sidx(*me)), dst_ref=at(out_ref, didx(*me)),
                send_sem=send_sems.at[k], recv_sem=recv_sems.at[k], device_id=peer, device_id_type=MESH)
            cp.start()
            copies.append(cp)
        for cp in copies:
            cp.wait()
        if local is not None:
            mine.wait()

    hbm = pl.BlockSpec(memory_space=pltpu.HBM)
    return pl.pallas_call(
        body, name=name, in_specs=[hbm], out_specs=hbm, out_shape=jax.ShapeDtypeStruct(out_shape, dtype),
        scratch_shapes=[pltpu.SemaphoreType.DMA((n,)), pltpu.SemaphoreType.DMA((n,)), pltpu.SemaphoreType.DMA],
    )(src)


def _f_in(x, g, w_in, tm):
    n = x.shape[0]

    def body(x_ref, g_ref, w_ref, h_ref, zgm_ref, zs5_ref, zg_ref):
        h = _rms_fwd(x_ref[...], g_ref[...]).astype(BF16)
        h_ref[...] = h
        zgm_ref[...] = _dot(h, w_ref[:, 0:2 * D])
        zs5_ref[...] = _dot(h, w_ref[:, 2 * D:2 * D + S5_W])
        zg_ref[...] = _dot(h, w_ref[:, 2 * D + S5_W:])

    return _rowcall("f_in", body, n, tm, [(x, 'row'), (g, 'full'), (w_in, 'full')],
                    [('row', D, BF16), ('row', 2 * D, F32), ('row', S5_W, F32), ('row', 2 * D, F32)])


def _tril():
    r = lax.broadcasted_iota(jnp.int32, (GM_CHUNK, GM_CHUNK), 0)
    c = lax.broadcasted_iota(jnp.int32, (GM_CHUNK, GM_CHUNK), 1)
    return r >= c


def _ln_stats(v):
    mu = jnp.mean(v, axis=-1, keepdims=True)
    vc = v - mu
    r = lax.rsqrt(jnp.mean(vc * vc, axis=-1, keepdims=True) + EPS)
    return vc * r, r


def _f_gmlp(zgm, ln_g, ln_b, w_s, b_s, tm):
    n = zgm.shape[0]

    def body(z_ref, lg_ref, lb_ref, ws_ref, bs_ref, y_ref):
        zg = _gelu(z_ref[...])
        u = zg[:, :D]
        vh, _ = _ln_stats(zg[:, D:])
        vn = (vh * lg_ref[...] + lb_ref[...]).astype(BF16)
        keep = _tril()
        for g in range(GM_GROUPS):
            w = jnp.where(keep, ws_ref[g], 0.0).astype(BF16)
            cs = slice(g * LANE, (g + 1) * LANE)
            for c in range(tm // GM_CHUNK):
                rs = slice(c * GM_CHUNK, (c + 1) * GM_CHUNK)
                sv = _dot(w, vn[rs, cs]) + bs_ref[g]
                y_ref[rs, cs] = (u[rs, cs] * sv).astype(BF16)

    return _rowcall("f_gmlp", body, n, tm,
                    [(zgm, 'row'), (ln_g, 'full'), (ln_b, 'full'), (w_s, 'full'), (b_s, 'full')],
                    [('row', D, BF16)])[0]


def _s5_specs(nb, nc, rev):
    def cc(c):
        return nc - 1 - c if rev else c
    slab = pl.BlockSpec((S5_T, LANE), lambda j, b, c: (b * nc + cc(c), j))
    bb = pl.BlockSpec((None, LANE, 1024), lambda j, b, c: (j, 0, 0))
    cm = pl.BlockSpec((None, 1024, LANE), lambda j, b, c: (j, 0, 0))
    av = pl.BlockSpec((None, 1, 1024), lambda j, b, c: (j, 0, 0))
    dv = pl.BlockSpec((None, 1, LANE), lambda j, b, c: (j, 0, 0))
    st = pl.BlockSpec((None, None, None, 1, 1024), lambda j, b, c: (j, b, cc(c), 0, 0))
    return slab, bb, cm, av, dv, st


def _s5_states(u, bb_ref, a_ref, carry):
    bu = _dot(u.astype(BF16), bb_ref[...])
    ar, ai = a_ref[:, :512], a_ref[:, 512:]
    cr, ci = carry[:, :512], carry[:, 512:]
    first = lax.broadcasted_iota(jnp.int32, (S5_T, 512), 0) == 0
    br = bu[:, :512] + jnp.where(first, ar * cr - ai * ci, 0.0)
    bi = bu[:, 512:] + jnp.where(first, ar * ci + ai * cr, 0.0)
    return _cscan(br, bi, ar, ai, reverse=False)


def _f_s5(zs5, bbc, ccm, avec, dvec, nb):
    n = zs5.shape[0]
    nc = n // nb // S5_T
    slab, bb, cm, av, dv, st = _s5_specs(nb, nc, False)

    def body(u_ref, bb_ref, cc_ref, a_ref, d_ref, y_ref, st_ref, carry):
        @pl.when(pl.program_id(2) == 0)
        def _():
            carry[...] = jnp.zeros_like(carry)
        st_ref[...] = carry[...]
        u = u_ref[...]
        sr, si = _s5_states(u, bb_ref, a_ref, carry[...])
        last = lax.broadcasted_iota(jnp.int32, (S5_T, 512), 0) == S5_T - 1
        carry[:, :512] = jnp.sum(jnp.where(last, sr, 0.0), axis=0, keepdims=True)
        carry[:, 512:] = jnp.sum(jnp.where(last, si, 0.0), axis=0, keepdims=True)
        s = jnp.concatenate([sr, si], axis=1).astype(BF16)
        y_ref[...] = _dot(s, cc_ref[...]) + d_ref[...] * u

    return pl.pallas_call(
        body, name="f_s5", grid=(S5_SUPER, nb, nc), in_specs=[slab, bb, cm, av, dv], out_specs=[slab, st],
        out_shape=[jax.ShapeDtypeStruct((n, S5_W), F32), jax.ShapeDtypeStruct((S5_SUPER, nb, nc, 1, 1024), F32)],
        scratch_shapes=[pltpu.VMEM((1, 1024), F32)],
        compiler_params=_params(("arbitrary", "arbitrary", "arbitrary")),
    )(zs5, bbc, ccm, avec, dvec)


def _f_mix(ygm, ypre, zg, x, w_glu, w_br_gm, w_br_s5, w_mix_out, g_post, tm):
    n = x.shape[0]

    def body(ygm_ref, ypre_ref, zg_ref, x_ref, wglu_ref, wgm_ref, ws5_ref, wout_ref, g_ref,
             yg_ref, ys5_ref, a_ref, b_ref, mg_ref, mo_ref, x1_ref):
        yg = _gelu(ypre_ref[...])
        ygb = yg.astype(BF16)
        yg_ref[...] = ygb
        ys5 = (yg * _sig(_dot(ygb, wglu_ref[...]))).astype(BF16)
        ys5_ref[...] = ys5
        a = _dot(ygm_ref[...], wgm_ref[...])
        b = _dot(ys5, ws5_ref[...])
        a_ref[...] = a.astype(BF16)
        b_ref[...] = b.astype(BF16)
        zg = zg_ref[...]
        merged = (_sig(zg[:, :D]) * a + _sig(zg[:, D:]) * b).astype(BF16)
        mg_ref[...] = merged
        mo = _dot(merged, wout_ref[...])
        mo_ref[...] = mo
        x1_ref[...] = x_ref[...] + _rms_fwd(mo, g_ref[...])

    return _rowcall("f_mix", body, n, tm,
                    [(ygm, 'row'), (ypre, 'row'), (zg, 'row'), (x, 'row'), (w_glu, 'full'), (w_br_gm, 'full'),
                     (w_br_s5, 'full'), (w_mix_out, 'full'), (g_post, 'full')],
                    [('row', S5_W, BF16), ('row', S5_W, BF16), ('row', D, BF16), ('row', D, BF16),
                     ('row', D, BF16), ('row', D, F32), ('row', D, F32)])


def _f_mem(mem, g_mem, w_kv, tm):
    n = mem.shape[0]

    def body(m_ref, g_ref, w_ref, mn_ref, k_ref, v_ref):
        mn = _rms_fwd(m_ref[...], g_ref[...]).astype(BF16)
        mn_ref[...] = mn
        k_ref[...] = _dot(mn, w_ref[:, :D]).astype(BF16)
        v_ref[...] = _dot(mn, w_ref[:, D:]).astype(BF16)

    return _rowcall("f_mem", body, n, tm, [(mem, 'row'), (g_mem, 'full'), (w_kv, 'full')],
                    [('row', D, BF16), ('row', D, BF16), ('row', D, BF16)])


def _softmax(s):
    m = jnp.max(s, axis=-1, keepdims=True)
    e = jnp.exp(s - m)
    return e / jnp.sum(e, axis=-1, keepdims=True)


def _f_attn(x1, g_pre, w_q, k, v, w_o, g_post, tm, tpb, mlen):
    n = x1.shape[0]
    kv_spec = pl.BlockSpec((mlen, D), lambda i: (i // tpb, 0))
    scale = HEAD_DIM ** -0.5

    def body(x_ref, gp_ref, wq_ref, k_ref, v_ref, wo_ref, go_ref, hc_ref, o_ref, ao_ref, x2_ref):
        x1v = x_ref[...]
        hc = _rms_fwd(x1v, gp_ref[...]).astype(BF16)
        hc_ref[...] = hc
        q = _dot(hc, wq_ref[...])
        for h in range(HEADS):
            hs = slice(h * HEAD_DIM, (h + 1) * HEAD_DIM)
            p = _softmax(_dot_nt(q[:, hs].astype(BF16), k_ref[:, hs]) * scale)
            o_ref[:, hs] = _dot(p.astype(BF16), v_ref[:, hs]).astype(BF16)
        ao = _dot(o_ref[...], wo_ref[...])
        ao_ref[...] = ao
        x2_ref[...] = x1v + _rms_fwd(ao, go_ref[...])

    return _rowcall("f_attn", body, n, tm,
                    [(x1, 'row'), (g_pre, 'full'), (w_q, 'full'), (k, kv_spec), (v, kv_spec), (w_o, 'full'),
                     (g_post, 'full')],
                    [('row', D, BF16), ('row', D, BF16), ('row', D, F32), ('row', D, F32)])


def _f_ffn_up(x2, g_pre, w_gu, tm):
    n = x2.shape[0]

    def body(x_ref, g_ref, w_ref, hf_ref, gu_ref, act_ref):
        hf = _rms_fwd(x_ref[...], g_ref[...]).astype(BF16)
        hf_ref[...] = hf
        gg = _dot(hf, w_ref[:, :FFN_H])
        uu = _dot(hf, w_ref[:, FFN_H:])
        gu_ref[:, :FFN_H] = gg.astype(BF16)
        gu_ref[:, FFN_H:] = uu.astype(BF16)
        act_ref[...] = (gg * _sig(gg) * uu).astype(BF16)

    return _rowcall("f_ffn_up", body, n, tm, [(x2, 'row'), (g_pre, 'full'), (w_gu, 'full')],
                    [('row', D, BF16), ('row', 2 * FFN_H, BF16), ('row', FFN_H, BF16)])


def _f_ffn_down(act, x2, tgt, w_down, g_post, tm):
    n = x2.shape[0]

    def body(a_ref, x_ref, t_ref, w_ref, g_ref, dn_ref, d3_ref, loss_ref):
        @pl.when(pl.program_id(0) == 0)
        def _():
            loss_ref[...] = jnp.zeros_like(loss_ref)
        dn = _dot(a_ref[...], w_ref[...])
        dn_ref[...] = dn
        err = x_ref[...] + _rms_fwd(dn, g_ref[...]) - t_ref[...]
        d3_ref[...] = err * (1.0 / D)
        loss_ref[...] += jnp.sum(err * err, axis=0, keepdims=True)

    return _rowcall("f_ffn_down", body, n, tm,
                    [(act, 'row'), (x2, 'row'), (tgt, 'row'), (w_down, 'full'), (g_post, 'full')],
                    [('row', D, F32), ('row', D, F32), ('acc', (1, D), F32)])


def _b_ffn_down(d3, dn, gu, g_post, w_down, tm):
    n = d3.shape[0]

    def body(d3_ref, dn_ref, gu_ref, g_ref, w_ref, ddn_ref, dgu_ref, dg_ref):
        @pl.when(pl.program_id(0) == 0)
        def _():
            dg_ref[...] = jnp.zeros_like(dg_ref)
        ddn, dg = _rms_bwd(dn_ref[...], g_ref[...], d3_ref[...])
        dg_ref[...] += dg
        ddn = ddn.astype(BF16)
        ddn_ref[...] = ddn
        dact = _dot_nt(ddn, w_ref[...])
        gg = gu_ref[:, :FFN_H].astype(F32)
        uu = gu_ref[:, FFN_H:].astype(F32)
        sg = _sig(gg)
        dgu_ref[:, :FFN_H] = (dact * uu * sg * (1.0 + gg * (1.0 - sg))).astype(BF16)
        dgu_ref[:, FFN_H:] = (dact * gg * sg).astype(BF16)

    return _rowcall("b_ffn_down", body, n, tm,
                    [(d3, 'row'), (dn, 'row'), (gu, 'row'), (g_post, 'full'), (w_down, 'full')],
                    [('row', D, BF16), ('row', 2 * FFN_H, BF16), ('acc', (1, D), F32)])


def _b_ffn_up(dgu, d3, x2, w_gu, g_pre, tm):
    n = d3.shape[0]

    def body(dgu_ref, d3_ref, x_ref, w_ref, g_ref, dx_ref, dg_ref):
        @pl.when(pl.program_id(0) == 0)
        def _():
            dg_ref[...] = jnp.zeros_like(dg_ref)
        dhf = _dot_nt(dgu_ref[...], w_ref[...])
        dx, dg = _rms_bwd(x_ref[...], g_ref[...], dhf)
        dg_ref[...] += dg
        dx_ref[...] = d3_ref[...] + dx

    return _rowcall("b_ffn_up", body, n, tm,
                    [(dgu, 'row'), (d3, 'row'), (x2, 'row'), (w_gu, 'full'), (g_pre, 'full')],
                    [('row', D, F32), ('acc', (1, D), F32)])


def _b_attn(dx2, ao, x1, hc, g_post, w_o, w_q, g_pre, k, v, tm, tpb, mlen):
    n = dx2.shape[0]
    nb = n // (tm * tpb)
    kv_spec = pl.BlockSpec((mlen, D), lambda i: (i // tpb, 0))
    scale = HEAD_DIM ** -0.5

    def body(dx2_ref, ao_ref, x1_ref, hc_ref, go_ref, wo_ref, wq_ref, gp_ref, k_ref, v_ref,
             dao_ref, dq_ref, dx1_ref, dk_ref, dv_ref, dgo_ref, dgp_ref):
        i = pl.program_id(0)

        @pl.when(i == 0)
        def _():
            dgo_ref[...] = jnp.zeros_like(dgo_ref)
            dgp_ref[...] = jnp.zeros_like(dgp_ref)

        @pl.when(i % tpb == 0)
        def _():
            dk_ref[...] = jnp.zeros_like(dk_ref)
            dv_ref[...] = jnp.zeros_like(dv_ref)

        dx2v = dx2_ref[...]
        dao, dgo = _rms_bwd(ao_ref[...], go_ref[...], dx2v)
        dgo_ref[...] += dgo
        dao = dao.astype(BF16)
        dao_ref[...] = dao
        do = _dot_nt(dao, wo_ref[...])
        q = _dot(hc_ref[...], wq_ref[...])
        for h in range(HEADS):
            hs = slice(h * HEAD_DIM, (h + 1) * HEAD_DIM)
            qh = q[:, hs].astype(BF16)
            kh = k_ref[:, hs]
            p = _softmax(_dot_nt(qh, kh) * scale)
            doh = do[:, hs].astype(BF16)
            dp = _dot_nt(doh, v_ref[:, hs])
            ds = (p * (dp - jnp.sum(dp * p, axis=-1, keepdims=True)) * scale).astype(BF16)
            dq_ref[:, hs] = _dot(ds, kh).astype(BF16)
            dk_ref[:, hs] += _dot_tn(ds, qh)
            dv_ref[:, hs] += _dot_tn(p.astype(BF16), doh)
        dhc = _dot_nt(dq_ref[...], wq_ref[...])
        dx, dgp = _rms_bwd(x1_ref[...], gp_ref[...], dhc)
        dgp_ref[...] += dgp
        dx1_ref[...] = dx2v + dx

    return _rowcall("b_attn", body, n, tm,
                    [(dx2, 'row'), (ao, 'row'), (x1, 'row'), (hc, 'row'), (g_post, 'full'), (w_o, 'full'),
                     (w_q, 'full'), (g_pre, 'full'), (k, kv_spec), (v, kv_spec)],
                    [('row', D, BF16), ('row', D, BF16), ('row', D, F32),
                     (kv_spec, (nb * mlen, D), F32), (kv_spec, (nb * mlen, D), F32),
                     ('acc', (1, D), F32), ('acc', (1, D), F32)])


def _b_mem(dk, dv, mem, w_kv, g_mem, tm):
    n = mem.shape[0]

    def body(dk_ref, dv_ref, m_ref, w_ref, g_ref, dkv_ref, dg_ref):
        @pl.when(pl.program_id(0) == 0)
        def _():
            dg_ref[...] = jnp.zeros_like(dg_ref)
        dkb = dk_ref[...].astype(BF16)
        dvb = dv_ref[...].astype(BF16)
        dkv_ref[:, :D] = dkb
        dkv_ref[:, D:] = dvb
        dmn = _dot_nt(dkb, w_ref[:, :D]) + _dot_nt(dvb, w_ref[:, D:])
        _, dg = _rms_bwd(m_ref[...], g_ref[...], dmn)
        dg_ref[...] += dg

    return _rowcall("b_mem", body, n, tm, [(dk, 'row'), (dv, 'row'), (mem, 'row'), (w_kv, 'full'), (g_mem, 'full')],
                    [('row', 2 * D, BF16), ('acc', (1, D), F32)])


def _b_mix(dx1, mo, a, b, zg, ypre, g_post, w_mix_out, w_br_gm, w_br_s5, w_glu, tm):
    n = dx1.shape[0]

    def body(dx1_ref, mo_ref, a_ref, b_ref, zg_ref, ypre_ref, g_ref, wout_ref, wgm_ref, ws5_ref, wglu_ref,
             dmo_ref, da_ref, db_ref, dzg_ref, dygm_ref, dgate_ref, dypre_ref, dg_ref):
        @pl.when(pl.program_id(0) == 0)
        def _():
            dg_ref[...] = jnp.zeros_like(dg_ref)
        dmo, dg = _rms_bwd(mo_ref[...], g_ref[...], dx1_ref[...])
        dg_ref[...] += dg
        dmo = dmo.astype(BF16)
        dmo_ref[...] = dmo
        dmg = _dot_nt(dmo, wout_ref[...])
        zg = zg_ref[...]
        sa, sb = _sig(zg[:, :D]), _sig(zg[:, D:])
        da = (dmg * sa).astype(BF16)
        db = (dmg * sb).astype(BF16)
        da_ref[...] = da
        db_ref[...] = db
        dzg_ref[:, :D] = (dmg * a_ref[...].astype(F32) * sa * (1.0 - sa)).astype(BF16)
        dzg_ref[:, D:] = (dmg * b_ref[...].astype(F32) * sb * (1.0 - sb)).astype(BF16)
        dygm_ref[...] = _dot_nt(da, wgm_ref[...])
        dys5 = _dot_nt(db, ws5_ref[...])
        ypre = ypre_ref[...]
        yg = _gelu(ypre)
        sgt = _sig(_dot(yg.astype(BF16), wglu_ref[...]))
        dgate = (dys5 * yg * sgt * (1.0 - sgt)).astype(BF16)
        dgate_ref[...] = dgate
        dyg = dys5 * sgt + _dot_nt(dgate, wglu_ref[...])
        dypre_ref[...] = dyg * _gelu_grad(ypre)

    return _rowcall("b_mix", body, n, tm,
                    [(dx1, 'row'), (mo, 'row'), (a, 'row'), (b, 'row'), (zg, 'row'), (ypre, 'row'), (g_post, 'full'),
                     (w_mix_out, 'full'), (w_br_gm, 'full'), (w_br_s5, 'full'), (w_glu, 'full')],
                    [('row', D, BF16), ('row', D, BF16), ('row', D, BF16), ('row', 2 * D, BF16), ('row', D, F32),
                     ('row', S5_W, BF16), ('row', S5_W, F32), ('acc', (1, D), F32)])


def _b_s5(zs5, dypre, states, bbc, ccm, avec, dvec, nb):
    n = zs5.shape[0]
    nc = n // nb // S5_T
    slab, bb, cm, av, dv, st = _s5_specs(nb, nc, True)

    def body(u_ref, dy_ref, st_ref, bb_ref, cc_ref, a_ref, d_ref,
             du_ref, dbb_ref, dcc_ref, da_ref, dd_ref, lcarry):
        first = jnp.logical_and(pl.program_id(1) == 0, pl.program_id(2) == 0)

        @pl.when(first)
        def _():
            dbb_ref[...] = jnp.zeros_like(dbb_ref)
            dcc_ref[...] = jnp.zeros_like(dcc_ref)
            da_ref[...] = jnp.zeros_like(da_ref)
            dd_ref[...] = jnp.zeros_like(dd_ref)

        @pl.when(pl.program_id(2) == 0)
        def _():
            lcarry[...] = jnp.zeros_like(lcarry)

        u = u_ref[...]
        dy = dy_ref[...]
        ub, dyb = u.astype(BF16), dy.astype(BF16)
        sin = st_ref[...]
        sr, si = _s5_states(u, bb_ref, a_ref, sin)
        ar, ai = a_ref[:, :512], a_ref[:, 512:]
        ds = _dot_nt(dyb, cc_ref[...])
        lr_, li_ = lcarry[:, :512], lcarry[:, 512:]
        row = lax.broadcasted_iota(jnp.int32, (S5_T, 512), 0)
        last = row == S5_T - 1
        gr = ds[:, :512] + jnp.where(last, ar * lr_ + ai * li_, 0.0)
        gi = ds[:, 512:] + jnp.where(last, ar * li_ - ai * lr_, 0.0)
        lr, li = _cscan(gr, gi, ar, -ai, reverse=True)
        head = row == 0
        lcarry[:, :512] = jnp.sum(jnp.where(head, lr, 0.0), axis=0, keepdims=True)
        lcarry[:, 512:] = jnp.sum(jnp.where(head, li, 0.0), axis=0, keepdims=True)
        s = jnp.concatenate([sr, si], axis=1).astype(BF16)
        lam = jnp.concatenate([lr, li], axis=1).astype(BF16)
        dcc_ref[...] += _dot_tn(s, dyb)
        dbb_ref[...] += _dot_tn(ub, lam)
        du_ref[...] = (_dot_nt(lam, bb_ref[...]) + d_ref[...] * dy).astype(BF16)
        dd_ref[...] += jnp.sum(dy * u, axis=0, keepdims=True)
        pr = jnp.where(head, sin[:, :512], pltpu.roll(sr, 1, 0))
        pi = jnp.where(head, sin[:, 512:], pltpu.roll(si, 1, 0))
        da_ref[:, :512] += jnp.sum(lr * pr + li * pi, axis=0, keepdims=True)
        da_ref[:, 512:] += jnp.sum(li * pr - lr * pi, axis=0, keepdims=True)

    return pl.pallas_call(
        body, name="b_s5", grid=(S5_SUPER, nb, nc), in_specs=[slab, slab, st, bb, cm, av, dv],
        out_specs=[slab, bb, cm, av, dv],
        out_shape=[jax.ShapeDtypeStruct((n, S5_W), BF16), jax.ShapeDtypeStruct(bbc.shape, F32),
                   jax.ShapeDtypeStruct(ccm.shape, F32), jax.ShapeDtypeStruct(avec.shape, F32),
                   jax.ShapeDtypeStruct(dvec.shape, F32)],
        scratch_shapes=[pltpu.VMEM((1, 1024), F32)],
        compiler_params=_params(("arbitrary", "arbitrary", "arbitrary")),
    )(zs5, dypre, states, bbc, ccm, avec, dvec)


def _b_gmlp(zgm, dygm, ln_g, ln_b, w_s, b_s, tm):
    n = zgm.shape[0]

    def body(z_ref, dy_ref, lg_ref, lb_ref, ws_ref, bs_ref, dz_ref, dws_ref, dbs_ref, dlg_ref, dlb_ref,
             du_s, dvn_s):
        @pl.when(pl.program_id(0) == 0)
        def _():
            dws_ref[...] = jnp.zeros_like(dws_ref)
            dbs_ref[...] = jnp.zeros_like(dbs_ref)
            dlg_ref[...] = jnp.zeros_like(dlg_ref)
            dlb_ref[...] = jnp.zeros_like(dlb_ref)
        z = z_ref[...]
        zg = _gelu(z)
        u = zg[:, :D]
        vh, r = _ln_stats(zg[:, D:])
        vn = (vh * lg_ref[...] + lb_ref[...]).astype(BF16)
        dy = dy_ref[...]
        keep = _tril()
        for g in range(GM_GROUPS):
            w = jnp.where(keep, ws_ref[g], 0.0).astype(BF16)
            cs = slice(g * LANE, (g + 1) * LANE)
            for c in range(tm // GM_CHUNK):
                rs = slice(c * GM_CHUNK, (c + 1) * GM_CHUNK)
                vb = vn[rs, cs]
                sv = _dot(w, vb) + bs_ref[g]
                dyb = dy[rs, cs]
                du_s[rs, cs] = dyb * sv
                dsv = dyb * u[rs, cs]
                dsvb = dsv.astype(BF16)
                dvn_s[rs, cs] = _dot_tn(w, dsvb)
                dws_ref[g] += jnp.where(keep, _dot_nt(dsvb, vb), 0.0)
                dbs_ref[g] += jnp.sum(dsv, axis=1, keepdims=True)
        dvn = dvn_s[...]
        dlg_ref[...] += jnp.sum(dvn * vh, axis=0, keepdims=True)
        dlb_ref[...] += jnp.sum(dvn, axis=0, keepdims=True)
        dvh = dvn * lg_ref[...]
        dv = r * (dvh - jnp.mean(dvh, axis=-1, keepdims=True) - vh * jnp.mean(dvh * vh, axis=-1, keepdims=True))
        dz_ref[:, :D] = (du_s[...] * _gelu_grad(z[:, :D])).astype(BF16)
        dz_ref[:, D:] = (dv * _gelu_grad(z[:, D:])).astype(BF16)

    return _rowcall("b_gmlp", body, n, tm,
                    [(zgm, 'row'), (dygm, 'row'), (ln_g, 'full'), (ln_b, 'full'), (w_s, 'full'), (b_s, 'full')],
                    [('row', 2 * D, BF16), ('acc', w_s.shape, F32), ('acc', b_s.shape, F32), ('acc', (1, D), F32),
                     ('acc', (1, D), F32)],
                    scratch=[pltpu.VMEM((tm, D), F32), pltpu.VMEM((tm, D), F32)])


def _b_in(dzgm, dzs5, dzg, dx1, x, w_in, g_pre, tm):
    n = x.shape[0]

    def body(d1_ref, d2_ref, d3_ref, dx1_ref, x_ref, w_ref, g_ref, gx_ref, dg_ref):
        @pl.when(pl.program_id(0) == 0)
        def _():
            dg_ref[...] = jnp.zeros_like(dg_ref)
        dh = (_dot_nt(d1_ref[...], w_ref[:, 0:2 * D]) + _dot_nt(d2_ref[...], w_ref[:, 2 * D:2 * D + S5_W])
              + _dot_nt(d3_ref[...], w_ref[:, 2 * D + S5_W:]))
        dx, dg = _rms_bwd(x_ref[...], g_ref[...], dh)
        dg_ref[...] += dg
        gx_ref[...] = dx1_ref[...] + dx

    return _rowcall("b_in", body, n, tm,
                    [(dzgm, 'row'), (dzs5, 'row'), (dzg, 'row'), (dx1, 'row'), (x, 'row'), (w_in, 'full'),
                     (g_pre, 'full')],
                    [('row', D, F32), ('acc', (1, D), F32)])


def _whole(name, body, ins, outs):
    return pl.pallas_call(body, name=name, out_shape=[jax.ShapeDtypeStruct(s, dt) for s, dt in outs],
                          compiler_params=_params())(*ins)


def _s5_disc_fwd(lr, li, ls, br, bi):
    def body(lr_ref, li_ref, ls_ref, br_ref, bi_ref, o1, o2, o3, o4):
        outs = _s5_disc(lr_ref[...], li_ref[...], ls_ref[...], br_ref[...], bi_ref[...])
        for o, val in zip((o1, o2, o3, o4), outs):
            o[...] = val

    return _whole("s5_disc_fwd", body, [lr, li, ls, br, bi],
                  [(lr.shape, F32), (lr.shape, F32), (br.shape, F32), (br.shape, F32)])


def _s5_disc_bwd(lr, li, ls, br, bi, cts):
    def body(lr_ref, li_ref, ls_ref, br_ref, bi_ref, c1, c2, c3, c4, o1, o2, o3, o4, o5):
        _, vjp = jax.vjp(_s5_disc, lr_ref[...], li_ref[...], ls_ref[...], br_ref[...], bi_ref[...])
        grads = vjp((c1[...], c2[...], c3[...], c4[...]))
        for o, val in zip((o1, o2, o3, o4, o5), grads):
            o[...] = val

    return _whole("s5_disc_bwd", body, [lr, li, ls, br, bi, *cts],
                  [(lr.shape, F32), (lr.shape, F32), (lr.shape, F32), (br.shape, F32), (br.shape, F32)])


def _cast_rows(name, a, dtype, tm):
    def body(a_ref, o_ref):
        o_ref[...] = a_ref[...].astype(dtype)

    return _rowcall(name, body, a.shape[0], tm, [(a, 'row')], [('row', a.shape[1], dtype)])[0]


def _sum_rows(name, parts, tm):
    def body(*refs):
        acc = refs[0][...]
        for r in refs[1:-1]:
            acc = acc + r[...]
        refs[-1][...] = acc

    return _rowcall(name, body, parts[0].shape[0], tm, [(p, 'row') for p in parts],
                    [('row', parts[0].shape[1], F32)])[0]


def _adamw(name, w, g, m, v, tm):
    c1 = 1.0 - ADAM_B1 ** ADAM_STEP
    c2 = 1.0 - ADAM_B2 ** ADAM_STEP

    def body(w_ref, g_ref, m_ref, v_ref, d_ref, mo_ref, vo_ref):
        gg = g_ref[...]
        mn = ADAM_B1 * m_ref[...] + (1.0 - ADAM_B1) * gg
        vn = ADAM_B2 * v_ref[...] + (1.0 - ADAM_B2) * (gg * gg)
        mo_ref[...] = mn
        vo_ref[...] = vn
        d_ref[...] = -ADAM_LR * ((mn / c1) / (jnp.sqrt(vn / c2) + ADAM_EPS) + ADAM_WD * w_ref[...])

    cols = w.shape[1]
    return _rowcall(name, body, w.shape[0], tm, [(w, 'row'), (g, 'row'), (m, 'row'), (v, 'row')],
                    [('row', cols, F32), ('row', cols, F32), ('row', cols, F32)])


def _rows1024(a):
    flat = a.reshape(-1)
    pad = (-flat.shape[0]) % D
    if pad:
        flat = jnp.concatenate([flat, jnp.zeros((pad,), flat.dtype)])
    return flat.reshape(-1, D)


def _pack(arrs, pad_rows_to=8):
    parts = [_rows1024(a) for a in arrs]
    rows = sum(p.shape[0] for p in parts)
    pad = (-rows) % pad_rows_to
    if pad:
        parts.append(jnp.zeros((pad, D), parts[0].dtype))
    return jnp.concatenate(parts, axis=0)


def _unpack(packed, shapes):
    out, r = [], 0
    for s in shapes:
        size = math.prod(s)
        nr = -(-size // D)
        out.append(packed[r:r + nr].reshape(-1)[:size].reshape(s))
        r += nr
    return out


def _shard_slabs(full, axis):
    r, c = full.shape
    if axis == 0:
        return full.reshape(4, -1, D)
    return full.reshape(r, 4, c // 4).transpose(1, 0, 2).reshape(4, -1, D)


def _unshard(slabs, local_shape, axis):
    r, c = local_shape
    if axis == 0:
        return slabs.reshape(4 * r, c)
    return slabs.reshape(4, r, c).transpose(1, 0, 2).reshape(r, 4 * c)


def _block_diag(t):
    eye = jnp.eye(8, dtype=t.dtype)
    j, g, a, b = t.shape
    return (t[:, :, :, None, :] * eye[None, :, None, :, None]).reshape(j, g * a, g * b)


def _block_diag_take(m, a, b):
    eye = jnp.eye(8, dtype=m.dtype)
    return (m.reshape(4, 8, a, 8, b) * eye[None, :, None, :, None]).sum(axis=3)


def kernel(x, mem, g_mix_pre, w_in, gm_ln_g, gm_ln_b, gm_w_s, gm_b_s, s5_lam_re, s5_lam_im, s5_log_step, s5_b_re, s5_b_im, s5_c_re, s5_c_im, s5_d, s5_w_glu, w_br_gm, w_br_s5, w_mix_out, g_mix_post, g_ca_pre, g_mem, ca_w_q, ca_w_kv, ca_w_o, g_ca_post, g_ffn_pre, ffn_w_gu, ffn_w_down, g_ffn_post, loss_target, m_g_mix_pre, m_w_in, m_gm_ln_g, m_gm_ln_b, m_gm_w_s, m_gm_b_s, m_s5_lam_re, m_s5_lam_im, m_s5_log_step, m_s5_b_re, m_s5_b_im, m_s5_c_re, m_s5_c_im, m_s5_d, m_s5_w_glu, m_w_br_gm, m_w_br_s5, m_w_mix_out, m_g_mix_post, m_g_ca_pre, m_g_mem, m_ca_w_q, m_ca_w_kv, m_ca_w_o, m_g_ca_post, m_g_ffn_pre, m_ffn_w_gu, m_ffn_w_down, m_g_ffn_post, v_g_mix_pre, v_w_in, v_gm_ln_g, v_gm_ln_b, v_gm_w_s, v_gm_b_s, v_s5_lam_re, v_s5_lam_im, v_s5_log_step, v_s5_b_re, v_s5_b_im, v_s5_c_re, v_s5_c_im, v_s5_d, v_s5_w_glu, v_w_br_gm, v_w_br_s5, v_w_mix_out, v_g_mix_post, v_g_ca_pre, v_g_mem, v_ca_w_q, v_ca_w_kv, v_ca_w_o, v_g_ca_post, v_g_ffn_pre, v_ffn_w_gu, v_ffn_w_down, v_g_ffn_post):
    a = dict(locals())
    w = {n: a[n][0] for n in WNAMES}
    nb, seq, _ = x.shape
    n = nb * seq
    mlen = mem.shape[1]
    tm = min(256, seq)
    tpb = seq // tm
    xf = x.reshape(n, D)
    tgt = loss_target.reshape(n, D)
    memf = mem.reshape(nb * mlen, D)
    tmm = min(256, mlen)

    big_names = list(BIG)
    local_shapes = {k: w[k].shape for k in big_names}
    wpack = _pack([w[k] for k in big_names], 1)
    rows_big = wpack.shape[0]
    half = rows_big // 2
    tm_pack = _tile_rows(half)
    wpack_bf = _cast_rows("cast_w", wpack, BF16, tm_pack)
    chip = lambda px, py, pc: 2 * px + py
    gathered = _exchange(
        "gather_w", wpack_bf, (4, rows_big, D), BF16,
        [(f, lambda px, py, pc: None, chip) for f in ((1, 0, 0), (0, 1, 0), (1, 1, 0))],
        local=(lambda px, py, pc: None, chip))
    wf, r0 = {}, 0
    for k in big_names:
        nr = math.prod(local_shapes[k]) // D
        wf[k] = _unshard(gathered[:, r0:r0 + nr], local_shapes[k], BIG[k])
        r0 += nr

    vec = lambda name: w[name].reshape(1, D)

    to_lane = lambda p: p.reshape(1, -1)
    b_t = lambda p: p.transpose(2, 0, 1).reshape(16, -1)
    lr_l, li_l = to_lane(w['s5_lam_re']), to_lane(w['s5_lam_im'])
    ls_l = jnp.repeat(w['s5_log_step'], 64).reshape(1, -1)
    br_t, bi_t = b_t(w['s5_b_re']), b_t(w['s5_b_im'])
    ab_re, ab_im, bb_re, bb_im = _s5_disc_fwd(lr_l, li_l, ls_l, br_t, bi_t)
    blk = lambda t: _block_diag(t.reshape(16, 4, 8, 64).transpose(1, 2, 0, 3))
    bbc = jnp.concatenate([blk(bb_re), blk(bb_im)], axis=2).astype(BF16)
    cblk = lambda c: _block_diag(c.reshape(4, 8, 16, 64).transpose(0, 1, 3, 2))
    ccm = jnp.concatenate([cblk(w['s5_c_re']), -cblk(w['s5_c_im'])], axis=1).astype(BF16)
    avec = jnp.concatenate([ab_re.reshape(4, 1, 512), ab_im.reshape(4, 1, 512)], axis=2)
    dvec = w['s5_d'].reshape(4, 1, LANE)

    bs3 = w['gm_b_s'].reshape(GM_GROUPS, GM_CHUNK, 1)
    h, zgm, zs5, zg = _f_in(xf, vec('g_mix_pre'), wf['w_in'], tm)
    ygm = _f_gmlp(zgm, vec('gm_ln_g'), vec('gm_ln_b'), w['gm_w_s'], bs3, tm)
    ypre, states = _f_s5(zs5, bbc, ccm, avec, dvec, nb)
    yg, ys5, a_br, b_br, merged, mo, x1 = _f_mix(ygm, ypre, zg, xf, wf['s5_w_glu'], wf['w_br_gm'], wf['w_br_s5'],
                                                 wf['w_mix_out'], vec('g_mix_post'), tm)
    mem_n, kk, vv = _f_mem(memf, vec('g_mem'), wf['ca_w_kv'], tmm)
    hc, o_att, ao, x2 = _f_attn(x1, vec('g_ca_pre'), wf['ca_w_q'], kk, vv, wf['ca_w_o'], vec('g_ca_post'),
                                tm, tpb, mlen)
    hf, gu, act = _f_ffn_up(x2, vec('g_ffn_pre'), wf['ffn_w_gu'], tm)
    dn, d3, loss_cols = _f_ffn_down(act, x2, tgt, wf['ffn_w_down'], vec('g_ffn_post'), tm)

    gsm = {}
    gbig = {}
    ddn, dgu, gsm['g_ffn_post'] = _b_ffn_down(d3, dn, gu, vec('g_ffn_post'), wf['ffn_w_down'], tm)
    dx2, gsm['g_ffn_pre'] = _b_ffn_up(dgu, d3, x2, wf['ffn_w_gu'], vec('g_ffn_pre'), tm)
    gbig['ffn_w_down'] = _mm_tn("dw_ffn_down", act, ddn)
    gbig['ffn_w_gu'] = _mm_tn("dw_ffn_gu", hf, dgu)
    dao, dq, dx1, dk, dv, gsm['g_ca_post'], gsm['g_ca_pre'] = _b_attn(
        dx2, ao, x1, hc, vec('g_ca_post'), wf['ca_w_o'], wf['ca_w_q'], vec('g_ca_pre'), kk, vv, tm, tpb, mlen)
    gbig['ca_w_o'] = _mm_tn("dw_ca_o", o_att, dao)
    gbig['ca_w_q'] = _mm_tn("dw_ca_q", hc, dq)
    dkv, gsm['g_mem'] = _b_mem(dk, dv, memf, wf['ca_w_kv'], vec('g_mem'), tmm)
    gbig['ca_w_kv'] = _mm_tn("dw_ca_kv", mem_n, dkv)
    dmo, da_br, db_br, dzg, dygm, dgate, dypre, gsm['g_mix_post'] = _b_mix(
        dx1, mo, a_br, b_br, zg, ypre, vec('g_mix_post'), wf['w_mix_out'], wf['w_br_gm'], wf['w_br_s5'],
        wf['s5_w_glu'], tm)
    gbig['w_mix_out'] = _mm_tn("dw_mix_out", merged, dmo)
    gbig['w_br_gm'] = _mm_tn("dw_br_gm", ygm, da_br)
    gbig['w_br_s5'] = _mm_tn("dw_br_s5", ys5, db_br)
    gbig['s5_w_glu'] = _mm_tn("dw_s5_glu", yg, dgate)
    dzs5, dbbc, dccm, davec, ddvec = _b_s5(zs5, dypre, states, bbc, ccm, avec, dvec, nb)
    dzgm, gsm['gm_w_s'], dbs3, gsm['gm_ln_g'], gsm['gm_ln_b'] = _b_gmlp(
        zgm, dygm, vec('gm_ln_g'), vec('gm_ln_b'), w['gm_w_s'], bs3, tm)
    gsm['gm_b_s'] = dbs3
    grad_x, gsm['g_mix_pre'] = _b_in(dzgm, dzs5, dzg, dx1, xf, wf['w_in'], vec('g_mix_pre'), tm)
    gbig['w_in'] = jnp.concatenate(
        [_mm_tn("dw_in_gm", h, dzgm), _mm_tn("dw_in_s5", h, dzs5), _mm_tn("dw_in_g", h, dzg)], axis=1)

    unblk = lambda m_: _block_diag_take(m_, 16, 64).transpose(2, 0, 1, 3).reshape(16, -1)
    d_bb_re, d_bb_im = unblk(dbbc[:, :, :512]), unblk(dbbc[:, :, 512:])
    cunblk = lambda m_: _block_diag_take(m_, 64, 16).transpose(0, 1, 3, 2).reshape(32, 16, 64)
    gsm['s5_c_re'] = cunblk(dccm[:, :512, :])
    gsm['s5_c_im'] = -cunblk(dccm[:, 512:, :])
    d_ab_re, d_ab_im = davec[:, :, :512].reshape(1, -1), davec[:, :, 512:].reshape(1, -1)
    g_lr, g_li, g_ls, g_br, g_bi = _s5_disc_bwd(lr_l, li_l, ls_l, br_t, bi_t, (d_ab_re, d_ab_im, d_bb_re, d_bb_im))
    gsm['s5_lam_re'], gsm['s5_lam_im'] = g_lr.reshape(32, 64), g_li.reshape(32, 64)
    gsm['s5_log_step'] = g_ls.reshape(32, 64).sum(axis=1)
    from_t = lambda t: t.reshape(16, 32, 64).transpose(1, 2, 0)
    gsm['s5_b_re'], gsm['s5_b_im'] = from_t(g_br), from_t(g_bi)
    gsm['s5_d'] = ddvec.reshape(32, 16)

    small_shapes = [w[k].shape for k in SMALL]
    spack = _pack([gsm[k].reshape(w[k].shape) for k in SMALL] + [loss_cols], 8)
    rs = spack.shape[0]
    me_slot = lambda px, py, pc: 4 * px + 2 * py + pc
    flips7 = [(fx, fy, fc) for fx in (0, 1) for fy in (0, 1) for fc in (0, 1) if fx or fy or fc]
    sall = _exchange("gather_small", spack, (8, rs, D), F32,
                     [(f, lambda px, py, pc: None, me_slot) for f in flips7],
                     local=(lambda px, py, pc: None, me_slot))
    ssum = _sum_rows("sum_small", [sall[i] for i in range(8)], rs)
    small_red = _unpack(ssum, small_shapes + [(1, D)])
    loss = 0.5 * jnp.sum(small_red[-1]) / D

    gpack = jnp.concatenate([_shard_slabs(gbig[k], BIG[k]) for k in big_names], axis=1)
    gpack = gpack.reshape(4, 2, half, D)
    other_half = lambda px, py, pc: (slice(None), 1 - pc)
    from_sib = _exchange("reduce_pair", gpack, (4, half, D), F32,
                         [((0, 0, 1), other_half, lambda px, py, pc: None)])
    my_c = lax.axis_index("c")
    my_chip = 2 * lax.axis_index("x") + lax.axis_index("y")
    mine = lax.dynamic_index_in_dim(gpack, my_c, axis=1, keepdims=False)
    pair = _sum_rows("sum_pair", [mine.reshape(4 * half, D), from_sib.reshape(4 * half, D)], tm_pack)
    pair = pair.reshape(4, half, D)
    to_chip = [((1, 0, 0), 0), ((0, 1, 0), 1), ((1, 1, 0), 2)]

    def peer_chip(f):
        return lambda px, py, pc: 2 * (1 - px if f[0] else px) + (1 - py if f[1] else py)

    from_chips = _exchange("reduce_chips", pair, (3, half, D), F32,
                           [(f, peer_chip(f), (lambda s: lambda px, py, pc: s)(slot)) for f, slot in to_chip])
    own = lax.dynamic_index_in_dim(pair, my_chip, axis=0, keepdims=False)
    red_half = _sum_rows("sum_chips", [own, from_chips[0], from_chips[1], from_chips[2]], tm_pack)
    gred = _exchange("share_pair", red_half, (2, half, D), F32,
                     [((0, 0, 1), lambda px, py, pc: None, lambda px, py, pc: pc)],
                     local=(lambda px, py, pc: None, lambda px, py, pc: pc)).reshape(rows_big, D)

    mpack = _pack([a['m_' + k][0] for k in big_names], 1)
    vpack = _pack([a['v_' + k][0] for k in big_names], 1)
    dl_b, mn_b, vn_b = _adamw("adamw_big", wpack, gred, mpack, vpack, tm_pack)
    big_shapes = [local_shapes[k] for k in big_names]
    res_big = {k: vals for k, vals in zip(big_names, zip(*[_unpack(p, big_shapes) for p in (gred, dl_b, mn_b, vn_b)]))}
    pad_loss = [jnp.zeros((1, D), F32)]
    wsm = _pack([w[k] for k in SMALL] + pad_loss, 8)
    msm = _pack([a['m_' + k][0] for k in SMALL] + pad_loss, 8)
    vsm = _pack([a['v_' + k][0] for k in SMALL] + pad_loss, 8)
    dl_s, mn_s, vn_s = _adamw("adamw_small", wsm, ssum, msm, vsm, rs)
    res_small = {k: vals for k, vals in zip(SMALL, zip(*[_unpack(p, small_shapes) for p in (ssum, dl_s, mn_s, vn_s)]))}

    res = {**res_big, **res_small}
    outs = [loss, grad_x.reshape(nb, seq, D)]
    for i in range(4):
        outs += [res[k][i][None] for k in WNAMES]
    return tuple(outs)


def _tile_rows(rows):
    for t in (416, 512, 256, 128, 64, 32, 16, 8):
        if rows % t == 0:
            return t
    return rows
```

```python
import functools
import math

import jax
import jax.numpy as jnp
from jax import lax
from jax.experimental import pallas as pl
from jax.experimental.pallas import tpu as pltpu

F32 = jnp.float32
BF16 = jnp.bfloat16
EPS = 1e-6
D = 1024
GM_CHUNK = 128
GM_GROUPS = 8
S5_W = 512
S5_SUPER = 4
S5_T = 128
HEADS = 4
HEAD_DIM = 256
FFN_H = 2816
LANE = 128
VMEM_LIMIT = 56 * 1024 * 1024
MESH = pl.DeviceIdType.MESH

ADAM_LR, ADAM_B1, ADAM_B2, ADAM_EPS, ADAM_WD, ADAM_STEP = 0.001, 0.9, 0.999, 1e-08, 0.01, 10

WNAMES = ['g_mix_pre', 'w_in', 'gm_ln_g', 'gm_ln_b', 'gm_w_s', 'gm_b_s', 's5_lam_re', 's5_lam_im', 's5_log_step',
          's5_b_re', 's5_b_im', 's5_c_re', 's5_c_im', 's5_d', 's5_w_glu', 'w_br_gm', 'w_br_s5', 'w_mix_out',
          'g_mix_post', 'g_ca_pre', 'g_mem', 'ca_w_q', 'ca_w_kv', 'ca_w_o', 'g_ca_post', 'g_ffn_pre', 'ffn_w_gu',
          'ffn_w_down', 'g_ffn_post']
BIG = {'w_in': 1, 's5_w_glu': 0, 'w_br_gm': 0, 'w_br_s5': 1, 'w_mix_out': 0, 'ca_w_q': 0, 'ca_w_kv': 1,
       'ca_w_o': 0, 'ffn_w_gu': 1, 'ffn_w_down': 0}
SMALL = [n for n in WNAMES if n not in BIG]


def _dot(a, b):
    return jnp.dot(a, b, preferred_element_type=F32)


def _dot_nt(a, b):
    return lax.dot_general(a, b, (((1,), (1,)), ((), ())), preferred_element_type=F32)


def _dot_tn(a, b):
    return lax.dot_general(a, b, (((0,), (0,)), ((), ())), preferred_element_type=F32)


def _rms_fwd(x, g):
    r = lax.rsqrt(jnp.mean(x * x, axis=-1, keepdims=True) + EPS)
    return x * r * g


def _rms_bwd(x, g, dy):
    r = lax.rsqrt(jnp.mean(x * x, axis=-1, keepdims=True) + EPS)
    xh = x * r
    gdy = dy * g
    dx = r * (gdy - xh * jnp.mean(gdy * xh, axis=-1, keepdims=True))
    dg = jnp.sum(dy * xh, axis=0, keepdims=True)
    return dx, dg


_GC = math.sqrt(2.0 / math.pi)


def _gelu(x):
    return 0.5 * x * (1.0 + jnp.tanh(_GC * (x + 0.044715 * x * x * x)))


def _gelu_grad(x):
    t = jnp.tanh(_GC * (x + 0.044715 * x * x * x))
    return 0.5 * (1.0 + t) + 0.5 * x * (1.0 - t * t) * _GC * (1.0 + 3 * 0.044715 * x * x)


def _sig(x):
    return 1.0 / (1.0 + jnp.exp(-x))


def _cscan(br, bi, ar, ai, reverse):
    t = br.shape[0]
    row = lax.broadcasted_iota(jnp.int32, br.shape, 0)
    pr, pi = ar, ai
    sh = 1
    while sh < t:
        if reverse:
            keep = row < t - sh
            rr, ri = pltpu.roll(br, t - sh, 0), pltpu.roll(bi, t - sh, 0)
        else:
            keep = row >= sh
            rr, ri = pltpu.roll(br, sh, 0), pltpu.roll(bi, sh, 0)
        rr = jnp.where(keep, rr, 0.0)
        ri = jnp.where(keep, ri, 0.0)
        br, bi = br + pr * rr - pi * ri, bi + pr * ri + pi * rr
        pr, pi = pr * pr - pi * pi, 2.0 * pr * pi
        sh *= 2
    return br, bi


def _s5_disc(lr, li, ls, br, bi):
    step = jnp.exp(ls)
    mag = jnp.exp(lr * step)
    ab_re = mag * jnp.cos(li * step)
    ab_im = mag * jnp.sin(li * step)
    den = lr * lr + li * li
    nr = ab_re - 1.0
    co_re = (nr * lr + ab_im * li) / den
    co_im = (ab_im * lr - nr * li) / den
    return ab_re, ab_im, co_re * br - co_im * bi, co_re * bi + co_im * br


def _params(sem=None):
    return pltpu.CompilerParams(dimension_semantics=sem, vmem_limit_bytes=VMEM_LIMIT)


def _rowcall(name, body, n_rows, tm, ins, outs, scratch=()):
    def spec(kind, shape):
        if kind == 'row':
            return pl.BlockSpec((tm, shape[1]), lambda i: (i, 0))
        if kind in ('full', 'acc'):
            nd = len(shape)
            return pl.BlockSpec(tuple(shape), lambda i: (0,) * nd)
        return kind

    in_specs = [spec(k, a.shape) for a, k in ins]
    out_shape, out_specs = [], []
    for k, s, dt in outs:
        shape = (n_rows, s) if k == 'row' else tuple(s)
        out_shape.append(jax.ShapeDtypeStruct(shape, dt))
        out_specs.append(spec(k, shape))
    return pl.pallas_call(
        body, name=name, grid=(n_rows // tm,), in_specs=in_specs, out_specs=out_specs, out_shape=out_shape,
        scratch_shapes=list(scratch), compiler_params=_params(("arbitrary",)),
    )(*[a for a, _ in ins])


def _tile(n, cap):
    if n <= cap:
        return n
    best = LANE
    for k in range(1, n // LANE + 1):
        t = k * LANE
        if n % t == 0 and t <= cap:
            best = t
    return best


def _mm_tn(name, a, b, col_shards=1):
    n, k = a.shape
    m = b.shape[1]
    mloc = m // col_shards
    tk, tn, tr = _tile(k, 1536), _tile(mloc, 1536), min(n, 512)
    per = mloc // tn

    def body(a_ref, b_ref, o_ref):
        @pl.when(pl.program_id(2) == 0)
        def _():
            o_ref[...] = jnp.zeros_like(o_ref)
        o_ref[...] += _dot_tn(a_ref[...], b_ref[...])

    if col_shards == 1:
        out_spec = pl.BlockSpec((tk, tn), lambda i, j, r: (i, j))
        out_shape = (k, m)
    else:
        out_spec = pl.BlockSpec((None, tk, tn), lambda i, j, r: (j // per, i, j % per))
        out_shape = (col_shards, k, mloc)
    return pl.pallas_call(
        body, name=name, grid=(k // tk, m // tn, n // tr),
        in_specs=[pl.BlockSpec((tr, tk), lambda i, j, r: (r, i)), pl.BlockSpec((tr, tn), lambda i, j, r: (r, j))],
        out_specs=out_spec, out_shape=jax.ShapeDtypeStruct(out_shape, F32),
        compiler_params=_params(("parallel", "parallel", "arbitrary")),
    )(a, b)


def _comm_call(name, srcs, outs, nsem, nlocal, body_fn):
    ns, no = len(srcs), len(outs)

    def body(*refs):
        body_fn(refs[:ns], refs[ns:ns + no], refs[ns + no], refs[ns + no + 1], refs[ns + no + 2])

    hbm = pl.BlockSpec(memory_space=pltpu.HBM)
    return pl.pallas_call(
        body, name=name, in_specs=[hbm] * ns, out_specs=[hbm] * no,
        out_shape=[jax.ShapeDtypeStruct(s, d) for s, d in outs],
        scratch_shapes=[pltpu.SemaphoreType.DMA((nsem,)), pltpu.SemaphoreType.DMA((nsem,)),
                        pltpu.SemaphoreType.DMA((max(nlocal, 1),))],
    )(*srcs)


def _place():
    x, y, c = lax.axis_index("x"), lax.axis_index("y"), lax.axis_index("c")
    chips = (2 * x + y, 2 * (1 - x) + y, 2 * x + (1 - y), 2 * (1 - x) + (1 - y))
    peers = ((1 - x, y, c), (x, 1 - y, c), (x, y, 1 - c))
    return c, chips, peers


def _gather_weights(shards, axes):
    nw = len(shards)
    outs = [((4,) + s.shape if ax == 0 else (s.shape[0], 4 * s.shape[1]), BF16) for s, ax in zip(shards, axes)]

    def body_fn(srcs, dsts, ss, rs, ls):
        c, (me, xn, yn, _), (px, py, psib) = _place()

        def rdma(k, src, dst, peer):
            return pltpu.make_async_remote_copy(src_ref=src, dst_ref=dst, send_sem=ss.at[k], recv_sem=rs.at[k],
                                                device_id=peer, device_id_type=MESH)

        def win(i, chip, start, rows):
            r, cols = shards[i].shape
            if axes[i] == 0:
                return dsts[i].at[chip, pl.ds(start, rows), :]
            return dsts[i].at[pl.ds(start, rows), pl.ds(chip * cols, cols)]

        first, later, local = [], [], []
        for i in range(nw):
            r = shards[i].shape[0]
            qr, k0 = r // 4, 7 * i
            q0, q1 = 2 * c * qr, (2 * c + 1) * qr
            lc = pltpu.make_async_copy(srcs[i].at[pl.ds(q0, 2 * qr), :], win(i, me, q0, 2 * qr), ls.at[i])
            lc.start()
            local.append(lc)
            s0, s1 = srcs[i].at[pl.ds(q0, qr), :], srcs[i].at[pl.ds(q1, qr), :]
            cps = [rdma(k0, s0, win(i, me, q0, qr), px), rdma(k0 + 1, s1, win(i, me, q1, qr), py),
                   rdma(k0 + 2, s0, win(i, me, q0, qr), py), rdma(k0 + 3, s1, win(i, me, q1, qr), px)]
            for cp in cps:
                cp.start()
            first.append(cps)
        for i in range(nw):
            qr, k0 = shards[i].shape[0] // 4, 7 * i
            q0, q1 = 2 * c * qr, (2 * c + 1) * qr
            first[i][0].wait_recv()
            f0 = rdma(k0 + 4, win(i, xn, q0, qr), win(i, xn, q0, qr), py)
            f0.start()
            first[i][1].wait_recv()
            f1 = rdma(k0 + 5, win(i, yn, q1, qr), win(i, yn, q1, qr), px)
            f1.start()
            later.append([f0, f1])
        swaps = []
        for i in range(nw):
            r = shards[i].shape[0]
            first[i][2].wait_recv()
            first[i][3].wait_recv()
            later[i][0].wait_recv()
            later[i][1].wait_recv()
            local[i].wait()
            if axes[i] == 0:
                halfwin = dsts[i].at[:, pl.ds(c * (r // 2), r // 2), :]
            else:
                halfwin = dsts[i].at[pl.ds(c * (r // 2), r // 2), :]
            sw = rdma(7 * i + 6, halfwin, halfwin, psib)
            sw.start()
            swaps.append(sw)
        for i in range(nw):
            swaps[i].wait()
            for cp in first[i] + later[i]:
                cp.wait_send()

    return _comm_call("gather_w", shards, outs, 7 * nw, nw, body_fn)


def _exchange(name, src, out_shape, dtype, sends, local=None):
    n = len(sends)

    def body(src_ref, out_ref, send_sems, recv_sems, local_sem):
        me = (lax.axis_index("x"), lax.axis_index("y"), lax.axis_index("c"))

        def at(ref, idx):
            return ref.at[idx] if idx is not None else ref

        if local is not None:
            mine = pltpu.make_async_copy(at(src_ref, local[0](*me)), at(out_ref, local[1](*me)), local_sem)
            mine.start()
        copies = []
        for k, (flips, sidx, didx) in enumerate(sends):
            peer = tuple(1 - p if f else p for p, f in zip(me, flips))
            cp = pltpu.make_async_remote_copy(
                src_ref=at(src_ref, sidx(*me)), dst_ref=at(out_ref, didx(*me)),
                send_sem=send_sems.at[k], recv_sem=recv_sems.at[k], device_id=peer, device_id_type=MESH)
            cp.start()
            copies.append(cp)
        for cp in copies:
            cp.wait()
        if local is not None:
            mine.wait()

    hbm = pl.BlockSpec(memory_space=pltpu.HBM)
    return pl.pallas_call(
        body, name=name, in_specs=[hbm], out_specs=hbm, out_shape=jax.ShapeDtypeStruct(out_shape, dtype),
        scratch_shapes=[pltpu.SemaphoreType.DMA((n,)), pltpu.SemaphoreType.DMA((n,)), pltpu.SemaphoreType.DMA],
    )(src)


def _f_in(x, g, w_in, tm):
    n = x.shape[0]

    def body(x_ref, g_ref, w_ref, h_ref, zgm_ref, zs5_ref, zg_ref):
        h = _rms_fwd(x_ref[...], g_ref[...]).astype(BF16)
        h_ref[...] = h
        zgm_ref[...] = _dot(h, w_ref[:, 0:2 * D])
        zs5_ref[...] = _dot(h, w_ref[:, 2 * D:2 * D + S5_W])
        zg_ref[...] = _dot(h, w_ref[:, 2 * D + S5_W:])

    return _rowcall("f_in", body, n, tm, [(x, 'row'), (g, 'full'), (w_in, 'full')],
                    [('row', D, BF16), ('row', 2 * D, F32), ('row', S5_W, F32), ('row', 2 * D, F32)])


def _tril():
    r = lax.broadcasted_iota(jnp.int32, (GM_CHUNK, GM_CHUNK), 0)
    c = lax.broadcasted_iota(jnp.int32, (GM_CHUNK, GM_CHUNK), 1)
    return r >= c


def _ln_stats(v):
    mu = jnp.mean(v, axis=-1, keepdims=True)
    vc = v - mu
    r = lax.rsqrt(jnp.mean(vc * vc, axis=-1, keepdims=True) + EPS)
    return vc * r, r


def _f_gmlp(zgm, ln_g, ln_b, w_s, b_s, tm):
    n = zgm.shape[0]

    def body(z_ref, lg_ref, lb_ref, ws_ref, bs_ref, y_ref):
        zg = _gelu(z_ref[...])
        u = zg[:, :D]
        vh, _ = _ln_stats(zg[:, D:])
        vn = (vh * lg_ref[...] + lb_ref[...]).astype(BF16)
        keep = _tril()
        for g in range(GM_GROUPS):
            w = jnp.where(keep, ws_ref[g], 0.0).astype(BF16)
            cs = slice(g * LANE, (g + 1) * LANE)
            for c in range(tm // GM_CHUNK):
                rs = slice(c * GM_CHUNK, (c + 1) * GM_CHUNK)
                sv = _dot(w, vn[rs, cs]) + bs_ref[g]
                y_ref[rs, cs] = (u[rs, cs] * sv).astype(BF16)

    return _rowcall("f_gmlp", body, n, tm,
                    [(zgm, 'row'), (ln_g, 'full'), (ln_b, 'full'), (w_s, 'full'), (b_s, 'full')],
                    [('row', D, BF16)])[0]


def _s5_specs(nb, nc, rev):
    def cc(c):
        return nc - 1 - c if rev else c
    slab = pl.BlockSpec((S5_T, LANE), lambda j, b, c: (b * nc + cc(c), j))
    bb = pl.BlockSpec((None, LANE, 1024), lambda j, b, c: (j, 0, 0))
    cm = pl.BlockSpec((None, 1024, LANE), lambda j, b, c: (j, 0, 0))
    av = pl.BlockSpec((None, 1, 1024), lambda j, b, c: (j, 0, 0))
    dv = pl.BlockSpec((None, 1, LANE), lambda j, b, c: (j, 0, 0))
    st = pl.BlockSpec((None, None, None, 1, 1024), lambda j, b, c: (j, b, cc(c), 0, 0))
    return slab, bb, cm, av, dv, st


def _s5_states(u, bb_ref, a_ref, carry):
    bu = _dot(u.astype(BF16), bb_ref[...])
    ar, ai = a_ref[:, :512], a_ref[:, 512:]
    cr, ci = carry[:, :512], carry[:, 512:]
    first = lax.broadcasted_iota(jnp.int32, (S5_T, 512), 0) == 0
    br = bu[:, :512] + jnp.where(first, ar * cr - ai * ci, 0.0)
    bi = bu[:, 512:] + jnp.where(first, ar * ci + ai * cr, 0.0)
    return _cscan(br, bi, ar, ai, reverse=False)


def _f_s5(zs5, bbc, ccm, avec, dvec, nb):
    n = zs5.shape[0]
    nc = n // nb // S5_T
    slab, bb, cm, av, dv, st = _s5_specs(nb, nc, False)

    def body(u_ref, bb_ref, cc_ref, a_ref, d_ref, y_ref, st_ref, carry):
        @pl.when(pl.program_id(2) == 0)
        def _():
            carry[...] = jnp.zeros_like(carry)
        st_ref[...] = carry[...]
        u = u_ref[...]
        sr, si = _s5_states(u, bb_ref, a_ref, carry[...])
        last = lax.broadcasted_iota(jnp.int32, (S5_T, 512), 0) == S5_T - 1
        carry[:, :512] = jnp.sum(jnp.where(last, sr, 0.0), axis=0, keepdims=True)
        carry[:, 512:] = jnp.sum(jnp.where(last, si, 0.0), axis=0, keepdims=True)
        s = jnp.concatenate([sr, si], axis=1).astype(BF16)
        y_ref[...] = _dot(s, cc_ref[...]) + d_ref[...] * u

    return pl.pallas_call(
        body, name="f_s5", grid=(S5_SUPER, nb, nc), in_specs=[slab, bb, cm, av, dv], out_specs=[slab, st],
        out_shape=[jax.ShapeDtypeStruct((n, S5_W), F32), jax.ShapeDtypeStruct((S5_SUPER, nb, nc, 1, 1024), F32)],
        scratch_shapes=[pltpu.VMEM((1, 1024), F32)],
        compiler_params=_params(("arbitrary", "arbitrary", "arbitrary")),
    )(zs5, bbc, ccm, avec, dvec)


def _f_mix(ygm, ypre, zg, x, w_glu, w_br_gm, w_br_s5, w_mix_out, g_post, tm):
    n = x.shape[0]

    def body(ygm_ref, ypre_ref, zg_ref, x_ref, wglu_ref, wgm_ref, ws5_ref, wout_ref, g_ref,
             yg_ref, ys5_ref, a_ref, b_ref, mg_ref, mo_ref, x1_ref):
        yg = _gelu(ypre_ref[...])
        ygb = yg.astype(BF16)
        yg_ref[...] = ygb
        ys5 = (yg * _sig(_dot(ygb, wglu_ref[...]))).astype(BF16)
        ys5_ref[...] = ys5
        a = _dot(ygm_ref[...], wgm_ref[...])
        b = _dot(ys5, ws5_ref[...])
        a_ref[...] = a.astype(BF16)
        b_ref[...] = b.astype(BF16)
        zg = zg_ref[...]
        merged = (_sig(zg[:, :D]) * a + _sig(zg[:, D:]) * b).astype(BF16)
        mg_ref[...] = merged
        mo = _dot(merged, wout_ref[...])
        mo_ref[...] = mo
        x1_ref[...] = x_ref[...] + _rms_fwd(mo, g_ref[...])

    return _rowcall("f_mix", body, n, tm,
                    [(ygm, 'row'), (ypre, 'row'), (zg, 'row'), (x, 'row'), (w_glu, 'full'), (w_br_gm, 'full'),
                     (w_br_s5, 'full'), (w_mix_out, 'full'), (g_post, 'full')],
                    [('row', S5_W, BF16), ('row', S5_W, BF16), ('row', D, BF16), ('row', D, BF16),
                     ('row', D, BF16), ('row', D, F32), ('row', D, F32)])


def _f_mem(mem, g_mem, w_kv, tm):
    n = mem.shape[0]

    def body(m_ref, g_ref, w_ref, mn_ref, k_ref, v_ref):
        mn = _rms_fwd(m_ref[...], g_ref[...]).astype(BF16)
        mn_ref[...] = mn
        k_ref[...] = _dot(mn, w_ref[:, :D]).astype(BF16)
        v_ref[...] = _dot(mn, w_ref[:, D:]).astype(BF16)

    return _rowcall("f_mem", body, n, tm, [(mem, 'row'), (g_mem, 'full'), (w_kv, 'full')],
                    [('row', D, BF16), ('row', D, BF16), ('row', D, BF16)])


def _softmax(s):
    m = jnp.max(s, axis=-1, keepdims=True)
    e = jnp.exp(s - m)
    return e / jnp.sum(e, axis=-1, keepdims=True)


def _f_attn(x1, g_pre, w_q, k, v, w_o, g_post, tm, tpb, mlen):
    n = x1.shape[0]
    kv_spec = pl.BlockSpec((mlen, D), lambda i: (i // tpb, 0))
    scale = HEAD_DIM ** -0.5

    def body(x_ref, gp_ref, wq_ref, k_ref, v_ref, wo_ref, go_ref, hc_ref, o_ref, ao_ref, x2_ref):
        x1v = x_ref[...]
        hc = _rms_fwd(x1v, gp_ref[...]).astype(BF16)
        hc_ref[...] = hc
        q = _dot(hc, wq_ref[...])
        for h in range(HEADS):
            hs = slice(h * HEAD_DIM, (h + 1) * HEAD_DIM)
            p = _softmax(_dot_nt(q[:, hs].astype(BF16), k_ref[:, hs]) * scale)
            o_ref[:, hs] = _dot(p.astype(BF16), v_ref[:, hs]).astype(BF16)
        ao = _dot(o_ref[...], wo_ref[...])
        ao_ref[...] = ao
        x2_ref[...] = x1v + _rms_fwd(ao, go_ref[...])

    return _rowcall("f_attn", body, n, tm,
                    [(x1, 'row'), (g_pre, 'full'), (w_q, 'full'), (k, kv_spec), (v, kv_spec), (w_o, 'full'),
                     (g_post, 'full')],
                    [('row', D, BF16), ('row', D, BF16), ('row', D, F32), ('row', D, F32)])


def _f_ffn_up(x2, g_pre, w_gu, tm):
    n = x2.shape[0]

    def body(x_ref, g_ref, w_ref, hf_ref, gu_ref, act_ref):
        hf = _rms_fwd(x_ref[...], g_ref[...]).astype(BF16)
        hf_ref[...] = hf
        gg = _dot(hf, w_ref[:, :FFN_H])
        uu = _dot(hf, w_ref[:, FFN_H:])
        gu_ref[:, :FFN_H] = gg.astype(BF16)
        gu_ref[:, FFN_H:] = uu.astype(BF16)
        act_ref[...] = (gg * _sig(gg) * uu).astype(BF16)

    return _rowcall("f_ffn_up", body, n, tm, [(x2, 'row'), (g_pre, 'full'), (w_gu, 'full')],
                    [('row', D, BF16), ('row', 2 * FFN_H, BF16), ('row', FFN_H, BF16)])


def _f_ffn_down(act, x2, tgt, w_down, g_post, tm):
    n = x2.shape[0]

    def body(a_ref, x_ref, t_ref, w_ref, g_ref, dn_ref, d3_ref, loss_ref):
        @pl.when(pl.program_id(0) == 0)
        def _():
            loss_ref[...] = jnp.zeros_like(loss_ref)
        dn = _dot(a_ref[...], w_ref[...])
        dn_ref[...] = dn
        err = x_ref[...] + _rms_fwd(dn, g_ref[...]) - t_ref[...]
        d3_ref[...] = err * (1.0 / D)
        loss_ref[...] += jnp.sum(err * err, axis=0, keepdims=True)

    return _rowcall("f_ffn_down", body, n, tm,
                    [(act, 'row'), (x2, 'row'), (tgt, 'row'), (w_down, 'full'), (g_post, 'full')],
                    [('row', D, F32), ('row', D, F32), ('acc', (1, D), F32)])


def _b_ffn_down(d3, dn, gu, g_post, w_down, tm):
    n = d3.shape[0]

    def body(d3_ref, dn_ref, gu_ref, g_ref, w_ref, ddn_ref, dgu_ref, dg_ref):
        @pl.when(pl.program_id(0) == 0)
        def _():
            dg_ref[...] = jnp.zeros_like(dg_ref)
        ddn, dg = _rms_bwd(dn_ref[...], g_ref[...], d3_ref[...])
        dg_ref[...] += dg
        ddn = ddn.astype(BF16)
        ddn_ref[...] = ddn
        dact = _dot_nt(ddn, w_ref[...])
        gg = gu_ref[:, :FFN_H].astype(F32)
        uu = gu_ref[:, FFN_H:].astype(F32)
        sg = _sig(gg)
        dgu_ref[:, :FFN_H] = (dact * uu * sg * (1.0 + gg * (1.0 - sg))).astype(BF16)
        dgu_ref[:, FFN_H:] = (dact * gg * sg).astype(BF16)

    return _rowcall("b_ffn_down", body, n, tm,
                    [(d3, 'row'), (dn, 'row'), (gu, 'row'), (g_post, 'full'), (w_down, 'full')],
                    [('row', D, BF16), ('row', 2 * FFN_H, BF16), ('acc', (1, D), F32)])


def _b_ffn_up(dgu, d3, x2, w_gu, g_pre, tm):
    n = d3.shape[0]

    def body(dgu_ref, d3_ref, x_ref, w_ref, g_ref, dx_ref, dg_ref):
        @pl.when(pl.program_id(0) == 0)
        def _():
            dg_ref[...] = jnp.zeros_like(dg_ref)
        dhf = _dot_nt(dgu_ref[...], w_ref[...])
        dx, dg = _rms_bwd(x_ref[...], g_ref[...], dhf)
        dg_ref[...] += dg
        dx_ref[...] = d3_ref[...] + dx

    return _rowcall("b_ffn_up", body, n, tm,
                    [(dgu, 'row'), (d3, 'row'), (x2, 'row'), (w_gu, 'full'), (g_pre, 'full')],
                    [('row', D, F32), ('acc', (1, D), F32)])


def _b_attn(dx2, ao, x1, hc, g_post, w_o, w_q, g_pre, k, v, tm, tpb, mlen):
    n = dx2.shape[0]
    nb = n // (tm * tpb)
    kv_spec = pl.BlockSpec((mlen, D), lambda i: (i // tpb, 0))
    scale = HEAD_DIM ** -0.5

    def body(dx2_ref, ao_ref, x1_ref, hc_ref, go_ref, wo_ref, wq_ref, gp_ref, k_ref, v_ref,
             dao_ref, dq_ref, dx1_ref, dk_ref, dv_ref, dgo_ref, dgp_ref):
        i = pl.program_id(0)

        @pl.when(i == 0)
        def _():
            dgo_ref[...] = jnp.zeros_like(dgo_ref)
            dgp_ref[...] = jnp.zeros_like(dgp_ref)

        @pl.when(i % tpb == 0)
        def _():
            dk_ref[...] = jnp.zeros_like(dk_ref)
            dv_ref[...] = jnp.zeros_like(dv_ref)

        dx2v = dx2_ref[...]
        dao, dgo = _rms_bwd(ao_ref[...], go_ref[...], dx2v)
        dgo_ref[...] += dgo
        dao = dao.astype(BF16)
        dao_ref[...] = dao
        do = _dot_nt(dao, wo_ref[...])
        q = _dot(hc_ref[...], wq_ref[...])
        for h in range(HEADS):
            hs = slice(h * HEAD_DIM, (h + 1) * HEAD_DIM)
            qh = q[:, hs].astype(BF16)
            kh = k_ref[:, hs]
            p = _softmax(_dot_nt(qh, kh) * scale)
            doh = do[:, hs].astype(BF16)
            dp = _dot_nt(doh, v_ref[:, hs])
            ds = (p * (dp - jnp.sum(dp * p, axis=-1, keepdims=True)) * scale).astype(BF16)
            dq_ref[:, hs] = _dot(ds, kh).astype(BF16)
            dk_ref[:, hs] += _dot_tn(ds, qh)
            dv_ref[:, hs] += _dot_tn(p.astype(BF16), doh)
        dhc = _dot_nt(dq_ref[...], wq_ref[...])
        dx, dgp = _rms_bwd(x1_ref[...], gp_ref[...], dhc)
        dgp_ref[...] += dgp
        dx1_ref[...] = dx2v + dx

    return _rowcall("b_attn", body, n, tm,
                    [(dx2, 'row'), (ao, 'row'), (x1, 'row'), (hc, 'row'), (g_post, 'full'), (w_o, 'full'),
                     (w_q, 'full'), (g_pre, 'full'), (k, kv_spec), (v, kv_spec)],
                    [('row', D, BF16), ('row', D, BF16), ('row', D, F32),
                     (kv_spec, (nb * mlen, D), F32), (kv_spec, (nb * mlen, D), F32),
                     ('acc', (1, D), F32), ('acc', (1, D), F32)])


def _b_mem(dk, dv, mem, w_kv, g_mem, tm):
    n = mem.shape[0]

    def body(dk_ref, dv_ref, m_ref, w_ref, g_ref, dkv_ref, dg_ref):
        @pl.when(pl.program_id(0) == 0)
        def _():
            dg_ref[...] = jnp.zeros_like(dg_ref)
        dkb = dk_ref[...].astype(BF16)
        dvb = dv_ref[...].astype(BF16)
        dkv_ref[:, :D] = dkb
        dkv_ref[:, D:] = dvb
        dmn = _dot_nt(dkb, w_ref[:, :D]) + _dot_nt(dvb, w_ref[:, D:])
        _, dg = _rms_bwd(m_ref[...], g_ref[...], dmn)
        dg_ref[...] += dg

    return _rowcall("b_mem", body, n, tm, [(dk, 'row'), (dv, 'row'), (mem, 'row'), (w_kv, 'full'), (g_mem, 'full')],
                    [('row', 2 * D, BF16), ('acc', (1, D), F32)])


def _b_mix(dx1, mo, a, b, zg, ypre, g_post, w_mix_out, w_br_gm, w_br_s5, w_glu, tm):
    n = dx1.shape[0]

    def body(dx1_ref, mo_ref, a_ref, b_ref, zg_ref, ypre_ref, g_ref, wout_ref, wgm_ref, ws5_ref, wglu_ref,
             dmo_ref, da_ref, db_ref, dzg_ref, dygm_ref, dgate_ref, dypre_ref, dg_ref):
        @pl.when(pl.program_id(0) == 0)
        def _():
            dg_ref[...] = jnp.zeros_like(dg_ref)
        dmo, dg = _rms_bwd(mo_ref[...], g_ref[...], dx1_ref[...])
        dg_ref[...] += dg
        dmo = dmo.astype(BF16)
        dmo_ref[...] = dmo
        dmg = _dot_nt(dmo, wout_ref[...])
        zg = zg_ref[...]
        sa, sb = _sig(zg[:, :D]), _sig(zg[:, D:])
        da = (dmg * sa).astype(BF16)
        db = (dmg * sb).astype(BF16)
        da_ref[...] = da
        db_ref[...] = db
        dzg_ref[:, :D] = (dmg * a_ref[...].astype(F32) * sa * (1.0 - sa)).astype(BF16)
        dzg_ref[:, D:] = (dmg * b_ref[...].astype(F32) * sb * (1.0 - sb)).astype(BF16)
        dygm_ref[...] = _dot_nt(da, wgm_ref[...])
        dys5 = _dot_nt(db, ws5_ref[...])
        ypre = ypre_ref[...]
        yg = _gelu(ypre)
        sgt = _sig(_dot(yg.astype(BF16), wglu_ref[...]))
        dgate = (dys5 * yg * sgt * (1.0 - sgt)).astype(BF16)
        dgate_ref[...] = dgate
        dyg = dys5 * sgt + _dot_nt(dgate, wglu_ref[...])
        dypre_ref[...] = dyg * _gelu_grad(ypre)

    return _rowcall("b_mix", body, n, tm,
                    [(dx1, 'row'), (mo, 'row'), (a, 'row'), (b, 'row'), (zg, 'row'), (ypre, 'row'), (g_post, 'full'),
                     (w_mix_out, 'full'), (w_br_gm, 'full'), (w_br_s5, 'full'), (w_glu, 'full')],
                    [('row', D, BF16), ('row', D, BF16), ('row', D, BF16), ('row', 2 * D, BF16), ('row', D, F32),
                     ('row', S5_W, BF16), ('row', S5_W, F32), ('acc', (1, D), F32)])


def _b_s5(zs5, dypre, states, bbc, ccm, avec, dvec, nb):
    n = zs5.shape[0]
    nc = n // nb // S5_T
    slab, bb, cm, av, dv, st = _s5_specs(nb, nc, True)

    def body(u_ref, dy_ref, st_ref, bb_ref, cc_ref, a_ref, d_ref,
             du_ref, dbb_ref, dcc_ref, da_ref, dd_ref, lcarry):
        first = jnp.logical_and(pl.program_id(1) == 0, pl.program_id(2) == 0)

        @pl.when(first)
        def _():
            dbb_ref[...] = jnp.zeros_like(dbb_ref)
            dcc_ref[...] = jnp.zeros_like(dcc_ref)
            da_ref[...] = jnp.zeros_like(da_ref)
            dd_ref[...] = jnp.zeros_like(dd_ref)

        @pl.when(pl.program_id(2) == 0)
        def _():
            lcarry[...] = jnp.zeros_like(lcarry)

        u = u_ref[...]
        dy = dy_ref[...]
        ub, dyb = u.astype(BF16), dy.astype(BF16)
        sin = st_ref[...]
        sr, si = _s5_states(u, bb_ref, a_ref, sin)
        ar, ai = a_ref[:, :512], a_ref[:, 512:]
        ds = _dot_nt(dyb, cc_ref[...])
        lr_, li_ = lcarry[:, :512], lcarry[:, 512:]
        row = lax.broadcasted_iota(jnp.int32, (S5_T, 512), 0)
        last = row == S5_T - 1
        gr = ds[:, :512] + jnp.where(last, ar * lr_ + ai * li_, 0.0)
        gi = ds[:, 512:] + jnp.where(last, ar * li_ - ai * lr_, 0.0)
        lr, li = _cscan(gr, gi, ar, -ai, reverse=True)
        head = row == 0
        lcarry[:, :512] = jnp.sum(jnp.where(head, lr, 0.0), axis=0, keepdims=True)
        lcarry[:, 512:] = jnp.sum(jnp.where(head, li, 0.0), axis=0, keepdims=True)
        s = jnp.concatenate([sr, si], axis=1).astype(BF16)
        lam = jnp.concatenate([lr, li], axis=1).astype(BF16)
        dcc_ref[...] += _dot_tn(s, dyb)
        dbb_ref[...] += _dot_tn(ub, lam)
        du_ref[...] = (_dot_nt(lam, bb_ref[...]) + d_ref[...] * dy).astype(BF16)
        dd_ref[...] += jnp.sum(dy * u, axis=0, keepdims=True)
        pr = jnp.where(head, sin[:, :512], pltpu.roll(sr, 1, 0))
        pi = jnp.where(head, sin[:, 512:], pltpu.roll(si, 1, 0))
        da_ref[:, :512] += jnp.sum(lr * pr + li * pi, axis=0, keepdims=True)
        da_ref[:, 512:] += jnp.sum(li * pr - lr * pi, axis=0, keepdims=True)

    return pl.pallas_call(
        body, name="b_s5", grid=(S5_SUPER, nb, nc), in_specs=[slab, slab, st, bb, cm, av, dv],
        out_specs=[slab, bb, cm, av, dv],
        out_shape=[jax.ShapeDtypeStruct((n, S5_W), BF16), jax.ShapeDtypeStruct(bbc.shape, F32),
                   jax.ShapeDtypeStruct(ccm.shape, F32), jax.ShapeDtypeStruct(avec.shape, F32),
                   jax.ShapeDtypeStruct(dvec.shape, F32)],
        scratch_shapes=[pltpu.VMEM((1, 1024), F32)],
        compiler_params=_params(("arbitrary", "arbitrary", "arbitrary")),
    )(zs5, dypre, states, bbc, ccm, avec, dvec)


def _b_gmlp(zgm, dygm, ln_g, ln_b, w_s, b_s, tm):
    n = zgm.shape[0]

    def body(z_ref, dy_ref, lg_ref, lb_ref, ws_ref, bs_ref, dz_ref, dws_ref, dbs_ref, dlg_ref, dlb_ref,
             du_s, dvn_s):
        @pl.when(pl.program_id(0) == 0)
        def _():
            dws_ref[...] = jnp.zeros_like(dws_ref)
            dbs_ref[...] = jnp.zeros_like(dbs_ref)
            dlg_ref[...] = jnp.zeros_like(dlg_ref)
            dlb_ref[...] = jnp.zeros_like(dlb_ref)
        z = z_ref[...]
        zg = _gelu(z)
        u = zg[:, :D]
        vh, r = _ln_stats(zg[:, D:])
        vn = (vh * lg_ref[...] + lb_ref[...]).astype(BF16)
        dy = dy_ref[...]
        keep = _tril()
        for g in range(GM_GROUPS):
            w = jnp.where(keep, ws_ref[g], 0.0).astype(BF16)
            cs = slice(g * LANE, (g + 1) * LANE)
            for c in range(tm // GM_CHUNK):
                rs = slice(c * GM_CHUNK, (c + 1) * GM_CHUNK)
                vb = vn[rs, cs]
                sv = _dot(w, vb) + bs_ref[g]
                dyb = dy[rs, cs]
                du_s[rs, cs] = dyb * sv
                dsv = dyb * u[rs, cs]
                dsvb = dsv.astype(BF16)
                dvn_s[rs, cs] = _dot_tn(w, dsvb)
                dws_ref[g] += jnp.where(keep, _dot_nt(dsvb, vb), 0.0)
                dbs_ref[g] += jnp.sum(dsv, axis=1, keepdims=True)
        dvn = dvn_s[...]
        dlg_ref[...] += jnp.sum(dvn * vh, axis=0, keepdims=True)
        dlb_ref[...] += jnp.sum(dvn, axis=0, keepdims=True)
        dvh = dvn * lg_ref[...]
        dv = r * (dvh - jnp.mean(dvh, axis=-1, keepdims=True) - vh * jnp.mean(dvh * vh, axis=-1, keepdims=True))
        dz_ref[:, :D] = (du_s[...] * _gelu_grad(z[:, :D])).astype(BF16)
        dz_ref[:, D:] = (dv * _gelu_grad(z[:, D:])).astype(BF16)

    return _rowcall("b_gmlp", body, n, tm,
                    [(zgm, 'row'), (dygm, 'row'), (ln_g, 'full'), (ln_b, 'full'), (w_s, 'full'), (b_s, 'full')],
                    [('row', 2 * D, BF16), ('acc', w_s.shape, F32), ('acc', b_s.shape, F32), ('acc', (1, D), F32),
                     ('acc', (1, D), F32)],
                    scratch=[pltpu.VMEM((tm, D), F32), pltpu.VMEM((tm, D), F32)])


def _b_in(dzgm, dzs5, dzg, dx1, x, w_in, g_pre, tm):
    n = x.shape[0]

    def body(d1_ref, d2_ref, d3_ref, dx1_ref, x_ref, w_ref, g_ref, gx_ref, dg_ref):
        @pl.when(pl.program_id(0) == 0)
        def _():
            dg_ref[...] = jnp.zeros_like(dg_ref)
        dh = (_dot_nt(d1_ref[...], w_ref[:, 0:2 * D]) + _dot_nt(d2_ref[...], w_ref[:, 2 * D:2 * D + S5_W])
              + _dot_nt(d3_ref[...], w_ref[:, 2 * D + S5_W:]))
        dx, dg = _rms_bwd(x_ref[...], g_ref[...], dh)
        dg_ref[...] += dg
        gx_ref[...] = dx1_ref[...] + dx

    return _rowcall("b_in", body, n, tm,
                    [(dzgm, 'row'), (dzs5, 'row'), (dzg, 'row'), (dx1, 'row'), (x, 'row'), (w_in, 'full'),
                     (g_pre, 'full')],
                    [('row', D, F32), ('acc', (1, D), F32)])


def _whole(name, body, ins, outs):
    return pl.pallas_call(body, name=name, out_shape=[jax.ShapeDtypeStruct(s, dt) for s, dt in outs],
                          compiler_params=_params())(*ins)


def _s5_disc_fwd(lr, li, ls, br, bi):
    def body(lr_ref, li_ref, ls_ref, br_ref, bi_ref, o1, o2, o3, o4):
        outs = _s5_disc(lr_ref[...], li_ref[...], ls_ref[...], br_ref[...], bi_ref[...])
        for o, val in zip((o1, o2, o3, o4), outs):
            o[...] = val

    return _whole("s5_disc_fwd", body, [lr, li, ls, br, bi],
                  [(lr.shape, F32), (lr.shape, F32), (br.shape, F32), (br.shape, F32)])


def _s5_disc_bwd(lr, li, ls, br, bi, cts):
    def body(lr_ref, li_ref, ls_ref, br_ref, bi_ref, c1, c2, c3, c4, o1, o2, o3, o4, o5):
        _, vjp = jax.vjp(_s5_disc, lr_ref[...], li_ref[...], ls_ref[...], br_ref[...], bi_ref[...])
        grads = vjp((c1[...], c2[...], c3[...], c4[...]))
        for o, val in zip((o1, o2, o3, o4, o5), grads):
            o[...] = val

    return _whole("s5_disc_bwd", body, [lr, li, ls, br, bi, *cts],
                  [(lr.shape, F32), (lr.shape, F32), (lr.shape, F32), (br.shape, F32), (br.shape, F32)])


def _sum_rows(name, parts, tm):
    def body(*refs):
        acc = refs[0][...]
        for r in refs[1:-1]:
            acc = acc + r[...]
        refs[-1][...] = acc

    return _rowcall(name, body, parts[0].shape[0], tm, [(p, 'row') for p in parts],
                    [('row', parts[0].shape[1], F32)])[0]


def _adamw(name, w, g, m, v, tm):
    c1 = 1.0 - ADAM_B1 ** ADAM_STEP
    c2 = 1.0 - ADAM_B2 ** ADAM_STEP

    def body(w_ref, g_ref, m_ref, v_ref, d_ref, mo_ref, vo_ref):
        gg = g_ref[...]
        mn = ADAM_B1 * m_ref[...] + (1.0 - ADAM_B1) * gg
        vn = ADAM_B2 * v_ref[...] + (1.0 - ADAM_B2) * (gg * gg)
        mo_ref[...] = mn
        vo_ref[...] = vn
        d_ref[...] = -ADAM_LR * ((mn / c1) / (jnp.sqrt(vn / c2) + ADAM_EPS) + ADAM_WD * w_ref[...])

    cols = w.shape[1]
    return _rowcall(name, body, w.shape[0], tm, [(w, 'row'), (g, 'row'), (m, 'row'), (v, 'row')],
                    [('row', cols, F32), ('row', cols, F32), ('row', cols, F32)])


def _cast_all(name, arrs, dtype):
    k = len(arrs)

    def body(*refs):
        for src, dst in zip(refs[:k], refs[k:]):
            dst[...] = src[...].astype(dtype)

    return _whole(name, body, arrs, [(a.shape, dtype) for a in arrs])


def _reduce_big(slabs):
    rows = [s.shape[1] for s in slabs]
    blk = sum(rows) // 4
    tm = _tile_rows(blk, 512)
    per = blk // tm
    gpack = jnp.concatenate([s.reshape(4, 2, 2, r // 4, D) for s, r in zip(slabs, rows)], axis=3)

    def rdma(ss, rs, k, src, dst, peer):
        return pltpu.make_async_remote_copy(src_ref=src, dst_ref=dst, send_sem=ss.at[k], recv_sem=rs.at[k],
                                            device_id=peer, device_id_type=MESH)

    def pair_body(srcs, dsts, ss, rs, ls):
        c, _, (_, _, psib) = _place()
        keep = pltpu.make_async_copy(srcs[0].at[:, c], dsts[0], ls.at[0])
        keep.start()
        cp = rdma(ss, rs, 0, srcs[0].at[:, 1 - c], dsts[1], psib)
        cp.start()
        cp.wait()
        keep.wait()

    mine, from_sib = _comm_call("reduce_pair", [gpack], [((4, 2, blk, D), F32)] * 2, 1, 1, pair_body)

    def sum_pair(a_ref, b_ref, o32_ref, o16_ref):
        s = a_ref[...] + b_ref[...]
        o32_ref[...] = s
        o16_ref[...] = s.astype(BF16)

    p32, p16 = _rowcall("sum_pair", sum_pair, 8 * blk, tm,
                        [(mine.reshape(8 * blk, D), 'row'), (from_sib.reshape(8 * blk, D), 'row')],
                        [('row', D, F32), ('row', D, BF16)])
    p32, p16 = p32.reshape(4, 2, blk, D), p16.reshape(4, 2, blk, D)

    def step1_body(srcs, dsts, ss, rs, ls):
        _, (me, xn, yn, dg), (px, py, _) = _place()
        s16, s32 = srcs
        keeps = [pltpu.make_async_copy(s32.at[chip, q], dsts[1].at[k], ls.at[k])
                 for k, (chip, q) in enumerate(((me, 0), (xn, 0), (me, 1), (yn, 1)))]
        for kp in keeps:
            kp.start()
        cps = [rdma(ss, rs, 0, s16.at[yn, 0], dsts[0].at[0], py), rdma(ss, rs, 1, s16.at[dg, 0], dsts[0].at[1], py),
               rdma(ss, rs, 2, s16.at[xn, 1], dsts[0].at[2], px), rdma(ss, rs, 3, s16.at[dg, 1], dsts[0].at[3], px)]
        for cp in cps:
            cp.start()
        for cp in cps:
            cp.wait()
        for kp in keeps:
            kp.wait()

    recv1, own = _comm_call("reduce_step1", [p16, p32], [((4, blk, D), BF16), ((4, blk, D), F32)], 4, 4, step1_body)

    def sum_step1(a_ref, b_ref, o32_ref, o16_ref):
        s = a_ref[...] + b_ref[...].astype(F32)
        o32_ref[...] = s
        o16_ref[...] = s.astype(BF16)

    s32, s16 = _rowcall("sum_step1", sum_step1, 4 * blk, tm,
                        [(own.reshape(4 * blk, D), 'row'), (recv1.reshape(4 * blk, D), 'row')],
                        [('row', D, F32), ('row', D, BF16)])

    def step2_body(srcs, dsts, ss, rs, ls):
        _, _, (px, py, _) = _place()
        cps = [rdma(ss, rs, 0, srcs[0].at[1], dsts[0].at[0], px), rdma(ss, rs, 1, srcs[0].at[3], dsts[0].at[1], py)]
        for cp in cps:
            cp.start()
        for cp in cps:
            cp.wait()

    recv2 = _comm_call("reduce_step2", [s16.reshape(4, blk, D)], [((2, blk, D), BF16)], 2, 0, step2_body)[0]

    def sum_step2(a_ref, b_ref, o_ref):
        o_ref[...] = a_ref[...] + b_ref[...].astype(F32)

    red = _rowcall("sum_step2", sum_step2, 2 * blk, tm,
                   [(s32.reshape(4, blk, D), pl.BlockSpec((None, tm, D), lambda i: (2 * (i // per), i % per, 0))),
                    (recv2.reshape(2 * blk, D), 'row')],
                   [('row', D, F32)])[0].reshape(2, blk, D)

    offs = [sum(rows[:i]) // 4 for i in range(len(rows))]

    def share_body(srcs, dsts, ss, rs, ls):
        c, _, (_, _, psib) = _place()
        keeps, cps = [], []
        for i, (off, r) in enumerate(zip(offs, rows)):
            piece = srcs[0].at[:, pl.ds(off, r // 4), :]
            keeps.append(pltpu.make_async_copy(piece, dsts[i].at[c], ls.at[i]))
            cps.append(rdma(ss, rs, i, piece, dsts[i].at[c], psib))
            keeps[-1].start()
            cps[-1].start()
        for kp, cp in zip(keeps, cps):
            cp.wait()
            kp.wait()

    outs = _comm_call("share_pair", [red], [((2, 2, r // 4, D), F32) for r in rows], len(rows), len(rows), share_body)
    return [o.reshape(r, D) for o, r in zip(outs, rows)]


def _rows1024(a):
    flat = a.reshape(-1)
    pad = (-flat.shape[0]) % D
    if pad:
        flat = jnp.concatenate([flat, jnp.zeros((pad,), flat.dtype)])
    return flat.reshape(-1, D)


def _pack(arrs, pad_rows_to=8):
    parts = [_rows1024(a) for a in arrs]
    rows = sum(p.shape[0] for p in parts)
    pad = (-rows) % pad_rows_to
    if pad:
        parts.append(jnp.zeros((pad, D), parts[0].dtype))
    return jnp.concatenate(parts, axis=0)


def _unpack(packed, shapes):
    out, r = [], 0
    for s in shapes:
        size = math.prod(s)
        nr = -(-size // D)
        out.append(packed[r:r + nr].reshape(-1)[:size].reshape(s))
        r += nr
    return out


def _shard_slabs(full, axis):
    r, c = full.shape
    if axis == 0:
        return full.reshape(4, -1, D)
    return full.reshape(r, 4, c // 4).transpose(1, 0, 2).reshape(4, -1, D)


def _block_diag(t):
    eye = jnp.eye(8, dtype=t.dtype)
    j, g, a, b = t.shape
    return (t[:, :, :, None, :] * eye[None, :, None, :, None]).reshape(j, g * a, g * b)


def _block_diag_take(m, a, b):
    eye = jnp.eye(8, dtype=m.dtype)
    return (m.reshape(4, 8, a, 8, b) * eye[None, :, None, :, None]).sum(axis=3)


def kernel(x, mem, g_mix_pre, w_in, gm_ln_g, gm_ln_b, gm_w_s, gm_b_s, s5_lam_re, s5_lam_im, s5_log_step, s5_b_re, s5_b_im, s5_c_re, s5_c_im, s5_d, s5_w_glu, w_br_gm, w_br_s5, w_mix_out, g_mix_post, g_ca_pre, g_mem, ca_w_q, ca_w_kv, ca_w_o, g_ca_post, g_ffn_pre, ffn_w_gu, ffn_w_down, g_ffn_post, loss_target, m_g_mix_pre, m_w_in, m_gm_ln_g, m_gm_ln_b, m_gm_w_s, m_gm_b_s, m_s5_lam_re, m_s5_lam_im, m_s5_log_step, m_s5_b_re, m_s5_b_im, m_s5_c_re, m_s5_c_im, m_s5_d, m_s5_w_glu, m_w_br_gm, m_w_br_s5, m_w_mix_out, m_g_mix_post, m_g_ca_pre, m_g_mem, m_ca_w_q, m_ca_w_kv, m_ca_w_o, m_g_ca_post, m_g_ffn_pre, m_ffn_w_gu, m_ffn_w_down, m_g_ffn_post, v_g_mix_pre, v_w_in, v_gm_ln_g, v_gm_ln_b, v_gm_w_s, v_gm_b_s, v_s5_lam_re, v_s5_lam_im, v_s5_log_step, v_s5_b_re, v_s5_b_im, v_s5_c_re, v_s5_c_im, v_s5_d, v_s5_w_glu, v_w_br_gm, v_w_br_s5, v_w_mix_out, v_g_mix_post, v_g_ca_pre, v_g_mem, v_ca_w_q, v_ca_w_kv, v_ca_w_o, v_g_ca_post, v_g_ffn_pre, v_ffn_w_gu, v_ffn_w_down, v_g_ffn_post):
    a = dict(locals())
    w = {n: a[n][0] for n in WNAMES}
    nb, seq, _ = x.shape
    n = nb * seq
    mlen = mem.shape[1]
    tm = min(256, seq)
    tpb = seq // tm
    xf = x.reshape(n, D)
    tgt = loss_target.reshape(n, D)
    memf = mem.reshape(nb * mlen, D)
    tmm = min(256, mlen)

    big_names = list(BIG)
    local_shapes = {k: w[k].shape for k in big_names}
    axes = [BIG[k] for k in big_names]
    shards_bf = _cast_all("cast_w", [w[k] for k in big_names], BF16)
    gathered = _gather_weights(shards_bf, axes)
    wf = {k: (g.reshape(-1, g.shape[-1]) if ax == 0 else g) for k, g, ax in zip(big_names, gathered, axes)}

    vec = lambda name: w[name].reshape(1, D)

    to_lane = lambda p: p.reshape(1, -1)
    b_t = lambda p: p.transpose(2, 0, 1).reshape(16, -1)
    lr_l, li_l = to_lane(w['s5_lam_re']), to_lane(w['s5_lam_im'])
    ls_l = jnp.repeat(w['s5_log_step'], 64).reshape(1, -1)
    br_t, bi_t = b_t(w['s5_b_re']), b_t(w['s5_b_im'])
    ab_re, ab_im, bb_re, bb_im = _s5_disc_fwd(lr_l, li_l, ls_l, br_t, bi_t)
    blk = lambda t: _block_diag(t.reshape(16, 4, 8, 64).transpose(1, 2, 0, 3))
    bbc = jnp.concatenate([blk(bb_re), blk(bb_im)], axis=2).astype(BF16)
    cblk = lambda c: _block_diag(c.reshape(4, 8, 16, 64).transpose(0, 1, 3, 2))
    ccm = jnp.concatenate([cblk(w['s5_c_re']), -cblk(w['s5_c_im'])], axis=1).astype(BF16)
    avec = jnp.concatenate([ab_re.reshape(4, 1, 512), ab_im.reshape(4, 1, 512)], axis=2)
    dvec = w['s5_d'].reshape(4, 1, LANE)

    bs3 = w['gm_b_s'].reshape(GM_GROUPS, GM_CHUNK, 1)
    h, zgm, zs5, zg = _f_in(xf, vec('g_mix_pre'), wf['w_in'], tm)
    ygm = _f_gmlp(zgm, vec('gm_ln_g'), vec('gm_ln_b'), w['gm_w_s'], bs3, tm)
    ypre, states = _f_s5(zs5, bbc, ccm, avec, dvec, nb)
    yg, ys5, a_br, b_br, merged, mo, x1 = _f_mix(ygm, ypre, zg, xf, wf['s5_w_glu'], wf['w_br_gm'], wf['w_br_s5'],
                                                 wf['w_mix_out'], vec('g_mix_post'), tm)
    mem_n, kk, vv = _f_mem(memf, vec('g_mem'), wf['ca_w_kv'], tmm)
    hc, o_att, ao, x2 = _f_attn(x1, vec('g_ca_pre'), wf['ca_w_q'], kk, vv, wf['ca_w_o'], vec('g_ca_post'),
                                tm, tpb, mlen)
    hf, gu, act = _f_ffn_up(x2, vec('g_ffn_pre'), wf['ffn_w_gu'], tm)
    dn, d3, loss_cols = _f_ffn_down(act, x2, tgt, wf['ffn_w_down'], vec('g_ffn_post'), tm)

    gsm = {}
    gbig = {}
    ddn, dgu, gsm['g_ffn_post'] = _b_ffn_down(d3, dn, gu, vec('g_ffn_post'), wf['ffn_w_down'], tm)
    dx2, gsm['g_ffn_pre'] = _b_ffn_up(dgu, d3, x2, wf['ffn_w_gu'], vec('g_ffn_pre'), tm)
    gbig['ffn_w_down'] = _mm_tn("dw_ffn_down", act, ddn)
    gbig['ffn_w_gu'] = _mm_tn("dw_ffn_gu", hf, dgu, col_shards=4)
    dao, dq, dx1, dk, dv, gsm['g_ca_post'], gsm['g_ca_pre'] = _b_attn(
        dx2, ao, x1, hc, vec('g_ca_post'), wf['ca_w_o'], wf['ca_w_q'], vec('g_ca_pre'), kk, vv, tm, tpb, mlen)
    gbig['ca_w_o'] = _mm_tn("dw_ca_o", o_att, dao)
    gbig['ca_w_q'] = _mm_tn("dw_ca_q", hc, dq)
    dkv, gsm['g_mem'] = _b_mem(dk, dv, memf, wf['ca_w_kv'], vec('g_mem'), tmm)
    gbig['ca_w_kv'] = _mm_tn("dw_ca_kv", mem_n, dkv, col_shards=4)
    dmo, da_br, db_br, dzg, dygm, dgate, dypre, gsm['g_mix_post'] = _b_mix(
        dx1, mo, a_br, b_br, zg, ypre, vec('g_mix_post'), wf['w_mix_out'], wf['w_br_gm'], wf['w_br_s5'],
        wf['s5_w_glu'], tm)
    gbig['w_mix_out'] = _mm_tn("dw_mix_out", merged, dmo)
    gbig['w_br_gm'] = _mm_tn("dw_br_gm", ygm, da_br)
    gbig['w_br_s5'] = _mm_tn("dw_br_s5", ys5, db_br, col_shards=4)
    gbig['s5_w_glu'] = _mm_tn("dw_s5_glu", yg, dgate)
    dzs5, dbbc, dccm, davec, ddvec = _b_s5(zs5, dypre, states, bbc, ccm, avec, dvec, nb)
    dzgm, gsm['gm_w_s'], dbs3, gsm['gm_ln_g'], gsm['gm_ln_b'] = _b_gmlp(
        zgm, dygm, vec('gm_ln_g'), vec('gm_ln_b'), w['gm_w_s'], bs3, tm)
    gsm['gm_b_s'] = dbs3
    grad_x, gsm['g_mix_pre'] = _b_in(dzgm, dzs5, dzg, dx1, xf, wf['w_in'], vec('g_mix_pre'), tm)
    gbig['w_in'] = jnp.concatenate(
        [_mm_tn("dw_in_gm", h, dzgm), _mm_tn("dw_in_s5", h, dzs5), _mm_tn("dw_in_g", h, dzg)], axis=1)

    unblk = lambda m_: _block_diag_take(m_, 16, 64).transpose(2, 0, 1, 3).reshape(16, -1)
    d_bb_re, d_bb_im = unblk(dbbc[:, :, :512]), unblk(dbbc[:, :, 512:])
    cunblk = lambda m_: _block_diag_take(m_, 64, 16).transpose(0, 1, 3, 2).reshape(32, 16, 64)
    gsm['s5_c_re'] = cunblk(dccm[:, :512, :])
    gsm['s5_c_im'] = -cunblk(dccm[:, 512:, :])
    d_ab_re, d_ab_im = davec[:, :, :512].reshape(1, -1), davec[:, :, 512:].reshape(1, -1)
    g_lr, g_li, g_ls, g_br, g_bi = _s5_disc_bwd(lr_l, li_l, ls_l, br_t, bi_t, (d_ab_re, d_ab_im, d_bb_re, d_bb_im))
    gsm['s5_lam_re'], gsm['s5_lam_im'] = g_lr.reshape(32, 64), g_li.reshape(32, 64)
    gsm['s5_log_step'] = g_ls.reshape(32, 64).sum(axis=1)
    from_t = lambda t: t.reshape(16, 32, 64).transpose(1, 2, 0)
    gsm['s5_b_re'], gsm['s5_b_im'] = from_t(g_br), from_t(g_bi)
    gsm['s5_d'] = ddvec.reshape(32, 16)

    small_shapes = [w[k].shape for k in SMALL]
    spack = _pack([gsm[k].reshape(w[k].shape) for k in SMALL] + [loss_cols], 8)
    rs = spack.shape[0]
    me_slot = lambda px, py, pc: 4 * px + 2 * py + pc
    flips7 = [(fx, fy, fc) for fx in (0, 1) for fy in (0, 1) for fc in (0, 1) if fx or fy or fc]
    sall = _exchange("gather_small", spack, (8, rs, D), F32,
                     [(f, lambda px, py, pc: None, me_slot) for f in flips7],
                     local=(lambda px, py, pc: None, me_slot))
    ssum = _sum_rows("sum_small", [sall[i] for i in range(8)], rs)
    small_red = _unpack(ssum, small_shapes + [(1, D)])
    loss = 0.5 * jnp.sum(small_red[-1]) / D

    slabs = [gbig[k].reshape(4, -1, D) if k != 'w_in' else _shard_slabs(gbig[k], 1) for k in big_names]
    greds = _reduce_big(slabs)

    res_big = {}
    for k, g in zip(big_names, greds):
        rows_w = g.shape[0]
        flat = lambda t: t.reshape(rows_w, D)
        dl, mn, vn = _adamw("adamw_" + k, flat(w[k]), g, flat(a['m_' + k][0]), flat(a['v_' + k][0]),
                            _tile_rows(rows_w))
        res_big[k] = tuple(t.reshape(local_shapes[k]) for t in (g, dl, mn, vn))
    pad_loss = [jnp.zeros((1, D), F32)]
    wsm = _pack([w[k] for k in SMALL] + pad_loss, 8)
    msm = _pack([a['m_' + k][0] for k in SMALL] + pad_loss, 8)
    vsm = _pack([a['v_' + k][0] for k in SMALL] + pad_loss, 8)
    dl_s, mn_s, vn_s = _adamw("adamw_small", wsm, ssum, msm, vsm, rs)
    res_small = {k: vals for k, vals in zip(SMALL, zip(*[_unpack(p, small_shapes) for p in (ssum, dl_s, mn_s, vn_s)]))}

    res = {**res_big, **res_small}
    outs = [loss, grad_x.reshape(nb, seq, D)]
    for i in range(4):
        outs += [res[k][i][None] for k in WNAMES]
    return tuple(outs)


def _tile_rows(rows, cap=384):
    best = rows
    for t in range(16, min(rows, cap) + 1, 16):
        if rows % t == 0:
            best = t
    return best
```

```python
import functools
import math

import jax
import jax.numpy as jnp
from jax import lax
from jax.experimental import pallas as pl
from jax.experimental.pallas import tpu as pltpu

F32 = jnp.float32
BF16 = jnp.bfloat16
EPS = 1e-6
D = 1024
GM_CHUNK = 128
GM_GROUPS = 8
S5_W = 512
S5_SUPER = 4
S5_T = 128
HEADS = 4
HEAD_DIM = 256
FFN_H = 2816
LANE = 128
VMEM_LIMIT = 56 * 1024 * 1024
MESH = pl.DeviceIdType.MESH

ADAM_LR, ADAM_B1, ADAM_B2, ADAM_EPS, ADAM_WD, ADAM_STEP = 0.001, 0.9, 0.999, 1e-08, 0.01, 10

WNAMES = ['g_mix_pre', 'w_in', 'gm_ln_g', 'gm_ln_b', 'gm_w_s', 'gm_b_s', 's5_lam_re', 's5_lam_im', 's5_log_step',
          's5_b_re', 's5_b_im', 's5_c_re', 's5_c_im', 's5_d', 's5_w_glu', 'w_br_gm', 'w_br_s5', 'w_mix_out',
          'g_mix_post', 'g_ca_pre', 'g_mem', 'ca_w_q', 'ca_w_kv', 'ca_w_o', 'g_ca_post', 'g_ffn_pre', 'ffn_w_gu',
          'ffn_w_down', 'g_ffn_post']
BIG = {'w_in': 1, 's5_w_glu': 0, 'w_br_gm': 0, 'w_br_s5': 1, 'w_mix_out': 0, 'ca_w_q': 0, 'ca_w_kv': 1,
       'ca_w_o': 0, 'ffn_w_gu': 1, 'ffn_w_down': 0}
SMALL = [n for n in WNAMES if n not in BIG]


def _dot(a, b):
    return jnp.dot(a, b, preferred_element_type=F32)


def _dot_nt(a, b):
    return lax.dot_general(a, b, (((1,), (1,)), ((), ())), preferred_element_type=F32)


def _dot_tn(a, b):
    return lax.dot_general(a, b, (((0,), (0,)), ((), ())), preferred_element_type=F32)


def _rms_fwd(x, g):
    r = lax.rsqrt(jnp.mean(x * x, axis=-1, keepdims=True) + EPS)
    return x * r * g


def _rms_bwd(x, g, dy):
    r = lax.rsqrt(jnp.mean(x * x, axis=-1, keepdims=True) + EPS)
    xh = x * r
    gdy = dy * g
    dx = r * (gdy - xh * jnp.mean(gdy * xh, axis=-1, keepdims=True))
    dg = jnp.sum(dy * xh, axis=0, keepdims=True)
    return dx, dg


_GC = math.sqrt(2.0 / math.pi)


def _gelu(x):
    return 0.5 * x * (1.0 + jnp.tanh(_GC * (x + 0.044715 * x * x * x)))


def _gelu_grad(x):
    t = jnp.tanh(_GC * (x + 0.044715 * x * x * x))
    return 0.5 * (1.0 + t) + 0.5 * x * (1.0 - t * t) * _GC * (1.0 + 3 * 0.044715 * x * x)


def _sig(x):
    return 1.0 / (1.0 + jnp.exp(-x))


def _cscan(br, bi, ar, ai, reverse):
    t = br.shape[0]
    row = lax.broadcasted_iota(jnp.int32, br.shape, 0)
    pr, pi = ar, ai
    sh = 1
    while sh < t:
        if reverse:
            keep = row < t - sh
            rr, ri = pltpu.roll(br, t - sh, 0), pltpu.roll(bi, t - sh, 0)
        else:
            keep = row >= sh
            rr, ri = pltpu.roll(br, sh, 0), pltpu.roll(bi, sh, 0)
        rr = jnp.where(keep, rr, 0.0)
        ri = jnp.where(keep, ri, 0.0)
        br, bi = br + pr * rr - pi * ri, bi + pr * ri + pi * rr
        pr, pi = pr * pr - pi * pi, 2.0 * pr * pi
        sh *= 2
    return br, bi


def _s5_disc(lr, li, ls, br, bi):
    step = jnp.exp(ls)
    mag = jnp.exp(lr * step)
    ab_re = mag * jnp.cos(li * step)
    ab_im = mag * jnp.sin(li * step)
    den = lr * lr + li * li
    nr = ab_re - 1.0
    co_re = (nr * lr + ab_im * li) / den
    co_im = (ab_im * lr - nr * li) / den
    return ab_re, ab_im, co_re * br - co_im * bi, co_re * bi + co_im * br


def _params(sem=None):
    return pltpu.CompilerParams(dimension_semantics=sem, vmem_limit_bytes=VMEM_LIMIT)


def _rowcall(name, body, n_rows, tm, ins, outs, scratch=()):
    def spec(kind, shape):
        if kind == 'row':
            return pl.BlockSpec((tm, shape[1]), lambda i: (i, 0))
        if kind in ('full', 'acc'):
            nd = len(shape)
            return pl.BlockSpec(tuple(shape), lambda i: (0,) * nd)
        return kind

    in_specs = [spec(k, a.shape) for a, k in ins]
    out_shape, out_specs = [], []
    for k, s, dt in outs:
        shape = (n_rows, s) if k == 'row' else tuple(s)
        out_shape.append(jax.ShapeDtypeStruct(shape, dt))
        out_specs.append(spec(k, shape))
    return pl.pallas_call(
        body, name=name, grid=(n_rows // tm,), in_specs=in_specs, out_specs=out_specs, out_shape=out_shape,
        scratch_shapes=list(scratch), compiler_params=_params(("arbitrary",)),
    )(*[a for a, _ in ins])


def _tile(n, cap):
    if n <= cap:
        return n
    best = LANE
    for k in range(1, n // LANE + 1):
        t = k * LANE
        if n % t == 0 and t <= cap:
            best = t
    return best


def _mm_tn(name, a, b, col_shards=1):
    n, k = a.shape
    m = b.shape[1]
    mloc = m // col_shards
    tk, tn, tr = _tile(k, 1536), _tile(mloc, 1536), min(n, 512)
    per = mloc // tn

    def body(a_ref, b_ref, o_ref):
        @pl.when(pl.program_id(2) == 0)
        def _():
            o_ref[...] = jnp.zeros_like(o_ref)
        o_ref[...] += _dot_tn(a_ref[...], b_ref[...])

    if col_shards == 1:
        out_spec = pl.BlockSpec((tk, tn), lambda i, j, r: (i, j))
        out_shape = (k, m)
    else:
        out_spec = pl.BlockSpec((None, tk, tn), lambda i, j, r: (j // per, i, j % per))
        out_shape = (col_shards, k, mloc)
    return pl.pallas_call(
        body, name=name, grid=(k // tk, m // tn, n // tr),
        in_specs=[pl.BlockSpec((tr, tk), lambda i, j, r: (r, i)), pl.BlockSpec((tr, tn), lambda i, j, r: (r, j))],
        out_specs=out_spec, out_shape=jax.ShapeDtypeStruct(out_shape, F32),
        compiler_params=_params(("parallel", "parallel", "arbitrary")),
    )(a, b)


def _comm_call(name, srcs, outs, nsem, body_fn, aliases=None):
    ns, no = len(srcs), len(outs)

    def body(*refs):
        body_fn(refs[:ns], refs[ns:ns + no], refs[ns + no], refs[ns + no + 1])

    hbm = pl.BlockSpec(memory_space=pltpu.HBM)
    return pl.pallas_call(
        body, name=name, in_specs=[hbm] * ns, out_specs=[hbm] * no,
        out_shape=[jax.ShapeDtypeStruct(s, d) for s, d in outs],
        scratch_shapes=[pltpu.SemaphoreType.DMA((nsem,)), pltpu.SemaphoreType.DMA((nsem,))],
        input_output_aliases=aliases or {},
    )(*srcs)


def _place():
    x, y, c = lax.axis_index("x"), lax.axis_index("y"), lax.axis_index("c")
    chips = (2 * x + y, 2 * (1 - x) + y, 2 * x + (1 - y), 2 * (1 - x) + (1 - y))
    peers = ((1 - x, y, c), (x, 1 - y, c), (x, y, 1 - c))
    return c, chips, peers


def _gather_weights(shards, axes):
    nw = len(shards)
    shapes = [s.shape for s in shards]
    outs = [((4,) + s if ax == 0 else (s[0], 4 * s[1]), BF16) for s, ax in zip(shapes, axes)]

    def win(ref, i, chip, start, rows):
        r, cols = shapes[i]
        if axes[i] == 0:
            return ref.at[chip, pl.ds(start, rows), :]
        return ref.at[pl.ds(start, rows), pl.ds(chip * cols, cols)]

    def place_body(*refs):
        srcs, dsts, bufs, sem = refs[:nw], refs[nw:2 * nw], refs[2 * nw:3 * nw], refs[3 * nw]
        me = 2 * lax.axis_index("x") + lax.axis_index("y")
        cps = []
        for i in range(nw):
            bufs[i][...] = srcs[i][...].astype(BF16)
            cps.append(pltpu.make_async_copy(bufs[i], win(dsts[i], i, me, 0, shapes[i][0]), sem.at[i]))
            cps[-1].start()
        for cp in cps:
            cp.wait()

    placed = pl.pallas_call(
        place_body, name="cast_w", out_shape=[jax.ShapeDtypeStruct(s, d) for s, d in outs],
        in_specs=[pl.BlockSpec(memory_space=pltpu.VMEM)] * nw, out_specs=[pl.BlockSpec(memory_space=pltpu.HBM)] * nw,
        scratch_shapes=[pltpu.VMEM(s, BF16) for s in shapes] + [pltpu.SemaphoreType.DMA((nw,))],
        compiler_params=_params(),
    )(*shards)

    def body_fn(srcs, dsts, ss, rs):
        c, (me, xn, yn, dg), (px, py, psib) = _place()

        def rdma(k, src, dst, peer):
            return pltpu.make_async_remote_copy(src_ref=src, dst_ref=dst, send_sem=ss.at[k], recv_sem=rs.at[k],
                                                device_id=peer, device_id_type=MESH)

        first, later = [], []
        for i in range(nw):
            qr, k0 = shapes[i][0] // 4, 6 * i
            q0, q1 = 2 * c * qr, (2 * c + 1) * qr
            w0, w1 = win(dsts[i], i, me, q0, qr), win(dsts[i], i, me, q1, qr)
            cps = [rdma(k0, w0, w0, px), rdma(k0 + 1, w1, w1, py), rdma(k0 + 2, w0, w0, py), rdma(k0 + 3, w1, w1, px)]
            for cp in cps:
                cp.start()
            first.append(cps)
        for i in range(nw):
            qr, k0 = shapes[i][0] // 4, 6 * i
            q0, q1 = 2 * c * qr, (2 * c + 1) * qr
            first[i][0].wait_recv()
            f0 = rdma(k0 + 4, win(dsts[i], i, xn, q0, qr), win(dsts[i], i, xn, q0, qr), py)
            f0.start()
            first[i][1].wait_recv()
            f1 = rdma(k0 + 5, win(dsts[i], i, yn, q1, qr), win(dsts[i], i, yn, q1, qr), px)
            f1.start()
            later.append([f0, f1])
        swaps = []
        for i in range(nw):
            hr = shapes[i][0] // 2
            first[i][2].wait_recv()
            first[i][3].wait_recv()
            later[i][0].wait_recv()
            later[i][1].wait_recv()
            for j, chip in enumerate((xn, yn, dg)):
                hw = win(dsts[i], i, chip, c * hr, hr)
                sw = rdma(6 * nw + 3 * i + j, hw, hw, psib)
                sw.start()
                swaps.append(sw)
        for sw in swaps:
            sw.wait()
        for i in range(nw):
            for cp in first[i] + later[i]:
                cp.wait_send()

    return _comm_call("gather_w", placed, outs, 9 * nw, body_fn, aliases={i: i for i in range(nw)})


def _gather_all(name, src):
    flips7 = [(fx, fy, fc) for fx in (0, 1) for fy in (0, 1) for fc in (0, 1) if fx or fy or fc]

    def body_fn(srcs, dsts, ss, rs):
        me = (lax.axis_index("x"), lax.axis_index("y"), lax.axis_index("c"))
        slot = 4 * me[0] + 2 * me[1] + me[2]
        cps = []
        for k, flips in enumerate(flips7):
            peer = tuple(1 - p if f else p for p, f in zip(me, flips))
            cps.append(pltpu.make_async_remote_copy(src_ref=srcs[0], dst_ref=dsts[0].at[slot], send_sem=ss.at[k],
                                                    recv_sem=rs.at[k], device_id=peer, device_id_type=MESH))
            cps[-1].start()
        for cp in cps:
            cp.wait()

    return _comm_call(name, [src], [((8,) + src.shape, src.dtype)], 7, body_fn)[0]


def _f_in(x, g, w_in, tm):
    n = x.shape[0]

    def body(x_ref, g_ref, w_ref, h_ref, zgm_ref, zs5_ref, zg_ref):
        h = _rms_fwd(x_ref[...], g_ref[...]).astype(BF16)
        h_ref[...] = h
        zgm_ref[...] = _dot(h, w_ref[:, 0:2 * D])
        zs5_ref[...] = _dot(h, w_ref[:, 2 * D:2 * D + S5_W])
        zg_ref[...] = _dot(h, w_ref[:, 2 * D + S5_W:])

    return _rowcall("f_in", body, n, tm, [(x, 'row'), (g, 'full'), (w_in, 'full')],
                    [('row', D, BF16), ('row', 2 * D, F32), ('row', S5_W, F32), ('row', 2 * D, F32)])


def _tril():
    r = lax.broadcasted_iota(jnp.int32, (GM_CHUNK, GM_CHUNK), 0)
    c = lax.broadcasted_iota(jnp.int32, (GM_CHUNK, GM_CHUNK), 1)
    return r >= c


def _ln_stats(v):
    mu = jnp.mean(v, axis=-1, keepdims=True)
    vc = v - mu
    r = lax.rsqrt(jnp.mean(vc * vc, axis=-1, keepdims=True) + EPS)
    return vc * r, r


def _f_gmlp(zgm, ln_g, ln_b, w_s, b_s, tm):
    n = zgm.shape[0]

    def body(z_ref, lg_ref, lb_ref, ws_ref, bs_ref, y_ref):
        zg = _gelu(z_ref[...])
        u = zg[:, :D]
        vh, _ = _ln_stats(zg[:, D:])
        vn = (vh * lg_ref[...] + lb_ref[...]).astype(BF16)
        keep = _tril()
        for g in range(GM_GROUPS):
            w = jnp.where(keep, ws_ref[g], 0.0).astype(BF16)
            cs = slice(g * LANE, (g + 1) * LANE)
            for c in range(tm // GM_CHUNK):
                rs = slice(c * GM_CHUNK, (c + 1) * GM_CHUNK)
                sv = _dot(w, vn[rs, cs]) + bs_ref[g]
                y_ref[rs, cs] = (u[rs, cs] * sv).astype(BF16)

    return _rowcall("f_gmlp", body, n, tm,
                    [(zgm, 'row'), (ln_g, 'full'), (ln_b, 'full'), (w_s, 'full'), (b_s, 'full')],
                    [('row', D, BF16)])[0]


def _s5_specs(nb, nc, rev):
    def cc(c):
        return nc - 1 - c if rev else c
    slab = pl.BlockSpec((S5_T, LANE), lambda j, b, c: (b * nc + cc(c), j))
    bb = pl.BlockSpec((None, LANE, 1024), lambda j, b, c: (j, 0, 0))
    cm = pl.BlockSpec((None, 1024, LANE), lambda j, b, c: (j, 0, 0))
    av = pl.BlockSpec((None, 1, 1024), lambda j, b, c: (j, 0, 0))
    dv = pl.BlockSpec((None, 1, LANE), lambda j, b, c: (j, 0, 0))
    st = pl.BlockSpec((None, None, None, 1, 1024), lambda j, b, c: (j, b, cc(c), 0, 0))
    return slab, bb, cm, av, dv, st


def _s5_states(u, bb_ref, a_ref, carry):
    bu = _dot(u.astype(BF16), bb_ref[...])
    ar, ai = a_ref[:, :512], a_ref[:, 512:]
    cr, ci = carry[:, :512], carry[:, 512:]
    first = lax.broadcasted_iota(jnp.int32, (S5_T, 512), 0) == 0
    br = bu[:, :512] + jnp.where(first, ar * cr - ai * ci, 0.0)
    bi = bu[:, 512:] + jnp.where(first, ar * ci + ai * cr, 0.0)
    return _cscan(br, bi, ar, ai, reverse=False)


def _f_s5(zs5, bbc, ccm, avec, dvec, nb):
    n = zs5.shape[0]
    nc = n // nb // S5_T
    slab, bb, cm, av, dv, st = _s5_specs(nb, nc, False)

    def body(u_ref, bb_ref, cc_ref, a_ref, d_ref, y_ref, st_ref, carry):
        @pl.when(pl.program_id(2) == 0)
        def _():
            carry[...] = jnp.zeros_like(carry)
        st_ref[...] = carry[...]
        u = u_ref[...]
        sr, si = _s5_states(u, bb_ref, a_ref, carry[...])
        last = lax.broadcasted_iota(jnp.int32, (S5_T, 512), 0) == S5_T - 1
        carry[:, :512] = jnp.sum(jnp.where(last, sr, 0.0), axis=0, keepdims=True)
        carry[:, 512:] = jnp.sum(jnp.where(last, si, 0.0), axis=0, keepdims=True)
        s = jnp.concatenate([sr, si], axis=1).astype(BF16)
        y_ref[...] = _dot(s, cc_ref[...]) + d_ref[...] * u

    return pl.pallas_call(
        body, name="f_s5", grid=(S5_SUPER, nb, nc), in_specs=[slab, bb, cm, av, dv], out_specs=[slab, st],
        out_shape=[jax.ShapeDtypeStruct((n, S5_W), F32), jax.ShapeDtypeStruct((S5_SUPER, nb, nc, 1, 1024), F32)],
        scratch_shapes=[pltpu.VMEM((1, 1024), F32)],
        compiler_params=_params(("arbitrary", "arbitrary", "arbitrary")),
    )(zs5, bbc, ccm, avec, dvec)


def _f_mix(ygm, ypre, zg, x, w_glu, w_br_gm, w_br_s5, w_mix_out, g_post, tm):
    n = x.shape[0]

    def body(ygm_ref, ypre_ref, zg_ref, x_ref, wglu_ref, wgm_ref, ws5_ref, wout_ref, g_ref,
             yg_ref, ys5_ref, a_ref, b_ref, mg_ref, mo_ref, x1_ref):
        yg = _gelu(ypre_ref[...])
        ygb = yg.astype(BF16)
        yg_ref[...] = ygb
        ys5 = (yg * _sig(_dot(ygb, wglu_ref[...]))).astype(BF16)
        ys5_ref[...] = ys5
        a = _dot(ygm_ref[...], wgm_ref[...])
        b = _dot(ys5, ws5_ref[...])
        a_ref[...] = a.astype(BF16)
        b_ref[...] = b.astype(BF16)
        zg = zg_ref[...]
        merged = (_sig(zg[:, :D]) * a + _sig(zg[:, D:]) * b).astype(BF16)
        mg_ref[...] = merged
        mo = _dot(merged, wout_ref[...])
        mo_ref[...] = mo
        x1_ref[...] = x_ref[...] + _rms_fwd(mo, g_ref[...])

    return _rowcall("f_mix", body, n, tm,
                    [(ygm, 'row'), (ypre, 'row'), (zg, 'row'), (x, 'row'), (w_glu, 'full'), (w_br_gm, 'full'),
                     (w_br_s5, 'full'), (w_mix_out, 'full'), (g_post, 'full')],
                    [('row', S5_W, BF16), ('row', S5_W, BF16), ('row', D, BF16), ('row', D, BF16),
                     ('row', D, BF16), ('row', D, F32), ('row', D, F32)])


def _f_mem(mem, g_mem, w_kv, tm):
    n = mem.shape[0]

    def body(m_ref, g_ref, w_ref, mn_ref, k_ref, v_ref):
        mn = _rms_fwd(m_ref[...], g_ref[...]).astype(BF16)
        mn_ref[...] = mn
        k_ref[...] = _dot(mn, w_ref[:, :D]).astype(BF16)
        v_ref[...] = _dot(mn, w_ref[:, D:]).astype(BF16)

    return _rowcall("f_mem", body, n, tm, [(mem, 'row'), (g_mem, 'full'), (w_kv, 'full')],
                    [('row', D, BF16), ('row', D, BF16), ('row', D, BF16)])


def _softmax(s):
    m = jnp.max(s, axis=-1, keepdims=True)
    e = jnp.exp(s - m)
    return e / jnp.sum(e, axis=-1, keepdims=True)


def _f_attn(x1, g_pre, w_q, k, v, w_o, g_post, tm, tpb, mlen):
    n = x1.shape[0]
    kv_spec = pl.BlockSpec((mlen, D), lambda i: (i // tpb, 0))
    scale = HEAD_DIM ** -0.5

    def body(x_ref, gp_ref, wq_ref, k_ref, v_ref, wo_ref, go_ref, hc_ref, o_ref, ao_ref, x2_ref):
        x1v = x_ref[...]
        hc = _rms_fwd(x1v, gp_ref[...]).astype(BF16)
        hc_ref[...] = hc
        q = _dot(hc, wq_ref[...])
        for h in range(HEADS):
            hs = slice(h * HEAD_DIM, (h + 1) * HEAD_DIM)
            p = _softmax(_dot_nt(q[:, hs].astype(BF16), k_ref[:, hs]) * scale)
            o_ref[:, hs] = _dot(p.astype(BF16), v_ref[:, hs]).astype(BF16)
        ao = _dot(o_ref[...], wo_ref[...])
        ao_ref[...] = ao
        x2_ref[...] = x1v + _rms_fwd(ao, go_ref[...])

    return _rowcall("f_attn", body, n, tm,
                    [(x1, 'row'), (g_pre, 'full'), (w_q, 'full'), (k, kv_spec), (v, kv_spec), (w_o, 'full'),
                     (g_post, 'full')],
                    [('row', D, BF16), ('row', D, BF16), ('row', D, F32), ('row', D, F32)])


def _f_ffn_up(x2, g_pre, w_gu, tm):
    n = x2.shape[0]

    def body(x_ref, g_ref, w_ref, hf_ref, gu_ref, act_ref):
        hf = _rms_fwd(x_ref[...], g_ref[...]).astype(BF16)
        hf_ref[...] = hf
        gg = _dot(hf, w_ref[:, :FFN_H])
        uu = _dot(hf, w_ref[:, FFN_H:])
        gu_ref[:, :FFN_H] = gg.astype(BF16)
        gu_ref[:, FFN_H:] = uu.astype(BF16)
        act_ref[...] = (gg * _sig(gg) * uu).astype(BF16)

    return _rowcall("f_ffn_up", body, n, tm, [(x2, 'row'), (g_pre, 'full'), (w_gu, 'full')],
                    [('row', D, BF16), ('row', 2 * FFN_H, BF16), ('row', FFN_H, BF16)])


def _f_ffn_down(act, x2, tgt, w_down, g_post, tm):
    n = x2.shape[0]

    def body(a_ref, x_ref, t_ref, w_ref, g_ref, dn_ref, d3_ref, loss_ref):
        @pl.when(pl.program_id(0) == 0)
        def _():
            loss_ref[...] = jnp.zeros_like(loss_ref)
        dn = _dot(a_ref[...], w_ref[...])
        dn_ref[...] = dn
        err = x_ref[...] + _rms_fwd(dn, g_ref[...]) - t_ref[...]
        d3_ref[...] = err * (1.0 / D)
        loss_ref[...] += jnp.sum(err * err, axis=0, keepdims=True)

    return _rowcall("f_ffn_down", body, n, tm,
                    [(act, 'row'), (x2, 'row'), (tgt, 'row'), (w_down, 'full'), (g_post, 'full')],
                    [('row', D, F32), ('row', D, F32), ('acc', (1, D), F32)])


def _b_ffn_down(d3, dn, gu, g_post, w_down, tm):
    n = d3.shape[0]

    def body(d3_ref, dn_ref, gu_ref, g_ref, w_ref, ddn_ref, dgu_ref, dg_ref):
        @pl.when(pl.program_id(0) == 0)
        def _():
            dg_ref[...] = jnp.zeros_like(dg_ref)
        ddn, dg = _rms_bwd(dn_ref[...], g_ref[...], d3_ref[...])
        dg_ref[...] += dg
        ddn = ddn.astype(BF16)
        ddn_ref[...] = ddn
        dact = _dot_nt(ddn, w_ref[...])
        gg = gu_ref[:, :FFN_H].astype(F32)
        uu = gu_ref[:, FFN_H:].astype(F32)
        sg = _sig(gg)
        dgu_ref[:, :FFN_H] = (dact * uu * sg * (1.0 + gg * (1.0 - sg))).astype(BF16)
        dgu_ref[:, FFN_H:] = (dact * gg * sg).astype(BF16)

    return _rowcall("b_ffn_down", body, n, tm,
                    [(d3, 'row'), (dn, 'row'), (gu, 'row'), (g_post, 'full'), (w_down, 'full')],
                    [('row', D, BF16), ('row', 2 * FFN_H, BF16), ('acc', (1, D), F32)])


def _b_ffn_up(dgu, d3, x2, w_gu, g_pre, tm):
    n = d3.shape[0]

    def body(dgu_ref, d3_ref, x_ref, w_ref, g_ref, dx_ref, dg_ref):
        @pl.when(pl.program_id(0) == 0)
        def _():
            dg_ref[...] = jnp.zeros_like(dg_ref)
        dhf = _dot_nt(dgu_ref[...], w_ref[...])
        dx, dg = _rms_bwd(x_ref[...], g_ref[...], dhf)
        dg_ref[...] += dg
        dx_ref[...] = d3_ref[...] + dx

    return _rowcall("b_ffn_up", body, n, tm,
                    [(dgu, 'row'), (d3, 'row'), (x2, 'row'), (w_gu, 'full'), (g_pre, 'full')],
                    [('row', D, F32), ('acc', (1, D), F32)])


def _b_attn(dx2, ao, x1, hc, g_post, w_o, w_q, g_pre, k, v, tm, tpb, mlen):
    n = dx2.shape[0]
    nb = n // (tm * tpb)
    kv_spec = pl.BlockSpec((mlen, D), lambda i: (i // tpb, 0))
    scale = HEAD_DIM ** -0.5

    def body(dx2_ref, ao_ref, x1_ref, hc_ref, go_ref, wo_ref, wq_ref, gp_ref, k_ref, v_ref,
             dao_ref, dq_ref, dx1_ref, dk_ref, dv_ref, dgo_ref, dgp_ref):
        i = pl.program_id(0)

        @pl.when(i == 0)
        def _():
            dgo_ref[...] = jnp.zeros_like(dgo_ref)
            dgp_ref[...] = jnp.zeros_like(dgp_ref)

        @pl.when(i % tpb == 0)
        def _():
            dk_ref[...] = jnp.zeros_like(dk_ref)
            dv_ref[...] = jnp.zeros_like(dv_ref)

        dx2v = dx2_ref[...]
        dao, dgo = _rms_bwd(ao_ref[...], go_ref[...], dx2v)
        dgo_ref[...] += dgo
        dao = dao.astype(BF16)
        dao_ref[...] = dao
        do = _dot_nt(dao, wo_ref[...])
        q = _dot(hc_ref[...], wq_ref[...])
        for h in range(HEADS):
            hs = slice(h * HEAD_DIM, (h + 1) * HEAD_DIM)
            qh = q[:, hs].astype(BF16)
            kh = k_ref[:, hs]
            p = _softmax(_dot_nt(qh, kh) * scale)
            doh = do[:, hs].astype(BF16)
            dp = _dot_nt(doh, v_ref[:, hs])
            ds = (p * (dp - jnp.sum(dp * p, axis=-1, keepdims=True)) * scale).astype(BF16)
            dq_ref[:, hs] = _dot(ds, kh).astype(BF16)
            dk_ref[:, hs] += _dot_tn(ds, qh)
            dv_ref[:, hs] += _dot_tn(p.astype(BF16), doh)
        dhc = _dot_nt(dq_ref[...], wq_ref[...])
        dx, dgp = _rms_bwd(x1_ref[...], gp_ref[...], dhc)
        dgp_ref[...] += dgp
        dx1_ref[...] = dx2v + dx

    return _rowcall("b_attn", body, n, tm,
                    [(dx2, 'row'), (ao, 'row'), (x1, 'row'), (hc, 'row'), (g_post, 'full'), (w_o, 'full'),
                     (w_q, 'full'), (g_pre, 'full'), (k, kv_spec), (v, kv_spec)],
                    [('row', D, BF16), ('row', D, BF16), ('row', D, F32),
                     (kv_spec, (nb * mlen, D), F32), (kv_spec, (nb * mlen, D), F32),
                     ('acc', (1, D), F32), ('acc', (1, D), F32)])


def _b_mem(dk, dv, mem, w_kv, g_mem, tm):
    n = mem.shape[0]

    def body(dk_ref, dv_ref, m_ref, w_ref, g_ref, dkv_ref, dg_ref):
        @pl.when(pl.program_id(0) == 0)
        def _():
            dg_ref[...] = jnp.zeros_like(dg_ref)
        dkb = dk_ref[...].astype(BF16)
        dvb = dv_ref[...].astype(BF16)
        dkv_ref[:, :D] = dkb
        dkv_ref[:, D:] = dvb
        dmn = _dot_nt(dkb, w_ref[:, :D]) + _dot_nt(dvb, w_ref[:, D:])
        _, dg = _rms_bwd(m_ref[...], g_ref[...], dmn)
        dg_ref[...] += dg

    return _rowcall("b_mem", body, n, tm, [(dk, 'row'), (dv, 'row'), (mem, 'row'), (w_kv, 'full'), (g_mem, 'full')],
                    [('row', 2 * D, BF16), ('acc', (1, D), F32)])


def _b_mix(dx1, mo, a, b, zg, ypre, g_post, w_mix_out, w_br_gm, w_br_s5, w_glu, tm):
    n = dx1.shape[0]

    def body(dx1_ref, mo_ref, a_ref, b_ref, zg_ref, ypre_ref, g_ref, wout_ref, wgm_ref, ws5_ref, wglu_ref,
             dmo_ref, da_ref, db_ref, dzg_ref, dygm_ref, dgate_ref, dypre_ref, dg_ref):
        @pl.when(pl.program_id(0) == 0)
        def _():
            dg_ref[...] = jnp.zeros_like(dg_ref)
        dmo, dg = _rms_bwd(mo_ref[...], g_ref[...], dx1_ref[...])
        dg_ref[...] += dg
        dmo = dmo.astype(BF16)
        dmo_ref[...] = dmo
        dmg = _dot_nt(dmo, wout_ref[...])
        zg = zg_ref[...]
        sa, sb = _sig(zg[:, :D]), _sig(zg[:, D:])
        da = (dmg * sa).astype(BF16)
        db = (dmg * sb).astype(BF16)
        da_ref[...] = da
        db_ref[...] = db
        dzg_ref[:, :D] = (dmg * a_ref[...].astype(F32) * sa * (1.0 - sa)).astype(BF16)
        dzg_ref[:, D:] = (dmg * b_ref[...].astype(F32) * sb * (1.0 - sb)).astype(BF16)
        dygm_ref[...] = _dot_nt(da, wgm_ref[...])
        dys5 = _dot_nt(db, ws5_ref[...])
        ypre = ypre_ref[...]
        yg = _gelu(ypre)
        sgt = _sig(_dot(yg.astype(BF16), wglu_ref[...]))
        dgate = (dys5 * yg * sgt * (1.0 - sgt)).astype(BF16)
        dgate_ref[...] = dgate
        dyg = dys5 * sgt + _dot_nt(dgate, wglu_ref[...])
        dypre_ref[...] = dyg * _gelu_grad(ypre)

    return _rowcall("b_mix", body, n, tm,
                    [(dx1, 'row'), (mo, 'row'), (a, 'row'), (b, 'row'), (zg, 'row'), (ypre, 'row'), (g_post, 'full'),
                     (w_mix_out, 'full'), (w_br_gm, 'full'), (w_br_s5, 'full'), (w_glu, 'full')],
                    [('row', D, BF16), ('row', D, BF16), ('row', D, BF16), ('row', 2 * D, BF16), ('row', D, F32),
                     ('row', S5_W, BF16), ('row', S5_W, F32), ('acc', (1, D), F32)])


def _b_s5(zs5, dypre, states, bbc, ccm, avec, dvec, nb):
    n = zs5.shape[0]
    nc = n // nb // S5_T
    slab, bb, cm, av, dv, st = _s5_specs(nb, nc, True)

    def body(u_ref, dy_ref, st_ref, bb_ref, cc_ref, a_ref, d_ref,
             du_ref, dbb_ref, dcc_ref, da_ref, dd_ref, lcarry):
        first = jnp.logical_and(pl.program_id(1) == 0, pl.program_id(2) == 0)

        @pl.when(first)
        def _():
            dbb_ref[...] = jnp.zeros_like(dbb_ref)
            dcc_ref[...] = jnp.zeros_like(dcc_ref)
            da_ref[...] = jnp.zeros_like(da_ref)
            dd_ref[...] = jnp.zeros_like(dd_ref)

        @pl.when(pl.program_id(2) == 0)
        def _():
            lcarry[...] = jnp.zeros_like(lcarry)

        u = u_ref[...]
        dy = dy_ref[...]
        ub, dyb = u.astype(BF16), dy.astype(BF16)
        sin = st_ref[...]
        sr, si = _s5_states(u, bb_ref, a_ref, sin)
        ar, ai = a_ref[:, :512], a_ref[:, 512:]
        ds = _dot_nt(dyb, cc_ref[...])
        lr_, li_ = lcarry[:, :512], lcarry[:, 512:]
        row = lax.broadcasted_iota(jnp.int32, (S5_T, 512), 0)
        last = row == S5_T - 1
        gr = ds[:, :512] + jnp.where(last, ar * lr_ + ai * li_, 0.0)
        gi = ds[:, 512:] + jnp.where(last, ar * li_ - ai * lr_, 0.0)
        lr, li = _cscan(gr, gi, ar, -ai, reverse=True)
        head = row == 0
        lcarry[:, :512] = jnp.sum(jnp.where(head, lr, 0.0), axis=0, keepdims=True)
        lcarry[:, 512:] = jnp.sum(jnp.where(head, li, 0.0), axis=0, keepdims=True)
        s = jnp.concatenate([sr, si], axis=1).astype(BF16)
        lam = jnp.concatenate([lr, li], axis=1).astype(BF16)
        dcc_ref[...] += _dot_tn(s, dyb)
        dbb_ref[...] += _dot_tn(ub, lam)
        du_ref[...] = (_dot_nt(lam, bb_ref[...]) + d_ref[...] * dy).astype(BF16)
        dd_ref[...] += jnp.sum(dy * u, axis=0, keepdims=True)
        pr = jnp.where(head, sin[:, :512], pltpu.roll(sr, 1, 0))
        pi = jnp.where(head, sin[:, 512:], pltpu.roll(si, 1, 0))
        da_ref[:, :512] += jnp.sum(lr * pr + li * pi, axis=0, keepdims=True)
        da_ref[:, 512:] += jnp.sum(li * pr - lr * pi, axis=0, keepdims=True)

    return pl.pallas_call(
        body, name="b_s5", grid=(S5_SUPER, nb, nc), in_specs=[slab, slab, st, bb, cm, av, dv],
        out_specs=[slab, bb, cm, av, dv],
        out_shape=[jax.ShapeDtypeStruct((n, S5_W), BF16), jax.ShapeDtypeStruct(bbc.shape, F32),
                   jax.ShapeDtypeStruct(ccm.shape, F32), jax.ShapeDtypeStruct(avec.shape, F32),
                   jax.ShapeDtypeStruct(dvec.shape, F32)],
        scratch_shapes=[pltpu.VMEM((1, 1024), F32)],
        compiler_params=_params(("arbitrary", "arbitrary", "arbitrary")),
    )(zs5, dypre, states, bbc, ccm, avec, dvec)


def _b_gmlp(zgm, dygm, ln_g, ln_b, w_s, b_s, tm):
    n = zgm.shape[0]

    def body(z_ref, dy_ref, lg_ref, lb_ref, ws_ref, bs_ref, dz_ref, dws_ref, dbs_ref, dlg_ref, dlb_ref,
             du_s, dvn_s):
        @pl.when(pl.program_id(0) == 0)
        def _():
            dws_ref[...] = jnp.zeros_like(dws_ref)
            dbs_ref[...] = jnp.zeros_like(dbs_ref)
            dlg_ref[...] = jnp.zeros_like(dlg_ref)
            dlb_ref[...] = jnp.zeros_like(dlb_ref)
        z = z_ref[...]
        zg = _gelu(z)
        u = zg[:, :D]
        vh, r = _ln_stats(zg[:, D:])
        vn = (vh * lg_ref[...] + lb_ref[...]).astype(BF16)
        dy = dy_ref[...]
        keep = _tril()
        for g in range(GM_GROUPS):
            w = jnp.where(keep, ws_ref[g], 0.0).astype(BF16)
            cs = slice(g * LANE, (g + 1) * LANE)
            for c in range(tm // GM_CHUNK):
                rs = slice(c * GM_CHUNK, (c + 1) * GM_CHUNK)
                vb = vn[rs, cs]
                sv = _dot(w, vb) + bs_ref[g]
                dyb = dy[rs, cs]
                du_s[rs, cs] = dyb * sv
                dsv = dyb * u[rs, cs]
                dsvb = dsv.astype(BF16)
                dvn_s[rs, cs] = _dot_tn(w, dsvb)
                dws_ref[g] += jnp.where(keep, _dot_nt(dsvb, vb), 0.0)
                dbs_ref[g] += jnp.sum(dsv, axis=1, keepdims=True)
        dvn = dvn_s[...]
        dlg_ref[...] += jnp.sum(dvn * vh, axis=0, keepdims=True)
        dlb_ref[...] += jnp.sum(dvn, axis=0, keepdims=True)
        dvh = dvn * lg_ref[...]
        dv = r * (dvh - jnp.mean(dvh, axis=-1, keepdims=True) - vh * jnp.mean(dvh * vh, axis=-1, keepdims=True))
        dz_ref[:, :D] = (du_s[...] * _gelu_grad(z[:, :D])).astype(BF16)
        dz_ref[:, D:] = (dv * _gelu_grad(z[:, D:])).astype(BF16)

    return _rowcall("b_gmlp", body, n, tm,
                    [(zgm, 'row'), (dygm, 'row'), (ln_g, 'full'), (ln_b, 'full'), (w_s, 'full'), (b_s, 'full')],
                    [('row', 2 * D, BF16), ('acc', w_s.shape, F32), ('acc', b_s.shape, F32), ('acc', (1, D), F32),
                     ('acc', (1, D), F32)],
                    scratch=[pltpu.VMEM((tm, D), F32), pltpu.VMEM((tm, D), F32)])


def _b_in(dzgm, dzs5, dzg, dx1, x, w_in, g_pre, tm):
    n = x.shape[0]

    def body(d1_ref, d2_ref, d3_ref, dx1_ref, x_ref, w_ref, g_ref, gx_ref, dg_ref):
        @pl.when(pl.program_id(0) == 0)
        def _():
            dg_ref[...] = jnp.zeros_like(dg_ref)
        dh = (_dot_nt(d1_ref[...], w_ref[:, 0:2 * D]) + _dot_nt(d2_ref[...], w_ref[:, 2 * D:2 * D + S5_W])
              + _dot_nt(d3_ref[...], w_ref[:, 2 * D + S5_W:]))
        dx, dg = _rms_bwd(x_ref[...], g_ref[...], dh)
        dg_ref[...] += dg
        gx_ref[...] = dx1_ref[...] + dx

    return _rowcall("b_in", body, n, tm,
                    [(dzgm, 'row'), (dzs5, 'row'), (dzg, 'row'), (dx1, 'row'), (x, 'row'), (w_in, 'full'),
                     (g_pre, 'full')],
                    [('row', D, F32), ('acc', (1, D), F32)])


def _whole(name, body, ins, outs):
    return pl.pallas_call(body, name=name, out_shape=[jax.ShapeDtypeStruct(s, dt) for s, dt in outs],
                          compiler_params=_params())(*ins)


def _s5_disc_fwd(lr, li, ls, br, bi):
    def body(lr_ref, li_ref, ls_ref, br_ref, bi_ref, o1, o2, o3, o4):
        outs = _s5_disc(lr_ref[...], li_ref[...], ls_ref[...], br_ref[...], bi_ref[...])
        for o, val in zip((o1, o2, o3, o4), outs):
            o[...] = val

    return _whole("s5_disc_fwd", body, [lr, li, ls, br, bi],
                  [(lr.shape, F32), (lr.shape, F32), (br.shape, F32), (br.shape, F32)])


def _s5_disc_bwd(lr, li, ls, br, bi, cts):
    def body(lr_ref, li_ref, ls_ref, br_ref, bi_ref, c1, c2, c3, c4, o1, o2, o3, o4, o5):
        _, vjp = jax.vjp(_s5_disc, lr_ref[...], li_ref[...], ls_ref[...], br_ref[...], bi_ref[...])
        grads = vjp((c1[...], c2[...], c3[...], c4[...]))
        for o, val in zip((o1, o2, o3, o4, o5), grads):
            o[...] = val

    return _whole("s5_disc_bwd", body, [lr, li, ls, br, bi, *cts],
                  [(lr.shape, F32), (lr.shape, F32), (lr.shape, F32), (br.shape, F32), (br.shape, F32)])


def _sum_rows(name, parts, tm):
    def body(*refs):
        acc = refs[0][...]
        for r in refs[1:-1]:
            acc = acc + r[...]
        refs[-1][...] = acc

    return _rowcall(name, body, parts[0].shape[0], tm, [(p, 'row') for p in parts],
                    [('row', parts[0].shape[1], F32)])[0]


def _adamw(name, w, g, m, v, tm):
    c1 = 1.0 - ADAM_B1 ** ADAM_STEP
    c2 = 1.0 - ADAM_B2 ** ADAM_STEP

    def body(w_ref, g_ref, m_ref, v_ref, d_ref, mo_ref, vo_ref):
        gg = g_ref[...]
        mn = ADAM_B1 * m_ref[...] + (1.0 - ADAM_B1) * gg
        vn = ADAM_B2 * v_ref[...] + (1.0 - ADAM_B2) * (gg * gg)
        mo_ref[...] = mn
        vo_ref[...] = vn
        d_ref[...] = -ADAM_LR * ((mn / c1) / (jnp.sqrt(vn / c2) + ADAM_EPS) + ADAM_WD * w_ref[...])

    cols = w.shape[1]
    return _rowcall(name, body, w.shape[0], tm, [(w, 'row'), (g, 'row'), (m, 'row'), (v, 'row')],
                    [('row', cols, F32), ('row', cols, F32), ('row', cols, F32)])


def _picked_rowcall(name, body, grid, in_specs, out_cols, tm):
    return pl.pallas_call(
        lambda p_ref, *refs: body(*refs), name=name,
        grid_spec=pltpu.PrefetchScalarGridSpec(
            num_scalar_prefetch=1, grid=(grid,), in_specs=in_specs,
            out_specs=[pl.BlockSpec((tm, D), lambda i, p: (i, 0)) for _ in out_cols]),
        out_shape=[jax.ShapeDtypeStruct((grid * tm, D), dt) for dt in out_cols],
        compiler_params=_params(("arbitrary",)),
    )


def _reduce_big(slabs):
    rows = [s.shape[1] for s in slabs]
    blk = sum(rows) // 4
    tm = _tile_rows(blk, 512)
    per = blk // tm
    gpack = jnp.concatenate([s.reshape(4, 2, 2, r // 4, D) for s, r in zip(slabs, rows)], axis=3)

    def rdma(ss, rs, k, src, dst, peer):
        return pltpu.make_async_remote_copy(src_ref=src, dst_ref=dst, send_sem=ss.at[k], recv_sem=rs.at[k],
                                            device_id=peer, device_id_type=MESH)

    mx, my, mc = lax.axis_index("x"), lax.axis_index("y"), lax.axis_index("c")
    chip_me, chip_xn, chip_yn = 2 * mx + my, 2 * (1 - mx) + my, 2 * mx + (1 - my)

    def pair_body(srcs, dsts, ss, rs):
        c, _, (_, _, psib) = _place()
        cp = rdma(ss, rs, 0, srcs[0].at[:, 1 - c], dsts[0], psib)
        cp.start()
        cp.wait()

    from_sib = _comm_call("reduce_pair", [gpack], [((4, 2, blk, D), F32)], 1, pair_body)[0]

    def sum_both(a_ref, b_ref, o32_ref, o16_ref):
        s = a_ref[...].astype(F32) + b_ref[...].astype(F32)
        o32_ref[...] = s
        o16_ref[...] = s.astype(BF16)

    p32, p16 = _picked_rowcall(
        "sum_pair", sum_both, 8 * per,
        [pl.BlockSpec((None, None, None, tm, D), lambda i, p: (i // (2 * per), p[0], (i // per) % 2, i % per, 0)),
         pl.BlockSpec((tm, D), lambda i, p: (i, 0))],
        [F32, BF16], tm)(jnp.stack([mc]).astype(jnp.int32), gpack, from_sib.reshape(8 * blk, D))
    p32, p16 = p32.reshape(4, 2, blk, D), p16.reshape(4, 2, blk, D)

    def step1_body(srcs, dsts, ss, rs):
        _, (me, xn, yn, dg), (px, py, _) = _place()
        s16 = srcs[0]
        cps = [rdma(ss, rs, 0, s16.at[yn, 0], dsts[0].at[0], py), rdma(ss, rs, 1, s16.at[dg, 0], dsts[0].at[1], py),
               rdma(ss, rs, 2, s16.at[xn, 1], dsts[0].at[2], px), rdma(ss, rs, 3, s16.at[dg, 1], dsts[0].at[3], px)]
        for cp in cps:
            cp.start()
        for cp in cps:
            cp.wait()

    recv1 = _comm_call("reduce_step1", [p16], [((4, blk, D), BF16)], 4, step1_body)[0]

    s32, s16 = _picked_rowcall(
        "sum_step1", sum_both, 4 * per,
        [pl.BlockSpec((None, None, tm, D), lambda i, p: (p[i // per], i // (2 * per), i % per, 0)),
         pl.BlockSpec((tm, D), lambda i, p: (i, 0))],
        [F32, BF16], tm)(jnp.stack([chip_me, chip_xn, chip_me, chip_yn]).astype(jnp.int32), p32,
                         recv1.reshape(4 * blk, D))

    def step2_body(srcs, dsts, ss, rs):
        _, _, (px, py, _) = _place()
        cps = [rdma(ss, rs, 0, srcs[0].at[1], dsts[0].at[0], px), rdma(ss, rs, 1, srcs[0].at[3], dsts[0].at[1], py)]
        for cp in cps:
            cp.start()
        for cp in cps:
            cp.wait()

    recv2 = _comm_call("reduce_step2", [s16.reshape(4, blk, D)], [((2, blk, D), BF16)], 2, step2_body)[0]

    def sum_step2(a_ref, b_ref, o_ref):
        o_ref[...] = a_ref[...] + b_ref[...].astype(F32)

    red = _rowcall("sum_step2", sum_step2, 2 * blk, tm,
                   [(s32.reshape(4, blk, D), pl.BlockSpec((None, tm, D), lambda i: (2 * (i // per), i % per, 0))),
                    (recv2.reshape(2 * blk, D), 'row')],
                   [('row', D, F32)])[0].reshape(2, blk, D)

    def share_body(srcs, dsts, ss, rs):
        _, _, (_, _, psib) = _place()
        cp = rdma(ss, rs, 0, srcs[0], dsts[0], psib)
        cp.start()
        cp.wait()

    other = _comm_call("share_pair", [red], [((2, blk, D), F32)], 1, share_body)[0]
    lo = jnp.where(mc == 0, red, other)
    hi = jnp.where(mc == 0, other, red)
    out, off = [], 0
    for r in rows:
        q = r // 4
        out.append(jnp.concatenate([lo[0, off:off + q], lo[1, off:off + q], hi[0, off:off + q], hi[1, off:off + q]]))
        off += q
    return out


def _rows1024(a):
    flat = a.reshape(-1)
    pad = (-flat.shape[0]) % D
    if pad:
        flat = jnp.concatenate([flat, jnp.zeros((pad,), flat.dtype)])
    return flat.reshape(-1, D)


def _pack(arrs, pad_rows_to=8):
    parts = [_rows1024(a) for a in arrs]
    rows = sum(p.shape[0] for p in parts)
    pad = (-rows) % pad_rows_to
    if pad:
        parts.append(jnp.zeros((pad, D), parts[0].dtype))
    return jnp.concatenate(parts, axis=0)


def _unpack(packed, shapes):
    out, r = [], 0
    for s in shapes:
        size = math.prod(s)
        nr = -(-size // D)
        out.append(packed[r:r + nr].reshape(-1)[:size].reshape(s))
        r += nr
    return out


def _shard_slabs(full, axis):
    r, c = full.shape
    if axis == 0:
        return full.reshape(4, -1, D)
    return full.reshape(r, 4, c // 4).transpose(1, 0, 2).reshape(4, -1, D)


def _block_diag(t):
    eye = jnp.eye(8, dtype=t.dtype)
    j, g, a, b = t.shape
    return (t[:, :, :, None, :] * eye[None, :, None, :, None]).reshape(j, g * a, g * b)


def _block_diag_take(m, a, b):
    eye = jnp.eye(8, dtype=m.dtype)
    return (m.reshape(4, 8, a, 8, b) * eye[None, :, None, :, None]).sum(axis=3)


def kernel(x, mem, g_mix_pre, w_in, gm_ln_g, gm_ln_b, gm_w_s, gm_b_s, s5_lam_re, s5_lam_im, s5_log_step, s5_b_re, s5_b_im, s5_c_re, s5_c_im, s5_d, s5_w_glu, w_br_gm, w_br_s5, w_mix_out, g_mix_post, g_ca_pre, g_mem, ca_w_q, ca_w_kv, ca_w_o, g_ca_post, g_ffn_pre, ffn_w_gu, ffn_w_down, g_ffn_post, loss_target, m_g_mix_pre, m_w_in, m_gm_ln_g, m_gm_ln_b, m_gm_w_s, m_gm_b_s, m_s5_lam_re, m_s5_lam_im, m_s5_log_step, m_s5_b_re, m_s5_b_im, m_s5_c_re, m_s5_c_im, m_s5_d, m_s5_w_glu, m_w_br_gm, m_w_br_s5, m_w_mix_out, m_g_mix_post, m_g_ca_pre, m_g_mem, m_ca_w_q, m_ca_w_kv, m_ca_w_o, m_g_ca_post, m_g_ffn_pre, m_ffn_w_gu, m_ffn_w_down, m_g_ffn_post, v_g_mix_pre, v_w_in, v_gm_ln_g, v_gm_ln_b, v_gm_w_s, v_gm_b_s, v_s5_lam_re, v_s5_lam_im, v_s5_log_step, v_s5_b_re, v_s5_b_im, v_s5_c_re, v_s5_c_im, v_s5_d, v_s5_w_glu, v_w_br_gm, v_w_br_s5, v_w_mix_out, v_g_mix_post, v_g_ca_pre, v_g_mem, v_ca_w_q, v_ca_w_kv, v_ca_w_o, v_g_ca_post, v_g_ffn_pre, v_ffn_w_gu, v_ffn_w_down, v_g_ffn_post):
    a = dict(locals())
    w = {n: a[n][0] for n in WNAMES}
    nb, seq, _ = x.shape
    n = nb * seq
    mlen = mem.shape[1]
    tm = min(256, seq)
    tpb = seq // tm
    xf = x.reshape(n, D)
    tgt = loss_target.reshape(n, D)
    memf = mem.reshape(nb * mlen, D)
    tmm = min(256, mlen)

    big_names = list(BIG)
    local_shapes = {k: w[k].shape for k in big_names}
    axes = [BIG[k] for k in big_names]
    gathered = _gather_weights([w[k] for k in big_names], axes)
    wf = {k: (g.reshape(-1, g.shape[-1]) if ax == 0 else g) for k, g, ax in zip(big_names, gathered, axes)}

    vec = lambda name: w[name].reshape(1, D)

    to_lane = lambda p: p.reshape(1, -1)
    b_t = lambda p: p.transpose(2, 0, 1).reshape(16, -1)
    lr_l, li_l = to_lane(w['s5_lam_re']), to_lane(w['s5_lam_im'])
    ls_l = jnp.repeat(w['s5_log_step'], 64).reshape(1, -1)
    br_t, bi_t = b_t(w['s5_b_re']), b_t(w['s5_b_im'])
    ab_re, ab_im, bb_re, bb_im = _s5_disc_fwd(lr_l, li_l, ls_l, br_t, bi_t)
    blk = lambda t: _block_diag(t.reshape(16, 4, 8, 64).transpose(1, 2, 0, 3))
    bbc = jnp.concatenate([blk(bb_re), blk(bb_im)], axis=2).astype(BF16)
    cblk = lambda c: _block_diag(c.reshape(4, 8, 16, 64).transpose(0, 1, 3, 2))
    ccm = jnp.concatenate([cblk(w['s5_c_re']), -cblk(w['s5_c_im'])], axis=1).astype(BF16)
    avec = jnp.concatenate([ab_re.reshape(4, 1, 512), ab_im.reshape(4, 1, 512)], axis=2)
    dvec = w['s5_d'].reshape(4, 1, LANE)

    bs3 = w['gm_b_s'].reshape(GM_GROUPS, GM_CHUNK, 1)
    h, zgm, zs5, zg = _f_in(xf, vec('g_mix_pre'), wf['w_in'], tm)
    ygm = _f_gmlp(zgm, vec('gm_ln_g'), vec('gm_ln_b'), w['gm_w_s'], bs3, tm)
    ypre, states = _f_s5(zs5, bbc, ccm, avec, dvec, nb)
    yg, ys5, a_br, b_br, merged, mo, x1 = _f_mix(ygm, ypre, zg, xf, wf['s5_w_glu'], wf['w_br_gm'], wf['w_br_s5'],
                                                 wf['w_mix_out'], vec('g_mix_post'), tm)
    mem_n, kk, vv = _f_mem(memf, vec('g_mem'), wf['ca_w_kv'], tmm)
    hc, o_att, ao, x2 = _f_attn(x1, vec('g_ca_pre'), wf['ca_w_q'], kk, vv, wf['ca_w_o'], vec('g_ca_post'),
                                tm, tpb, mlen)
    hf, gu, act = _f_ffn_up(x2, vec('g_ffn_pre'), wf['ffn_w_gu'], tm)
    dn, d3, loss_cols = _f_ffn_down(act, x2, tgt, wf['ffn_w_down'], vec('g_ffn_post'), tm)

    gsm = {}
    gbig = {}
    ddn, dgu, gsm['g_ffn_post'] = _b_ffn_down(d3, dn, gu, vec('g_ffn_post'), wf['ffn_w_down'], tm)
    dx2, gsm['g_ffn_pre'] = _b_ffn_up(dgu, d3, x2, wf['ffn_w_gu'], vec('g_ffn_pre'), tm)
    gbig['ffn_w_down'] = _mm_tn("dw_ffn_down", act, ddn)
    gbig['ffn_w_gu'] = _mm_tn("dw_ffn_gu", hf, dgu, col_shards=4)
    dao, dq, dx1, dk, dv, gsm['g_ca_post'], gsm['g_ca_pre'] = _b_attn(
        dx2, ao, x1, hc, vec('g_ca_post'), wf['ca_w_o'], wf['ca_w_q'], vec('g_ca_pre'), kk, vv, tm, tpb, mlen)
    gbig['ca_w_o'] = _mm_tn("dw_ca_o", o_att, dao)
    gbig['ca_w_q'] = _mm_tn("dw_ca_q", hc, dq)
    dkv, gsm['g_mem'] = _b_mem(dk, dv, memf, wf['ca_w_kv'], vec('g_mem'), tmm)
    gbig['ca_w_kv'] = _mm_tn("dw_ca_kv", mem_n, dkv, col_shards=4)
    dmo, da_br, db_br, dzg, dygm, dgate, dypre, gsm['g_mix_post'] = _b_mix(
        dx1, mo, a_br, b_br, zg, ypre, vec('g_mix_post'), wf['w_mix_out'], wf['w_br_gm'], wf['w_br_s5'],
        wf['s5_w_glu'], tm)
    gbig['w_mix_out'] = _mm_tn("dw_mix_out", merged, dmo)
    gbig['w_br_gm'] = _mm_tn("dw_br_gm", ygm, da_br)
    gbig['w_br_s5'] = _mm_tn("dw_br_s5", ys5, db_br, col_shards=4)
    gbig['s5_w_glu'] = _mm_tn("dw_s5_glu", yg, dgate)
    dzs5, dbbc, dccm, davec, ddvec = _b_s5(zs5, dypre, states, bbc, ccm, avec, dvec, nb)
    dzgm, gsm['gm_w_s'], dbs3, gsm['gm_ln_g'], gsm['gm_ln_b'] = _b_gmlp(
        zgm, dygm, vec('gm_ln_g'), vec('gm_ln_b'), w['gm_w_s'], bs3, tm)
    gsm['gm_b_s'] = dbs3
    grad_x, gsm['g_mix_pre'] = _b_in(dzgm, dzs5, dzg, dx1, xf, wf['w_in'], vec('g_mix_pre'), tm)
    gbig['w_in'] = jnp.concatenate(
        [_mm_tn("dw_in_gm", h, dzgm), _mm_tn("dw_in_s5", h, dzs5), _mm_tn("dw_in_g", h, dzg)], axis=1)

    unblk = lambda m_: _block_diag_take(m_, 16, 64).transpose(2, 0, 1, 3).reshape(16, -1)
    d_bb_re, d_bb_im = unblk(dbbc[:, :, :512]), unblk(dbbc[:, :, 512:])
    cunblk = lambda m_: _block_diag_take(m_, 64, 16).transpose(0, 1, 3, 2).reshape(32, 16, 64)
    gsm['s5_c_re'] = cunblk(dccm[:, :512, :])
    gsm['s5_c_im'] = -cunblk(dccm[:, 512:, :])
    d_ab_re, d_ab_im = davec[:, :, :512].reshape(1, -1), davec[:, :, 512:].reshape(1, -1)
    g_lr, g_li, g_ls, g_br, g_bi = _s5_disc_bwd(lr_l, li_l, ls_l, br_t, bi_t, (d_ab_re, d_ab_im, d_bb_re, d_bb_im))
    gsm['s5_lam_re'], gsm['s5_lam_im'] = g_lr.reshape(32, 64), g_li.reshape(32, 64)
    gsm['s5_log_step'] = g_ls.reshape(32, 64).sum(axis=1)
    from_t = lambda t: t.reshape(16, 32, 64).transpose(1, 2, 0)
    gsm['s5_b_re'], gsm['s5_b_im'] = from_t(g_br), from_t(g_bi)
    gsm['s5_d'] = ddvec.reshape(32, 16)

    small_shapes = [w[k].shape for k in SMALL]
    spack = _pack([gsm[k].reshape(w[k].shape) for k in SMALL] + [loss_cols], 8)
    rs = spack.shape[0]
    me_slot = 4 * lax.axis_index("x") + 2 * lax.axis_index("y") + lax.axis_index("c")
    sall = lax.dynamic_update_index_in_dim(_gather_all("gather_small", spack), spack, me_slot, 0)
    ssum = _sum_rows("sum_small", [sall[i] for i in range(8)], rs)
    small_red = _unpack(ssum, small_shapes + [(1, D)])
    loss = 0.5 * jnp.sum(small_red[-1]) / D

    slabs = [gbig[k].reshape(4, -1, D) if k != 'w_in' else _shard_slabs(gbig[k], 1) for k in big_names]
    greds = _reduce_big(slabs)

    res_big = {}
    for k, g in zip(big_names, greds):
        rows_w = g.shape[0]
        flat = lambda t: t.reshape(rows_w, D)
        dl, mn, vn = _adamw("adamw_" + k, flat(w[k]), g, flat(a['m_' + k][0]), flat(a['v_' + k][0]),
                            _tile_rows(rows_w))
        res_big[k] = tuple(t.reshape(local_shapes[k]) for t in (g, dl, mn, vn))
    pad_loss = [jnp.zeros((1, D), F32)]
    wsm = _pack([w[k] for k in SMALL] + pad_loss, 8)
    msm = _pack([a['m_' + k][0] for k in SMALL] + pad_loss, 8)
    vsm = _pack([a['v_' + k][0] for k in SMALL] + pad_loss, 8)
    dl_s, mn_s, vn_s = _adamw("adamw_small", wsm, ssum, msm, vsm, rs)
    res_small = {k: vals for k, vals in zip(SMALL, zip(*[_unpack(p, small_shapes) for p in (ssum, dl_s, mn_s, vn_s)]))}

    res = {**res_big, **res_small}
    outs = [loss, grad_x.reshape(nb, seq, D)]
    for i in range(4):
        outs += [res[k][i][None] for k in WNAMES]
    return tuple(outs)


def _tile_rows(rows, cap=384):
    best = rows
    for t in range(16, min(rows, cap) + 1, 16):
        if rows % t == 0:
            best = t
    return best
```

```python
import functools
import math

import jax
import jax.numpy as jnp
from jax import lax
from jax.experimental import pallas as pl
from jax.experimental.pallas import tpu as pltpu

F32 = jnp.float32
BF16 = jnp.bfloat16
EPS = 1e-6
D = 1024
GM_CHUNK = 128
GM_GROUPS = 8
S5_W = 512
S5_SUPER = 4
S5_T = 512
HEADS = 4
HEAD_DIM = 256
FFN_H = 2816
LANE = 128
VMEM_LIMIT = 56 * 1024 * 1024
MESH = pl.DeviceIdType.MESH

ADAM_LR, ADAM_B1, ADAM_B2, ADAM_EPS, ADAM_WD, ADAM_STEP = 0.001, 0.9, 0.999, 1e-08, 0.01, 10

WNAMES = ['g_mix_pre', 'w_in', 'gm_ln_g', 'gm_ln_b', 'gm_w_s', 'gm_b_s', 's5_lam_re', 's5_lam_im', 's5_log_step',
          's5_b_re', 's5_b_im', 's5_c_re', 's5_c_im', 's5_d', 's5_w_glu', 'w_br_gm', 'w_br_s5', 'w_mix_out',
          'g_mix_post', 'g_ca_pre', 'g_mem', 'ca_w_q', 'ca_w_kv', 'ca_w_o', 'g_ca_post', 'g_ffn_pre', 'ffn_w_gu',
          'ffn_w_down', 'g_ffn_post']
BIG = {'w_in': 1, 's5_w_glu': 0, 'w_br_gm': 0, 'w_br_s5': 1, 'w_mix_out': 0, 'ca_w_q': 0, 'ca_w_kv': 1,
       'ca_w_o': 0, 'ffn_w_gu': 1, 'ffn_w_down': 0}
SMALL = [n for n in WNAMES if n not in BIG]


def _dot(a, b):
    return jnp.dot(a, b, preferred_element_type=F32)


def _dot_nt(a, b):
    return lax.dot_general(a, b, (((1,), (1,)), ((), ())), preferred_element_type=F32)


def _dot_tn(a, b):
    return lax.dot_general(a, b, (((0,), (0,)), ((), ())), preferred_element_type=F32)


def _rms_fwd(x, g):
    r = lax.rsqrt(jnp.mean(x * x, axis=-1, keepdims=True) + EPS)
    return x * r * g


def _rms_bwd(x, g, dy):
    r = lax.rsqrt(jnp.mean(x * x, axis=-1, keepdims=True) + EPS)
    xh = x * r
    gdy = dy * g
    dx = r * (gdy - xh * jnp.mean(gdy * xh, axis=-1, keepdims=True))
    dg = jnp.sum(dy * xh, axis=0, keepdims=True)
    return dx, dg


_GC = math.sqrt(2.0 / math.pi)


def _gelu(x):
    return 0.5 * x * (1.0 + jnp.tanh(_GC * (x + 0.044715 * x * x * x)))


def _gelu_grad(x):
    t = jnp.tanh(_GC * (x + 0.044715 * x * x * x))
    return 0.5 * (1.0 + t) + 0.5 * x * (1.0 - t * t) * _GC * (1.0 + 3 * 0.044715 * x * x)


def _sig(x):
    return 1.0 / (1.0 + jnp.exp(-x))


def _cscan(br, bi, ar, ai, reverse):
    t = br.shape[0]
    row = lax.broadcasted_iota(jnp.int32, br.shape, 0)
    pr, pi = ar, ai
    sh = 1
    while sh < t:
        if reverse:
            keep = row < t - sh
            rr, ri = pltpu.roll(br, t - sh, 0), pltpu.roll(bi, t - sh, 0)
        else:
            keep = row >= sh
            rr, ri = pltpu.roll(br, sh, 0), pltpu.roll(bi, sh, 0)
        rr = jnp.where(keep, rr, 0.0)
        ri = jnp.where(keep, ri, 0.0)
        br, bi = br + pr * rr - pi * ri, bi + pr * ri + pi * rr
        pr, pi = pr * pr - pi * pi, 2.0 * pr * pi
        sh *= 2
    return br, bi


def _s5_disc(lr, li, ls, br, bi):
    step = jnp.exp(ls)
    mag = jnp.exp(lr * step)
    ab_re = mag * jnp.cos(li * step)
    ab_im = mag * jnp.sin(li * step)
    den = lr * lr + li * li
    nr = ab_re - 1.0
    co_re = (nr * lr + ab_im * li) / den
    co_im = (ab_im * lr - nr * li) / den
    return ab_re, ab_im, co_re * br - co_im * bi, co_re * bi + co_im * br


def _params(sem=None):
    return pltpu.CompilerParams(dimension_semantics=sem, vmem_limit_bytes=VMEM_LIMIT)


def _rowcall(name, body, n_rows, tm, ins, outs, scratch=()):
    def spec(kind, shape):
        if kind == 'row':
            return pl.BlockSpec((tm, shape[1]), lambda i: (i, 0))
        if kind in ('full', 'acc'):
            nd = len(shape)
            return pl.BlockSpec(tuple(shape), lambda i: (0,) * nd)
        return kind

    in_specs = [spec(k, a.shape) for a, k in ins]
    out_shape, out_specs = [], []
    for k, s, dt in outs:
        shape = (n_rows, s) if k == 'row' else tuple(s)
        out_shape.append(jax.ShapeDtypeStruct(shape, dt))
        out_specs.append(spec(k, shape))
    return pl.pallas_call(
        body, name=name, grid=(n_rows // tm,), in_specs=in_specs, out_specs=out_specs, out_shape=out_shape,
        scratch_shapes=list(scratch), compiler_params=_params(("arbitrary",)),
    )(*[a for a, _ in ins])


def _tile(n, cap):
    if n <= cap:
        return n
    best = LANE
    for k in range(1, n // LANE + 1):
        t = k * LANE
        if n % t == 0 and t <= cap:
            best = t
    return best


def _mm_tn(name, a, b, col_shards=1):
    n, k = a.shape
    m = b.shape[1]
    mloc = m // col_shards
    tk, tn, tr = _tile(k, 1536), _tile(mloc, 1536), min(n, 512)
    per = mloc // tn

    def body(a_ref, b_ref, o_ref):
        @pl.when(pl.program_id(2) == 0)
        def _():
            o_ref[...] = jnp.zeros_like(o_ref)
        o_ref[...] += _dot_tn(a_ref[...].astype(BF16), b_ref[...].astype(BF16))

    if col_shards == 1:
        out_spec = pl.BlockSpec((tk, tn), lambda i, j, r: (i, j))
        out_shape = (k, m)
    else:
        out_spec = pl.BlockSpec((None, tk, tn), lambda i, j, r: (j // per, i, j % per))
        out_shape = (col_shards, k, mloc)
    return pl.pallas_call(
        body, name=name, grid=(k // tk, m // tn, n // tr),
        in_specs=[pl.BlockSpec((tr, tk), lambda i, j, r: (r, i)), pl.BlockSpec((tr, tn), lambda i, j, r: (r, j))],
        out_specs=out_spec, out_shape=jax.ShapeDtypeStruct(out_shape, F32),
        compiler_params=_params(("parallel", "parallel", "arbitrary")),
    )(a, b)


def _comm_call(name, srcs, outs, nsem, body_fn, aliases=None):
    ns, no = len(srcs), len(outs)

    def body(*refs):
        body_fn(refs[:ns], refs[ns:ns + no], refs[ns + no], refs[ns + no + 1])

    hbm = pl.BlockSpec(memory_space=pltpu.HBM)
    return pl.pallas_call(
        body, name=name, in_specs=[hbm] * ns, out_specs=[hbm] * no,
        out_shape=[jax.ShapeDtypeStruct(s, d) for s, d in outs],
        scratch_shapes=[pltpu.SemaphoreType.DMA((nsem,)), pltpu.SemaphoreType.DMA((nsem,))],
        input_output_aliases=aliases or {},
    )(*srcs)


def _place():
    x, y, c = lax.axis_index("x"), lax.axis_index("y"), lax.axis_index("c")
    chips = (2 * x + y, 2 * (1 - x) + y, 2 * x + (1 - y), 2 * (1 - x) + (1 - y))
    peers = ((1 - x, y, c), (x, 1 - y, c), (x, y, 1 - c))
    return c, chips, peers


def _gather_weights(shards, axes):
    nw = len(shards)
    shapes = [s.shape for s in shards]
    outs = [((4,) + s if ax == 0 else (s[0], 4 * s[1]), BF16) for s, ax in zip(shapes, axes)]

    def win(ref, i, chip, start, rows):
        r, cols = shapes[i]
        if axes[i] == 0:
            return ref.at[chip, pl.ds(start, rows), :]
        return ref.at[pl.ds(start, rows), pl.ds(chip * cols, cols)]

    def place_body(*refs):
        srcs, dsts, bufs, sem = refs[:nw], refs[nw:2 * nw], refs[2 * nw:3 * nw], refs[3 * nw]
        me = 2 * lax.axis_index("x") + lax.axis_index("y")
        cps = []
        for i in range(nw):
            bufs[i][...] = srcs[i][...].astype(BF16)
            cps.append(pltpu.make_async_copy(bufs[i], win(dsts[i], i, me, 0, shapes[i][0]), sem.at[i]))
            cps[-1].start()
        for cp in cps:
            cp.wait()

    placed = pl.pallas_call(
        place_body, name="cast_w", out_shape=[jax.ShapeDtypeStruct(s, d) for s, d in outs],
        in_specs=[pl.BlockSpec(memory_space=pltpu.VMEM)] * nw, out_specs=[pl.BlockSpec(memory_space=pltpu.HBM)] * nw,
        scratch_shapes=[pltpu.VMEM(s, BF16) for s in shapes] + [pltpu.SemaphoreType.DMA((nw,))],
        compiler_params=_params(),
    )(*shards)

    def body_fn(srcs, dsts, ss, rs):
        c, (me, xn, yn, dg), (px, py, psib) = _place()

        def rdma(k, src, dst, peer):
            return pltpu.make_async_remote_copy(src_ref=src, dst_ref=dst, send_sem=ss.at[k], recv_sem=rs.at[k],
                                                device_id=peer, device_id_type=MESH)

        first, later = [], []
        for i in range(nw):
            qr, k0 = shapes[i][0] // 4, 6 * i
            q0, q1 = 2 * c * qr, (2 * c + 1) * qr
            w0, w1 = win(dsts[i], i, me, q0, qr), win(dsts[i], i, me, q1, qr)
            cps = [rdma(k0, w0, w0, px), rdma(k0 + 1, w1, w1, py), rdma(k0 + 2, w0, w0, py), rdma(k0 + 3, w1, w1, px)]
            for cp in cps:
                cp.start()
            first.append(cps)
        for i in range(nw):
            qr, k0 = shapes[i][0] // 4, 6 * i
            q0, q1 = 2 * c * qr, (2 * c + 1) * qr
            first[i][0].wait_recv()
            f0 = rdma(k0 + 4, win(dsts[i], i, xn, q0, qr), win(dsts[i], i, xn, q0, qr), py)
            f0.start()
            first[i][1].wait_recv()
            f1 = rdma(k0 + 5, win(dsts[i], i, yn, q1, qr), win(dsts[i], i, yn, q1, qr), px)
            f1.start()
            later.append([f0, f1])
        swaps = []
        for i in range(nw):
            hr = shapes[i][0] // 2
            first[i][2].wait_recv()
            first[i][3].wait_recv()
            later[i][0].wait_recv()
            later[i][1].wait_recv()
            for j, chip in enumerate((xn, yn, dg)):
                hw = win(dsts[i], i, chip, c * hr, hr)
                sw = rdma(6 * nw + 3 * i + j, hw, hw, psib)
                sw.start()
                swaps.append(sw)
        for sw in swaps:
            sw.wait()
        for i in range(nw):
            for cp in first[i] + later[i]:
                cp.wait_send()

    return _comm_call("gather_w", placed, outs, 9 * nw, body_fn, aliases={i: i for i in range(nw)})


def _gather_all(name, src):
    flips7 = [(fx, fy, fc) for fx in (0, 1) for fy in (0, 1) for fc in (0, 1) if fx or fy or fc]

    def body_fn(srcs, dsts, ss, rs):
        me = (lax.axis_index("x"), lax.axis_index("y"), lax.axis_index("c"))
        slot = 4 * me[0] + 2 * me[1] + me[2]
        cps = []
        for k, flips in enumerate(flips7):
            peer = tuple(1 - p if f else p for p, f in zip(me, flips))
            cps.append(pltpu.make_async_remote_copy(src_ref=srcs[0], dst_ref=dsts[0].at[slot], send_sem=ss.at[k],
                                                    recv_sem=rs.at[k], device_id=peer, device_id_type=MESH))
            cps[-1].start()
        for cp in cps:
            cp.wait()

    return _comm_call(name, [src], [((8,) + src.shape, src.dtype)], 7, body_fn)[0]


def _f_in(x, g, w_in, tm):
    n = x.shape[0]

    def body(x_ref, g_ref, w_ref, h_ref, zgm_ref, zs5_ref, zg_ref):
        h = _rms_fwd(x_ref[...], g_ref[...]).astype(BF16)
        h_ref[...] = h
        zgm_ref[...] = _dot(h, w_ref[:, 0:2 * D])
        zs5_ref[...] = _dot(h, w_ref[:, 2 * D:2 * D + S5_W])
        zg_ref[...] = _dot(h, w_ref[:, 2 * D + S5_W:])

    return _rowcall("f_in", body, n, tm, [(x, 'row'), (g, 'full'), (w_in, 'full')],
                    [('row', D, BF16), ('row', 2 * D, F32), ('row', S5_W, F32), ('row', 2 * D, F32)])


def _tril():
    r = lax.broadcasted_iota(jnp.int32, (GM_CHUNK, GM_CHUNK), 0)
    c = lax.broadcasted_iota(jnp.int32, (GM_CHUNK, GM_CHUNK), 1)
    return r >= c


def _ln_stats(v):
    mu = jnp.mean(v, axis=-1, keepdims=True)
    vc = v - mu
    r = lax.rsqrt(jnp.mean(vc * vc, axis=-1, keepdims=True) + EPS)
    return vc * r, r


def _f_gmlp(zgm, ln_g, ln_b, w_s, b_s, tm):
    n = zgm.shape[0]

    def body(z_ref, lg_ref, lb_ref, ws_ref, bs_ref, y_ref):
        zg = _gelu(z_ref[...])
        u = zg[:, :D]
        vh, _ = _ln_stats(zg[:, D:])
        vn = (vh * lg_ref[...] + lb_ref[...]).astype(BF16)
        keep = _tril()
        for g in range(GM_GROUPS):
            w = jnp.where(keep, ws_ref[g], 0.0).astype(BF16)
            cs = slice(g * LANE, (g + 1) * LANE)
            for c in range(tm // GM_CHUNK):
                rs = slice(c * GM_CHUNK, (c + 1) * GM_CHUNK)
                sv = _dot(w, vn[rs, cs]) + bs_ref[g]
                y_ref[rs, cs] = (u[rs, cs] * sv).astype(BF16)

    return _rowcall("f_gmlp", body, n, tm,
                    [(zgm, 'row'), (ln_g, 'full'), (ln_b, 'full'), (w_s, 'full'), (b_s, 'full')],
                    [('row', D, BF16)])[0]


def _s5_specs(nb, nc, t, rev):
    def cc(c):
        return nc - 1 - c if rev else c
    slab = pl.BlockSpec((t, LANE), lambda j, b, c: (b * nc + cc(c), j))
    bb = pl.BlockSpec((None, LANE, 1024), lambda j, b, c: (j, 0, 0))
    cm = pl.BlockSpec((None, 1024, LANE), lambda j, b, c: (j, 0, 0))
    av = pl.BlockSpec((None, 1, 1024), lambda j, b, c: (j, 0, 0))
    dv = pl.BlockSpec((None, 1, LANE), lambda j, b, c: (j, 0, 0))
    st = pl.BlockSpec((None, None, None, 1, 1024), lambda j, b, c: (j, b, cc(c), 0, 0))
    return slab, bb, cm, av, dv, st


def _seg_load(ref, seg):
    return jnp.concatenate([ref[pl.ds(k, 8, stride=seg), :] for k in range(seg)], axis=0)


def _seg_store(ref, val, seg):
    for k in range(seg):
        ref[pl.ds(k, 8, stride=seg), :] = val[8 * k:8 * k + 8, :]


def _seg_scan(s_ref, ar, ai, seg, reverse):
    def step(i, carry):
        k = seg - 1 - i if reverse else i
        sr, si = carry
        nr = ar * sr - ai * si + s_ref[k, :, :512]
        ni = ar * si + ai * sr + s_ref[k, :, 512:]
        s_ref[k, :, :512] = nr
        s_ref[k, :, 512:] = ni
        return nr, ni

    zero = jnp.zeros((8, 512), F32)
    return lax.fori_loop(0, seg, step, (zero, zero), unroll=4)


def _seg_entries(tr, ti, pr, pi, cin, reverse):
    row = lax.broadcasted_iota(jnp.int32, (8, 512), 0)
    edge = row == (7 if reverse else 0)
    cr, ci = cin[:, :512], cin[:, 512:]
    xr = tr + jnp.where(edge, pr * cr - pi * ci, 0.0)
    xi = ti + jnp.where(edge, pr * ci + pi * cr, 0.0)
    ir, ii = _cscan(xr, xi, pr, pi, reverse)
    shift = 7 if reverse else 1
    er = jnp.where(edge, cr, pltpu.roll(ir, shift, 0))
    ei = jnp.where(edge, ci, pltpu.roll(ii, shift, 0))
    far = row == (0 if reverse else 7)
    out = jnp.concatenate([jnp.sum(jnp.where(far, ir, 0.0), axis=0, keepdims=True),
                           jnp.sum(jnp.where(far, ii, 0.0), axis=0, keepdims=True)], axis=1)
    return er, ei, out


def _seg_power(ar, ai, seg):
    pr, pi = ar, ai
    for _ in range(seg.bit_length() - 1):
        pr, pi = pr * pr - pi * pi, 2.0 * pr * pi
    return pr, pi


def _f_s5(zs5, bbc, ccm, avec, dvec, nb, t):
    n = zs5.shape[0]
    nc = n // nb // t
    seg = t // 8
    slab, bb, cm, av, dv, st = _s5_specs(nb, nc, t, False)

    def body(u_ref, bb_ref, cc_ref, a_ref, d_ref, y_ref, st_ref, carry, s_ref):
        @pl.when(pl.program_id(2) == 0)
        def _():
            carry[...] = jnp.zeros_like(carry)
        cin = carry[...]
        st_ref[...] = cin
        up = _seg_load(u_ref, seg)
        s_ref[...] = _dot(up.astype(BF16), bb_ref[...]).reshape(seg, 8, 1024)
        a1r, a1i = a_ref[:, :512], a_ref[:, 512:]
        ar, ai = jnp.broadcast_to(a1r, (8, 512)), jnp.broadcast_to(a1i, (8, 512))
        tr, ti = _seg_scan(s_ref, ar, ai, seg, False)
        er, ei, cout = _seg_entries(tr, ti, *_seg_power(a1r, a1i, seg), cin, False)
        carry[...] = cout

        def apply(k, pe):
            pr, pi = pe
            pr, pi = ar * pr - ai * pi, ar * pi + ai * pr
            s_ref[k, :, :512] = s_ref[k, :, :512] + pr
            s_ref[k, :, 512:] = s_ref[k, :, 512:] + pi
            return pr, pi

        lax.fori_loop(0, seg, apply, (er, ei), unroll=4)
        s = s_ref[...].reshape(t, 1024).astype(BF16)
        _seg_store(y_ref, _dot(s, cc_ref[...]) + d_ref[...] * up, seg)

    return pl.pallas_call(
        body, name="f_s5", grid=(S5_SUPER, nb, nc), in_specs=[slab, bb, cm, av, dv], out_specs=[slab, st],
        out_shape=[jax.ShapeDtypeStruct((n, S5_W), F32), jax.ShapeDtypeStruct((S5_SUPER, nb, nc, 1, 1024), F32)],
        scratch_shapes=[pltpu.VMEM((1, 1024), F32), pltpu.VMEM((seg, 8, 1024), F32)],
        compiler_params=_params(("arbitrary", "arbitrary", "arbitrary")),
    )(zs5, bbc, ccm, avec, dvec)


def _f_mix(ygm, ypre, zg, x, w_glu, w_br_gm, w_br_s5, w_mix_out, g_post, tm):
    n = x.shape[0]

    def body(ygm_ref, ypre_ref, zg_ref, x_ref, wglu_ref, wgm_ref, ws5_ref, wout_ref, g_ref,
             yg_ref, ys5_ref, a_ref, b_ref, mg_ref, mo_ref, x1_ref):
        yg = _gelu(ypre_ref[...])
        ygb = yg.astype(BF16)
        yg_ref[...] = ygb
        ys5 = (yg * _sig(_dot(ygb, wglu_ref[...]))).astype(BF16)
        ys5_ref[...] = ys5
        a = _dot(ygm_ref[...], wgm_ref[...])
        b = _dot(ys5, ws5_ref[...])
        a_ref[...] = a.astype(BF16)
        b_ref[...] = b.astype(BF16)
        zg = zg_ref[...]
        merged = (_sig(zg[:, :D]) * a + _sig(zg[:, D:]) * b).astype(BF16)
        mg_ref[...] = merged
        mo = _dot(merged, wout_ref[...])
        mo_ref[...] = mo
        x1_ref[...] = x_ref[...] + _rms_fwd(mo, g_ref[...])

    return _rowcall("f_mix", body, n, tm,
                    [(ygm, 'row'), (ypre, 'row'), (zg, 'row'), (x, 'row'), (w_glu, 'full'), (w_br_gm, 'full'),
                     (w_br_s5, 'full'), (w_mix_out, 'full'), (g_post, 'full')],
                    [('row', S5_W, BF16), ('row', S5_W, BF16), ('row', D, BF16), ('row', D, BF16),
                     ('row', D, BF16), ('row', D, F32), ('row', D, F32)])


def _f_mem(mem, g_mem, w_kv, tm):
    n = mem.shape[0]

    def body(m_ref, g_ref, w_ref, mn_ref, k_ref, v_ref):
        mn = _rms_fwd(m_ref[...], g_ref[...]).astype(BF16)
        mn_ref[...] = mn
        k_ref[...] = _dot(mn, w_ref[:, :D]).astype(BF16)
        v_ref[...] = _dot(mn, w_ref[:, D:]).astype(BF16)

    return _rowcall("f_mem", body, n, tm, [(mem, 'row'), (g_mem, 'full'), (w_kv, 'full')],
                    [('row', D, BF16), ('row', D, BF16), ('row', D, BF16)])


def _softmax(s):
    m = jnp.max(s, axis=-1, keepdims=True)
    e = jnp.exp(s - m)
    return e / jnp.sum(e, axis=-1, keepdims=True)


def _f_attn(x1, g_pre, w_q, k, v, w_o, g_post, tm, tpb, mlen):
    n = x1.shape[0]
    kv_spec = pl.BlockSpec((mlen, D), lambda i: (i // tpb, 0))
    scale = HEAD_DIM ** -0.5

    def body(x_ref, gp_ref, wq_ref, k_ref, v_ref, wo_ref, go_ref, hc_ref, o_ref, ao_ref, x2_ref):
        x1v = x_ref[...]
        hc = _rms_fwd(x1v, gp_ref[...]).astype(BF16)
        hc_ref[...] = hc
        q = _dot(hc, wq_ref[...])
        for h in range(HEADS):
            hs = slice(h * HEAD_DIM, (h + 1) * HEAD_DIM)
            p = _softmax(_dot_nt(q[:, hs].astype(BF16), k_ref[:, hs]) * scale)
            o_ref[:, hs] = _dot(p.astype(BF16), v_ref[:, hs]).astype(BF16)
        ao = _dot(o_ref[...], wo_ref[...])
        ao_ref[...] = ao
        x2_ref[...] = x1v + _rms_fwd(ao, go_ref[...])

    return _rowcall("f_attn", body, n, tm,
                    [(x1, 'row'), (g_pre, 'full'), (w_q, 'full'), (k, kv_spec), (v, kv_spec), (w_o, 'full'),
                     (g_post, 'full')],
                    [('row', D, BF16), ('row', D, BF16), ('row', D, F32), ('row', D, F32)])


def _f_ffn_up(x2, g_pre, w_gu, tm):
    n = x2.shape[0]

    def body(x_ref, g_ref, w_ref, hf_ref, gu_ref, act_ref):
        hf = _rms_fwd(x_ref[...], g_ref[...]).astype(BF16)
        hf_ref[...] = hf
        gg = _dot(hf, w_ref[:, :FFN_H])
        uu = _dot(hf, w_ref[:, FFN_H:])
        gu_ref[:, :FFN_H] = gg.astype(BF16)
        gu_ref[:, FFN_H:] = uu.astype(BF16)
        act_ref[...] = (gg * _sig(gg) * uu).astype(BF16)

    return _rowcall("f_ffn_up", body, n, tm, [(x2, 'row'), (g_pre, 'full'), (w_gu, 'full')],
                    [('row', D, BF16), ('row', 2 * FFN_H, BF16), ('row', FFN_H, BF16)])


def _f_ffn_down(act, x2, tgt, w_down, g_post, tm):
    n = x2.shape[0]

    def body(a_ref, x_ref, t_ref, w_ref, g_ref, dn_ref, d3_ref, loss_ref):
        @pl.when(pl.program_id(0) == 0)
        def _():
            loss_ref[...] = jnp.zeros_like(loss_ref)
        dn = _dot(a_ref[...], w_ref[...])
        dn_ref[...] = dn
        err = x_ref[...] + _rms_fwd(dn, g_ref[...]) - t_ref[...]
        d3_ref[...] = err * (1.0 / D)
        loss_ref[...] += jnp.sum(err * err, axis=0, keepdims=True)

    return _rowcall("f_ffn_down", body, n, tm,
                    [(act, 'row'), (x2, 'row'), (tgt, 'row'), (w_down, 'full'), (g_post, 'full')],
                    [('row', D, F32), ('row', D, F32), ('acc', (1, D), F32)])


def _b_ffn_down(d3, dn, gu, g_post, w_down, tm):
    n = d3.shape[0]

    def body(d3_ref, dn_ref, gu_ref, g_ref, w_ref, ddn_ref, dgu_ref, dg_ref):
        @pl.when(pl.program_id(0) == 0)
        def _():
            dg_ref[...] = jnp.zeros_like(dg_ref)
        ddn, dg = _rms_bwd(dn_ref[...], g_ref[...], d3_ref[...])
        dg_ref[...] += dg
        ddn = ddn.astype(BF16)
        ddn_ref[...] = ddn
        dact = _dot_nt(ddn, w_ref[...])
        gg = gu_ref[:, :FFN_H].astype(F32)
        uu = gu_ref[:, FFN_H:].astype(F32)
        sg = _sig(gg)
        dgu_ref[:, :FFN_H] = (dact * uu * sg * (1.0 + gg * (1.0 - sg))).astype(BF16)
        dgu_ref[:, FFN_H:] = (dact * gg * sg).astype(BF16)

    return _rowcall("b_ffn_down", body, n, tm,
                    [(d3, 'row'), (dn, 'row'), (gu, 'row'), (g_post, 'full'), (w_down, 'full')],
                    [('row', D, BF16), ('row', 2 * FFN_H, BF16), ('acc', (1, D), F32)])


def _b_ffn_up(dgu, d3, x2, w_gu, g_pre, tm):
    n = d3.shape[0]

    def body(dgu_ref, d3_ref, x_ref, w_ref, g_ref, dx_ref, dg_ref):
        @pl.when(pl.program_id(0) == 0)
        def _():
            dg_ref[...] = jnp.zeros_like(dg_ref)
        dhf = _dot_nt(dgu_ref[...], w_ref[...])
        dx, dg = _rms_bwd(x_ref[...], g_ref[...], dhf)
        dg_ref[...] += dg
        dx_ref[...] = d3_ref[...] + dx

    return _rowcall("b_ffn_up", body, n, tm,
                    [(dgu, 'row'), (d3, 'row'), (x2, 'row'), (w_gu, 'full'), (g_pre, 'full')],
                    [('row', D, F32), ('acc', (1, D), F32)])


def _b_attn(dx2, ao, x1, hc, g_post, w_o, w_q, g_pre, k, v, tm, tpb, mlen):
    n = dx2.shape[0]
    nb = n // (tm * tpb)
    kv_spec = pl.BlockSpec((mlen, D), lambda i: (i // tpb, 0))
    scale = HEAD_DIM ** -0.5

    def body(dx2_ref, ao_ref, x1_ref, hc_ref, go_ref, wo_ref, wq_ref, gp_ref, k_ref, v_ref,
             dao_ref, dq_ref, dx1_ref, dk_ref, dv_ref, dgo_ref, dgp_ref):
        i = pl.program_id(0)

        @pl.when(i == 0)
        def _():
            dgo_ref[...] = jnp.zeros_like(dgo_ref)
            dgp_ref[...] = jnp.zeros_like(dgp_ref)

        @pl.when(i % tpb == 0)
        def _():
            dk_ref[...] = jnp.zeros_like(dk_ref)
            dv_ref[...] = jnp.zeros_like(dv_ref)

        dx2v = dx2_ref[...]
        dao, dgo = _rms_bwd(ao_ref[...], go_ref[...], dx2v)
        dgo_ref[...] += dgo
        dao = dao.astype(BF16)
        dao_ref[...] = dao
        do = _dot_nt(dao, wo_ref[...])
        q = _dot(hc_ref[...], wq_ref[...])
        for h in range(HEADS):
            hs = slice(h * HEAD_DIM, (h + 1) * HEAD_DIM)
            qh = q[:, hs].astype(BF16)
            kh = k_ref[:, hs]
            p = _softmax(_dot_nt(qh, kh) * scale)
            doh = do[:, hs].astype(BF16)
            dp = _dot_nt(doh, v_ref[:, hs])
            ds = (p * (dp - jnp.sum(dp * p, axis=-1, keepdims=True)) * scale).astype(BF16)
            dq_ref[:, hs] = _dot(ds, kh).astype(BF16)
            dk_ref[:, hs] += _dot_tn(ds, qh)
            dv_ref[:, hs] += _dot_tn(p.astype(BF16), doh)
        dhc = _dot_nt(dq_ref[...], wq_ref[...])
        dx, dgp = _rms_bwd(x1_ref[...], gp_ref[...], dhc)
        dgp_ref[...] += dgp
        dx1_ref[...] = dx2v + dx

    return _rowcall("b_attn", body, n, tm,
                    [(dx2, 'row'), (ao, 'row'), (x1, 'row'), (hc, 'row'), (g_post, 'full'), (w_o, 'full'),
                     (w_q, 'full'), (g_pre, 'full'), (k, kv_spec), (v, kv_spec)],
                    [('row', D, BF16), ('row', D, BF16), ('row', D, F32),
                     (kv_spec, (nb * mlen, D), F32), (kv_spec, (nb * mlen, D), F32),
                     ('acc', (1, D), F32), ('acc', (1, D), F32)])


def _b_mem(dk, dv, mem, w_kv, g_mem, tm):
    n = mem.shape[0]

    def body(dk_ref, dv_ref, m_ref, w_ref, g_ref, dkv_ref, dg_ref):
        @pl.when(pl.program_id(0) == 0)
        def _():
            dg_ref[...] = jnp.zeros_like(dg_ref)
        dkb = dk_ref[...].astype(BF16)
        dvb = dv_ref[...].astype(BF16)
        dkv_ref[:, :D] = dkb
        dkv_ref[:, D:] = dvb
        dmn = _dot_nt(dkb, w_ref[:, :D]) + _dot_nt(dvb, w_ref[:, D:])
        _, dg = _rms_bwd(m_ref[...], g_ref[...], dmn)
        dg_ref[...] += dg

    return _rowcall("b_mem", body, n, tm, [(dk, 'row'), (dv, 'row'), (mem, 'row'), (w_kv, 'full'), (g_mem, 'full')],
                    [('row', 2 * D, BF16), ('acc', (1, D), F32)])


def _b_mix(dx1, mo, a, b, zg, ypre, g_post, w_mix_out, w_br_gm, w_br_s5, w_glu, tm):
    n = dx1.shape[0]

    def body(dx1_ref, mo_ref, a_ref, b_ref, zg_ref, ypre_ref, g_ref, wout_ref, wgm_ref, ws5_ref, wglu_ref,
             dmo_ref, da_ref, db_ref, dzg_ref, dygm_ref, dgate_ref, dypre_ref, dg_ref):
        @pl.when(pl.program_id(0) == 0)
        def _():
            dg_ref[...] = jnp.zeros_like(dg_ref)
        dmo, dg = _rms_bwd(mo_ref[...], g_ref[...], dx1_ref[...])
        dg_ref[...] += dg
        dmo = dmo.astype(BF16)
        dmo_ref[...] = dmo
        dmg = _dot_nt(dmo, wout_ref[...])
        zg = zg_ref[...]
        sa, sb = _sig(zg[:, :D]), _sig(zg[:, D:])
        da = (dmg * sa).astype(BF16)
        db = (dmg * sb).astype(BF16)
        da_ref[...] = da
        db_ref[...] = db
        dzg_ref[:, :D] = (dmg * a_ref[...].astype(F32) * sa * (1.0 - sa)).astype(BF16)
        dzg_ref[:, D:] = (dmg * b_ref[...].astype(F32) * sb * (1.0 - sb)).astype(BF16)
        dygm_ref[...] = _dot_nt(da, wgm_ref[...])
        dys5 = _dot_nt(db, ws5_ref[...])
        ypre = ypre_ref[...]
        yg = _gelu(ypre)
        sgt = _sig(_dot(yg.astype(BF16), wglu_ref[...]))
        dgate = (dys5 * yg * sgt * (1.0 - sgt)).astype(BF16)
        dgate_ref[...] = dgate
        dyg = dys5 * sgt + _dot_nt(dgate, wglu_ref[...])
        dypre_ref[...] = dyg * _gelu_grad(ypre)

    return _rowcall("b_mix", body, n, tm,
                    [(dx1, 'row'), (mo, 'row'), (a, 'row'), (b, 'row'), (zg, 'row'), (ypre, 'row'), (g_post, 'full'),
                     (w_mix_out, 'full'), (w_br_gm, 'full'), (w_br_s5, 'full'), (w_glu, 'full')],
                    [('row', D, BF16), ('row', D, BF16), ('row', D, BF16), ('row', 2 * D, BF16), ('row', D, F32),
                     ('row', S5_W, BF16), ('row', S5_W, F32), ('acc', (1, D), F32)])


def _b_s5(zs5, dypre, states, bbc, ccm, avec, dvec, nb, t):
    n = zs5.shape[0]
    nc = n // nb // t
    seg = t // 8
    slab, bb, cm, av, dv, st = _s5_specs(nb, nc, t, True)

    def body(u_ref, dy_ref, st_ref, bb_ref, cc_ref, a_ref, d_ref,
             du_ref, dbb_ref, dcc_ref, da_ref, dd_ref, lcarry, s_ref, l_ref):
        first = jnp.logical_and(pl.program_id(1) == 0, pl.program_id(2) == 0)

        @pl.when(first)
        def _():
            dbb_ref[...] = jnp.zeros_like(dbb_ref)
            dcc_ref[...] = jnp.zeros_like(dcc_ref)
            da_ref[...] = jnp.zeros_like(da_ref)
            dd_ref[...] = jnp.zeros_like(dd_ref)

        @pl.when(pl.program_id(2) == 0)
        def _():
            lcarry[...] = jnp.zeros_like(lcarry)

        up, dyp = _seg_load(u_ref, seg), _seg_load(dy_ref, seg)
        ub, dyb = up.astype(BF16), dyp.astype(BF16)
        a1r, a1i = a_ref[:, :512], a_ref[:, 512:]
        ar, ai = jnp.broadcast_to(a1r, (8, 512)), jnp.broadcast_to(a1i, (8, 512))
        pr, pi = _seg_power(a1r, a1i, seg)

        s_ref[...] = _dot(ub, bb_ref[...]).reshape(seg, 8, 1024)
        tr, ti = _seg_scan(s_ref, ar, ai, seg, False)
        er, ei, _ = _seg_entries(tr, ti, pr, pi, st_ref[...], False)

        def apply(k, pe):
            qr, qi = pe
            qr, qi = ar * qr - ai * qi, ar * qi + ai * qr
            s_ref[k, :, :512] = s_ref[k, :, :512] + qr
            s_ref[k, :, 512:] = s_ref[k, :, 512:] + qi
            return qr, qi

        lax.fori_loop(0, seg, apply, (er, ei), unroll=4)

        l_ref[...] = _dot_nt(dyb, cc_ref[...]).reshape(seg, 8, 1024)
        tr, ti = _seg_scan(l_ref, ar, -ai, seg, True)
        fr, fi, lout = _seg_entries(tr, ti, pr, -pi, lcarry[...], True)
        lcarry[...] = lout

        def apply_back(i, carry):
            qr, qi, accr, acci = carry
            k = seg - 1 - i
            qr, qi = ar * qr + ai * qi, ar * qi - ai * qr
            lr = l_ref[k, :, :512] + qr
            li = l_ref[k, :, 512:] + qi
            l_ref[k, :, :512] = lr
            l_ref[k, :, 512:] = li
            kp = jnp.maximum(k - 1, 0)
            sr = jnp.where(k == 0, er, s_ref[kp, :, :512])
            si = jnp.where(k == 0, ei, s_ref[kp, :, 512:])
            return qr, qi, accr + lr * sr + li * si, acci + li * sr - lr * si

        zero = jnp.zeros((8, 512), F32)
        _, _, accr, acci = lax.fori_loop(0, seg, apply_back, (fr, fi, zero, zero), unroll=4)
        da_ref[:, :512] += jnp.sum(accr, axis=0, keepdims=True)
        da_ref[:, 512:] += jnp.sum(acci, axis=0, keepdims=True)

        s = s_ref[...].reshape(t, 1024).astype(BF16)
        lam = l_ref[...].reshape(t, 1024).astype(BF16)
        dcc_ref[...] += _dot_tn(s, dyb)
        dbb_ref[...] += _dot_tn(ub, lam)
        _seg_store(du_ref, _dot_nt(lam, bb_ref[...]) + d_ref[...] * dyp, seg)
        dd_ref[...] += jnp.sum(dyp * up, axis=0, keepdims=True)

    return pl.pallas_call(
        body, name="b_s5", grid=(S5_SUPER, nb, nc), in_specs=[slab, slab, st, bb, cm, av, dv],
        out_specs=[slab, bb, cm, av, dv],
        out_shape=[jax.ShapeDtypeStruct((n, S5_W), F32), jax.ShapeDtypeStruct(bbc.shape, F32),
                   jax.ShapeDtypeStruct(ccm.shape, F32), jax.ShapeDtypeStruct(avec.shape, F32),
                   jax.ShapeDtypeStruct(dvec.shape, F32)],
        scratch_shapes=[pltpu.VMEM((1, 1024), F32), pltpu.VMEM((seg, 8, 1024), F32),
                        pltpu.VMEM((seg, 8, 1024), F32)],
        compiler_params=_params(("arbitrary", "arbitrary", "arbitrary")),
    )(zs5, dypre, states, bbc, ccm, avec, dvec)


def _b_gmlp(zgm, dygm, ln_g, ln_b, w_s, b_s, tm):
    n = zgm.shape[0]

    def body(z_ref, dy_ref, lg_ref, lb_ref, ws_ref, bs_ref, dz_ref, dws_ref, dbs_ref, dlg_ref, dlb_ref,
             du_s, dvn_s):
        @pl.when(pl.program_id(0) == 0)
        def _():
            dws_ref[...] = jnp.zeros_like(dws_ref)
            dbs_ref[...] = jnp.zeros_like(dbs_ref)
            dlg_ref[...] = jnp.zeros_like(dlg_ref)
            dlb_ref[...] = jnp.zeros_like(dlb_ref)
        z = z_ref[...]
        zg = _gelu(z)
        u = zg[:, :D]
        vh, r = _ln_stats(zg[:, D:])
        vn = (vh * lg_ref[...] + lb_ref[...]).astype(BF16)
        dy = dy_ref[...]
        keep = _tril()
        for g in range(GM_GROUPS):
            w = jnp.where(keep, ws_ref[g], 0.0).astype(BF16)
            cs = slice(g * LANE, (g + 1) * LANE)
            for c in range(tm // GM_CHUNK):
                rs = slice(c * GM_CHUNK, (c + 1) * GM_CHUNK)
                vb = vn[rs, cs]
                sv = _dot(w, vb) + bs_ref[g]
                dyb = dy[rs, cs]
                du_s[rs, cs] = dyb * sv
                dsv = dyb * u[rs, cs]
                dsvb = dsv.astype(BF16)
                dvn_s[rs, cs] = _dot_tn(w, dsvb)
                dws_ref[g] += jnp.where(keep, _dot_nt(dsvb, vb), 0.0)
                dbs_ref[g] += jnp.sum(dsv, axis=1, keepdims=True)
        dvn = dvn_s[...]
        dlg_ref[...] += jnp.sum(dvn * vh, axis=0, keepdims=True)
        dlb_ref[...] += jnp.sum(dvn, axis=0, keepdims=True)
        dvh = dvn * lg_ref[...]
        dv = r * (dvh - jnp.mean(dvh, axis=-1, keepdims=True) - vh * jnp.mean(dvh * vh, axis=-1, keepdims=True))
        dz_ref[:, :D] = (du_s[...] * _gelu_grad(z[:, :D])).astype(BF16)
        dz_ref[:, D:] = (dv * _gelu_grad(z[:, D:])).astype(BF16)

    return _rowcall("b_gmlp", body, n, tm,
                    [(zgm, 'row'), (dygm, 'row'), (ln_g, 'full'), (ln_b, 'full'), (w_s, 'full'), (b_s, 'full')],
                    [('row', 2 * D, BF16), ('acc', w_s.shape, F32), ('acc', b_s.shape, F32), ('acc', (1, D), F32),
                     ('acc', (1, D), F32)],
                    scratch=[pltpu.VMEM((tm, D), F32), pltpu.VMEM((tm, D), F32)])


def _b_in(dzgm, dzs5, dzg, dx1, x, w_in, g_pre, tm):
    n = x.shape[0]

    def body(d1_ref, d2_ref, d3_ref, dx1_ref, x_ref, w_ref, g_ref, gx_ref, dg_ref):
        @pl.when(pl.program_id(0) == 0)
        def _():
            dg_ref[...] = jnp.zeros_like(dg_ref)
        dh = (_dot_nt(d1_ref[...], w_ref[:, 0:2 * D]) + _dot_nt(d2_ref[...].astype(BF16), w_ref[:, 2 * D:2 * D + S5_W])
              + _dot_nt(d3_ref[...], w_ref[:, 2 * D + S5_W:]))
        dx, dg = _rms_bwd(x_ref[...], g_ref[...], dh)
        dg_ref[...] += dg
        gx_ref[...] = dx1_ref[...] + dx

    return _rowcall("b_in", body, n, tm,
                    [(dzgm, 'row'), (dzs5, 'row'), (dzg, 'row'), (dx1, 'row'), (x, 'row'), (w_in, 'full'),
                     (g_pre, 'full')],
                    [('row', D, F32), ('acc', (1, D), F32)])


def _whole(name, body, ins, outs):
    return pl.pallas_call(body, name=name, out_shape=[jax.ShapeDtypeStruct(s, dt) for s, dt in outs],
                          compiler_params=_params())(*ins)


def _s5_disc_fwd(lr, li, ls, br, bi):
    def body(lr_ref, li_ref, ls_ref, br_ref, bi_ref, o1, o2, o3, o4):
        outs = _s5_disc(lr_ref[...], li_ref[...], ls_ref[...], br_ref[...], bi_ref[...])
        for o, val in zip((o1, o2, o3, o4), outs):
            o[...] = val

    return _whole("s5_disc_fwd", body, [lr, li, ls, br, bi],
                  [(lr.shape, F32), (lr.shape, F32), (br.shape, F32), (br.shape, F32)])


def _s5_disc_bwd(lr, li, ls, br, bi, cts):
    def body(lr_ref, li_ref, ls_ref, br_ref, bi_ref, c1, c2, c3, c4, o1, o2, o3, o4, o5):
        _, vjp = jax.vjp(_s5_disc, lr_ref[...], li_ref[...], ls_ref[...], br_ref[...], bi_ref[...])
        grads = vjp((c1[...], c2[...], c3[...], c4[...]))
        for o, val in zip((o1, o2, o3, o4, o5), grads):
            o[...] = val

    return _whole("s5_disc_bwd", body, [lr, li, ls, br, bi, *cts],
                  [(lr.shape, F32), (lr.shape, F32), (lr.shape, F32), (br.shape, F32), (br.shape, F32)])


def _sum_rows(name, parts, tm):
    def body(*refs):
        acc = refs[0][...]
        for r in refs[1:-1]:
            acc = acc + r[...]
        refs[-1][...] = acc

    return _rowcall(name, body, parts[0].shape[0], tm, [(p, 'row') for p in parts],
                    [('row', parts[0].shape[1], F32)])[0]


def _adamw(name, w, g, m, v, tm):
    c1 = 1.0 - ADAM_B1 ** ADAM_STEP
    c2 = 1.0 - ADAM_B2 ** ADAM_STEP

    def body(w_ref, g_ref, m_ref, v_ref, d_ref, mo_ref, vo_ref):
        gg = g_ref[...]
        mn = ADAM_B1 * m_ref[...] + (1.0 - ADAM_B1) * gg
        vn = ADAM_B2 * v_ref[...] + (1.0 - ADAM_B2) * (gg * gg)
        mo_ref[...] = mn
        vo_ref[...] = vn
        d_ref[...] = -ADAM_LR * ((mn / c1) / (jnp.sqrt(vn / c2) + ADAM_EPS) + ADAM_WD * w_ref[...])

    cols = w.shape[1]
    return _rowcall(name, body, w.shape[0], tm, [(w, 'row'), (g, 'row'), (m, 'row'), (v, 'row')],
                    [('row', cols, F32), ('row', cols, F32), ('row', cols, F32)])


def _picked_rowcall(name, body, grid, in_specs, out_cols, tm):
    return pl.pallas_call(
        lambda p_ref, *refs: body(*refs), name=name,
        grid_spec=pltpu.PrefetchScalarGridSpec(
            num_scalar_prefetch=1, grid=(grid,), in_specs=in_specs,
            out_specs=[pl.BlockSpec((tm, D), lambda i, p: (i, 0)) for _ in out_cols]),
        out_shape=[jax.ShapeDtypeStruct((grid * tm, D), dt) for dt in out_cols],
        compiler_params=_params(("arbitrary",)),
    )


def _reduce_big(slabs):
    rows = [s.shape[1] for s in slabs]
    blk = sum(rows) // 4
    tm = _tile_rows(blk, 512)
    per = blk // tm
    gpack = jnp.concatenate([s.reshape(4, 2, 2, r // 4, D) for s, r in zip(slabs, rows)], axis=3)

    def rdma(ss, rs, k, src, dst, peer):
        return pltpu.make_async_remote_copy(src_ref=src, dst_ref=dst, send_sem=ss.at[k], recv_sem=rs.at[k],
                                            device_id=peer, device_id_type=MESH)

    mx, my, mc = lax.axis_index("x"), lax.axis_index("y"), lax.axis_index("c")
    chip_me, chip_xn, chip_yn = 2 * mx + my, 2 * (1 - mx) + my, 2 * mx + (1 - my)

    def pair_body(srcs, dsts, ss, rs):
        c, _, (_, _, psib) = _place()
        cp = rdma(ss, rs, 0, srcs[0].at[:, 1 - c], dsts[0], psib)
        cp.start()
        cp.wait()

    from_sib = _comm_call("reduce_pair", [gpack], [((4, 2, blk, D), F32)], 1, pair_body)[0]

    def sum_both(a_ref, b_ref, o32_ref, o16_ref):
        s = a_ref[...].astype(F32) + b_ref[...].astype(F32)
        o32_ref[...] = s
        o16_ref[...] = s.astype(BF16)

    p32, p16 = _picked_rowcall(
        "sum_pair", sum_both, 8 * per,
        [pl.BlockSpec((None, None, None, tm, D), lambda i, p: (i // (2 * per), p[0], (i // per) % 2, i % per, 0)),
         pl.BlockSpec((tm, D), lambda i, p: (i, 0))],
        [F32, BF16], tm)(jnp.stack([mc]).astype(jnp.int32), gpack, from_sib.reshape(8 * blk, D))
    p32, p16 = p32.reshape(4, 2, blk, D), p16.reshape(4, 2, blk, D)

    def step1_body(srcs, dsts, ss, rs):
        _, (me, xn, yn, dg), (px, py, _) = _place()
        s16 = srcs[0]
        cps = [rdma(ss, rs, 0, s16.at[yn, 0], dsts[0].at[0], py), rdma(ss, rs, 1, s16.at[dg, 0], dsts[0].at[1], py),
               rdma(ss, rs, 2, s16.at[xn, 1], dsts[0].at[2], px), rdma(ss, rs, 3, s16.at[dg, 1], dsts[0].at[3], px)]
        for cp in cps:
            cp.start()
        for cp in cps:
            cp.wait()

    recv1 = _comm_call("reduce_step1", [p16], [((4, blk, D), BF16)], 4, step1_body)[0]

    s32, s16 = _picked_rowcall(
        "sum_step1", sum_both, 4 * per,
        [pl.BlockSpec((None, None, tm, D), lambda i, p: (p[i // per], i // (2 * per), i % per, 0)),
         pl.BlockSpec((tm, D), lambda i, p: (i, 0))],
        [F32, BF16], tm)(jnp.stack([chip_me, chip_xn, chip_me, chip_yn]).astype(jnp.int32), p32,
                         recv1.reshape(4 * blk, D))

    def step2_body(srcs, dsts, ss, rs):
        _, _, (px, py, _) = _place()
        cps = [rdma(ss, rs, 0, srcs[0].at[1], dsts[0].at[0], px), rdma(ss, rs, 1, srcs[0].at[3], dsts[0].at[1], py)]
        for cp in cps:
            cp.start()
        for cp in cps:
            cp.wait()

    recv2 = _comm_call("reduce_step2", [s16.reshape(4, blk, D)], [((2, blk, D), BF16)], 2, step2_body)[0]

    def sum_step2(a_ref, b_ref, o_ref):
        o_ref[...] = a_ref[...] + b_ref[...].astype(F32)

    red = _rowcall("sum_step2", sum_step2, 2 * blk, tm,
                   [(s32.reshape(4, blk, D), pl.BlockSpec((None, tm, D), lambda i: (2 * (i // per), i % per, 0))),
                    (recv2.reshape(2 * blk, D), 'row')],
                   [('row', D, F32)])[0].reshape(2, blk, D)

    def share_body(srcs, dsts, ss, rs):
        _, _, (_, _, psib) = _place()
        cp = rdma(ss, rs, 0, srcs[0], dsts[0], psib)
        cp.start()
        cp.wait()

    other = _comm_call("share_pair", [red], [((2, blk, D), F32)], 1, share_body)[0]
    lo = jnp.where(mc == 0, red, other)
    hi = jnp.where(mc == 0, other, red)
    out, off = [], 0
    for r in rows:
        q = r // 4
        out.append(jnp.concatenate([lo[0, off:off + q], lo[1, off:off + q], hi[0, off:off + q], hi[1, off:off + q]]))
        off += q
    return out


def _rows1024(a):
    flat = a.reshape(-1)
    pad = (-flat.shape[0]) % D
    if pad:
        flat = jnp.concatenate([flat, jnp.zeros((pad,), flat.dtype)])
    return flat.reshape(-1, D)


def _pack(arrs, pad_rows_to=8):
    parts = [_rows1024(a) for a in arrs]
    rows = sum(p.shape[0] for p in parts)
    pad = (-rows) % pad_rows_to
    if pad:
        parts.append(jnp.zeros((pad, D), parts[0].dtype))
    return jnp.concatenate(parts, axis=0)


def _unpack(packed, shapes):
    out, r = [], 0
    for s in shapes:
        size = math.prod(s)
        nr = -(-size // D)
        out.append(packed[r:r + nr].reshape(-1)[:size].reshape(s))
        r += nr
    return out


def _shard_slabs(full, axis):
    r, c = full.shape
    if axis == 0:
        return full.reshape(4, -1, D)
    return full.reshape(r, 4, c // 4).transpose(1, 0, 2).reshape(4, -1, D)


def _block_diag(t):
    eye = jnp.eye(8, dtype=t.dtype)
    j, g, a, b = t.shape
    return (t[:, :, :, None, :] * eye[None, :, None, :, None]).reshape(j, g * a, g * b)


def _block_diag_take(m, a, b):
    eye = jnp.eye(8, dtype=m.dtype)
    return (m.reshape(4, 8, a, 8, b) * eye[None, :, None, :, None]).sum(axis=3)


def kernel(x, mem, g_mix_pre, w_in, gm_ln_g, gm_ln_b, gm_w_s, gm_b_s, s5_lam_re, s5_lam_im, s5_log_step, s5_b_re, s5_b_im, s5_c_re, s5_c_im, s5_d, s5_w_glu, w_br_gm, w_br_s5, w_mix_out, g_mix_post, g_ca_pre, g_mem, ca_w_q, ca_w_kv, ca_w_o, g_ca_post, g_ffn_pre, ffn_w_gu, ffn_w_down, g_ffn_post, loss_target, m_g_mix_pre, m_w_in, m_gm_ln_g, m_gm_ln_b, m_gm_w_s, m_gm_b_s, m_s5_lam_re, m_s5_lam_im, m_s5_log_step, m_s5_b_re, m_s5_b_im, m_s5_c_re, m_s5_c_im, m_s5_d, m_s5_w_glu, m_w_br_gm, m_w_br_s5, m_w_mix_out, m_g_mix_post, m_g_ca_pre, m_g_mem, m_ca_w_q, m_ca_w_kv, m_ca_w_o, m_g_ca_post, m_g_ffn_pre, m_ffn_w_gu, m_ffn_w_down, m_g_ffn_post, v_g_mix_pre, v_w_in, v_gm_ln_g, v_gm_ln_b, v_gm_w_s, v_gm_b_s, v_s5_lam_re, v_s5_lam_im, v_s5_log_step, v_s5_b_re, v_s5_b_im, v_s5_c_re, v_s5_c_im, v_s5_d, v_s5_w_glu, v_w_br_gm, v_w_br_s5, v_w_mix_out, v_g_mix_post, v_g_ca_pre, v_g_mem, v_ca_w_q, v_ca_w_kv, v_ca_w_o, v_g_ca_post, v_g_ffn_pre, v_ffn_w_gu, v_ffn_w_down, v_g_ffn_post):
    a = dict(locals())
    w = {n: a[n][0] for n in WNAMES}
    nb, seq, _ = x.shape
    n = nb * seq
    mlen = mem.shape[1]
    tm = min(256, seq)
    tpb = seq // tm
    xf = x.reshape(n, D)
    tgt = loss_target.reshape(n, D)
    memf = mem.reshape(nb * mlen, D)
    tmm = min(256, mlen)

    big_names = list(BIG)
    local_shapes = {k: w[k].shape for k in big_names}
    axes = [BIG[k] for k in big_names]
    gathered = _gather_weights([w[k] for k in big_names], axes)
    wf = {k: (g.reshape(-1, g.shape[-1]) if ax == 0 else g) for k, g, ax in zip(big_names, gathered, axes)}

    vec = lambda name: w[name].reshape(1, D)

    to_lane = lambda p: p.reshape(1, -1)
    b_t = lambda p: p.transpose(2, 0, 1).reshape(16, -1)
    lr_l, li_l = to_lane(w['s5_lam_re']), to_lane(w['s5_lam_im'])
    ls_l = jnp.repeat(w['s5_log_step'], 64).reshape(1, -1)
    br_t, bi_t = b_t(w['s5_b_re']), b_t(w['s5_b_im'])
    ab_re, ab_im, bb_re, bb_im = _s5_disc_fwd(lr_l, li_l, ls_l, br_t, bi_t)
    blk = lambda t: _block_diag(t.reshape(16, 4, 8, 64).transpose(1, 2, 0, 3))
    bbc = jnp.concatenate([blk(bb_re), blk(bb_im)], axis=2).astype(BF16)
    cblk = lambda c: _block_diag(c.reshape(4, 8, 16, 64).transpose(0, 1, 3, 2))
    ccm = jnp.concatenate([cblk(w['s5_c_re']), -cblk(w['s5_c_im'])], axis=1).astype(BF16)
    avec = jnp.concatenate([ab_re.reshape(4, 1, 512), ab_im.reshape(4, 1, 512)], axis=2)
    dvec = w['s5_d'].reshape(4, 1, LANE)

    bs3 = w['gm_b_s'].reshape(GM_GROUPS, GM_CHUNK, 1)
    h, zgm, zs5, zg = _f_in(xf, vec('g_mix_pre'), wf['w_in'], tm)
    ygm = _f_gmlp(zgm, vec('gm_ln_g'), vec('gm_ln_b'), w['gm_w_s'], bs3, tm)
    s5_t = min(S5_T, seq)
    ypre, states = _f_s5(zs5, bbc, ccm, avec, dvec, nb, s5_t)
    yg, ys5, a_br, b_br, merged, mo, x1 = _f_mix(ygm, ypre, zg, xf, wf['s5_w_glu'], wf['w_br_gm'], wf['w_br_s5'],
                                                 wf['w_mix_out'], vec('g_mix_post'), tm)
    mem_n, kk, vv = _f_mem(memf, vec('g_mem'), wf['ca_w_kv'], tmm)
    hc, o_att, ao, x2 = _f_attn(x1, vec('g_ca_pre'), wf['ca_w_q'], kk, vv, wf['ca_w_o'], vec('g_ca_post'),
                                tm, tpb, mlen)
    hf, gu, act = _f_ffn_up(x2, vec('g_ffn_pre'), wf['ffn_w_gu'], tm)
    dn, d3, loss_cols = _f_ffn_down(act, x2, tgt, wf['ffn_w_down'], vec('g_ffn_post'), tm)

    gsm = {}
    gbig = {}
    ddn, dgu, gsm['g_ffn_post'] = _b_ffn_down(d3, dn, gu, vec('g_ffn_post'), wf['ffn_w_down'], tm)
    dx2, gsm['g_ffn_pre'] = _b_ffn_up(dgu, d3, x2, wf['ffn_w_gu'], vec('g_ffn_pre'), tm)
    gbig['ffn_w_down'] = _mm_tn("dw_ffn_down", act, ddn)
    gbig['ffn_w_gu'] = _mm_tn("dw_ffn_gu", hf, dgu, col_shards=4)
    dao, dq, dx1, dk, dv, gsm['g_ca_post'], gsm['g_ca_pre'] = _b_attn(
        dx2, ao, x1, hc, vec('g_ca_post'), wf['ca_w_o'], wf['ca_w_q'], vec('g_ca_pre'), kk, vv, tm, tpb, mlen)
    gbig['ca_w_o'] = _mm_tn("dw_ca_o", o_att, dao)
    gbig['ca_w_q'] = _mm_tn("dw_ca_q", hc, dq)
    dkv, gsm['g_mem'] = _b_mem(dk, dv, memf, wf['ca_w_kv'], vec('g_mem'), tmm)
    gbig['ca_w_kv'] = _mm_tn("dw_ca_kv", mem_n, dkv, col_shards=4)
    dmo, da_br, db_br, dzg, dygm, dgate, dypre, gsm['g_mix_post'] = _b_mix(
        dx1, mo, a_br, b_br, zg, ypre, vec('g_mix_post'), wf['w_mix_out'], wf['w_br_gm'], wf['w_br_s5'],
        wf['s5_w_glu'], tm)
    gbig['w_mix_out'] = _mm_tn("dw_mix_out", merged, dmo)
    gbig['w_br_gm'] = _mm_tn("dw_br_gm", ygm, da_br)
    gbig['w_br_s5'] = _mm_tn("dw_br_s5", ys5, db_br, col_shards=4)
    gbig['s5_w_glu'] = _mm_tn("dw_s5_glu", yg, dgate)
    dzs5, dbbc, dccm, davec, ddvec = _b_s5(zs5, dypre, states, bbc, ccm, avec, dvec, nb, s5_t)
    dzgm, gsm['gm_w_s'], dbs3, gsm['gm_ln_g'], gsm['gm_ln_b'] = _b_gmlp(
        zgm, dygm, vec('gm_ln_g'), vec('gm_ln_b'), w['gm_w_s'], bs3, tm)
    gsm['gm_b_s'] = dbs3
    grad_x, gsm['g_mix_pre'] = _b_in(dzgm, dzs5, dzg, dx1, xf, wf['w_in'], vec('g_mix_pre'), tm)
    gbig['w_in'] = jnp.concatenate(
        [_mm_tn("dw_in_gm", h, dzgm), _mm_tn("dw_in_s5", h, dzs5), _mm_tn("dw_in_g", h, dzg)], axis=1)

    unblk = lambda m_: _block_diag_take(m_, 16, 64).transpose(2, 0, 1, 3).reshape(16, -1)
    d_bb_re, d_bb_im = unblk(dbbc[:, :, :512]), unblk(dbbc[:, :, 512:])
    cunblk = lambda m_: _block_diag_take(m_, 64, 16).transpose(0, 1, 3, 2).reshape(32, 16, 64)
    gsm['s5_c_re'] = cunblk(dccm[:, :512, :])
    gsm['s5_c_im'] = -cunblk(dccm[:, 512:, :])
    d_ab_re, d_ab_im = davec[:, :, :512].reshape(1, -1), davec[:, :, 512:].reshape(1, -1)
    g_lr, g_li, g_ls, g_br, g_bi = _s5_disc_bwd(lr_l, li_l, ls_l, br_t, bi_t, (d_ab_re, d_ab_im, d_bb_re, d_bb_im))
    gsm['s5_lam_re'], gsm['s5_lam_im'] = g_lr.reshape(32, 64), g_li.reshape(32, 64)
    gsm['s5_log_step'] = g_ls.reshape(32, 64).sum(axis=1)
    from_t = lambda t: t.reshape(16, 32, 64).transpose(1, 2, 0)
    gsm['s5_b_re'], gsm['s5_b_im'] = from_t(g_br), from_t(g_bi)
    gsm['s5_d'] = ddvec.reshape(32, 16)

    small_shapes = [w[k].shape for k in SMALL]
    spack = _pack([gsm[k].reshape(w[k].shape) for k in SMALL] + [loss_cols], 8)
    rs = spack.shape[0]
    me_slot = 4 * lax.axis_index("x") + 2 * lax.axis_index("y") + lax.axis_index("c")
    sall = lax.dynamic_update_index_in_dim(_gather_all("gather_small", spack), spack, me_slot, 0)
    ssum = _sum_rows("sum_small", [sall[i] for i in range(8)], rs)
    small_red = _unpack(ssum, small_shapes + [(1, D)])
    loss = 0.5 * jnp.sum(small_red[-1]) / D

    slabs = [gbig[k].reshape(4, -1, D) if k != 'w_in' else _shard_slabs(gbig[k], 1) for k in big_names]
    greds = _reduce_big(slabs)

    res_big = {}
    for k, g in zip(big_names, greds):
        rows_w = g.shape[0]
        flat = lambda t: t.reshape(rows_w, D)
        dl, mn, vn = _adamw("adamw_" + k, flat(w[k]), g, flat(a['m_' + k][0]), flat(a['v_' + k][0]),
                            _tile_rows(rows_w))
        res_big[k] = tuple(t.reshape(local_shapes[k]) for t in (g, dl, mn, vn))
    pad_loss = [jnp.zeros((1, D), F32)]
    wsm = _pack([w[k] for k in SMALL] + pad_loss, 8)
    msm = _pack([a['m_' + k][0] for k in SMALL] + pad_loss, 8)
    vsm = _pack([a['v_' + k][0] for k in SMALL] + pad_loss, 8)
    dl_s, mn_s, vn_s = _adamw("adamw_small", wsm, ssum, msm, vsm, rs)
    res_small = {k: vals for k, vals in zip(SMALL, zip(*[_unpack(p, small_shapes) for p in (ssum, dl_s, mn_s, vn_s)]))}

    res = {**res_big, **res_small}
    outs = [loss, grad_x.reshape(nb, seq, D)]
    for i in range(4):
        outs += [res[k][i][None] for k in WNAMES]
    return tuple(outs)


def _tile_rows(rows, cap=384):
    best = rows
    for t in range(16, min(rows, cap) + 1, 16):
        if rows % t == 0:
            best = t
    return best
```

```python
import math
from typing import Any, NamedTuple

import jax
import jax.numpy as jnp
from jax import lax
from jax.experimental import pallas as pl
from jax.experimental.pallas import tpu as pltpu

F32 = jnp.float32
BF16 = jnp.bfloat16
EPS = 1e-6
D = 1024
GM_CHUNK = 128
GM_GROUPS = 8
S5_W = 512
S5_SUPER = 4
S5_T = 512
HEADS = 4
HEAD_DIM = 256
FFN_H = 2816
LANE = 128
VMEM_LIMIT = 56 * 1024 * 1024
MESH = pl.DeviceIdType.MESH

ADAM_LR, ADAM_B1, ADAM_B2, ADAM_EPS, ADAM_WD, ADAM_STEP = 0.001, 0.9, 0.999, 1e-08, 0.01, 10

WNAMES = ['g_mix_pre', 'w_in', 'gm_ln_g', 'gm_ln_b', 'gm_w_s', 'gm_b_s', 's5_lam_re', 's5_lam_im', 's5_log_step',
          's5_b_re', 's5_b_im', 's5_c_re', 's5_c_im', 's5_d', 's5_w_glu', 'w_br_gm', 'w_br_s5', 'w_mix_out',
          'g_mix_post', 'g_ca_pre', 'g_mem', 'ca_w_q', 'ca_w_kv', 'ca_w_o', 'g_ca_post', 'g_ffn_pre', 'ffn_w_gu',
          'ffn_w_down', 'g_ffn_post']
BIG = {'w_in': 1, 's5_w_glu': 0, 'w_br_gm': 0, 'w_br_s5': 1, 'w_mix_out': 0, 'ca_w_q': 0, 'ca_w_kv': 1,
       'ca_w_o': 0, 'ffn_w_gu': 1, 'ffn_w_down': 0}
SMALL = [n for n in WNAMES if n not in BIG]


def _dot(a, b):
    return jnp.dot(a, b, preferred_element_type=F32)


def _dot_nt(a, b):
    return lax.dot_general(a, b, (((1,), (1,)), ((), ())), preferred_element_type=F32)


def _dot_tn(a, b):
    return lax.dot_general(a, b, (((0,), (0,)), ((), ())), preferred_element_type=F32)


def _rms_fwd(x, g):
    r = lax.rsqrt(jnp.mean(x * x, axis=-1, keepdims=True) + EPS)
    return x * r * g


def _rms_bwd(x, g, dy):
    r = lax.rsqrt(jnp.mean(x * x, axis=-1, keepdims=True) + EPS)
    xh = x * r
    gdy = dy * g
    dx = r * (gdy - xh * jnp.mean(gdy * xh, axis=-1, keepdims=True))
    dg = jnp.sum(dy * xh, axis=0, keepdims=True)
    return dx, dg


_GC = math.sqrt(2.0 / math.pi)


def _gelu(x):
    return 0.5 * x * (1.0 + jnp.tanh(_GC * (x + 0.044715 * x * x * x)))


def _gelu_grad(x):
    t = jnp.tanh(_GC * (x + 0.044715 * x * x * x))
    return 0.5 * (1.0 + t) + 0.5 * x * (1.0 - t * t) * _GC * (1.0 + 3 * 0.044715 * x * x)


def _sig(x):
    return 1.0 / (1.0 + jnp.exp(-x))


def _cscan(br, bi, ar, ai, reverse):
    t = br.shape[0]
    row = lax.broadcasted_iota(jnp.int32, br.shape, 0)
    pr, pi = ar, ai
    sh = 1
    while sh < t:
        if reverse:
            keep = row < t - sh
            rr, ri = pltpu.roll(br, t - sh, 0), pltpu.roll(bi, t - sh, 0)
        else:
            keep = row >= sh
            rr, ri = pltpu.roll(br, sh, 0), pltpu.roll(bi, sh, 0)
        rr = jnp.where(keep, rr, 0.0)
        ri = jnp.where(keep, ri, 0.0)
        br, bi = br + pr * rr - pi * ri, bi + pr * ri + pi * rr
        pr, pi = pr * pr - pi * pi, 2.0 * pr * pi
        sh *= 2
    return br, bi


def _s5_disc(lr, li, ls, br, bi):
    step = jnp.exp(ls)
    mag = jnp.exp(lr * step)
    ab_re = mag * jnp.cos(li * step)
    ab_im = mag * jnp.sin(li * step)
    den = lr * lr + li * li
    nr = ab_re - 1.0
    co_re = (nr * lr + ab_im * li) / den
    co_im = (ab_im * lr - nr * li) / den
    return ab_re, ab_im, co_re * br - co_im * bi, co_re * bi + co_im * br


def _params(sem=None):
    return pltpu.CompilerParams(dimension_semantics=sem, vmem_limit_bytes=VMEM_LIMIT)


def _rowcall(name, body, n_rows, tm, ins, outs, scratch=()):
    def spec(kind, shape):
        if kind == 'row':
            return pl.BlockSpec((tm, shape[1]), lambda i: (i, 0))
        if kind in ('full', 'acc'):
            nd = len(shape)
            return pl.BlockSpec(tuple(shape), lambda i: (0,) * nd)
        return kind

    in_specs = [spec(k, a.shape) for a, k in ins]
    out_shape, out_specs = [], []
    for k, s, dt in outs:
        shape = (n_rows, s) if k == 'row' else tuple(s)
        out_shape.append(jax.ShapeDtypeStruct(shape, dt))
        out_specs.append(spec(k, shape))
    return pl.pallas_call(
        body, name=name, grid=(n_rows // tm,), in_specs=in_specs, out_specs=out_specs, out_shape=out_shape,
        scratch_shapes=list(scratch), compiler_params=_params(("arbitrary",)),
    )(*[a for a, _ in ins])


def _tile(n, cap):
    if n <= cap:
        return n
    best = LANE
    for k in range(1, n // LANE + 1):
        t = k * LANE
        if n % t == 0 and t <= cap:
            best = t
    return best


def _mm_tn(name, a, b, col_shards=1, comm=None):
    n, k = a.shape
    m = b.shape[1]
    mloc = m // col_shards
    tk, tn, tr = _tile(k, 1536), _tile(mloc, 1536), min(n, 512)
    per = mloc // tn

    def body(a_ref, b_ref, o_ref):
        @pl.when(pl.program_id(2) == 0)
        def _():
            o_ref[...] = jnp.zeros_like(o_ref)
        o_ref[...] += _dot_tn(a_ref[...].astype(BF16), b_ref[...].astype(BF16))

    if col_shards == 1:
        out_spec = pl.BlockSpec((tk, tn), lambda i, j, r: (i, j))
        out_shape = (k, m)
    else:
        out_spec = pl.BlockSpec((None, tk, tn), lambda i, j, r: (j // per, i, j % per))
        out_shape = (col_shards, k, mloc)
    grid = (k // tk, m // tn, n // tr)
    in_specs = [pl.BlockSpec((tr, tk), lambda i, j, r: (r, i)), pl.BlockSpec((tr, tn), lambda i, j, r: (r, j))]
    if comm is not None:
        (res,), extra = _hosting_call(body, name, grid, in_specs, [out_spec], [jax.ShapeDtypeStruct(out_shape, F32)],
                                      [], ("arbitrary", "arbitrary", "arbitrary"), (a, b), comm)
        return res, extra
    return pl.pallas_call(
        body, name=name, grid=grid, in_specs=in_specs, out_specs=out_spec,
        out_shape=jax.ShapeDtypeStruct(out_shape, F32),
        compiler_params=_params(("parallel", "parallel", "arbitrary")),
    )(a, b)


def _comm_call(name, srcs, outs, nsem, body_fn, aliases=None):
    ns, no = len(srcs), len(outs)

    def body(*refs):
        body_fn(refs[:ns], refs[ns:ns + no], refs[ns + no], refs[ns + no + 1])

    hbm = pl.BlockSpec(memory_space=pltpu.HBM)
    return pl.pallas_call(
        body, name=name, in_specs=[hbm] * ns, out_specs=[hbm] * no,
        out_shape=[jax.ShapeDtypeStruct(s, d) for s, d in outs],
        scratch_shapes=[pltpu.SemaphoreType.DMA((nsem,)), pltpu.SemaphoreType.DMA((nsem,))],
        input_output_aliases=aliases or {},
    )(*srcs)


def _place():
    x, y, c = lax.axis_index("x"), lax.axis_index("y"), lax.axis_index("c")
    chips = (2 * x + y, 2 * (1 - x) + y, 2 * x + (1 - y), 2 * (1 - x) + (1 - y))
    peers = ((1 - x, y, c), (x, 1 - y, c), (x, y, 1 - c))
    return c, chips, peers


class _Comm(NamedTuple):
    srcs: Any
    outs: Any
    nsem: int
    stages: Any
    aliases: Any
    fracs: Any


class _Lazy:
    def __init__(self, make):
        self._make = make

    def start(self):
        self._make().start()

    def wait(self):
        self._make().wait()

    def wait_recv(self):
        self._make().wait_recv()

    def wait_send(self):
        self._make().wait_send()


def _run_comm(name, comm):
    def body_fn(srcs, dsts, ss, rs):
        for stage in comm.stages:
            stage(srcs, dsts, ss, rs)

    return _comm_call(name, comm.srcs, comm.outs, comm.nsem, body_fn, aliases=comm.aliases)


def _hosting_call(body, name, grid, in_specs, out_specs, out_shape, scratch, sem, args, comm):
    hbm = pl.BlockSpec(memory_space=pltpu.HBM)
    n_in, n_out, n_sc = len(in_specs), len(out_specs), len(scratch)
    ns, no = len(comm.srcs), len(comm.outs)
    total = math.prod(grid)
    steps = [min(total - 1, int(f * total)) for f in comm.fracs]

    def wrapped(*refs):
        ins, csrc = refs[:n_in], refs[n_in:n_in + ns]
        outs, cdst = refs[n_in + ns:n_in + ns + n_out], refs[n_in + ns + n_out:n_in + ns + n_out + no]
        rest = refs[n_in + ns + n_out + no:]
        lin = 0
        for d, size in enumerate(grid):
            lin = lin * size + pl.program_id(d)
        for stage, at in zip(comm.stages, steps):
            @pl.when(lin == at)
            def _(stage=stage):
                stage(csrc, cdst, rest[n_sc], rest[n_sc + 1])
        body(*ins, *outs, *rest[:n_sc])

    res = pl.pallas_call(
        wrapped, name=name, grid=grid, in_specs=list(in_specs) + [hbm] * ns, out_specs=list(out_specs) + [hbm] * no,
        out_shape=list(out_shape) + [jax.ShapeDtypeStruct(s, d) for s, d in comm.outs],
        scratch_shapes=list(scratch) + [pltpu.SemaphoreType.DMA((comm.nsem,)), pltpu.SemaphoreType.DMA((comm.nsem,))],
        input_output_aliases={n_in + i: n_out + o for i, o in comm.aliases.items()},
        compiler_params=_params(sem),
    )(*args, *comm.srcs)
    return res[:n_out], res[n_out:]


def _gather_weights(name, shards, axes):
    nw = len(shards)
    shapes = [s.shape for s in shards]
    outs = [((4,) + s if ax == 0 else (s[0], 4 * s[1]), BF16) for s, ax in zip(shapes, axes)]

    def win(ref, i, chip, start, rows):
        r, cols = shapes[i]
        if axes[i] == 0:
            return ref.at[chip, pl.ds(start, rows), :]
        return ref.at[pl.ds(start, rows), pl.ds(chip * cols, cols)]

    def place_body(*refs):
        srcs, dsts, bufs, sem = refs[:nw], refs[nw:2 * nw], refs[2 * nw:3 * nw], refs[3 * nw]
        me = 2 * lax.axis_index("x") + lax.axis_index("y")
        cps = []
        for i in range(nw):
            bufs[i][...] = srcs[i][...].astype(BF16)
            cps.append(pltpu.make_async_copy(bufs[i], win(dsts[i], i, me, 0, shapes[i][0]), sem.at[i]))
            cps[-1].start()
        for cp in cps:
            cp.wait()

    placed = pl.pallas_call(
        place_body, name=name, out_shape=[jax.ShapeDtypeStruct(s, d) for s, d in outs],
        in_specs=[pl.BlockSpec(memory_space=pltpu.VMEM)] * nw, out_specs=[pl.BlockSpec(memory_space=pltpu.HBM)] * nw,
        scratch_shapes=[pltpu.VMEM(s, BF16) for s in shapes] + [pltpu.SemaphoreType.DMA((nw,))],
        compiler_params=_params(),
    )(*shards)

    def copies(dsts, ss, rs):
        c, (me, xn, yn, dg), (px, py, psib) = _place()

        def rdma(k, window, peer):
            return _Lazy(lambda: pltpu.make_async_remote_copy(
                src_ref=window, dst_ref=window, send_sem=ss.at[k], recv_sem=rs.at[k], device_id=peer,
                device_id_type=MESH))

        first, later, swaps = [], [], []
        for i in range(nw):
            qr, hr, k0 = shapes[i][0] // 4, shapes[i][0] // 2, 6 * i
            q0, q1 = 2 * c * qr, (2 * c + 1) * qr
            w0, w1 = win(dsts[i], i, me, q0, qr), win(dsts[i], i, me, q1, qr)
            first.append([rdma(k0, w0, px), rdma(k0 + 1, w1, py), rdma(k0 + 2, w0, py), rdma(k0 + 3, w1, px)])
            later.append([rdma(k0 + 4, win(dsts[i], i, xn, q0, qr), py), rdma(k0 + 5, win(dsts[i], i, yn, q1, qr), px)])
            swaps.append([rdma(6 * nw + 3 * i + j, win(dsts[i], i, chip, c * hr, hr), psib)
                          for j, chip in enumerate((xn, yn, dg))])
        return first, later, swaps

    def send(srcs, dsts, ss, rs):
        for cps in copies(dsts, ss, rs)[0]:
            for cp in cps:
                cp.start()

    def forward(srcs, dsts, ss, rs):
        first, later, _ = copies(dsts, ss, rs)
        for i in range(nw):
            first[i][0].wait_recv()
            later[i][0].start()
            first[i][1].wait_recv()
            later[i][1].start()

    def swap(srcs, dsts, ss, rs):
        first, later, swaps = copies(dsts, ss, rs)
        for i in range(nw):
            for cp in first[i][2:] + later[i]:
                cp.wait_recv()
            for sw in swaps[i]:
                sw.start()

    def finish(srcs, dsts, ss, rs):
        first, later, swaps = copies(dsts, ss, rs)
        for i in range(nw):
            for sw in swaps[i]:
                sw.wait()
            for cp in first[i] + later[i]:
                cp.wait_send()

    return _Comm(placed, outs, 9 * nw, [send, forward, swap, finish], {i: i for i in range(nw)},
                 (0.0, 0.55, 0.85, 1.0))


def _gather_all(src):
    def copies(srcs, dsts, ss, rs):
        x, y, c = lax.axis_index("x"), lax.axis_index("y"), lax.axis_index("c")
        me, sib = (x, y, c), (x, y, 1 - c)
        chips = [(1 - x, y), (x, 1 - y), (1 - x, 1 - y)]

        def rows(px, py, pc):
            return dsts[0].at[4 * px + 2 * py + pc]

        def copy(k, block, to, src=None):
            return _Lazy(lambda: pltpu.make_async_remote_copy(
                src_ref=rows(*block) if src is None else src, dst_ref=rows(*block), send_sem=ss.at[k],
                recv_sem=rs.at[k], device_id=to, device_id_type=MESH))

        first = [copy(0, me, sib, src=srcs[0])] + [copy(1 + j, me, (*chip, c), src=srcs[0])
                                                   for j, chip in enumerate(chips)]
        passed = [copy(4 + j, (*chip, c), sib) for j, chip in enumerate(chips)]
        landed = [copy(0, sib, me)] + [copy(1 + j, (*chip, c), me) for j, chip in enumerate(chips)]
        landed += [copy(4 + j, (*chip, 1 - c), me) for j, chip in enumerate(chips)]
        return first, passed, landed

    def send(*refs):
        for cp in copies(*refs)[0]:
            cp.start()

    def forward(*refs):
        _, passed, landed = copies(*refs)
        for j in range(3):
            landed[1 + j].wait_recv()
            passed[j].start()

    def finish(*refs):
        first, passed, landed = copies(*refs)
        landed[0].wait_recv()
        for cp in landed[4:]:
            cp.wait_recv()
        for cp in first + passed:
            cp.wait_send()

    return _Comm([src], [((8,) + src.shape, src.dtype)], 7, [send, forward, finish], {}, (0.0, 0.5, 1.0))


def _f_in(x, g, w_in, tm):
    n = x.shape[0]

    def body(x_ref, g_ref, w_ref, h_ref, zgm_ref, zs5_ref, zg_ref):
        h = _rms_fwd(x_ref[...], g_ref[...]).astype(BF16)
        h_ref[...] = h
        zgm_ref[...] = _dot(h, w_ref[:, 0:2 * D])
        zs5_ref[...] = _dot(h, w_ref[:, 2 * D:2 * D + S5_W])
        zg_ref[...] = _dot(h, w_ref[:, 2 * D + S5_W:])

    return _rowcall("f_in", body, n, tm, [(x, 'row'), (g, 'full'), (w_in, 'full')],
                    [('row', D, BF16), ('row', 2 * D, F32), ('row', S5_W, F32), ('row', 2 * D, F32)])


def _tril():
    r = lax.broadcasted_iota(jnp.int32, (GM_CHUNK, GM_CHUNK), 0)
    c = lax.broadcasted_iota(jnp.int32, (GM_CHUNK, GM_CHUNK), 1)
    return r >= c


def _ln_stats(v):
    mu = jnp.mean(v, axis=-1, keepdims=True)
    vc = v - mu
    r = lax.rsqrt(jnp.mean(vc * vc, axis=-1, keepdims=True) + EPS)
    return vc * r, r


def _f_gmlp(zgm, ln_g, ln_b, w_s, b_s, tm):
    n = zgm.shape[0]

    def body(z_ref, lg_ref, lb_ref, ws_ref, bs_ref, y_ref):
        zg = _gelu(z_ref[...])
        u = zg[:, :D]
        vh, _ = _ln_stats(zg[:, D:])
        vn = (vh * lg_ref[...] + lb_ref[...]).astype(BF16)
        keep = _tril()
        for g in range(GM_GROUPS):
            w = jnp.where(keep, ws_ref[g], 0.0).astype(BF16)
            cs = slice(g * LANE, (g + 1) * LANE)
            for c in range(tm // GM_CHUNK):
                rs = slice(c * GM_CHUNK, (c + 1) * GM_CHUNK)
                sv = _dot(w, vn[rs, cs]) + bs_ref[g]
                y_ref[rs, cs] = (u[rs, cs] * sv).astype(BF16)

    return _rowcall("f_gmlp", body, n, tm,
                    [(zgm, 'row'), (ln_g, 'full'), (ln_b, 'full'), (w_s, 'full'), (b_s, 'full')],
                    [('row', D, BF16)])[0]


def _s5_specs(nb, nc, t, rev):
    def cc(c):
        return nc - 1 - c if rev else c
    slab = pl.BlockSpec((t, LANE), lambda j, b, c: (b * nc + cc(c), j))
    bb = pl.BlockSpec((None, LANE, 1024), lambda j, b, c: (j, 0, 0))
    cm = pl.BlockSpec((None, 1024, LANE), lambda j, b, c: (j, 0, 0))
    av = pl.BlockSpec((None, 1, 1024), lambda j, b, c: (j, 0, 0))
    dv = pl.BlockSpec((None, 1, LANE), lambda j, b, c: (j, 0, 0))
    st = pl.BlockSpec((None, None, None, 1, 1024), lambda j, b, c: (j, b, cc(c), 0, 0))
    return slab, bb, cm, av, dv, st


def _seg_load(ref, seg):
    return jnp.concatenate([ref[pl.ds(k, 8, stride=seg), :] for k in range(seg)], axis=0)


def _seg_store(ref, val, seg):
    for k in range(seg):
        ref[pl.ds(k, 8, stride=seg), :] = val[8 * k:8 * k + 8, :]


def _seg_scan(s_ref, ar, ai, seg, reverse):
    def step(i, carry):
        k = seg - 1 - i if reverse else i
        sr, si = carry
        nr = ar * sr - ai * si + s_ref[k, :, :512]
        ni = ar * si + ai * sr + s_ref[k, :, 512:]
        s_ref[k, :, :512] = nr
        s_ref[k, :, 512:] = ni
        return nr, ni

    zero = jnp.zeros((8, 512), F32)
    return lax.fori_loop(0, seg, step, (zero, zero), unroll=4)


def _seg_entries(tr, ti, pr, pi, cin, reverse):
    row = lax.broadcasted_iota(jnp.int32, (8, 512), 0)
    edge = row == (7 if reverse else 0)
    cr, ci = cin[:, :512], cin[:, 512:]
    xr = tr + jnp.where(edge, pr * cr - pi * ci, 0.0)
    xi = ti + jnp.where(edge, pr * ci + pi * cr, 0.0)
    ir, ii = _cscan(xr, xi, pr, pi, reverse)
    shift = 7 if reverse else 1
    er = jnp.where(edge, cr, pltpu.roll(ir, shift, 0))
    ei = jnp.where(edge, ci, pltpu.roll(ii, shift, 0))
    far = row == (0 if reverse else 7)
    out = jnp.concatenate([jnp.sum(jnp.where(far, ir, 0.0), axis=0, keepdims=True),
                           jnp.sum(jnp.where(far, ii, 0.0), axis=0, keepdims=True)], axis=1)
    return er, ei, out


def _seg_power(ar, ai, seg):
    pr, pi = ar, ai
    for _ in range(seg.bit_length() - 1):
        pr, pi = pr * pr - pi * pi, 2.0 * pr * pi
    return pr, pi


def _f_s5(zs5, bbc, ccm, avec, dvec, nb, t, comm):
    n = zs5.shape[0]
    nc = n // nb // t
    seg = t // 8
    slab, bb, cm, av, dv, st = _s5_specs(nb, nc, t, False)

    def body(u_ref, bb_ref, cc_ref, a_ref, d_ref, y_ref, st_ref, carry, s_ref):
        @pl.when(pl.program_id(2) == 0)
        def _():
            carry[...] = jnp.zeros_like(carry)
        cin = carry[...]
        st_ref[...] = cin
        up = _seg_load(u_ref, seg)
        s_ref[...] = _dot(up.astype(BF16), bb_ref[...]).reshape(seg, 8, 1024)
        a1r, a1i = a_ref[:, :512], a_ref[:, 512:]
        ar, ai = jnp.broadcast_to(a1r, (8, 512)), jnp.broadcast_to(a1i, (8, 512))
        tr, ti = _seg_scan(s_ref, ar, ai, seg, False)
        er, ei, cout = _seg_entries(tr, ti, *_seg_power(a1r, a1i, seg), cin, False)
        carry[...] = cout

        def apply(k, pe):
            pr, pi = pe
            pr, pi = ar * pr - ai * pi, ar * pi + ai * pr
            s_ref[k, :, :512] = s_ref[k, :, :512] + pr
            s_ref[k, :, 512:] = s_ref[k, :, 512:] + pi
            return pr, pi

        lax.fori_loop(0, seg, apply, (er, ei), unroll=4)
        s = s_ref[...].reshape(t, 1024).astype(BF16)
        _seg_store(y_ref, _dot(s, cc_ref[...]) + d_ref[...] * up, seg)

    return _hosting_call(
        body, "f_s5", (S5_SUPER, nb, nc), [slab, bb, cm, av, dv], [slab, st],
        [jax.ShapeDtypeStruct((n, S5_W), F32), jax.ShapeDtypeStruct((S5_SUPER, nb, nc, 1, 1024), F32)],
        [pltpu.VMEM((1, 1024), F32), pltpu.VMEM((seg, 8, 1024), F32)], ("arbitrary", "arbitrary", "arbitrary"),
        (zs5, bbc, ccm, avec, dvec), comm)


def _f_mix(ygm, ypre, zg, x, w_glu, w_br_gm, w_br_s5, w_mix_out, g_post, tm):
    n = x.shape[0]

    def body(ygm_ref, ypre_ref, zg_ref, x_ref, wglu_ref, wgm_ref, ws5_ref, wout_ref, g_ref,
             yg_ref, ys5_ref, a_ref, b_ref, mg_ref, mo_ref, x1_ref):
        yg = _gelu(ypre_ref[...])
        ygb = yg.astype(BF16)
        yg_ref[...] = ygb
        ys5 = (yg * _sig(_dot(ygb, wglu_ref[...]))).astype(BF16)
        ys5_ref[...] = ys5
        a = _dot(ygm_ref[...], wgm_ref[...])
        b = _dot(ys5, ws5_ref[...])
        a_ref[...] = a.astype(BF16)
        b_ref[...] = b.astype(BF16)
        zg = zg_ref[...]
        merged = (_sig(zg[:, :D]) * a + _sig(zg[:, D:]) * b).astype(BF16)
        mg_ref[...] = merged
        mo = _dot(merged, wout_ref[...])
        mo_ref[...] = mo
        x1_ref[...] = x_ref[...] + _rms_fwd(mo, g_ref[...])

    return _rowcall("f_mix", body, n, tm,
                    [(ygm, 'row'), (ypre, 'row'), (zg, 'row'), (x, 'row'), (w_glu, 'full'), (w_br_gm, 'full'),
                     (w_br_s5, 'full'), (w_mix_out, 'full'), (g_post, 'full')],
                    [('row', S5_W, BF16), ('row', S5_W, BF16), ('row', D, BF16), ('row', D, BF16),
                     ('row', D, BF16), ('row', D, F32), ('row', D, F32)])


def _f_mem(mem, g_mem, w_kv, tm):
    n = mem.shape[0]

    def body(m_ref, g_ref, w_ref, mn_ref, k_ref, v_ref):
        mn = _rms_fwd(m_ref[...], g_ref[...]).astype(BF16)
        mn_ref[...] = mn
        k_ref[...] = _dot(mn, w_ref[:, :D]).astype(BF16)
        v_ref[...] = _dot(mn, w_ref[:, D:]).astype(BF16)

    return _rowcall("f_mem", body, n, tm, [(mem, 'row'), (g_mem, 'full'), (w_kv, 'full')],
                    [('row', D, BF16), ('row', D, BF16), ('row', D, BF16)])


def _softmax(s):
    m = jnp.max(s, axis=-1, keepdims=True)
    e = jnp.exp(s - m)
    return e / jnp.sum(e, axis=-1, keepdims=True)


def _f_attn(x1, g_pre, w_q, k, v, w_o, g_post, tm, tpb, mlen):
    n = x1.shape[0]
    kv_spec = pl.BlockSpec((mlen, D), lambda i: (i // tpb, 0))
    scale = HEAD_DIM ** -0.5

    def body(x_ref, gp_ref, wq_ref, k_ref, v_ref, wo_ref, go_ref, hc_ref, o_ref, ao_ref, x2_ref):
        x1v = x_ref[...]
        hc = _rms_fwd(x1v, gp_ref[...]).astype(BF16)
        hc_ref[...] = hc
        q = _dot(hc, wq_ref[...])
        for h in range(HEADS):
            hs = slice(h * HEAD_DIM, (h + 1) * HEAD_DIM)
            p = _softmax(_dot_nt(q[:, hs].astype(BF16), k_ref[:, hs]) * scale)
            o_ref[:, hs] = _dot(p.astype(BF16), v_ref[:, hs]).astype(BF16)
        ao = _dot(o_ref[...], wo_ref[...])
        ao_ref[...] = ao
        x2_ref[...] = x1v + _rms_fwd(ao, go_ref[...])

    return _rowcall("f_attn", body, n, tm,
                    [(x1, 'row'), (g_pre, 'full'), (w_q, 'full'), (k, kv_spec), (v, kv_spec), (w_o, 'full'),
                     (g_post, 'full')],
                    [('row', D, BF16), ('row', D, BF16), ('row', D, F32), ('row', D, F32)])


def _f_ffn_up(x2, g_pre, w_gu, tm):
    n = x2.shape[0]

    def body(x_ref, g_ref, w_ref, hf_ref, gu_ref, act_ref):
        hf = _rms_fwd(x_ref[...], g_ref[...]).astype(BF16)
        hf_ref[...] = hf
        gg = _dot(hf, w_ref[:, :FFN_H])
        uu = _dot(hf, w_ref[:, FFN_H:])
        gu_ref[:, :FFN_H] = gg.astype(BF16)
        gu_ref[:, FFN_H:] = uu.astype(BF16)
        act_ref[...] = (gg * _sig(gg) * uu).astype(BF16)

    return _rowcall("f_ffn_up", body, n, tm, [(x2, 'row'), (g_pre, 'full'), (w_gu, 'full')],
                    [('row', D, BF16), ('row', 2 * FFN_H, BF16), ('row', FFN_H, BF16)])


def _f_ffn_down(act, x2, tgt, w_down, g_post, tm):
    n = x2.shape[0]

    def body(a_ref, x_ref, t_ref, w_ref, g_ref, dn_ref, d3_ref, loss_ref):
        @pl.when(pl.program_id(0) == 0)
        def _():
            loss_ref[...] = jnp.zeros_like(loss_ref)
        dn = _dot(a_ref[...], w_ref[...])
        dn_ref[...] = dn
        err = x_ref[...] + _rms_fwd(dn, g_ref[...]) - t_ref[...]
        d3_ref[...] = err * (1.0 / D)
        loss_ref[...] += jnp.sum(err * err, axis=0, keepdims=True)

    return _rowcall("f_ffn_down", body, n, tm,
                    [(act, 'row'), (x2, 'row'), (tgt, 'row'), (w_down, 'full'), (g_post, 'full')],
                    [('row', D, F32), ('row', D, F32), ('acc', (1, D), F32)])


def _b_ffn_down(d3, dn, gu, g_post, w_down, tm):
    n = d3.shape[0]

    def body(d3_ref, dn_ref, gu_ref, g_ref, w_ref, ddn_ref, dgu_ref, dg_ref):
        @pl.when(pl.program_id(0) == 0)
        def _():
            dg_ref[...] = jnp.zeros_like(dg_ref)
        ddn, dg = _rms_bwd(dn_ref[...], g_ref[...], d3_ref[...])
        dg_ref[...] += dg
        ddn = ddn.astype(BF16)
        ddn_ref[...] = ddn
        dact = _dot_nt(ddn, w_ref[...])
        gg = gu_ref[:, :FFN_H].astype(F32)
        uu = gu_ref[:, FFN_H:].astype(F32)
        sg = _sig(gg)
        dgu_ref[:, :FFN_H] = (dact * uu * sg * (1.0 + gg * (1.0 - sg))).astype(BF16)
        dgu_ref[:, FFN_H:] = (dact * gg * sg).astype(BF16)

    return _rowcall("b_ffn_down", body, n, tm,
                    [(d3, 'row'), (dn, 'row'), (gu, 'row'), (g_post, 'full'), (w_down, 'full')],
                    [('row', D, BF16), ('row', 2 * FFN_H, BF16), ('acc', (1, D), F32)])


def _b_ffn_up(dgu, d3, x2, w_gu, g_pre, tm):
    n = d3.shape[0]

    def body(dgu_ref, d3_ref, x_ref, w_ref, g_ref, dx_ref, dg_ref):
        @pl.when(pl.program_id(0) == 0)
        def _():
            dg_ref[...] = jnp.zeros_like(dg_ref)
        dhf = _dot_nt(dgu_ref[...], w_ref[...])
        dx, dg = _rms_bwd(x_ref[...], g_ref[...], dhf)
        dg_ref[...] += dg
        dx_ref[...] = d3_ref[...] + dx

    return _rowcall("b_ffn_up", body, n, tm,
                    [(dgu, 'row'), (d3, 'row'), (x2, 'row'), (w_gu, 'full'), (g_pre, 'full')],
                    [('row', D, F32), ('acc', (1, D), F32)])


def _b_attn(dx2, ao, x1, hc, g_post, w_o, w_q, g_pre, k, v, tm, tpb, mlen):
    n = dx2.shape[0]
    nb = n // (tm * tpb)
    kv_spec = pl.BlockSpec((mlen, D), lambda i: (i // tpb, 0))
    scale = HEAD_DIM ** -0.5

    def body(dx2_ref, ao_ref, x1_ref, hc_ref, go_ref, wo_ref, wq_ref, gp_ref, k_ref, v_ref,
             dao_ref, dq_ref, dx1_ref, dk_ref, dv_ref, dgo_ref, dgp_ref):
        i = pl.program_id(0)

        @pl.when(i == 0)
        def _():
            dgo_ref[...] = jnp.zeros_like(dgo_ref)
            dgp_ref[...] = jnp.zeros_like(dgp_ref)

        @pl.when(i % tpb == 0)
        def _():
            dk_ref[...] = jnp.zeros_like(dk_ref)
            dv_ref[...] = jnp.zeros_like(dv_ref)

        dx2v = dx2_ref[...]
        dao, dgo = _rms_bwd(ao_ref[...], go_ref[...], dx2v)
        dgo_ref[...] += dgo
        dao = dao.astype(BF16)
        dao_ref[...] = dao
        do = _dot_nt(dao, wo_ref[...])
        q = _dot(hc_ref[...], wq_ref[...])
        for h in range(HEADS):
            hs = slice(h * HEAD_DIM, (h + 1) * HEAD_DIM)
            qh = q[:, hs].astype(BF16)
            kh = k_ref[:, hs]
            p = _softmax(_dot_nt(qh, kh) * scale)
            doh = do[:, hs].astype(BF16)
            dp = _dot_nt(doh, v_ref[:, hs])
            ds = (p * (dp - jnp.sum(dp * p, axis=-1, keepdims=True)) * scale).astype(BF16)
            dq_ref[:, hs] = _dot(ds, kh).astype(BF16)
            dk_ref[:, hs] += _dot_tn(ds, qh)
            dv_ref[:, hs] += _dot_tn(p.astype(BF16), doh)
        dhc = _dot_nt(dq_ref[...], wq_ref[...])
        dx, dgp = _rms_bwd(x1_ref[...], gp_ref[...], dhc)
        dgp_ref[...] += dgp
        dx1_ref[...] = dx2v + dx

    return _rowcall("b_attn", body, n, tm,
                    [(dx2, 'row'), (ao, 'row'), (x1, 'row'), (hc, 'row'), (g_post, 'full'), (w_o, 'full'),
                     (w_q, 'full'), (g_pre, 'full'), (k, kv_spec), (v, kv_spec)],
                    [('row', D, BF16), ('row', D, BF16), ('row', D, F32),
                     (kv_spec, (nb * mlen, D), F32), (kv_spec, (nb * mlen, D), F32),
                     ('acc', (1, D), F32), ('acc', (1, D), F32)])


def _b_mem(dk, dv, mem, w_kv, g_mem, tm):
    n = mem.shape[0]

    def body(dk_ref, dv_ref, m_ref, w_ref, g_ref, dkv_ref, dg_ref):
        @pl.when(pl.program_id(0) == 0)
        def _():
            dg_ref[...] = jnp.zeros_like(dg_ref)
        dkb = dk_ref[...].astype(BF16)
        dvb = dv_ref[...].astype(BF16)
        dkv_ref[:, :D] = dkb
        dkv_ref[:, D:] = dvb
        dmn = _dot_nt(dkb, w_ref[:, :D]) + _dot_nt(dvb, w_ref[:, D:])
        _, dg = _rms_bwd(m_ref[...], g_ref[...], dmn)
        dg_ref[...] += dg

    return _rowcall("b_mem", body, n, tm, [(dk, 'row'), (dv, 'row'), (mem, 'row'), (w_kv, 'full'), (g_mem, 'full')],
                    [('row', 2 * D, BF16), ('acc', (1, D), F32)])


def _b_mix(dx1, mo, a, b, zg, ypre, g_post, w_mix_out, w_br_gm, w_br_s5, w_glu, tm):
    n = dx1.shape[0]

    def body(dx1_ref, mo_ref, a_ref, b_ref, zg_ref, ypre_ref, g_ref, wout_ref, wgm_ref, ws5_ref, wglu_ref,
             dmo_ref, da_ref, db_ref, dzg_ref, dygm_ref, dgate_ref, dypre_ref, dg_ref):
        @pl.when(pl.program_id(0) == 0)
        def _():
            dg_ref[...] = jnp.zeros_like(dg_ref)
        dmo, dg = _rms_bwd(mo_ref[...], g_ref[...], dx1_ref[...])
        dg_ref[...] += dg
        dmo = dmo.astype(BF16)
        dmo_ref[...] = dmo
        dmg = _dot_nt(dmo, wout_ref[...])
        zg = zg_ref[...]
        sa, sb = _sig(zg[:, :D]), _sig(zg[:, D:])
        da = (dmg * sa).astype(BF16)
        db = (dmg * sb).astype(BF16)
        da_ref[...] = da
        db_ref[...] = db
        dzg_ref[:, :D] = (dmg * a_ref[...].astype(F32) * sa * (1.0 - sa)).astype(BF16)
        dzg_ref[:, D:] = (dmg * b_ref[...].astype(F32) * sb * (1.0 - sb)).astype(BF16)
        dygm_ref[...] = _dot_nt(da, wgm_ref[...])
        dys5 = _dot_nt(db, ws5_ref[...])
        ypre = ypre_ref[...]
        yg = _gelu(ypre)
        sgt = _sig(_dot(yg.astype(BF16), wglu_ref[...]))
        dgate = (dys5 * yg * sgt * (1.0 - sgt)).astype(BF16)
        dgate_ref[...] = dgate
        dyg = dys5 * sgt + _dot_nt(dgate, wglu_ref[...])
        dypre_ref[...] = dyg * _gelu_grad(ypre)

    return _rowcall("b_mix", body, n, tm,
                    [(dx1, 'row'), (mo, 'row'), (a, 'row'), (b, 'row'), (zg, 'row'), (ypre, 'row'), (g_post, 'full'),
                     (w_mix_out, 'full'), (w_br_gm, 'full'), (w_br_s5, 'full'), (w_glu, 'full')],
                    [('row', D, BF16), ('row', D, BF16), ('row', D, BF16), ('row', 2 * D, BF16), ('row', D, F32),
                     ('row', S5_W, BF16), ('row', S5_W, F32), ('acc', (1, D), F32)])


def _b_s5(zs5, dypre, states, bbc, ccm, avec, dvec, nb, t):
    n = zs5.shape[0]
    nc = n // nb // t
    seg = t // 8
    slab, bb, cm, av, dv, st = _s5_specs(nb, nc, t, True)

    def body(u_ref, dy_ref, st_ref, bb_ref, cc_ref, a_ref, d_ref,
             du_ref, dbb_ref, dcc_ref, da_ref, dd_ref, lcarry, s_ref, l_ref):
        first = jnp.logical_and(pl.program_id(1) == 0, pl.program_id(2) == 0)

        @pl.when(first)
        def _():
            dbb_ref[...] = jnp.zeros_like(dbb_ref)
            dcc_ref[...] = jnp.zeros_like(dcc_ref)
            da_ref[...] = jnp.zeros_like(da_ref)
            dd_ref[...] = jnp.zeros_like(dd_ref)

        @pl.when(pl.program_id(2) == 0)
        def _():
            lcarry[...] = jnp.zeros_like(lcarry)

        up, dyp = _seg_load(u_ref, seg), _seg_load(dy_ref, seg)
        ub, dyb = up.astype(BF16), dyp.astype(BF16)
        a1r, a1i = a_ref[:, :512], a_ref[:, 512:]
        ar, ai = jnp.broadcast_to(a1r, (8, 512)), jnp.broadcast_to(a1i, (8, 512))
        pr, pi = _seg_power(a1r, a1i, seg)

        s_ref[...] = _dot(ub, bb_ref[...]).reshape(seg, 8, 1024)
        tr, ti = _seg_scan(s_ref, ar, ai, seg, False)
        er, ei, _ = _seg_entries(tr, ti, pr, pi, st_ref[...], False)

        def apply(k, pe):
            qr, qi = pe
            qr, qi = ar * qr - ai * qi, ar * qi + ai * qr
            s_ref[k, :, :512] = s_ref[k, :, :512] + qr
            s_ref[k, :, 512:] = s_ref[k, :, 512:] + qi
            return qr, qi

        lax.fori_loop(0, seg, apply, (er, ei), unroll=4)

        l_ref[...] = _dot_nt(dyb, cc_ref[...]).reshape(seg, 8, 1024)
        tr, ti = _seg_scan(l_ref, ar, -ai, seg, True)
        fr, fi, lout = _seg_entries(tr, ti, pr, -pi, lcarry[...], True)
        lcarry[...] = lout

        def apply_back(i, carry):
            qr, qi, accr, acci = carry
            k = seg - 1 - i
            qr, qi = ar * qr + ai * qi, ar * qi - ai * qr
            lr = l_ref[k, :, :512] + qr
            li = l_ref[k, :, 512:] + qi
            l_ref[k, :, :512] = lr
            l_ref[k, :, 512:] = li
            kp = jnp.maximum(k - 1, 0)
            sr = jnp.where(k == 0, er, s_ref[kp, :, :512])
            si = jnp.where(k == 0, ei, s_ref[kp, :, 512:])
            return qr, qi, accr + lr * sr + li * si, acci + li * sr - lr * si

        zero = jnp.zeros((8, 512), F32)
        _, _, accr, acci = lax.fori_loop(0, seg, apply_back, (fr, fi, zero, zero), unroll=4)
        da_ref[:, :512] += jnp.sum(accr, axis=0, keepdims=True)
        da_ref[:, 512:] += jnp.sum(acci, axis=0, keepdims=True)

        s = s_ref[...].reshape(t, 1024).astype(BF16)
        lam = l_ref[...].reshape(t, 1024).astype(BF16)
        dcc_ref[...] += _dot_tn(dyb, s)
        dbb_ref[...] += _dot_tn(ub, lam)
        _seg_store(du_ref, _dot_nt(lam, bb_ref[...]) + d_ref[...] * dyp, seg)
        dd_ref[...] += jnp.sum(dyp * up, axis=0, keepdims=True)

    return pl.pallas_call(
        body, name="b_s5", grid=(S5_SUPER, nb, nc), in_specs=[slab, slab, st, bb, cm, av, dv],
        out_specs=[slab, bb, bb, av, dv],
        out_shape=[jax.ShapeDtypeStruct((n, S5_W), F32), jax.ShapeDtypeStruct(bbc.shape, F32),
                   jax.ShapeDtypeStruct(bbc.shape, F32), jax.ShapeDtypeStruct(avec.shape, F32),
                   jax.ShapeDtypeStruct(dvec.shape, F32)],
        scratch_shapes=[pltpu.VMEM((1, 1024), F32), pltpu.VMEM((seg, 8, 1024), F32),
                        pltpu.VMEM((seg, 8, 1024), F32)],
        compiler_params=_params(("arbitrary", "arbitrary", "arbitrary")),
    )(zs5, dypre, states, bbc, ccm, avec, dvec)


def _b_gmlp(zgm, dygm, ln_g, ln_b, w_s, b_s, tm):
    n = zgm.shape[0]

    def body(z_ref, dy_ref, lg_ref, lb_ref, ws_ref, bs_ref, dz_ref, dws_ref, dbs_ref, dlg_ref, dlb_ref,
             du_s, dvn_s):
        @pl.when(pl.program_id(0) == 0)
        def _():
            dws_ref[...] = jnp.zeros_like(dws_ref)
            dbs_ref[...] = jnp.zeros_like(dbs_ref)
            dlg_ref[...] = jnp.zeros_like(dlg_ref)
            dlb_ref[...] = jnp.zeros_like(dlb_ref)
        z = z_ref[...]
        zg = _gelu(z)
        u = zg[:, :D]
        vh, r = _ln_stats(zg[:, D:])
        vn = (vh * lg_ref[...] + lb_ref[...]).astype(BF16)
        dy = dy_ref[...]
        keep = _tril()
        for g in range(GM_GROUPS):
            w = jnp.where(keep, ws_ref[g], 0.0).astype(BF16)
            cs = slice(g * LANE, (g + 1) * LANE)
            for c in range(tm // GM_CHUNK):
                rs = slice(c * GM_CHUNK, (c + 1) * GM_CHUNK)
                vb = vn[rs, cs]
                sv = _dot(w, vb) + bs_ref[g]
                dyb = dy[rs, cs]
                du_s[rs, cs] = dyb * sv
                dsv = dyb * u[rs, cs]
                dsvb = dsv.astype(BF16)
                dvn_s[rs, cs] = _dot_tn(w, dsvb)
                dws_ref[g] += jnp.where(keep, _dot_nt(dsvb, vb), 0.0)
                dbs_ref[g] += jnp.sum(dsv, axis=1, keepdims=True)
        dvn = dvn_s[...]
        dlg_ref[...] += jnp.sum(dvn * vh, axis=0, keepdims=True)
        dlb_ref[...] += jnp.sum(dvn, axis=0, keepdims=True)
        dvh = dvn * lg_ref[...]
        dv = r * (dvh - jnp.mean(dvh, axis=-1, keepdims=True) - vh * jnp.mean(dvh * vh, axis=-1, keepdims=True))
        dz_ref[:, :D] = (du_s[...] * _gelu_grad(z[:, :D])).astype(BF16)
        dz_ref[:, D:] = (dv * _gelu_grad(z[:, D:])).astype(BF16)

    return _rowcall("b_gmlp", body, n, tm,
                    [(zgm, 'row'), (dygm, 'row'), (ln_g, 'full'), (ln_b, 'full'), (w_s, 'full'), (b_s, 'full')],
                    [('row', 2 * D, BF16), ('acc', w_s.shape, F32), ('acc', b_s.shape, F32), ('acc', (1, D), F32),
                     ('acc', (1, D), F32)],
                    scratch=[pltpu.VMEM((tm, D), F32), pltpu.VMEM((tm, D), F32)])


def _b_in(dzgm, dzs5, dzg, dx1, x, w_in, g_pre, tm):
    n = x.shape[0]

    def body(d1_ref, d2_ref, d3_ref, dx1_ref, x_ref, w_ref, g_ref, gx_ref, dg_ref):
        @pl.when(pl.program_id(0) == 0)
        def _():
            dg_ref[...] = jnp.zeros_like(dg_ref)
        dh = (_dot_nt(d1_ref[...], w_ref[:, 0:2 * D]) + _dot_nt(d2_ref[...].astype(BF16), w_ref[:, 2 * D:2 * D + S5_W])
              + _dot_nt(d3_ref[...], w_ref[:, 2 * D + S5_W:]))
        dx, dg = _rms_bwd(x_ref[...], g_ref[...], dh)
        dg_ref[...] += dg
        gx_ref[...] = dx1_ref[...] + dx

    return _rowcall("b_in", body, n, tm,
                    [(dzgm, 'row'), (dzs5, 'row'), (dzg, 'row'), (dx1, 'row'), (x, 'row'), (w_in, 'full'),
                     (g_pre, 'full')],
                    [('row', D, F32), ('acc', (1, D), F32)])


def _whole(name, body, ins, outs):
    return pl.pallas_call(body, name=name, out_shape=[jax.ShapeDtypeStruct(s, dt) for s, dt in outs],
                          compiler_params=_params())(*ins)


def _s5_disc_fwd(lr, li, ls, br, bi):
    def body(lr_ref, li_ref, ls_ref, br_ref, bi_ref, o1, o2, o3, o4):
        outs = _s5_disc(lr_ref[...], li_ref[...], ls_ref[...], br_ref[...], bi_ref[...])
        for o, val in zip((o1, o2, o3, o4), outs):
            o[...] = val

    return _whole("s5_disc_fwd", body, [lr, li, ls, br, bi],
                  [(lr.shape, F32), (lr.shape, F32), (br.shape, F32), (br.shape, F32)])


def _s5_disc_bwd(lr, li, ls, br, bi, cts):
    def body(lr_ref, li_ref, ls_ref, br_ref, bi_ref, c1, c2, c3, c4, o1, o2, o3, o4, o5):
        _, vjp = jax.vjp(_s5_disc, lr_ref[...], li_ref[...], ls_ref[...], br_ref[...], bi_ref[...])
        grads = vjp((c1[...], c2[...], c3[...], c4[...]))
        for o, val in zip((o1, o2, o3, o4, o5), grads):
            o[...] = val

    return _whole("s5_disc_bwd", body, [lr, li, ls, br, bi, *cts],
                  [(lr.shape, F32), (lr.shape, F32), (lr.shape, F32), (br.shape, F32), (br.shape, F32)])


def _sum_slots(name, slots):
    def body(s_ref, o_ref):
        acc = s_ref[0]
        for k in range(1, slots.shape[0]):
            acc = acc + s_ref[k]
        o_ref[...] = acc

    return _whole(name, body, [slots], [(slots.shape[1:], F32)])[0]


def _adamw(name, w, g, m, v, tm):
    c1 = 1.0 - ADAM_B1 ** ADAM_STEP
    c2 = 1.0 - ADAM_B2 ** ADAM_STEP

    def body(w_ref, g_ref, m_ref, v_ref, d_ref, mo_ref, vo_ref):
        gg = g_ref[...]
        mn = ADAM_B1 * m_ref[...] + (1.0 - ADAM_B1) * gg
        vn = ADAM_B2 * v_ref[...] + (1.0 - ADAM_B2) * (gg * gg)
        mo_ref[...] = mn
        vo_ref[...] = vn
        d_ref[...] = -ADAM_LR * ((mn / c1) / (jnp.sqrt(vn / c2) + ADAM_EPS) + ADAM_WD * w_ref[...])

    cols = w.shape[1]
    return _rowcall(name, body, w.shape[0], tm, [(w, 'row'), (g, 'row'), (m, 'row'), (v, 'row')],
                    [('row', cols, F32), ('row', cols, F32), ('row', cols, F32)])


def _picked_rowcall(name, body, grid, in_specs, out_cols, tm):
    return pl.pallas_call(
        lambda p_ref, *refs: body(*refs), name=name,
        grid_spec=pltpu.PrefetchScalarGridSpec(
            num_scalar_prefetch=1, grid=(grid,), in_specs=in_specs,
            out_specs=[pl.BlockSpec((tm, D), lambda i, p: (i, 0)) for _ in out_cols]),
        out_shape=[jax.ShapeDtypeStruct((grid * tm, D), dt) for dt in out_cols],
        compiler_params=_params(("arbitrary",)),
    )


def _reduce_big(slabs):
    rows = [s.shape[1] for s in slabs]
    blk = sum(rows) // 4
    tm = _tile_rows(blk, 512)
    per = blk // tm
    gpack = jnp.concatenate([s.reshape(4, 2, 2, r // 4, D) for s, r in zip(slabs, rows)], axis=3)

    def rdma(ss, rs, k, src, dst, peer):
        return pltpu.make_async_remote_copy(src_ref=src, dst_ref=dst, send_sem=ss.at[k], recv_sem=rs.at[k],
                                            device_id=peer, device_id_type=MESH)

    mx, my, mc = lax.axis_index("x"), lax.axis_index("y"), lax.axis_index("c")
    chip_me, chip_xn, chip_yn = 2 * mx + my, 2 * (1 - mx) + my, 2 * mx + (1 - my)

    def pair_body(srcs, dsts, ss, rs):
        c, _, (_, _, psib) = _place()
        cp = rdma(ss, rs, 0, srcs[0].at[:, 1 - c], dsts[0], psib)
        cp.start()
        cp.wait()

    from_sib = _comm_call("reduce_pair", [gpack], [((4, 2, blk, D), F32)], 1, pair_body)[0]

    def sum_both(a_ref, b_ref, o32_ref, o16_ref):
        s = a_ref[...].astype(F32) + b_ref[...].astype(F32)
        o32_ref[...] = s
        o16_ref[...] = s.astype(BF16)

    p32, p16 = _picked_rowcall(
        "sum_pair", sum_both, 8 * per,
        [pl.BlockSpec((None, None, None, tm, D), lambda i, p: (i // (2 * per), p[0], (i // per) % 2, i % per, 0)),
         pl.BlockSpec((tm, D), lambda i, p: (i, 0))],
        [F32, BF16], tm)(jnp.stack([mc]).astype(jnp.int32), gpack, from_sib.reshape(8 * blk, D))
    p32, p16 = p32.reshape(4, 2, blk, D), p16.reshape(4, 2, blk, D)

    def step1_body(srcs, dsts, ss, rs):
        _, (me, xn, yn, dg), (px, py, _) = _place()
        s16 = srcs[0]
        cps = [rdma(ss, rs, 0, s16.at[yn, 0], dsts[0].at[0], py), rdma(ss, rs, 1, s16.at[dg, 0], dsts[0].at[1], py),
               rdma(ss, rs, 2, s16.at[xn, 1], dsts[0].at[2], px), rdma(ss, rs, 3, s16.at[dg, 1], dsts[0].at[3], px)]
        for cp in cps:
            cp.start()
        for cp in cps:
            cp.wait()

    recv1 = _comm_call("reduce_step1", [p16], [((4, blk, D), BF16)], 4, step1_body)[0]

    s32, s16 = _picked_rowcall(
        "sum_step1", sum_both, 4 * per,
        [pl.BlockSpec((None, None, tm, D), lambda i, p: (p[i // per], i // (2 * per), i % per, 0)),
         pl.BlockSpec((tm, D), lambda i, p: (i, 0))],
        [F32, BF16], tm)(jnp.stack([chip_me, chip_xn, chip_me, chip_yn]).astype(jnp.int32), p32,
                         recv1.reshape(4 * blk, D))

    def step2_body(srcs, dsts, ss, rs):
        _, _, (px, py, _) = _place()
        cps = [rdma(ss, rs, 0, srcs[0].at[1], dsts[0].at[0], px), rdma(ss, rs, 1, srcs[0].at[3], dsts[0].at[1], py)]
        for cp in cps:
            cp.start()
        for cp in cps:
            cp.wait()

    recv2 = _comm_call("reduce_step2", [s16.reshape(4, blk, D)], [((2, blk, D), BF16)], 2, step2_body)[0]

    def sum_step2(a_ref, b_ref, o_ref):
        o_ref[...] = a_ref[...] + b_ref[...].astype(F32)

    red = _rowcall("sum_step2", sum_step2, 2 * blk, tm,
                   [(s32.reshape(4, blk, D), pl.BlockSpec((None, tm, D), lambda i: (2 * (i // per), i % per, 0))),
                    (recv2.reshape(2 * blk, D), 'row')],
                   [('row', D, F32)])[0].reshape(2, blk, D)

    def share_body(srcs, dsts, ss, rs):
        _, _, (_, _, psib) = _place()
        cp = rdma(ss, rs, 0, srcs[0], dsts[0], psib)
        cp.start()
        cp.wait()

    other = _comm_call("share_pair", [red], [((2, blk, D), F32)], 1, share_body)[0]
    lo = jnp.where(mc == 0, red, other)
    hi = jnp.where(mc == 0, other, red)
    out, off = [], 0
    for r in rows:
        q = r // 4
        out.append(jnp.concatenate([lo[0, off:off + q], lo[1, off:off + q], hi[0, off:off + q], hi[1, off:off + q]]))
        off += q
    return out


def _rows1024(a):
    flat = a.reshape(-1)
    pad = (-flat.shape[0]) % D
    if pad:
        flat = jnp.concatenate([flat, jnp.zeros((pad,), flat.dtype)])
    return flat.reshape(-1, D)


def _pack(arrs, pad_rows_to=8):
    parts = [_rows1024(a) for a in arrs]
    rows = sum(p.shape[0] for p in parts)
    pad = (-rows) % pad_rows_to
    if pad:
        parts.append(jnp.zeros((pad, D), parts[0].dtype))
    return jnp.concatenate(parts, axis=0)


def _unpack(packed, shapes):
    out, r = [], 0
    for s in shapes:
        size = math.prod(s)
        nr = -(-size // D)
        out.append(packed[r:r + nr].reshape(-1)[:size].reshape(s))
        r += nr
    return out


def _shard_slabs(full, axis):
    r, c = full.shape
    if axis == 0:
        return full.reshape(4, -1, D)
    return full.reshape(r, 4, c // 4).transpose(1, 0, 2).reshape(4, -1, D)


def _block_diag(t):
    eye = jnp.eye(8, dtype=t.dtype)
    j, g, a, b = t.shape
    return (t[:, :, :, None, :] * eye[None, :, None, :, None]).reshape(j, g * a, g * b)


def _block_diag_take(m, a, b):
    eye = jnp.eye(8, dtype=m.dtype)
    return (m.reshape(4, 8, a, 8, b) * eye[None, :, None, :, None]).sum(axis=3)


def kernel(x, mem, g_mix_pre, w_in, gm_ln_g, gm_ln_b, gm_w_s, gm_b_s, s5_lam_re, s5_lam_im, s5_log_step, s5_b_re, s5_b_im, s5_c_re, s5_c_im, s5_d, s5_w_glu, w_br_gm, w_br_s5, w_mix_out, g_mix_post, g_ca_pre, g_mem, ca_w_q, ca_w_kv, ca_w_o, g_ca_post, g_ffn_pre, ffn_w_gu, ffn_w_down, g_ffn_post, loss_target, m_g_mix_pre, m_w_in, m_gm_ln_g, m_gm_ln_b, m_gm_w_s, m_gm_b_s, m_s5_lam_re, m_s5_lam_im, m_s5_log_step, m_s5_b_re, m_s5_b_im, m_s5_c_re, m_s5_c_im, m_s5_d, m_s5_w_glu, m_w_br_gm, m_w_br_s5, m_w_mix_out, m_g_mix_post, m_g_ca_pre, m_g_mem, m_ca_w_q, m_ca_w_kv, m_ca_w_o, m_g_ca_post, m_g_ffn_pre, m_ffn_w_gu, m_ffn_w_down, m_g_ffn_post, v_g_mix_pre, v_w_in, v_gm_ln_g, v_gm_ln_b, v_gm_w_s, v_gm_b_s, v_s5_lam_re, v_s5_lam_im, v_s5_log_step, v_s5_b_re, v_s5_b_im, v_s5_c_re, v_s5_c_im, v_s5_d, v_s5_w_glu, v_w_br_gm, v_w_br_s5, v_w_mix_out, v_g_mix_post, v_g_ca_pre, v_g_mem, v_ca_w_q, v_ca_w_kv, v_ca_w_o, v_g_ca_post, v_g_ffn_pre, v_ffn_w_gu, v_ffn_w_down, v_g_ffn_post):
    a = dict(locals())
    w = {n: a[n][0] for n in WNAMES}
    nb, seq, _ = x.shape
    n = nb * seq
    mlen = mem.shape[1]
    tm = min(256, seq)
    tpb = seq // tm
    xf = x.reshape(n, D)
    tgt = loss_target.reshape(n, D)
    memf = mem.reshape(nb * mlen, D)
    tmm = min(256, mlen)

    big_names = list(BIG)
    local_shapes = {k: w[k].shape for k in big_names}
    axes = [BIG[k] for k in big_names]
    w_in_full = _run_comm("gather_w_in", _gather_weights("cast_w_in", [w['w_in']], [BIG['w_in']]))[0]
    rest = [k for k in big_names if k != 'w_in']
    gather_rest = _gather_weights("cast_w_rest", [w[k] for k in rest], [BIG[k] for k in rest])

    vec = lambda name: w[name].reshape(1, D)

    to_lane = lambda p: p.reshape(1, -1)
    b_t = lambda p: p.transpose(2, 0, 1).reshape(16, -1)
    lr_l, li_l = to_lane(w['s5_lam_re']), to_lane(w['s5_lam_im'])
    ls_l = jnp.repeat(w['s5_log_step'], 64).reshape(1, -1)
    br_t, bi_t = b_t(w['s5_b_re']), b_t(w['s5_b_im'])
    ab_re, ab_im, bb_re, bb_im = _s5_disc_fwd(lr_l, li_l, ls_l, br_t, bi_t)
    blk = lambda t: _block_diag(t.reshape(16, 4, 8, 64).transpose(1, 2, 0, 3))
    bbc = jnp.concatenate([blk(bb_re), blk(bb_im)], axis=2).astype(BF16)
    cblk = lambda c: _block_diag(c.reshape(4, 8, 16, 64).transpose(0, 1, 3, 2))
    ccm = jnp.concatenate([cblk(w['s5_c_re']), -cblk(w['s5_c_im'])], axis=1).astype(BF16)
    avec = jnp.concatenate([ab_re.reshape(4, 1, 512), ab_im.reshape(4, 1, 512)], axis=2)
    dvec = w['s5_d'].reshape(4, 1, LANE)

    bs3 = w['gm_b_s'].reshape(GM_GROUPS, GM_CHUNK, 1)
    wf = {'w_in': w_in_full}
    h, zgm, zs5, zg = _f_in(xf, vec('g_mix_pre'), wf['w_in'], tm)
    ygm = _f_gmlp(zgm, vec('gm_ln_g'), vec('gm_ln_b'), w['gm_w_s'], bs3, tm)
    s5_t = min(S5_T, seq)
    (ypre, states), gathered = _f_s5(zs5, bbc, ccm, avec, dvec, nb, s5_t, gather_rest)
    for k, g in zip(rest, gathered):
        wf[k] = g.reshape(-1, g.shape[-1]) if BIG[k] == 0 else g
    yg, ys5, a_br, b_br, merged, mo, x1 = _f_mix(ygm, ypre, zg, xf, wf['s5_w_glu'], wf['w_br_gm'], wf['w_br_s5'],
                                                 wf['w_mix_out'], vec('g_mix_post'), tm)
    mem_n, kk, vv = _f_mem(memf, vec('g_mem'), wf['ca_w_kv'], tmm)
    hc, o_att, ao, x2 = _f_attn(x1, vec('g_ca_pre'), wf['ca_w_q'], kk, vv, wf['ca_w_o'], vec('g_ca_post'),
                                tm, tpb, mlen)
    hf, gu, act = _f_ffn_up(x2, vec('g_ffn_pre'), wf['ffn_w_gu'], tm)
    dn, d3, loss_cols = _f_ffn_down(act, x2, tgt, wf['ffn_w_down'], vec('g_ffn_post'), tm)

    gsm = {}
    gbig = {}
    ddn, dgu, gsm['g_ffn_post'] = _b_ffn_down(d3, dn, gu, vec('g_ffn_post'), wf['ffn_w_down'], tm)
    dx2, gsm['g_ffn_pre'] = _b_ffn_up(dgu, d3, x2, wf['ffn_w_gu'], vec('g_ffn_pre'), tm)
    gbig['ffn_w_down'] = _mm_tn("dw_ffn_down", act, ddn)
    gbig['ffn_w_gu'] = _mm_tn("dw_ffn_gu", hf, dgu, col_shards=4)
    dao, dq, dx1, dk, dv, gsm['g_ca_post'], gsm['g_ca_pre'] = _b_attn(
        dx2, ao, x1, hc, vec('g_ca_post'), wf['ca_w_o'], wf['ca_w_q'], vec('g_ca_pre'), kk, vv, tm, tpb, mlen)
    gbig['ca_w_o'] = _mm_tn("dw_ca_o", o_att, dao)
    gbig['ca_w_q'] = _mm_tn("dw_ca_q", hc, dq)
    dkv, gsm['g_mem'] = _b_mem(dk, dv, memf, wf['ca_w_kv'], vec('g_mem'), tmm)
    gbig['ca_w_kv'] = _mm_tn("dw_ca_kv", mem_n, dkv, col_shards=4)
    dmo, da_br, db_br, dzg, dygm, dgate, dypre, gsm['g_mix_post'] = _b_mix(
        dx1, mo, a_br, b_br, zg, ypre, vec('g_mix_post'), wf['w_mix_out'], wf['w_br_gm'], wf['w_br_s5'],
        wf['s5_w_glu'], tm)
    gbig['w_mix_out'] = _mm_tn("dw_mix_out", merged, dmo)
    gbig['w_br_gm'] = _mm_tn("dw_br_gm", ygm, da_br)
    gbig['w_br_s5'] = _mm_tn("dw_br_s5", ys5, db_br)
    gbig['s5_w_glu'] = _mm_tn("dw_s5_glu", yg, dgate)
    dzs5, dbbc, dccm_t, davec, ddvec = _b_s5(zs5, dypre, states, bbc, ccm, avec, dvec, nb, s5_t)
    dccm = dccm_t.transpose(0, 2, 1)
    dzgm, gsm['gm_w_s'], dbs3, gsm['gm_ln_g'], gsm['gm_ln_b'] = _b_gmlp(
        zgm, dygm, vec('gm_ln_g'), vec('gm_ln_b'), w['gm_w_s'], bs3, tm)
    gsm['gm_b_s'] = dbs3
    grad_x, gsm['g_mix_pre'] = _b_in(dzgm, dzs5, dzg, dx1, xf, wf['w_in'], vec('g_mix_pre'), tm)
    dw_in_gm, dw_in_s5 = _mm_tn("dw_in_gm", h, dzgm), _mm_tn("dw_in_s5", h, dzs5)

    unblk = lambda m_: _block_diag_take(m_, 16, 64).transpose(2, 0, 1, 3).reshape(16, -1)
    d_bb_re, d_bb_im = unblk(dbbc[:, :, :512]), unblk(dbbc[:, :, 512:])
    cunblk = lambda m_: _block_diag_take(m_, 64, 16).transpose(0, 1, 3, 2).reshape(32, 16, 64)
    gsm['s5_c_re'] = cunblk(dccm[:, :512, :])
    gsm['s5_c_im'] = -cunblk(dccm[:, 512:, :])
    d_ab_re, d_ab_im = davec[:, :, :512].reshape(1, -1), davec[:, :, 512:].reshape(1, -1)
    g_lr, g_li, g_ls, g_br, g_bi = _s5_disc_bwd(lr_l, li_l, ls_l, br_t, bi_t, (d_ab_re, d_ab_im, d_bb_re, d_bb_im))
    gsm['s5_lam_re'], gsm['s5_lam_im'] = g_lr.reshape(32, 64), g_li.reshape(32, 64)
    gsm['s5_log_step'] = g_ls.reshape(32, 64).sum(axis=1)
    from_t = lambda t: t.reshape(16, 32, 64).transpose(1, 2, 0)
    gsm['s5_b_re'], gsm['s5_b_im'] = from_t(g_br), from_t(g_bi)
    gsm['s5_d'] = ddvec.reshape(32, 16)

    small_shapes = [w[k].shape for k in SMALL]
    spack = _pack([gsm[k].reshape(w[k].shape) for k in SMALL] + [loss_cols], 8)
    rs = spack.shape[0]
    me_slot = 4 * lax.axis_index("x") + 2 * lax.axis_index("y") + lax.axis_index("c")
    dw_in_g, (sall,) = _mm_tn("dw_in_g", h, dzg, comm=_gather_all(spack))
    gbig['w_in'] = jnp.concatenate([dw_in_gm, dw_in_s5, dw_in_g], axis=1)
    sall = lax.dynamic_update_index_in_dim(sall, spack, me_slot, 0)
    ssum = _sum_slots("sum_small", sall)
    small_red = _unpack(ssum, small_shapes + [(1, D)])
    loss = 0.5 * jnp.sum(small_red[-1]) / D

    slabs = [gbig[k].reshape(4, -1, D) if gbig[k].ndim == 3 or BIG[k] == 0 else _shard_slabs(gbig[k], 1)
             for k in big_names]
    greds = _reduce_big(slabs)

    res_big = {}
    for k, g in zip(big_names, greds):
        g = g.reshape(local_shapes[k])
        dl, mn, vn = _adamw("adamw_" + k, w[k], g, a['m_' + k][0], a['v_' + k][0], _tile_rows(g.shape[0], 256))
        res_big[k] = (g, dl, mn, vn)
    pad_loss = [jnp.zeros((1, D), F32)]
    wsm = _pack([w[k] for k in SMALL] + pad_loss, 8)
    msm = _pack([a['m_' + k][0] for k in SMALL] + pad_loss, 8)
    vsm = _pack([a['v_' + k][0] for k in SMALL] + pad_loss, 8)
    dl_s, mn_s, vn_s = _adamw("adamw_small", wsm, ssum, msm, vsm, rs)
    res_small = {k: vals for k, vals in zip(SMALL, zip(*[_unpack(p, small_shapes) for p in (ssum, dl_s, mn_s, vn_s)]))}

    res = {**res_big, **res_small}
    outs = [loss, grad_x.reshape(nb, seq, D)]
    for i in range(4):
        outs += [res[k][i][None] for k in WNAMES]
    return tuple(outs)


def _tile_rows(rows, cap=384):
    best = rows
    for t in range(16, min(rows, cap) + 1, 16):
        if rows % t == 0:
            best = t
    return best
```

```python
import math
from typing import Any, NamedTuple

import jax
import jax.numpy as jnp
from jax import lax
from jax.experimental import pallas as pl
from jax.experimental.pallas import tpu as pltpu

F32 = jnp.float32
BF16 = jnp.bfloat16
EPS = 1e-6
D = 1024
GM_CHUNK = 128
GM_GROUPS = 8
S5_W = 512
S5_SUPER = 4
S5_T = 512
HEADS = 4
HEAD_DIM = 256
FFN_H = 2816
LANE = 128
VMEM_LIMIT = 56 * 1024 * 1024
MESH = pl.DeviceIdType.MESH

ADAM_LR, ADAM_B1, ADAM_B2, ADAM_EPS, ADAM_WD, ADAM_STEP = 0.001, 0.9, 0.999, 1e-08, 0.01, 10

WNAMES = ['g_mix_pre', 'w_in', 'gm_ln_g', 'gm_ln_b', 'gm_w_s', 'gm_b_s', 's5_lam_re', 's5_lam_im', 's5_log_step',
          's5_b_re', 's5_b_im', 's5_c_re', 's5_c_im', 's5_d', 's5_w_glu', 'w_br_gm', 'w_br_s5', 'w_mix_out',
          'g_mix_post', 'g_ca_pre', 'g_mem', 'ca_w_q', 'ca_w_kv', 'ca_w_o', 'g_ca_post', 'g_ffn_pre', 'ffn_w_gu',
          'ffn_w_down', 'g_ffn_post']
BIG = {'w_in': 1, 's5_w_glu': 0, 'w_br_gm': 0, 'w_br_s5': 1, 'w_mix_out': 0, 'ca_w_q': 0, 'ca_w_kv': 1,
       'ca_w_o': 0, 'ffn_w_gu': 1, 'ffn_w_down': 0}
SMALL = [n for n in WNAMES if n not in BIG]


def _dot(a, b):
    return jnp.dot(a, b, preferred_element_type=F32)


def _dot_nt(a, b):
    return lax.dot_general(a, b, (((1,), (1,)), ((), ())), preferred_element_type=F32)


def _dot_tn(a, b):
    return lax.dot_general(a, b, (((0,), (0,)), ((), ())), preferred_element_type=F32)


def _rms_fwd(x, g):
    r = lax.rsqrt(jnp.mean(x * x, axis=-1, keepdims=True) + EPS)
    return x * r * g


def _rms_bwd(x, g, dy):
    r = lax.rsqrt(jnp.mean(x * x, axis=-1, keepdims=True) + EPS)
    xh = x * r
    gdy = dy * g
    dx = r * (gdy - xh * jnp.mean(gdy * xh, axis=-1, keepdims=True))
    dg = jnp.sum(dy * xh, axis=0, keepdims=True)
    return dx, dg


_GC = math.sqrt(2.0 / math.pi)


def _gelu(x):
    return 0.5 * x * (1.0 + jnp.tanh(_GC * (x + 0.044715 * x * x * x)))


def _gelu_grad(x):
    t = jnp.tanh(_GC * (x + 0.044715 * x * x * x))
    return 0.5 * (1.0 + t) + 0.5 * x * (1.0 - t * t) * _GC * (1.0 + 3 * 0.044715 * x * x)


def _sig(x):
    return 1.0 / (1.0 + jnp.exp(-x))


def _cscan(br, bi, ar, ai, reverse):
    t = br.shape[0]
    row = lax.broadcasted_iota(jnp.int32, br.shape, 0)
    pr, pi = ar, ai
    sh = 1
    while sh < t:
        if reverse:
            keep = row < t - sh
            rr, ri = pltpu.roll(br, t - sh, 0), pltpu.roll(bi, t - sh, 0)
        else:
            keep = row >= sh
            rr, ri = pltpu.roll(br, sh, 0), pltpu.roll(bi, sh, 0)
        rr = jnp.where(keep, rr, 0.0)
        ri = jnp.where(keep, ri, 0.0)
        br, bi = br + pr * rr - pi * ri, bi + pr * ri + pi * rr
        pr, pi = pr * pr - pi * pi, 2.0 * pr * pi
        sh *= 2
    return br, bi


def _s5_disc(lr, li, ls, br, bi):
    step = jnp.exp(ls)
    mag = jnp.exp(lr * step)
    ab_re = mag * jnp.cos(li * step)
    ab_im = mag * jnp.sin(li * step)
    den = lr * lr + li * li
    nr = ab_re - 1.0
    co_re = (nr * lr + ab_im * li) / den
    co_im = (ab_im * lr - nr * li) / den
    return ab_re, ab_im, co_re * br - co_im * bi, co_re * bi + co_im * br


def _params(sem=None):
    return pltpu.CompilerParams(dimension_semantics=sem, vmem_limit_bytes=VMEM_LIMIT)


def _rowcall(name, body, n_rows, tm, ins, outs, scratch=(), comm=None):
    def spec(kind, shape):
        if kind == 'row':
            return pl.BlockSpec((tm, shape[1]), lambda i: (i, 0))
        if kind in ('full', 'acc'):
            nd = len(shape)
            return pl.BlockSpec(tuple(shape), lambda i: (0,) * nd)
        return kind

    in_specs = [spec(k, a.shape) for a, k in ins]
    out_shape, out_specs = [], []
    for k, s, dt in outs:
        shape = (n_rows, s) if k == 'row' else tuple(s)
        out_shape.append(jax.ShapeDtypeStruct(shape, dt))
        out_specs.append(spec(k, shape))
    args = [a for a, _ in ins]
    if comm is not None:
        return _hosting_call(body, name, (n_rows // tm,), in_specs, out_specs, out_shape, list(scratch),
                             ("arbitrary",), args, comm)
    return pl.pallas_call(
        body, name=name, grid=(n_rows // tm,), in_specs=in_specs, out_specs=out_specs, out_shape=out_shape,
        scratch_shapes=list(scratch), compiler_params=_params(("arbitrary",)),
    )(*args)


def _tile(n, cap):
    if n <= cap:
        return n
    best = LANE
    for k in range(1, n // LANE + 1):
        t = k * LANE
        if n % t == 0 and t <= cap:
            best = t
    return best


def _mm_tn(name, a, b, col_shards=1, comm=None):
    n, k = a.shape
    m = b.shape[1]
    mloc = m // col_shards
    tk, tn, tr = _tile(k, 1536), _tile(mloc, 1536), min(n, 512)
    per = mloc // tn

    def body(a_ref, b_ref, o_ref):
        @pl.when(pl.program_id(2) == 0)
        def _():
            o_ref[...] = jnp.zeros_like(o_ref)
        o_ref[...] += _dot_tn(a_ref[...].astype(BF16), b_ref[...].astype(BF16))

    if col_shards == 1:
        out_spec = pl.BlockSpec((tk, tn), lambda i, j, r: (i, j))
        out_shape = (k, m)
    else:
        out_spec = pl.BlockSpec((None, tk, tn), lambda i, j, r: (j // per, i, j % per))
        out_shape = (col_shards, k, mloc)
    grid = (k // tk, m // tn, n // tr)
    in_specs = [pl.BlockSpec((tr, tk), lambda i, j, r: (r, i)), pl.BlockSpec((tr, tn), lambda i, j, r: (r, j))]
    if comm is not None:
        (res,), extra = _hosting_call(body, name, grid, in_specs, [out_spec], [jax.ShapeDtypeStruct(out_shape, F32)],
                                      [], ("arbitrary", "arbitrary", "arbitrary"), (a, b), comm)
        return res, extra
    return pl.pallas_call(
        body, name=name, grid=grid, in_specs=in_specs, out_specs=out_spec,
        out_shape=jax.ShapeDtypeStruct(out_shape, F32),
        compiler_params=_params(("parallel", "parallel", "arbitrary")),
    )(a, b)


def _comm_call(name, srcs, outs, nsem, body_fn, aliases=None):
    ns, no = len(srcs), len(outs)

    def body(*refs):
        body_fn(refs[:ns], refs[ns:ns + no], refs[ns + no], refs[ns + no + 1])

    hbm = pl.BlockSpec(memory_space=pltpu.HBM)
    return pl.pallas_call(
        body, name=name, in_specs=[hbm] * ns, out_specs=[hbm] * no,
        out_shape=[jax.ShapeDtypeStruct(s, d) for s, d in outs],
        scratch_shapes=[pltpu.SemaphoreType.DMA((nsem,)), pltpu.SemaphoreType.DMA((nsem,))],
        input_output_aliases=aliases or {},
    )(*srcs)


def _place():
    x, y, c = lax.axis_index("x"), lax.axis_index("y"), lax.axis_index("c")
    chips = (2 * x + y, 2 * (1 - x) + y, 2 * x + (1 - y), 2 * (1 - x) + (1 - y))
    peers = ((1 - x, y, c), (x, 1 - y, c), (x, y, 1 - c))
    return c, chips, peers


class _Comm(NamedTuple):
    srcs: Any
    outs: Any
    nsem: int
    stages: Any
    aliases: Any
    fracs: Any


class _Lazy:
    def __init__(self, make):
        self._make = make

    def start(self):
        self._make().start()

    def wait(self):
        self._make().wait()

    def wait_recv(self):
        self._make().wait_recv()

    def wait_send(self):
        self._make().wait_send()


def _run_comm(name, comm):
    def body_fn(srcs, dsts, ss, rs):
        for stage in comm.stages:
            stage(srcs, dsts, ss, rs)

    return _comm_call(name, comm.srcs, comm.outs, comm.nsem, body_fn, aliases=comm.aliases)


def _hosting_call(body, name, grid, in_specs, out_specs, out_shape, scratch, sem, args, comm):
    hbm = pl.BlockSpec(memory_space=pltpu.HBM)
    n_in, n_out, n_sc = len(in_specs), len(out_specs), len(scratch)
    ns, no = len(comm.srcs), len(comm.outs)
    total = math.prod(grid)
    steps = [min(total - 1, int(f * total)) for f in comm.fracs]

    def wrapped(*refs):
        ins, csrc = refs[:n_in], refs[n_in:n_in + ns]
        outs, cdst = refs[n_in + ns:n_in + ns + n_out], refs[n_in + ns + n_out:n_in + ns + n_out + no]
        rest = refs[n_in + ns + n_out + no:]
        lin = 0
        for d, size in enumerate(grid):
            lin = lin * size + pl.program_id(d)
        for stage, at in zip(comm.stages, steps):
            @pl.when(lin == at)
            def _(stage=stage):
                stage(csrc, cdst, rest[n_sc], rest[n_sc + 1])
        body(*ins, *outs, *rest[:n_sc])

    res = pl.pallas_call(
        wrapped, name=name, grid=grid, in_specs=list(in_specs) + [hbm] * ns, out_specs=list(out_specs) + [hbm] * no,
        out_shape=list(out_shape) + [jax.ShapeDtypeStruct(s, d) for s, d in comm.outs],
        scratch_shapes=list(scratch) + [pltpu.SemaphoreType.DMA((comm.nsem,)), pltpu.SemaphoreType.DMA((comm.nsem,))],
        input_output_aliases={n_in + i: n_out + o for i, o in comm.aliases.items()},
        compiler_params=_params(sem),
    )(*args, *comm.srcs)
    return res[:n_out], res[n_out:]


def _gather_weights(name, shards, axes):
    nw = len(shards)
    shapes = [s.shape for s in shards]
    outs = [((4,) + s if ax == 0 else (s[0], 4 * s[1]), BF16) for s, ax in zip(shapes, axes)]

    def win(ref, i, chip, start, rows):
        r, cols = shapes[i]
        if axes[i] == 0:
            return ref.at[chip, pl.ds(start, rows), :]
        return ref.at[pl.ds(start, rows), pl.ds(chip * cols, cols)]

    def place_body(*refs):
        srcs, dsts, bufs, sem = refs[:nw], refs[nw:2 * nw], refs[2 * nw:3 * nw], refs[3 * nw]
        me = 2 * lax.axis_index("x") + lax.axis_index("y")
        cps = []
        for i in range(nw):
            bufs[i][...] = srcs[i][...].astype(BF16)
            cps.append(pltpu.make_async_copy(bufs[i], win(dsts[i], i, me, 0, shapes[i][0]), sem.at[i]))
            cps[-1].start()
        for cp in cps:
            cp.wait()

    placed = pl.pallas_call(
        place_body, name=name, out_shape=[jax.ShapeDtypeStruct(s, d) for s, d in outs],
        in_specs=[pl.BlockSpec(memory_space=pltpu.VMEM)] * nw, out_specs=[pl.BlockSpec(memory_space=pltpu.HBM)] * nw,
        scratch_shapes=[pltpu.VMEM(s, BF16) for s in shapes] + [pltpu.SemaphoreType.DMA((nw,))],
        compiler_params=_params(),
    )(*shards)

    def copies(dsts, ss, rs):
        c, (me, xn, yn, dg), (px, py, psib) = _place()

        def rdma(k, window, peer):
            return _Lazy(lambda: pltpu.make_async_remote_copy(
                src_ref=window, dst_ref=window, send_sem=ss.at[k], recv_sem=rs.at[k], device_id=peer,
                device_id_type=MESH))

        first, later, swaps = [], [], []
        for i in range(nw):
            qr, hr, k0 = shapes[i][0] // 4, shapes[i][0] // 2, 6 * i
            q0, q1 = 2 * c * qr, (2 * c + 1) * qr
            w0, w1 = win(dsts[i], i, me, q0, qr), win(dsts[i], i, me, q1, qr)
            first.append([rdma(k0, w0, px), rdma(k0 + 1, w1, py), rdma(k0 + 2, w0, py), rdma(k0 + 3, w1, px)])
            later.append([rdma(k0 + 4, win(dsts[i], i, xn, q0, qr), py), rdma(k0 + 5, win(dsts[i], i, yn, q1, qr), px)])
            swaps.append([rdma(6 * nw + 3 * i + j, win(dsts[i], i, chip, c * hr, hr), psib)
                          for j, chip in enumerate((xn, yn, dg))])
        return first, later, swaps

    def send(srcs, dsts, ss, rs):
        for cps in copies(dsts, ss, rs)[0]:
            for cp in cps:
                cp.start()

    def forward(srcs, dsts, ss, rs):
        first, later, _ = copies(dsts, ss, rs)
        for i in range(nw):
            first[i][0].wait_recv()
            later[i][0].start()
            first[i][1].wait_recv()
            later[i][1].start()

    def swap(srcs, dsts, ss, rs):
        first, later, swaps = copies(dsts, ss, rs)
        for i in range(nw):
            for cp in first[i][2:] + later[i]:
                cp.wait_recv()
            for sw in swaps[i]:
                sw.start()

    def finish(srcs, dsts, ss, rs):
        first, later, swaps = copies(dsts, ss, rs)
        for i in range(nw):
            for sw in swaps[i]:
                sw.wait()
            for cp in first[i] + later[i]:
                cp.wait_send()

    return _Comm(placed, outs, 9 * nw, [send, forward, swap, finish], {i: i for i in range(nw)},
                 (0.0, 0.55, 0.85, 1.0))


def _gather_all(src):
    def copies(srcs, dsts, ss, rs):
        x, y, c = lax.axis_index("x"), lax.axis_index("y"), lax.axis_index("c")
        me, sib = (x, y, c), (x, y, 1 - c)
        chips = [(1 - x, y), (x, 1 - y), (1 - x, 1 - y)]

        def rows(px, py, pc):
            return dsts[0].at[4 * px + 2 * py + pc]

        def copy(k, block, to, src=None):
            return _Lazy(lambda: pltpu.make_async_remote_copy(
                src_ref=rows(*block) if src is None else src, dst_ref=rows(*block), send_sem=ss.at[k],
                recv_sem=rs.at[k], device_id=to, device_id_type=MESH))

        first = [copy(0, me, sib, src=srcs[0])] + [copy(1 + j, me, (*chip, c), src=srcs[0])
                                                   for j, chip in enumerate(chips)]
        passed = [copy(4 + j, (*chip, c), sib) for j, chip in enumerate(chips)]
        landed = [copy(0, sib, me)] + [copy(1 + j, (*chip, c), me) for j, chip in enumerate(chips)]
        landed += [copy(4 + j, (*chip, 1 - c), me) for j, chip in enumerate(chips)]
        return first, passed, landed

    def send(*refs):
        for cp in copies(*refs)[0]:
            cp.start()

    def forward(*refs):
        _, passed, landed = copies(*refs)
        for j in range(3):
            landed[1 + j].wait_recv()
            passed[j].start()

    def finish(*refs):
        first, passed, landed = copies(*refs)
        landed[0].wait_recv()
        for cp in landed[4:]:
            cp.wait_recv()
        for cp in first + passed:
            cp.wait_send()

    return _Comm([src], [((8,) + src.shape, src.dtype)], 7, [send, forward, finish], {}, (0.0, 0.5, 1.0))


def _f_in(x, g, w_in, tm):
    n = x.shape[0]

    def body(x_ref, g_ref, w_ref, h_ref, zgm_ref, zs5_ref, zg_ref):
        h = _rms_fwd(x_ref[...], g_ref[...]).astype(BF16)
        h_ref[...] = h
        zgm_ref[...] = _dot(h, w_ref[:, 0:2 * D])
        zs5_ref[...] = _dot(h, w_ref[:, 2 * D:2 * D + S5_W])
        zg_ref[...] = _dot(h, w_ref[:, 2 * D + S5_W:])

    return _rowcall("f_in", body, n, tm, [(x, 'row'), (g, 'full'), (w_in, 'full')],
                    [('row', D, BF16), ('row', 2 * D, F32), ('row', S5_W, F32), ('row', 2 * D, F32)])


def _tril():
    r = lax.broadcasted_iota(jnp.int32, (GM_CHUNK, GM_CHUNK), 0)
    c = lax.broadcasted_iota(jnp.int32, (GM_CHUNK, GM_CHUNK), 1)
    return r >= c


def _ln_stats(v):
    mu = jnp.mean(v, axis=-1, keepdims=True)
    vc = v - mu
    r = lax.rsqrt(jnp.mean(vc * vc, axis=-1, keepdims=True) + EPS)
    return vc * r, r


def _f_gmlp(zgm, ln_g, ln_b, w_s, b_s, tm):
    n = zgm.shape[0]

    def body(z_ref, lg_ref, lb_ref, ws_ref, bs_ref, y_ref):
        zg = _gelu(z_ref[...])
        u = zg[:, :D]
        vh, _ = _ln_stats(zg[:, D:])
        vn = (vh * lg_ref[...] + lb_ref[...]).astype(BF16)
        keep = _tril()
        for g in range(GM_GROUPS):
            w = jnp.where(keep, ws_ref[g], 0.0).astype(BF16)
            cs = slice(g * LANE, (g + 1) * LANE)
            for c in range(tm // GM_CHUNK):
                rs = slice(c * GM_CHUNK, (c + 1) * GM_CHUNK)
                sv = _dot(w, vn[rs, cs]) + bs_ref[g]
                y_ref[rs, cs] = (u[rs, cs] * sv).astype(BF16)

    return _rowcall("f_gmlp", body, n, tm,
                    [(zgm, 'row'), (ln_g, 'full'), (ln_b, 'full'), (w_s, 'full'), (b_s, 'full')],
                    [('row', D, BF16)])[0]


def _s5_specs(nb, nc, t, rev):
    def cc(c):
        return nc - 1 - c if rev else c
    slab = pl.BlockSpec((t, LANE), lambda j, b, c: (b * nc + cc(c), j))
    bb = pl.BlockSpec((None, LANE, 1024), lambda j, b, c: (j, 0, 0))
    cm = pl.BlockSpec((None, 1024, LANE), lambda j, b, c: (j, 0, 0))
    av = pl.BlockSpec((None, 1, 1024), lambda j, b, c: (j, 0, 0))
    dv = pl.BlockSpec((None, 1, LANE), lambda j, b, c: (j, 0, 0))
    st = pl.BlockSpec((None, None, None, 1, 1024), lambda j, b, c: (j, b, cc(c), 0, 0))
    return slab, bb, cm, av, dv, st


def _seg_load(ref, seg):
    return jnp.concatenate([ref[pl.ds(k, 8, stride=seg), :] for k in range(seg)], axis=0)


def _seg_store(ref, val, seg):
    for k in range(seg):
        ref[pl.ds(k, 8, stride=seg), :] = val[8 * k:8 * k + 8, :]


def _seg_scan(s_ref, ar, ai, seg, reverse):
    def step(i, carry):
        k = seg - 1 - i if reverse else i
        sr, si = carry
        nr = ar * sr - ai * si + s_ref[k, :, :512]
        ni = ar * si + ai * sr + s_ref[k, :, 512:]
        s_ref[k, :, :512] = nr
        s_ref[k, :, 512:] = ni
        return nr, ni

    zero = jnp.zeros((8, 512), F32)
    return lax.fori_loop(0, seg, step, (zero, zero), unroll=4)


def _seg_entries(tr, ti, pr, pi, cin, reverse):
    row = lax.broadcasted_iota(jnp.int32, (8, 512), 0)
    edge = row == (7 if reverse else 0)
    cr, ci = cin[:, :512], cin[:, 512:]
    xr = tr + jnp.where(edge, pr * cr - pi * ci, 0.0)
    xi = ti + jnp.where(edge, pr * ci + pi * cr, 0.0)
    ir, ii = _cscan(xr, xi, pr, pi, reverse)
    shift = 7 if reverse else 1
    er = jnp.where(edge, cr, pltpu.roll(ir, shift, 0))
    ei = jnp.where(edge, ci, pltpu.roll(ii, shift, 0))
    far = row == (0 if reverse else 7)
    out = jnp.concatenate([jnp.sum(jnp.where(far, ir, 0.0), axis=0, keepdims=True),
                           jnp.sum(jnp.where(far, ii, 0.0), axis=0, keepdims=True)], axis=1)
    return er, ei, out


def _seg_power(ar, ai, seg):
    pr, pi = ar, ai
    for _ in range(seg.bit_length() - 1):
        pr, pi = pr * pr - pi * pi, 2.0 * pr * pi
    return pr, pi


def _f_s5(zs5, bbc, ccm, avec, dvec, nb, t, comm):
    n = zs5.shape[0]
    nc = n // nb // t
    seg = t // 8
    slab, bb, cm, av, dv, st = _s5_specs(nb, nc, t, False)

    def body(u_ref, bb_ref, cc_ref, a_ref, d_ref, y_ref, st_ref, carry, s_ref):
        @pl.when(pl.program_id(2) == 0)
        def _():
            carry[...] = jnp.zeros_like(carry)
        cin = carry[...]
        st_ref[...] = cin
        up = _seg_load(u_ref, seg)
        s_ref[...] = _dot(up.astype(BF16), bb_ref[...]).reshape(seg, 8, 1024)
        a1r, a1i = a_ref[:, :512], a_ref[:, 512:]
        ar, ai = jnp.broadcast_to(a1r, (8, 512)), jnp.broadcast_to(a1i, (8, 512))
        tr, ti = _seg_scan(s_ref, ar, ai, seg, False)
        er, ei, cout = _seg_entries(tr, ti, *_seg_power(a1r, a1i, seg), cin, False)
        carry[...] = cout

        def apply(k, pe):
            pr, pi = pe
            pr, pi = ar * pr - ai * pi, ar * pi + ai * pr
            s_ref[k, :, :512] = s_ref[k, :, :512] + pr
            s_ref[k, :, 512:] = s_ref[k, :, 512:] + pi
            return pr, pi

        lax.fori_loop(0, seg, apply, (er, ei), unroll=4)
        s = s_ref[...].reshape(t, 1024).astype(BF16)
        _seg_store(y_ref, _dot(s, cc_ref[...]) + d_ref[...] * up, seg)

    return _hosting_call(
        body, "f_s5", (S5_SUPER, nb, nc), [slab, bb, cm, av, dv], [slab, st],
        [jax.ShapeDtypeStruct((n, S5_W), F32), jax.ShapeDtypeStruct((S5_SUPER, nb, nc, 1, 1024), F32)],
        [pltpu.VMEM((1, 1024), F32), pltpu.VMEM((seg, 8, 1024), F32)], ("arbitrary", "arbitrary", "arbitrary"),
        (zs5, bbc, ccm, avec, dvec), comm)


def _f_mix(ygm, ypre, zg, x, w_glu, w_br_gm, w_br_s5, w_mix_out, g_post, tm):
    n = x.shape[0]

    def body(ygm_ref, ypre_ref, zg_ref, x_ref, wglu_ref, wgm_ref, ws5_ref, wout_ref, g_ref,
             yg_ref, ys5_ref, a_ref, b_ref, mg_ref, mo_ref, x1_ref):
        yg = _gelu(ypre_ref[...])
        ygb = yg.astype(BF16)
        yg_ref[...] = ygb
        ys5 = (yg * _sig(_dot(ygb, wglu_ref[...]))).astype(BF16)
        ys5_ref[...] = ys5
        a = _dot(ygm_ref[...], wgm_ref[...])
        b = _dot(ys5, ws5_ref[...])
        a_ref[...] = a.astype(BF16)
        b_ref[...] = b.astype(BF16)
        zg = zg_ref[...]
        merged = (_sig(zg[:, :D]) * a + _sig(zg[:, D:]) * b).astype(BF16)
        mg_ref[...] = merged
        mo = _dot(merged, wout_ref[...])
        mo_ref[...] = mo
        x1_ref[...] = x_ref[...] + _rms_fwd(mo, g_ref[...])

    return _rowcall("f_mix", body, n, tm,
                    [(ygm, 'row'), (ypre, 'row'), (zg, 'row'), (x, 'row'), (w_glu, 'full'), (w_br_gm, 'full'),
                     (w_br_s5, 'full'), (w_mix_out, 'full'), (g_post, 'full')],
                    [('row', S5_W, BF16), ('row', S5_W, BF16), ('row', D, BF16), ('row', D, BF16),
                     ('row', D, BF16), ('row', D, F32), ('row', D, F32)])


def _f_mem(mem, g_mem, w_kv, tm):
    n = mem.shape[0]

    def body(m_ref, g_ref, w_ref, mn_ref, k_ref, v_ref):
        mn = _rms_fwd(m_ref[...], g_ref[...]).astype(BF16)
        mn_ref[...] = mn
        k_ref[...] = _dot(mn, w_ref[:, :D]).astype(BF16)
        v_ref[...] = _dot(mn, w_ref[:, D:]).astype(BF16)

    return _rowcall("f_mem", body, n, tm, [(mem, 'row'), (g_mem, 'full'), (w_kv, 'full')],
                    [('row', D, BF16), ('row', D, BF16), ('row', D, BF16)])


def _softmax(s):
    m = jnp.max(s, axis=-1, keepdims=True)
    e = jnp.exp(s - m)
    return e / jnp.sum(e, axis=-1, keepdims=True)


def _f_attn(x1, g_pre, w_q, k, v, w_o, g_post, tm, tpb, mlen):
    n = x1.shape[0]
    kv_spec = pl.BlockSpec((mlen, D), lambda i: (i // tpb, 0))
    scale = HEAD_DIM ** -0.5

    def body(x_ref, gp_ref, wq_ref, k_ref, v_ref, wo_ref, go_ref, hc_ref, o_ref, ao_ref, x2_ref):
        x1v = x_ref[...]
        hc = _rms_fwd(x1v, gp_ref[...]).astype(BF16)
        hc_ref[...] = hc
        q = _dot(hc, wq_ref[...])
        for h in range(HEADS):
            hs = slice(h * HEAD_DIM, (h + 1) * HEAD_DIM)
            p = _softmax(_dot_nt(q[:, hs].astype(BF16), k_ref[:, hs]) * scale)
            o_ref[:, hs] = _dot(p.astype(BF16), v_ref[:, hs]).astype(BF16)
        ao = _dot(o_ref[...], wo_ref[...])
        ao_ref[...] = ao
        x2_ref[...] = x1v + _rms_fwd(ao, go_ref[...])

    return _rowcall("f_attn", body, n, tm,
                    [(x1, 'row'), (g_pre, 'full'), (w_q, 'full'), (k, kv_spec), (v, kv_spec), (w_o, 'full'),
                     (g_post, 'full')],
                    [('row', D, BF16), ('row', D, BF16), ('row', D, F32), ('row', D, F32)])


def _f_ffn_up(x2, g_pre, w_gu, tm):
    n = x2.shape[0]

    def body(x_ref, g_ref, w_ref, hf_ref, gu_ref, act_ref):
        hf = _rms_fwd(x_ref[...], g_ref[...]).astype(BF16)
        hf_ref[...] = hf
        gg = _dot(hf, w_ref[:, :FFN_H])
        uu = _dot(hf, w_ref[:, FFN_H:])
        gu_ref[:, :FFN_H] = gg.astype(BF16)
        gu_ref[:, FFN_H:] = uu.astype(BF16)
        act_ref[...] = (gg * _sig(gg) * uu).astype(BF16)

    return _rowcall("f_ffn_up", body, n, tm, [(x2, 'row'), (g_pre, 'full'), (w_gu, 'full')],
                    [('row', D, BF16), ('row', 2 * FFN_H, BF16), ('row', FFN_H, BF16)])


def _f_ffn_down(act, x2, tgt, w_down, g_post, tm):
    n = x2.shape[0]

    def body(a_ref, x_ref, t_ref, w_ref, g_ref, dn_ref, d3_ref, loss_ref):
        @pl.when(pl.program_id(0) == 0)
        def _():
            loss_ref[...] = jnp.zeros_like(loss_ref)
        dn = _dot(a_ref[...], w_ref[...])
        dn_ref[...] = dn
        err = x_ref[...] + _rms_fwd(dn, g_ref[...]) - t_ref[...]
        d3_ref[...] = err * (1.0 / D)
        loss_ref[...] += jnp.sum(err * err, axis=0, keepdims=True)

    return _rowcall("f_ffn_down", body, n, tm,
                    [(act, 'row'), (x2, 'row'), (tgt, 'row'), (w_down, 'full'), (g_post, 'full')],
                    [('row', D, F32), ('row', D, F32), ('acc', (1, D), F32)])


def _b_ffn_down(d3, dn, gu, g_post, w_down, tm):
    n = d3.shape[0]

    def body(d3_ref, dn_ref, gu_ref, g_ref, w_ref, ddn_ref, dgu_ref, dg_ref):
        @pl.when(pl.program_id(0) == 0)
        def _():
            dg_ref[...] = jnp.zeros_like(dg_ref)
        ddn, dg = _rms_bwd(dn_ref[...], g_ref[...], d3_ref[...])
        dg_ref[...] += dg
        ddn = ddn.astype(BF16)
        ddn_ref[...] = ddn
        dact = _dot_nt(ddn, w_ref[...])
        gg = gu_ref[:, :FFN_H].astype(F32)
        uu = gu_ref[:, FFN_H:].astype(F32)
        sg = _sig(gg)
        dgu_ref[:, :FFN_H] = (dact * uu * sg * (1.0 + gg * (1.0 - sg))).astype(BF16)
        dgu_ref[:, FFN_H:] = (dact * gg * sg).astype(BF16)

    return _rowcall("b_ffn_down", body, n, tm,
                    [(d3, 'row'), (dn, 'row'), (gu, 'row'), (g_post, 'full'), (w_down, 'full')],
                    [('row', D, BF16), ('row', 2 * FFN_H, BF16), ('acc', (1, D), F32)])


def _b_ffn_up(dgu, d3, x2, w_gu, g_pre, tm):
    n = d3.shape[0]

    def body(dgu_ref, d3_ref, x_ref, w_ref, g_ref, dx_ref, dg_ref):
        @pl.when(pl.program_id(0) == 0)
        def _():
            dg_ref[...] = jnp.zeros_like(dg_ref)
        dhf = _dot_nt(dgu_ref[...], w_ref[...])
        dx, dg = _rms_bwd(x_ref[...], g_ref[...], dhf)
        dg_ref[...] += dg
        dx_ref[...] = d3_ref[...] + dx

    return _rowcall("b_ffn_up", body, n, tm,
                    [(dgu, 'row'), (d3, 'row'), (x2, 'row'), (w_gu, 'full'), (g_pre, 'full')],
                    [('row', D, F32), ('acc', (1, D), F32)])


def _b_attn(dx2, ao, x1, hc, g_post, w_o, w_q, g_pre, k, v, tm, tpb, mlen):
    n = dx2.shape[0]
    nb = n // (tm * tpb)
    kv_spec = pl.BlockSpec((mlen, D), lambda i: (i // tpb, 0))
    scale = HEAD_DIM ** -0.5

    def body(dx2_ref, ao_ref, x1_ref, hc_ref, go_ref, wo_ref, wq_ref, gp_ref, k_ref, v_ref,
             dao_ref, dq_ref, dx1_ref, dk_ref, dv_ref, dgo_ref, dgp_ref):
        i = pl.program_id(0)

        @pl.when(i == 0)
        def _():
            dgo_ref[...] = jnp.zeros_like(dgo_ref)
            dgp_ref[...] = jnp.zeros_like(dgp_ref)

        @pl.when(i % tpb == 0)
        def _():
            dk_ref[...] = jnp.zeros_like(dk_ref)
            dv_ref[...] = jnp.zeros_like(dv_ref)

        dx2v = dx2_ref[...]
        dao, dgo = _rms_bwd(ao_ref[...], go_ref[...], dx2v)
        dgo_ref[...] += dgo
        dao = dao.astype(BF16)
        dao_ref[...] = dao
        do = _dot_nt(dao, wo_ref[...])
        q = _dot(hc_ref[...], wq_ref[...])
        for h in range(HEADS):
            hs = slice(h * HEAD_DIM, (h + 1) * HEAD_DIM)
            qh = q[:, hs].astype(BF16)
            kh = k_ref[:, hs]
            p = _softmax(_dot_nt(qh, kh) * scale)
            doh = do[:, hs].astype(BF16)
            dp = _dot_nt(doh, v_ref[:, hs])
            ds = (p * (dp - jnp.sum(dp * p, axis=-1, keepdims=True)) * scale).astype(BF16)
            dq_ref[:, hs] = _dot(ds, kh).astype(BF16)
            dk_ref[:, hs] += _dot_tn(ds, qh)
            dv_ref[:, hs] += _dot_tn(p.astype(BF16), doh)
        dhc = _dot_nt(dq_ref[...], wq_ref[...])
        dx, dgp = _rms_bwd(x1_ref[...], gp_ref[...], dhc)
        dgp_ref[...] += dgp
        dx1_ref[...] = dx2v + dx

    return _rowcall("b_attn", body, n, tm,
                    [(dx2, 'row'), (ao, 'row'), (x1, 'row'), (hc, 'row'), (g_post, 'full'), (w_o, 'full'),
                     (w_q, 'full'), (g_pre, 'full'), (k, kv_spec), (v, kv_spec)],
                    [('row', D, BF16), ('row', D, BF16), ('row', D, F32),
                     (kv_spec, (nb * mlen, D), F32), (kv_spec, (nb * mlen, D), F32),
                     ('acc', (1, D), F32), ('acc', (1, D), F32)])


def _b_mem(dk, dv, mem, w_kv, g_mem, tm):
    n = mem.shape[0]

    def body(dk_ref, dv_ref, m_ref, w_ref, g_ref, dkv_ref, dg_ref):
        @pl.when(pl.program_id(0) == 0)
        def _():
            dg_ref[...] = jnp.zeros_like(dg_ref)
        dkb = dk_ref[...].astype(BF16)
        dvb = dv_ref[...].astype(BF16)
        dkv_ref[:, :D] = dkb
        dkv_ref[:, D:] = dvb
        dmn = _dot_nt(dkb, w_ref[:, :D]) + _dot_nt(dvb, w_ref[:, D:])
        _, dg = _rms_bwd(m_ref[...], g_ref[...], dmn)
        dg_ref[...] += dg

    return _rowcall("b_mem", body, n, tm, [(dk, 'row'), (dv, 'row'), (mem, 'row'), (w_kv, 'full'), (g_mem, 'full')],
                    [('row', 2 * D, BF16), ('acc', (1, D), F32)])


def _b_mix(dx1, mo, a, b, zg, ypre, g_post, w_mix_out, w_br_gm, w_br_s5, w_glu, tm):
    n = dx1.shape[0]

    def body(dx1_ref, mo_ref, a_ref, b_ref, zg_ref, ypre_ref, g_ref, wout_ref, wgm_ref, ws5_ref, wglu_ref,
             dmo_ref, da_ref, db_ref, dzg_ref, dygm_ref, dgate_ref, dypre_ref, dg_ref):
        @pl.when(pl.program_id(0) == 0)
        def _():
            dg_ref[...] = jnp.zeros_like(dg_ref)
        dmo, dg = _rms_bwd(mo_ref[...], g_ref[...], dx1_ref[...])
        dg_ref[...] += dg
        dmo = dmo.astype(BF16)
        dmo_ref[...] = dmo
        dmg = _dot_nt(dmo, wout_ref[...])
        zg = zg_ref[...]
        sa, sb = _sig(zg[:, :D]), _sig(zg[:, D:])
        da = (dmg * sa).astype(BF16)
        db = (dmg * sb).astype(BF16)
        da_ref[...] = da
        db_ref[...] = db
        dzg_ref[:, :D] = (dmg * a_ref[...].astype(F32) * sa * (1.0 - sa)).astype(BF16)
        dzg_ref[:, D:] = (dmg * b_ref[...].astype(F32) * sb * (1.0 - sb)).astype(BF16)
        dygm_ref[...] = _dot_nt(da, wgm_ref[...])
        dys5 = _dot_nt(db, ws5_ref[...])
        ypre = ypre_ref[...]
        yg = _gelu(ypre)
        sgt = _sig(_dot(yg.astype(BF16), wglu_ref[...]))
        dgate = (dys5 * yg * sgt * (1.0 - sgt)).astype(BF16)
        dgate_ref[...] = dgate
        dyg = dys5 * sgt + _dot_nt(dgate, wglu_ref[...])
        dypre_ref[...] = dyg * _gelu_grad(ypre)

    return _rowcall("b_mix", body, n, tm,
                    [(dx1, 'row'), (mo, 'row'), (a, 'row'), (b, 'row'), (zg, 'row'), (ypre, 'row'), (g_post, 'full'),
                     (w_mix_out, 'full'), (w_br_gm, 'full'), (w_br_s5, 'full'), (w_glu, 'full')],
                    [('row', D, BF16), ('row', D, BF16), ('row', D, BF16), ('row', 2 * D, BF16), ('row', D, F32),
                     ('row', S5_W, BF16), ('row', S5_W, F32), ('acc', (1, D), F32)])


def _b_s5(zs5, dypre, states, bbc, ccm, avec, dvec, nb, t, comm):
    n = zs5.shape[0]
    nc = n // nb // t
    seg = t // 8
    slab, bb, cm, av, dv, st = _s5_specs(nb, nc, t, True)

    def body(u_ref, dy_ref, st_ref, bb_ref, cc_ref, a_ref, d_ref,
             du_ref, dbb_ref, dcc_ref, da_ref, dd_ref, lcarry, s_ref, l_ref):
        first = jnp.logical_and(pl.program_id(1) == 0, pl.program_id(2) == 0)

        @pl.when(first)
        def _():
            dbb_ref[...] = jnp.zeros_like(dbb_ref)
            dcc_ref[...] = jnp.zeros_like(dcc_ref)
            da_ref[...] = jnp.zeros_like(da_ref)
            dd_ref[...] = jnp.zeros_like(dd_ref)

        @pl.when(pl.program_id(2) == 0)
        def _():
            lcarry[...] = jnp.zeros_like(lcarry)

        up, dyp = _seg_load(u_ref, seg), _seg_load(dy_ref, seg)
        ub, dyb = up.astype(BF16), dyp.astype(BF16)
        a1r, a1i = a_ref[:, :512], a_ref[:, 512:]
        ar, ai = jnp.broadcast_to(a1r, (8, 512)), jnp.broadcast_to(a1i, (8, 512))
        pr, pi = _seg_power(a1r, a1i, seg)

        s_ref[...] = _dot(ub, bb_ref[...]).reshape(seg, 8, 1024)
        tr, ti = _seg_scan(s_ref, ar, ai, seg, False)
        er, ei, _ = _seg_entries(tr, ti, pr, pi, st_ref[...], False)

        def apply(k, pe):
            qr, qi = pe
            qr, qi = ar * qr - ai * qi, ar * qi + ai * qr
            s_ref[k, :, :512] = s_ref[k, :, :512] + qr
            s_ref[k, :, 512:] = s_ref[k, :, 512:] + qi
            return qr, qi

        lax.fori_loop(0, seg, apply, (er, ei), unroll=4)

        l_ref[...] = _dot_nt(dyb, cc_ref[...]).reshape(seg, 8, 1024)
        tr, ti = _seg_scan(l_ref, ar, -ai, seg, True)
        fr, fi, lout = _seg_entries(tr, ti, pr, -pi, lcarry[...], True)
        lcarry[...] = lout

        def apply_back(i, carry):
            qr, qi, accr, acci = carry
            k = seg - 1 - i
            qr, qi = ar * qr + ai * qi, ar * qi - ai * qr
            lr = l_ref[k, :, :512] + qr
            li = l_ref[k, :, 512:] + qi
            l_ref[k, :, :512] = lr
            l_ref[k, :, 512:] = li
            kp = jnp.maximum(k - 1, 0)
            sr = jnp.where(k == 0, er, s_ref[kp, :, :512])
            si = jnp.where(k == 0, ei, s_ref[kp, :, 512:])
            return qr, qi, accr + lr * sr + li * si, acci + li * sr - lr * si

        zero = jnp.zeros((8, 512), F32)
        _, _, accr, acci = lax.fori_loop(0, seg, apply_back, (fr, fi, zero, zero), unroll=4)
        da_ref[:, :512] += jnp.sum(accr, axis=0, keepdims=True)
        da_ref[:, 512:] += jnp.sum(acci, axis=0, keepdims=True)

        s = s_ref[...].reshape(t, 1024).astype(BF16)
        lam = l_ref[...].reshape(t, 1024).astype(BF16)
        dcc_ref[...] += _dot_tn(dyb, s)
        dbb_ref[...] += _dot_tn(ub, lam)
        _seg_store(du_ref, _dot_nt(lam, bb_ref[...]) + d_ref[...] * dyp, seg)
        dd_ref[...] += jnp.sum(dyp * up, axis=0, keepdims=True)

    return _hosting_call(
        body, "b_s5", (S5_SUPER, nb, nc), [slab, slab, st, bb, cm, av, dv], [slab, bb, bb, av, dv],
        [jax.ShapeDtypeStruct((n, S5_W), F32), jax.ShapeDtypeStruct(bbc.shape, F32),
         jax.ShapeDtypeStruct(bbc.shape, F32), jax.ShapeDtypeStruct(avec.shape, F32),
         jax.ShapeDtypeStruct(dvec.shape, F32)],
        [pltpu.VMEM((1, 1024), F32), pltpu.VMEM((seg, 8, 1024), F32), pltpu.VMEM((seg, 8, 1024), F32)],
        ("arbitrary", "arbitrary", "arbitrary"), (zs5, dypre, states, bbc, ccm, avec, dvec), comm)


def _b_gmlp(zgm, dygm, ln_g, ln_b, w_s, b_s, tm, comm):
    n = zgm.shape[0]

    def body(z_ref, dy_ref, lg_ref, lb_ref, ws_ref, bs_ref, dz_ref, dws_ref, dbs_ref, dlg_ref, dlb_ref,
             du_s, dvn_s):
        @pl.when(pl.program_id(0) == 0)
        def _():
            dws_ref[...] = jnp.zeros_like(dws_ref)
            dbs_ref[...] = jnp.zeros_like(dbs_ref)
            dlg_ref[...] = jnp.zeros_like(dlg_ref)
            dlb_ref[...] = jnp.zeros_like(dlb_ref)
        z = z_ref[...]
        zg = _gelu(z)
        u = zg[:, :D]
        vh, r = _ln_stats(zg[:, D:])
        vn = (vh * lg_ref[...] + lb_ref[...]).astype(BF16)
        dy = dy_ref[...]
        keep = _tril()
        for g in range(GM_GROUPS):
            w = jnp.where(keep, ws_ref[g], 0.0).astype(BF16)
            cs = slice(g * LANE, (g + 1) * LANE)
            for c in range(tm // GM_CHUNK):
                rs = slice(c * GM_CHUNK, (c + 1) * GM_CHUNK)
                vb = vn[rs, cs]
                sv = _dot(w, vb) + bs_ref[g]
                dyb = dy[rs, cs]
                du_s[rs, cs] = dyb * sv
                dsv = dyb * u[rs, cs]
                dsvb = dsv.astype(BF16)
                dvn_s[rs, cs] = _dot_tn(w, dsvb)
                dws_ref[g] += jnp.where(keep, _dot_nt(dsvb, vb), 0.0)
                dbs_ref[g] += jnp.sum(dsv, axis=1, keepdims=True)
        dvn = dvn_s[...]
        dlg_ref[...] += jnp.sum(dvn * vh, axis=0, keepdims=True)
        dlb_ref[...] += jnp.sum(dvn, axis=0, keepdims=True)
        dvh = dvn * lg_ref[...]
        dv = r * (dvh - jnp.mean(dvh, axis=-1, keepdims=True) - vh * jnp.mean(dvh * vh, axis=-1, keepdims=True))
        dz_ref[:, :D] = (du_s[...] * _gelu_grad(z[:, :D])).astype(BF16)
        dz_ref[:, D:] = (dv * _gelu_grad(z[:, D:])).astype(BF16)

    return _rowcall("b_gmlp", body, n, tm,
                    [(zgm, 'row'), (dygm, 'row'), (ln_g, 'full'), (ln_b, 'full'), (w_s, 'full'), (b_s, 'full')],
                    [('row', 2 * D, BF16), ('acc', w_s.shape, F32), ('acc', b_s.shape, F32), ('acc', (1, D), F32),
                     ('acc', (1, D), F32)],
                    scratch=[pltpu.VMEM((tm, D), F32), pltpu.VMEM((tm, D), F32)], comm=comm)


def _b_in(dzgm, dzs5, dzg, dx1, x, w_in, g_pre, tm, comm):
    n = x.shape[0]

    def body(d1_ref, d2_ref, d3_ref, dx1_ref, x_ref, w_ref, g_ref, gx_ref, dg_ref):
        @pl.when(pl.program_id(0) == 0)
        def _():
            dg_ref[...] = jnp.zeros_like(dg_ref)
        dh = (_dot_nt(d1_ref[...], w_ref[:, 0:2 * D]) + _dot_nt(d2_ref[...].astype(BF16), w_ref[:, 2 * D:2 * D + S5_W])
              + _dot_nt(d3_ref[...], w_ref[:, 2 * D + S5_W:]))
        dx, dg = _rms_bwd(x_ref[...], g_ref[...], dh)
        dg_ref[...] += dg
        gx_ref[...] = dx1_ref[...] + dx

    return _rowcall("b_in", body, n, tm,
                    [(dzgm, 'row'), (dzs5, 'row'), (dzg, 'row'), (dx1, 'row'), (x, 'row'), (w_in, 'full'),
                     (g_pre, 'full')],
                    [('row', D, F32), ('acc', (1, D), F32)], comm=comm)


def _whole(name, body, ins, outs):
    return pl.pallas_call(body, name=name, out_shape=[jax.ShapeDtypeStruct(s, dt) for s, dt in outs],
                          compiler_params=_params())(*ins)


def _s5_disc_fwd(lr, li, ls, br, bi):
    def body(lr_ref, li_ref, ls_ref, br_ref, bi_ref, o1, o2, o3, o4):
        outs = _s5_disc(lr_ref[...], li_ref[...], ls_ref[...], br_ref[...], bi_ref[...])
        for o, val in zip((o1, o2, o3, o4), outs):
            o[...] = val

    return _whole("s5_disc_fwd", body, [lr, li, ls, br, bi],
                  [(lr.shape, F32), (lr.shape, F32), (br.shape, F32), (br.shape, F32)])


def _s5_disc_bwd(lr, li, ls, br, bi, cts):
    def body(lr_ref, li_ref, ls_ref, br_ref, bi_ref, c1, c2, c3, c4, o1, o2, o3, o4, o5):
        _, vjp = jax.vjp(_s5_disc, lr_ref[...], li_ref[...], ls_ref[...], br_ref[...], bi_ref[...])
        grads = vjp((c1[...], c2[...], c3[...], c4[...]))
        for o, val in zip((o1, o2, o3, o4, o5), grads):
            o[...] = val

    return _whole("s5_disc_bwd", body, [lr, li, ls, br, bi, *cts],
                  [(lr.shape, F32), (lr.shape, F32), (lr.shape, F32), (br.shape, F32), (br.shape, F32)])


def _sum_slots(name, slots):
    def body(s_ref, o_ref):
        acc = s_ref[0]
        for k in range(1, slots.shape[0]):
            acc = acc + s_ref[k]
        o_ref[...] = acc

    return _whole(name, body, [slots], [(slots.shape[1:], F32)])[0]


def _adamw(name, w, g, m, v, tm):
    c1 = 1.0 - ADAM_B1 ** ADAM_STEP
    c2 = 1.0 - ADAM_B2 ** ADAM_STEP

    def body(w_ref, g_ref, m_ref, v_ref, d_ref, mo_ref, vo_ref):
        gg = g_ref[...]
        mn = ADAM_B1 * m_ref[...] + (1.0 - ADAM_B1) * gg
        vn = ADAM_B2 * v_ref[...] + (1.0 - ADAM_B2) * (gg * gg)
        mo_ref[...] = mn
        vo_ref[...] = vn
        d_ref[...] = -ADAM_LR * ((mn / c1) / (jnp.sqrt(vn / c2) + ADAM_EPS) + ADAM_WD * w_ref[...])

    cols = w.shape[1]
    return _rowcall(name, body, w.shape[0], tm, [(w, 'row'), (g, 'row'), (m, 'row'), (v, 'row')],
                    [('row', cols, F32), ('row', cols, F32), ('row', cols, F32)])


def _picked_rowcall(name, body, grid, in_specs, out_cols, tm):
    return pl.pallas_call(
        lambda p_ref, *refs: body(*refs), name=name,
        grid_spec=pltpu.PrefetchScalarGridSpec(
            num_scalar_prefetch=1, grid=(grid,), in_specs=in_specs,
            out_specs=[pl.BlockSpec((tm, D), lambda i, p: (i, 0)) for _ in out_cols]),
        out_shape=[jax.ShapeDtypeStruct((grid * tm, D), dt) for dt in out_cols],
        compiler_params=_params(("arbitrary",)),
    )


def _exchange(copies):
    def build(src_refs, out_refs, ss, rs):
        def rdma(k, src, dst, peer):
            return _Lazy(lambda: pltpu.make_async_remote_copy(
                src_ref=src, dst_ref=dst, send_sem=ss.at[k], recv_sem=rs.at[k], device_id=peer, device_id_type=MESH))
        return copies(src_refs, out_refs, rdma)

    def start(*refs):
        for cp in build(*refs):
            cp.start()

    def finish(*refs):
        for cp in build(*refs):
            cp.wait()

    return start, finish


def _reduce_big(tag, slabs):
    rows = [s.shape[1] for s in slabs]
    blk = sum(rows) // 4
    tm = _tile_rows(blk, 512)
    per = blk // tm
    gpack = jnp.concatenate([s.reshape(4, 2, 2, r // 4, D) for s, r in zip(slabs, rows)], axis=3)

    mx, my, mc = lax.axis_index("x"), lax.axis_index("y"), lax.axis_index("c")
    chip_me, chip_xn, chip_yn = 2 * mx + my, 2 * (1 - mx) + my, 2 * mx + (1 - my)

    def pair_copies(srcs, dsts, rdma):
        c, _, (_, _, psib) = _place()
        return [rdma(0, srcs[0].at[:, 1 - c], dsts[0], psib)]

    from_sib = (yield _Comm([gpack], [((4, 2, blk, D), F32)], 1, _exchange(pair_copies), {}, (0.0, 1.0)))[0]

    def sum_both(a_ref, b_ref, o32_ref, o16_ref):
        s = a_ref[...].astype(F32) + b_ref[...].astype(F32)
        o32_ref[...] = s
        o16_ref[...] = s.astype(BF16)

    p32, p16 = _picked_rowcall(
        "sum_pair_" + tag, sum_both, 8 * per,
        [pl.BlockSpec((None, None, None, tm, D), lambda i, p: (i // (2 * per), p[0], (i // per) % 2, i % per, 0)),
         pl.BlockSpec((tm, D), lambda i, p: (i, 0))],
        [F32, BF16], tm)(jnp.stack([mc]).astype(jnp.int32), gpack, from_sib.reshape(8 * blk, D))
    p32, p16 = p32.reshape(4, 2, blk, D), p16.reshape(4, 2, blk, D)

    def step1_copies(srcs, dsts, rdma):
        _, (me, xn, yn, dg), (px, py, _) = _place()
        s16 = srcs[0]
        return [rdma(0, s16.at[yn, 0], dsts[0].at[0], py), rdma(1, s16.at[dg, 0], dsts[0].at[1], py),
                rdma(2, s16.at[xn, 1], dsts[0].at[2], px), rdma(3, s16.at[dg, 1], dsts[0].at[3], px)]

    recv1 = (yield _Comm([p16], [((4, blk, D), BF16)], 4, _exchange(step1_copies), {}, (0.0, 1.0)))[0]

    s32, s16 = _picked_rowcall(
        "sum_step1_" + tag, sum_both, 4 * per,
        [pl.BlockSpec((None, None, tm, D), lambda i, p: (p[i // per], i // (2 * per), i % per, 0)),
         pl.BlockSpec((tm, D), lambda i, p: (i, 0))],
        [F32, BF16], tm)(jnp.stack([chip_me, chip_xn, chip_me, chip_yn]).astype(jnp.int32), p32,
                         recv1.reshape(4 * blk, D))

    def step2_copies(srcs, dsts, rdma):
        _, _, (px, py, _) = _place()
        return [rdma(0, srcs[0].at[1], dsts[0].at[0], px), rdma(1, srcs[0].at[3], dsts[0].at[1], py)]

    recv2 = (yield _Comm([s16.reshape(4, blk, D)], [((2, blk, D), BF16)], 2, _exchange(step2_copies), {},
                         (0.0, 1.0)))[0]

    def sum_step2(a_ref, b_ref, o_ref):
        o_ref[...] = a_ref[...] + b_ref[...].astype(F32)

    red = _rowcall("sum_step2_" + tag, sum_step2, 2 * blk, tm,
                   [(s32.reshape(4, blk, D), pl.BlockSpec((None, tm, D), lambda i: (2 * (i // per), i % per, 0))),
                    (recv2.reshape(2 * blk, D), 'row')],
                   [('row', D, F32)])[0].reshape(2, blk, D)

    def share_copies(srcs, dsts, rdma):
        _, _, (_, _, psib) = _place()
        return [rdma(0, srcs[0], dsts[0], psib)]

    other = (yield _Comm([red], [((2, blk, D), F32)], 1, _exchange(share_copies), {}, (0.0, 1.0)))[0]
    lo = jnp.where(mc == 0, red, other)
    hi = jnp.where(mc == 0, other, red)
    out, off = [], 0
    for r in rows:
        q = r // 4
        out.append(jnp.concatenate([lo[0, off:off + q], lo[1, off:off + q], hi[0, off:off + q], hi[1, off:off + q]]))
        off += q
    return out


def _rows1024(a):
    flat = a.reshape(-1)
    pad = (-flat.shape[0]) % D
    if pad:
        flat = jnp.concatenate([flat, jnp.zeros((pad,), flat.dtype)])
    return flat.reshape(-1, D)


def _pack(arrs, pad_rows_to=8):
    parts = [_rows1024(a) for a in arrs]
    rows = sum(p.shape[0] for p in parts)
    pad = (-rows) % pad_rows_to
    if pad:
        parts.append(jnp.zeros((pad, D), parts[0].dtype))
    return jnp.concatenate(parts, axis=0)


def _unpack(packed, shapes):
    out, r = [], 0
    for s in shapes:
        size = math.prod(s)
        nr = -(-size // D)
        out.append(packed[r:r + nr].reshape(-1)[:size].reshape(s))
        r += nr
    return out


def _shard_slabs(full, axis):
    r, c = full.shape
    if axis == 0:
        return full.reshape(4, -1, D)
    return full.reshape(r, 4, c // 4).transpose(1, 0, 2).reshape(4, -1, D)


def _block_diag(t):
    eye = jnp.eye(8, dtype=t.dtype)
    j, g, a, b = t.shape
    return (t[:, :, :, None, :] * eye[None, :, None, :, None]).reshape(j, g * a, g * b)


def _block_diag_take(m, a, b):
    eye = jnp.eye(8, dtype=m.dtype)
    return (m.reshape(4, 8, a, 8, b) * eye[None, :, None, :, None]).sum(axis=3)


def kernel(x, mem, g_mix_pre, w_in, gm_ln_g, gm_ln_b, gm_w_s, gm_b_s, s5_lam_re, s5_lam_im, s5_log_step, s5_b_re, s5_b_im, s5_c_re, s5_c_im, s5_d, s5_w_glu, w_br_gm, w_br_s5, w_mix_out, g_mix_post, g_ca_pre, g_mem, ca_w_q, ca_w_kv, ca_w_o, g_ca_post, g_ffn_pre, ffn_w_gu, ffn_w_down, g_ffn_post, loss_target, m_g_mix_pre, m_w_in, m_gm_ln_g, m_gm_ln_b, m_gm_w_s, m_gm_b_s, m_s5_lam_re, m_s5_lam_im, m_s5_log_step, m_s5_b_re, m_s5_b_im, m_s5_c_re, m_s5_c_im, m_s5_d, m_s5_w_glu, m_w_br_gm, m_w_br_s5, m_w_mix_out, m_g_mix_post, m_g_ca_pre, m_g_mem, m_ca_w_q, m_ca_w_kv, m_ca_w_o, m_g_ca_post, m_g_ffn_pre, m_ffn_w_gu, m_ffn_w_down, m_g_ffn_post, v_g_mix_pre, v_w_in, v_gm_ln_g, v_gm_ln_b, v_gm_w_s, v_gm_b_s, v_s5_lam_re, v_s5_lam_im, v_s5_log_step, v_s5_b_re, v_s5_b_im, v_s5_c_re, v_s5_c_im, v_s5_d, v_s5_w_glu, v_w_br_gm, v_w_br_s5, v_w_mix_out, v_g_mix_post, v_g_ca_pre, v_g_mem, v_ca_w_q, v_ca_w_kv, v_ca_w_o, v_g_ca_post, v_g_ffn_pre, v_ffn_w_gu, v_ffn_w_down, v_g_ffn_post):
    a = dict(locals())
    w = {n: a[n][0] for n in WNAMES}
    nb, seq, _ = x.shape
    n = nb * seq
    mlen = mem.shape[1]
    tm = min(256, seq)
    tpb = seq // tm
    xf = x.reshape(n, D)
    tgt = loss_target.reshape(n, D)
    memf = mem.reshape(nb * mlen, D)
    tmm = min(256, mlen)

    big_names = list(BIG)
    local_shapes = {k: w[k].shape for k in big_names}
    w_in_full = _run_comm("gather_w_in", _gather_weights("cast_w_in", [w['w_in']], [BIG['w_in']]))[0]
    rest = [k for k in big_names if k != 'w_in']
    gather_rest = _gather_weights("cast_w_rest", [w[k] for k in rest], [BIG[k] for k in rest])

    vec = lambda name: w[name].reshape(1, D)

    to_lane = lambda p: p.reshape(1, -1)
    b_t = lambda p: p.transpose(2, 0, 1).reshape(16, -1)
    lr_l, li_l = to_lane(w['s5_lam_re']), to_lane(w['s5_lam_im'])
    ls_l = jnp.repeat(w['s5_log_step'], 64).reshape(1, -1)
    br_t, bi_t = b_t(w['s5_b_re']), b_t(w['s5_b_im'])
    ab_re, ab_im, bb_re, bb_im = _s5_disc_fwd(lr_l, li_l, ls_l, br_t, bi_t)
    blk = lambda t: _block_diag(t.reshape(16, 4, 8, 64).transpose(1, 2, 0, 3))
    bbc = jnp.concatenate([blk(bb_re), blk(bb_im)], axis=2).astype(BF16)
    cblk = lambda c: _block_diag(c.reshape(4, 8, 16, 64).transpose(0, 1, 3, 2))
    ccm = jnp.concatenate([cblk(w['s5_c_re']), -cblk(w['s5_c_im'])], axis=1).astype(BF16)
    avec = jnp.concatenate([ab_re.reshape(4, 1, 512), ab_im.reshape(4, 1, 512)], axis=2)
    dvec = w['s5_d'].reshape(4, 1, LANE)

    bs3 = w['gm_b_s'].reshape(GM_GROUPS, GM_CHUNK, 1)
    wf = {'w_in': w_in_full}
    h, zgm, zs5, zg = _f_in(xf, vec('g_mix_pre'), wf['w_in'], tm)
    ygm = _f_gmlp(zgm, vec('gm_ln_g'), vec('gm_ln_b'), w['gm_w_s'], bs3, tm)
    s5_t = min(S5_T, seq)
    (ypre, states), gathered = _f_s5(zs5, bbc, ccm, avec, dvec, nb, s5_t, gather_rest)
    for k, g in zip(rest, gathered):
        wf[k] = g.reshape(-1, g.shape[-1]) if BIG[k] == 0 else g
    yg, ys5, a_br, b_br, merged, mo, x1 = _f_mix(ygm, ypre, zg, xf, wf['s5_w_glu'], wf['w_br_gm'], wf['w_br_s5'],
                                                 wf['w_mix_out'], vec('g_mix_post'), tm)
    mem_n, kk, vv = _f_mem(memf, vec('g_mem'), wf['ca_w_kv'], tmm)
    hc, o_att, ao, x2 = _f_attn(x1, vec('g_ca_pre'), wf['ca_w_q'], kk, vv, wf['ca_w_o'], vec('g_ca_post'),
                                tm, tpb, mlen)
    hf, gu, act = _f_ffn_up(x2, vec('g_ffn_pre'), wf['ffn_w_gu'], tm)
    dn, d3, loss_cols = _f_ffn_down(act, x2, tgt, wf['ffn_w_down'], vec('g_ffn_post'), tm)

    gsm = {}
    gbig = {}
    ddn, dgu, gsm['g_ffn_post'] = _b_ffn_down(d3, dn, gu, vec('g_ffn_post'), wf['ffn_w_down'], tm)
    dx2, gsm['g_ffn_pre'] = _b_ffn_up(dgu, d3, x2, wf['ffn_w_gu'], vec('g_ffn_pre'), tm)
    gbig['ffn_w_down'] = _mm_tn("dw_ffn_down", act, ddn)
    gbig['ffn_w_gu'] = _mm_tn("dw_ffn_gu", hf, dgu, col_shards=4)
    dao, dq, dx1, dk, dv, gsm['g_ca_post'], gsm['g_ca_pre'] = _b_attn(
        dx2, ao, x1, hc, vec('g_ca_post'), wf['ca_w_o'], wf['ca_w_q'], vec('g_ca_pre'), kk, vv, tm, tpb, mlen)
    gbig['ca_w_o'] = _mm_tn("dw_ca_o", o_att, dao)
    gbig['ca_w_q'] = _mm_tn("dw_ca_q", hc, dq)
    dkv, gsm['g_mem'] = _b_mem(dk, dv, memf, wf['ca_w_kv'], vec('g_mem'), tmm)
    gbig['ca_w_kv'] = _mm_tn("dw_ca_kv", mem_n, dkv, col_shards=4)
    dmo, da_br, db_br, dzg, dygm, dgate, dypre, gsm['g_mix_post'] = _b_mix(
        dx1, mo, a_br, b_br, zg, ypre, vec('g_mix_post'), wf['w_mix_out'], wf['w_br_gm'], wf['w_br_s5'],
        wf['s5_w_glu'], tm)
    gbig['w_mix_out'] = _mm_tn("dw_mix_out", merged, dmo)
    gbig['w_br_gm'] = _mm_tn("dw_br_gm", ygm, da_br)
    gbig['w_br_s5'] = _mm_tn("dw_br_s5", ys5, db_br)
    gbig['s5_w_glu'] = _mm_tn("dw_s5_glu", yg, dgate)
    slab_of = lambda k: (gbig[k].reshape(4, -1, D) if gbig[k].ndim == 3 or BIG[k] == 0 else _shard_slabs(gbig[k], 1))
    red_rest = _reduce_big("rest", [slab_of(k) for k in rest])
    (dzs5, dbbc, dccm_t, davec, ddvec), got = _b_s5(zs5, dypre, states, bbc, ccm, avec, dvec, nb, s5_t,
                                                    next(red_rest))
    dccm = dccm_t.transpose(0, 2, 1)
    (dzgm, gsm['gm_w_s'], dbs3, gsm['gm_ln_g'], gsm['gm_ln_b']), got = _b_gmlp(
        zgm, dygm, vec('gm_ln_g'), vec('gm_ln_b'), w['gm_w_s'], bs3, tm, red_rest.send(got))
    gsm['gm_b_s'] = dbs3
    (grad_x, gsm['g_mix_pre']), got = _b_in(dzgm, dzs5, dzg, dx1, xf, wf['w_in'], vec('g_mix_pre'), tm,
                                            red_rest.send(got))
    dw_in_gm, got = _mm_tn("dw_in_gm", h, dzgm, comm=red_rest.send(got))
    greds = dict(zip(rest, _finish(red_rest, got)))
    dw_in_s5 = _mm_tn("dw_in_s5", h, dzs5)

    unblk = lambda m_: _block_diag_take(m_, 16, 64).transpose(2, 0, 1, 3).reshape(16, -1)
    d_bb_re, d_bb_im = unblk(dbbc[:, :, :512]), unblk(dbbc[:, :, 512:])
    cunblk = lambda m_: _block_diag_take(m_, 64, 16).transpose(0, 1, 3, 2).reshape(32, 16, 64)
    gsm['s5_c_re'] = cunblk(dccm[:, :512, :])
    gsm['s5_c_im'] = -cunblk(dccm[:, 512:, :])
    d_ab_re, d_ab_im = davec[:, :, :512].reshape(1, -1), davec[:, :, 512:].reshape(1, -1)
    g_lr, g_li, g_ls, g_br, g_bi = _s5_disc_bwd(lr_l, li_l, ls_l, br_t, bi_t, (d_ab_re, d_ab_im, d_bb_re, d_bb_im))
    gsm['s5_lam_re'], gsm['s5_lam_im'] = g_lr.reshape(32, 64), g_li.reshape(32, 64)
    gsm['s5_log_step'] = g_ls.reshape(32, 64).sum(axis=1)
    from_t = lambda t: t.reshape(16, 32, 64).transpose(1, 2, 0)
    gsm['s5_b_re'], gsm['s5_b_im'] = from_t(g_br), from_t(g_bi)
    gsm['s5_d'] = ddvec.reshape(32, 16)

    small_shapes = [w[k].shape for k in SMALL]
    spack = _pack([gsm[k].reshape(w[k].shape) for k in SMALL] + [loss_cols], 8)
    rs = spack.shape[0]
    me_slot = 4 * lax.axis_index("x") + 2 * lax.axis_index("y") + lax.axis_index("c")
    dw_in_g, (sall,) = _mm_tn("dw_in_g", h, dzg, comm=_gather_all(spack))
    gbig['w_in'] = jnp.concatenate([dw_in_gm, dw_in_s5, dw_in_g], axis=1)
    sall = lax.dynamic_update_index_in_dim(sall, spack, me_slot, 0)
    ssum = _sum_slots("sum_small", sall)
    small_red = _unpack(ssum, small_shapes + [(1, D)])
    loss = 0.5 * jnp.sum(small_red[-1]) / D

    red_in = _reduce_big("w_in", [slab_of('w_in')])
    got = _run_comm("reduce_pair_w_in", next(red_in))
    for phase in ("reduce_step1_w_in", "reduce_step2_w_in"):
        got = _run_comm(phase, red_in.send(got))
    got = _run_comm("share_pair_w_in", red_in.send(got))
    greds['w_in'] = _finish(red_in, got)[0]

    res_big = {}
    for k in big_names:
        g = greds[k].reshape(local_shapes[k])
        dl, mn, vn = _adamw("adamw_" + k, w[k], g, a['m_' + k][0], a['v_' + k][0], _tile_rows(g.shape[0], 256))
        res_big[k] = (g, dl, mn, vn)
    pad_loss = [jnp.zeros((1, D), F32)]
    wsm = _pack([w[k] for k in SMALL] + pad_loss, 8)
    msm = _pack([a['m_' + k][0] for k in SMALL] + pad_loss, 8)
    vsm = _pack([a['v_' + k][0] for k in SMALL] + pad_loss, 8)
    dl_s, mn_s, vn_s = _adamw("adamw_small", wsm, ssum, msm, vsm, rs)
    res_small = {k: vals for k, vals in zip(SMALL, zip(*[_unpack(p, small_shapes) for p in (ssum, dl_s, mn_s, vn_s)]))}

    res = {**res_big, **res_small}
    outs = [loss, grad_x.reshape(nb, seq, D)]
    for i in range(4):
        outs += [res[k][i][None] for k in WNAMES]
    return tuple(outs)


def _finish(gen, got):
    try:
        gen.send(got)
    except StopIteration as done:
        return done.value
    raise AssertionError("the generator has more exchanges")


def _tile_rows(rows, cap=384):
    best = rows
    for t in range(16, min(rows, cap) + 1, 16):
        if rows % t == 0:
            best = t
    return best
```

```python
import math
from typing import Any, NamedTuple

import jax
import jax.numpy as jnp
from jax import lax
from jax.experimental import pallas as pl
from jax.experimental.pallas import tpu as pltpu

F32 = jnp.float32
BF16 = jnp.bfloat16
EPS = 1e-6
D = 1024
GM_CHUNK = 128
GM_GROUPS = 8
S5_W = 512
S5_SUPER = 4
S5_T = 512
HEADS = 4
HEAD_DIM = 256
FFN_H = 2816
LANE = 128
VMEM_LIMIT = 56 * 1024 * 1024
MESH = pl.DeviceIdType.MESH

ADAM_LR, ADAM_B1, ADAM_B2, ADAM_EPS, ADAM_WD, ADAM_STEP = 0.001, 0.9, 0.999, 1e-08, 0.01, 10

WNAMES = ['g_mix_pre', 'w_in', 'gm_ln_g', 'gm_ln_b', 'gm_w_s', 'gm_b_s', 's5_lam_re', 's5_lam_im', 's5_log_step',
          's5_b_re', 's5_b_im', 's5_c_re', 's5_c_im', 's5_d', 's5_w_glu', 'w_br_gm', 'w_br_s5', 'w_mix_out',
          'g_mix_post', 'g_ca_pre', 'g_mem', 'ca_w_q', 'ca_w_kv', 'ca_w_o', 'g_ca_post', 'g_ffn_pre', 'ffn_w_gu',
          'ffn_w_down', 'g_ffn_post']
BIG = {'w_in': 1, 's5_w_glu': 0, 'w_br_gm': 0, 'w_br_s5': 1, 'w_mix_out': 0, 'ca_w_q': 0, 'ca_w_kv': 1,
       'ca_w_o': 0, 'ffn_w_gu': 1, 'ffn_w_down': 0}
SMALL = [n for n in WNAMES if n not in BIG]


def _dot(a, b):
    return jnp.dot(a, b, preferred_element_type=F32)


def _dot_nt(a, b):
    return lax.dot_general(a, b, (((1,), (1,)), ((), ())), preferred_element_type=F32)


def _dot_tn(a, b):
    return lax.dot_general(a, b, (((0,), (0,)), ((), ())), preferred_element_type=F32)


def _rms_fwd(x, g):
    r = lax.rsqrt(jnp.mean(x * x, axis=-1, keepdims=True) + EPS)
    return x * r * g


def _rms_bwd(x, g, dy):
    r = lax.rsqrt(jnp.mean(x * x, axis=-1, keepdims=True) + EPS)
    xh = x * r
    gdy = dy * g
    dx = r * (gdy - xh * jnp.mean(gdy * xh, axis=-1, keepdims=True))
    dg = jnp.sum(dy * xh, axis=0, keepdims=True)
    return dx, dg


_GC = math.sqrt(2.0 / math.pi)


def _gelu(x):
    return 0.5 * x * (1.0 + jnp.tanh(_GC * (x + 0.044715 * x * x * x)))


def _gelu_grad(x):
    t = jnp.tanh(_GC * (x + 0.044715 * x * x * x))
    return 0.5 * (1.0 + t) + 0.5 * x * (1.0 - t * t) * _GC * (1.0 + 3 * 0.044715 * x * x)


def _sig(x):
    return 1.0 / (1.0 + jnp.exp(-x))


def _cscan(br, bi, ar, ai, reverse):
    t = br.shape[0]
    row = lax.broadcasted_iota(jnp.int32, br.shape, 0)
    pr, pi = ar, ai
    sh = 1
    while sh < t:
        if reverse:
            keep = row < t - sh
            rr, ri = pltpu.roll(br, t - sh, 0), pltpu.roll(bi, t - sh, 0)
        else:
            keep = row >= sh
            rr, ri = pltpu.roll(br, sh, 0), pltpu.roll(bi, sh, 0)
        rr = jnp.where(keep, rr, 0.0)
        ri = jnp.where(keep, ri, 0.0)
        br, bi = br + pr * rr - pi * ri, bi + pr * ri + pi * rr
        pr, pi = pr * pr - pi * pi, 2.0 * pr * pi
        sh *= 2
    return br, bi


def _s5_disc(lr, li, ls, br, bi):
    step = jnp.exp(ls)
    mag = jnp.exp(lr * step)
    ab_re = mag * jnp.cos(li * step)
    ab_im = mag * jnp.sin(li * step)
    den = lr * lr + li * li
    nr = ab_re - 1.0
    co_re = (nr * lr + ab_im * li) / den
    co_im = (ab_im * lr - nr * li) / den
    return ab_re, ab_im, co_re * br - co_im * bi, co_re * bi + co_im * br


def _params(sem=None):
    return pltpu.CompilerParams(dimension_semantics=sem, vmem_limit_bytes=VMEM_LIMIT)


def _rowcall(name, body, n_rows, tm, ins, outs, scratch=(), comm=None):
    def spec(kind, shape):
        if kind == 'row':
            return pl.BlockSpec((tm, shape[1]), lambda i: (i, 0))
        if kind == 'full':
            nd = len(shape)
            return pl.BlockSpec(tuple(shape), lambda i: (0,) * nd, pipeline_mode=pl.Buffered(1))
        if kind == 'acc':
            nd = len(shape)
            return pl.BlockSpec(tuple(shape), lambda i: (0,) * nd)
        return kind

    in_specs = [spec(k, a.shape) for a, k in ins]
    out_shape, out_specs = [], []
    for k, s, dt in outs:
        shape = (n_rows, s) if k == 'row' else tuple(s)
        out_shape.append(jax.ShapeDtypeStruct(shape, dt))
        out_specs.append(spec(k, shape))
    args = [a for a, _ in ins]
    if comm is not None:
        return _hosting_call(body, name, (n_rows // tm,), in_specs, out_specs, out_shape, list(scratch),
                             ("arbitrary",), args, comm)
    return pl.pallas_call(
        body, name=name, grid=(n_rows // tm,), in_specs=in_specs, out_specs=out_specs, out_shape=out_shape,
        scratch_shapes=list(scratch), compiler_params=_params(("arbitrary",)),
    )(*args)


def _tile(n, cap):
    if n <= cap:
        return n
    best = LANE
    for k in range(1, n // LANE + 1):
        t = k * LANE
        if n % t == 0 and t <= cap:
            best = t
    return best


def _mm_tn(name, a, b, col_shards=1, comm=None):
    n, k = a.shape
    m = b.shape[1]
    mloc = m // col_shards
    tk, tn, tr = _tile(k, 1536), _tile(mloc, 1536), min(n, 1024)
    per = mloc // tn

    def body(a_ref, b_ref, o_ref):
        @pl.when(pl.program_id(2) == 0)
        def _():
            o_ref[...] = jnp.zeros_like(o_ref)
        o_ref[...] += _dot_tn(a_ref[...].astype(BF16), b_ref[...].astype(BF16))

    if col_shards == 1:
        out_spec = pl.BlockSpec((tk, tn), lambda i, j, r: (i, j))
        out_shape = (k, m)
    else:
        out_spec = pl.BlockSpec((None, tk, tn), lambda i, j, r: (j // per, i, j % per))
        out_shape = (col_shards, k, mloc)
    grid = (k // tk, m // tn, n // tr)
    in_specs = [pl.BlockSpec((tr, tk), lambda i, j, r: (r, i)), pl.BlockSpec((tr, tn), lambda i, j, r: (r, j))]
    if comm is not None:
        (res,), extra = _hosting_call(body, name, grid, in_specs, [out_spec], [jax.ShapeDtypeStruct(out_shape, F32)],
                                      [], ("arbitrary", "arbitrary", "arbitrary"), (a, b), comm)
        return res, extra
    return pl.pallas_call(
        body, name=name, grid=grid, in_specs=in_specs, out_specs=out_spec,
        out_shape=jax.ShapeDtypeStruct(out_shape, F32),
        compiler_params=_params(("parallel", "parallel", "arbitrary")),
    )(a, b)


def _comm_call(name, srcs, outs, nsem, body_fn, aliases=None):
    ns, no = len(srcs), len(outs)

    def body(*refs):
        body_fn(refs[:ns], refs[ns:ns + no], refs[ns + no], refs[ns + no + 1])

    hbm = pl.BlockSpec(memory_space=pltpu.HBM)
    return pl.pallas_call(
        body, name=name, in_specs=[hbm] * ns, out_specs=[hbm] * no,
        out_shape=[jax.ShapeDtypeStruct(s, d) for s, d in outs],
        scratch_shapes=[pltpu.SemaphoreType.DMA((nsem,)), pltpu.SemaphoreType.DMA((nsem,))],
        input_output_aliases=aliases or {},
    )(*srcs)


def _place():
    x, y, c = lax.axis_index("x"), lax.axis_index("y"), lax.axis_index("c")
    chips = (2 * x + y, 2 * (1 - x) + y, 2 * x + (1 - y), 2 * (1 - x) + (1 - y))
    peers = ((1 - x, y, c), (x, 1 - y, c), (x, y, 1 - c))
    return c, chips, peers


class _Comm(NamedTuple):
    srcs: Any
    outs: Any
    nsem: int
    stages: Any
    aliases: Any
    fracs: Any


class _Lazy:
    def __init__(self, make):
        self._make = make

    def start(self):
        self._make().start()

    def wait(self):
        self._make().wait()

    def wait_recv(self):
        self._make().wait_recv()

    def wait_send(self):
        self._make().wait_send()


def _run_comm(name, comm):
    def body_fn(srcs, dsts, ss, rs):
        for stage in comm.stages:
            stage(srcs, dsts, ss, rs)

    return _comm_call(name, comm.srcs, comm.outs, comm.nsem, body_fn, aliases=comm.aliases)


def _hosting_call(body, name, grid, in_specs, out_specs, out_shape, scratch, sem, args, comm):
    hbm = pl.BlockSpec(memory_space=pltpu.HBM)
    n_in, n_out, n_sc = len(in_specs), len(out_specs), len(scratch)
    ns, no = len(comm.srcs), len(comm.outs)
    total = math.prod(grid)
    steps = [min(total - 1, int(f * total)) for f in comm.fracs]

    def wrapped(*refs):
        ins, csrc = refs[:n_in], refs[n_in:n_in + ns]
        outs, cdst = refs[n_in + ns:n_in + ns + n_out], refs[n_in + ns + n_out:n_in + ns + n_out + no]
        rest = refs[n_in + ns + n_out + no:]
        lin = 0
        for d, size in enumerate(grid):
            lin = lin * size + pl.program_id(d)
        for stage, at in zip(comm.stages, steps):
            @pl.when(lin == at)
            def _(stage=stage):
                stage(csrc, cdst, rest[n_sc], rest[n_sc + 1])
        body(*ins, *outs, *rest[:n_sc])

    res = pl.pallas_call(
        wrapped, name=name, grid=grid, in_specs=list(in_specs) + [hbm] * ns, out_specs=list(out_specs) + [hbm] * no,
        out_shape=list(out_shape) + [jax.ShapeDtypeStruct(s, d) for s, d in comm.outs],
        scratch_shapes=list(scratch) + [pltpu.SemaphoreType.DMA((comm.nsem,)), pltpu.SemaphoreType.DMA((comm.nsem,))],
        input_output_aliases={n_in + i: n_out + o for i, o in comm.aliases.items()},
        compiler_params=_params(sem),
    )(*args, *comm.srcs)
    return res[:n_out], res[n_out:]


def _gather_weights(name, shards, axes):
    nw = len(shards)
    shapes = [s.shape for s in shards]
    outs = [((4,) + s if ax == 0 else (s[0], 4 * s[1]), BF16) for s, ax in zip(shapes, axes)]

    def win(ref, i, chip, start, rows):
        r, cols = shapes[i]
        if axes[i] == 0:
            return ref.at[chip, pl.ds(start, rows), :]
        return ref.at[pl.ds(start, rows), pl.ds(chip * cols, cols)]

    def place_body(*refs):
        srcs, dsts, bufs, sem = refs[:nw], refs[nw:2 * nw], refs[2 * nw:3 * nw], refs[3 * nw]
        me = 2 * lax.axis_index("x") + lax.axis_index("y")
        cps = []
        for i in range(nw):
            bufs[i][...] = srcs[i][...].astype(BF16)
            cps.append(pltpu.make_async_copy(bufs[i], win(dsts[i], i, me, 0, shapes[i][0]), sem.at[i]))
            cps[-1].start()
        for cp in cps:
            cp.wait()

    placed = pl.pallas_call(
        place_body, name=name, out_shape=[jax.ShapeDtypeStruct(s, d) for s, d in outs],
        in_specs=[pl.BlockSpec(memory_space=pltpu.VMEM)] * nw, out_specs=[pl.BlockSpec(memory_space=pltpu.HBM)] * nw,
        scratch_shapes=[pltpu.VMEM(s, BF16) for s in shapes] + [pltpu.SemaphoreType.DMA((nw,))],
        compiler_params=_params(),
    )(*shards)

    def copies(dsts, ss, rs):
        c, (me, xn, yn, dg), (px, py, psib) = _place()

        def rdma(k, window, peer):
            return _Lazy(lambda: pltpu.make_async_remote_copy(
                src_ref=window, dst_ref=window, send_sem=ss.at[k], recv_sem=rs.at[k], device_id=peer,
                device_id_type=MESH))

        first, later, swaps = [], [], []
        for i in range(nw):
            qr, hr, k0 = shapes[i][0] // 4, shapes[i][0] // 2, 6 * i
            q0, q1 = 2 * c * qr, (2 * c + 1) * qr
            w0, w1 = win(dsts[i], i, me, q0, qr), win(dsts[i], i, me, q1, qr)
            first.append([rdma(k0, w0, px), rdma(k0 + 1, w1, py), rdma(k0 + 2, w0, py), rdma(k0 + 3, w1, px)])
            later.append([rdma(k0 + 4, win(dsts[i], i, xn, q0, qr), py), rdma(k0 + 5, win(dsts[i], i, yn, q1, qr), px)])
            swaps.append([rdma(6 * nw + 3 * i + j, win(dsts[i], i, chip, c * hr, hr), psib)
                          for j, chip in enumerate((xn, yn, dg))])
        return first, later, swaps

    def send(srcs, dsts, ss, rs):
        for cps in copies(dsts, ss, rs)[0]:
            for cp in cps:
                cp.start()

    def forward(srcs, dsts, ss, rs):
        first, later, _ = copies(dsts, ss, rs)
        for i in range(nw):
            first[i][0].wait_recv()
            later[i][0].start()
            first[i][1].wait_recv()
            later[i][1].start()

    def swap(srcs, dsts, ss, rs):
        first, later, swaps = copies(dsts, ss, rs)
        for i in range(nw):
            for cp in first[i][2:] + later[i]:
                cp.wait_recv()
            for sw in swaps[i]:
                sw.start()

    def finish(srcs, dsts, ss, rs):
        first, later, swaps = copies(dsts, ss, rs)
        for i in range(nw):
            for sw in swaps[i]:
                sw.wait()
            for cp in first[i] + later[i]:
                cp.wait_send()

    return _Comm(placed, outs, 9 * nw, [send, forward, swap, finish], {i: i for i in range(nw)},
                 (0.0, 0.55, 0.85, 1.0))


def _gather_all(src):
    def copies(srcs, dsts, ss, rs):
        x, y, c = lax.axis_index("x"), lax.axis_index("y"), lax.axis_index("c")
        me, sib = (x, y, c), (x, y, 1 - c)
        chips = [(1 - x, y), (x, 1 - y), (1 - x, 1 - y)]

        def rows(px, py, pc):
            return dsts[0].at[4 * px + 2 * py + pc]

        def copy(k, block, to, src=None):
            return _Lazy(lambda: pltpu.make_async_remote_copy(
                src_ref=rows(*block) if src is None else src, dst_ref=rows(*block), send_sem=ss.at[k],
                recv_sem=rs.at[k], device_id=to, device_id_type=MESH))

        first = [copy(0, me, sib, src=srcs[0])] + [copy(1 + j, me, (*chip, c), src=srcs[0])
                                                   for j, chip in enumerate(chips)]
        passed = [copy(4 + j, (*chip, c), sib) for j, chip in enumerate(chips)]
        landed = [copy(0, sib, me)] + [copy(1 + j, (*chip, c), me) for j, chip in enumerate(chips)]
        landed += [copy(4 + j, (*chip, 1 - c), me) for j, chip in enumerate(chips)]
        return first, passed, landed

    def send(*refs):
        for cp in copies(*refs)[0]:
            cp.start()

    def forward(*refs):
        _, passed, landed = copies(*refs)
        for j in range(3):
            landed[1 + j].wait_recv()
            passed[j].start()

    def finish(*refs):
        first, passed, landed = copies(*refs)
        landed[0].wait_recv()
        for cp in landed[4:]:
            cp.wait_recv()
        for cp in first + passed:
            cp.wait_send()

    return _Comm([src], [((8,) + src.shape, src.dtype)], 7, [send, forward, finish], {}, (0.0, 0.5, 1.0))


def _f_in(x, g, w_in, tm):
    n = x.shape[0]

    def body(x_ref, g_ref, w_ref, h_ref, zgm_ref, zs5_ref, zg_ref):
        h = _rms_fwd(x_ref[...], g_ref[...]).astype(BF16)
        h_ref[...] = h
        zgm_ref[...] = _dot(h, w_ref[:, 0:2 * D])
        zs5_ref[...] = _dot(h, w_ref[:, 2 * D:2 * D + S5_W])
        zg_ref[...] = _dot(h, w_ref[:, 2 * D + S5_W:])

    return _rowcall("f_in", body, n, tm, [(x, 'row'), (g, 'full'), (w_in, 'full')],
                    [('row', D, BF16), ('row', 2 * D, F32), ('row', S5_W, F32), ('row', 2 * D, F32)])


def _tril():
    r = lax.broadcasted_iota(jnp.int32, (GM_CHUNK, GM_CHUNK), 0)
    c = lax.broadcasted_iota(jnp.int32, (GM_CHUNK, GM_CHUNK), 1)
    return r >= c


def _ln_stats(v):
    mu = jnp.mean(v, axis=-1, keepdims=True)
    vc = v - mu
    r = lax.rsqrt(jnp.mean(vc * vc, axis=-1, keepdims=True) + EPS)
    return vc * r, r


def _f_gmlp(zgm, ln_g, ln_b, w_s, b_s, tm):
    n = zgm.shape[0]

    def body(z_ref, lg_ref, lb_ref, ws_ref, bs_ref, y_ref):
        zg = _gelu(z_ref[...])
        u = zg[:, :D]
        vh, _ = _ln_stats(zg[:, D:])
        vn = (vh * lg_ref[...] + lb_ref[...]).astype(BF16)
        keep = _tril()
        for g in range(GM_GROUPS):
            w = jnp.where(keep, ws_ref[g], 0.0).astype(BF16)
            cs = slice(g * LANE, (g + 1) * LANE)
            for c in range(tm // GM_CHUNK):
                rs = slice(c * GM_CHUNK, (c + 1) * GM_CHUNK)
                sv = _dot(w, vn[rs, cs]) + bs_ref[g]
                y_ref[rs, cs] = (u[rs, cs] * sv).astype(BF16)

    return _rowcall("f_gmlp", body, n, tm,
                    [(zgm, 'row'), (ln_g, 'full'), (ln_b, 'full'), (w_s, 'full'), (b_s, 'full')],
                    [('row', D, BF16)])[0]


def _s5_specs(nb, nc, t, rev):
    def cc(c):
        return nc - 1 - c if rev else c
    slab = pl.BlockSpec((t, LANE), lambda j, b, c: (b * nc + cc(c), j))
    bb = pl.BlockSpec((None, LANE, 1024), lambda j, b, c: (j, 0, 0))
    cm = pl.BlockSpec((None, 1024, LANE), lambda j, b, c: (j, 0, 0))
    av = pl.BlockSpec((None, 1, 1024), lambda j, b, c: (j, 0, 0))
    dv = pl.BlockSpec((None, 1, LANE), lambda j, b, c: (j, 0, 0))
    st = pl.BlockSpec((None, None, None, 1, 1024), lambda j, b, c: (j, b, cc(c), 0, 0))
    return slab, bb, cm, av, dv, st


def _seg_load(ref, seg):
    return jnp.concatenate([ref[pl.ds(k, 8, stride=seg), :] for k in range(seg)], axis=0)


def _seg_store(ref, val, seg):
    for k in range(seg):
        ref[pl.ds(k, 8, stride=seg), :] = val[8 * k:8 * k + 8, :]


def _seg_scan(s_ref, ar, ai, seg, reverse):
    def step(i, carry):
        k = seg - 1 - i if reverse else i
        sr, si = carry
        nr = ar * sr - ai * si + s_ref[k, :, :512]
        ni = ar * si + ai * sr + s_ref[k, :, 512:]
        s_ref[k, :, :512] = nr
        s_ref[k, :, 512:] = ni
        return nr, ni

    zero = jnp.zeros((8, 512), F32)
    return lax.fori_loop(0, seg, step, (zero, zero), unroll=4)


def _seg_entries(tr, ti, pr, pi, cin, reverse):
    row = lax.broadcasted_iota(jnp.int32, (8, 512), 0)
    edge = row == (7 if reverse else 0)
    cr, ci = cin[:, :512], cin[:, 512:]
    xr = tr + jnp.where(edge, pr * cr - pi * ci, 0.0)
    xi = ti + jnp.where(edge, pr * ci + pi * cr, 0.0)
    ir, ii = _cscan(xr, xi, pr, pi, reverse)
    shift = 7 if reverse else 1
    er = jnp.where(edge, cr, pltpu.roll(ir, shift, 0))
    ei = jnp.where(edge, ci, pltpu.roll(ii, shift, 0))
    far = row == (0 if reverse else 7)
    out = jnp.concatenate([jnp.sum(jnp.where(far, ir, 0.0), axis=0, keepdims=True),
                           jnp.sum(jnp.where(far, ii, 0.0), axis=0, keepdims=True)], axis=1)
    return er, ei, out


def _seg_power(ar, ai, seg):
    pr, pi = ar, ai
    for _ in range(seg.bit_length() - 1):
        pr, pi = pr * pr - pi * pi, 2.0 * pr * pi
    return pr, pi


def _f_s5(zs5, bbc, ccm, avec, dvec, nb, t, comm):
    n = zs5.shape[0]
    nc = n // nb // t
    seg = t // 8
    slab, bb, cm, av, dv, st = _s5_specs(nb, nc, t, False)

    def body(u_ref, bb_ref, cc_ref, a_ref, d_ref, y_ref, st_ref, carry, s_ref):
        @pl.when(pl.program_id(2) == 0)
        def _():
            carry[...] = jnp.zeros_like(carry)
        cin = carry[...]
        st_ref[...] = cin
        up = _seg_load(u_ref, seg)
        s_ref[...] = _dot(up.astype(BF16), bb_ref[...]).reshape(seg, 8, 1024)
        a1r, a1i = a_ref[:, :512], a_ref[:, 512:]
        ar, ai = jnp.broadcast_to(a1r, (8, 512)), jnp.broadcast_to(a1i, (8, 512))
        tr, ti = _seg_scan(s_ref, ar, ai, seg, False)
        er, ei, cout = _seg_entries(tr, ti, *_seg_power(a1r, a1i, seg), cin, False)
        carry[...] = cout

        def apply(k, pe):
            pr, pi = pe
            pr, pi = ar * pr - ai * pi, ar * pi + ai * pr
            s_ref[k, :, :512] = s_ref[k, :, :512] + pr
            s_ref[k, :, 512:] = s_ref[k, :, 512:] + pi
            return pr, pi

        lax.fori_loop(0, seg, apply, (er, ei), unroll=4)
        s = s_ref[...].reshape(t, 1024).astype(BF16)
        _seg_store(y_ref, _dot(s, cc_ref[...]) + d_ref[...] * up, seg)

    return _hosting_call(
        body, "f_s5", (S5_SUPER, nb, nc), [slab, bb, cm, av, dv], [slab, st],
        [jax.ShapeDtypeStruct((n, S5_W), F32), jax.ShapeDtypeStruct((S5_SUPER, nb, nc, 1, 1024), F32)],
        [pltpu.VMEM((1, 1024), F32), pltpu.VMEM((seg, 8, 1024), F32)], ("arbitrary", "arbitrary", "arbitrary"),
        (zs5, bbc, ccm, avec, dvec), comm)


def _f_mix(ygm, ypre, zg, x, w_glu, w_br_gm, w_br_s5, w_mix_out, g_post, tm):
    n = x.shape[0]

    def body(ygm_ref, ypre_ref, zg_ref, x_ref, wglu_ref, wgm_ref, ws5_ref, wout_ref, g_ref,
             yg_ref, ys5_ref, a_ref, b_ref, mg_ref, mo_ref, x1_ref):
        yg = _gelu(ypre_ref[...])
        ygb = yg.astype(BF16)
        yg_ref[...] = ygb
        ys5 = (yg * _sig(_dot(ygb, wglu_ref[...]))).astype(BF16)
        ys5_ref[...] = ys5
        a = _dot(ygm_ref[...], wgm_ref[...])
        b = _dot(ys5, ws5_ref[...])
        a_ref[...] = a.astype(BF16)
        b_ref[...] = b.astype(BF16)
        zg = zg_ref[...]
        merged = (_sig(zg[:, :D]) * a + _sig(zg[:, D:]) * b).astype(BF16)
        mg_ref[...] = merged
        mo = _dot(merged, wout_ref[...])
        mo_ref[...] = mo
        x1_ref[...] = x_ref[...] + _rms_fwd(mo, g_ref[...])

    return _rowcall("f_mix", body, n, tm,
                    [(ygm, 'row'), (ypre, 'row'), (zg, 'row'), (x, 'row'), (w_glu, 'full'), (w_br_gm, 'full'),
                     (w_br_s5, 'full'), (w_mix_out, 'full'), (g_post, 'full')],
                    [('row', S5_W, BF16), ('row', S5_W, BF16), ('row', D, BF16), ('row', D, BF16),
                     ('row', D, BF16), ('row', D, F32), ('row', D, F32)])


def _f_mem(mem, g_mem, w_kv, tm):
    n = mem.shape[0]

    def body(m_ref, g_ref, w_ref, mn_ref, k_ref, v_ref):
        mn = _rms_fwd(m_ref[...], g_ref[...]).astype(BF16)
        mn_ref[...] = mn
        k_ref[...] = _dot(mn, w_ref[:, :D]).astype(BF16)
        v_ref[...] = _dot(mn, w_ref[:, D:]).astype(BF16)

    return _rowcall("f_mem", body, n, tm, [(mem, 'row'), (g_mem, 'full'), (w_kv, 'full')],
                    [('row', D, BF16), ('row', D, BF16), ('row', D, BF16)])


def _softmax(s):
    m = jnp.max(s, axis=-1, keepdims=True)
    e = jnp.exp(s - m)
    return e / jnp.sum(e, axis=-1, keepdims=True)


def _f_attn(x1, g_pre, w_q, k, v, w_o, g_post, tm, tpb, mlen):
    n = x1.shape[0]
    kv_spec = pl.BlockSpec((mlen, D), lambda i: (i // tpb, 0))
    scale = HEAD_DIM ** -0.5

    def body(x_ref, gp_ref, wq_ref, k_ref, v_ref, wo_ref, go_ref, hc_ref, o_ref, ao_ref, x2_ref):
        x1v = x_ref[...]
        hc = _rms_fwd(x1v, gp_ref[...]).astype(BF16)
        hc_ref[...] = hc
        q = _dot(hc, wq_ref[...])
        for h in range(HEADS):
            hs = slice(h * HEAD_DIM, (h + 1) * HEAD_DIM)
            p = _softmax(_dot_nt(q[:, hs].astype(BF16), k_ref[:, hs]) * scale)
            o_ref[:, hs] = _dot(p.astype(BF16), v_ref[:, hs]).astype(BF16)
        ao = _dot(o_ref[...], wo_ref[...])
        ao_ref[...] = ao
        x2_ref[...] = x1v + _rms_fwd(ao, go_ref[...])

    return _rowcall("f_attn", body, n, tm,
                    [(x1, 'row'), (g_pre, 'full'), (w_q, 'full'), (k, kv_spec), (v, kv_spec), (w_o, 'full'),
                     (g_post, 'full')],
                    [('row', D, BF16), ('row', D, BF16), ('row', D, F32), ('row', D, F32)])


def _f_ffn_up(x2, g_pre, w_gu, tm):
    n = x2.shape[0]

    def body(x_ref, g_ref, w_ref, hf_ref, gu_ref, act_ref):
        hf = _rms_fwd(x_ref[...], g_ref[...]).astype(BF16)
        hf_ref[...] = hf
        gg = _dot(hf, w_ref[:, :FFN_H])
        uu = _dot(hf, w_ref[:, FFN_H:])
        gu_ref[:, :FFN_H] = gg.astype(BF16)
        gu_ref[:, FFN_H:] = uu.astype(BF16)
        act_ref[...] = (gg * _sig(gg) * uu).astype(BF16)

    return _rowcall("f_ffn_up", body, n, tm, [(x2, 'row'), (g_pre, 'full'), (w_gu, 'full')],
                    [('row', D, BF16), ('row', 2 * FFN_H, BF16), ('row', FFN_H, BF16)])


def _f_ffn_down(act, x2, tgt, w_down, g_post, tm):
    n = x2.shape[0]

    def body(a_ref, x_ref, t_ref, w_ref, g_ref, dn_ref, d3_ref, loss_ref):
        @pl.when(pl.program_id(0) == 0)
        def _():
            loss_ref[...] = jnp.zeros_like(loss_ref)
        dn = _dot(a_ref[...], w_ref[...])
        dn_ref[...] = dn
        err = x_ref[...] + _rms_fwd(dn, g_ref[...]) - t_ref[...]
        d3_ref[...] = err * (1.0 / D)
        loss_ref[...] += jnp.sum(err * err, axis=0, keepdims=True)

    return _rowcall("f_ffn_down", body, n, tm,
                    [(act, 'row'), (x2, 'row'), (tgt, 'row'), (w_down, 'full'), (g_post, 'full')],
                    [('row', D, F32), ('row', D, F32), ('acc', (1, D), F32)])


def _b_ffn_down(d3, dn, gu, g_post, w_down, tm):
    n = d3.shape[0]

    def body(d3_ref, dn_ref, gu_ref, g_ref, w_ref, ddn_ref, dgu_ref, dg_ref):
        @pl.when(pl.program_id(0) == 0)
        def _():
            dg_ref[...] = jnp.zeros_like(dg_ref)
        ddn, dg = _rms_bwd(dn_ref[...], g_ref[...], d3_ref[...])
        dg_ref[...] += dg
        ddn = ddn.astype(BF16)
        ddn_ref[...] = ddn
        dact = _dot_nt(ddn, w_ref[...])
        gg = gu_ref[:, :FFN_H].astype(F32)
        uu = gu_ref[:, FFN_H:].astype(F32)
        sg = _sig(gg)
        dgu_ref[:, :FFN_H] = (dact * uu * sg * (1.0 + gg * (1.0 - sg))).astype(BF16)
        dgu_ref[:, FFN_H:] = (dact * gg * sg).astype(BF16)

    return _rowcall("b_ffn_down", body, n, tm,
                    [(d3, 'row'), (dn, 'row'), (gu, 'row'), (g_post, 'full'), (w_down, 'full')],
                    [('row', D, BF16), ('row', 2 * FFN_H, BF16), ('acc', (1, D), F32)])


def _b_ffn_up(dgu, d3, x2, w_gu, g_pre, tm):
    n = d3.shape[0]

    def body(dgu_ref, d3_ref, x_ref, w_ref, g_ref, dx_ref, dg_ref):
        @pl.when(pl.program_id(0) == 0)
        def _():
            dg_ref[...] = jnp.zeros_like(dg_ref)
        dhf = _dot_nt(dgu_ref[...], w_ref[...])
        dx, dg = _rms_bwd(x_ref[...], g_ref[...], dhf)
        dg_ref[...] += dg
        dx_ref[...] = d3_ref[...] + dx

    return _rowcall("b_ffn_up", body, n, tm,
                    [(dgu, 'row'), (d3, 'row'), (x2, 'row'), (w_gu, 'full'), (g_pre, 'full')],
                    [('row', D, F32), ('acc', (1, D), F32)])


def _b_attn(dx2, ao, x1, hc, g_post, w_o, w_q, g_pre, k, v, tm, tpb, mlen):
    n = dx2.shape[0]
    nb = n // (tm * tpb)
    kv_spec = pl.BlockSpec((mlen, D), lambda i: (i // tpb, 0))
    scale = HEAD_DIM ** -0.5

    def body(dx2_ref, ao_ref, x1_ref, hc_ref, go_ref, wo_ref, wq_ref, gp_ref, k_ref, v_ref,
             dao_ref, dq_ref, dx1_ref, dk_ref, dv_ref, dgo_ref, dgp_ref):
        i = pl.program_id(0)

        @pl.when(i == 0)
        def _():
            dgo_ref[...] = jnp.zeros_like(dgo_ref)
            dgp_ref[...] = jnp.zeros_like(dgp_ref)

        @pl.when(i % tpb == 0)
        def _():
            dk_ref[...] = jnp.zeros_like(dk_ref)
            dv_ref[...] = jnp.zeros_like(dv_ref)

        dx2v = dx2_ref[...]
        dao, dgo = _rms_bwd(ao_ref[...], go_ref[...], dx2v)
        dgo_ref[...] += dgo
        dao = dao.astype(BF16)
        dao_ref[...] = dao
        do = _dot_nt(dao, wo_ref[...])
        q = _dot(hc_ref[...], wq_ref[...])
        for h in range(HEADS):
            hs = slice(h * HEAD_DIM, (h + 1) * HEAD_DIM)
            qh = q[:, hs].astype(BF16)
            kh = k_ref[:, hs]
            p = _softmax(_dot_nt(qh, kh) * scale)
            doh = do[:, hs].astype(BF16)
            dp = _dot_nt(doh, v_ref[:, hs])
            ds = (p * (dp - jnp.sum(dp * p, axis=-1, keepdims=True)) * scale).astype(BF16)
            dq_ref[:, hs] = _dot(ds, kh).astype(BF16)
            dk_ref[:, hs] += _dot_tn(ds, qh)
            dv_ref[:, hs] += _dot_tn(p.astype(BF16), doh)
        dhc = _dot_nt(dq_ref[...], wq_ref[...])
        dx, dgp = _rms_bwd(x1_ref[...], gp_ref[...], dhc)
        dgp_ref[...] += dgp
        dx1_ref[...] = dx2v + dx

    return _rowcall("b_attn", body, n, tm,
                    [(dx2, 'row'), (ao, 'row'), (x1, 'row'), (hc, 'row'), (g_post, 'full'), (w_o, 'full'),
                     (w_q, 'full'), (g_pre, 'full'), (k, kv_spec), (v, kv_spec)],
                    [('row', D, BF16), ('row', D, BF16), ('row', D, F32),
                     (kv_spec, (nb * mlen, D), F32), (kv_spec, (nb * mlen, D), F32),
                     ('acc', (1, D), F32), ('acc', (1, D), F32)])


def _b_mem(dk, dv, mem, w_kv, g_mem, tm):
    n = mem.shape[0]

    def body(dk_ref, dv_ref, m_ref, w_ref, g_ref, dkv_ref, dg_ref):
        @pl.when(pl.program_id(0) == 0)
        def _():
            dg_ref[...] = jnp.zeros_like(dg_ref)
        dkb = dk_ref[...].astype(BF16)
        dvb = dv_ref[...].astype(BF16)
        dkv_ref[:, :D] = dkb
        dkv_ref[:, D:] = dvb
        dmn = _dot_nt(dkb, w_ref[:, :D]) + _dot_nt(dvb, w_ref[:, D:])
        _, dg = _rms_bwd(m_ref[...], g_ref[...], dmn)
        dg_ref[...] += dg

    return _rowcall("b_mem", body, n, tm, [(dk, 'row'), (dv, 'row'), (mem, 'row'), (w_kv, 'full'), (g_mem, 'full')],
                    [('row', 2 * D, BF16), ('acc', (1, D), F32)])


def _b_mix(dx1, mo, a, b, zg, ypre, g_post, w_mix_out, w_br_gm, w_br_s5, w_glu, tm):
    n = dx1.shape[0]

    def body(dx1_ref, mo_ref, a_ref, b_ref, zg_ref, ypre_ref, g_ref, wout_ref, wgm_ref, ws5_ref, wglu_ref,
             dmo_ref, da_ref, db_ref, dzg_ref, dygm_ref, dgate_ref, dypre_ref, dg_ref):
        @pl.when(pl.program_id(0) == 0)
        def _():
            dg_ref[...] = jnp.zeros_like(dg_ref)
        dmo, dg = _rms_bwd(mo_ref[...], g_ref[...], dx1_ref[...])
        dg_ref[...] += dg
        dmo = dmo.astype(BF16)
        dmo_ref[...] = dmo
        dmg = _dot_nt(dmo, wout_ref[...])
        zg = zg_ref[...]
        sa, sb = _sig(zg[:, :D]), _sig(zg[:, D:])
        da = (dmg * sa).astype(BF16)
        db = (dmg * sb).astype(BF16)
        da_ref[...] = da
        db_ref[...] = db
        dzg_ref[:, :D] = (dmg * a_ref[...].astype(F32) * sa * (1.0 - sa)).astype(BF16)
        dzg_ref[:, D:] = (dmg * b_ref[...].astype(F32) * sb * (1.0 - sb)).astype(BF16)
        dygm_ref[...] = _dot_nt(da, wgm_ref[...])
        dys5 = _dot_nt(db, ws5_ref[...])
        ypre = ypre_ref[...]
        yg = _gelu(ypre)
        sgt = _sig(_dot(yg.astype(BF16), wglu_ref[...]))
        dgate = (dys5 * yg * sgt * (1.0 - sgt)).astype(BF16)
        dgate_ref[...] = dgate
        dyg = dys5 * sgt + _dot_nt(dgate, wglu_ref[...])
        dypre_ref[...] = dyg * _gelu_grad(ypre)

    return _rowcall("b_mix", body, n, tm,
                    [(dx1, 'row'), (mo, 'row'), (a, 'row'), (b, 'row'), (zg, 'row'), (ypre, 'row'), (g_post, 'full'),
                     (w_mix_out, 'full'), (w_br_gm, 'full'), (w_br_s5, 'full'), (w_glu, 'full')],
                    [('row', D, BF16), ('row', D, BF16), ('row', D, BF16), ('row', 2 * D, BF16), ('row', D, F32),
                     ('row', S5_W, BF16), ('row', S5_W, F32), ('acc', (1, D), F32)])


def _b_s5(zs5, dypre, states, bbc, ccm, avec, dvec, nb, t, comm):
    n = zs5.shape[0]
    nc = n // nb // t
    seg = t // 8
    slab, bb, cm, av, dv, st = _s5_specs(nb, nc, t, True)

    def body(u_ref, dy_ref, st_ref, bb_ref, cc_ref, a_ref, d_ref,
             du_ref, dbb_ref, dcc_ref, da_ref, dd_ref, lcarry, s_ref, l_ref):
        first = jnp.logical_and(pl.program_id(1) == 0, pl.program_id(2) == 0)

        @pl.when(first)
        def _():
            dbb_ref[...] = jnp.zeros_like(dbb_ref)
            dcc_ref[...] = jnp.zeros_like(dcc_ref)
            da_ref[...] = jnp.zeros_like(da_ref)
            dd_ref[...] = jnp.zeros_like(dd_ref)

        @pl.when(pl.program_id(2) == 0)
        def _():
            lcarry[...] = jnp.zeros_like(lcarry)

        up, dyp = _seg_load(u_ref, seg), _seg_load(dy_ref, seg)
        ub, dyb = up.astype(BF16), dyp.astype(BF16)
        a1r, a1i = a_ref[:, :512], a_ref[:, 512:]
        ar, ai = jnp.broadcast_to(a1r, (8, 512)), jnp.broadcast_to(a1i, (8, 512))
        pr, pi = _seg_power(a1r, a1i, seg)

        s_ref[...] = _dot(ub, bb_ref[...]).reshape(seg, 8, 1024)
        tr, ti = _seg_scan(s_ref, ar, ai, seg, False)
        er, ei, _ = _seg_entries(tr, ti, pr, pi, st_ref[...], False)

        def apply(k, pe):
            qr, qi = pe
            qr, qi = ar * qr - ai * qi, ar * qi + ai * qr
            s_ref[k, :, :512] = s_ref[k, :, :512] + qr
            s_ref[k, :, 512:] = s_ref[k, :, 512:] + qi
            return qr, qi

        lax.fori_loop(0, seg, apply, (er, ei), unroll=4)

        l_ref[...] = _dot_nt(dyb, cc_ref[...]).reshape(seg, 8, 1024)
        tr, ti = _seg_scan(l_ref, ar, -ai, seg, True)
        fr, fi, lout = _seg_entries(tr, ti, pr, -pi, lcarry[...], True)
        lcarry[...] = lout

        def apply_back(i, carry):
            qr, qi, accr, acci = carry
            k = seg - 1 - i
            qr, qi = ar * qr + ai * qi, ar * qi - ai * qr
            lr = l_ref[k, :, :512] + qr
            li = l_ref[k, :, 512:] + qi
            l_ref[k, :, :512] = lr
            l_ref[k, :, 512:] = li
            kp = jnp.maximum(k - 1, 0)
            sr = jnp.where(k == 0, er, s_ref[kp, :, :512])
            si = jnp.where(k == 0, ei, s_ref[kp, :, 512:])
            return qr, qi, accr + lr * sr + li * si, acci + li * sr - lr * si

        zero = jnp.zeros((8, 512), F32)
        _, _, accr, acci = lax.fori_loop(0, seg, apply_back, (fr, fi, zero, zero), unroll=4)
        da_ref[:, :512] += jnp.sum(accr, axis=0, keepdims=True)
        da_ref[:, 512:] += jnp.sum(acci, axis=0, keepdims=True)

        s = s_ref[...].reshape(t, 1024).astype(BF16)
        lam = l_ref[...].reshape(t, 1024).astype(BF16)
        dcc_ref[...] += _dot_tn(dyb, s)
        dbb_ref[...] += _dot_tn(ub, lam)
        _seg_store(du_ref, _dot_nt(lam, bb_ref[...]) + d_ref[...] * dyp, seg)
        dd_ref[...] += jnp.sum(dyp * up, axis=0, keepdims=True)

    return _hosting_call(
        body, "b_s5", (S5_SUPER, nb, nc), [slab, slab, st, bb, cm, av, dv], [slab, bb, bb, av, dv],
        [jax.ShapeDtypeStruct((n, S5_W), F32), jax.ShapeDtypeStruct(bbc.shape, F32),
         jax.ShapeDtypeStruct(bbc.shape, F32), jax.ShapeDtypeStruct(avec.shape, F32),
         jax.ShapeDtypeStruct(dvec.shape, F32)],
        [pltpu.VMEM((1, 1024), F32), pltpu.VMEM((seg, 8, 1024), F32), pltpu.VMEM((seg, 8, 1024), F32)],
        ("arbitrary", "arbitrary", "arbitrary"), (zs5, dypre, states, bbc, ccm, avec, dvec), comm)


def _b_gmlp(zgm, dygm, ln_g, ln_b, w_s, b_s, tm, comm):
    n = zgm.shape[0]

    def body(z_ref, dy_ref, lg_ref, lb_ref, ws_ref, bs_ref, dz_ref, dws_ref, dbs_ref, dlg_ref, dlb_ref,
             du_s, dvn_s):
        @pl.when(pl.program_id(0) == 0)
        def _():
            dws_ref[...] = jnp.zeros_like(dws_ref)
            dbs_ref[...] = jnp.zeros_like(dbs_ref)
            dlg_ref[...] = jnp.zeros_like(dlg_ref)
            dlb_ref[...] = jnp.zeros_like(dlb_ref)
        z = z_ref[...]
        zg = _gelu(z)
        u = zg[:, :D]
        vh, r = _ln_stats(zg[:, D:])
        vn = (vh * lg_ref[...] + lb_ref[...]).astype(BF16)
        dy = dy_ref[...]
        keep = _tril()
        for g in range(GM_GROUPS):
            w = jnp.where(keep, ws_ref[g], 0.0).astype(BF16)
            cs = slice(g * LANE, (g + 1) * LANE)
            for c in range(tm // GM_CHUNK):
                rs = slice(c * GM_CHUNK, (c + 1) * GM_CHUNK)
                vb = vn[rs, cs]
                sv = _dot(w, vb) + bs_ref[g]
                dyb = dy[rs, cs]
                du_s[rs, cs] = dyb * sv
                dsv = dyb * u[rs, cs]
                dsvb = dsv.astype(BF16)
                dvn_s[rs, cs] = _dot_tn(w, dsvb)
                dws_ref[g] += jnp.where(keep, _dot_nt(dsvb, vb), 0.0)
                dbs_ref[g] += jnp.sum(dsv, axis=1, keepdims=True)
        dvn = dvn_s[...]
        dlg_ref[...] += jnp.sum(dvn * vh, axis=0, keepdims=True)
        dlb_ref[...] += jnp.sum(dvn, axis=0, keepdims=True)
        dvh = dvn * lg_ref[...]
        dv = r * (dvh - jnp.mean(dvh, axis=-1, keepdims=True) - vh * jnp.mean(dvh * vh, axis=-1, keepdims=True))
        dz_ref[:, :D] = (du_s[...] * _gelu_grad(z[:, :D])).astype(BF16)
        dz_ref[:, D:] = (dv * _gelu_grad(z[:, D:])).astype(BF16)

    return _rowcall("b_gmlp", body, n, tm,
                    [(zgm, 'row'), (dygm, 'row'), (ln_g, 'full'), (ln_b, 'full'), (w_s, 'full'), (b_s, 'full')],
                    [('row', 2 * D, BF16), ('acc', w_s.shape, F32), ('acc', b_s.shape, F32), ('acc', (1, D), F32),
                     ('acc', (1, D), F32)],
                    scratch=[pltpu.VMEM((tm, D), F32), pltpu.VMEM((tm, D), F32)], comm=comm)


def _b_in(dzgm, dzs5, dzg, dx1, x, w_in, g_pre, tm, comm):
    n = x.shape[0]

    def body(d1_ref, d2_ref, d3_ref, dx1_ref, x_ref, w_ref, g_ref, gx_ref, dg_ref):
        @pl.when(pl.program_id(0) == 0)
        def _():
            dg_ref[...] = jnp.zeros_like(dg_ref)
        dh = (_dot_nt(d1_ref[...], w_ref[:, 0:2 * D]) + _dot_nt(d2_ref[...].astype(BF16), w_ref[:, 2 * D:2 * D + S5_W])
              + _dot_nt(d3_ref[...], w_ref[:, 2 * D + S5_W:]))
        dx, dg = _rms_bwd(x_ref[...], g_ref[...], dh)
        dg_ref[...] += dg
        gx_ref[...] = dx1_ref[...] + dx

    return _rowcall("b_in", body, n, tm,
                    [(dzgm, 'row'), (dzs5, 'row'), (dzg, 'row'), (dx1, 'row'), (x, 'row'), (w_in, 'full'),
                     (g_pre, 'full')],
                    [('row', D, F32), ('acc', (1, D), F32)], comm=comm)


def _whole(name, body, ins, outs):
    return pl.pallas_call(body, name=name, out_shape=[jax.ShapeDtypeStruct(s, dt) for s, dt in outs],
                          compiler_params=_params())(*ins)


def _s5_disc_fwd(lr, li, ls, br, bi):
    def body(lr_ref, li_ref, ls_ref, br_ref, bi_ref, o1, o2, o3, o4):
        outs = _s5_disc(lr_ref[...], li_ref[...], ls_ref[...], br_ref[...], bi_ref[...])
        for o, val in zip((o1, o2, o3, o4), outs):
            o[...] = val

    return _whole("s5_disc_fwd", body, [lr, li, ls, br, bi],
                  [(lr.shape, F32), (lr.shape, F32), (br.shape, F32), (br.shape, F32)])


def _s5_disc_bwd(lr, li, ls, br, bi, cts):
    def body(lr_ref, li_ref, ls_ref, br_ref, bi_ref, c1, c2, c3, c4, o1, o2, o3, o4, o5):
        _, vjp = jax.vjp(_s5_disc, lr_ref[...], li_ref[...], ls_ref[...], br_ref[...], bi_ref[...])
        grads = vjp((c1[...], c2[...], c3[...], c4[...]))
        for o, val in zip((o1, o2, o3, o4, o5), grads):
            o[...] = val

    return _whole("s5_disc_bwd", body, [lr, li, ls, br, bi, *cts],
                  [(lr.shape, F32), (lr.shape, F32), (lr.shape, F32), (br.shape, F32), (br.shape, F32)])


def _sum_slots(name, slots):
    def body(s_ref, o_ref):
        acc = s_ref[0]
        for k in range(1, slots.shape[0]):
            acc = acc + s_ref[k]
        o_ref[...] = acc

    return _whole(name, body, [slots], [(slots.shape[1:], F32)])[0]


def _adamw_math(w, g, m, v):
    c1 = 1.0 - ADAM_B1 ** ADAM_STEP
    c2 = 1.0 - ADAM_B2 ** ADAM_STEP
    mn = ADAM_B1 * m + (1.0 - ADAM_B1) * g
    vn = ADAM_B2 * v + (1.0 - ADAM_B2) * (g * g)
    return -ADAM_LR * ((mn / c1) / (jnp.sqrt(vn / c2) + ADAM_EPS) + ADAM_WD * w), mn, vn


def _adamw_many(name, ws, gs, ms, vs):
    k = len(ws)

    def body(*refs):
        for i in range(k):
            dl, mn, vn = _adamw_math(refs[i][...], refs[k + i][...], refs[2 * k + i][...], refs[3 * k + i][...])
            refs[4 * k + i][...] = dl
            refs[5 * k + i][...] = mn
            refs[6 * k + i][...] = vn

    res = _whole(name, body, [*ws, *gs, *ms, *vs], [(t.shape, F32) for t in ws] * 3)
    return res[:k], res[k:2 * k], res[2 * k:]


def _adamw(name, w, g, m, v, tm):
    def body(w_ref, g_ref, m_ref, v_ref, d_ref, mo_ref, vo_ref):
        d_ref[...], mo_ref[...], vo_ref[...] = _adamw_math(w_ref[...], g_ref[...], m_ref[...], v_ref[...])

    cols = w.shape[1]
    return _rowcall(name, body, w.shape[0], tm, [(w, 'row'), (g, 'row'), (m, 'row'), (v, 'row')],
                    [('row', cols, F32), ('row', cols, F32), ('row', cols, F32)])


def _picked_rowcall(name, body, grid, in_specs, out_cols, tm):
    return pl.pallas_call(
        lambda p_ref, *refs: body(*refs), name=name,
        grid_spec=pltpu.PrefetchScalarGridSpec(
            num_scalar_prefetch=1, grid=(grid,), in_specs=in_specs,
            out_specs=[pl.BlockSpec((tm, D), lambda i, p: (i, 0)) for _ in out_cols]),
        out_shape=[jax.ShapeDtypeStruct((grid * tm, D), dt) for dt in out_cols],
        compiler_params=_params(("arbitrary",)),
    )


def _exchange(copies):
    def build(src_refs, out_refs, ss, rs):
        def rdma(k, src, dst, peer):
            return _Lazy(lambda: pltpu.make_async_remote_copy(
                src_ref=src, dst_ref=dst, send_sem=ss.at[k], recv_sem=rs.at[k], device_id=peer, device_id_type=MESH))
        return copies(src_refs, out_refs, rdma)

    def start(*refs):
        for cp in build(*refs):
            cp.start()

    def finish(*refs):
        for cp in build(*refs):
            cp.wait()

    return start, finish


def _reduce_big(tag, slabs):
    rows = [s.shape[1] for s in slabs]
    blk = sum(rows) // 4
    tm = _tile_rows(blk, 512)
    per = blk // tm
    gpack = jnp.concatenate([s.reshape(4, 2, 2, r // 4, D) for s, r in zip(slabs, rows)], axis=3)

    mx, my, mc = lax.axis_index("x"), lax.axis_index("y"), lax.axis_index("c")
    chip_me, chip_xn, chip_yn = 2 * mx + my, 2 * (1 - mx) + my, 2 * mx + (1 - my)

    def pair_copies(srcs, dsts, rdma):
        c, _, (_, _, psib) = _place()
        return [rdma(0, srcs[0].at[:, 1 - c], dsts[0], psib)]

    from_sib = (yield _Comm([gpack], [((4, 2, blk, D), F32)], 1, _exchange(pair_copies), {}, (0.0, 1.0)))[0]

    def sum_both(a_ref, b_ref, o32_ref, o16_ref):
        s = a_ref[...].astype(F32) + b_ref[...].astype(F32)
        o32_ref[...] = s
        o16_ref[...] = s.astype(BF16)

    p32, p16 = _picked_rowcall(
        "sum_pair_" + tag, sum_both, 8 * per,
        [pl.BlockSpec((None, None, None, tm, D), lambda i, p: (i // (2 * per), p[0], (i // per) % 2, i % per, 0)),
         pl.BlockSpec((tm, D), lambda i, p: (i, 0))],
        [F32, BF16], tm)(jnp.stack([mc]).astype(jnp.int32), gpack, from_sib.reshape(8 * blk, D))
    p32, p16 = p32.reshape(4, 2, blk, D), p16.reshape(4, 2, blk, D)

    def step1_copies(srcs, dsts, rdma):
        _, (me, xn, yn, dg), (px, py, _) = _place()
        s16 = srcs[0]
        return [rdma(0, s16.at[yn, 0], dsts[0].at[0], py), rdma(1, s16.at[dg, 0], dsts[0].at[1], py),
                rdma(2, s16.at[xn, 1], dsts[0].at[2], px), rdma(3, s16.at[dg, 1], dsts[0].at[3], px)]

    recv1 = (yield _Comm([p16], [((4, blk, D), BF16)], 4, _exchange(step1_copies), {}, (0.0, 1.0)))[0]

    s32, s16 = _picked_rowcall(
        "sum_step1_" + tag, sum_both, 4 * per,
        [pl.BlockSpec((None, None, tm, D), lambda i, p: (p[i // per], i // (2 * per), i % per, 0)),
         pl.BlockSpec((tm, D), lambda i, p: (i, 0))],
        [F32, BF16], tm)(jnp.stack([chip_me, chip_xn, chip_me, chip_yn]).astype(jnp.int32), p32,
                         recv1.reshape(4 * blk, D))

    def step2_copies(srcs, dsts, rdma):
        _, _, (px, py, _) = _place()
        return [rdma(0, srcs[0].at[1], dsts[0].at[0], px), rdma(1, srcs[0].at[3], dsts[0].at[1], py)]

    recv2 = (yield _Comm([s16.reshape(4, blk, D)], [((2, blk, D), BF16)], 2, _exchange(step2_copies), {},
                         (0.0, 1.0)))[0]

    def sum_step2(a_ref, b_ref, o_ref):
        o_ref[...] = a_ref[...] + b_ref[...].astype(F32)

    red = _rowcall("sum_step2_" + tag, sum_step2, 2 * blk, tm,
                   [(s32.reshape(4, blk, D), pl.BlockSpec((None, tm, D), lambda i: (2 * (i // per), i % per, 0))),
                    (recv2.reshape(2 * blk, D), 'row')],
                   [('row', D, F32)])[0].reshape(2, blk, D)

    def share_copies(srcs, dsts, rdma):
        _, _, (_, _, psib) = _place()
        return [rdma(0, srcs[0], dsts[0], psib)]

    other = (yield _Comm([red], [((2, blk, D), F32)], 1, _exchange(share_copies), {}, (0.0, 1.0)))[0]
    lo = jnp.where(mc == 0, red, other)
    hi = jnp.where(mc == 0, other, red)
    out, off = [], 0
    for r in rows:
        q = r // 4
        out.append(jnp.concatenate([lo[0, off:off + q], lo[1, off:off + q], hi[0, off:off + q], hi[1, off:off + q]]))
        off += q
    return out


def _rows1024(a):
    flat = a.reshape(-1)
    pad = (-flat.shape[0]) % D
    if pad:
        flat = jnp.concatenate([flat, jnp.zeros((pad,), flat.dtype)])
    return flat.reshape(-1, D)


def _pack(arrs, pad_rows_to=8):
    parts = [_rows1024(a) for a in arrs]
    rows = sum(p.shape[0] for p in parts)
    pad = (-rows) % pad_rows_to
    if pad:
        parts.append(jnp.zeros((pad, D), parts[0].dtype))
    return jnp.concatenate(parts, axis=0)


def _unpack(packed, shapes):
    out, r = [], 0
    for s in shapes:
        size = math.prod(s)
        nr = -(-size // D)
        out.append(packed[r:r + nr].reshape(-1)[:size].reshape(s))
        r += nr
    return out


def _shard_slabs(full, axis):
    r, c = full.shape
    if axis == 0:
        return full.reshape(4, -1, D)
    return full.reshape(r, 4, c // 4).transpose(1, 0, 2).reshape(4, -1, D)


def _block_diag(t):
    eye = jnp.eye(8, dtype=t.dtype)
    j, g, a, b = t.shape
    return (t[:, :, :, None, :] * eye[None, :, None, :, None]).reshape(j, g * a, g * b)


def _block_diag_take(m, a, b):
    eye = jnp.eye(8, dtype=m.dtype)
    return (m.reshape(4, 8, a, 8, b) * eye[None, :, None, :, None]).sum(axis=3)


def kernel(x, mem, g_mix_pre, w_in, gm_ln_g, gm_ln_b, gm_w_s, gm_b_s, s5_lam_re, s5_lam_im, s5_log_step, s5_b_re, s5_b_im, s5_c_re, s5_c_im, s5_d, s5_w_glu, w_br_gm, w_br_s5, w_mix_out, g_mix_post, g_ca_pre, g_mem, ca_w_q, ca_w_kv, ca_w_o, g_ca_post, g_ffn_pre, ffn_w_gu, ffn_w_down, g_ffn_post, loss_target, m_g_mix_pre, m_w_in, m_gm_ln_g, m_gm_ln_b, m_gm_w_s, m_gm_b_s, m_s5_lam_re, m_s5_lam_im, m_s5_log_step, m_s5_b_re, m_s5_b_im, m_s5_c_re, m_s5_c_im, m_s5_d, m_s5_w_glu, m_w_br_gm, m_w_br_s5, m_w_mix_out, m_g_mix_post, m_g_ca_pre, m_g_mem, m_ca_w_q, m_ca_w_kv, m_ca_w_o, m_g_ca_post, m_g_ffn_pre, m_ffn_w_gu, m_ffn_w_down, m_g_ffn_post, v_g_mix_pre, v_w_in, v_gm_ln_g, v_gm_ln_b, v_gm_w_s, v_gm_b_s, v_s5_lam_re, v_s5_lam_im, v_s5_log_step, v_s5_b_re, v_s5_b_im, v_s5_c_re, v_s5_c_im, v_s5_d, v_s5_w_glu, v_w_br_gm, v_w_br_s5, v_w_mix_out, v_g_mix_post, v_g_ca_pre, v_g_mem, v_ca_w_q, v_ca_w_kv, v_ca_w_o, v_g_ca_post, v_g_ffn_pre, v_ffn_w_gu, v_ffn_w_down, v_g_ffn_post):
    a = dict(locals())
    w = {n: a[n][0] for n in WNAMES}
    nb, seq, _ = x.shape
    n = nb * seq
    mlen = mem.shape[1]
    tm = min(256, seq)
    tmb = min(512, seq)
    tpb = seq // tmb
    xf = x.reshape(n, D)
    tgt = loss_target.reshape(n, D)
    memf = mem.reshape(nb * mlen, D)
    tmm = min(256, mlen)

    big_names = list(BIG)
    local_shapes = {k: w[k].shape for k in big_names}
    w_in_full = _run_comm("gather_w_in", _gather_weights("cast_w_in", [w['w_in']], [BIG['w_in']]))[0]
    rest = [k for k in big_names if k != 'w_in']
    gather_rest = _gather_weights("cast_w_rest", [w[k] for k in rest], [BIG[k] for k in rest])

    vec = lambda name: w[name].reshape(1, D)

    to_lane = lambda p: p.reshape(1, -1)
    b_t = lambda p: p.transpose(2, 0, 1).reshape(16, -1)
    lr_l, li_l = to_lane(w['s5_lam_re']), to_lane(w['s5_lam_im'])
    ls_l = jnp.repeat(w['s5_log_step'], 64).reshape(1, -1)
    br_t, bi_t = b_t(w['s5_b_re']), b_t(w['s5_b_im'])
    ab_re, ab_im, bb_re, bb_im = _s5_disc_fwd(lr_l, li_l, ls_l, br_t, bi_t)
    blk = lambda t: _block_diag(t.reshape(16, 4, 8, 64).transpose(1, 2, 0, 3))
    bbc = jnp.concatenate([blk(bb_re), blk(bb_im)], axis=2).astype(BF16)
    cblk = lambda c: _block_diag(c.reshape(4, 8, 16, 64).transpose(0, 1, 3, 2))
    ccm = jnp.concatenate([cblk(w['s5_c_re']), -cblk(w['s5_c_im'])], axis=1).astype(BF16)
    avec = jnp.concatenate([ab_re.reshape(4, 1, 512), ab_im.reshape(4, 1, 512)], axis=2)
    dvec = w['s5_d'].reshape(4, 1, LANE)

    bs3 = w['gm_b_s'].reshape(GM_GROUPS, GM_CHUNK, 1)
    wf = {'w_in': w_in_full}
    h, zgm, zs5, zg = _f_in(xf, vec('g_mix_pre'), wf['w_in'], tmb)
    ygm = _f_gmlp(zgm, vec('gm_ln_g'), vec('gm_ln_b'), w['gm_w_s'], bs3, tmb)
    s5_t = min(S5_T, seq)
    (ypre, states), gathered = _f_s5(zs5, bbc, ccm, avec, dvec, nb, s5_t, gather_rest)
    for k, g in zip(rest, gathered):
        wf[k] = g.reshape(-1, g.shape[-1]) if BIG[k] == 0 else g
    yg, ys5, a_br, b_br, merged, mo, x1 = _f_mix(ygm, ypre, zg, xf, wf['s5_w_glu'], wf['w_br_gm'], wf['w_br_s5'],
                                                 wf['w_mix_out'], vec('g_mix_post'), tm)
    mem_n, kk, vv = _f_mem(memf, vec('g_mem'), wf['ca_w_kv'], tmm)
    hc, o_att, ao, x2 = _f_attn(x1, vec('g_ca_pre'), wf['ca_w_q'], kk, vv, wf['ca_w_o'], vec('g_ca_post'),
                                tmb, tpb, mlen)
    hf, gu, act = _f_ffn_up(x2, vec('g_ffn_pre'), wf['ffn_w_gu'], tm)
    dn, d3, loss_cols = _f_ffn_down(act, x2, tgt, wf['ffn_w_down'], vec('g_ffn_post'), tmb)

    gsm = {}
    gbig = {}
    ddn, dgu, gsm['g_ffn_post'] = _b_ffn_down(d3, dn, gu, vec('g_ffn_post'), wf['ffn_w_down'], tm)
    dx2, gsm['g_ffn_pre'] = _b_ffn_up(dgu, d3, x2, wf['ffn_w_gu'], vec('g_ffn_pre'), tmb)
    gbig['ffn_w_down'] = _mm_tn("dw_ffn_down", act, ddn)
    gbig['ffn_w_gu'] = _mm_tn("dw_ffn_gu", dgu, hf)
    dao, dq, dx1, dk, dv, gsm['g_ca_post'], gsm['g_ca_pre'] = _b_attn(
        dx2, ao, x1, hc, vec('g_ca_post'), wf['ca_w_o'], wf['ca_w_q'], vec('g_ca_pre'), kk, vv, tmb, tpb, mlen)
    gbig['ca_w_o'] = _mm_tn("dw_ca_o", o_att, dao)
    gbig['ca_w_q'] = _mm_tn("dw_ca_q", hc, dq)
    dkv, gsm['g_mem'] = _b_mem(dk, dv, memf, wf['ca_w_kv'], vec('g_mem'), tmm)
    gbig['ca_w_kv'] = _mm_tn("dw_ca_kv", dkv, mem_n)
    dmo, da_br, db_br, dzg, dygm, dgate, dypre, gsm['g_mix_post'] = _b_mix(
        dx1, mo, a_br, b_br, zg, ypre, vec('g_mix_post'), wf['w_mix_out'], wf['w_br_gm'], wf['w_br_s5'],
        wf['s5_w_glu'], tm)
    gbig['w_mix_out'] = _mm_tn("dw_mix_out", merged, dmo)
    gbig['w_br_gm'] = _mm_tn("dw_br_gm", ygm, da_br)
    gbig['w_br_s5'] = _mm_tn("dw_br_s5", db_br, ys5)
    gbig['s5_w_glu'] = _mm_tn("dw_s5_glu", yg, dgate)
    slab_of = lambda k: gbig[k].reshape(4, -1, D)
    red_rest = _reduce_big("rest", [slab_of(k) for k in rest])
    (dzs5, dbbc, dccm_t, davec, ddvec), got = _b_s5(zs5, dypre, states, bbc, ccm, avec, dvec, nb, s5_t,
                                                    next(red_rest))
    dccm = dccm_t.transpose(0, 2, 1)
    (dzgm, gsm['gm_w_s'], dbs3, gsm['gm_ln_g'], gsm['gm_ln_b']), got = _b_gmlp(
        zgm, dygm, vec('gm_ln_g'), vec('gm_ln_b'), w['gm_w_s'], bs3, tmb, red_rest.send(got))
    gsm['gm_b_s'] = dbs3
    (grad_x, gsm['g_mix_pre']), got = _b_in(dzgm, dzs5, dzg, dx1, xf, wf['w_in'], vec('g_mix_pre'), tmb,
                                            red_rest.send(got))
    dw_in_gm, got = _mm_tn("dw_in_gm", dzgm, h, comm=red_rest.send(got))
    greds = dict(zip(rest, _finish(red_rest, got)))
    dw_in_s5 = _mm_tn("dw_in_s5", dzs5, h)

    unblk = lambda m_: _block_diag_take(m_, 16, 64).transpose(2, 0, 1, 3).reshape(16, -1)
    d_bb_re, d_bb_im = unblk(dbbc[:, :, :512]), unblk(dbbc[:, :, 512:])
    cunblk = lambda m_: _block_diag_take(m_, 64, 16).transpose(0, 1, 3, 2).reshape(32, 16, 64)
    gsm['s5_c_re'] = cunblk(dccm[:, :512, :])
    gsm['s5_c_im'] = -cunblk(dccm[:, 512:, :])
    d_ab_re, d_ab_im = davec[:, :, :512].reshape(1, -1), davec[:, :, 512:].reshape(1, -1)
    g_lr, g_li, g_ls, g_br, g_bi = _s5_disc_bwd(lr_l, li_l, ls_l, br_t, bi_t, (d_ab_re, d_ab_im, d_bb_re, d_bb_im))
    gsm['s5_lam_re'], gsm['s5_lam_im'] = g_lr.reshape(32, 64), g_li.reshape(32, 64)
    gsm['s5_log_step'] = g_ls.reshape(32, 64).sum(axis=1)
    from_t = lambda t: t.reshape(16, 32, 64).transpose(1, 2, 0)
    gsm['s5_b_re'], gsm['s5_b_im'] = from_t(g_br), from_t(g_bi)
    gsm['s5_d'] = ddvec.reshape(32, 16)

    small_shapes = [w[k].shape for k in SMALL]
    spack = _pack([gsm[k].reshape(w[k].shape) for k in SMALL] + [loss_cols], 8)
    rs = spack.shape[0]
    me_slot = 4 * lax.axis_index("x") + 2 * lax.axis_index("y") + lax.axis_index("c")
    dw_in_g, (sall,) = _mm_tn("dw_in_g", dzg, h, comm=_gather_all(spack))
    gbig['w_in'] = jnp.concatenate([dw_in_gm, dw_in_s5, dw_in_g], axis=0)
    sall = lax.dynamic_update_index_in_dim(sall, spack, me_slot, 0)
    ssum = _sum_slots("sum_small", sall)
    small_red = _unpack(ssum, small_shapes + [(1, D)])
    loss = 0.5 * jnp.sum(small_red[-1]) / D

    red_in = _reduce_big("w_in", [slab_of('w_in')])
    got = _run_comm("reduce_pair_w_in", next(red_in))
    for phase in ("reduce_step1_w_in", "reduce_step2_w_in"):
        got = _run_comm(phase, red_in.send(got))
    got = _run_comm("share_pair_w_in", red_in.send(got))
    greds['w_in'] = _finish(red_in, got)[0]

    res_big = {}
    for k in big_names:
        r_, c_ = local_shapes[k]
        g = greds[k].reshape(r_, c_) if BIG[k] == 0 else greds[k].reshape(c_, r_).T
        dl, mn, vn = _adamw("adamw_" + k, w[k], g, a['m_' + k][0], a['v_' + k][0], _tile_rows(g.shape[0], 256))
        res_big[k] = (g, dl, mn, vn)
    as2d = lambda t: t.reshape(1, -1) if t.ndim == 1 else t
    dl_s, mn_s, vn_s = _adamw_many("adamw_small", [as2d(w[k]) for k in SMALL], [as2d(g) for g in small_red[:-1]],
                                   [as2d(a['m_' + k][0]) for k in SMALL], [as2d(a['v_' + k][0]) for k in SMALL])
    res_small = {k: (g, dl.reshape(w[k].shape), mn.reshape(w[k].shape), vn.reshape(w[k].shape))
                 for k, g, dl, mn, vn in zip(SMALL, small_red[:-1], dl_s, mn_s, vn_s)}

    res = {**res_big, **res_small}
    outs = [loss, grad_x.reshape(nb, seq, D)]
    for i in range(4):
        outs += [res[k][i][None] for k in WNAMES]
    return tuple(outs)


def _finish(gen, got):
    try:
        gen.send(got)
    except StopIteration as done:
        return done.value
    raise AssertionError("the generator has more exchanges")


def _tile_rows(rows, cap=384):
    best = rows
    for t in range(16, min(rows, cap) + 1, 16):
        if rows % t == 0:
            best = t
    return best
```

```python
import math
from typing import Any, NamedTuple

import jax
import jax.numpy as jnp
from jax import lax
from jax.experimental import pallas as pl
from jax.experimental.pallas import tpu as pltpu

F32 = jnp.float32
BF16 = jnp.bfloat16
EPS = 1e-6
D = 1024
GM_CHUNK = 128
GM_GROUPS = 8
S5_W = 512
S5_SUPER = 4
S5_T = 512
HEADS = 4
HEAD_DIM = 256
FFN_H = 2816
LANE = 128
VMEM_LIMIT = 56 * 1024 * 1024
MESH = pl.DeviceIdType.MESH

ADAM_LR, ADAM_B1, ADAM_B2, ADAM_EPS, ADAM_WD, ADAM_STEP = 0.001, 0.9, 0.999, 1e-08, 0.01, 10

WNAMES = ['g_mix_pre', 'w_in', 'gm_ln_g', 'gm_ln_b', 'gm_w_s', 'gm_b_s', 's5_lam_re', 's5_lam_im', 's5_log_step',
          's5_b_re', 's5_b_im', 's5_c_re', 's5_c_im', 's5_d', 's5_w_glu', 'w_br_gm', 'w_br_s5', 'w_mix_out',
          'g_mix_post', 'g_ca_pre', 'g_mem', 'ca_w_q', 'ca_w_kv', 'ca_w_o', 'g_ca_post', 'g_ffn_pre', 'ffn_w_gu',
          'ffn_w_down', 'g_ffn_post']
BIG = {'w_in': 1, 's5_w_glu': 0, 'w_br_gm': 0, 'w_br_s5': 1, 'w_mix_out': 0, 'ca_w_q': 0, 'ca_w_kv': 1,
       'ca_w_o': 0, 'ffn_w_gu': 1, 'ffn_w_down': 0}
SMALL = [n for n in WNAMES if n not in BIG]


def _dot(a, b):
    return jnp.dot(a, b, preferred_element_type=F32)


def _dot_nt(a, b):
    return lax.dot_general(a, b, (((1,), (1,)), ((), ())), preferred_element_type=F32)


def _dot_tn(a, b):
    return lax.dot_general(a, b, (((0,), (0,)), ((), ())), preferred_element_type=F32)


def _rms_fwd(x, g):
    r = lax.rsqrt(jnp.mean(x * x, axis=-1, keepdims=True) + EPS)
    return x * r * g


def _rms_bwd(x, g, dy):
    r = lax.rsqrt(jnp.mean(x * x, axis=-1, keepdims=True) + EPS)
    xh = x * r
    gdy = dy * g
    dx = r * (gdy - xh * jnp.mean(gdy * xh, axis=-1, keepdims=True))
    dg = jnp.sum(dy * xh, axis=0, keepdims=True)
    return dx, dg


_GC = math.sqrt(2.0 / math.pi)


def _gelu(x):
    return 0.5 * x * (1.0 + jnp.tanh(_GC * (x + 0.044715 * x * x * x)))


def _gelu_grad(x):
    t = jnp.tanh(_GC * (x + 0.044715 * x * x * x))
    return 0.5 * (1.0 + t) + 0.5 * x * (1.0 - t * t) * _GC * (1.0 + 3 * 0.044715 * x * x)


def _sig(x):
    return 1.0 / (1.0 + jnp.exp(-x))


def _cscan(br, bi, ar, ai, reverse):
    t = br.shape[0]
    row = lax.broadcasted_iota(jnp.int32, br.shape, 0)
    pr, pi = ar, ai
    sh = 1
    while sh < t:
        if reverse:
            keep = row < t - sh
            rr, ri = pltpu.roll(br, t - sh, 0), pltpu.roll(bi, t - sh, 0)
        else:
            keep = row >= sh
            rr, ri = pltpu.roll(br, sh, 0), pltpu.roll(bi, sh, 0)
        rr = jnp.where(keep, rr, 0.0)
        ri = jnp.where(keep, ri, 0.0)
        br, bi = br + pr * rr - pi * ri, bi + pr * ri + pi * rr
        pr, pi = pr * pr - pi * pi, 2.0 * pr * pi
        sh *= 2
    return br, bi


def _s5_disc(lr, li, ls, br, bi):
    step = jnp.exp(ls)
    mag = jnp.exp(lr * step)
    ab_re = mag * jnp.cos(li * step)
    ab_im = mag * jnp.sin(li * step)
    den = lr * lr + li * li
    nr = ab_re - 1.0
    co_re = (nr * lr + ab_im * li) / den
    co_im = (ab_im * lr - nr * li) / den
    return ab_re, ab_im, co_re * br - co_im * bi, co_re * bi + co_im * br


def _params(sem=None):
    return pltpu.CompilerParams(dimension_semantics=sem, vmem_limit_bytes=VMEM_LIMIT)


def _rowcall(name, body, n_rows, tm, ins, outs, scratch=(), comm=None):
    def spec(kind, shape):
        if kind == 'row':
            return pl.BlockSpec((tm, shape[1]), lambda i: (i, 0))
        if kind == 'full':
            nd = len(shape)
            return pl.BlockSpec(tuple(shape), lambda i: (0,) * nd, pipeline_mode=pl.Buffered(1))
        if kind == 'acc':
            nd = len(shape)
            return pl.BlockSpec(tuple(shape), lambda i: (0,) * nd)
        return kind

    in_specs = [spec(k, a.shape) for a, k in ins]
    out_shape, out_specs = [], []
    for k, s, dt in outs:
        shape = (n_rows, s) if k == 'row' else tuple(s)
        out_shape.append(jax.ShapeDtypeStruct(shape, dt))
        out_specs.append(spec(k, shape))
    args = [a for a, _ in ins]
    if comm is not None:
        return _hosting_call(body, name, (n_rows // tm,), in_specs, out_specs, out_shape, list(scratch),
                             ("arbitrary",), args, comm)
    return pl.pallas_call(
        body, name=name, grid=(n_rows // tm,), in_specs=in_specs, out_specs=out_specs, out_shape=out_shape,
        scratch_shapes=list(scratch), compiler_params=_params(("arbitrary",)),
    )(*args)


def _tile(n, cap):
    if n <= cap:
        return n
    best = LANE
    for k in range(1, n // LANE + 1):
        t = k * LANE
        if n % t == 0 and t <= cap:
            best = t
    return best


def _mm_tn(name, a, b, col_shards=1, comm=None):
    n, k = a.shape
    m = b.shape[1]
    mloc = m // col_shards
    tk, tn, tr = _tile(k, 1536), _tile(mloc, 1536), min(n, 1024)
    per = mloc // tn

    def body(a_ref, b_ref, o_ref):
        @pl.when(pl.program_id(2) == 0)
        def _():
            o_ref[...] = jnp.zeros_like(o_ref)
        o_ref[...] += _dot_tn(a_ref[...].astype(BF16), b_ref[...].astype(BF16))

    if col_shards == 1:
        out_spec = pl.BlockSpec((tk, tn), lambda i, j, r: (i, j))
        out_shape = (k, m)
    else:
        out_spec = pl.BlockSpec((None, tk, tn), lambda i, j, r: (j // per, i, j % per))
        out_shape = (col_shards, k, mloc)
    grid = (k // tk, m // tn, n // tr)
    in_specs = [pl.BlockSpec((tr, tk), lambda i, j, r: (r, i)), pl.BlockSpec((tr, tn), lambda i, j, r: (r, j))]
    if comm is not None:
        (res,), extra = _hosting_call(body, name, grid, in_specs, [out_spec], [jax.ShapeDtypeStruct(out_shape, F32)],
                                      [], ("arbitrary", "arbitrary", "arbitrary"), (a, b), comm)
        return res, extra
    return pl.pallas_call(
        body, name=name, grid=grid, in_specs=in_specs, out_specs=out_spec,
        out_shape=jax.ShapeDtypeStruct(out_shape, F32),
        compiler_params=_params(("parallel", "parallel", "arbitrary")),
    )(a, b)


def _comm_call(name, srcs, outs, nsem, body_fn, aliases=None):
    ns, no = len(srcs), len(outs)

    def body(*refs):
        body_fn(refs[:ns], refs[ns:ns + no], refs[ns + no], refs[ns + no + 1])

    hbm = pl.BlockSpec(memory_space=pltpu.HBM)
    return pl.pallas_call(
        body, name=name, in_specs=[hbm] * ns, out_specs=[hbm] * no,
        out_shape=[jax.ShapeDtypeStruct(s, d) for s, d in outs],
        scratch_shapes=[pltpu.SemaphoreType.DMA((nsem,)), pltpu.SemaphoreType.DMA((nsem,))],
        input_output_aliases=aliases or {},
    )(*srcs)


def _place():
    x, y, c = lax.axis_index("x"), lax.axis_index("y"), lax.axis_index("c")
    chips = (2 * x + y, 2 * (1 - x) + y, 2 * x + (1 - y), 2 * (1 - x) + (1 - y))
    peers = ((1 - x, y, c), (x, 1 - y, c), (x, y, 1 - c))
    return c, chips, peers


class _Comm(NamedTuple):
    srcs: Any
    outs: Any
    nsem: int
    stages: Any
    aliases: Any
    fracs: Any


class _Lazy:
    def __init__(self, make):
        self._make = make

    def start(self):
        self._make().start()

    def wait(self):
        self._make().wait()

    def wait_recv(self):
        self._make().wait_recv()

    def wait_send(self):
        self._make().wait_send()


def _run_comm(name, comm):
    def body_fn(srcs, dsts, ss, rs):
        for stage in comm.stages:
            stage(srcs, dsts, ss, rs)

    return _comm_call(name, comm.srcs, comm.outs, comm.nsem, body_fn, aliases=comm.aliases)


def _hosting_call(body, name, grid, in_specs, out_specs, out_shape, scratch, sem, args, comm):
    hbm = pl.BlockSpec(memory_space=pltpu.HBM)
    n_in, n_out, n_sc = len(in_specs), len(out_specs), len(scratch)
    ns, no = len(comm.srcs), len(comm.outs)
    total = math.prod(grid)
    steps = [min(total - 1, int(f * total)) for f in comm.fracs]

    def wrapped(*refs):
        ins, csrc = refs[:n_in], refs[n_in:n_in + ns]
        outs, cdst = refs[n_in + ns:n_in + ns + n_out], refs[n_in + ns + n_out:n_in + ns + n_out + no]
        rest = refs[n_in + ns + n_out + no:]
        lin = 0
        for d, size in enumerate(grid):
            lin = lin * size + pl.program_id(d)
        for stage, at in zip(comm.stages, steps):
            @pl.when(lin == at)
            def _(stage=stage):
                stage(csrc, cdst, rest[n_sc], rest[n_sc + 1])
        body(*ins, *outs, *rest[:n_sc])

    res = pl.pallas_call(
        wrapped, name=name, grid=grid, in_specs=list(in_specs) + [hbm] * ns, out_specs=list(out_specs) + [hbm] * no,
        out_shape=list(out_shape) + [jax.ShapeDtypeStruct(s, d) for s, d in comm.outs],
        scratch_shapes=list(scratch) + [pltpu.SemaphoreType.DMA((comm.nsem,)), pltpu.SemaphoreType.DMA((comm.nsem,))],
        input_output_aliases={n_in + i: n_out + o for i, o in comm.aliases.items()},
        compiler_params=_params(sem),
    )(*args, *comm.srcs)
    return res[:n_out], res[n_out:]


def _gather_weights(name, shards, axes):
    nw = len(shards)
    shapes = [s.shape for s in shards]
    outs = [((4,) + s if ax == 0 else (s[0], 4 * s[1]), BF16) for s, ax in zip(shapes, axes)]

    def win(ref, i, chip, start, rows):
        r, cols = shapes[i]
        if axes[i] == 0:
            return ref.at[chip, pl.ds(start, rows), :]
        return ref.at[pl.ds(start, rows), pl.ds(chip * cols, cols)]

    def place_body(*refs):
        srcs, dsts, bufs, sem = refs[:nw], refs[nw:2 * nw], refs[2 * nw:3 * nw], refs[3 * nw]
        me = 2 * lax.axis_index("x") + lax.axis_index("y")
        cps = []
        for i in range(nw):
            bufs[i][...] = srcs[i][...].astype(BF16)
            cps.append(pltpu.make_async_copy(bufs[i], win(dsts[i], i, me, 0, shapes[i][0]), sem.at[i]))
            cps[-1].start()
        for cp in cps:
            cp.wait()

    placed = pl.pallas_call(
        place_body, name=name, out_shape=[jax.ShapeDtypeStruct(s, d) for s, d in outs],
        in_specs=[pl.BlockSpec(memory_space=pltpu.VMEM)] * nw, out_specs=[pl.BlockSpec(memory_space=pltpu.HBM)] * nw,
        scratch_shapes=[pltpu.VMEM(s, BF16) for s in shapes] + [pltpu.SemaphoreType.DMA((nw,))],
        compiler_params=_params(),
    )(*shards)

    def copies(dsts, ss, rs):
        c, (me, xn, yn, dg), (px, py, psib) = _place()

        def rdma(k, window, peer):
            return _Lazy(lambda: pltpu.make_async_remote_copy(
                src_ref=window, dst_ref=window, send_sem=ss.at[k], recv_sem=rs.at[k], device_id=peer,
                device_id_type=MESH))

        first, later, swaps = [], [], []
        for i in range(nw):
            qr, hr, k0 = shapes[i][0] // 4, shapes[i][0] // 2, 6 * i
            q0, q1 = 2 * c * qr, (2 * c + 1) * qr
            w0, w1 = win(dsts[i], i, me, q0, qr), win(dsts[i], i, me, q1, qr)
            first.append([rdma(k0, w0, px), rdma(k0 + 1, w1, py), rdma(k0 + 2, w0, py), rdma(k0 + 3, w1, px)])
            later.append([rdma(k0 + 4, win(dsts[i], i, xn, q0, qr), py), rdma(k0 + 5, win(dsts[i], i, yn, q1, qr), px)])
            swaps.append([rdma(6 * nw + 3 * i + j, win(dsts[i], i, chip, c * hr, hr), psib)
                          for j, chip in enumerate((xn, yn, dg))])
        return first, later, swaps

    def send(srcs, dsts, ss, rs):
        for cps in copies(dsts, ss, rs)[0]:
            for cp in cps:
                cp.start()

    def forward(srcs, dsts, ss, rs):
        first, later, _ = copies(dsts, ss, rs)
        for i in range(nw):
            first[i][0].wait_recv()
            later[i][0].start()
            first[i][1].wait_recv()
            later[i][1].start()

    def swap(srcs, dsts, ss, rs):
        first, later, swaps = copies(dsts, ss, rs)
        for i in range(nw):
            for cp in first[i][2:] + later[i]:
                cp.wait_recv()
            for sw in swaps[i]:
                sw.start()

    def finish(srcs, dsts, ss, rs):
        first, later, swaps = copies(dsts, ss, rs)
        for i in range(nw):
            for sw in swaps[i]:
                sw.wait()
            for cp in first[i] + later[i]:
                cp.wait_send()

    return _Comm(placed, outs, 9 * nw, [send, forward, swap, finish], {i: i for i in range(nw)},
                 (0.0, 0.55, 0.85, 1.0))


def _gather_all(src):
    def copies(srcs, dsts, ss, rs):
        x, y, c = lax.axis_index("x"), lax.axis_index("y"), lax.axis_index("c")
        me, sib = (x, y, c), (x, y, 1 - c)
        chips = [(1 - x, y), (x, 1 - y), (1 - x, 1 - y)]

        def rows(px, py, pc):
            return dsts[0].at[4 * px + 2 * py + pc]

        def copy(k, block, to, src=None):
            return _Lazy(lambda: pltpu.make_async_remote_copy(
                src_ref=rows(*block) if src is None else src, dst_ref=rows(*block), send_sem=ss.at[k],
                recv_sem=rs.at[k], device_id=to, device_id_type=MESH))

        first = [copy(0, me, sib, src=srcs[0])] + [copy(1 + j, me, (*chip, c), src=srcs[0])
                                                   for j, chip in enumerate(chips)]
        passed = [copy(4 + j, (*chip, c), sib) for j, chip in enumerate(chips)]
        landed = [copy(0, sib, me)] + [copy(1 + j, (*chip, c), me) for j, chip in enumerate(chips)]
        landed += [copy(4 + j, (*chip, 1 - c), me) for j, chip in enumerate(chips)]
        return first, passed, landed

    def send(*refs):
        for cp in copies(*refs)[0]:
            cp.start()

    def forward(*refs):
        _, passed, landed = copies(*refs)
        for j in range(3):
            landed[1 + j].wait_recv()
            passed[j].start()

    def finish(*refs):
        first, passed, landed = copies(*refs)
        landed[0].wait_recv()
        for cp in landed[4:]:
            cp.wait_recv()
        for cp in first + passed:
            cp.wait_send()

    return _Comm([src], [((8,) + src.shape, src.dtype)], 7, [send, forward, finish], {}, (0.0, 0.5, 1.0))


def _f_in(x, g, w_in, tm):
    n = x.shape[0]

    def body(x_ref, g_ref, w_ref, h_ref, zgm_ref, zs5_ref, zg_ref):
        h = _rms_fwd(x_ref[...], g_ref[...]).astype(BF16)
        h_ref[...] = h
        zgm_ref[...] = _dot(h, w_ref[:, 0:2 * D])
        zs5_ref[...] = _dot(h, w_ref[:, 2 * D:2 * D + S5_W])
        zg_ref[...] = _dot(h, w_ref[:, 2 * D + S5_W:])

    return _rowcall("f_in", body, n, tm, [(x, 'row'), (g, 'full'), (w_in, 'full')],
                    [('row', D, BF16), ('row', 2 * D, F32), ('row', S5_W, F32), ('row', 2 * D, F32)])


def _tril():
    r = lax.broadcasted_iota(jnp.int32, (GM_CHUNK, GM_CHUNK), 0)
    c = lax.broadcasted_iota(jnp.int32, (GM_CHUNK, GM_CHUNK), 1)
    return r >= c


def _ln_stats(v):
    mu = jnp.mean(v, axis=-1, keepdims=True)
    vc = v - mu
    r = lax.rsqrt(jnp.mean(vc * vc, axis=-1, keepdims=True) + EPS)
    return vc * r, r


def _f_gmlp(zgm, ln_g, ln_b, w_s, b_s, tm):
    n = zgm.shape[0]

    def body(z_ref, lg_ref, lb_ref, ws_ref, bs_ref, y_ref):
        zg = _gelu(z_ref[...])
        u = zg[:, :D]
        vh, _ = _ln_stats(zg[:, D:])
        vn = (vh * lg_ref[...] + lb_ref[...]).astype(BF16)
        keep = _tril()
        for g in range(GM_GROUPS):
            w = jnp.where(keep, ws_ref[g], 0.0).astype(BF16)
            cs = slice(g * LANE, (g + 1) * LANE)
            for c in range(tm // GM_CHUNK):
                rs = slice(c * GM_CHUNK, (c + 1) * GM_CHUNK)
                sv = _dot(w, vn[rs, cs]) + bs_ref[g]
                y_ref[rs, cs] = (u[rs, cs] * sv).astype(BF16)

    return _rowcall("f_gmlp", body, n, tm,
                    [(zgm, 'row'), (ln_g, 'full'), (ln_b, 'full'), (w_s, 'full'), (b_s, 'full')],
                    [('row', D, BF16)])[0]


def _s5_specs(nb, nc, t, rev):
    def cc(c):
        return nc - 1 - c if rev else c
    slab = pl.BlockSpec((t, LANE), lambda j, b, c: (b * nc + cc(c), j))
    bb = pl.BlockSpec((None, LANE, 1024), lambda j, b, c: (j, 0, 0))
    cm = pl.BlockSpec((None, 1024, LANE), lambda j, b, c: (j, 0, 0))
    av = pl.BlockSpec((None, 1, 1024), lambda j, b, c: (j, 0, 0))
    dv = pl.BlockSpec((None, 1, LANE), lambda j, b, c: (j, 0, 0))
    st = pl.BlockSpec((None, None, None, 1, 1024), lambda j, b, c: (j, b, cc(c), 0, 0))
    return slab, bb, cm, av, dv, st


def _seg_load(ref, seg):
    return jnp.concatenate([ref[pl.ds(k, 8, stride=seg), :] for k in range(seg)], axis=0)


def _seg_store(ref, val, seg):
    for k in range(seg):
        ref[pl.ds(k, 8, stride=seg), :] = val[8 * k:8 * k + 8, :]


def _seg_scan(s_ref, ar, ai, seg, reverse):
    def step(i, carry):
        k = seg - 1 - i if reverse else i
        sr, si = carry
        nr = ar * sr - ai * si + s_ref[k, :, :512]
        ni = ar * si + ai * sr + s_ref[k, :, 512:]
        s_ref[k, :, :512] = nr
        s_ref[k, :, 512:] = ni
        return nr, ni

    zero = jnp.zeros((8, 512), F32)
    return lax.fori_loop(0, seg, step, (zero, zero), unroll=4)


def _seg_entries(tr, ti, pr, pi, cin, reverse):
    row = lax.broadcasted_iota(jnp.int32, (8, 512), 0)
    edge = row == (7 if reverse else 0)
    cr, ci = cin[:, :512], cin[:, 512:]
    xr = tr + jnp.where(edge, pr * cr - pi * ci, 0.0)
    xi = ti + jnp.where(edge, pr * ci + pi * cr, 0.0)
    ir, ii = _cscan(xr, xi, pr, pi, reverse)
    shift = 7 if reverse else 1
    er = jnp.where(edge, cr, pltpu.roll(ir, shift, 0))
    ei = jnp.where(edge, ci, pltpu.roll(ii, shift, 0))
    far = row == (0 if reverse else 7)
    out = jnp.concatenate([jnp.sum(jnp.where(far, ir, 0.0), axis=0, keepdims=True),
                           jnp.sum(jnp.where(far, ii, 0.0), axis=0, keepdims=True)], axis=1)
    return er, ei, out


def _seg_power(ar, ai, seg):
    pr, pi = ar, ai
    for _ in range(seg.bit_length() - 1):
        pr, pi = pr * pr - pi * pi, 2.0 * pr * pi
    return pr, pi


def _f_s5(zs5, bbc, ccm, avec, dvec, nb, t, comm):
    n = zs5.shape[0]
    nc = n // nb // t
    seg = t // 8
    slab, bb, cm, av, dv, st = _s5_specs(nb, nc, t, False)

    def body(u_ref, bb_ref, cc_ref, a_ref, d_ref, y_ref, st_ref, carry, s_ref):
        @pl.when(pl.program_id(2) == 0)
        def _():
            carry[...] = jnp.zeros_like(carry)
        cin = carry[...]
        st_ref[...] = cin
        up = _seg_load(u_ref, seg)
        s_ref[...] = _dot(up.astype(BF16), bb_ref[...]).reshape(seg, 8, 1024)
        a1r, a1i = a_ref[:, :512], a_ref[:, 512:]
        ar, ai = jnp.broadcast_to(a1r, (8, 512)), jnp.broadcast_to(a1i, (8, 512))
        tr, ti = _seg_scan(s_ref, ar, ai, seg, False)
        er, ei, cout = _seg_entries(tr, ti, *_seg_power(a1r, a1i, seg), cin, False)
        carry[...] = cout

        def apply(k, pe):
            pr, pi = pe
            pr, pi = ar * pr - ai * pi, ar * pi + ai * pr
            s_ref[k, :, :512] = s_ref[k, :, :512] + pr
            s_ref[k, :, 512:] = s_ref[k, :, 512:] + pi
            return pr, pi

        lax.fori_loop(0, seg, apply, (er, ei), unroll=4)
        s = s_ref[...].reshape(t, 1024).astype(BF16)
        _seg_store(y_ref, _dot(s, cc_ref[...]) + d_ref[...] * up, seg)

    return _hosting_call(
        body, "f_s5", (S5_SUPER, nb, nc), [slab, bb, cm, av, dv], [slab, st],
        [jax.ShapeDtypeStruct((n, S5_W), F32), jax.ShapeDtypeStruct((S5_SUPER, nb, nc, 1, 1024), F32)],
        [pltpu.VMEM((1, 1024), F32), pltpu.VMEM((seg, 8, 1024), F32)], ("arbitrary", "arbitrary", "arbitrary"),
        (zs5, bbc, ccm, avec, dvec), comm)


def _f_mix(ygm, ypre, zg, x, w_glu, w_br_gm, w_br_s5, w_mix_out, g_post, tm):
    n = x.shape[0]

    def body(ygm_ref, ypre_ref, zg_ref, x_ref, wglu_ref, wgm_ref, ws5_ref, wout_ref, g_ref,
             yg_ref, ys5_ref, a_ref, b_ref, mg_ref, mo_ref, x1_ref):
        yg = _gelu(ypre_ref[...])
        ygb = yg.astype(BF16)
        yg_ref[...] = ygb
        ys5 = (yg * _sig(_dot(ygb, wglu_ref[...]))).astype(BF16)
        ys5_ref[...] = ys5
        a = _dot(ygm_ref[...], wgm_ref[...]).astype(BF16)
        b = _dot(ys5, ws5_ref[...]).astype(BF16)
        a_ref[...] = a
        b_ref[...] = b
        zg = zg_ref[...].astype(BF16)
        merged = _sig(zg[:, :D]) * a + _sig(zg[:, D:]) * b
        mg_ref[...] = merged
        mo = _dot(merged, wout_ref[...])
        mo_ref[...] = mo
        x1_ref[...] = x_ref[...] + _rms_fwd(mo, g_ref[...])

    return _rowcall("f_mix", body, n, tm,
                    [(ygm, 'row'), (ypre, 'row'), (zg, 'row'), (x, 'row'), (w_glu, 'full'), (w_br_gm, 'full'),
                     (w_br_s5, 'full'), (w_mix_out, 'full'), (g_post, 'full')],
                    [('row', S5_W, BF16), ('row', S5_W, BF16), ('row', D, BF16), ('row', D, BF16),
                     ('row', D, BF16), ('row', D, F32), ('row', D, F32)])


def _f_mem(mem, g_mem, w_kv, tm):
    n = mem.shape[0]

    def body(m_ref, g_ref, w_ref, mn_ref, k_ref, v_ref):
        mn = _rms_fwd(m_ref[...], g_ref[...]).astype(BF16)
        mn_ref[...] = mn
        k_ref[...] = _dot(mn, w_ref[:, :D]).astype(BF16)
        v_ref[...] = _dot(mn, w_ref[:, D:]).astype(BF16)

    return _rowcall("f_mem", body, n, tm, [(mem, 'row'), (g_mem, 'full'), (w_kv, 'full')],
                    [('row', D, BF16), ('row', D, BF16), ('row', D, BF16)])


def _softmax(s):
    m = jnp.max(s, axis=-1, keepdims=True)
    e = jnp.exp(s - m)
    return e / jnp.sum(e, axis=-1, keepdims=True)


def _f_attn(x1, g_pre, w_q, k, v, w_o, g_post, tm, tpb, mlen):
    n = x1.shape[0]
    kv_spec = pl.BlockSpec((mlen, D), lambda i: (i // tpb, 0))
    scale = HEAD_DIM ** -0.5

    def body(x_ref, gp_ref, wq_ref, k_ref, v_ref, wo_ref, go_ref, hc_ref, o_ref, ao_ref, x2_ref):
        x1v = x_ref[...]
        hc = _rms_fwd(x1v, gp_ref[...]).astype(BF16)
        hc_ref[...] = hc
        q = _dot(hc, wq_ref[...])
        for h in range(HEADS):
            hs = slice(h * HEAD_DIM, (h + 1) * HEAD_DIM)
            p = _softmax(_dot_nt(q[:, hs].astype(BF16), k_ref[:, hs]) * scale)
            o_ref[:, hs] = _dot(p.astype(BF16), v_ref[:, hs]).astype(BF16)
        ao = _dot(o_ref[...], wo_ref[...])
        ao_ref[...] = ao
        x2_ref[...] = x1v + _rms_fwd(ao, go_ref[...])

    return _rowcall("f_attn", body, n, tm,
                    [(x1, 'row'), (g_pre, 'full'), (w_q, 'full'), (k, kv_spec), (v, kv_spec), (w_o, 'full'),
                     (g_post, 'full')],
                    [('row', D, BF16), ('row', D, BF16), ('row', D, F32), ('row', D, F32)])


def _f_ffn_up(x2, g_pre, w_gu, tm):
    n = x2.shape[0]

    def body(x_ref, g_ref, w_ref, hf_ref, gu_ref, act_ref):
        hf = _rms_fwd(x_ref[...], g_ref[...]).astype(BF16)
        hf_ref[...] = hf
        gg = _dot(hf, w_ref[:, :FFN_H])
        uu = _dot(hf, w_ref[:, FFN_H:])
        gu_ref[:, :FFN_H] = gg.astype(BF16)
        gu_ref[:, FFN_H:] = uu.astype(BF16)
        act_ref[...] = (gg * _sig(gg) * uu).astype(BF16)

    return _rowcall("f_ffn_up", body, n, tm, [(x2, 'row'), (g_pre, 'full'), (w_gu, 'full')],
                    [('row', D, BF16), ('row', 2 * FFN_H, BF16), ('row', FFN_H, BF16)])


def _f_ffn_down(act, x2, tgt, w_down, g_post, tm):
    n = x2.shape[0]

    def body(a_ref, x_ref, t_ref, w_ref, g_ref, dn_ref, d3_ref, loss_ref):
        @pl.when(pl.program_id(0) == 0)
        def _():
            loss_ref[...] = jnp.zeros_like(loss_ref)
        dn = _dot(a_ref[...], w_ref[...])
        dn_ref[...] = dn
        err = x_ref[...] + _rms_fwd(dn, g_ref[...]) - t_ref[...]
        d3_ref[...] = err * (1.0 / D)
        loss_ref[...] += jnp.sum(err * err, axis=0, keepdims=True)

    return _rowcall("f_ffn_down", body, n, tm,
                    [(act, 'row'), (x2, 'row'), (tgt, 'row'), (w_down, 'full'), (g_post, 'full')],
                    [('row', D, F32), ('row', D, F32), ('acc', (1, D), F32)])


def _b_ffn_down(d3, dn, gu, g_post, w_down, tm):
    n = d3.shape[0]

    def body(d3_ref, dn_ref, gu_ref, g_ref, w_ref, ddn_ref, dgu_ref, dg_ref):
        @pl.when(pl.program_id(0) == 0)
        def _():
            dg_ref[...] = jnp.zeros_like(dg_ref)
        ddn, dg = _rms_bwd(dn_ref[...], g_ref[...], d3_ref[...])
        dg_ref[...] += dg
        ddn = ddn.astype(BF16)
        ddn_ref[...] = ddn
        dact = _dot_nt(ddn, w_ref[...]).astype(BF16)
        gg = gu_ref[:, :FFN_H]
        uu = gu_ref[:, FFN_H:]
        sg = _sig(gg)
        silu = gg * sg
        dgu_ref[:, :FFN_H] = dact * uu * (sg + silu * (1.0 - sg))
        dgu_ref[:, FFN_H:] = dact * silu

    return _rowcall("b_ffn_down", body, n, tm,
                    [(d3, 'row'), (dn, 'row'), (gu, 'row'), (g_post, 'full'), (w_down, 'full')],
                    [('row', D, BF16), ('row', 2 * FFN_H, BF16), ('acc', (1, D), F32)])


def _b_ffn_up(dgu, d3, x2, w_gu, g_pre, tm):
    n = d3.shape[0]

    def body(dgu_ref, d3_ref, x_ref, w_ref, g_ref, dx_ref, dg_ref):
        @pl.when(pl.program_id(0) == 0)
        def _():
            dg_ref[...] = jnp.zeros_like(dg_ref)
        dhf = _dot_nt(dgu_ref[...], w_ref[...])
        dx, dg = _rms_bwd(x_ref[...], g_ref[...], dhf)
        dg_ref[...] += dg
        dx_ref[...] = d3_ref[...] + dx

    return _rowcall("b_ffn_up", body, n, tm,
                    [(dgu, 'row'), (d3, 'row'), (x2, 'row'), (w_gu, 'full'), (g_pre, 'full')],
                    [('row', D, F32), ('acc', (1, D), F32)])


def _b_attn(dx2, ao, x1, hc, g_post, w_o, w_q, g_pre, k, v, tm, tpb, mlen):
    n = dx2.shape[0]
    nb = n // (tm * tpb)
    kv_spec = pl.BlockSpec((mlen, D), lambda i: (i // tpb, 0))
    scale = HEAD_DIM ** -0.5

    def body(dx2_ref, ao_ref, x1_ref, hc_ref, go_ref, wo_ref, wq_ref, gp_ref, k_ref, v_ref,
             dao_ref, dq_ref, dx1_ref, dk_ref, dv_ref, dgo_ref, dgp_ref):
        i = pl.program_id(0)

        @pl.when(i == 0)
        def _():
            dgo_ref[...] = jnp.zeros_like(dgo_ref)
            dgp_ref[...] = jnp.zeros_like(dgp_ref)

        @pl.when(i % tpb == 0)
        def _():
            dk_ref[...] = jnp.zeros_like(dk_ref)
            dv_ref[...] = jnp.zeros_like(dv_ref)

        dx2v = dx2_ref[...]
        dao, dgo = _rms_bwd(ao_ref[...], go_ref[...], dx2v)
        dgo_ref[...] += dgo
        dao = dao.astype(BF16)
        dao_ref[...] = dao
        do = _dot_nt(dao, wo_ref[...])
        q = _dot(hc_ref[...], wq_ref[...])
        for h in range(HEADS):
            hs = slice(h * HEAD_DIM, (h + 1) * HEAD_DIM)
            qh = q[:, hs].astype(BF16)
            kh = k_ref[:, hs]
            p = _softmax(_dot_nt(qh, kh) * scale)
            doh = do[:, hs].astype(BF16)
            dp = _dot_nt(doh, v_ref[:, hs])
            ds = (p * (dp - jnp.sum(dp * p, axis=-1, keepdims=True)) * scale).astype(BF16)
            dq_ref[:, hs] = _dot(ds, kh).astype(BF16)
            dk_ref[:, hs] += _dot_tn(ds, qh)
            dv_ref[:, hs] += _dot_tn(p.astype(BF16), doh)
        dhc = _dot_nt(dq_ref[...], wq_ref[...])
        dx, dgp = _rms_bwd(x1_ref[...], gp_ref[...], dhc)
        dgp_ref[...] += dgp
        dx1_ref[...] = dx2v + dx

    return _rowcall("b_attn", body, n, tm,
                    [(dx2, 'row'), (ao, 'row'), (x1, 'row'), (hc, 'row'), (g_post, 'full'), (w_o, 'full'),
                     (w_q, 'full'), (g_pre, 'full'), (k, kv_spec), (v, kv_spec)],
                    [('row', D, BF16), ('row', D, BF16), ('row', D, F32),
                     (kv_spec, (nb * mlen, D), F32), (kv_spec, (nb * mlen, D), F32),
                     ('acc', (1, D), F32), ('acc', (1, D), F32)])


def _b_mem(dk, dv, mem, w_kv, g_mem, tm):
    n = mem.shape[0]

    def body(dk_ref, dv_ref, m_ref, w_ref, g_ref, dkv_ref, dg_ref):
        @pl.when(pl.program_id(0) == 0)
        def _():
            dg_ref[...] = jnp.zeros_like(dg_ref)
        dkb = dk_ref[...].astype(BF16)
        dvb = dv_ref[...].astype(BF16)
        dkv_ref[:, :D] = dkb
        dkv_ref[:, D:] = dvb
        dmn = _dot_nt(dkb, w_ref[:, :D]) + _dot_nt(dvb, w_ref[:, D:])
        _, dg = _rms_bwd(m_ref[...], g_ref[...], dmn)
        dg_ref[...] += dg

    return _rowcall("b_mem", body, n, tm, [(dk, 'row'), (dv, 'row'), (mem, 'row'), (w_kv, 'full'), (g_mem, 'full')],
                    [('row', 2 * D, BF16), ('acc', (1, D), F32)])


def _b_mix(dx1, mo, a, b, zg, ypre, g_post, w_mix_out, w_br_gm, w_br_s5, w_glu, tm):
    n = dx1.shape[0]

    def body(dx1_ref, mo_ref, a_ref, b_ref, zg_ref, ypre_ref, g_ref, wout_ref, wgm_ref, ws5_ref, wglu_ref,
             dmo_ref, da_ref, db_ref, dzg_ref, dygm_ref, dgate_ref, dypre_ref, dg_ref):
        @pl.when(pl.program_id(0) == 0)
        def _():
            dg_ref[...] = jnp.zeros_like(dg_ref)
        dmo, dg = _rms_bwd(mo_ref[...], g_ref[...], dx1_ref[...])
        dg_ref[...] += dg
        dmo = dmo.astype(BF16)
        dmo_ref[...] = dmo
        dmg = _dot_nt(dmo, wout_ref[...]).astype(BF16)
        zg = zg_ref[...].astype(BF16)
        sa, sb = _sig(zg[:, :D]), _sig(zg[:, D:])
        da = dmg * sa
        db = dmg * sb
        da_ref[...] = da
        db_ref[...] = db
        dzg_ref[:, :D] = dmg * a_ref[...] * (sa * (1.0 - sa))
        dzg_ref[:, D:] = dmg * b_ref[...] * (sb * (1.0 - sb))
        dygm_ref[...] = _dot_nt(da, wgm_ref[...])
        dys5 = _dot_nt(db, ws5_ref[...])
        ypre = ypre_ref[...]
        yg = _gelu(ypre)
        sgt = _sig(_dot(yg.astype(BF16), wglu_ref[...]))
        dgate = (dys5 * yg * sgt * (1.0 - sgt)).astype(BF16)
        dgate_ref[...] = dgate
        dyg = dys5 * sgt + _dot_nt(dgate, wglu_ref[...])
        dypre_ref[...] = dyg * _gelu_grad(ypre)

    return _rowcall("b_mix", body, n, tm,
                    [(dx1, 'row'), (mo, 'row'), (a, 'row'), (b, 'row'), (zg, 'row'), (ypre, 'row'), (g_post, 'full'),
                     (w_mix_out, 'full'), (w_br_gm, 'full'), (w_br_s5, 'full'), (w_glu, 'full')],
                    [('row', D, BF16), ('row', D, BF16), ('row', D, BF16), ('row', 2 * D, BF16), ('row', D, F32),
                     ('row', S5_W, BF16), ('row', S5_W, F32), ('acc', (1, D), F32)])


def _b_s5(zs5, dypre, states, bbc, ccm, avec, dvec, nb, t, comm):
    n = zs5.shape[0]
    nc = n // nb // t
    seg = t // 8
    slab, bb, cm, av, dv, st = _s5_specs(nb, nc, t, True)

    def body(u_ref, dy_ref, st_ref, bb_ref, cc_ref, a_ref, d_ref,
             du_ref, dbb_ref, dcc_ref, da_ref, dd_ref, lcarry, s_ref, l_ref):
        first = jnp.logical_and(pl.program_id(1) == 0, pl.program_id(2) == 0)

        @pl.when(first)
        def _():
            dbb_ref[...] = jnp.zeros_like(dbb_ref)
            dcc_ref[...] = jnp.zeros_like(dcc_ref)
            da_ref[...] = jnp.zeros_like(da_ref)
            dd_ref[...] = jnp.zeros_like(dd_ref)

        @pl.when(pl.program_id(2) == 0)
        def _():
            lcarry[...] = jnp.zeros_like(lcarry)

        up, dyp = _seg_load(u_ref, seg), _seg_load(dy_ref, seg)
        ub, dyb = up.astype(BF16), dyp.astype(BF16)
        a1r, a1i = a_ref[:, :512], a_ref[:, 512:]
        ar, ai = jnp.broadcast_to(a1r, (8, 512)), jnp.broadcast_to(a1i, (8, 512))
        pr, pi = _seg_power(a1r, a1i, seg)

        s_ref[...] = _dot(ub, bb_ref[...]).reshape(seg, 8, 1024)
        tr, ti = _seg_scan(s_ref, ar, ai, seg, False)
        er, ei, _ = _seg_entries(tr, ti, pr, pi, st_ref[...], False)

        def apply(k, pe):
            qr, qi = pe
            qr, qi = ar * qr - ai * qi, ar * qi + ai * qr
            s_ref[k, :, :512] = s_ref[k, :, :512] + qr
            s_ref[k, :, 512:] = s_ref[k, :, 512:] + qi
            return qr, qi

        lax.fori_loop(0, seg, apply, (er, ei), unroll=4)

        l_ref[...] = _dot_nt(dyb, cc_ref[...]).reshape(seg, 8, 1024)
        tr, ti = _seg_scan(l_ref, ar, -ai, seg, True)
        fr, fi, lout = _seg_entries(tr, ti, pr, -pi, lcarry[...], True)
        lcarry[...] = lout

        def apply_back(i, carry):
            qr, qi, accr, acci = carry
            k = seg - 1 - i
            qr, qi = ar * qr + ai * qi, ar * qi - ai * qr
            lr = l_ref[k, :, :512] + qr
            li = l_ref[k, :, 512:] + qi
            l_ref[k, :, :512] = lr
            l_ref[k, :, 512:] = li
            kp = jnp.maximum(k - 1, 0)
            sr = jnp.where(k == 0, er, s_ref[kp, :, :512])
            si = jnp.where(k == 0, ei, s_ref[kp, :, 512:])
            return qr, qi, accr + lr * sr + li * si, acci + li * sr - lr * si

        zero = jnp.zeros((8, 512), F32)
        _, _, accr, acci = lax.fori_loop(0, seg, apply_back, (fr, fi, zero, zero), unroll=4)
        da_ref[:, :512] += jnp.sum(accr, axis=0, keepdims=True)
        da_ref[:, 512:] += jnp.sum(acci, axis=0, keepdims=True)

        s = s_ref[...].reshape(t, 1024).astype(BF16)
        lam = l_ref[...].reshape(t, 1024).astype(BF16)
        dcc_ref[...] += _dot_tn(dyb, s)
        dbb_ref[...] += _dot_tn(ub, lam)
        _seg_store(du_ref, _dot_nt(lam, bb_ref[...]) + d_ref[...] * dyp, seg)
        dd_ref[...] += jnp.sum(dyp * up, axis=0, keepdims=True)

    return _hosting_call(
        body, "b_s5", (S5_SUPER, nb, nc), [slab, slab, st, bb, cm, av, dv], [slab, bb, bb, av, dv],
        [jax.ShapeDtypeStruct((n, S5_W), F32), jax.ShapeDtypeStruct(bbc.shape, F32),
         jax.ShapeDtypeStruct(bbc.shape, F32), jax.ShapeDtypeStruct(avec.shape, F32),
         jax.ShapeDtypeStruct(dvec.shape, F32)],
        [pltpu.VMEM((1, 1024), F32), pltpu.VMEM((seg, 8, 1024), F32), pltpu.VMEM((seg, 8, 1024), F32)],
        ("arbitrary", "arbitrary", "arbitrary"), (zs5, dypre, states, bbc, ccm, avec, dvec), comm)


def _b_gmlp(zgm, dygm, ln_g, ln_b, w_s, b_s, tm, comm):
    n = zgm.shape[0]

    def body(z_ref, dy_ref, lg_ref, lb_ref, ws_ref, bs_ref, dz_ref, dws_ref, dbs_ref, dlg_ref, dlb_ref,
             du_s, dvn_s):
        @pl.when(pl.program_id(0) == 0)
        def _():
            dws_ref[...] = jnp.zeros_like(dws_ref)
            dbs_ref[...] = jnp.zeros_like(dbs_ref)
            dlg_ref[...] = jnp.zeros_like(dlg_ref)
            dlb_ref[...] = jnp.zeros_like(dlb_ref)
        z = z_ref[...]
        zg = _gelu(z)
        u = zg[:, :D]
        vh, r = _ln_stats(zg[:, D:])
        vn = (vh * lg_ref[...] + lb_ref[...]).astype(BF16)
        dy = dy_ref[...]
        keep = _tril()
        for g in range(GM_GROUPS):
            w = jnp.where(keep, ws_ref[g], 0.0).astype(BF16)
            cs = slice(g * LANE, (g + 1) * LANE)
            for c in range(tm // GM_CHUNK):
                rs = slice(c * GM_CHUNK, (c + 1) * GM_CHUNK)
                vb = vn[rs, cs]
                sv = _dot(w, vb) + bs_ref[g]
                dyb = dy[rs, cs]
                du_s[rs, cs] = dyb * sv
                dsv = dyb * u[rs, cs]
                dsvb = dsv.astype(BF16)
                dvn_s[rs, cs] = _dot_tn(w, dsvb)
                dws_ref[g] += jnp.where(keep, _dot_nt(dsvb, vb), 0.0)
                dbs_ref[g] += jnp.sum(dsv, axis=1, keepdims=True)
        dvn = dvn_s[...]
        dlg_ref[...] += jnp.sum(dvn * vh, axis=0, keepdims=True)
        dlb_ref[...] += jnp.sum(dvn, axis=0, keepdims=True)
        dvh = dvn * lg_ref[...]
        dv = r * (dvh - jnp.mean(dvh, axis=-1, keepdims=True) - vh * jnp.mean(dvh * vh, axis=-1, keepdims=True))
        dz_ref[:, :D] = (du_s[...] * _gelu_grad(z[:, :D])).astype(BF16)
        dz_ref[:, D:] = (dv * _gelu_grad(z[:, D:])).astype(BF16)

    return _rowcall("b_gmlp", body, n, tm,
                    [(zgm, 'row'), (dygm, 'row'), (ln_g, 'full'), (ln_b, 'full'), (w_s, 'full'), (b_s, 'full')],
                    [('row', 2 * D, BF16), ('acc', w_s.shape, F32), ('acc', b_s.shape, F32), ('acc', (1, D), F32),
                     ('acc', (1, D), F32)],
                    scratch=[pltpu.VMEM((tm, D), F32), pltpu.VMEM((tm, D), F32)], comm=comm)


def _b_in(dzgm, dzs5, dzg, dx1, x, w_in, g_pre, tm, comm):
    n = x.shape[0]

    def body(d1_ref, d2_ref, d3_ref, dx1_ref, x_ref, w_ref, g_ref, gx_ref, dg_ref):
        @pl.when(pl.program_id(0) == 0)
        def _():
            dg_ref[...] = jnp.zeros_like(dg_ref)
        dh = (_dot_nt(d1_ref[...], w_ref[:, 0:2 * D]) + _dot_nt(d2_ref[...].astype(BF16), w_ref[:, 2 * D:2 * D + S5_W])
              + _dot_nt(d3_ref[...], w_ref[:, 2 * D + S5_W:]))
        dx, dg = _rms_bwd(x_ref[...], g_ref[...], dh)
        dg_ref[...] += dg
        gx_ref[...] = dx1_ref[...] + dx

    return _rowcall("b_in", body, n, tm,
                    [(dzgm, 'row'), (dzs5, 'row'), (dzg, 'row'), (dx1, 'row'), (x, 'row'), (w_in, 'full'),
                     (g_pre, 'full')],
                    [('row', D, F32), ('acc', (1, D), F32)], comm=comm)


def _whole(name, body, ins, outs):
    return pl.pallas_call(body, name=name, out_shape=[jax.ShapeDtypeStruct(s, dt) for s, dt in outs],
                          compiler_params=_params())(*ins)


def _s5_disc_fwd(lr, li, ls, br, bi):
    def body(lr_ref, li_ref, ls_ref, br_ref, bi_ref, o1, o2, o3, o4):
        outs = _s5_disc(lr_ref[...], li_ref[...], ls_ref[...], br_ref[...], bi_ref[...])
        for o, val in zip((o1, o2, o3, o4), outs):
            o[...] = val

    return _whole("s5_disc_fwd", body, [lr, li, ls, br, bi],
                  [(lr.shape, F32), (lr.shape, F32), (br.shape, F32), (br.shape, F32)])


def _s5_disc_bwd(lr, li, ls, br, bi, cts):
    def body(lr_ref, li_ref, ls_ref, br_ref, bi_ref, c1, c2, c3, c4, o1, o2, o3, o4, o5):
        _, vjp = jax.vjp(_s5_disc, lr_ref[...], li_ref[...], ls_ref[...], br_ref[...], bi_ref[...])
        grads = vjp((c1[...], c2[...], c3[...], c4[...]))
        for o, val in zip((o1, o2, o3, o4, o5), grads):
            o[...] = val

    return _whole("s5_disc_bwd", body, [lr, li, ls, br, bi, *cts],
                  [(lr.shape, F32), (lr.shape, F32), (lr.shape, F32), (br.shape, F32), (br.shape, F32)])


def _sum_slots(name, slots):
    def body(s_ref, o_ref):
        acc = s_ref[0]
        for k in range(1, slots.shape[0]):
            acc = acc + s_ref[k]
        o_ref[...] = acc

    return _whole(name, body, [slots], [(slots.shape[1:], F32)])[0]


def _adamw_math(w, g, m, v):
    c1 = 1.0 - ADAM_B1 ** ADAM_STEP
    c2 = 1.0 - ADAM_B2 ** ADAM_STEP
    mn = ADAM_B1 * m + (1.0 - ADAM_B1) * g
    vn = ADAM_B2 * v + (1.0 - ADAM_B2) * (g * g)
    return -ADAM_LR * ((mn / c1) / (jnp.sqrt(vn / c2) + ADAM_EPS) + ADAM_WD * w), mn, vn


def _adamw_many(name, ws, gs, ms, vs):
    k = len(ws)

    def body(*refs):
        for i in range(k):
            dl, mn, vn = _adamw_math(refs[i][...], refs[k + i][...], refs[2 * k + i][...], refs[3 * k + i][...])
            refs[4 * k + i][...] = dl
            refs[5 * k + i][...] = mn
            refs[6 * k + i][...] = vn

    res = _whole(name, body, [*ws, *gs, *ms, *vs], [(t.shape, F32) for t in ws] * 3)
    return res[:k], res[k:2 * k], res[2 * k:]


def _adamw(name, w, g, m, v, tm):
    def body(w_ref, g_ref, m_ref, v_ref, d_ref, mo_ref, vo_ref):
        d_ref[...], mo_ref[...], vo_ref[...] = _adamw_math(w_ref[...], g_ref[...], m_ref[...], v_ref[...])

    cols = w.shape[1]
    return _rowcall(name, body, w.shape[0], tm, [(w, 'row'), (g, 'row'), (m, 'row'), (v, 'row')],
                    [('row', cols, F32), ('row', cols, F32), ('row', cols, F32)])


def _picked_rowcall(name, body, grid, in_specs, out_cols, tm):
    return pl.pallas_call(
        lambda p_ref, *refs: body(*refs), name=name,
        grid_spec=pltpu.PrefetchScalarGridSpec(
            num_scalar_prefetch=1, grid=(grid,), in_specs=in_specs,
            out_specs=[pl.BlockSpec((tm, D), lambda i, p: (i, 0)) for _ in out_cols]),
        out_shape=[jax.ShapeDtypeStruct((grid * tm, D), dt) for dt in out_cols],
        compiler_params=_params(("arbitrary",)),
    )


def _exchange(copies):
    def build(src_refs, out_refs, ss, rs):
        def rdma(k, src, dst, peer):
            return _Lazy(lambda: pltpu.make_async_remote_copy(
                src_ref=src, dst_ref=dst, send_sem=ss.at[k], recv_sem=rs.at[k], device_id=peer, device_id_type=MESH))
        return copies(src_refs, out_refs, rdma)

    def start(*refs):
        for cp in build(*refs):
            cp.start()

    def finish(*refs):
        for cp in build(*refs):
            cp.wait()

    return start, finish


def _reduce_big(tag, slabs):
    rows = [s.shape[1] for s in slabs]
    blk = sum(rows) // 4
    tm = _tile_rows(blk, 512)
    per = blk // tm
    gpack = jnp.concatenate([s.reshape(4, 2, 2, r // 4, D) for s, r in zip(slabs, rows)], axis=3)

    mx, my, mc = lax.axis_index("x"), lax.axis_index("y"), lax.axis_index("c")
    chip_me, chip_xn, chip_yn = 2 * mx + my, 2 * (1 - mx) + my, 2 * mx + (1 - my)

    def pair_copies(srcs, dsts, rdma):
        c, _, (_, _, psib) = _place()
        return [rdma(0, srcs[0].at[:, 1 - c], dsts[0], psib)]

    from_sib = (yield _Comm([gpack], [((4, 2, blk, D), F32)], 1, _exchange(pair_copies), {}, (0.0, 1.0)))[0]

    def sum_both(a_ref, b_ref, o32_ref, o16_ref):
        s = a_ref[...].astype(F32) + b_ref[...].astype(F32)
        o32_ref[...] = s
        o16_ref[...] = s.astype(BF16)

    p32, p16 = _picked_rowcall(
        "sum_pair_" + tag, sum_both, 8 * per,
        [pl.BlockSpec((None, None, None, tm, D), lambda i, p: (i // (2 * per), p[0], (i // per) % 2, i % per, 0)),
         pl.BlockSpec((tm, D), lambda i, p: (i, 0))],
        [F32, BF16], tm)(jnp.stack([mc]).astype(jnp.int32), gpack, from_sib.reshape(8 * blk, D))
    p32, p16 = p32.reshape(4, 2, blk, D), p16.reshape(4, 2, blk, D)

    def step1_copies(srcs, dsts, rdma):
        _, (me, xn, yn, dg), (px, py, _) = _place()
        s16 = srcs[0]
        return [rdma(0, s16.at[yn, 0], dsts[0].at[0], py), rdma(1, s16.at[dg, 0], dsts[0].at[1], py),
                rdma(2, s16.at[xn, 1], dsts[0].at[2], px), rdma(3, s16.at[dg, 1], dsts[0].at[3], px)]

    recv1 = (yield _Comm([p16], [((4, blk, D), BF16)], 4, _exchange(step1_copies), {}, (0.0, 1.0)))[0]

    s32, s16 = _picked_rowcall(
        "sum_step1_" + tag, sum_both, 4 * per,
        [pl.BlockSpec((None, None, tm, D), lambda i, p: (p[i // per], i // (2 * per), i % per, 0)),
         pl.BlockSpec((tm, D), lambda i, p: (i, 0))],
        [F32, BF16], tm)(jnp.stack([chip_me, chip_xn, chip_me, chip_yn]).astype(jnp.int32), p32,
                         recv1.reshape(4 * blk, D))

    def step2_copies(srcs, dsts, rdma):
        _, _, (px, py, _) = _place()
        return [rdma(0, srcs[0].at[1], dsts[0].at[0], px), rdma(1, srcs[0].at[3], dsts[0].at[1], py)]

    recv2 = (yield _Comm([s16.reshape(4, blk, D)], [((2, blk, D), BF16)], 2, _exchange(step2_copies), {},
                         (0.0, 1.0)))[0]

    def sum_step2(a_ref, b_ref, o_ref):
        o_ref[...] = a_ref[...] + b_ref[...].astype(F32)

    red = _rowcall("sum_step2_" + tag, sum_step2, 2 * blk, tm,
                   [(s32.reshape(4, blk, D), pl.BlockSpec((None, tm, D), lambda i: (2 * (i // per), i % per, 0))),
                    (recv2.reshape(2 * blk, D), 'row')],
                   [('row', D, F32)])[0].reshape(2, blk, D)

    def share_copies(srcs, dsts, rdma):
        _, _, (_, _, psib) = _place()
        return [rdma(0, srcs[0], dsts[0], psib)]

    other = (yield _Comm([red], [((2, blk, D), F32)], 1, _exchange(share_copies), {}, (0.0, 1.0)))[0]
    lo = jnp.where(mc == 0, red, other)
    hi = jnp.where(mc == 0, other, red)
    out, off = [], 0
    for r in rows:
        q = r // 4
        out.append(jnp.concatenate([lo[0, off:off + q], lo[1, off:off + q], hi[0, off:off + q], hi[1, off:off + q]]))
        off += q
    return out


def _rows1024(a):
    flat = a.reshape(-1)
    pad = (-flat.shape[0]) % D
    if pad:
        flat = jnp.concatenate([flat, jnp.zeros((pad,), flat.dtype)])
    return flat.reshape(-1, D)


def _pack(arrs, pad_rows_to=8):
    parts = [_rows1024(a) for a in arrs]
    rows = sum(p.shape[0] for p in parts)
    pad = (-rows) % pad_rows_to
    if pad:
        parts.append(jnp.zeros((pad, D), parts[0].dtype))
    return jnp.concatenate(parts, axis=0)


def _unpack(packed, shapes):
    out, r = [], 0
    for s in shapes:
        size = math.prod(s)
        nr = -(-size // D)
        out.append(packed[r:r + nr].reshape(-1)[:size].reshape(s))
        r += nr
    return out


def _shard_slabs(full, axis):
    r, c = full.shape
    if axis == 0:
        return full.reshape(4, -1, D)
    return full.reshape(r, 4, c // 4).transpose(1, 0, 2).reshape(4, -1, D)


def _block_diag(t):
    eye = jnp.eye(8, dtype=t.dtype)
    j, g, a, b = t.shape
    return (t[:, :, :, None, :] * eye[None, :, None, :, None]).reshape(j, g * a, g * b)


def _block_diag_take(m, a, b):
    eye = jnp.eye(8, dtype=m.dtype)
    return (m.reshape(4, 8, a, 8, b) * eye[None, :, None, :, None]).sum(axis=3)


def kernel(x, mem, g_mix_pre, w_in, gm_ln_g, gm_ln_b, gm_w_s, gm_b_s, s5_lam_re, s5_lam_im, s5_log_step, s5_b_re, s5_b_im, s5_c_re, s5_c_im, s5_d, s5_w_glu, w_br_gm, w_br_s5, w_mix_out, g_mix_post, g_ca_pre, g_mem, ca_w_q, ca_w_kv, ca_w_o, g_ca_post, g_ffn_pre, ffn_w_gu, ffn_w_down, g_ffn_post, loss_target, m_g_mix_pre, m_w_in, m_gm_ln_g, m_gm_ln_b, m_gm_w_s, m_gm_b_s, m_s5_lam_re, m_s5_lam_im, m_s5_log_step, m_s5_b_re, m_s5_b_im, m_s5_c_re, m_s5_c_im, m_s5_d, m_s5_w_glu, m_w_br_gm, m_w_br_s5, m_w_mix_out, m_g_mix_post, m_g_ca_pre, m_g_mem, m_ca_w_q, m_ca_w_kv, m_ca_w_o, m_g_ca_post, m_g_ffn_pre, m_ffn_w_gu, m_ffn_w_down, m_g_ffn_post, v_g_mix_pre, v_w_in, v_gm_ln_g, v_gm_ln_b, v_gm_w_s, v_gm_b_s, v_s5_lam_re, v_s5_lam_im, v_s5_log_step, v_s5_b_re, v_s5_b_im, v_s5_c_re, v_s5_c_im, v_s5_d, v_s5_w_glu, v_w_br_gm, v_w_br_s5, v_w_mix_out, v_g_mix_post, v_g_ca_pre, v_g_mem, v_ca_w_q, v_ca_w_kv, v_ca_w_o, v_g_ca_post, v_g_ffn_pre, v_ffn_w_gu, v_ffn_w_down, v_g_ffn_post):
    a = dict(locals())
    w = {n: a[n][0] for n in WNAMES}
    nb, seq, _ = x.shape
    n = nb * seq
    mlen = mem.shape[1]
    tm = min(256, seq)
    tmb = min(512, seq)
    tpb = seq // tmb
    xf = x.reshape(n, D)
    tgt = loss_target.reshape(n, D)
    memf = mem.reshape(nb * mlen, D)
    tmm = min(256, mlen)

    big_names = list(BIG)
    local_shapes = {k: w[k].shape for k in big_names}
    w_in_full = _run_comm("gather_w_in", _gather_weights("cast_w_in", [w['w_in']], [BIG['w_in']]))[0]
    rest = [k for k in big_names if k != 'w_in']
    gather_rest = _gather_weights("cast_w_rest", [w[k] for k in rest], [BIG[k] for k in rest])

    vec = lambda name: w[name].reshape(1, D)

    to_lane = lambda p: p.reshape(1, -1)
    b_t = lambda p: p.transpose(2, 0, 1).reshape(16, -1)
    lr_l, li_l = to_lane(w['s5_lam_re']), to_lane(w['s5_lam_im'])
    ls_l = jnp.repeat(w['s5_log_step'], 64).reshape(1, -1)
    br_t, bi_t = b_t(w['s5_b_re']), b_t(w['s5_b_im'])
    ab_re, ab_im, bb_re, bb_im = _s5_disc_fwd(lr_l, li_l, ls_l, br_t, bi_t)
    blk = lambda t: _block_diag(t.reshape(16, 4, 8, 64).transpose(1, 2, 0, 3))
    bbc = jnp.concatenate([blk(bb_re), blk(bb_im)], axis=2).astype(BF16)
    cblk = lambda c: _block_diag(c.reshape(4, 8, 16, 64).transpose(0, 1, 3, 2))
    ccm = jnp.concatenate([cblk(w['s5_c_re']), -cblk(w['s5_c_im'])], axis=1).astype(BF16)
    avec = jnp.concatenate([ab_re.reshape(4, 1, 512), ab_im.reshape(4, 1, 512)], axis=2)
    dvec = w['s5_d'].reshape(4, 1, LANE)

    bs3 = w['gm_b_s'].reshape(GM_GROUPS, GM_CHUNK, 1)
    wf = {'w_in': w_in_full}
    h, zgm, zs5, zg = _f_in(xf, vec('g_mix_pre'), wf['w_in'], tmb)
    ygm = _f_gmlp(zgm, vec('gm_ln_g'), vec('gm_ln_b'), w['gm_w_s'], bs3, tmb)
    s5_t = min(S5_T, seq)
    (ypre, states), gathered = _f_s5(zs5, bbc, ccm, avec, dvec, nb, s5_t, gather_rest)
    for k, g in zip(rest, gathered):
        wf[k] = g.reshape(-1, g.shape[-1]) if BIG[k] == 0 else g
    yg, ys5, a_br, b_br, merged, mo, x1 = _f_mix(ygm, ypre, zg, xf, wf['s5_w_glu'], wf['w_br_gm'], wf['w_br_s5'],
                                                 wf['w_mix_out'], vec('g_mix_post'), tm)
    mem_n, kk, vv = _f_mem(memf, vec('g_mem'), wf['ca_w_kv'], tmm)
    hc, o_att, ao, x2 = _f_attn(x1, vec('g_ca_pre'), wf['ca_w_q'], kk, vv, wf['ca_w_o'], vec('g_ca_post'),
                                tmb, tpb, mlen)
    hf, gu, act = _f_ffn_up(x2, vec('g_ffn_pre'), wf['ffn_w_gu'], tm)
    dn, d3, loss_cols = _f_ffn_down(act, x2, tgt, wf['ffn_w_down'], vec('g_ffn_post'), tmb)

    gsm = {}
    gbig = {}
    ddn, dgu, gsm['g_ffn_post'] = _b_ffn_down(d3, dn, gu, vec('g_ffn_post'), wf['ffn_w_down'], tm)
    dx2, gsm['g_ffn_pre'] = _b_ffn_up(dgu, d3, x2, wf['ffn_w_gu'], vec('g_ffn_pre'), tmb)
    gbig['ffn_w_down'] = _mm_tn("dw_ffn_down", act, ddn)
    gbig['ffn_w_gu'] = _mm_tn("dw_ffn_gu", dgu, hf)
    dao, dq, dx1, dk, dv, gsm['g_ca_post'], gsm['g_ca_pre'] = _b_attn(
        dx2, ao, x1, hc, vec('g_ca_post'), wf['ca_w_o'], wf['ca_w_q'], vec('g_ca_pre'), kk, vv, tmb, tpb, mlen)
    gbig['ca_w_o'] = _mm_tn("dw_ca_o", o_att, dao)
    gbig['ca_w_q'] = _mm_tn("dw_ca_q", hc, dq)
    dkv, gsm['g_mem'] = _b_mem(dk, dv, memf, wf['ca_w_kv'], vec('g_mem'), tmm)
    gbig['ca_w_kv'] = _mm_tn("dw_ca_kv", dkv, mem_n)
    dmo, da_br, db_br, dzg, dygm, dgate, dypre, gsm['g_mix_post'] = _b_mix(
        dx1, mo, a_br, b_br, zg, ypre, vec('g_mix_post'), wf['w_mix_out'], wf['w_br_gm'], wf['w_br_s5'],
        wf['s5_w_glu'], tm)
    gbig['w_mix_out'] = _mm_tn("dw_mix_out", merged, dmo)
    gbig['w_br_gm'] = _mm_tn("dw_br_gm", ygm, da_br)
    gbig['w_br_s5'] = _mm_tn("dw_br_s5", db_br, ys5)
    gbig['s5_w_glu'] = _mm_tn("dw_s5_glu", yg, dgate)
    slab_of = lambda k: gbig[k].reshape(4, -1, D)
    red_rest = _reduce_big("rest", [slab_of(k) for k in rest])
    (dzs5, dbbc, dccm_t, davec, ddvec), got = _b_s5(zs5, dypre, states, bbc, ccm, avec, dvec, nb, s5_t,
                                                    next(red_rest))
    dccm = dccm_t.transpose(0, 2, 1)
    (dzgm, gsm['gm_w_s'], dbs3, gsm['gm_ln_g'], gsm['gm_ln_b']), got = _b_gmlp(
        zgm, dygm, vec('gm_ln_g'), vec('gm_ln_b'), w['gm_w_s'], bs3, tmb, red_rest.send(got))
    gsm['gm_b_s'] = dbs3
    grad_x, gsm['g_mix_pre'] = _b_in(dzgm, dzs5, dzg, dx1, xf, wf['w_in'], vec('g_mix_pre'), tmb, None)
    dw_in_gm, got = _mm_tn("dw_in_gm", dzgm, h, comm=red_rest.send(got))
    dw_in_s5, got = _mm_tn("dw_in_s5", dzs5, h, comm=red_rest.send(got))
    greds = dict(zip(rest, _finish(red_rest, got)))

    unblk = lambda m_: _block_diag_take(m_, 16, 64).transpose(2, 0, 1, 3).reshape(16, -1)
    d_bb_re, d_bb_im = unblk(dbbc[:, :, :512]), unblk(dbbc[:, :, 512:])
    cunblk = lambda m_: _block_diag_take(m_, 64, 16).transpose(0, 1, 3, 2).reshape(32, 16, 64)
    gsm['s5_c_re'] = cunblk(dccm[:, :512, :])
    gsm['s5_c_im'] = -cunblk(dccm[:, 512:, :])
    d_ab_re, d_ab_im = davec[:, :, :512].reshape(1, -1), davec[:, :, 512:].reshape(1, -1)
    g_lr, g_li, g_ls, g_br, g_bi = _s5_disc_bwd(lr_l, li_l, ls_l, br_t, bi_t, (d_ab_re, d_ab_im, d_bb_re, d_bb_im))
    gsm['s5_lam_re'], gsm['s5_lam_im'] = g_lr.reshape(32, 64), g_li.reshape(32, 64)
    gsm['s5_log_step'] = g_ls.reshape(32, 64).sum(axis=1)
    from_t = lambda t: t.reshape(16, 32, 64).transpose(1, 2, 0)
    gsm['s5_b_re'], gsm['s5_b_im'] = from_t(g_br), from_t(g_bi)
    gsm['s5_d'] = ddvec.reshape(32, 16)

    small_shapes = [w[k].shape for k in SMALL]
    spack = _pack([gsm[k].reshape(w[k].shape) for k in SMALL] + [loss_cols], 8)
    rs = spack.shape[0]
    me_slot = 4 * lax.axis_index("x") + 2 * lax.axis_index("y") + lax.axis_index("c")
    dw_in_g, (sall,) = _mm_tn("dw_in_g", dzg, h, comm=_gather_all(spack))
    gbig['w_in'] = jnp.concatenate([dw_in_gm, dw_in_s5, dw_in_g], axis=0)
    sall = lax.dynamic_update_index_in_dim(sall, spack, me_slot, 0)
    ssum = _sum_slots("sum_small", sall)
    small_red = _unpack(ssum, small_shapes + [(1, D)])
    loss = 0.5 * jnp.sum(small_red[-1]) / D

    red_in = _reduce_big("w_in", [slab_of('w_in')])
    got = _run_comm("reduce_pair_w_in", next(red_in))
    for phase in ("reduce_step1_w_in", "reduce_step2_w_in"):
        got = _run_comm(phase, red_in.send(got))
    got = _run_comm("share_pair_w_in", red_in.send(got))
    greds['w_in'] = _finish(red_in, got)[0]

    res_big = {}
    for k in big_names:
        r_, c_ = local_shapes[k]
        g = greds[k].reshape(r_, c_) if BIG[k] == 0 else greds[k].reshape(c_, r_).T
        dl, mn, vn = _adamw("adamw_" + k, w[k], g, a['m_' + k][0], a['v_' + k][0], _tile_rows(g.shape[0], 256))
        res_big[k] = (g, dl, mn, vn)
    as2d = lambda t: t.reshape(1, -1) if t.ndim == 1 else t
    dl_s, mn_s, vn_s = _adamw_many("adamw_small", [as2d(w[k]) for k in SMALL], [as2d(g) for g in small_red[:-1]],
                                   [as2d(a['m_' + k][0]) for k in SMALL], [as2d(a['v_' + k][0]) for k in SMALL])
    res_small = {k: (g, dl.reshape(w[k].shape), mn.reshape(w[k].shape), vn.reshape(w[k].shape))
                 for k, g, dl, mn, vn in zip(SMALL, small_red[:-1], dl_s, mn_s, vn_s)}

    res = {**res_big, **res_small}
    outs = [loss, grad_x.reshape(nb, seq, D)]
    for i in range(4):
        outs += [res[k][i][None] for k in WNAMES]
    return tuple(outs)


def _finish(gen, got):
    try:
        gen.send(got)
    except StopIteration as done:
        return done.value
    raise AssertionError("the generator has more exchanges")


def _tile_rows(rows, cap=384):
    best = rows
    for t in range(16, min(rows, cap) + 1, 16):
        if rows % t == 0:
            best = t
    return best
```

```python
import math
from typing import Any, NamedTuple

import jax
import jax.numpy as jnp
from jax import lax
from jax.experimental import pallas as pl
from jax.experimental.pallas import tpu as pltpu

F32 = jnp.float32
BF16 = jnp.bfloat16
EPS = 1e-6
D = 1024
GM_CHUNK = 128
GM_GROUPS = 8
S5_W = 512
S5_SUPER = 4
S5_T = 512
HEADS = 4
HEAD_DIM = 256
FFN_H = 2816
LANE = 128
VMEM_LIMIT = 56 * 1024 * 1024
MESH = pl.DeviceIdType.MESH

ADAM_LR, ADAM_B1, ADAM_B2, ADAM_EPS, ADAM_WD, ADAM_STEP = 0.001, 0.9, 0.999, 1e-08, 0.01, 10

WNAMES = ['g_mix_pre', 'w_in', 'gm_ln_g', 'gm_ln_b', 'gm_w_s', 'gm_b_s', 's5_lam_re', 's5_lam_im', 's5_log_step',
          's5_b_re', 's5_b_im', 's5_c_re', 's5_c_im', 's5_d', 's5_w_glu', 'w_br_gm', 'w_br_s5', 'w_mix_out',
          'g_mix_post', 'g_ca_pre', 'g_mem', 'ca_w_q', 'ca_w_kv', 'ca_w_o', 'g_ca_post', 'g_ffn_pre', 'ffn_w_gu',
          'ffn_w_down', 'g_ffn_post']
BIG = {'w_in': 1, 's5_w_glu': 0, 'w_br_gm': 0, 'w_br_s5': 1, 'w_mix_out': 0, 'ca_w_q': 0, 'ca_w_kv': 1,
       'ca_w_o': 0, 'ffn_w_gu': 1, 'ffn_w_down': 0}
SMALL = [n for n in WNAMES if n not in BIG]


def _dot(a, b):
    return jnp.dot(a, b, preferred_element_type=F32)


def _dot_nt(a, b):
    return lax.dot_general(a, b, (((1,), (1,)), ((), ())), preferred_element_type=F32)


def _dot_tn(a, b):
    return lax.dot_general(a, b, (((0,), (0,)), ((), ())), preferred_element_type=F32)


def _rms_fwd(x, g):
    r = lax.rsqrt(jnp.mean(x * x, axis=-1, keepdims=True) + EPS)
    return x * r * g


def _rms_bwd(x, g, dy):
    r = lax.rsqrt(jnp.mean(x * x, axis=-1, keepdims=True) + EPS)
    xh = x * r
    gdy = dy * g
    dx = r * (gdy - xh * jnp.mean(gdy * xh, axis=-1, keepdims=True))
    dg = jnp.sum(dy * xh, axis=0, keepdims=True)
    return dx, dg


_GC = math.sqrt(2.0 / math.pi)


def _gelu(x):
    return 0.5 * x * (1.0 + jnp.tanh(_GC * (x + 0.044715 * x * x * x)))


def _gelu_grad(x):
    t = jnp.tanh(_GC * (x + 0.044715 * x * x * x))
    return 0.5 * (1.0 + t) + 0.5 * x * (1.0 - t * t) * _GC * (1.0 + 3 * 0.044715 * x * x)


def _sig(x):
    return 1.0 / (1.0 + jnp.exp(-x))


def _cscan(br, bi, ar, ai, reverse):
    t = br.shape[0]
    row = lax.broadcasted_iota(jnp.int32, br.shape, 0)
    pr, pi = ar, ai
    sh = 1
    while sh < t:
        if reverse:
            keep = row < t - sh
            rr, ri = pltpu.roll(br, t - sh, 0), pltpu.roll(bi, t - sh, 0)
        else:
            keep = row >= sh
            rr, ri = pltpu.roll(br, sh, 0), pltpu.roll(bi, sh, 0)
        rr = jnp.where(keep, rr, 0.0)
        ri = jnp.where(keep, ri, 0.0)
        br, bi = br + pr * rr - pi * ri, bi + pr * ri + pi * rr
        pr, pi = pr * pr - pi * pi, 2.0 * pr * pi
        sh *= 2
    return br, bi


def _s5_disc(lr, li, ls, br, bi):
    step = jnp.exp(ls)
    mag = jnp.exp(lr * step)
    ab_re = mag * jnp.cos(li * step)
    ab_im = mag * jnp.sin(li * step)
    den = lr * lr + li * li
    nr = ab_re - 1.0
    co_re = (nr * lr + ab_im * li) / den
    co_im = (ab_im * lr - nr * li) / den
    return ab_re, ab_im, co_re * br - co_im * bi, co_re * bi + co_im * br


def _params(sem=None):
    return pltpu.CompilerParams(dimension_semantics=sem, vmem_limit_bytes=VMEM_LIMIT)


def _rowcall(name, body, n_rows, tm, ins, outs, scratch=(), comm=None):
    def spec(kind, shape):
        if kind == 'row':
            return pl.BlockSpec((tm, shape[1]), lambda i: (i, 0))
        if kind == 'full':
            nd = len(shape)
            return pl.BlockSpec(tuple(shape), lambda i: (0,) * nd, pipeline_mode=pl.Buffered(1))
        if kind == 'acc':
            nd = len(shape)
            return pl.BlockSpec(tuple(shape), lambda i: (0,) * nd)
        return kind

    in_specs = [spec(k, a.shape) for a, k in ins]
    out_shape, out_specs = [], []
    for k, s, dt in outs:
        shape = (n_rows, s) if k == 'row' else tuple(s)
        out_shape.append(jax.ShapeDtypeStruct(shape, dt))
        out_specs.append(spec(k, shape))
    args = [a for a, _ in ins]
    if comm is not None:
        return _hosting_call(body, name, (n_rows // tm,), in_specs, out_specs, out_shape, list(scratch),
                             ("arbitrary",), args, comm)
    return pl.pallas_call(
        body, name=name, grid=(n_rows // tm,), in_specs=in_specs, out_specs=out_specs, out_shape=out_shape,
        scratch_shapes=list(scratch), compiler_params=_params(("arbitrary",)),
    )(*args)


def _tile(n, cap):
    if n <= cap:
        return n
    best = LANE
    for k in range(1, n // LANE + 1):
        t = k * LANE
        if n % t == 0 and t <= cap:
            best = t
    return best


def _mm_tn(name, a, b, col_shards=1, comm=None):
    n, k = a.shape
    m = b.shape[1]
    mloc = m // col_shards
    tk, tn, tr = _tile(k, 1536), _tile(mloc, 1536), min(n, 1024)
    per = mloc // tn

    def body(a_ref, b_ref, o_ref):
        @pl.when(pl.program_id(2) == 0)
        def _():
            o_ref[...] = jnp.zeros_like(o_ref)
        o_ref[...] += _dot_tn(a_ref[...].astype(BF16), b_ref[...].astype(BF16))

    if col_shards == 1:
        out_spec = pl.BlockSpec((tk, tn), lambda i, j, r: (i, j))
        out_shape = (k, m)
    else:
        out_spec = pl.BlockSpec((None, tk, tn), lambda i, j, r: (j // per, i, j % per))
        out_shape = (col_shards, k, mloc)
    grid = (k // tk, m // tn, n // tr)
    in_specs = [pl.BlockSpec((tr, tk), lambda i, j, r: (r, i)), pl.BlockSpec((tr, tn), lambda i, j, r: (r, j))]
    if comm is not None:
        (res,), extra = _hosting_call(body, name, grid, in_specs, [out_spec], [jax.ShapeDtypeStruct(out_shape, F32)],
                                      [], ("arbitrary", "arbitrary", "arbitrary"), (a, b), comm)
        return res, extra
    return pl.pallas_call(
        body, name=name, grid=grid, in_specs=in_specs, out_specs=out_spec,
        out_shape=jax.ShapeDtypeStruct(out_shape, F32),
        compiler_params=_params(("parallel", "parallel", "arbitrary")),
    )(a, b)


def _comm_call(name, srcs, outs, nsem, body_fn, aliases=None):
    ns, no = len(srcs), len(outs)

    def body(*refs):
        body_fn(refs[:ns], refs[ns:ns + no], refs[ns + no], refs[ns + no + 1])

    hbm = pl.BlockSpec(memory_space=pltpu.HBM)
    return pl.pallas_call(
        body, name=name, in_specs=[hbm] * ns, out_specs=[hbm] * no,
        out_shape=[jax.ShapeDtypeStruct(s, d) for s, d in outs],
        scratch_shapes=[pltpu.SemaphoreType.DMA((nsem,)), pltpu.SemaphoreType.DMA((nsem,))],
        input_output_aliases=aliases or {},
    )(*srcs)


def _place():
    x, y, c = lax.axis_index("x"), lax.axis_index("y"), lax.axis_index("c")
    chips = (2 * x + y, 2 * (1 - x) + y, 2 * x + (1 - y), 2 * (1 - x) + (1 - y))
    peers = ((1 - x, y, c), (x, 1 - y, c), (x, y, 1 - c))
    return c, chips, peers


class _Comm(NamedTuple):
    srcs: Any
    outs: Any
    nsem: int
    stages: Any
    aliases: Any
    fracs: Any


class _Lazy:
    def __init__(self, make):
        self._make = make

    def start(self):
        self._make().start()

    def wait(self):
        self._make().wait()

    def wait_recv(self):
        self._make().wait_recv()

    def wait_send(self):
        self._make().wait_send()


def _run_comm(name, comm):
    def body_fn(srcs, dsts, ss, rs):
        for stage in comm.stages:
            stage(srcs, dsts, ss, rs)

    return _comm_call(name, comm.srcs, comm.outs, comm.nsem, body_fn, aliases=comm.aliases)


def _hosting_call(body, name, grid, in_specs, out_specs, out_shape, scratch, sem, args, comm):
    hbm = pl.BlockSpec(memory_space=pltpu.HBM)
    n_in, n_out, n_sc = len(in_specs), len(out_specs), len(scratch)
    ns, no = len(comm.srcs), len(comm.outs)
    total = math.prod(grid)
    steps = [min(total - 1, int(f * total)) for f in comm.fracs]

    def wrapped(*refs):
        ins, csrc = refs[:n_in], refs[n_in:n_in + ns]
        outs, cdst = refs[n_in + ns:n_in + ns + n_out], refs[n_in + ns + n_out:n_in + ns + n_out + no]
        rest = refs[n_in + ns + n_out + no:]
        lin = 0
        for d, size in enumerate(grid):
            lin = lin * size + pl.program_id(d)
        for stage, at in zip(comm.stages, steps):
            @pl.when(lin == at)
            def _(stage=stage):
                stage(csrc, cdst, rest[n_sc], rest[n_sc + 1])
        body(*ins, *outs, *rest[:n_sc])

    res = pl.pallas_call(
        wrapped, name=name, grid=grid, in_specs=list(in_specs) + [hbm] * ns, out_specs=list(out_specs) + [hbm] * no,
        out_shape=list(out_shape) + [jax.ShapeDtypeStruct(s, d) for s, d in comm.outs],
        scratch_shapes=list(scratch) + [pltpu.SemaphoreType.DMA((comm.nsem,)), pltpu.SemaphoreType.DMA((comm.nsem,))],
        input_output_aliases={n_in + i: n_out + o for i, o in comm.aliases.items()},
        compiler_params=_params(sem),
    )(*args, *comm.srcs)
    return res[:n_out], res[n_out:]


def _gather_weights(name, shards, axes):
    nw = len(shards)
    shapes = [s.shape for s in shards]
    outs = [((4,) + s if ax == 0 else (s[0], 4 * s[1]), BF16) for s, ax in zip(shapes, axes)]

    def win(ref, i, chip, start, rows):
        r, cols = shapes[i]
        if axes[i] == 0:
            return ref.at[chip, pl.ds(start, rows), :]
        return ref.at[pl.ds(start, rows), pl.ds(chip * cols, cols)]

    def place_body(*refs):
        srcs, dsts, bufs, sem = refs[:nw], refs[nw:2 * nw], refs[2 * nw:3 * nw], refs[3 * nw]
        me = 2 * lax.axis_index("x") + lax.axis_index("y")
        cps = []
        for i in range(nw):
            bufs[i][...] = srcs[i][...].astype(BF16)
            cps.append(pltpu.make_async_copy(bufs[i], win(dsts[i], i, me, 0, shapes[i][0]), sem.at[i]))
            cps[-1].start()
        for cp in cps:
            cp.wait()

    placed = pl.pallas_call(
        place_body, name=name, out_shape=[jax.ShapeDtypeStruct(s, d) for s, d in outs],
        in_specs=[pl.BlockSpec(memory_space=pltpu.VMEM)] * nw, out_specs=[pl.BlockSpec(memory_space=pltpu.HBM)] * nw,
        scratch_shapes=[pltpu.VMEM(s, BF16) for s in shapes] + [pltpu.SemaphoreType.DMA((nw,))],
        compiler_params=_params(),
    )(*shards)

    def copies(dsts, ss, rs):
        c, (me, xn, yn, dg), (px, py, psib) = _place()

        def rdma(k, window, peer):
            return _Lazy(lambda: pltpu.make_async_remote_copy(
                src_ref=window, dst_ref=window, send_sem=ss.at[k], recv_sem=rs.at[k], device_id=peer,
                device_id_type=MESH))

        first, later, swaps = [], [], []
        for i in range(nw):
            qr, hr, k0 = shapes[i][0] // 4, shapes[i][0] // 2, 6 * i
            q0, q1 = 2 * c * qr, (2 * c + 1) * qr
            w0, w1 = win(dsts[i], i, me, q0, qr), win(dsts[i], i, me, q1, qr)
            first.append([rdma(k0, w0, px), rdma(k0 + 1, w1, py), rdma(k0 + 2, w0, py), rdma(k0 + 3, w1, px)])
            later.append([rdma(k0 + 4, win(dsts[i], i, xn, q0, qr), py), rdma(k0 + 5, win(dsts[i], i, yn, q1, qr), px)])
            swaps.append([rdma(6 * nw + 3 * i + j, win(dsts[i], i, chip, c * hr, hr), psib)
                          for j, chip in enumerate((xn, yn, dg))])
        return first, later, swaps

    def send(srcs, dsts, ss, rs):
        for cps in copies(dsts, ss, rs)[0]:
            for cp in cps:
                cp.start()

    def forward(srcs, dsts, ss, rs):
        first, later, _ = copies(dsts, ss, rs)
        for i in range(nw):
            first[i][0].wait_recv()
            later[i][0].start()
            first[i][1].wait_recv()
            later[i][1].start()

    def swap(srcs, dsts, ss, rs):
        first, later, swaps = copies(dsts, ss, rs)
        for i in range(nw):
            for cp in first[i][2:] + later[i]:
                cp.wait_recv()
            for sw in swaps[i]:
                sw.start()

    def finish(srcs, dsts, ss, rs):
        first, later, swaps = copies(dsts, ss, rs)
        for i in range(nw):
            for sw in swaps[i]:
                sw.wait()
            for cp in first[i] + later[i]:
                cp.wait_send()

    return _Comm(placed, outs, 9 * nw, [send, forward, swap, finish], {i: i for i in range(nw)},
                 (0.0, 0.55, 0.85, 1.0))


def _gather_all(src):
    def copies(srcs, dsts, ss, rs):
        x, y, c = lax.axis_index("x"), lax.axis_index("y"), lax.axis_index("c")
        me, sib = (x, y, c), (x, y, 1 - c)
        chips = [(1 - x, y), (x, 1 - y), (1 - x, 1 - y)]

        def rows(px, py, pc):
            return dsts[0].at[4 * px + 2 * py + pc]

        def copy(k, block, to, src=None):
            return _Lazy(lambda: pltpu.make_async_remote_copy(
                src_ref=rows(*block) if src is None else src, dst_ref=rows(*block), send_sem=ss.at[k],
                recv_sem=rs.at[k], device_id=to, device_id_type=MESH))

        first = [copy(0, me, sib, src=srcs[0])] + [copy(1 + j, me, (*chip, c), src=srcs[0])
                                                   for j, chip in enumerate(chips)]
        passed = [copy(4 + j, (*chip, c), sib) for j, chip in enumerate(chips)]
        landed = [copy(0, sib, me)] + [copy(1 + j, (*chip, c), me) for j, chip in enumerate(chips)]
        landed += [copy(4 + j, (*chip, 1 - c), me) for j, chip in enumerate(chips)]
        return first, passed, landed

    def send(*refs):
        for cp in copies(*refs)[0]:
            cp.start()

    def forward(*refs):
        _, passed, landed = copies(*refs)
        for j in range(3):
            landed[1 + j].wait_recv()
            passed[j].start()

    def finish(*refs):
        first, passed, landed = copies(*refs)
        landed[0].wait_recv()
        for cp in landed[4:]:
            cp.wait_recv()
        for cp in first + passed:
            cp.wait_send()

    return _Comm([src], [((8,) + src.shape, src.dtype)], 7, [send, forward, finish], {}, (0.0, 0.5, 1.0))


def _f_in(x, g, w_in, tm):
    n = x.shape[0]

    def body(x_ref, g_ref, w_ref, h_ref, zgm_ref, zs5_ref, zg_ref):
        h = _rms_fwd(x_ref[...], g_ref[...]).astype(BF16)
        h_ref[...] = h
        zgm_ref[...] = _dot(h, w_ref[:, 0:2 * D])
        zs5_ref[...] = _dot(h, w_ref[:, 2 * D:2 * D + S5_W])
        zg_ref[...] = _dot(h, w_ref[:, 2 * D + S5_W:])

    return _rowcall("f_in", body, n, tm, [(x, 'row'), (g, 'full'), (w_in, 'full')],
                    [('row', D, BF16), ('row', 2 * D, F32), ('row', S5_W, F32), ('row', 2 * D, F32)])


def _tril():
    r = lax.broadcasted_iota(jnp.int32, (GM_CHUNK, GM_CHUNK), 0)
    c = lax.broadcasted_iota(jnp.int32, (GM_CHUNK, GM_CHUNK), 1)
    return r >= c


def _ln_stats(v):
    mu = jnp.mean(v, axis=-1, keepdims=True)
    vc = v - mu
    r = lax.rsqrt(jnp.mean(vc * vc, axis=-1, keepdims=True) + EPS)
    return vc * r, r


def _f_gmlp(zgm, ln_g, ln_b, w_s, b_s, tm):
    n = zgm.shape[0]

    def body(z_ref, lg_ref, lb_ref, ws_ref, bs_ref, y_ref):
        zg = _gelu(z_ref[...].astype(BF16))
        u = zg[:, :D]
        vh, _ = _ln_stats(zg[:, D:].astype(F32))
        vn = (vh * lg_ref[...] + lb_ref[...]).astype(BF16)
        keep = _tril()
        for g in range(GM_GROUPS):
            w = jnp.where(keep, ws_ref[g], 0.0).astype(BF16)
            cs = slice(g * LANE, (g + 1) * LANE)
            for c in range(tm // GM_CHUNK):
                rs = slice(c * GM_CHUNK, (c + 1) * GM_CHUNK)
                sv = _dot(w, vn[rs, cs]) + bs_ref[g]
                y_ref[rs, cs] = u[rs, cs] * sv.astype(BF16)

    return _rowcall("f_gmlp", body, n, tm,
                    [(zgm, 'row'), (ln_g, 'full'), (ln_b, 'full'), (w_s, 'full'), (b_s, 'full')],
                    [('row', D, BF16)])[0]


def _s5_specs(nb, nc, t, rev):
    def cc(c):
        return nc - 1 - c if rev else c
    slab = pl.BlockSpec((t, LANE), lambda j, b, c: (b * nc + cc(c), j))
    bb = pl.BlockSpec((None, LANE, 1024), lambda j, b, c: (j, 0, 0))
    cm = pl.BlockSpec((None, 1024, LANE), lambda j, b, c: (j, 0, 0))
    av = pl.BlockSpec((None, 1, 1024), lambda j, b, c: (j, 0, 0))
    dv = pl.BlockSpec((None, 1, LANE), lambda j, b, c: (j, 0, 0))
    st = pl.BlockSpec((None, None, None, 1, 1024), lambda j, b, c: (j, b, cc(c), 0, 0))
    return slab, bb, cm, av, dv, st


def _seg_load(ref, seg):
    return jnp.concatenate([ref[pl.ds(k, 8, stride=seg), :] for k in range(seg)], axis=0)


def _seg_store(ref, val, seg):
    for k in range(seg):
        ref[pl.ds(k, 8, stride=seg), :] = val[8 * k:8 * k + 8, :]


def _seg_scan(s_ref, ar, ai, seg, reverse):
    def step(i, carry):
        k = seg - 1 - i if reverse else i
        sr, si = carry
        nr = ar * sr - ai * si + s_ref[k, :, :512]
        ni = ar * si + ai * sr + s_ref[k, :, 512:]
        s_ref[k, :, :512] = nr
        s_ref[k, :, 512:] = ni
        return nr, ni

    zero = jnp.zeros((8, 512), F32)
    return lax.fori_loop(0, seg, step, (zero, zero), unroll=4)


def _seg_entries(tr, ti, pr, pi, cin, reverse):
    row = lax.broadcasted_iota(jnp.int32, (8, 512), 0)
    edge = row == (7 if reverse else 0)
    cr, ci = cin[:, :512], cin[:, 512:]
    xr = tr + jnp.where(edge, pr * cr - pi * ci, 0.0)
    xi = ti + jnp.where(edge, pr * ci + pi * cr, 0.0)
    ir, ii = _cscan(xr, xi, pr, pi, reverse)
    shift = 7 if reverse else 1
    er = jnp.where(edge, cr, pltpu.roll(ir, shift, 0))
    ei = jnp.where(edge, ci, pltpu.roll(ii, shift, 0))
    far = row == (0 if reverse else 7)
    out = jnp.concatenate([jnp.sum(jnp.where(far, ir, 0.0), axis=0, keepdims=True),
                           jnp.sum(jnp.where(far, ii, 0.0), axis=0, keepdims=True)], axis=1)
    return er, ei, out


def _seg_power(ar, ai, seg):
    pr, pi = ar, ai
    for _ in range(seg.bit_length() - 1):
        pr, pi = pr * pr - pi * pi, 2.0 * pr * pi
    return pr, pi


def _f_s5(zs5, bbc, ccm, avec, dvec, nb, t, comm):
    n = zs5.shape[0]
    nc = n // nb // t
    seg = t // 8
    slab, bb, cm, av, dv, st = _s5_specs(nb, nc, t, False)

    def body(u_ref, bb_ref, cc_ref, a_ref, d_ref, y_ref, st_ref, carry, s_ref):
        @pl.when(pl.program_id(2) == 0)
        def _():
            carry[...] = jnp.zeros_like(carry)
        cin = carry[...]
        st_ref[...] = cin
        up = _seg_load(u_ref, seg)
        s_ref[...] = _dot(up.astype(BF16), bb_ref[...]).reshape(seg, 8, 1024)
        a1r, a1i = a_ref[:, :512], a_ref[:, 512:]
        ar, ai = jnp.broadcast_to(a1r, (8, 512)), jnp.broadcast_to(a1i, (8, 512))
        tr, ti = _seg_scan(s_ref, ar, ai, seg, False)
        er, ei, cout = _seg_entries(tr, ti, *_seg_power(a1r, a1i, seg), cin, False)
        carry[...] = cout

        def apply(k, pe):
            pr, pi = pe
            pr, pi = ar * pr - ai * pi, ar * pi + ai * pr
            s_ref[k, :, :512] = s_ref[k, :, :512] + pr
            s_ref[k, :, 512:] = s_ref[k, :, 512:] + pi
            return pr, pi

        lax.fori_loop(0, seg, apply, (er, ei), unroll=4)
        s = s_ref[...].reshape(t, 1024).astype(BF16)
        _seg_store(y_ref, _dot(s, cc_ref[...]) + d_ref[...] * up, seg)

    return _hosting_call(
        body, "f_s5", (S5_SUPER, nb, nc), [slab, bb, cm, av, dv], [slab, st],
        [jax.ShapeDtypeStruct((n, S5_W), F32), jax.ShapeDtypeStruct((S5_SUPER, nb, nc, 1, 1024), F32)],
        [pltpu.VMEM((1, 1024), F32), pltpu.VMEM((seg, 8, 1024), F32)], ("arbitrary", "arbitrary", "arbitrary"),
        (zs5, bbc, ccm, avec, dvec), comm)


def _f_mix(ygm, ypre, zg, x, w_glu, w_br_gm, w_br_s5, w_mix_out, g_post, tm):
    n = x.shape[0]

    def body(ygm_ref, ypre_ref, zg_ref, x_ref, wglu_ref, wgm_ref, ws5_ref, wout_ref, g_ref,
             yg_ref, ys5_ref, a_ref, b_ref, mg_ref, mo_ref, x1_ref):
        yg = _gelu(ypre_ref[...])
        ygb = yg.astype(BF16)
        yg_ref[...] = ygb
        ys5 = (yg * _sig(_dot(ygb, wglu_ref[...]))).astype(BF16)
        ys5_ref[...] = ys5
        a = _dot(ygm_ref[...], wgm_ref[...])
        b = _dot(ys5, ws5_ref[...])
        a_ref[...] = a.astype(BF16)
        b_ref[...] = b.astype(BF16)
        zg = zg_ref[...]
        merged = (_sig(zg[:, :D]) * a + _sig(zg[:, D:]) * b).astype(BF16)
        mg_ref[...] = merged
        mo = _dot(merged, wout_ref[...])
        mo_ref[...] = mo
        x1_ref[...] = x_ref[...] + _rms_fwd(mo, g_ref[...])

    return _rowcall("f_mix", body, n, tm,
                    [(ygm, 'row'), (ypre, 'row'), (zg, 'row'), (x, 'row'), (w_glu, 'full'), (w_br_gm, 'full'),
                     (w_br_s5, 'full'), (w_mix_out, 'full'), (g_post, 'full')],
                    [('row', S5_W, BF16), ('row', S5_W, BF16), ('row', D, BF16), ('row', D, BF16),
                     ('row', D, BF16), ('row', D, F32), ('row', D, F32)])


def _f_mem(mem, g_mem, w_kv, tm):
    n = mem.shape[0]

    def body(m_ref, g_ref, w_ref, mn_ref, k_ref, v_ref):
        mn = _rms_fwd(m_ref[...], g_ref[...]).astype(BF16)
        mn_ref[...] = mn
        k_ref[...] = _dot(mn, w_ref[:, :D]).astype(BF16)
        v_ref[...] = _dot(mn, w_ref[:, D:]).astype(BF16)

    return _rowcall("f_mem", body, n, tm, [(mem, 'row'), (g_mem, 'full'), (w_kv, 'full')],
                    [('row', D, BF16), ('row', D, BF16), ('row', D, BF16)])


def _softmax(s):
    m = jnp.max(s, axis=-1, keepdims=True)
    e = jnp.exp(s - m)
    return e / jnp.sum(e, axis=-1, keepdims=True)


def _f_attn(x1, g_pre, w_q, k, v, w_o, g_post, tm, tpb, mlen):
    n = x1.shape[0]
    kv_spec = pl.BlockSpec((mlen, D), lambda i: (i // tpb, 0))
    scale = HEAD_DIM ** -0.5

    def body(x_ref, gp_ref, wq_ref, k_ref, v_ref, wo_ref, go_ref, hc_ref, o_ref, ao_ref, x2_ref):
        x1v = x_ref[...]
        hc = _rms_fwd(x1v, gp_ref[...]).astype(BF16)
        hc_ref[...] = hc
        q = _dot(hc, wq_ref[...])
        for h in range(HEADS):
            hs = slice(h * HEAD_DIM, (h + 1) * HEAD_DIM)
            p = _softmax(_dot_nt(q[:, hs].astype(BF16), k_ref[:, hs]) * scale)
            o_ref[:, hs] = _dot(p.astype(BF16), v_ref[:, hs]).astype(BF16)
        ao = _dot(o_ref[...], wo_ref[...])
        ao_ref[...] = ao
        x2_ref[...] = x1v + _rms_fwd(ao, go_ref[...])

    return _rowcall("f_attn", body, n, tm,
                    [(x1, 'row'), (g_pre, 'full'), (w_q, 'full'), (k, kv_spec), (v, kv_spec), (w_o, 'full'),
                     (g_post, 'full')],
                    [('row', D, BF16), ('row', D, BF16), ('row', D, F32), ('row', D, F32)])


def _f_ffn_up(x2, g_pre, w_gu, tm):
    n = x2.shape[0]

    def body(x_ref, g_ref, w_ref, hf_ref, gu_ref, act_ref):
        hf = _rms_fwd(x_ref[...], g_ref[...]).astype(BF16)
        hf_ref[...] = hf
        gg = _dot(hf, w_ref[:, :FFN_H])
        uu = _dot(hf, w_ref[:, FFN_H:])
        gu_ref[:, :FFN_H] = gg.astype(BF16)
        gu_ref[:, FFN_H:] = uu.astype(BF16)
        act_ref[...] = (gg * _sig(gg) * uu).astype(BF16)

    return _rowcall("f_ffn_up", body, n, tm, [(x2, 'row'), (g_pre, 'full'), (w_gu, 'full')],
                    [('row', D, BF16), ('row', 2 * FFN_H, BF16), ('row', FFN_H, BF16)])


def _f_ffn_down(act, x2, tgt, w_down, g_post, tm):
    n = x2.shape[0]

    def body(a_ref, x_ref, t_ref, w_ref, g_ref, dn_ref, d3_ref, loss_ref):
        @pl.when(pl.program_id(0) == 0)
        def _():
            loss_ref[...] = jnp.zeros_like(loss_ref)
        dn = _dot(a_ref[...], w_ref[...])
        dn_ref[...] = dn
        err = x_ref[...] + _rms_fwd(dn, g_ref[...]) - t_ref[...]
        d3_ref[...] = err * (1.0 / D)
        loss_ref[...] += jnp.sum(err * err, axis=0, keepdims=True)

    return _rowcall("f_ffn_down", body, n, tm,
                    [(act, 'row'), (x2, 'row'), (tgt, 'row'), (w_down, 'full'), (g_post, 'full')],
                    [('row', D, F32), ('row', D, F32), ('acc', (1, D), F32)])


def _b_ffn_down(d3, dn, gu, g_post, w_down, tm):
    n = d3.shape[0]

    def body(d3_ref, dn_ref, gu_ref, g_ref, w_ref, ddn_ref, dgu_ref, dg_ref):
        @pl.when(pl.program_id(0) == 0)
        def _():
            dg_ref[...] = jnp.zeros_like(dg_ref)
        ddn, dg = _rms_bwd(dn_ref[...], g_ref[...], d3_ref[...])
        dg_ref[...] += dg
        ddn = ddn.astype(BF16)
        ddn_ref[...] = ddn
        dact = _dot_nt(ddn, w_ref[...]).astype(BF16)
        gg = gu_ref[:, :FFN_H]
        uu = gu_ref[:, FFN_H:]
        sg = _sig(gg)
        silu = gg * sg
        dgu_ref[:, :FFN_H] = dact * uu * (sg + silu * (1.0 - sg))
        dgu_ref[:, FFN_H:] = dact * silu

    return _rowcall("b_ffn_down", body, n, tm,
                    [(d3, 'row'), (dn, 'row'), (gu, 'row'), (g_post, 'full'), (w_down, 'full')],
                    [('row', D, BF16), ('row', 2 * FFN_H, BF16), ('acc', (1, D), F32)])


def _b_ffn_up(dgu, d3, x2, w_gu, g_pre, tm):
    n = d3.shape[0]

    def body(dgu_ref, d3_ref, x_ref, w_ref, g_ref, dx_ref, dg_ref):
        @pl.when(pl.program_id(0) == 0)
        def _():
            dg_ref[...] = jnp.zeros_like(dg_ref)
        dhf = _dot_nt(dgu_ref[...], w_ref[...])
        dx, dg = _rms_bwd(x_ref[...], g_ref[...], dhf)
        dg_ref[...] += dg
        dx_ref[...] = d3_ref[...] + dx

    return _rowcall("b_ffn_up", body, n, tm,
                    [(dgu, 'row'), (d3, 'row'), (x2, 'row'), (w_gu, 'full'), (g_pre, 'full')],
                    [('row', D, F32), ('acc', (1, D), F32)])


def _b_attn(dx2, ao, x1, hc, g_post, w_o, w_q, g_pre, k, v, tm, tpb, mlen):
    n = dx2.shape[0]
    nb = n // (tm * tpb)
    kv_spec = pl.BlockSpec((mlen, D), lambda i: (i // tpb, 0))
    scale = HEAD_DIM ** -0.5

    def body(dx2_ref, ao_ref, x1_ref, hc_ref, go_ref, wo_ref, wq_ref, gp_ref, k_ref, v_ref,
             dao_ref, dq_ref, dx1_ref, dk_ref, dv_ref, dgo_ref, dgp_ref):
        i = pl.program_id(0)

        @pl.when(i == 0)
        def _():
            dgo_ref[...] = jnp.zeros_like(dgo_ref)
            dgp_ref[...] = jnp.zeros_like(dgp_ref)

        @pl.when(i % tpb == 0)
        def _():
            dk_ref[...] = jnp.zeros_like(dk_ref)
            dv_ref[...] = jnp.zeros_like(dv_ref)

        dx2v = dx2_ref[...]
        dao, dgo = _rms_bwd(ao_ref[...], go_ref[...], dx2v)
        dgo_ref[...] += dgo
        dao = dao.astype(BF16)
        dao_ref[...] = dao
        do = _dot_nt(dao, wo_ref[...])
        q = _dot(hc_ref[...], wq_ref[...])
        for h in range(HEADS):
            hs = slice(h * HEAD_DIM, (h + 1) * HEAD_DIM)
            qh = q[:, hs].astype(BF16)
            kh = k_ref[:, hs]
            p = _softmax(_dot_nt(qh, kh) * scale)
            doh = do[:, hs].astype(BF16)
            dp = _dot_nt(doh, v_ref[:, hs])
            ds = (p * (dp - jnp.sum(dp * p, axis=-1, keepdims=True)) * scale).astype(BF16)
            dq_ref[:, hs] = _dot(ds, kh).astype(BF16)
            dk_ref[:, hs] += _dot_tn(ds, qh)
            dv_ref[:, hs] += _dot_tn(p.astype(BF16), doh)
        dhc = _dot_nt(dq_ref[...], wq_ref[...])
        dx, dgp = _rms_bwd(x1_ref[...], gp_ref[...], dhc)
        dgp_ref[...] += dgp
        dx1_ref[...] = dx2v + dx

    return _rowcall("b_attn", body, n, tm,
                    [(dx2, 'row'), (ao, 'row'), (x1, 'row'), (hc, 'row'), (g_post, 'full'), (w_o, 'full'),
                     (w_q, 'full'), (g_pre, 'full'), (k, kv_spec), (v, kv_spec)],
                    [('row', D, BF16), ('row', D, BF16), ('row', D, F32),
                     (kv_spec, (nb * mlen, D), F32), (kv_spec, (nb * mlen, D), F32),
                     ('acc', (1, D), F32), ('acc', (1, D), F32)])


def _b_mem(dk, dv, mem, w_kv, g_mem, tm):
    n = mem.shape[0]

    def body(dk_ref, dv_ref, m_ref, w_ref, g_ref, dkv_ref, dg_ref):
        @pl.when(pl.program_id(0) == 0)
        def _():
            dg_ref[...] = jnp.zeros_like(dg_ref)
        dkb = dk_ref[...].astype(BF16)
        dvb = dv_ref[...].astype(BF16)
        dkv_ref[:, :D] = dkb
        dkv_ref[:, D:] = dvb
        dmn = _dot_nt(dkb, w_ref[:, :D]) + _dot_nt(dvb, w_ref[:, D:])
        _, dg = _rms_bwd(m_ref[...], g_ref[...], dmn)
        dg_ref[...] += dg

    return _rowcall("b_mem", body, n, tm, [(dk, 'row'), (dv, 'row'), (mem, 'row'), (w_kv, 'full'), (g_mem, 'full')],
                    [('row', 2 * D, BF16), ('acc', (1, D), F32)])


def _b_mix(dx1, mo, a, b, zg, ypre, g_post, w_mix_out, w_br_gm, w_br_s5, w_glu, tm):
    n = dx1.shape[0]

    def body(dx1_ref, mo_ref, a_ref, b_ref, zg_ref, ypre_ref, g_ref, wout_ref, wgm_ref, ws5_ref, wglu_ref,
             dmo_ref, da_ref, db_ref, dzg_ref, dygm_ref, dgate_ref, dypre_ref, dg_ref):
        @pl.when(pl.program_id(0) == 0)
        def _():
            dg_ref[...] = jnp.zeros_like(dg_ref)
        dmo, dg = _rms_bwd(mo_ref[...], g_ref[...], dx1_ref[...])
        dg_ref[...] += dg
        dmo = dmo.astype(BF16)
        dmo_ref[...] = dmo
        dmg = _dot_nt(dmo, wout_ref[...]).astype(BF16)
        zg = zg_ref[...].astype(BF16)
        sa, sb = _sig(zg[:, :D]), _sig(zg[:, D:])
        da = dmg * sa
        db = dmg * sb
        da_ref[...] = da
        db_ref[...] = db
        dzg_ref[:, :D] = dmg * a_ref[...] * (sa * (1.0 - sa))
        dzg_ref[:, D:] = dmg * b_ref[...] * (sb * (1.0 - sb))
        dygm_ref[...] = _dot_nt(da, wgm_ref[...])
        dys5 = _dot_nt(db, ws5_ref[...])
        ypre = ypre_ref[...]
        yg = _gelu(ypre)
        sgt = _sig(_dot(yg.astype(BF16), wglu_ref[...]))
        dgate = (dys5 * yg * sgt * (1.0 - sgt)).astype(BF16)
        dgate_ref[...] = dgate
        dyg = dys5 * sgt + _dot_nt(dgate, wglu_ref[...])
        dypre_ref[...] = dyg * _gelu_grad(ypre)

    return _rowcall("b_mix", body, n, tm,
                    [(dx1, 'row'), (mo, 'row'), (a, 'row'), (b, 'row'), (zg, 'row'), (ypre, 'row'), (g_post, 'full'),
                     (w_mix_out, 'full'), (w_br_gm, 'full'), (w_br_s5, 'full'), (w_glu, 'full')],
                    [('row', D, BF16), ('row', D, BF16), ('row', D, BF16), ('row', 2 * D, BF16), ('row', D, F32),
                     ('row', S5_W, BF16), ('row', S5_W, F32), ('acc', (1, D), F32)])


def _b_s5(zs5, dypre, states, bbc, ccm, avec, dvec, nb, t, comm):
    n = zs5.shape[0]
    nc = n // nb // t
    seg = t // 8
    slab, bb, cm, av, dv, st = _s5_specs(nb, nc, t, True)

    def body(u_ref, dy_ref, st_ref, bb_ref, cc_ref, a_ref, d_ref,
             du_ref, dbb_ref, dcc_ref, da_ref, dd_ref, lcarry, s_ref, l_ref):
        first = jnp.logical_and(pl.program_id(1) == 0, pl.program_id(2) == 0)

        @pl.when(first)
        def _():
            dbb_ref[...] = jnp.zeros_like(dbb_ref)
            dcc_ref[...] = jnp.zeros_like(dcc_ref)
            da_ref[...] = jnp.zeros_like(da_ref)
            dd_ref[...] = jnp.zeros_like(dd_ref)

        @pl.when(pl.program_id(2) == 0)
        def _():
            lcarry[...] = jnp.zeros_like(lcarry)

        up, dyp = _seg_load(u_ref, seg), _seg_load(dy_ref, seg)
        ub, dyb = up.astype(BF16), dyp.astype(BF16)
        a1r, a1i = a_ref[:, :512], a_ref[:, 512:]
        ar, ai = jnp.broadcast_to(a1r, (8, 512)), jnp.broadcast_to(a1i, (8, 512))
        pr, pi = _seg_power(a1r, a1i, seg)

        s_ref[...] = _dot(ub, bb_ref[...]).reshape(seg, 8, 1024)
        tr, ti = _seg_scan(s_ref, ar, ai, seg, False)
        er, ei, _ = _seg_entries(tr, ti, pr, pi, st_ref[...], False)

        def apply(k, pe):
            qr, qi = pe
            qr, qi = ar * qr - ai * qi, ar * qi + ai * qr
            s_ref[k, :, :512] = s_ref[k, :, :512] + qr
            s_ref[k, :, 512:] = s_ref[k, :, 512:] + qi
            return qr, qi

        lax.fori_loop(0, seg, apply, (er, ei), unroll=4)

        l_ref[...] = _dot_nt(dyb, cc_ref[...]).reshape(seg, 8, 1024)
        tr, ti = _seg_scan(l_ref, ar, -ai, seg, True)
        fr, fi, lout = _seg_entries(tr, ti, pr, -pi, lcarry[...], True)
        lcarry[...] = lout

        def apply_back(i, carry):
            qr, qi, accr, acci = carry
            k = seg - 1 - i
            qr, qi = ar * qr + ai * qi, ar * qi - ai * qr
            lr = l_ref[k, :, :512] + qr
            li = l_ref[k, :, 512:] + qi
            l_ref[k, :, :512] = lr
            l_ref[k, :, 512:] = li
            kp = jnp.maximum(k - 1, 0)
            sr = jnp.where(k == 0, er, s_ref[kp, :, :512])
            si = jnp.where(k == 0, ei, s_ref[kp, :, 512:])
            return qr, qi, accr + lr * sr + li * si, acci + li * sr - lr * si

        zero = jnp.zeros((8, 512), F32)
        _, _, accr, acci = lax.fori_loop(0, seg, apply_back, (fr, fi, zero, zero), unroll=4)
        da_ref[:, :512] += jnp.sum(accr, axis=0, keepdims=True)
        da_ref[:, 512:] += jnp.sum(acci, axis=0, keepdims=True)

        s = s_ref[...].reshape(t, 1024).astype(BF16)
        lam = l_ref[...].reshape(t, 1024).astype(BF16)
        dcc_ref[...] += _dot_tn(dyb, s)
        dbb_ref[...] += _dot_tn(ub, lam)
        _seg_store(du_ref, _dot_nt(lam, bb_ref[...]) + d_ref[...] * dyp, seg)
        dd_ref[...] += jnp.sum(dyp * up, axis=0, keepdims=True)

    return _hosting_call(
        body, "b_s5", (S5_SUPER, nb, nc), [slab, slab, st, bb, cm, av, dv], [slab, bb, bb, av, dv],
        [jax.ShapeDtypeStruct((n, S5_W), F32), jax.ShapeDtypeStruct(bbc.shape, F32),
         jax.ShapeDtypeStruct(bbc.shape, F32), jax.ShapeDtypeStruct(avec.shape, F32),
         jax.ShapeDtypeStruct(dvec.shape, F32)],
        [pltpu.VMEM((1, 1024), F32), pltpu.VMEM((seg, 8, 1024), F32), pltpu.VMEM((seg, 8, 1024), F32)],
        ("arbitrary", "arbitrary", "arbitrary"), (zs5, dypre, states, bbc, ccm, avec, dvec), comm)


def _b_gmlp(zgm, dygm, ln_g, ln_b, w_s, b_s, tm, comm):
    n = zgm.shape[0]

    def body(z_ref, dy_ref, lg_ref, lb_ref, ws_ref, bs_ref, dz_ref, dws_ref, dbs_ref, dlg_ref, dlb_ref,
             du_s, dvn_s):
        @pl.when(pl.program_id(0) == 0)
        def _():
            dws_ref[...] = jnp.zeros_like(dws_ref)
            dbs_ref[...] = jnp.zeros_like(dbs_ref)
            dlg_ref[...] = jnp.zeros_like(dlg_ref)
            dlb_ref[...] = jnp.zeros_like(dlb_ref)
        z = z_ref[...].astype(BF16)
        zg = _gelu(z)
        u = zg[:, :D]
        vh, r = _ln_stats(zg[:, D:].astype(F32))
        vn = (vh * lg_ref[...] + lb_ref[...]).astype(BF16)
        dy = dy_ref[...]
        keep = _tril()
        for g in range(GM_GROUPS):
            w = jnp.where(keep, ws_ref[g], 0.0).astype(BF16)
            cs = slice(g * LANE, (g + 1) * LANE)
            for c in range(tm // GM_CHUNK):
                rs = slice(c * GM_CHUNK, (c + 1) * GM_CHUNK)
                vb = vn[rs, cs]
                sv = _dot(w, vb) + bs_ref[g]
                dyb = dy[rs, cs]
                du_s[rs, cs] = dyb * sv
                dsv = dyb * u[rs, cs]
                dsvb = dsv.astype(BF16)
                dvn_s[rs, cs] = _dot_tn(w, dsvb)
                dws_ref[g] += jnp.where(keep, _dot_nt(dsvb, vb), 0.0)
                dbs_ref[g] += jnp.sum(dsv, axis=1, keepdims=True)
        dvn = dvn_s[...]
        dlg_ref[...] += jnp.sum(dvn * vh, axis=0, keepdims=True)
        dlb_ref[...] += jnp.sum(dvn, axis=0, keepdims=True)
        dvh = dvn * lg_ref[...]
        dv = r * (dvh - jnp.mean(dvh, axis=-1, keepdims=True) - vh * jnp.mean(dvh * vh, axis=-1, keepdims=True))
        dz_ref[:, :D] = du_s[...].astype(BF16) * _gelu_grad(z[:, :D])
        dz_ref[:, D:] = dv.astype(BF16) * _gelu_grad(z[:, D:])

    return _rowcall("b_gmlp", body, n, tm,
                    [(zgm, 'row'), (dygm, 'row'), (ln_g, 'full'), (ln_b, 'full'), (w_s, 'full'), (b_s, 'full')],
                    [('row', 2 * D, BF16), ('acc', w_s.shape, F32), ('acc', b_s.shape, F32), ('acc', (1, D), F32),
                     ('acc', (1, D), F32)],
                    scratch=[pltpu.VMEM((tm, D), F32), pltpu.VMEM((tm, D), F32)], comm=comm)


def _b_in(dzgm, dzs5, dzg, dx1, x, w_in, g_pre, tm, comm):
    n = x.shape[0]

    def body(d1_ref, d2_ref, d3_ref, dx1_ref, x_ref, w_ref, g_ref, gx_ref, dg_ref):
        @pl.when(pl.program_id(0) == 0)
        def _():
            dg_ref[...] = jnp.zeros_like(dg_ref)
        dh = (_dot_nt(d1_ref[...], w_ref[:, 0:2 * D]) + _dot_nt(d2_ref[...].astype(BF16), w_ref[:, 2 * D:2 * D + S5_W])
              + _dot_nt(d3_ref[...], w_ref[:, 2 * D + S5_W:]))
        dx, dg = _rms_bwd(x_ref[...], g_ref[...], dh)
        dg_ref[...] += dg
        gx_ref[...] = dx1_ref[...] + dx

    return _rowcall("b_in", body, n, tm,
                    [(dzgm, 'row'), (dzs5, 'row'), (dzg, 'row'), (dx1, 'row'), (x, 'row'), (w_in, 'full'),
                     (g_pre, 'full')],
                    [('row', D, F32), ('acc', (1, D), F32)], comm=comm)


def _whole(name, body, ins, outs):
    return pl.pallas_call(body, name=name, out_shape=[jax.ShapeDtypeStruct(s, dt) for s, dt in outs],
                          compiler_params=_params())(*ins)


def _s5_disc_fwd(lr, li, ls, br, bi):
    def body(lr_ref, li_ref, ls_ref, br_ref, bi_ref, o1, o2, o3, o4):
        outs = _s5_disc(lr_ref[...], li_ref[...], ls_ref[...], br_ref[...], bi_ref[...])
        for o, val in zip((o1, o2, o3, o4), outs):
            o[...] = val

    return _whole("s5_disc_fwd", body, [lr, li, ls, br, bi],
                  [(lr.shape, F32), (lr.shape, F32), (br.shape, F32), (br.shape, F32)])


def _s5_disc_bwd(lr, li, ls, br, bi, cts):
    def body(lr_ref, li_ref, ls_ref, br_ref, bi_ref, c1, c2, c3, c4, o1, o2, o3, o4, o5):
        _, vjp = jax.vjp(_s5_disc, lr_ref[...], li_ref[...], ls_ref[...], br_ref[...], bi_ref[...])
        grads = vjp((c1[...], c2[...], c3[...], c4[...]))
        for o, val in zip((o1, o2, o3, o4, o5), grads):
            o[...] = val

    return _whole("s5_disc_bwd", body, [lr, li, ls, br, bi, *cts],
                  [(lr.shape, F32), (lr.shape, F32), (lr.shape, F32), (br.shape, F32), (br.shape, F32)])


def _sum_slots(name, slots):
    def body(s_ref, o_ref):
        acc = s_ref[0]
        for k in range(1, slots.shape[0]):
            acc = acc + s_ref[k]
        o_ref[...] = acc

    return _whole(name, body, [slots], [(slots.shape[1:], F32)])[0]


def _adamw_math(w, g, m, v):
    c1 = 1.0 - ADAM_B1 ** ADAM_STEP
    c2 = 1.0 - ADAM_B2 ** ADAM_STEP
    mn = ADAM_B1 * m + (1.0 - ADAM_B1) * g
    vn = ADAM_B2 * v + (1.0 - ADAM_B2) * (g * g)
    return -ADAM_LR * ((mn / c1) / (jnp.sqrt(vn / c2) + ADAM_EPS) + ADAM_WD * w), mn, vn


def _adamw_many(name, ws, gs, ms, vs):
    k = len(ws)

    def body(*refs):
        for i in range(k):
            dl, mn, vn = _adamw_math(refs[i][...], refs[k + i][...], refs[2 * k + i][...], refs[3 * k + i][...])
            refs[4 * k + i][...] = dl
            refs[5 * k + i][...] = mn
            refs[6 * k + i][...] = vn

    res = _whole(name, body, [*ws, *gs, *ms, *vs], [(t.shape, F32) for t in ws] * 3)
    return res[:k], res[k:2 * k], res[2 * k:]


def _adamw(name, w, g, m, v, tm):
    def body(w_ref, g_ref, m_ref, v_ref, d_ref, mo_ref, vo_ref):
        d_ref[...], mo_ref[...], vo_ref[...] = _adamw_math(w_ref[...], g_ref[...], m_ref[...], v_ref[...])

    cols = w.shape[1]
    return _rowcall(name, body, w.shape[0], tm, [(w, 'row'), (g, 'row'), (m, 'row'), (v, 'row')],
                    [('row', cols, F32), ('row', cols, F32), ('row', cols, F32)])


def _picked_rowcall(name, body, grid, in_specs, out_cols, tm):
    return pl.pallas_call(
        lambda p_ref, *refs: body(*refs), name=name,
        grid_spec=pltpu.PrefetchScalarGridSpec(
            num_scalar_prefetch=1, grid=(grid,), in_specs=in_specs,
            out_specs=[pl.BlockSpec((tm, D), lambda i, p: (i, 0)) for _ in out_cols]),
        out_shape=[jax.ShapeDtypeStruct((grid * tm, D), dt) for dt in out_cols],
        compiler_params=_params(("arbitrary",)),
    )


def _exchange(copies):
    def build(src_refs, out_refs, ss, rs):
        def rdma(k, src, dst, peer):
            return _Lazy(lambda: pltpu.make_async_remote_copy(
                src_ref=src, dst_ref=dst, send_sem=ss.at[k], recv_sem=rs.at[k], device_id=peer, device_id_type=MESH))
        return copies(src_refs, out_refs, rdma)

    def start(*refs):
        for cp in build(*refs):
            cp.start()

    def finish(*refs):
        for cp in build(*refs):
            cp.wait()

    return start, finish


def _reduce_big(tag, slabs):
    rows = [s.shape[1] for s in slabs]
    blk = sum(rows) // 4
    tm = _tile_rows(blk, 512)
    per = blk // tm
    gpack = jnp.concatenate([s.reshape(4, 2, 2, r // 4, D) for s, r in zip(slabs, rows)], axis=3)

    mx, my, mc = lax.axis_index("x"), lax.axis_index("y"), lax.axis_index("c")
    chip_me, chip_xn, chip_yn = 2 * mx + my, 2 * (1 - mx) + my, 2 * mx + (1 - my)

    def pair_copies(srcs, dsts, rdma):
        c, _, (_, _, psib) = _place()
        return [rdma(0, srcs[0].at[:, 1 - c], dsts[0], psib)]

    from_sib = (yield _Comm([gpack], [((4, 2, blk, D), F32)], 1, _exchange(pair_copies), {}, (0.0, 1.0)))[0]

    def sum_both(a_ref, b_ref, o32_ref, o16_ref):
        s = a_ref[...].astype(F32) + b_ref[...].astype(F32)
        o32_ref[...] = s
        o16_ref[...] = s.astype(BF16)

    p32, p16 = _picked_rowcall(
        "sum_pair_" + tag, sum_both, 8 * per,
        [pl.BlockSpec((None, None, None, tm, D), lambda i, p: (i // (2 * per), p[0], (i // per) % 2, i % per, 0)),
         pl.BlockSpec((tm, D), lambda i, p: (i, 0))],
        [F32, BF16], tm)(jnp.stack([mc]).astype(jnp.int32), gpack, from_sib.reshape(8 * blk, D))
    p32, p16 = p32.reshape(4, 2, blk, D), p16.reshape(4, 2, blk, D)

    def step1_copies(srcs, dsts, rdma):
        _, (me, xn, yn, dg), (px, py, _) = _place()
        s16 = srcs[0]
        return [rdma(0, s16.at[yn, 0], dsts[0].at[0], py), rdma(1, s16.at[dg, 0], dsts[0].at[1], py),
                rdma(2, s16.at[xn, 1], dsts[0].at[2], px), rdma(3, s16.at[dg, 1], dsts[0].at[3], px)]

    recv1 = (yield _Comm([p16], [((4, blk, D), BF16)], 4, _exchange(step1_copies), {}, (0.0, 1.0)))[0]

    s32, s16 = _picked_rowcall(
        "sum_step1_" + tag, sum_both, 4 * per,
        [pl.BlockSpec((None, None, tm, D), lambda i, p: (p[i // per], i // (2 * per), i % per, 0)),
         pl.BlockSpec((tm, D), lambda i, p: (i, 0))],
        [F32, BF16], tm)(jnp.stack([chip_me, chip_xn, chip_me, chip_yn]).astype(jnp.int32), p32,
                         recv1.reshape(4 * blk, D))

    def step2_copies(srcs, dsts, rdma):
        _, _, (px, py, _) = _place()
        return [rdma(0, srcs[0].at[1], dsts[0].at[0], px), rdma(1, srcs[0].at[3], dsts[0].at[1], py)]

    recv2 = (yield _Comm([s16.reshape(4, blk, D)], [((2, blk, D), BF16)], 2, _exchange(step2_copies), {},
                         (0.0, 1.0)))[0]

    def sum_step2(a_ref, b_ref, o_ref):
        o_ref[...] = a_ref[...] + b_ref[...].astype(F32)

    red = _rowcall("sum_step2_" + tag, sum_step2, 2 * blk, tm,
                   [(s32.reshape(4, blk, D), pl.BlockSpec((None, tm, D), lambda i: (2 * (i // per), i % per, 0))),
                    (recv2.reshape(2 * blk, D), 'row')],
                   [('row', D, F32)])[0].reshape(2, blk, D)

    def share_copies(srcs, dsts, rdma):
        _, _, (_, _, psib) = _place()
        return [rdma(0, srcs[0], dsts[0], psib)]

    other = (yield _Comm([red], [((2, blk, D), F32)], 1, _exchange(share_copies), {}, (0.0, 1.0)))[0]
    lo = jnp.where(mc == 0, red, other)
    hi = jnp.where(mc == 0, other, red)
    out, off = [], 0
    for r in rows:
        q = r // 4
        out.append(jnp.concatenate([lo[0, off:off + q], lo[1, off:off + q], hi[0, off:off + q], hi[1, off:off + q]]))
        off += q
    return out


def _rows1024(a):
    flat = a.reshape(-1)
    pad = (-flat.shape[0]) % D
    if pad:
        flat = jnp.concatenate([flat, jnp.zeros((pad,), flat.dtype)])
    return flat.reshape(-1, D)


def _pack(arrs, pad_rows_to=8):
    parts = [_rows1024(a) for a in arrs]
    rows = sum(p.shape[0] for p in parts)
    pad = (-rows) % pad_rows_to
    if pad:
        parts.append(jnp.zeros((pad, D), parts[0].dtype))
    return jnp.concatenate(parts, axis=0)


def _unpack(packed, shapes):
    out, r = [], 0
    for s in shapes:
        size = math.prod(s)
        nr = -(-size // D)
        out.append(packed[r:r + nr].reshape(-1)[:size].reshape(s))
        r += nr
    return out


def _shard_slabs(full, axis):
    r, c = full.shape
    if axis == 0:
        return full.reshape(4, -1, D)
    return full.reshape(r, 4, c // 4).transpose(1, 0, 2).reshape(4, -1, D)


def _block_diag(t):
    eye = jnp.eye(8, dtype=t.dtype)
    j, g, a, b = t.shape
    return (t[:, :, :, None, :] * eye[None, :, None, :, None]).reshape(j, g * a, g * b)


def _block_diag_take(m, a, b):
    eye = jnp.eye(8, dtype=m.dtype)
    return (m.reshape(4, 8, a, 8, b) * eye[None, :, None, :, None]).sum(axis=3)


def kernel(x, mem, g_mix_pre, w_in, gm_ln_g, gm_ln_b, gm_w_s, gm_b_s, s5_lam_re, s5_lam_im, s5_log_step, s5_b_re, s5_b_im, s5_c_re, s5_c_im, s5_d, s5_w_glu, w_br_gm, w_br_s5, w_mix_out, g_mix_post, g_ca_pre, g_mem, ca_w_q, ca_w_kv, ca_w_o, g_ca_post, g_ffn_pre, ffn_w_gu, ffn_w_down, g_ffn_post, loss_target, m_g_mix_pre, m_w_in, m_gm_ln_g, m_gm_ln_b, m_gm_w_s, m_gm_b_s, m_s5_lam_re, m_s5_lam_im, m_s5_log_step, m_s5_b_re, m_s5_b_im, m_s5_c_re, m_s5_c_im, m_s5_d, m_s5_w_glu, m_w_br_gm, m_w_br_s5, m_w_mix_out, m_g_mix_post, m_g_ca_pre, m_g_mem, m_ca_w_q, m_ca_w_kv, m_ca_w_o, m_g_ca_post, m_g_ffn_pre, m_ffn_w_gu, m_ffn_w_down, m_g_ffn_post, v_g_mix_pre, v_w_in, v_gm_ln_g, v_gm_ln_b, v_gm_w_s, v_gm_b_s, v_s5_lam_re, v_s5_lam_im, v_s5_log_step, v_s5_b_re, v_s5_b_im, v_s5_c_re, v_s5_c_im, v_s5_d, v_s5_w_glu, v_w_br_gm, v_w_br_s5, v_w_mix_out, v_g_mix_post, v_g_ca_pre, v_g_mem, v_ca_w_q, v_ca_w_kv, v_ca_w_o, v_g_ca_post, v_g_ffn_pre, v_ffn_w_gu, v_ffn_w_down, v_g_ffn_post):
    a = dict(locals())
    w = {n: a[n][0] for n in WNAMES}
    nb, seq, _ = x.shape
    n = nb * seq
    mlen = mem.shape[1]
    tm = min(256, seq)
    tmb = min(512, seq)
    tpb = seq // tmb
    xf = x.reshape(n, D)
    tgt = loss_target.reshape(n, D)
    memf = mem.reshape(nb * mlen, D)
    tmm = min(256, mlen)

    big_names = list(BIG)
    local_shapes = {k: w[k].shape for k in big_names}
    w_in_full = _run_comm("gather_w_in", _gather_weights("cast_w_in", [w['w_in']], [BIG['w_in']]))[0]
    rest = [k for k in big_names if k != 'w_in']
    gather_rest = _gather_weights("cast_w_rest", [w[k] for k in rest], [BIG[k] for k in rest])

    vec = lambda name: w[name].reshape(1, D)

    to_lane = lambda p: p.reshape(1, -1)
    b_t = lambda p: p.transpose(2, 0, 1).reshape(16, -1)
    lr_l, li_l = to_lane(w['s5_lam_re']), to_lane(w['s5_lam_im'])
    ls_l = jnp.repeat(w['s5_log_step'], 64).reshape(1, -1)
    br_t, bi_t = b_t(w['s5_b_re']), b_t(w['s5_b_im'])
    ab_re, ab_im, bb_re, bb_im = _s5_disc_fwd(lr_l, li_l, ls_l, br_t, bi_t)
    blk = lambda t: _block_diag(t.reshape(16, 4, 8, 64).transpose(1, 2, 0, 3))
    bbc = jnp.concatenate([blk(bb_re), blk(bb_im)], axis=2).astype(BF16)
    cblk = lambda c: _block_diag(c.reshape(4, 8, 16, 64).transpose(0, 1, 3, 2))
    ccm = jnp.concatenate([cblk(w['s5_c_re']), -cblk(w['s5_c_im'])], axis=1).astype(BF16)
    avec = jnp.concatenate([ab_re.reshape(4, 1, 512), ab_im.reshape(4, 1, 512)], axis=2)
    dvec = w['s5_d'].reshape(4, 1, LANE)

    bs3 = w['gm_b_s'].reshape(GM_GROUPS, GM_CHUNK, 1)
    wf = {'w_in': w_in_full}
    h, zgm, zs5, zg = _f_in(xf, vec('g_mix_pre'), wf['w_in'], tmb)
    ygm = _f_gmlp(zgm, vec('gm_ln_g'), vec('gm_ln_b'), w['gm_w_s'], bs3, tmb)
    s5_t = min(S5_T, seq)
    (ypre, states), gathered = _f_s5(zs5, bbc, ccm, avec, dvec, nb, s5_t, gather_rest)
    for k, g in zip(rest, gathered):
        wf[k] = g.reshape(-1, g.shape[-1]) if BIG[k] == 0 else g
    yg, ys5, a_br, b_br, merged, mo, x1 = _f_mix(ygm, ypre, zg, xf, wf['s5_w_glu'], wf['w_br_gm'], wf['w_br_s5'],
                                                 wf['w_mix_out'], vec('g_mix_post'), tm)
    mem_n, kk, vv = _f_mem(memf, vec('g_mem'), wf['ca_w_kv'], tmm)
    hc, o_att, ao, x2 = _f_attn(x1, vec('g_ca_pre'), wf['ca_w_q'], kk, vv, wf['ca_w_o'], vec('g_ca_post'),
                                tmb, tpb, mlen)
    hf, gu, act = _f_ffn_up(x2, vec('g_ffn_pre'), wf['ffn_w_gu'], tm)
    dn, d3, loss_cols = _f_ffn_down(act, x2, tgt, wf['ffn_w_down'], vec('g_ffn_post'), tmb)

    gsm = {}
    gbig = {}
    ddn, dgu, gsm['g_ffn_post'] = _b_ffn_down(d3, dn, gu, vec('g_ffn_post'), wf['ffn_w_down'], tm)
    dx2, gsm['g_ffn_pre'] = _b_ffn_up(dgu, d3, x2, wf['ffn_w_gu'], vec('g_ffn_pre'), tmb)
    gbig['ffn_w_down'] = _mm_tn("dw_ffn_down", act, ddn)
    gbig['ffn_w_gu'] = _mm_tn("dw_ffn_gu", dgu, hf)
    dao, dq, dx1, dk, dv, gsm['g_ca_post'], gsm['g_ca_pre'] = _b_attn(
        dx2, ao, x1, hc, vec('g_ca_post'), wf['ca_w_o'], wf['ca_w_q'], vec('g_ca_pre'), kk, vv, tmb, tpb, mlen)
    gbig['ca_w_o'] = _mm_tn("dw_ca_o", o_att, dao)
    gbig['ca_w_q'] = _mm_tn("dw_ca_q", hc, dq)
    dkv, gsm['g_mem'] = _b_mem(dk, dv, memf, wf['ca_w_kv'], vec('g_mem'), tmm)
    gbig['ca_w_kv'] = _mm_tn("dw_ca_kv", dkv, mem_n)
    dmo, da_br, db_br, dzg, dygm, dgate, dypre, gsm['g_mix_post'] = _b_mix(
        dx1, mo, a_br, b_br, zg, ypre, vec('g_mix_post'), wf['w_mix_out'], wf['w_br_gm'], wf['w_br_s5'],
        wf['s5_w_glu'], tm)
    gbig['w_mix_out'] = _mm_tn("dw_mix_out", merged, dmo)
    gbig['w_br_gm'] = _mm_tn("dw_br_gm", ygm, da_br)
    gbig['w_br_s5'] = _mm_tn("dw_br_s5", db_br, ys5)
    gbig['s5_w_glu'] = _mm_tn("dw_s5_glu", yg, dgate)
    slab_of = lambda k: gbig[k].reshape(4, -1, D)
    red_rest = _reduce_big("rest", [slab_of(k) for k in rest])
    (dzs5, dbbc, dccm_t, davec, ddvec), got = _b_s5(zs5, dypre, states, bbc, ccm, avec, dvec, nb, s5_t,
                                                    next(red_rest))
    dccm = dccm_t.transpose(0, 2, 1)
    (dzgm, gsm['gm_w_s'], dbs3, gsm['gm_ln_g'], gsm['gm_ln_b']), got = _b_gmlp(
        zgm, dygm, vec('gm_ln_g'), vec('gm_ln_b'), w['gm_w_s'], bs3, tmb, red_rest.send(got))
    gsm['gm_b_s'] = dbs3
    grad_x, gsm['g_mix_pre'] = _b_in(dzgm, dzs5, dzg, dx1, xf, wf['w_in'], vec('g_mix_pre'), tmb, None)
    dw_in_gm, got = _mm_tn("dw_in_gm", dzgm, h, comm=red_rest.send(got))
    dw_in_s5, got = _mm_tn("dw_in_s5", dzs5, h, comm=red_rest.send(got))
    greds = dict(zip(rest, _finish(red_rest, got)))

    unblk = lambda m_: _block_diag_take(m_, 16, 64).transpose(2, 0, 1, 3).reshape(16, -1)
    d_bb_re, d_bb_im = unblk(dbbc[:, :, :512]), unblk(dbbc[:, :, 512:])
    cunblk = lambda m_: _block_diag_take(m_, 64, 16).transpose(0, 1, 3, 2).reshape(32, 16, 64)
    gsm['s5_c_re'] = cunblk(dccm[:, :512, :])
    gsm['s5_c_im'] = -cunblk(dccm[:, 512:, :])
    d_ab_re, d_ab_im = davec[:, :, :512].reshape(1, -1), davec[:, :, 512:].reshape(1, -1)
    g_lr, g_li, g_ls, g_br, g_bi = _s5_disc_bwd(lr_l, li_l, ls_l, br_t, bi_t, (d_ab_re, d_ab_im, d_bb_re, d_bb_im))
    gsm['s5_lam_re'], gsm['s5_lam_im'] = g_lr.reshape(32, 64), g_li.reshape(32, 64)
    gsm['s5_log_step'] = g_ls.reshape(32, 64).sum(axis=1)
    from_t = lambda t: t.reshape(16, 32, 64).transpose(1, 2, 0)
    gsm['s5_b_re'], gsm['s5_b_im'] = from_t(g_br), from_t(g_bi)
    gsm['s5_d'] = ddvec.reshape(32, 16)

    small_shapes = [w[k].shape for k in SMALL]
    spack = _pack([gsm[k].reshape(w[k].shape) for k in SMALL] + [loss_cols], 8)
    rs = spack.shape[0]
    me_slot = 4 * lax.axis_index("x") + 2 * lax.axis_index("y") + lax.axis_index("c")
    dw_in_g, (sall,) = _mm_tn("dw_in_g", dzg, h, comm=_gather_all(spack))
    gbig['w_in'] = jnp.concatenate([dw_in_gm, dw_in_s5, dw_in_g], axis=0)
    sall = lax.dynamic_update_index_in_dim(sall, spack, me_slot, 0)
    ssum = _sum_slots("sum_small", sall)
    small_red = _unpack(ssum, small_shapes + [(1, D)])
    loss = 0.5 * jnp.sum(small_red[-1]) / D

    red_in = _reduce_big("w_in", [slab_of('w_in')])
    got = _run_comm("reduce_pair_w_in", next(red_in))
    for phase in ("reduce_step1_w_in", "reduce_step2_w_in"):
        got = _run_comm(phase, red_in.send(got))
    got = _run_comm("share_pair_w_in", red_in.send(got))
    greds['w_in'] = _finish(red_in, got)[0]

    res_big = {}
    for k in big_names:
        r_, c_ = local_shapes[k]
        g = greds[k].reshape(r_, c_) if BIG[k] == 0 else greds[k].reshape(c_, r_).T
        dl, mn, vn = _adamw("adamw_" + k, w[k], g, a['m_' + k][0], a['v_' + k][0], _tile_rows(g.shape[0], 256))
        res_big[k] = (g, dl, mn, vn)
    as2d = lambda t: t.reshape(1, -1) if t.ndim == 1 else t
    dl_s, mn_s, vn_s = _adamw_many("adamw_small", [as2d(w[k]) for k in SMALL], [as2d(g) for g in small_red[:-1]],
                                   [as2d(a['m_' + k][0]) for k in SMALL], [as2d(a['v_' + k][0]) for k in SMALL])
    res_small = {k: (g, dl.reshape(w[k].shape), mn.reshape(w[k].shape), vn.reshape(w[k].shape))
                 for k, g, dl, mn, vn in zip(SMALL, small_red[:-1], dl_s, mn_s, vn_s)}

    res = {**res_big, **res_small}
    outs = [loss, grad_x.reshape(nb, seq, D)]
    for i in range(4):
        outs += [res[k][i][None] for k in WNAMES]
    return tuple(outs)


def _finish(gen, got):
    try:
        gen.send(got)
    except StopIteration as done:
        return done.value
    raise AssertionError("the generator has more exchanges")


def _tile_rows(rows, cap=384):
    best = rows
    for t in range(16, min(rows, cap) + 1, 16):
        if rows % t == 0:
            best = t
    return best
```

```python
import math
from typing import Any, NamedTuple

import jax
import jax.numpy as jnp
from jax import lax
from jax.experimental import pallas as pl
from jax.experimental.pallas import tpu as pltpu

F32 = jnp.float32
BF16 = jnp.bfloat16
EPS = 1e-6
D = 1024
GM_CHUNK = 128
GM_GROUPS = 8
S5_W = 512
S5_SUPER = 4
S5_T = 512
HEADS = 4
HEAD_DIM = 256
FFN_H = 2816
LANE = 128
VMEM_LIMIT = 56 * 1024 * 1024
MESH = pl.DeviceIdType.MESH

ADAM_LR, ADAM_B1, ADAM_B2, ADAM_EPS, ADAM_WD, ADAM_STEP = 0.001, 0.9, 0.999, 1e-08, 0.01, 10

WNAMES = ['g_mix_pre', 'w_in', 'gm_ln_g', 'gm_ln_b', 'gm_w_s', 'gm_b_s', 's5_lam_re', 's5_lam_im', 's5_log_step',
          's5_b_re', 's5_b_im', 's5_c_re', 's5_c_im', 's5_d', 's5_w_glu', 'w_br_gm', 'w_br_s5', 'w_mix_out',
          'g_mix_post', 'g_ca_pre', 'g_mem', 'ca_w_q', 'ca_w_kv', 'ca_w_o', 'g_ca_post', 'g_ffn_pre', 'ffn_w_gu',
          'ffn_w_down', 'g_ffn_post']
BIG = {'w_in': 1, 's5_w_glu': 0, 'w_br_gm': 0, 'w_br_s5': 1, 'w_mix_out': 0, 'ca_w_q': 0, 'ca_w_kv': 1,
       'ca_w_o': 0, 'ffn_w_gu': 1, 'ffn_w_down': 0}
SMALL = [n for n in WNAMES if n not in BIG]


def _dot(a, b):
    return jnp.dot(a, b, preferred_element_type=F32)


def _dot_nt(a, b):
    return lax.dot_general(a, b, (((1,), (1,)), ((), ())), preferred_element_type=F32)


def _dot_tn(a, b):
    return lax.dot_general(a, b, (((0,), (0,)), ((), ())), preferred_element_type=F32)


def _rms_fwd(x, g):
    r = lax.rsqrt(jnp.mean(x * x, axis=-1, keepdims=True) + EPS)
    return x * r * g


def _rms_bwd(x, g, dy):
    r = lax.rsqrt(jnp.mean(x * x, axis=-1, keepdims=True) + EPS)
    xh = x * r
    gdy = dy * g
    dx = r * (gdy - xh * jnp.mean(gdy * xh, axis=-1, keepdims=True))
    dg = jnp.sum(dy * xh, axis=0, keepdims=True)
    return dx, dg


_GC = math.sqrt(2.0 / math.pi)


def _gelu(x):
    return 0.5 * x * (1.0 + jnp.tanh(_GC * (x + 0.044715 * x * x * x)))


def _gelu_grad(x):
    t = jnp.tanh(_GC * (x + 0.044715 * x * x * x))
    return 0.5 * (1.0 + t) + 0.5 * x * (1.0 - t * t) * _GC * (1.0 + 3 * 0.044715 * x * x)


def _sig(x):
    return 1.0 / (1.0 + jnp.exp(-x))


def _cscan(br, bi, ar, ai, reverse):
    t = br.shape[0]
    row = lax.broadcasted_iota(jnp.int32, br.shape, 0)
    pr, pi = ar, ai
    sh = 1
    while sh < t:
        if reverse:
            keep = row < t - sh
            rr, ri = pltpu.roll(br, t - sh, 0), pltpu.roll(bi, t - sh, 0)
        else:
            keep = row >= sh
            rr, ri = pltpu.roll(br, sh, 0), pltpu.roll(bi, sh, 0)
        rr = jnp.where(keep, rr, 0.0)
        ri = jnp.where(keep, ri, 0.0)
        br, bi = br + pr * rr - pi * ri, bi + pr * ri + pi * rr
        pr, pi = pr * pr - pi * pi, 2.0 * pr * pi
        sh *= 2
    return br, bi


def _s5_disc(lr, li, ls, br, bi):
    step = jnp.exp(ls)
    mag = jnp.exp(lr * step)
    ab_re = mag * jnp.cos(li * step)
    ab_im = mag * jnp.sin(li * step)
    den = lr * lr + li * li
    nr = ab_re - 1.0
    co_re = (nr * lr + ab_im * li) / den
    co_im = (ab_im * lr - nr * li) / den
    return ab_re, ab_im, co_re * br - co_im * bi, co_re * bi + co_im * br


def _params(sem=None):
    return pltpu.CompilerParams(dimension_semantics=sem, vmem_limit_bytes=VMEM_LIMIT)


def _rowcall(name, body, n_rows, tm, ins, outs, scratch=(), comm=None):
    def spec(kind, shape):
        if kind == 'row':
            return pl.BlockSpec((tm, shape[1]), lambda i: (i, 0))
        if kind == 'full':
            nd = len(shape)
            return pl.BlockSpec(tuple(shape), lambda i: (0,) * nd, pipeline_mode=pl.Buffered(1))
        if kind == 'acc':
            nd = len(shape)
            return pl.BlockSpec(tuple(shape), lambda i: (0,) * nd)
        return kind

    in_specs = [spec(k, a.shape) for a, k in ins]
    out_shape, out_specs = [], []
    for k, s, dt in outs:
        shape = (n_rows, s) if k == 'row' else tuple(s)
        out_shape.append(jax.ShapeDtypeStruct(shape, dt))
        out_specs.append(spec(k, shape))
    args = [a for a, _ in ins]
    if comm is not None:
        return _hosting_call(body, name, (n_rows // tm,), in_specs, out_specs, out_shape, list(scratch),
                             ("arbitrary",), args, comm)
    return pl.pallas_call(
        body, name=name, grid=(n_rows // tm,), in_specs=in_specs, out_specs=out_specs, out_shape=out_shape,
        scratch_shapes=list(scratch), compiler_params=_params(("arbitrary",)),
    )(*args)


def _tile(n, cap):
    if n <= cap:
        return n
    best = LANE
    for k in range(1, n // LANE + 1):
        t = k * LANE
        if n % t == 0 and t <= cap:
            best = t
    return best


def _mm_tn(name, a, b, comm=None):
    n, k = a.shape
    m = b.shape[1]
    tk, tn, tr = _tile(k, 1536), _tile(m, 1536), min(n, 1024)

    def body(a_ref, b_ref, o_ref):
        @pl.when(pl.program_id(2) == 0)
        def _():
            o_ref[...] = jnp.zeros_like(o_ref)
        o_ref[...] += _dot_tn(a_ref[...].astype(BF16), b_ref[...].astype(BF16))

    out_spec = pl.BlockSpec((tk, tn), lambda i, j, r: (i, j))
    out_shape = (k, m)
    grid = (k // tk, m // tn, n // tr)
    in_specs = [pl.BlockSpec((tr, tk), lambda i, j, r: (r, i)), pl.BlockSpec((tr, tn), lambda i, j, r: (r, j))]
    if comm is not None:
        (res,), extra = _hosting_call(body, name, grid, in_specs, [out_spec], [jax.ShapeDtypeStruct(out_shape, F32)],
                                      [], ("arbitrary", "arbitrary", "arbitrary"), (a, b), comm)
        return res, extra
    return pl.pallas_call(
        body, name=name, grid=grid, in_specs=in_specs, out_specs=out_spec,
        out_shape=jax.ShapeDtypeStruct(out_shape, F32),
        compiler_params=_params(("parallel", "parallel", "arbitrary")),
    )(a, b)


def _comm_call(name, srcs, outs, nsem, body_fn, aliases=None):
    ns, no = len(srcs), len(outs)

    def body(*refs):
        body_fn(refs[:ns], refs[ns:ns + no], refs[ns + no], refs[ns + no + 1])

    hbm = pl.BlockSpec(memory_space=pltpu.HBM)
    return pl.pallas_call(
        body, name=name, in_specs=[hbm] * ns, out_specs=[hbm] * no,
        out_shape=[jax.ShapeDtypeStruct(s, d) for s, d in outs],
        scratch_shapes=[pltpu.SemaphoreType.DMA((nsem,)), pltpu.SemaphoreType.DMA((nsem,))],
        input_output_aliases=aliases or {},
    )(*srcs)


def _place():
    x, y, c = lax.axis_index("x"), lax.axis_index("y"), lax.axis_index("c")
    chips = (2 * x + y, 2 * (1 - x) + y, 2 * x + (1 - y), 2 * (1 - x) + (1 - y))
    peers = ((1 - x, y, c), (x, 1 - y, c), (x, y, 1 - c))
    return c, chips, peers


class _Comm(NamedTuple):
    srcs: Any
    outs: Any
    nsem: int
    stages: Any
    aliases: Any
    fracs: Any


class _Lazy:
    def __init__(self, make):
        self._make = make

    def start(self):
        self._make().start()

    def wait(self):
        self._make().wait()

    def wait_recv(self):
        self._make().wait_recv()

    def wait_send(self):
        self._make().wait_send()


def _run_comm(name, comm):
    def body_fn(srcs, dsts, ss, rs):
        for stage in comm.stages:
            stage(srcs, dsts, ss, rs)

    return _comm_call(name, comm.srcs, comm.outs, comm.nsem, body_fn, aliases=comm.aliases)


def _hosting_call(body, name, grid, in_specs, out_specs, out_shape, scratch, sem, args, comm):
    hbm = pl.BlockSpec(memory_space=pltpu.HBM)
    n_in, n_out, n_sc = len(in_specs), len(out_specs), len(scratch)
    ns, no = len(comm.srcs), len(comm.outs)
    total = math.prod(grid)
    steps = [min(total - 1, int(f * total)) for f in comm.fracs]

    def wrapped(*refs):
        ins, csrc = refs[:n_in], refs[n_in:n_in + ns]
        outs, cdst = refs[n_in + ns:n_in + ns + n_out], refs[n_in + ns + n_out:n_in + ns + n_out + no]
        rest = refs[n_in + ns + n_out + no:]
        lin = 0
        for d, size in enumerate(grid):
            lin = lin * size + pl.program_id(d)
        for stage, at in zip(comm.stages, steps):
            @pl.when(lin == at)
            def _(stage=stage):
                stage(csrc, cdst, rest[n_sc], rest[n_sc + 1])
        body(*ins, *outs, *rest[:n_sc])

    res = pl.pallas_call(
        wrapped, name=name, grid=grid, in_specs=list(in_specs) + [hbm] * ns, out_specs=list(out_specs) + [hbm] * no,
        out_shape=list(out_shape) + [jax.ShapeDtypeStruct(s, d) for s, d in comm.outs],
        scratch_shapes=list(scratch) + [pltpu.SemaphoreType.DMA((comm.nsem,)), pltpu.SemaphoreType.DMA((comm.nsem,))],
        input_output_aliases={n_in + i: n_out + o for i, o in comm.aliases.items()},
        compiler_params=_params(sem),
    )(*args, *comm.srcs)
    return res[:n_out], res[n_out:]


def _gather_weights(name, shards, axes):
    nw = len(shards)
    shapes = [s.shape for s in shards]
    outs = [((4,) + s if ax == 0 else (s[0], 4 * s[1]), BF16) for s, ax in zip(shapes, axes)]

    def win(ref, i, chip, start, rows):
        r, cols = shapes[i]
        if axes[i] == 0:
            return ref.at[chip, pl.ds(start, rows), :]
        return ref.at[pl.ds(start, rows), pl.ds(chip * cols, cols)]

    def place_body(*refs):
        srcs, dsts, bufs, sem = refs[:nw], refs[nw:2 * nw], refs[2 * nw:3 * nw], refs[3 * nw]
        me = 2 * lax.axis_index("x") + lax.axis_index("y")
        cps = []
        for i in range(nw):
            bufs[i][...] = srcs[i][...].astype(BF16)
            cps.append(pltpu.make_async_copy(bufs[i], win(dsts[i], i, me, 0, shapes[i][0]), sem.at[i]))
            cps[-1].start()
        for cp in cps:
            cp.wait()

    placed = pl.pallas_call(
        place_body, name=name, out_shape=[jax.ShapeDtypeStruct(s, d) for s, d in outs],
        in_specs=[pl.BlockSpec(memory_space=pltpu.VMEM)] * nw, out_specs=[pl.BlockSpec(memory_space=pltpu.HBM)] * nw,
        scratch_shapes=[pltpu.VMEM(s, BF16) for s in shapes] + [pltpu.SemaphoreType.DMA((nw,))],
        compiler_params=_params(),
    )(*shards)

    def copies(dsts, ss, rs):
        c, (me, xn, yn, dg), (px, py, psib) = _place()

        def rdma(k, window, peer):
            return _Lazy(lambda: pltpu.make_async_remote_copy(
                src_ref=window, dst_ref=window, send_sem=ss.at[k], recv_sem=rs.at[k], device_id=peer,
                device_id_type=MESH))

        first, later, swaps = [], [], []
        for i in range(nw):
            qr, hr, k0 = shapes[i][0] // 4, shapes[i][0] // 2, 6 * i
            q0, q1 = 2 * c * qr, (2 * c + 1) * qr
            w0, w1 = win(dsts[i], i, me, q0, qr), win(dsts[i], i, me, q1, qr)
            first.append([rdma(k0, w0, px), rdma(k0 + 1, w1, py), rdma(k0 + 2, w0, py), rdma(k0 + 3, w1, px)])
            later.append([rdma(k0 + 4, win(dsts[i], i, xn, q0, qr), py), rdma(k0 + 5, win(dsts[i], i, yn, q1, qr), px)])
            swaps.append([rdma(6 * nw + 3 * i + j, win(dsts[i], i, chip, c * hr, hr), psib)
                          for j, chip in enumerate((xn, yn, dg))])
        return first, later, swaps

    def send(srcs, dsts, ss, rs):
        for cps in copies(dsts, ss, rs)[0]:
            for cp in cps:
                cp.start()

    def forward(srcs, dsts, ss, rs):
        first, later, _ = copies(dsts, ss, rs)
        for i in range(nw):
            first[i][0].wait_recv()
            later[i][0].start()
            first[i][1].wait_recv()
            later[i][1].start()

    def swap(srcs, dsts, ss, rs):
        first, later, swaps = copies(dsts, ss, rs)
        for i in range(nw):
            for cp in first[i][2:] + later[i]:
                cp.wait_recv()
            for sw in swaps[i]:
                sw.start()

    def finish(srcs, dsts, ss, rs):
        first, later, swaps = copies(dsts, ss, rs)
        for i in range(nw):
            for sw in swaps[i]:
                sw.wait()
            for cp in first[i] + later[i]:
                cp.wait_send()

    return _Comm(placed, outs, 9 * nw, [send, forward, swap, finish], {i: i for i in range(nw)},
                 (0.0, 0.55, 0.85, 1.0))


def _gather_all(src):
    def copies(srcs, dsts, ss, rs):
        x, y, c = lax.axis_index("x"), lax.axis_index("y"), lax.axis_index("c")
        me, sib = (x, y, c), (x, y, 1 - c)
        chips = [(1 - x, y), (x, 1 - y), (1 - x, 1 - y)]

        def rows(px, py, pc):
            return dsts[0].at[4 * px + 2 * py + pc]

        def copy(k, block, to, src=None):
            return _Lazy(lambda: pltpu.make_async_remote_copy(
                src_ref=rows(*block) if src is None else src, dst_ref=rows(*block), send_sem=ss.at[k],
                recv_sem=rs.at[k], device_id=to, device_id_type=MESH))

        first = [copy(0, me, sib, src=srcs[0])] + [copy(1 + j, me, (*chip, c), src=srcs[0])
                                                   for j, chip in enumerate(chips)]
        passed = [copy(4 + j, (*chip, c), sib) for j, chip in enumerate(chips)]
        landed = [copy(0, sib, me)] + [copy(1 + j, (*chip, c), me) for j, chip in enumerate(chips)]
        landed += [copy(4 + j, (*chip, 1 - c), me) for j, chip in enumerate(chips)]
        return first, passed, landed

    def send(*refs):
        for cp in copies(*refs)[0]:
            cp.start()

    def forward(*refs):
        _, passed, landed = copies(*refs)
        for j in range(3):
            landed[1 + j].wait_recv()
            passed[j].start()

    def finish(*refs):
        first, passed, landed = copies(*refs)
        landed[0].wait_recv()
        for cp in landed[4:]:
            cp.wait_recv()
        for cp in first + passed:
            cp.wait_send()

    return _Comm([src], [((8,) + src.shape, src.dtype)], 7, [send, forward, finish], {}, (0.0, 0.9, 1.0))


def _f_in(x, g, w_in, tm):
    n = x.shape[0]

    def body(x_ref, g_ref, w_ref, h_ref, zgm_ref, zs5_ref, zg_ref):
        h = _rms_fwd(x_ref[...], g_ref[...]).astype(BF16)
        h_ref[...] = h
        zgm_ref[...] = _dot(h, w_ref[:, 0:2 * D])
        zs5_ref[...] = _dot(h, w_ref[:, 2 * D:2 * D + S5_W])
        zg_ref[...] = _dot(h, w_ref[:, 2 * D + S5_W:])

    return _rowcall("f_in", body, n, tm, [(x, 'row'), (g, 'full'), (w_in, 'full')],
                    [('row', D, BF16), ('row', 2 * D, F32), ('row', S5_W, F32), ('row', 2 * D, F32)])


def _tril():
    r = lax.broadcasted_iota(jnp.int32, (GM_CHUNK, GM_CHUNK), 0)
    c = lax.broadcasted_iota(jnp.int32, (GM_CHUNK, GM_CHUNK), 1)
    return r >= c


def _ln_stats(v):
    mu = jnp.mean(v, axis=-1, keepdims=True)
    vc = v - mu
    r = lax.rsqrt(jnp.mean(vc * vc, axis=-1, keepdims=True) + EPS)
    return vc * r, r


def _f_gmlp(zgm, ln_g, ln_b, w_s, b_s, tm):
    n = zgm.shape[0]

    def body(z_ref, lg_ref, lb_ref, ws_ref, bs_ref, y_ref):
        zg = _gelu(z_ref[...].astype(BF16))
        u = zg[:, :D]
        vh, _ = _ln_stats(zg[:, D:].astype(F32))
        vn = (vh * lg_ref[...] + lb_ref[...]).astype(BF16)
        keep = _tril()
        for g in range(GM_GROUPS):
            w = jnp.where(keep, ws_ref[g], 0.0).astype(BF16)
            cs = slice(g * LANE, (g + 1) * LANE)
            for c in range(tm // GM_CHUNK):
                rs = slice(c * GM_CHUNK, (c + 1) * GM_CHUNK)
                sv = _dot(w, vn[rs, cs]) + bs_ref[g]
                y_ref[rs, cs] = u[rs, cs] * sv.astype(BF16)

    return _rowcall("f_gmlp", body, n, tm,
                    [(zgm, 'row'), (ln_g, 'full'), (ln_b, 'full'), (w_s, 'full'), (b_s, 'full')],
                    [('row', D, BF16)])[0]


def _s5_specs(nb, nc, t, rev):
    def cc(c):
        return nc - 1 - c if rev else c
    slab = pl.BlockSpec((t, LANE), lambda j, b, c: (b * nc + cc(c), j))
    bb = pl.BlockSpec((None, LANE, 1024), lambda j, b, c: (j, 0, 0))
    cm = pl.BlockSpec((None, 1024, LANE), lambda j, b, c: (j, 0, 0))
    av = pl.BlockSpec((None, 1, 1024), lambda j, b, c: (j, 0, 0))
    dv = pl.BlockSpec((None, 1, LANE), lambda j, b, c: (j, 0, 0))
    st = pl.BlockSpec((None, None, None, 1, 1024), lambda j, b, c: (j, b, cc(c), 0, 0))
    return slab, bb, cm, av, dv, st


def _seg_load(ref, seg):
    return jnp.concatenate([ref[pl.ds(k, 8, stride=seg), :] for k in range(seg)], axis=0)


def _seg_store(ref, val, seg):
    for k in range(seg):
        ref[pl.ds(k, 8, stride=seg), :] = val[8 * k:8 * k + 8, :]


def _seg_scan(s_ref, ar, ai, seg, reverse):
    def step(i, carry):
        k = seg - 1 - i if reverse else i
        sr, si = carry
        nr = ar * sr - ai * si + s_ref[k, :, :512]
        ni = ar * si + ai * sr + s_ref[k, :, 512:]
        s_ref[k, :, :512] = nr
        s_ref[k, :, 512:] = ni
        return nr, ni

    zero = jnp.zeros((8, 512), F32)
    return lax.fori_loop(0, seg, step, (zero, zero), unroll=4)


def _seg_entries(tr, ti, pr, pi, cin, reverse):
    row = lax.broadcasted_iota(jnp.int32, (8, 512), 0)
    edge = row == (7 if reverse else 0)
    cr, ci = cin[:, :512], cin[:, 512:]
    xr = tr + jnp.where(edge, pr * cr - pi * ci, 0.0)
    xi = ti + jnp.where(edge, pr * ci + pi * cr, 0.0)
    ir, ii = _cscan(xr, xi, pr, pi, reverse)
    shift = 7 if reverse else 1
    er = jnp.where(edge, cr, pltpu.roll(ir, shift, 0))
    ei = jnp.where(edge, ci, pltpu.roll(ii, shift, 0))
    far = row == (0 if reverse else 7)
    out = jnp.concatenate([jnp.sum(jnp.where(far, ir, 0.0), axis=0, keepdims=True),
                           jnp.sum(jnp.where(far, ii, 0.0), axis=0, keepdims=True)], axis=1)
    return er, ei, out


def _seg_power(ar, ai, seg):
    pr, pi = ar, ai
    for _ in range(seg.bit_length() - 1):
        pr, pi = pr * pr - pi * pi, 2.0 * pr * pi
    return pr, pi


def _f_s5(zs5, bbc, ccm, avec, dvec, nb, t, comm):
    n = zs5.shape[0]
    nc = n // nb // t
    seg = t // 8
    slab, bb, cm, av, dv, st = _s5_specs(nb, nc, t, False)

    def body(u_ref, bb_ref, cc_ref, a_ref, d_ref, y_ref, st_ref, carry, s_ref):
        @pl.when(pl.program_id(2) == 0)
        def _():
            carry[...] = jnp.zeros_like(carry)
        cin = carry[...]
        st_ref[...] = cin
        up = _seg_load(u_ref, seg)
        s_ref[...] = _dot(up.astype(BF16), bb_ref[...]).reshape(seg, 8, 1024)
        a1r, a1i = a_ref[:, :512], a_ref[:, 512:]
        ar, ai = jnp.broadcast_to(a1r, (8, 512)), jnp.broadcast_to(a1i, (8, 512))
        tr, ti = _seg_scan(s_ref, ar, ai, seg, False)
        er, ei, cout = _seg_entries(tr, ti, *_seg_power(a1r, a1i, seg), cin, False)
        carry[...] = cout

        def apply(k, pe):
            pr, pi = pe
            pr, pi = ar * pr - ai * pi, ar * pi + ai * pr
            s_ref[k, :, :512] = s_ref[k, :, :512] + pr
            s_ref[k, :, 512:] = s_ref[k, :, 512:] + pi
            return pr, pi

        lax.fori_loop(0, seg, apply, (er, ei), unroll=4)
        s = s_ref[...].reshape(t, 1024).astype(BF16)
        _seg_store(y_ref, _dot(s, cc_ref[...]) + d_ref[...] * up, seg)

    return _hosting_call(
        body, "f_s5", (S5_SUPER, nb, nc), [slab, bb, cm, av, dv], [slab, st],
        [jax.ShapeDtypeStruct((n, S5_W), F32), jax.ShapeDtypeStruct((S5_SUPER, nb, nc, 1, 1024), F32)],
        [pltpu.VMEM((1, 1024), F32), pltpu.VMEM((seg, 8, 1024), F32)], ("arbitrary", "arbitrary", "arbitrary"),
        (zs5, bbc, ccm, avec, dvec), comm)


def _f_mix(ygm, ypre, zg, x, w_glu, w_br_gm, w_br_s5, w_mix_out, g_post, tm):
    n = x.shape[0]

    def body(ygm_ref, ypre_ref, zg_ref, x_ref, wglu_ref, wgm_ref, ws5_ref, wout_ref, g_ref,
             yg_ref, ys5_ref, a_ref, b_ref, mg_ref, mo_ref, x1_ref):
        yg = _gelu(ypre_ref[...])
        ygb = yg.astype(BF16)
        yg_ref[...] = ygb
        ys5 = (yg * _sig(_dot(ygb, wglu_ref[...]))).astype(BF16)
        ys5_ref[...] = ys5
        a = _dot(ygm_ref[...], wgm_ref[...])
        b = _dot(ys5, ws5_ref[...])
        a_ref[...] = a.astype(BF16)
        b_ref[...] = b.astype(BF16)
        zg = zg_ref[...]
        merged = (_sig(zg[:, :D]) * a + _sig(zg[:, D:]) * b).astype(BF16)
        mg_ref[...] = merged
        mo = _dot(merged, wout_ref[...])
        mo_ref[...] = mo
        x1_ref[...] = x_ref[...] + _rms_fwd(mo, g_ref[...])

    return _rowcall("f_mix", body, n, tm,
                    [(ygm, 'row'), (ypre, 'row'), (zg, 'row'), (x, 'row'), (w_glu, 'full'), (w_br_gm, 'full'),
                     (w_br_s5, 'full'), (w_mix_out, 'full'), (g_post, 'full')],
                    [('row', S5_W, BF16), ('row', S5_W, BF16), ('row', D, BF16), ('row', D, BF16),
                     ('row', D, BF16), ('row', D, F32), ('row', D, F32)])


def _f_mem(mem, g_mem, w_kv, tm):
    n = mem.shape[0]

    def body(m_ref, g_ref, w_ref, mn_ref, k_ref, v_ref):
        mn = _rms_fwd(m_ref[...], g_ref[...]).astype(BF16)
        mn_ref[...] = mn
        k_ref[...] = _dot(mn, w_ref[:, :D]).astype(BF16)
        v_ref[...] = _dot(mn, w_ref[:, D:]).astype(BF16)

    return _rowcall("f_mem", body, n, tm, [(mem, 'row'), (g_mem, 'full'), (w_kv, 'full')],
                    [('row', D, BF16), ('row', D, BF16), ('row', D, BF16)])


def _softmax(s):
    m = jnp.max(s, axis=-1, keepdims=True)
    e = jnp.exp(s - m)
    return e / jnp.sum(e, axis=-1, keepdims=True)


def _f_attn(x1, g_pre, w_q, k, v, w_o, g_post, tm, tpb, mlen):
    n = x1.shape[0]
    kv_spec = pl.BlockSpec((mlen, D), lambda i: (i // tpb, 0))
    scale = HEAD_DIM ** -0.5

    def body(x_ref, gp_ref, wq_ref, k_ref, v_ref, wo_ref, go_ref, hc_ref, o_ref, ao_ref, x2_ref):
        x1v = x_ref[...]
        hc = _rms_fwd(x1v, gp_ref[...]).astype(BF16)
        hc_ref[...] = hc
        q = _dot(hc, wq_ref[...])
        for h in range(HEADS):
            hs = slice(h * HEAD_DIM, (h + 1) * HEAD_DIM)
            p = _softmax(_dot_nt(q[:, hs].astype(BF16), k_ref[:, hs]) * scale)
            o_ref[:, hs] = _dot(p.astype(BF16), v_ref[:, hs]).astype(BF16)
        ao = _dot(o_ref[...], wo_ref[...])
        ao_ref[...] = ao
        x2_ref[...] = x1v + _rms_fwd(ao, go_ref[...])

    return _rowcall("f_attn", body, n, tm,
                    [(x1, 'row'), (g_pre, 'full'), (w_q, 'full'), (k, kv_spec), (v, kv_spec), (w_o, 'full'),
                     (g_post, 'full')],
                    [('row', D, BF16), ('row', D, BF16), ('row', D, F32), ('row', D, F32)])


def _f_ffn_up(x2, g_pre, w_gu, tm):
    n = x2.shape[0]

    def body(x_ref, g_ref, w_ref, hf_ref, gu_ref, act_ref):
        hf = _rms_fwd(x_ref[...], g_ref[...]).astype(BF16)
        hf_ref[...] = hf
        gg = _dot(hf, w_ref[:, :FFN_H])
        uu = _dot(hf, w_ref[:, FFN_H:])
        gu_ref[:, :FFN_H] = gg.astype(BF16)
        gu_ref[:, FFN_H:] = uu.astype(BF16)
        act_ref[...] = (gg * _sig(gg) * uu).astype(BF16)

    return _rowcall("f_ffn_up", body, n, tm, [(x2, 'row'), (g_pre, 'full'), (w_gu, 'full')],
                    [('row', D, BF16), ('row', 2 * FFN_H, BF16), ('row', FFN_H, BF16)])


def _f_ffn_down(act, x2, tgt, w_down, g_post, tm):
    n = x2.shape[0]

    def body(a_ref, x_ref, t_ref, w_ref, g_ref, dn_ref, d3_ref, loss_ref):
        @pl.when(pl.program_id(0) == 0)
        def _():
            loss_ref[...] = jnp.zeros_like(loss_ref)
        dn = _dot(a_ref[...], w_ref[...])
        dn_ref[...] = dn
        err = x_ref[...] + _rms_fwd(dn, g_ref[...]) - t_ref[...]
        d3_ref[...] = err * (1.0 / D)
        loss_ref[...] += jnp.sum(err * err, axis=0, keepdims=True)

    return _rowcall("f_ffn_down", body, n, tm,
                    [(act, 'row'), (x2, 'row'), (tgt, 'row'), (w_down, 'full'), (g_post, 'full')],
                    [('row', D, F32), ('row', D, F32), ('acc', (1, D), F32)])


def _b_ffn_down(d3, dn, gu, g_post, w_down, tm):
    n = d3.shape[0]

    def body(d3_ref, dn_ref, gu_ref, g_ref, w_ref, ddn_ref, dgu_ref, dg_ref):
        @pl.when(pl.program_id(0) == 0)
        def _():
            dg_ref[...] = jnp.zeros_like(dg_ref)
        ddn, dg = _rms_bwd(dn_ref[...], g_ref[...], d3_ref[...])
        dg_ref[...] += dg
        ddn = ddn.astype(BF16)
        ddn_ref[...] = ddn
        dact = _dot_nt(ddn, w_ref[...]).astype(BF16)
        gg = gu_ref[:, :FFN_H]
        uu = gu_ref[:, FFN_H:]
        sg = _sig(gg)
        silu = gg * sg
        dgu_ref[:, :FFN_H] = dact * uu * (sg + silu * (1.0 - sg))
        dgu_ref[:, FFN_H:] = dact * silu

    return _rowcall("b_ffn_down", body, n, tm,
                    [(d3, 'row'), (dn, 'row'), (gu, 'row'), (g_post, 'full'), (w_down, 'full')],
                    [('row', D, BF16), ('row', 2 * FFN_H, BF16), ('acc', (1, D), F32)])


def _b_ffn_up(dgu, d3, x2, w_gu, g_pre, tm):
    n = d3.shape[0]

    def body(dgu_ref, d3_ref, x_ref, w_ref, g_ref, dx_ref, dg_ref):
        @pl.when(pl.program_id(0) == 0)
        def _():
            dg_ref[...] = jnp.zeros_like(dg_ref)
        dhf = _dot_nt(dgu_ref[...], w_ref[...])
        dx, dg = _rms_bwd(x_ref[...], g_ref[...], dhf)
        dg_ref[...] += dg
        dx_ref[...] = d3_ref[...] + dx

    return _rowcall("b_ffn_up", body, n, tm,
                    [(dgu, 'row'), (d3, 'row'), (x2, 'row'), (w_gu, 'full'), (g_pre, 'full')],
                    [('row', D, F32), ('acc', (1, D), F32)])


def _b_attn(dx2, ao, x1, hc, g_post, w_o, w_q, g_pre, k, v, tm, tpb, mlen):
    n = dx2.shape[0]
    nb = n // (tm * tpb)
    kv_spec = pl.BlockSpec((mlen, D), lambda i: (i // tpb, 0))
    scale = HEAD_DIM ** -0.5

    def body(dx2_ref, ao_ref, x1_ref, hc_ref, go_ref, wo_ref, wq_ref, gp_ref, k_ref, v_ref,
             dao_ref, dq_ref, dx1_ref, dk_ref, dv_ref, dgo_ref, dgp_ref):
        i = pl.program_id(0)

        @pl.when(i == 0)
        def _():
            dgo_ref[...] = jnp.zeros_like(dgo_ref)
            dgp_ref[...] = jnp.zeros_like(dgp_ref)

        @pl.when(i % tpb == 0)
        def _():
            dk_ref[...] = jnp.zeros_like(dk_ref)
            dv_ref[...] = jnp.zeros_like(dv_ref)

        dx2v = dx2_ref[...]
        dao, dgo = _rms_bwd(ao_ref[...], go_ref[...], dx2v)
        dgo_ref[...] += dgo
        dao = dao.astype(BF16)
        dao_ref[...] = dao
        do = _dot_nt(dao, wo_ref[...])
        q = _dot(hc_ref[...], wq_ref[...])
        for h in range(HEADS):
            hs = slice(h * HEAD_DIM, (h + 1) * HEAD_DIM)
            qh = q[:, hs].astype(BF16)
            kh = k_ref[:, hs]
            p = _softmax(_dot_nt(qh, kh) * scale)
            doh = do[:, hs].astype(BF16)
            dp = _dot_nt(doh, v_ref[:, hs])
            ds = (p * (dp - jnp.sum(dp * p, axis=-1, keepdims=True)) * scale).astype(BF16)
            dq_ref[:, hs] = _dot(ds, kh).astype(BF16)
            dk_ref[:, hs] += _dot_tn(ds, qh)
            dv_ref[:, hs] += _dot_tn(p.astype(BF16), doh)
        dhc = _dot_nt(dq_ref[...], wq_ref[...])
        dx, dgp = _rms_bwd(x1_ref[...], gp_ref[...], dhc)
        dgp_ref[...] += dgp
        dx1_ref[...] = dx2v + dx

    return _rowcall("b_attn", body, n, tm,
                    [(dx2, 'row'), (ao, 'row'), (x1, 'row'), (hc, 'row'), (g_post, 'full'), (w_o, 'full'),
                     (w_q, 'full'), (g_pre, 'full'), (k, kv_spec), (v, kv_spec)],
                    [('row', D, BF16), ('row', D, BF16), ('row', D, F32),
                     (kv_spec, (nb * mlen, D), F32), (kv_spec, (nb * mlen, D), F32),
                     ('acc', (1, D), F32), ('acc', (1, D), F32)])


def _b_mem(dk, dv, mem, w_kv, g_mem, tm):
    n = mem.shape[0]

    def body(dk_ref, dv_ref, m_ref, w_ref, g_ref, dkv_ref, dg_ref):
        @pl.when(pl.program_id(0) == 0)
        def _():
            dg_ref[...] = jnp.zeros_like(dg_ref)
        dkb = dk_ref[...].astype(BF16)
        dvb = dv_ref[...].astype(BF16)
        dkv_ref[:, :D] = dkb
        dkv_ref[:, D:] = dvb
        dmn = _dot_nt(dkb, w_ref[:, :D]) + _dot_nt(dvb, w_ref[:, D:])
        _, dg = _rms_bwd(m_ref[...], g_ref[...], dmn)
        dg_ref[...] += dg

    return _rowcall("b_mem", body, n, tm, [(dk, 'row'), (dv, 'row'), (mem, 'row'), (w_kv, 'full'), (g_mem, 'full')],
                    [('row', 2 * D, BF16), ('acc', (1, D), F32)])


def _b_mix(dx1, mo, a, b, zg, ypre, g_post, w_mix_out, w_br_gm, w_br_s5, w_glu, tm):
    n = dx1.shape[0]

    def body(dx1_ref, mo_ref, a_ref, b_ref, zg_ref, ypre_ref, g_ref, wout_ref, wgm_ref, ws5_ref, wglu_ref,
             dmo_ref, da_ref, db_ref, dzg_ref, dygm_ref, dgate_ref, dypre_ref, dg_ref):
        @pl.when(pl.program_id(0) == 0)
        def _():
            dg_ref[...] = jnp.zeros_like(dg_ref)
        dmo, dg = _rms_bwd(mo_ref[...], g_ref[...], dx1_ref[...])
        dg_ref[...] += dg
        dmo = dmo.astype(BF16)
        dmo_ref[...] = dmo
        dmg = _dot_nt(dmo, wout_ref[...]).astype(BF16)
        zg = zg_ref[...].astype(BF16)
        sa, sb = _sig(zg[:, :D]), _sig(zg[:, D:])
        da = dmg * sa
        db = dmg * sb
        da_ref[...] = da
        db_ref[...] = db
        dzg_ref[:, :D] = dmg * a_ref[...] * (sa * (1.0 - sa))
        dzg_ref[:, D:] = dmg * b_ref[...] * (sb * (1.0 - sb))
        dygm_ref[...] = _dot_nt(da, wgm_ref[...])
        dys5 = _dot_nt(db, ws5_ref[...])
        ypre = ypre_ref[...]
        yg = _gelu(ypre)
        sgt = _sig(_dot(yg.astype(BF16), wglu_ref[...]))
        dgate = (dys5 * yg * sgt * (1.0 - sgt)).astype(BF16)
        dgate_ref[...] = dgate
        dyg = dys5 * sgt + _dot_nt(dgate, wglu_ref[...])
        dypre_ref[...] = dyg * _gelu_grad(ypre)

    return _rowcall("b_mix", body, n, tm,
                    [(dx1, 'row'), (mo, 'row'), (a, 'row'), (b, 'row'), (zg, 'row'), (ypre, 'row'), (g_post, 'full'),
                     (w_mix_out, 'full'), (w_br_gm, 'full'), (w_br_s5, 'full'), (w_glu, 'full')],
                    [('row', D, BF16), ('row', D, BF16), ('row', D, BF16), ('row', 2 * D, BF16), ('row', D, F32),
                     ('row', S5_W, BF16), ('row', S5_W, F32), ('acc', (1, D), F32)])


def _b_s5(zs5, dypre, states, bbc, ccm, avec, dvec, nb, t, comm):
    n = zs5.shape[0]
    nc = n // nb // t
    seg = t // 8
    slab, bb, cm, av, dv, st = _s5_specs(nb, nc, t, True)

    def body(u_ref, dy_ref, st_ref, bb_ref, cc_ref, a_ref, d_ref,
             du_ref, dbb_ref, dcc_ref, da_ref, dd_ref, lcarry, s_ref, l_ref):
        first = jnp.logical_and(pl.program_id(1) == 0, pl.program_id(2) == 0)

        @pl.when(first)
        def _():
            dbb_ref[...] = jnp.zeros_like(dbb_ref)
            dcc_ref[...] = jnp.zeros_like(dcc_ref)
            da_ref[...] = jnp.zeros_like(da_ref)
            dd_ref[...] = jnp.zeros_like(dd_ref)

        @pl.when(pl.program_id(2) == 0)
        def _():
            lcarry[...] = jnp.zeros_like(lcarry)

        up, dyp = _seg_load(u_ref, seg), _seg_load(dy_ref, seg)
        ub, dyb = up.astype(BF16), dyp.astype(BF16)
        a1r, a1i = a_ref[:, :512], a_ref[:, 512:]
        ar, ai = jnp.broadcast_to(a1r, (8, 512)), jnp.broadcast_to(a1i, (8, 512))
        pr, pi = _seg_power(a1r, a1i, seg)

        s_ref[...] = _dot(ub, bb_ref[...]).reshape(seg, 8, 1024)
        tr, ti = _seg_scan(s_ref, ar, ai, seg, False)
        er, ei, _ = _seg_entries(tr, ti, pr, pi, st_ref[...], False)

        def apply(k, pe):
            qr, qi = pe
            qr, qi = ar * qr - ai * qi, ar * qi + ai * qr
            s_ref[k, :, :512] = s_ref[k, :, :512] + qr
            s_ref[k, :, 512:] = s_ref[k, :, 512:] + qi
            return qr, qi

        lax.fori_loop(0, seg, apply, (er, ei), unroll=4)

        l_ref[...] = _dot_nt(dyb, cc_ref[...]).reshape(seg, 8, 1024)
        tr, ti = _seg_scan(l_ref, ar, -ai, seg, True)
        fr, fi, lout = _seg_entries(tr, ti, pr, -pi, lcarry[...], True)
        lcarry[...] = lout

        def apply_back(i, carry):
            qr, qi, accr, acci = carry
            k = seg - 1 - i
            qr, qi = ar * qr + ai * qi, ar * qi - ai * qr
            lr = l_ref[k, :, :512] + qr
            li = l_ref[k, :, 512:] + qi
            l_ref[k, :, :512] = lr
            l_ref[k, :, 512:] = li
            kp = jnp.maximum(k - 1, 0)
            sr = jnp.where(k == 0, er, s_ref[kp, :, :512])
            si = jnp.where(k == 0, ei, s_ref[kp, :, 512:])
            return qr, qi, accr + lr * sr + li * si, acci + li * sr - lr * si

        zero = jnp.zeros((8, 512), F32)
        _, _, accr, acci = lax.fori_loop(0, seg, apply_back, (fr, fi, zero, zero), unroll=4)
        da_ref[:, :512] += jnp.sum(accr, axis=0, keepdims=True)
        da_ref[:, 512:] += jnp.sum(acci, axis=0, keepdims=True)

        s = s_ref[...].reshape(t, 1024).astype(BF16)
        lam = l_ref[...].reshape(t, 1024).astype(BF16)
        dcc_ref[...] += _dot_tn(dyb, s)
        dbb_ref[...] += _dot_tn(ub, lam)
        _seg_store(du_ref, _dot_nt(lam, bb_ref[...]) + d_ref[...] * dyp, seg)
        dd_ref[...] += jnp.sum(dyp * up, axis=0, keepdims=True)

    return _hosting_call(
        body, "b_s5", (S5_SUPER, nb, nc), [slab, slab, st, bb, cm, av, dv], [slab, bb, bb, av, dv],
        [jax.ShapeDtypeStruct((n, S5_W), F32), jax.ShapeDtypeStruct(bbc.shape, F32),
         jax.ShapeDtypeStruct(bbc.shape, F32), jax.ShapeDtypeStruct(avec.shape, F32),
         jax.ShapeDtypeStruct(dvec.shape, F32)],
        [pltpu.VMEM((1, 1024), F32), pltpu.VMEM((seg, 8, 1024), F32), pltpu.VMEM((seg, 8, 1024), F32)],
        ("arbitrary", "arbitrary", "arbitrary"), (zs5, dypre, states, bbc, ccm, avec, dvec), comm)


def _b_gmlp(zgm, dygm, ln_g, ln_b, w_s, b_s, tm, comm):
    n = zgm.shape[0]

    def body(z_ref, dy_ref, lg_ref, lb_ref, ws_ref, bs_ref, dz_ref, dws_ref, dbs_ref, dlg_ref, dlb_ref,
             du_s, dvn_s):
        @pl.when(pl.program_id(0) == 0)
        def _():
            dws_ref[...] = jnp.zeros_like(dws_ref)
            dbs_ref[...] = jnp.zeros_like(dbs_ref)
            dlg_ref[...] = jnp.zeros_like(dlg_ref)
            dlb_ref[...] = jnp.zeros_like(dlb_ref)
        z = z_ref[...].astype(BF16)
        zg = _gelu(z)
        u = zg[:, :D]
        vh, r = _ln_stats(zg[:, D:].astype(F32))
        vn = (vh * lg_ref[...] + lb_ref[...]).astype(BF16)
        dy = dy_ref[...]
        keep = _tril()
        for g in range(GM_GROUPS):
            w = jnp.where(keep, ws_ref[g], 0.0).astype(BF16)
            cs = slice(g * LANE, (g + 1) * LANE)
            for c in range(tm // GM_CHUNK):
                rs = slice(c * GM_CHUNK, (c + 1) * GM_CHUNK)
                vb = vn[rs, cs]
                sv = _dot(w, vb) + bs_ref[g]
                dyb = dy[rs, cs]
                du_s[rs, cs] = dyb * sv
                dsv = dyb * u[rs, cs]
                dsvb = dsv.astype(BF16)
                dvn_s[rs, cs] = _dot_tn(w, dsvb)
                dws_ref[g] += jnp.where(keep, _dot_nt(dsvb, vb), 0.0)
                dbs_ref[g] += jnp.sum(dsv, axis=1, keepdims=True)
        dvn = dvn_s[...]
        dlg_ref[...] += jnp.sum(dvn * vh, axis=0, keepdims=True)
        dlb_ref[...] += jnp.sum(dvn, axis=0, keepdims=True)
        dvh = dvn * lg_ref[...]
        dv = r * (dvh - jnp.mean(dvh, axis=-1, keepdims=True) - vh * jnp.mean(dvh * vh, axis=-1, keepdims=True))
        dz_ref[:, :D] = du_s[...].astype(BF16) * _gelu_grad(z[:, :D])
        dz_ref[:, D:] = dv.astype(BF16) * _gelu_grad(z[:, D:])

    return _rowcall("b_gmlp", body, n, tm,
                    [(zgm, 'row'), (dygm, 'row'), (ln_g, 'full'), (ln_b, 'full'), (w_s, 'full'), (b_s, 'full')],
                    [('row', 2 * D, BF16), ('acc', w_s.shape, F32), ('acc', b_s.shape, F32), ('acc', (1, D), F32),
                     ('acc', (1, D), F32)],
                    scratch=[pltpu.VMEM((tm, D), F32), pltpu.VMEM((tm, D), F32)], comm=comm)


def _b_in(dzgm, dzs5, dzg, dx1, x, w_in, g_pre, tm, comm):
    n = x.shape[0]

    def body(d1_ref, d2_ref, d3_ref, dx1_ref, x_ref, w_ref, g_ref, gx_ref, dg_ref):
        @pl.when(pl.program_id(0) == 0)
        def _():
            dg_ref[...] = jnp.zeros_like(dg_ref)
        dh = (_dot_nt(d1_ref[...], w_ref[:, 0:2 * D]) + _dot_nt(d2_ref[...].astype(BF16), w_ref[:, 2 * D:2 * D + S5_W])
              + _dot_nt(d3_ref[...], w_ref[:, 2 * D + S5_W:]))
        dx, dg = _rms_bwd(x_ref[...], g_ref[...], dh)
        dg_ref[...] += dg
        gx_ref[...] = dx1_ref[...] + dx

    return _rowcall("b_in", body, n, tm,
                    [(dzgm, 'row'), (dzs5, 'row'), (dzg, 'row'), (dx1, 'row'), (x, 'row'), (w_in, 'full'),
                     (g_pre, 'full')],
                    [('row', D, F32), ('acc', (1, D), F32)], comm=comm)


def _whole(name, body, ins, outs):
    return pl.pallas_call(body, name=name, out_shape=[jax.ShapeDtypeStruct(s, dt) for s, dt in outs],
                          compiler_params=_params())(*ins)


def _s5_disc_fwd(lr, li, ls, br, bi):
    def body(lr_ref, li_ref, ls_ref, br_ref, bi_ref, o1, o2, o3, o4):
        outs = _s5_disc(lr_ref[...], li_ref[...], ls_ref[...], br_ref[...], bi_ref[...])
        for o, val in zip((o1, o2, o3, o4), outs):
            o[...] = val

    return _whole("s5_disc_fwd", body, [lr, li, ls, br, bi],
                  [(lr.shape, F32), (lr.shape, F32), (br.shape, F32), (br.shape, F32)])


def _s5_disc_bwd(lr, li, ls, br, bi, cts):
    def body(lr_ref, li_ref, ls_ref, br_ref, bi_ref, c1, c2, c3, c4, o1, o2, o3, o4, o5):
        _, vjp = jax.vjp(_s5_disc, lr_ref[...], li_ref[...], ls_ref[...], br_ref[...], bi_ref[...])
        grads = vjp((c1[...], c2[...], c3[...], c4[...]))
        for o, val in zip((o1, o2, o3, o4, o5), grads):
            o[...] = val

    return _whole("s5_disc_bwd", body, [lr, li, ls, br, bi, *cts],
                  [(lr.shape, F32), (lr.shape, F32), (lr.shape, F32), (br.shape, F32), (br.shape, F32)])


def _sum_slots(name, slots):
    def body(s_ref, o_ref):
        acc = s_ref[0]
        for k in range(1, slots.shape[0]):
            acc = acc + s_ref[k]
        o_ref[...] = acc

    return _whole(name, body, [slots], [(slots.shape[1:], F32)])[0]


def _adamw_math(w, g, m, v):
    c1 = 1.0 - ADAM_B1 ** ADAM_STEP
    c2 = 1.0 - ADAM_B2 ** ADAM_STEP
    mn = ADAM_B1 * m + (1.0 - ADAM_B1) * g
    vn = ADAM_B2 * v + (1.0 - ADAM_B2) * (g * g)
    return -ADAM_LR * ((mn / c1) / (jnp.sqrt(vn / c2) + ADAM_EPS) + ADAM_WD * w), mn, vn


def _adamw_many(name, ws, gs, ms, vs):
    k = len(ws)

    def body(*refs):
        for i in range(k):
            dl, mn, vn = _adamw_math(refs[i][...], refs[k + i][...], refs[2 * k + i][...], refs[3 * k + i][...])
            refs[4 * k + i][...] = dl
            refs[5 * k + i][...] = mn
            refs[6 * k + i][...] = vn

    res = _whole(name, body, [*ws, *gs, *ms, *vs], [(t.shape, F32) for t in ws] * 3)
    return res[:k], res[k:2 * k], res[2 * k:]


def _adamw(name, w, g, m, v, tm):
    def body(w_ref, g_ref, m_ref, v_ref, d_ref, mo_ref, vo_ref):
        d_ref[...], mo_ref[...], vo_ref[...] = _adamw_math(w_ref[...], g_ref[...], m_ref[...], v_ref[...])

    cols = w.shape[1]
    return _rowcall(name, body, w.shape[0], tm, [(w, 'row'), (g, 'row'), (m, 'row'), (v, 'row')],
                    [('row', cols, F32), ('row', cols, F32), ('row', cols, F32)])


def _picked_rowcall(name, body, grid, in_specs, out_cols, tm):
    return pl.pallas_call(
        lambda p_ref, *refs: body(*refs), name=name,
        grid_spec=pltpu.PrefetchScalarGridSpec(
            num_scalar_prefetch=1, grid=(grid,), in_specs=in_specs,
            out_specs=[pl.BlockSpec((tm, D), lambda i, p: (i, 0)) for _ in out_cols]),
        out_shape=[jax.ShapeDtypeStruct((grid * tm, D), dt) for dt in out_cols],
        compiler_params=_params(("arbitrary",)),
    )


def _exchange(copies):
    def build(src_refs, out_refs, ss, rs):
        def rdma(k, src, dst, peer):
            return _Lazy(lambda: pltpu.make_async_remote_copy(
                src_ref=src, dst_ref=dst, send_sem=ss.at[k], recv_sem=rs.at[k], device_id=peer, device_id_type=MESH))
        return copies(src_refs, out_refs, rdma)

    def start(*refs):
        for cp in build(*refs):
            cp.start()

    def finish(*refs):
        for cp in build(*refs):
            cp.wait()

    return start, finish


def _reduce_big(tag, slabs):
    rows = [s.shape[1] for s in slabs]
    blk = sum(rows) // 4
    tm = _tile_rows(blk, 512)
    per = blk // tm
    gpack = jnp.concatenate([s.reshape(4, 2, 2, r // 4, D) for s, r in zip(slabs, rows)], axis=3)

    mx, my, mc = lax.axis_index("x"), lax.axis_index("y"), lax.axis_index("c")
    chip_me, chip_xn, chip_yn = 2 * mx + my, 2 * (1 - mx) + my, 2 * mx + (1 - my)

    def pair_copies(srcs, dsts, rdma):
        c, _, (_, _, psib) = _place()
        return [rdma(0, srcs[0].at[:, 1 - c], dsts[0], psib)]

    from_sib = (yield _Comm([gpack], [((4, 2, blk, D), F32)], 1, _exchange(pair_copies), {}, (0.0, 1.0)))[0]

    def sum_both(a_ref, b_ref, o32_ref, o16_ref):
        s = a_ref[...].astype(F32) + b_ref[...].astype(F32)
        o32_ref[...] = s
        o16_ref[...] = s.astype(BF16)

    p32, p16 = _picked_rowcall(
        "sum_pair_" + tag, sum_both, 8 * per,
        [pl.BlockSpec((None, None, None, tm, D), lambda i, p: (i // (2 * per), p[0], (i // per) % 2, i % per, 0)),
         pl.BlockSpec((tm, D), lambda i, p: (i, 0))],
        [F32, BF16], tm)(jnp.stack([mc]).astype(jnp.int32), gpack, from_sib.reshape(8 * blk, D))
    p32, p16 = p32.reshape(4, 2, blk, D), p16.reshape(4, 2, blk, D)

    def step1_copies(srcs, dsts, rdma):
        _, (me, xn, yn, dg), (px, py, _) = _place()
        s16 = srcs[0]
        return [rdma(0, s16.at[yn, 0], dsts[0].at[0], py), rdma(1, s16.at[dg, 0], dsts[0].at[1], py),
                rdma(2, s16.at[xn, 1], dsts[0].at[2], px), rdma(3, s16.at[dg, 1], dsts[0].at[3], px)]

    recv1 = (yield _Comm([p16], [((4, blk, D), BF16)], 4, _exchange(step1_copies), {}, (0.0, 1.0)))[0]

    s32, s16 = _picked_rowcall(
        "sum_step1_" + tag, sum_both, 4 * per,
        [pl.BlockSpec((None, None, tm, D), lambda i, p: (p[i // per], i // (2 * per), i % per, 0)),
         pl.BlockSpec((tm, D), lambda i, p: (i, 0))],
        [F32, BF16], tm)(jnp.stack([chip_me, chip_xn, chip_me, chip_yn]).astype(jnp.int32), p32,
                         recv1.reshape(4 * blk, D))

    def step2_copies(srcs, dsts, rdma):
        _, _, (px, py, _) = _place()
        return [rdma(0, srcs[0].at[1], dsts[0].at[0], px), rdma(1, srcs[0].at[3], dsts[0].at[1], py)]

    recv2 = (yield _Comm([s16.reshape(4, blk, D)], [((2, blk, D), BF16)], 2, _exchange(step2_copies), {},
                         (0.0, 1.0)))[0]

    def sum_step2(a_ref, b_ref, o_ref):
        o_ref[...] = a_ref[...] + b_ref[...].astype(F32)

    red = _rowcall("sum_step2_" + tag, sum_step2, 2 * blk, tm,
                   [(s32.reshape(4, blk, D), pl.BlockSpec((None, tm, D), lambda i: (2 * (i // per), i % per, 0))),
                    (recv2.reshape(2 * blk, D), 'row')],
                   [('row', D, F32)])[0].reshape(2, blk, D)

    def share_copies(srcs, dsts, rdma):
        _, _, (_, _, psib) = _place()
        return [rdma(0, srcs[0], dsts[0], psib)]

    other = (yield _Comm([red], [((2, blk, D), F32)], 1, _exchange(share_copies), {}, (0.0, 1.0)))[0]
    lo = jnp.where(mc == 0, red, other)
    hi = jnp.where(mc == 0, other, red)
    out, off = [], 0
    for r in rows:
        q = r // 4
        out.append(jnp.concatenate([lo[0, off:off + q], lo[1, off:off + q], hi[0, off:off + q], hi[1, off:off + q]]))
        off += q
    return out


def _rows1024(a):
    flat = a.reshape(-1)
    pad = (-flat.shape[0]) % D
    if pad:
        flat = jnp.concatenate([flat, jnp.zeros((pad,), flat.dtype)])
    return flat.reshape(-1, D)


def _pack(arrs, pad_rows_to=8):
    parts = [_rows1024(a) for a in arrs]
    rows = sum(p.shape[0] for p in parts)
    pad = (-rows) % pad_rows_to
    if pad:
        parts.append(jnp.zeros((pad, D), parts[0].dtype))
    return jnp.concatenate(parts, axis=0)


def _unpack(packed, shapes):
    out, r = [], 0
    for s in shapes:
        size = math.prod(s)
        nr = -(-size // D)
        out.append(packed[r:r + nr].reshape(-1)[:size].reshape(s))
        r += nr
    return out


def _block_diag(t):
    eye = jnp.eye(8, dtype=t.dtype)
    j, g, a, b = t.shape
    return (t[:, :, :, None, :] * eye[None, :, None, :, None]).reshape(j, g * a, g * b)


def _block_diag_take(m, a, b):
    eye = jnp.eye(8, dtype=m.dtype)
    return (m.reshape(4, 8, a, 8, b) * eye[None, :, None, :, None]).sum(axis=3)


def kernel(x, mem, g_mix_pre, w_in, gm_ln_g, gm_ln_b, gm_w_s, gm_b_s, s5_lam_re, s5_lam_im, s5_log_step, s5_b_re, s5_b_im, s5_c_re, s5_c_im, s5_d, s5_w_glu, w_br_gm, w_br_s5, w_mix_out, g_mix_post, g_ca_pre, g_mem, ca_w_q, ca_w_kv, ca_w_o, g_ca_post, g_ffn_pre, ffn_w_gu, ffn_w_down, g_ffn_post, loss_target, m_g_mix_pre, m_w_in, m_gm_ln_g, m_gm_ln_b, m_gm_w_s, m_gm_b_s, m_s5_lam_re, m_s5_lam_im, m_s5_log_step, m_s5_b_re, m_s5_b_im, m_s5_c_re, m_s5_c_im, m_s5_d, m_s5_w_glu, m_w_br_gm, m_w_br_s5, m_w_mix_out, m_g_mix_post, m_g_ca_pre, m_g_mem, m_ca_w_q, m_ca_w_kv, m_ca_w_o, m_g_ca_post, m_g_ffn_pre, m_ffn_w_gu, m_ffn_w_down, m_g_ffn_post, v_g_mix_pre, v_w_in, v_gm_ln_g, v_gm_ln_b, v_gm_w_s, v_gm_b_s, v_s5_lam_re, v_s5_lam_im, v_s5_log_step, v_s5_b_re, v_s5_b_im, v_s5_c_re, v_s5_c_im, v_s5_d, v_s5_w_glu, v_w_br_gm, v_w_br_s5, v_w_mix_out, v_g_mix_post, v_g_ca_pre, v_g_mem, v_ca_w_q, v_ca_w_kv, v_ca_w_o, v_g_ca_post, v_g_ffn_pre, v_ffn_w_gu, v_ffn_w_down, v_g_ffn_post):
    a = dict(locals())
    w = {n: a[n][0] for n in WNAMES}
    nb, seq, _ = x.shape
    n = nb * seq
    mlen = mem.shape[1]
    tm = min(256, seq)
    tmb = min(512, seq)
    tpb = seq // tmb
    xf = x.reshape(n, D)
    tgt = loss_target.reshape(n, D)
    memf = mem.reshape(nb * mlen, D)
    tmm = min(256, mlen)

    big_names = list(BIG)
    local_shapes = {k: w[k].shape for k in big_names}
    w_in_full = _run_comm("gather_w_in", _gather_weights("cast_w_in", [w['w_in']], [BIG['w_in']]))[0]
    rest = [k for k in big_names if k != 'w_in']
    gather_rest = _gather_weights("cast_w_rest", [w[k] for k in rest], [BIG[k] for k in rest])

    vec = lambda name: w[name].reshape(1, D)

    to_lane = lambda p: p.reshape(1, -1)
    b_t = lambda p: p.transpose(2, 0, 1).reshape(16, -1)
    lr_l, li_l = to_lane(w['s5_lam_re']), to_lane(w['s5_lam_im'])
    ls_l = jnp.repeat(w['s5_log_step'], 64).reshape(1, -1)
    br_t, bi_t = b_t(w['s5_b_re']), b_t(w['s5_b_im'])
    ab_re, ab_im, bb_re, bb_im = _s5_disc_fwd(lr_l, li_l, ls_l, br_t, bi_t)
    blk = lambda t: _block_diag(t.reshape(16, 4, 8, 64).transpose(1, 2, 0, 3))
    bbc = jnp.concatenate([blk(bb_re), blk(bb_im)], axis=2).astype(BF16)
    cblk = lambda c: _block_diag(c.reshape(4, 8, 16, 64).transpose(0, 1, 3, 2))
    ccm = jnp.concatenate([cblk(w['s5_c_re']), -cblk(w['s5_c_im'])], axis=1).astype(BF16)
    avec = jnp.concatenate([ab_re.reshape(4, 1, 512), ab_im.reshape(4, 1, 512)], axis=2)
    dvec = w['s5_d'].reshape(4, 1, LANE)

    bs3 = w['gm_b_s'].reshape(GM_GROUPS, GM_CHUNK, 1)
    wf = {'w_in': w_in_full}
    h, zgm, zs5, zg = _f_in(xf, vec('g_mix_pre'), wf['w_in'], tmb)
    ygm = _f_gmlp(zgm, vec('gm_ln_g'), vec('gm_ln_b'), w['gm_w_s'], bs3, tmb)
    s5_t = min(S5_T, seq)
    (ypre, states), gathered = _f_s5(zs5, bbc, ccm, avec, dvec, nb, s5_t, gather_rest)
    for k, g in zip(rest, gathered):
        wf[k] = g.reshape(-1, g.shape[-1]) if BIG[k] == 0 else g
    yg, ys5, a_br, b_br, merged, mo, x1 = _f_mix(ygm, ypre, zg, xf, wf['s5_w_glu'], wf['w_br_gm'], wf['w_br_s5'],
                                                 wf['w_mix_out'], vec('g_mix_post'), tmb)
    mem_n, kk, vv = _f_mem(memf, vec('g_mem'), wf['ca_w_kv'], tmm)
    hc, o_att, ao, x2 = _f_attn(x1, vec('g_ca_pre'), wf['ca_w_q'], kk, vv, wf['ca_w_o'], vec('g_ca_post'),
                                tmb, tpb, mlen)
    hf, gu, act = _f_ffn_up(x2, vec('g_ffn_pre'), wf['ffn_w_gu'], tm)
    dn, d3, loss_cols = _f_ffn_down(act, x2, tgt, wf['ffn_w_down'], vec('g_ffn_post'), tmb)

    gsm = {}
    gbig = {}
    ddn, dgu, gsm['g_ffn_post'] = _b_ffn_down(d3, dn, gu, vec('g_ffn_post'), wf['ffn_w_down'], tm)
    dx2, gsm['g_ffn_pre'] = _b_ffn_up(dgu, d3, x2, wf['ffn_w_gu'], vec('g_ffn_pre'), tmb)
    gbig['ffn_w_down'] = _mm_tn("dw_ffn_down", act, ddn)
    gbig['ffn_w_gu'] = _mm_tn("dw_ffn_gu", dgu, hf)
    dao, dq, dx1, dk, dv, gsm['g_ca_post'], gsm['g_ca_pre'] = _b_attn(
        dx2, ao, x1, hc, vec('g_ca_post'), wf['ca_w_o'], wf['ca_w_q'], vec('g_ca_pre'), kk, vv, tmb, tpb, mlen)
    gbig['ca_w_o'] = _mm_tn("dw_ca_o", o_att, dao)
    gbig['ca_w_q'] = _mm_tn("dw_ca_q", hc, dq)
    dkv, gsm['g_mem'] = _b_mem(dk, dv, memf, wf['ca_w_kv'], vec('g_mem'), tmm)
    gbig['ca_w_kv'] = _mm_tn("dw_ca_kv", dkv, mem_n)
    dmo, da_br, db_br, dzg, dygm, dgate, dypre, gsm['g_mix_post'] = _b_mix(
        dx1, mo, a_br, b_br, zg, ypre, vec('g_mix_post'), wf['w_mix_out'], wf['w_br_gm'], wf['w_br_s5'],
        wf['s5_w_glu'], tm)
    gbig['w_mix_out'] = _mm_tn("dw_mix_out", merged, dmo)
    gbig['w_br_gm'] = _mm_tn("dw_br_gm", ygm, da_br)
    gbig['w_br_s5'] = _mm_tn("dw_br_s5", db_br, ys5)
    gbig['s5_w_glu'] = _mm_tn("dw_s5_glu", yg, dgate)
    slab_of = lambda k: gbig[k].reshape(4, -1, D)
    red_rest = _reduce_big("rest", [slab_of(k) for k in rest])
    (dzs5, dbbc, dccm_t, davec, ddvec), got = _b_s5(zs5, dypre, states, bbc, ccm, avec, dvec, nb, s5_t,
                                                    next(red_rest))
    dccm = dccm_t.transpose(0, 2, 1)
    (dzgm, gsm['gm_w_s'], dbs3, gsm['gm_ln_g'], gsm['gm_ln_b']), got = _b_gmlp(
        zgm, dygm, vec('gm_ln_g'), vec('gm_ln_b'), w['gm_w_s'], bs3, tmb, red_rest.send(got))
    gsm['gm_b_s'] = dbs3
    grad_x, gsm['g_mix_pre'] = _b_in(dzgm, dzs5, dzg, dx1, xf, wf['w_in'], vec('g_mix_pre'), tmb, None)
    dw_in_gm, got = _mm_tn("dw_in_gm", dzgm, h, comm=red_rest.send(got))
    dw_in_s5, got = _mm_tn("dw_in_s5", dzs5, h, comm=red_rest.send(got))
    greds = dict(zip(rest, _finish(red_rest, got)))

    unblk = lambda m_: _block_diag_take(m_, 16, 64).transpose(2, 0, 1, 3).reshape(16, -1)
    d_bb_re, d_bb_im = unblk(dbbc[:, :, :512]), unblk(dbbc[:, :, 512:])
    cunblk = lambda m_: _block_diag_take(m_, 64, 16).transpose(0, 1, 3, 2).reshape(32, 16, 64)
    gsm['s5_c_re'] = cunblk(dccm[:, :512, :])
    gsm['s5_c_im'] = -cunblk(dccm[:, 512:, :])
    d_ab_re, d_ab_im = davec[:, :, :512].reshape(1, -1), davec[:, :, 512:].reshape(1, -1)
    g_lr, g_li, g_ls, g_br, g_bi = _s5_disc_bwd(lr_l, li_l, ls_l, br_t, bi_t, (d_ab_re, d_ab_im, d_bb_re, d_bb_im))
    gsm['s5_lam_re'], gsm['s5_lam_im'] = g_lr.reshape(32, 64), g_li.reshape(32, 64)
    gsm['s5_log_step'] = g_ls.reshape(32, 64).sum(axis=1)
    from_t = lambda t: t.reshape(16, 32, 64).transpose(1, 2, 0)
    gsm['s5_b_re'], gsm['s5_b_im'] = from_t(g_br), from_t(g_bi)
    gsm['s5_d'] = ddvec.reshape(32, 16)

    small_shapes = [w[k].shape for k in SMALL]
    spack = _pack([gsm[k].reshape(w[k].shape) for k in SMALL] + [loss_cols], 8)
    me_slot = 4 * lax.axis_index("x") + 2 * lax.axis_index("y") + lax.axis_index("c")
    dw_in_g, (sall,) = _mm_tn("dw_in_g", dzg, h, comm=_gather_all(spack))
    gbig['w_in'] = jnp.concatenate([dw_in_gm, dw_in_s5, dw_in_g], axis=0)
    sall = lax.dynamic_update_index_in_dim(sall, spack, me_slot, 0)
    ssum = _sum_slots("sum_small", sall)
    small_red = _unpack(ssum, small_shapes + [(1, D)])
    loss = 0.5 * jnp.sum(small_red[-1]) / D

    red_in = _reduce_big("w_in", [slab_of('w_in')])
    got = _run_comm("reduce_pair_w_in", next(red_in))
    for phase in ("reduce_step1_w_in", "reduce_step2_w_in"):
        got = _run_comm(phase, red_in.send(got))
    got = _run_comm("share_pair_w_in", red_in.send(got))
    greds['w_in'] = _finish(red_in, got)[0]

    res_big = {}
    for k in big_names:
        r_, c_ = local_shapes[k]
        g = greds[k].reshape(r_, c_) if BIG[k] == 0 else greds[k].reshape(c_, r_).T
        dl, mn, vn = _adamw("adamw_" + k, w[k], g, a['m_' + k][0], a['v_' + k][0], _tile_rows(g.shape[0], 256))
        res_big[k] = (g, dl, mn, vn)
    as2d = lambda t: t.reshape(1, -1) if t.ndim == 1 else t
    dl_s, mn_s, vn_s = _adamw_many("adamw_small", [as2d(w[k]) for k in SMALL], [as2d(g) for g in small_red[:-1]],
                                   [as2d(a['m_' + k][0]) for k in SMALL], [as2d(a['v_' + k][0]) for k in SMALL])
    res_small = {k: (g, dl.reshape(w[k].shape), mn.reshape(w[k].shape), vn.reshape(w[k].shape))
                 for k, g, dl, mn, vn in zip(SMALL, small_red[:-1], dl_s, mn_s, vn_s)}

    res = {**res_big, **res_small}
    outs = [loss, grad_x.reshape(nb, seq, D)]
    for i in range(4):
        outs += [res[k][i][None] for k in WNAMES]
    return tuple(outs)


def _finish(gen, got):
    try:
        gen.send(got)
    except StopIteration as done:
        return done.value
    raise AssertionError("the generator has more exchanges")


def _tile_rows(rows, cap=384):
    best = rows
    for t in range(16, min(rows, cap) + 1, 16):
        if rows % t == 0:
            best = t
    return best
```

```python
import math
from typing import Any, NamedTuple

import jax
import jax.numpy as jnp
from jax import lax
from jax.experimental import pallas as pl
from jax.experimental.pallas import tpu as pltpu

F32 = jnp.float32
BF16 = jnp.bfloat16
EPS = 1e-6
D = 1024
GM_CHUNK = 128
GM_GROUPS = 8
S5_W = 512
S5_SUPER = 4
S5_T = 512
HEADS = 4
HEAD_DIM = 256
FFN_H = 2816
LANE = 128
VMEM_LIMIT = 56 * 1024 * 1024
MESH = pl.DeviceIdType.MESH

ADAM_LR, ADAM_B1, ADAM_B2, ADAM_EPS, ADAM_WD, ADAM_STEP = 0.001, 0.9, 0.999, 1e-08, 0.01, 10

WNAMES = ['g_mix_pre', 'w_in', 'gm_ln_g', 'gm_ln_b', 'gm_w_s', 'gm_b_s', 's5_lam_re', 's5_lam_im', 's5_log_step',
          's5_b_re', 's5_b_im', 's5_c_re', 's5_c_im', 's5_d', 's5_w_glu', 'w_br_gm', 'w_br_s5', 'w_mix_out',
          'g_mix_post', 'g_ca_pre', 'g_mem', 'ca_w_q', 'ca_w_kv', 'ca_w_o', 'g_ca_post', 'g_ffn_pre', 'ffn_w_gu',
          'ffn_w_down', 'g_ffn_post']
BIG = {'w_in': 1, 's5_w_glu': 0, 'w_br_gm': 0, 'w_br_s5': 1, 'w_mix_out': 0, 'ca_w_q': 0, 'ca_w_kv': 1,
       'ca_w_o': 0, 'ffn_w_gu': 1, 'ffn_w_down': 0}
SMALL = [n for n in WNAMES if n not in BIG]


def _dot(a, b):
    return jnp.dot(a, b, preferred_element_type=F32)


def _dot_nt(a, b):
    return lax.dot_general(a, b, (((1,), (1,)), ((), ())), preferred_element_type=F32)


def _dot_tn(a, b):
    return lax.dot_general(a, b, (((0,), (0,)), ((), ())), preferred_element_type=F32)


def _rms_fwd(x, g):
    r = lax.rsqrt(jnp.mean(x * x, axis=-1, keepdims=True) + EPS)
    return x * r * g


def _rms_bwd(x, g, dy):
    r = lax.rsqrt(jnp.mean(x * x, axis=-1, keepdims=True) + EPS)
    xh = x * r
    gdy = dy * g
    dx = r * (gdy - xh * jnp.mean(gdy * xh, axis=-1, keepdims=True))
    dg = jnp.sum(dy * xh, axis=0, keepdims=True)
    return dx, dg


_GC = math.sqrt(2.0 / math.pi)


def _gelu(x):
    return 0.5 * x * (1.0 + jnp.tanh(_GC * (x + 0.044715 * x * x * x)))


def _gelu_grad(x):
    t = jnp.tanh(_GC * (x + 0.044715 * x * x * x))
    return 0.5 * (1.0 + t) + 0.5 * x * (1.0 - t * t) * _GC * (1.0 + 3 * 0.044715 * x * x)


def _sig(x):
    return 1.0 / (1.0 + jnp.exp(-x))


def _cscan(br, bi, ar, ai, reverse):
    t = br.shape[0]
    row = lax.broadcasted_iota(jnp.int32, br.shape, 0)
    pr, pi = ar, ai
    sh = 1
    while sh < t:
        if reverse:
            keep = row < t - sh
            rr, ri = pltpu.roll(br, t - sh, 0), pltpu.roll(bi, t - sh, 0)
        else:
            keep = row >= sh
            rr, ri = pltpu.roll(br, sh, 0), pltpu.roll(bi, sh, 0)
        rr = jnp.where(keep, rr, 0.0)
        ri = jnp.where(keep, ri, 0.0)
        br, bi = br + pr * rr - pi * ri, bi + pr * ri + pi * rr
        pr, pi = pr * pr - pi * pi, 2.0 * pr * pi
        sh *= 2
    return br, bi


def _s5_disc(lr, li, ls, br, bi):
    step = jnp.exp(ls)
    mag = jnp.exp(lr * step)
    ab_re = mag * jnp.cos(li * step)
    ab_im = mag * jnp.sin(li * step)
    den = lr * lr + li * li
    nr = ab_re - 1.0
    co_re = (nr * lr + ab_im * li) / den
    co_im = (ab_im * lr - nr * li) / den
    return ab_re, ab_im, co_re * br - co_im * bi, co_re * bi + co_im * br


def _params(sem=None):
    return pltpu.CompilerParams(dimension_semantics=sem, vmem_limit_bytes=VMEM_LIMIT)


def _rowcall(name, body, n_rows, tm, ins, outs, scratch=(), comm=None):
    def spec(kind, shape):
        if kind == 'row':
            return pl.BlockSpec((tm, shape[1]), lambda i: (i, 0))
        if kind == 'full':
            nd = len(shape)
            return pl.BlockSpec(tuple(shape), lambda i: (0,) * nd, pipeline_mode=pl.Buffered(1))
        if kind == 'acc':
            nd = len(shape)
            return pl.BlockSpec(tuple(shape), lambda i: (0,) * nd)
        return kind

    in_specs = [spec(k, a.shape) for a, k in ins]
    out_shape, out_specs = [], []
    for k, s, dt in outs:
        shape = (n_rows, s) if k == 'row' else tuple(s)
        out_shape.append(jax.ShapeDtypeStruct(shape, dt))
        out_specs.append(spec(k, shape))
    args = [a for a, _ in ins]
    if comm is not None:
        return _hosting_call(body, name, (n_rows // tm,), in_specs, out_specs, out_shape, list(scratch),
                             ("arbitrary",), args, comm)
    return pl.pallas_call(
        body, name=name, grid=(n_rows // tm,), in_specs=in_specs, out_specs=out_specs, out_shape=out_shape,
        scratch_shapes=list(scratch), compiler_params=_params(("arbitrary",)),
    )(*args)


def _tile(n, cap):
    if n <= cap:
        return n
    best = LANE
    for k in range(1, n // LANE + 1):
        t = k * LANE
        if n % t == 0 and t <= cap:
            best = t
    return best


def _mm_tn(name, a, b, comm=None):
    n, k = a.shape
    m = b.shape[1]
    tk, tn, tr = _tile(k, 1536), _tile(m, 1536), min(n, 1024)

    def body(a_ref, b_ref, o_ref):
        @pl.when(pl.program_id(2) == 0)
        def _():
            o_ref[...] = jnp.zeros_like(o_ref)
        o_ref[...] += _dot_tn(a_ref[...].astype(BF16), b_ref[...].astype(BF16))

    out_spec = pl.BlockSpec((tk, tn), lambda i, j, r: (i, j))
    out_shape = (k, m)
    grid = (k // tk, m // tn, n // tr)
    in_specs = [pl.BlockSpec((tr, tk), lambda i, j, r: (r, i)), pl.BlockSpec((tr, tn), lambda i, j, r: (r, j))]
    if comm is not None:
        (res,), extra = _hosting_call(body, name, grid, in_specs, [out_spec], [jax.ShapeDtypeStruct(out_shape, F32)],
                                      [], ("arbitrary", "arbitrary", "arbitrary"), (a, b), comm)
        return res, extra
    return pl.pallas_call(
        body, name=name, grid=grid, in_specs=in_specs, out_specs=out_spec,
        out_shape=jax.ShapeDtypeStruct(out_shape, F32),
        compiler_params=_params(("parallel", "parallel", "arbitrary")),
    )(a, b)


def _comm_call(name, srcs, outs, nsem, body_fn, aliases=None):
    ns, no = len(srcs), len(outs)

    def body(*refs):
        body_fn(refs[:ns], refs[ns:ns + no], refs[ns + no], refs[ns + no + 1])

    hbm = pl.BlockSpec(memory_space=pltpu.HBM)
    return pl.pallas_call(
        body, name=name, in_specs=[hbm] * ns, out_specs=[hbm] * no,
        out_shape=[jax.ShapeDtypeStruct(s, d) for s, d in outs],
        scratch_shapes=[pltpu.SemaphoreType.DMA((nsem,)), pltpu.SemaphoreType.DMA((nsem,))],
        input_output_aliases=aliases or {},
    )(*srcs)


def _place():
    x, y, c = lax.axis_index("x"), lax.axis_index("y"), lax.axis_index("c")
    chips = (2 * x + y, 2 * (1 - x) + y, 2 * x + (1 - y), 2 * (1 - x) + (1 - y))
    peers = ((1 - x, y, c), (x, 1 - y, c), (x, y, 1 - c))
    return c, chips, peers


class _Comm(NamedTuple):
    srcs: Any
    outs: Any
    nsem: int
    stages: Any
    aliases: Any
    fracs: Any


class _Lazy:
    def __init__(self, make):
        self._make = make

    def start(self):
        self._make().start()

    def wait(self):
        self._make().wait()

    def wait_recv(self):
        self._make().wait_recv()

    def wait_send(self):
        self._make().wait_send()


def _run_comm(name, comm):
    def body_fn(srcs, dsts, ss, rs):
        for stage in comm.stages:
            stage(srcs, dsts, ss, rs)

    return _comm_call(name, comm.srcs, comm.outs, comm.nsem, body_fn, aliases=comm.aliases)


def _hosting_call(body, name, grid, in_specs, out_specs, out_shape, scratch, sem, args, comm):
    hbm = pl.BlockSpec(memory_space=pltpu.HBM)
    n_in, n_out, n_sc = len(in_specs), len(out_specs), len(scratch)
    ns, no = len(comm.srcs), len(comm.outs)
    total = math.prod(grid)
    steps = [min(total - 1, int(f * total)) for f in comm.fracs]

    def wrapped(*refs):
        ins, csrc = refs[:n_in], refs[n_in:n_in + ns]
        outs, cdst = refs[n_in + ns:n_in + ns + n_out], refs[n_in + ns + n_out:n_in + ns + n_out + no]
        rest = refs[n_in + ns + n_out + no:]
        lin = 0
        for d, size in enumerate(grid):
            lin = lin * size + pl.program_id(d)
        for stage, at in zip(comm.stages, steps):
            @pl.when(lin == at)
            def _(stage=stage):
                stage(csrc, cdst, rest[n_sc], rest[n_sc + 1])
        body(*ins, *outs, *rest[:n_sc])

    res = pl.pallas_call(
        wrapped, name=name, grid=grid, in_specs=list(in_specs) + [hbm] * ns, out_specs=list(out_specs) + [hbm] * no,
        out_shape=list(out_shape) + [jax.ShapeDtypeStruct(s, d) for s, d in comm.outs],
        scratch_shapes=list(scratch) + [pltpu.SemaphoreType.DMA((comm.nsem,)), pltpu.SemaphoreType.DMA((comm.nsem,))],
        input_output_aliases={n_in + i: n_out + o for i, o in comm.aliases.items()},
        compiler_params=_params(sem),
    )(*args, *comm.srcs)
    return res[:n_out], res[n_out:]


def _gather_weights(name, shards, axes):
    nw = len(shards)
    shapes = [s.shape for s in shards]
    outs = [((4,) + s if ax == 0 else (s[0], 4 * s[1]), BF16) for s, ax in zip(shapes, axes)]

    def win(ref, i, chip, start, rows):
        r, cols = shapes[i]
        if axes[i] == 0:
            return ref.at[chip, pl.ds(start, rows), :]
        return ref.at[pl.ds(start, rows), pl.ds(chip * cols, cols)]

    def place_body(*refs):
        srcs, dsts, bufs, sem = refs[:nw], refs[nw:2 * nw], refs[2 * nw:3 * nw], refs[3 * nw]
        me = 2 * lax.axis_index("x") + lax.axis_index("y")
        cps = []
        for i in range(nw):
            bufs[i][...] = srcs[i][...].astype(BF16)
            cps.append(pltpu.make_async_copy(bufs[i], win(dsts[i], i, me, 0, shapes[i][0]), sem.at[i]))
            cps[-1].start()
        for cp in cps:
            cp.wait()

    placed = pl.pallas_call(
        place_body, name=name, out_shape=[jax.ShapeDtypeStruct(s, d) for s, d in outs],
        in_specs=[pl.BlockSpec(memory_space=pltpu.VMEM)] * nw, out_specs=[pl.BlockSpec(memory_space=pltpu.HBM)] * nw,
        scratch_shapes=[pltpu.VMEM(s, BF16) for s in shapes] + [pltpu.SemaphoreType.DMA((nw,))],
        compiler_params=_params(),
    )(*shards)

    def copies(dsts, ss, rs):
        c, (me, xn, yn, dg), (px, py, psib) = _place()

        def rdma(k, window, peer):
            return _Lazy(lambda: pltpu.make_async_remote_copy(
                src_ref=window, dst_ref=window, send_sem=ss.at[k], recv_sem=rs.at[k], device_id=peer,
                device_id_type=MESH))

        first, later, swaps = [], [], []
        for i in range(nw):
            qr, hr, k0 = shapes[i][0] // 4, shapes[i][0] // 2, 6 * i
            q0, q1 = 2 * c * qr, (2 * c + 1) * qr
            w0, w1 = win(dsts[i], i, me, q0, qr), win(dsts[i], i, me, q1, qr)
            first.append([rdma(k0, w0, px), rdma(k0 + 1, w1, py), rdma(k0 + 2, w0, py), rdma(k0 + 3, w1, px)])
            later.append([rdma(k0 + 4, win(dsts[i], i, xn, q0, qr), py), rdma(k0 + 5, win(dsts[i], i, yn, q1, qr), px)])
            swaps.append([rdma(6 * nw + 3 * i + j, win(dsts[i], i, chip, c * hr, hr), psib)
                          for j, chip in enumerate((xn, yn, dg))])
        return first, later, swaps

    def send(srcs, dsts, ss, rs):
        for cps in copies(dsts, ss, rs)[0]:
            for cp in cps:
                cp.start()

    def forward(srcs, dsts, ss, rs):
        first, later, _ = copies(dsts, ss, rs)
        for i in range(nw):
            first[i][0].wait_recv()
            later[i][0].start()
            first[i][1].wait_recv()
            later[i][1].start()

    def swap(srcs, dsts, ss, rs):
        first, later, swaps = copies(dsts, ss, rs)
        for i in range(nw):
            for cp in first[i][2:] + later[i]:
                cp.wait_recv()
            for sw in swaps[i]:
                sw.start()

    def finish(srcs, dsts, ss, rs):
        first, later, swaps = copies(dsts, ss, rs)
        for i in range(nw):
            for sw in swaps[i]:
                sw.wait()
            for cp in first[i] + later[i]:
                cp.wait_send()

    return _Comm(placed, outs, 9 * nw, [send, forward, swap, finish], {i: i for i in range(nw)},
                 (0.0, 0.55, 0.85, 1.0))


def _gather_all(src):
    def copies(srcs, dsts, ss, rs):
        x, y, c = lax.axis_index("x"), lax.axis_index("y"), lax.axis_index("c")
        me, sib = (x, y, c), (x, y, 1 - c)
        chips = [(1 - x, y), (x, 1 - y), (1 - x, 1 - y)]

        def rows(px, py, pc):
            return dsts[0].at[4 * px + 2 * py + pc]

        def copy(k, block, to, src=None):
            return _Lazy(lambda: pltpu.make_async_remote_copy(
                src_ref=rows(*block) if src is None else src, dst_ref=rows(*block), send_sem=ss.at[k],
                recv_sem=rs.at[k], device_id=to, device_id_type=MESH))

        first = [copy(0, me, sib, src=srcs[0])] + [copy(1 + j, me, (*chip, c), src=srcs[0])
                                                   for j, chip in enumerate(chips)]
        passed = [copy(4 + j, (*chip, c), sib) for j, chip in enumerate(chips)]
        landed = [copy(0, sib, me)] + [copy(1 + j, (*chip, c), me) for j, chip in enumerate(chips)]
        landed += [copy(4 + j, (*chip, 1 - c), me) for j, chip in enumerate(chips)]
        return first, passed, landed

    def send(*refs):
        for cp in copies(*refs)[0]:
            cp.start()

    def forward(*refs):
        _, passed, landed = copies(*refs)
        for j in range(3):
            landed[1 + j].wait_recv()
            passed[j].start()

    def finish(*refs):
        first, passed, landed = copies(*refs)
        landed[0].wait_recv()
        for cp in landed[4:]:
            cp.wait_recv()
        for cp in first + passed:
            cp.wait_send()

    return _Comm([src], [((8,) + src.shape, src.dtype)], 7, [send, forward, finish], {}, (0.0, 0.9, 1.0))


def _f_in(x, g, w_in, tm):
    n = x.shape[0]

    def body(x_ref, g_ref, w_ref, h_ref, zgm_ref, zs5_ref, zg_ref):
        h = _rms_fwd(x_ref[...], g_ref[...]).astype(BF16)
        h_ref[...] = h
        zgm_ref[...] = _dot(h, w_ref[:, 0:2 * D])
        zs5_ref[...] = _dot(h, w_ref[:, 2 * D:2 * D + S5_W])
        zg_ref[...] = _dot(h, w_ref[:, 2 * D + S5_W:])

    return _rowcall("f_in", body, n, tm, [(x, 'row'), (g, 'full'), (w_in, 'full')],
                    [('row', D, BF16), ('row', 2 * D, F32), ('row', S5_W, F32), ('row', 2 * D, F32)])


def _tril():
    r = lax.broadcasted_iota(jnp.int32, (GM_CHUNK, GM_CHUNK), 0)
    c = lax.broadcasted_iota(jnp.int32, (GM_CHUNK, GM_CHUNK), 1)
    return r >= c


def _ln_stats(v):
    mu = jnp.mean(v, axis=-1, keepdims=True)
    vc = v - mu
    r = lax.rsqrt(jnp.mean(vc * vc, axis=-1, keepdims=True) + EPS)
    return vc * r, r


def _f_gmlp(zgm, ln_g, ln_b, w_s, b_s, tm):
    n = zgm.shape[0]

    def body(z_ref, lg_ref, lb_ref, ws_ref, bs_ref, y_ref):
        zg = _gelu(z_ref[...].astype(BF16))
        u = zg[:, :D]
        vh, _ = _ln_stats(zg[:, D:].astype(F32))
        vn = (vh * lg_ref[...] + lb_ref[...]).astype(BF16)
        keep = _tril()
        for g in range(GM_GROUPS):
            w = jnp.where(keep, ws_ref[g], 0.0).astype(BF16)
            cs = slice(g * LANE, (g + 1) * LANE)
            for c in range(tm // GM_CHUNK):
                rs = slice(c * GM_CHUNK, (c + 1) * GM_CHUNK)
                sv = _dot(w, vn[rs, cs]) + bs_ref[g]
                y_ref[rs, cs] = u[rs, cs] * sv.astype(BF16)

    return _rowcall("f_gmlp", body, n, tm,
                    [(zgm, 'row'), (ln_g, 'full'), (ln_b, 'full'), (w_s, 'full'), (b_s, 'full')],
                    [('row', D, BF16)])[0]


def _s5_specs(nb, nc, t, rev):
    def cc(c):
        return nc - 1 - c if rev else c
    slab = pl.BlockSpec((t, LANE), lambda j, b, c: (b * nc + cc(c), j))
    bb = pl.BlockSpec((None, LANE, 1024), lambda j, b, c: (j, 0, 0))
    cm = pl.BlockSpec((None, 1024, LANE), lambda j, b, c: (j, 0, 0))
    av = pl.BlockSpec((None, 1, 1024), lambda j, b, c: (j, 0, 0))
    dv = pl.BlockSpec((None, 1, LANE), lambda j, b, c: (j, 0, 0))
    st = pl.BlockSpec((None, None, None, 1, 1024), lambda j, b, c: (j, b, cc(c), 0, 0))
    return slab, bb, cm, av, dv, st


def _seg_load(ref, seg):
    return jnp.concatenate([ref[pl.ds(k, 8, stride=seg), :] for k in range(seg)], axis=0)


def _seg_store(ref, val, seg):
    for k in range(seg):
        ref[pl.ds(k, 8, stride=seg), :] = val[8 * k:8 * k + 8, :]


def _seg_scan(s_ref, ar, ai, seg, reverse):
    def step(i, carry):
        k = seg - 1 - i if reverse else i
        sr, si = carry
        nr = ar * sr - ai * si + s_ref[k, :, :512]
        ni = ar * si + ai * sr + s_ref[k, :, 512:]
        s_ref[k, :, :512] = nr
        s_ref[k, :, 512:] = ni
        return nr, ni

    zero = jnp.zeros((8, 512), F32)
    return lax.fori_loop(0, seg, step, (zero, zero), unroll=4)


def _seg_entries(tr, ti, pr, pi, cin, reverse):
    row = lax.broadcasted_iota(jnp.int32, (8, 512), 0)
    edge = row == (7 if reverse else 0)
    cr, ci = cin[:, :512], cin[:, 512:]
    xr = tr + jnp.where(edge, pr * cr - pi * ci, 0.0)
    xi = ti + jnp.where(edge, pr * ci + pi * cr, 0.0)
    ir, ii = _cscan(xr, xi, pr, pi, reverse)
    shift = 7 if reverse else 1
    er = jnp.where(edge, cr, pltpu.roll(ir, shift, 0))
    ei = jnp.where(edge, ci, pltpu.roll(ii, shift, 0))
    far = row == (0 if reverse else 7)
    out = jnp.concatenate([jnp.sum(jnp.where(far, ir, 0.0), axis=0, keepdims=True),
                           jnp.sum(jnp.where(far, ii, 0.0), axis=0, keepdims=True)], axis=1)
    return er, ei, out


def _seg_power(ar, ai, seg):
    pr, pi = ar, ai
    for _ in range(seg.bit_length() - 1):
        pr, pi = pr * pr - pi * pi, 2.0 * pr * pi
    return pr, pi


def _f_s5(zs5, bbc, ccm, avec, dvec, nb, t, comm):
    n = zs5.shape[0]
    nc = n // nb // t
    seg = t // 8
    slab, bb, cm, av, dv, st = _s5_specs(nb, nc, t, False)

    def body(u_ref, bb_ref, cc_ref, a_ref, d_ref, y_ref, st_ref, carry, s_ref):
        @pl.when(pl.program_id(2) == 0)
        def _():
            carry[...] = jnp.zeros_like(carry)
        cin = carry[...]
        st_ref[...] = cin
        up = _seg_load(u_ref, seg)
        s_ref[...] = _dot(up.astype(BF16), bb_ref[...]).reshape(seg, 8, 1024)
        a1r, a1i = a_ref[:, :512], a_ref[:, 512:]
        ar, ai = jnp.broadcast_to(a1r, (8, 512)), jnp.broadcast_to(a1i, (8, 512))
        tr, ti = _seg_scan(s_ref, ar, ai, seg, False)
        er, ei, cout = _seg_entries(tr, ti, *_seg_power(a1r, a1i, seg), cin, False)
        carry[...] = cout

        def apply(k, pe):
            pr, pi = pe
            pr, pi = ar * pr - ai * pi, ar * pi + ai * pr
            s_ref[k, :, :512] = s_ref[k, :, :512] + pr
            s_ref[k, :, 512:] = s_ref[k, :, 512:] + pi
            return pr, pi

        lax.fori_loop(0, seg, apply, (er, ei), unroll=4)
        s = s_ref[...].reshape(t, 1024).astype(BF16)
        _seg_store(y_ref, _dot(s, cc_ref[...]) + d_ref[...] * up, seg)

    return _hosting_call(
        body, "f_s5", (S5_SUPER, nb, nc), [slab, bb, cm, av, dv], [slab, st],
        [jax.ShapeDtypeStruct((n, S5_W), F32), jax.ShapeDtypeStruct((S5_SUPER, nb, nc, 1, 1024), F32)],
        [pltpu.VMEM((1, 1024), F32), pltpu.VMEM((seg, 8, 1024), F32)], ("arbitrary", "arbitrary", "arbitrary"),
        (zs5, bbc, ccm, avec, dvec), comm)


def _f_mix(ygm, ypre, zg, x, w_glu, w_br_gm, w_br_s5, w_mix_out, g_post, tm):
    n = x.shape[0]

    def body(ygm_ref, ypre_ref, zg_ref, x_ref, wglu_ref, wgm_ref, ws5_ref, wout_ref, g_ref,
             yg_ref, ys5_ref, a_ref, b_ref, mg_ref, mo_ref, x1_ref):
        yg = _gelu(ypre_ref[...])
        ygb = yg.astype(BF16)
        yg_ref[...] = ygb
        ys5 = (yg * _sig(_dot(ygb, wglu_ref[...]))).astype(BF16)
        ys5_ref[...] = ys5
        a = _dot(ygm_ref[...], wgm_ref[...])
        b = _dot(ys5, ws5_ref[...])
        a_ref[...] = a.astype(BF16)
        b_ref[...] = b.astype(BF16)
        zg = zg_ref[...]
        merged = (_sig(zg[:, :D]) * a + _sig(zg[:, D:]) * b).astype(BF16)
        mg_ref[...] = merged
        mo = _dot(merged, wout_ref[...])
        mo_ref[...] = mo
        x1_ref[...] = x_ref[...] + _rms_fwd(mo, g_ref[...])

    return _rowcall("f_mix", body, n, tm,
                    [(ygm, 'row'), (ypre, 'row'), (zg, 'row'), (x, 'row'), (w_glu, 'full'), (w_br_gm, 'full'),
                     (w_br_s5, 'full'), (w_mix_out, 'full'), (g_post, 'full')],
                    [('row', S5_W, BF16), ('row', S5_W, BF16), ('row', D, BF16), ('row', D, BF16),
                     ('row', D, BF16), ('row', D, F32), ('row', D, F32)])


def _f_mem(mem, g_mem, w_kv, tm):
    n = mem.shape[0]

    def body(m_ref, g_ref, w_ref, mn_ref, k_ref, v_ref):
        mn = _rms_fwd(m_ref[...], g_ref[...]).astype(BF16)
        mn_ref[...] = mn
        k_ref[...] = _dot(mn, w_ref[:, :D]).astype(BF16)
        v_ref[...] = _dot(mn, w_ref[:, D:]).astype(BF16)

    return _rowcall("f_mem", body, n, tm, [(mem, 'row'), (g_mem, 'full'), (w_kv, 'full')],
                    [('row', D, BF16), ('row', D, BF16), ('row', D, BF16)])


def _softmax(s):
    m = jnp.max(s, axis=-1, keepdims=True)
    e = jnp.exp(s - m)
    return e / jnp.sum(e, axis=-1, keepdims=True)


def _f_attn(x1, g_pre, w_q, k, v, w_o, g_post, tm, tpb, mlen):
    n = x1.shape[0]
    kv_spec = pl.BlockSpec((mlen, D), lambda i: (i // tpb, 0))
    scale = HEAD_DIM ** -0.5

    def body(x_ref, gp_ref, wq_ref, k_ref, v_ref, wo_ref, go_ref, hc_ref, q_ref, o_ref, ao_ref, x2_ref):
        x1v = x_ref[...]
        hc = _rms_fwd(x1v, gp_ref[...]).astype(BF16)
        hc_ref[...] = hc
        q_ref[...] = _dot(hc, wq_ref[...]).astype(BF16)
        for h in range(HEADS):
            hs = slice(h * HEAD_DIM, (h + 1) * HEAD_DIM)
            p = _softmax(_dot_nt(q_ref[:, hs], k_ref[:, hs]) * scale)
            o_ref[:, hs] = _dot(p.astype(BF16), v_ref[:, hs]).astype(BF16)
        ao = _dot(o_ref[...], wo_ref[...])
        ao_ref[...] = ao
        x2_ref[...] = x1v + _rms_fwd(ao, go_ref[...])

    return _rowcall("f_attn", body, n, tm,
                    [(x1, 'row'), (g_pre, 'full'), (w_q, 'full'), (k, kv_spec), (v, kv_spec), (w_o, 'full'),
                     (g_post, 'full')],
                    [('row', D, BF16), ('row', D, BF16), ('row', D, BF16), ('row', D, F32), ('row', D, F32)])


def _f_ffn_up(x2, g_pre, w_gu, tm):
    n = x2.shape[0]

    def body(x_ref, g_ref, w_ref, hf_ref, gu_ref, act_ref):
        hf = _rms_fwd(x_ref[...], g_ref[...]).astype(BF16)
        hf_ref[...] = hf
        gg = _dot(hf, w_ref[:, :FFN_H])
        uu = _dot(hf, w_ref[:, FFN_H:])
        gu_ref[:, :FFN_H] = gg.astype(BF16)
        gu_ref[:, FFN_H:] = uu.astype(BF16)
        act_ref[...] = (gg * _sig(gg) * uu).astype(BF16)

    return _rowcall("f_ffn_up", body, n, tm, [(x2, 'row'), (g_pre, 'full'), (w_gu, 'full')],
                    [('row', D, BF16), ('row', 2 * FFN_H, BF16), ('row', FFN_H, BF16)])


def _f_ffn_down(act, x2, tgt, w_down, g_post, tm):
    n = x2.shape[0]

    def body(a_ref, x_ref, t_ref, w_ref, g_ref, dn_ref, d3_ref, loss_ref):
        @pl.when(pl.program_id(0) == 0)
        def _():
            loss_ref[...] = jnp.zeros_like(loss_ref)
        dn = _dot(a_ref[...], w_ref[...])
        dn_ref[...] = dn
        err = x_ref[...] + _rms_fwd(dn, g_ref[...]) - t_ref[...]
        d3_ref[...] = err * (1.0 / D)
        loss_ref[...] += jnp.sum(err * err, axis=0, keepdims=True)

    return _rowcall("f_ffn_down", body, n, tm,
                    [(act, 'row'), (x2, 'row'), (tgt, 'row'), (w_down, 'full'), (g_post, 'full')],
                    [('row', D, F32), ('row', D, F32), ('acc', (1, D), F32)])


def _b_ffn_down(d3, dn, gu, g_post, w_down, tm):
    n = d3.shape[0]

    def body(d3_ref, dn_ref, gu_ref, g_ref, w_ref, ddn_ref, dgu_ref, dg_ref):
        @pl.when(pl.program_id(0) == 0)
        def _():
            dg_ref[...] = jnp.zeros_like(dg_ref)
        ddn, dg = _rms_bwd(dn_ref[...], g_ref[...], d3_ref[...])
        dg_ref[...] += dg
        ddn = ddn.astype(BF16)
        ddn_ref[...] = ddn
        dact = _dot_nt(ddn, w_ref[...]).astype(BF16)
        gg = gu_ref[:, :FFN_H]
        uu = gu_ref[:, FFN_H:]
        sg = _sig(gg)
        silu = gg * sg
        dgu_ref[:, :FFN_H] = dact * uu * (sg + silu * (1.0 - sg))
        dgu_ref[:, FFN_H:] = dact * silu

    return _rowcall("b_ffn_down", body, n, tm,
                    [(d3, 'row'), (dn, 'row'), (gu, 'row'), (g_post, 'full'), (w_down, 'full')],
                    [('row', D, BF16), ('row', 2 * FFN_H, BF16), ('acc', (1, D), F32)])


def _b_ffn_up(dgu, d3, x2, w_gu, g_pre, tm):
    n = d3.shape[0]

    def body(dgu_ref, d3_ref, x_ref, w_ref, g_ref, dx_ref, dg_ref):
        @pl.when(pl.program_id(0) == 0)
        def _():
            dg_ref[...] = jnp.zeros_like(dg_ref)
        dhf = _dot_nt(dgu_ref[...], w_ref[...])
        dx, dg = _rms_bwd(x_ref[...], g_ref[...], dhf)
        dg_ref[...] += dg
        dx_ref[...] = d3_ref[...] + dx

    return _rowcall("b_ffn_up", body, n, tm,
                    [(dgu, 'row'), (d3, 'row'), (x2, 'row'), (w_gu, 'full'), (g_pre, 'full')],
                    [('row', D, F32), ('acc', (1, D), F32)])


def _b_attn(dx2, ao, x1, q, g_post, w_o, w_q, g_pre, k, v, tm, tpb, mlen):
    n = dx2.shape[0]
    nb = n // (tm * tpb)
    kv_spec = pl.BlockSpec((mlen, D), lambda i: (i // tpb, 0))
    scale = HEAD_DIM ** -0.5

    def body(dx2_ref, ao_ref, x1_ref, q_ref, go_ref, wo_ref, wq_ref, gp_ref, k_ref, v_ref,
             dao_ref, dq_ref, dx1_ref, dk_ref, dv_ref, dgo_ref, dgp_ref):
        i = pl.program_id(0)

        @pl.when(i == 0)
        def _():
            dgo_ref[...] = jnp.zeros_like(dgo_ref)
            dgp_ref[...] = jnp.zeros_like(dgp_ref)

        @pl.when(i % tpb == 0)
        def _():
            dk_ref[...] = jnp.zeros_like(dk_ref)
            dv_ref[...] = jnp.zeros_like(dv_ref)

        dx2v = dx2_ref[...]
        dao, dgo = _rms_bwd(ao_ref[...], go_ref[...], dx2v)
        dgo_ref[...] += dgo
        dao = dao.astype(BF16)
        dao_ref[...] = dao
        do = _dot_nt(dao, wo_ref[...])
        for h in range(HEADS):
            hs = slice(h * HEAD_DIM, (h + 1) * HEAD_DIM)
            qh = q_ref[:, hs]
            kh = k_ref[:, hs]
            p = _softmax(_dot_nt(qh, kh) * scale)
            doh = do[:, hs].astype(BF16)
            dp = _dot_nt(doh, v_ref[:, hs])
            ds = (p * (dp - jnp.sum(dp * p, axis=-1, keepdims=True)) * scale).astype(BF16)
            dq_ref[:, hs] = _dot(ds, kh).astype(BF16)
            dk_ref[:, hs] += _dot_tn(ds, qh)
            dv_ref[:, hs] += _dot_tn(p.astype(BF16), doh)
        dhc = _dot_nt(dq_ref[...], wq_ref[...])
        dx, dgp = _rms_bwd(x1_ref[...], gp_ref[...], dhc)
        dgp_ref[...] += dgp
        dx1_ref[...] = dx2v + dx

    return _rowcall("b_attn", body, n, tm,
                    [(dx2, 'row'), (ao, 'row'), (x1, 'row'), (q, 'row'), (g_post, 'full'), (w_o, 'full'),
                     (w_q, 'full'), (g_pre, 'full'), (k, kv_spec), (v, kv_spec)],
                    [('row', D, BF16), ('row', D, BF16), ('row', D, F32),
                     (kv_spec, (nb * mlen, D), F32), (kv_spec, (nb * mlen, D), F32),
                     ('acc', (1, D), F32), ('acc', (1, D), F32)])


def _b_mem(dk, dv, mem, w_kv, g_mem, tm):
    n = mem.shape[0]

    def body(dk_ref, dv_ref, m_ref, w_ref, g_ref, dkv_ref, dg_ref):
        @pl.when(pl.program_id(0) == 0)
        def _():
            dg_ref[...] = jnp.zeros_like(dg_ref)
        dkb = dk_ref[...].astype(BF16)
        dvb = dv_ref[...].astype(BF16)
        dkv_ref[:, :D] = dkb
        dkv_ref[:, D:] = dvb
        dmn = _dot_nt(dkb, w_ref[:, :D]) + _dot_nt(dvb, w_ref[:, D:])
        _, dg = _rms_bwd(m_ref[...], g_ref[...], dmn)
        dg_ref[...] += dg

    return _rowcall("b_mem", body, n, tm, [(dk, 'row'), (dv, 'row'), (mem, 'row'), (w_kv, 'full'), (g_mem, 'full')],
                    [('row', 2 * D, BF16), ('acc', (1, D), F32)])


def _b_mix(dx1, mo, a, b, zg, ypre, g_post, w_mix_out, w_br_gm, w_br_s5, w_glu, tm):
    n = dx1.shape[0]

    def body(dx1_ref, mo_ref, a_ref, b_ref, zg_ref, ypre_ref, g_ref, wout_ref, wgm_ref, ws5_ref, wglu_ref,
             dmo_ref, da_ref, db_ref, dzg_ref, dygm_ref, dgate_ref, dypre_ref, dg_ref):
        @pl.when(pl.program_id(0) == 0)
        def _():
            dg_ref[...] = jnp.zeros_like(dg_ref)
        dmo, dg = _rms_bwd(mo_ref[...], g_ref[...], dx1_ref[...])
        dg_ref[...] += dg
        dmo = dmo.astype(BF16)
        dmo_ref[...] = dmo
        dmg = _dot_nt(dmo, wout_ref[...]).astype(BF16)
        zg = zg_ref[...].astype(BF16)
        sa, sb = _sig(zg[:, :D]), _sig(zg[:, D:])
        da = dmg * sa
        db = dmg * sb
        da_ref[...] = da
        db_ref[...] = db
        dzg_ref[:, :D] = dmg * a_ref[...] * (sa * (1.0 - sa))
        dzg_ref[:, D:] = dmg * b_ref[...] * (sb * (1.0 - sb))
        dygm_ref[...] = _dot_nt(da, wgm_ref[...])
        dys5 = _dot_nt(db, ws5_ref[...])
        ypre = ypre_ref[...]
        yg = _gelu(ypre)
        sgt = _sig(_dot(yg.astype(BF16), wglu_ref[...]))
        dgate = (dys5 * yg * sgt * (1.0 - sgt)).astype(BF16)
        dgate_ref[...] = dgate
        dyg = dys5 * sgt + _dot_nt(dgate, wglu_ref[...])
        dypre_ref[...] = dyg * _gelu_grad(ypre)

    return _rowcall("b_mix", body, n, tm,
                    [(dx1, 'row'), (mo, 'row'), (a, 'row'), (b, 'row'), (zg, 'row'), (ypre, 'row'), (g_post, 'full'),
                     (w_mix_out, 'full'), (w_br_gm, 'full'), (w_br_s5, 'full'), (w_glu, 'full')],
                    [('row', D, BF16), ('row', D, BF16), ('row', D, BF16), ('row', 2 * D, BF16), ('row', D, F32),
                     ('row', S5_W, BF16), ('row', S5_W, F32), ('acc', (1, D), F32)])


def _b_s5(zs5, dypre, states, bbc, ccm, avec, dvec, nb, t, comm):
    n = zs5.shape[0]
    nc = n // nb // t
    seg = t // 8
    slab, bb, cm, av, dv, st = _s5_specs(nb, nc, t, True)

    def body(u_ref, dy_ref, st_ref, bb_ref, cc_ref, a_ref, d_ref,
             du_ref, dbb_ref, dcc_ref, da_ref, dd_ref, lcarry, s_ref, l_ref):
        first = jnp.logical_and(pl.program_id(1) == 0, pl.program_id(2) == 0)

        @pl.when(first)
        def _():
            dbb_ref[...] = jnp.zeros_like(dbb_ref)
            dcc_ref[...] = jnp.zeros_like(dcc_ref)
            da_ref[...] = jnp.zeros_like(da_ref)
            dd_ref[...] = jnp.zeros_like(dd_ref)

        @pl.when(pl.program_id(2) == 0)
        def _():
            lcarry[...] = jnp.zeros_like(lcarry)

        up, dyp = _seg_load(u_ref, seg), _seg_load(dy_ref, seg)
        ub, dyb = up.astype(BF16), dyp.astype(BF16)
        a1r, a1i = a_ref[:, :512], a_ref[:, 512:]
        ar, ai = jnp.broadcast_to(a1r, (8, 512)), jnp.broadcast_to(a1i, (8, 512))
        pr, pi = _seg_power(a1r, a1i, seg)

        s_ref[...] = _dot(ub, bb_ref[...]).reshape(seg, 8, 1024)
        tr, ti = _seg_scan(s_ref, ar, ai, seg, False)
        er, ei, _ = _seg_entries(tr, ti, pr, pi, st_ref[...], False)

        def apply(k, pe):
            qr, qi = pe
            qr, qi = ar * qr - ai * qi, ar * qi + ai * qr
            s_ref[k, :, :512] = s_ref[k, :, :512] + qr
            s_ref[k, :, 512:] = s_ref[k, :, 512:] + qi
            return qr, qi

        lax.fori_loop(0, seg, apply, (er, ei), unroll=4)

        l_ref[...] = _dot_nt(dyb, cc_ref[...]).reshape(seg, 8, 1024)
        tr, ti = _seg_scan(l_ref, ar, -ai, seg, True)
        fr, fi, lout = _seg_entries(tr, ti, pr, -pi, lcarry[...], True)
        lcarry[...] = lout

        def apply_back(i, carry):
            qr, qi, accr, acci = carry
            k = seg - 1 - i
            qr, qi = ar * qr + ai * qi, ar * qi - ai * qr
            lr = l_ref[k, :, :512] + qr
            li = l_ref[k, :, 512:] + qi
            l_ref[k, :, :512] = lr
            l_ref[k, :, 512:] = li
            kp = jnp.maximum(k - 1, 0)
            sr = jnp.where(k == 0, er, s_ref[kp, :, :512])
            si = jnp.where(k == 0, ei, s_ref[kp, :, 512:])
            return qr, qi, accr + lr * sr + li * si, acci + li * sr - lr * si

        zero = jnp.zeros((8, 512), F32)
        _, _, accr, acci = lax.fori_loop(0, seg, apply_back, (fr, fi, zero, zero), unroll=4)
        da_ref[:, :512] += jnp.sum(accr, axis=0, keepdims=True)
        da_ref[:, 512:] += jnp.sum(acci, axis=0, keepdims=True)

        s = s_ref[...].reshape(t, 1024).astype(BF16)
        lam = l_ref[...].reshape(t, 1024).astype(BF16)
        dcc_ref[...] += _dot_tn(dyb, s)
        dbb_ref[...] += _dot_tn(ub, lam)
        _seg_store(du_ref, _dot_nt(lam, bb_ref[...]) + d_ref[...] * dyp, seg)
        dd_ref[...] += jnp.sum(dyp * up, axis=0, keepdims=True)

    return _hosting_call(
        body, "b_s5", (S5_SUPER, nb, nc), [slab, slab, st, bb, cm, av, dv], [slab, bb, bb, av, dv],
        [jax.ShapeDtypeStruct((n, S5_W), F32), jax.ShapeDtypeStruct(bbc.shape, F32),
         jax.ShapeDtypeStruct(bbc.shape, F32), jax.ShapeDtypeStruct(avec.shape, F32),
         jax.ShapeDtypeStruct(dvec.shape, F32)],
        [pltpu.VMEM((1, 1024), F32), pltpu.VMEM((seg, 8, 1024), F32), pltpu.VMEM((seg, 8, 1024), F32)],
        ("arbitrary", "arbitrary", "arbitrary"), (zs5, dypre, states, bbc, ccm, avec, dvec), comm)


def _b_gmlp(zgm, dygm, ln_g, ln_b, w_s, b_s, tm, comm):
    n = zgm.shape[0]

    def body(z_ref, dy_ref, lg_ref, lb_ref, ws_ref, bs_ref, dz_ref, dws_ref, dbs_ref, dlg_ref, dlb_ref,
             du_s, dvn_s):
        @pl.when(pl.program_id(0) == 0)
        def _():
            dws_ref[...] = jnp.zeros_like(dws_ref)
            dbs_ref[...] = jnp.zeros_like(dbs_ref)
            dlg_ref[...] = jnp.zeros_like(dlg_ref)
            dlb_ref[...] = jnp.zeros_like(dlb_ref)
        z = z_ref[...].astype(BF16)
        zg = _gelu(z)
        u = zg[:, :D]
        vh, r = _ln_stats(zg[:, D:].astype(F32))
        vn = (vh * lg_ref[...] + lb_ref[...]).astype(BF16)
        dy = dy_ref[...]
        keep = _tril()
        for g in range(GM_GROUPS):
            w = jnp.where(keep, ws_ref[g], 0.0).astype(BF16)
            cs = slice(g * LANE, (g + 1) * LANE)
            for c in range(tm // GM_CHUNK):
                rs = slice(c * GM_CHUNK, (c + 1) * GM_CHUNK)
                vb = vn[rs, cs]
                sv = _dot(w, vb) + bs_ref[g]
                dyb = dy[rs, cs]
                du_s[rs, cs] = dyb * sv
                dsv = dyb * u[rs, cs]
                dsvb = dsv.astype(BF16)
                dvn_s[rs, cs] = _dot_tn(w, dsvb)
                dws_ref[g] += jnp.where(keep, _dot_nt(dsvb, vb), 0.0)
                dbs_ref[g] += jnp.sum(dsv, axis=1, keepdims=True)
        dvn = dvn_s[...]
        dlg_ref[...] += jnp.sum(dvn * vh, axis=0, keepdims=True)
        dlb_ref[...] += jnp.sum(dvn, axis=0, keepdims=True)
        dvh = dvn * lg_ref[...]
        dv = r * (dvh - jnp.mean(dvh, axis=-1, keepdims=True) - vh * jnp.mean(dvh * vh, axis=-1, keepdims=True))
        dz_ref[:, :D] = du_s[...].astype(BF16) * _gelu_grad(z[:, :D])
        dz_ref[:, D:] = dv.astype(BF16) * _gelu_grad(z[:, D:])

    return _rowcall("b_gmlp", body, n, tm,
                    [(zgm, 'row'), (dygm, 'row'), (ln_g, 'full'), (ln_b, 'full'), (w_s, 'full'), (b_s, 'full')],
                    [('row', 2 * D, BF16), ('acc', w_s.shape, F32), ('acc', b_s.shape, F32), ('acc', (1, D), F32),
                     ('acc', (1, D), F32)],
                    scratch=[pltpu.VMEM((tm, D), F32), pltpu.VMEM((tm, D), F32)], comm=comm)


def _b_in(dzgm, dzs5, dzg, dx1, x, w_in, g_pre, tm, comm):
    n = x.shape[0]

    def body(d1_ref, d2_ref, d3_ref, dx1_ref, x_ref, w_ref, g_ref, gx_ref, dg_ref):
        @pl.when(pl.program_id(0) == 0)
        def _():
            dg_ref[...] = jnp.zeros_like(dg_ref)
        dh = (_dot_nt(d1_ref[...], w_ref[:, 0:2 * D]) + _dot_nt(d2_ref[...].astype(BF16), w_ref[:, 2 * D:2 * D + S5_W])
              + _dot_nt(d3_ref[...], w_ref[:, 2 * D + S5_W:]))
        dx, dg = _rms_bwd(x_ref[...], g_ref[...], dh)
        dg_ref[...] += dg
        gx_ref[...] = dx1_ref[...] + dx

    return _rowcall("b_in", body, n, tm,
                    [(dzgm, 'row'), (dzs5, 'row'), (dzg, 'row'), (dx1, 'row'), (x, 'row'), (w_in, 'full'),
                     (g_pre, 'full')],
                    [('row', D, F32), ('acc', (1, D), F32)], comm=comm)


def _whole(name, body, ins, outs):
    return pl.pallas_call(body, name=name, out_shape=[jax.ShapeDtypeStruct(s, dt) for s, dt in outs],
                          compiler_params=_params())(*ins)


def _s5_disc_fwd(lr, li, ls, br, bi):
    def body(lr_ref, li_ref, ls_ref, br_ref, bi_ref, o1, o2, o3, o4):
        outs = _s5_disc(lr_ref[...], li_ref[...], ls_ref[...], br_ref[...], bi_ref[...])
        for o, val in zip((o1, o2, o3, o4), outs):
            o[...] = val

    return _whole("s5_disc_fwd", body, [lr, li, ls, br, bi],
                  [(lr.shape, F32), (lr.shape, F32), (br.shape, F32), (br.shape, F32)])


def _s5_disc_bwd(lr, li, ls, br, bi, cts):
    def body(lr_ref, li_ref, ls_ref, br_ref, bi_ref, c1, c2, c3, c4, o1, o2, o3, o4, o5):
        _, vjp = jax.vjp(_s5_disc, lr_ref[...], li_ref[...], ls_ref[...], br_ref[...], bi_ref[...])
        grads = vjp((c1[...], c2[...], c3[...], c4[...]))
        for o, val in zip((o1, o2, o3, o4, o5), grads):
            o[...] = val

    return _whole("s5_disc_bwd", body, [lr, li, ls, br, bi, *cts],
                  [(lr.shape, F32), (lr.shape, F32), (lr.shape, F32), (br.shape, F32), (br.shape, F32)])


def _sum_slots(name, slots):
    def body(s_ref, o_ref):
        acc = s_ref[0]
        for k in range(1, slots.shape[0]):
            acc = acc + s_ref[k]
        o_ref[...] = acc

    return _whole(name, body, [slots], [(slots.shape[1:], F32)])[0]


def _adamw_math(w, g, m, v):
    c1 = 1.0 - ADAM_B1 ** ADAM_STEP
    c2 = 1.0 - ADAM_B2 ** ADAM_STEP
    mn = ADAM_B1 * m + (1.0 - ADAM_B1) * g
    vn = ADAM_B2 * v + (1.0 - ADAM_B2) * (g * g)
    return -ADAM_LR * ((mn / c1) / (jnp.sqrt(vn / c2) + ADAM_EPS) + ADAM_WD * w), mn, vn


def _adamw_many(name, ws, gs, ms, vs):
    k = len(ws)

    def body(*refs):
        for i in range(k):
            dl, mn, vn = _adamw_math(refs[i][...], refs[k + i][...], refs[2 * k + i][...], refs[3 * k + i][...])
            refs[4 * k + i][...] = dl
            refs[5 * k + i][...] = mn
            refs[6 * k + i][...] = vn

    res = _whole(name, body, [*ws, *gs, *ms, *vs], [(t.shape, F32) for t in ws] * 3)
    return res[:k], res[k:2 * k], res[2 * k:]


def _adamw(name, w, g, m, v, tm):
    def body(w_ref, g_ref, m_ref, v_ref, d_ref, mo_ref, vo_ref):
        d_ref[...], mo_ref[...], vo_ref[...] = _adamw_math(w_ref[...], g_ref[...], m_ref[...], v_ref[...])

    cols = w.shape[1]
    return _rowcall(name, body, w.shape[0], tm, [(w, 'row'), (g, 'row'), (m, 'row'), (v, 'row')],
                    [('row', cols, F32), ('row', cols, F32), ('row', cols, F32)])


def _picked_rowcall(name, body, grid, in_specs, out_cols, tm):
    return pl.pallas_call(
        lambda p_ref, *refs: body(*refs), name=name,
        grid_spec=pltpu.PrefetchScalarGridSpec(
            num_scalar_prefetch=1, grid=(grid,), in_specs=in_specs,
            out_specs=[pl.BlockSpec((tm, D), lambda i, p: (i, 0)) for _ in out_cols]),
        out_shape=[jax.ShapeDtypeStruct((grid * tm, D), dt) for dt in out_cols],
        compiler_params=_params(("arbitrary",)),
    )


def _exchange(copies):
    def build(src_refs, out_refs, ss, rs):
        def rdma(k, src, dst, peer):
            return _Lazy(lambda: pltpu.make_async_remote_copy(
                src_ref=src, dst_ref=dst, send_sem=ss.at[k], recv_sem=rs.at[k], device_id=peer, device_id_type=MESH))
        return copies(src_refs, out_refs, rdma)

    def start(*refs):
        for cp in build(*refs):
            cp.start()

    def finish(*refs):
        for cp in build(*refs):
            cp.wait()

    return start, finish


def _reduce_big(tag, slabs):
    rows = [s.shape[1] for s in slabs]
    blk = sum(rows) // 4
    tm = _tile_rows(blk, 512)
    per = blk // tm
    gpack = jnp.concatenate([s.reshape(4, 2, 2, r // 4, D) for s, r in zip(slabs, rows)], axis=3)

    mx, my, mc = lax.axis_index("x"), lax.axis_index("y"), lax.axis_index("c")
    chip_me, chip_xn, chip_yn = 2 * mx + my, 2 * (1 - mx) + my, 2 * mx + (1 - my)

    def pair_copies(srcs, dsts, rdma):
        c, _, (_, _, psib) = _place()
        return [rdma(0, srcs[0].at[:, 1 - c], dsts[0], psib)]

    from_sib = (yield _Comm([gpack], [((4, 2, blk, D), F32)], 1, _exchange(pair_copies), {}, (0.0, 1.0)))[0]

    def sum_both(a_ref, b_ref, o32_ref, o16_ref):
        s = a_ref[...].astype(F32) + b_ref[...].astype(F32)
        o32_ref[...] = s
        o16_ref[...] = s.astype(BF16)

    p32, p16 = _picked_rowcall(
        "sum_pair_" + tag, sum_both, 8 * per,
        [pl.BlockSpec((None, None, None, tm, D), lambda i, p: (i // (2 * per), p[0], (i // per) % 2, i % per, 0)),
         pl.BlockSpec((tm, D), lambda i, p: (i, 0))],
        [F32, BF16], tm)(jnp.stack([mc]).astype(jnp.int32), gpack, from_sib.reshape(8 * blk, D))
    p32, p16 = p32.reshape(4, 2, blk, D), p16.reshape(4, 2, blk, D)

    def step1_copies(srcs, dsts, rdma):
        _, (me, xn, yn, dg), (px, py, _) = _place()
        s16 = srcs[0]
        return [rdma(0, s16.at[yn, 0], dsts[0].at[0], py), rdma(1, s16.at[dg, 0], dsts[0].at[1], py),
                rdma(2, s16.at[xn, 1], dsts[0].at[2], px), rdma(3, s16.at[dg, 1], dsts[0].at[3], px)]

    recv1 = (yield _Comm([p16], [((4, blk, D), BF16)], 4, _exchange(step1_copies), {}, (0.0, 1.0)))[0]

    s32, s16 = _picked_rowcall(
        "sum_step1_" + tag, sum_both, 4 * per,
        [pl.BlockSpec((None, None, tm, D), lambda i, p: (p[i // per], i // (2 * per), i % per, 0)),
         pl.BlockSpec((tm, D), lambda i, p: (i, 0))],
        [F32, BF16], tm)(jnp.stack([chip_me, chip_xn, chip_me, chip_yn]).astype(jnp.int32), p32,
                         recv1.reshape(4 * blk, D))

    def step2_copies(srcs, dsts, rdma):
        _, _, (px, py, _) = _place()
        return [rdma(0, srcs[0].at[1], dsts[0].at[0], px), rdma(1, srcs[0].at[3], dsts[0].at[1], py)]

    recv2 = (yield _Comm([s16.reshape(4, blk, D)], [((2, blk, D), BF16)], 2, _exchange(step2_copies), {},
                         (0.0, 1.0)))[0]

    def sum_step2(a_ref, b_ref, o_ref):
        o_ref[...] = a_ref[...] + b_ref[...].astype(F32)

    red = _rowcall("sum_step2_" + tag, sum_step2, 2 * blk, tm,
                   [(s32.reshape(4, blk, D), pl.BlockSpec((None, tm, D), lambda i: (2 * (i // per), i % per, 0))),
                    (recv2.reshape(2 * blk, D), 'row')],
                   [('row', D, F32)])[0].reshape(2, blk, D)

    def share_copies(srcs, dsts, rdma):
        _, _, (_, _, psib) = _place()
        return [rdma(0, srcs[0], dsts[0], psib)]

    other = (yield _Comm([red], [((2, blk, D), F32)], 1, _exchange(share_copies), {}, (0.0, 1.0)))[0]
    lo = jnp.where(mc == 0, red, other)
    hi = jnp.where(mc == 0, other, red)
    out, off = [], 0
    for r in rows:
        q = r // 4
        out.append(jnp.concatenate([lo[0, off:off + q], lo[1, off:off + q], hi[0, off:off + q], hi[1, off:off + q]]))
        off += q
    return out


def _rows1024(a):
    flat = a.reshape(-1)
    pad = (-flat.shape[0]) % D
    if pad:
        flat = jnp.concatenate([flat, jnp.zeros((pad,), flat.dtype)])
    return flat.reshape(-1, D)


def _pack(arrs, pad_rows_to=8):
    parts = [_rows1024(a) for a in arrs]
    rows = sum(p.shape[0] for p in parts)
    pad = (-rows) % pad_rows_to
    if pad:
        parts.append(jnp.zeros((pad, D), parts[0].dtype))
    return jnp.concatenate(parts, axis=0)


def _unpack(packed, shapes):
    out, r = [], 0
    for s in shapes:
        size = math.prod(s)
        nr = -(-size // D)
        out.append(packed[r:r + nr].reshape(-1)[:size].reshape(s))
        r += nr
    return out


def _block_diag(t):
    eye = jnp.eye(8, dtype=t.dtype)
    j, g, a, b = t.shape
    return (t[:, :, :, None, :] * eye[None, :, None, :, None]).reshape(j, g * a, g * b)


def _block_diag_take(m, a, b):
    eye = jnp.eye(8, dtype=m.dtype)
    return (m.reshape(4, 8, a, 8, b) * eye[None, :, None, :, None]).sum(axis=3)


def kernel(x, mem, g_mix_pre, w_in, gm_ln_g, gm_ln_b, gm_w_s, gm_b_s, s5_lam_re, s5_lam_im, s5_log_step, s5_b_re, s5_b_im, s5_c_re, s5_c_im, s5_d, s5_w_glu, w_br_gm, w_br_s5, w_mix_out, g_mix_post, g_ca_pre, g_mem, ca_w_q, ca_w_kv, ca_w_o, g_ca_post, g_ffn_pre, ffn_w_gu, ffn_w_down, g_ffn_post, loss_target, m_g_mix_pre, m_w_in, m_gm_ln_g, m_gm_ln_b, m_gm_w_s, m_gm_b_s, m_s5_lam_re, m_s5_lam_im, m_s5_log_step, m_s5_b_re, m_s5_b_im, m_s5_c_re, m_s5_c_im, m_s5_d, m_s5_w_glu, m_w_br_gm, m_w_br_s5, m_w_mix_out, m_g_mix_post, m_g_ca_pre, m_g_mem, m_ca_w_q, m_ca_w_kv, m_ca_w_o, m_g_ca_post, m_g_ffn_pre, m_ffn_w_gu, m_ffn_w_down, m_g_ffn_post, v_g_mix_pre, v_w_in, v_gm_ln_g, v_gm_ln_b, v_gm_w_s, v_gm_b_s, v_s5_lam_re, v_s5_lam_im, v_s5_log_step, v_s5_b_re, v_s5_b_im, v_s5_c_re, v_s5_c_im, v_s5_d, v_s5_w_glu, v_w_br_gm, v_w_br_s5, v_w_mix_out, v_g_mix_post, v_g_ca_pre, v_g_mem, v_ca_w_q, v_ca_w_kv, v_ca_w_o, v_g_ca_post, v_g_ffn_pre, v_ffn_w_gu, v_ffn_w_down, v_g_ffn_post):
    a = dict(locals())
    w = {n: a[n][0] for n in WNAMES}
    nb, seq, _ = x.shape
    n = nb * seq
    mlen = mem.shape[1]
    tm = min(256, seq)
    tmb = min(512, seq)
    tpb = seq // tmb
    xf = x.reshape(n, D)
    tgt = loss_target.reshape(n, D)
    memf = mem.reshape(nb * mlen, D)
    tmm = min(256, mlen)

    big_names = list(BIG)
    local_shapes = {k: w[k].shape for k in big_names}
    w_in_full = _run_comm("gather_w_in", _gather_weights("cast_w_in", [w['w_in']], [BIG['w_in']]))[0]
    rest = [k for k in big_names if k != 'w_in']
    gather_rest = _gather_weights("cast_w_rest", [w[k] for k in rest], [BIG[k] for k in rest])

    vec = lambda name: w[name].reshape(1, D)

    to_lane = lambda p: p.reshape(1, -1)
    b_t = lambda p: p.transpose(2, 0, 1).reshape(16, -1)
    lr_l, li_l = to_lane(w['s5_lam_re']), to_lane(w['s5_lam_im'])
    ls_l = jnp.repeat(w['s5_log_step'], 64).reshape(1, -1)
    br_t, bi_t = b_t(w['s5_b_re']), b_t(w['s5_b_im'])
    ab_re, ab_im, bb_re, bb_im = _s5_disc_fwd(lr_l, li_l, ls_l, br_t, bi_t)
    blk = lambda t: _block_diag(t.reshape(16, 4, 8, 64).transpose(1, 2, 0, 3))
    bbc = jnp.concatenate([blk(bb_re), blk(bb_im)], axis=2).astype(BF16)
    cblk = lambda c: _block_diag(c.reshape(4, 8, 16, 64).transpose(0, 1, 3, 2))
    ccm = jnp.concatenate([cblk(w['s5_c_re']), -cblk(w['s5_c_im'])], axis=1).astype(BF16)
    avec = jnp.concatenate([ab_re.reshape(4, 1, 512), ab_im.reshape(4, 1, 512)], axis=2)
    dvec = w['s5_d'].reshape(4, 1, LANE)

    bs3 = w['gm_b_s'].reshape(GM_GROUPS, GM_CHUNK, 1)
    wf = {'w_in': w_in_full}
    h, zgm, zs5, zg = _f_in(xf, vec('g_mix_pre'), wf['w_in'], tmb)
    ygm = _f_gmlp(zgm, vec('gm_ln_g'), vec('gm_ln_b'), w['gm_w_s'], bs3, tmb)
    s5_t = min(S5_T, seq)
    (ypre, states), gathered = _f_s5(zs5, bbc, ccm, avec, dvec, nb, s5_t, gather_rest)
    for k, g in zip(rest, gathered):
        wf[k] = g.reshape(-1, g.shape[-1]) if BIG[k] == 0 else g
    yg, ys5, a_br, b_br, merged, mo, x1 = _f_mix(ygm, ypre, zg, xf, wf['s5_w_glu'], wf['w_br_gm'], wf['w_br_s5'],
                                                 wf['w_mix_out'], vec('g_mix_post'), tmb)
    mem_n, kk, vv = _f_mem(memf, vec('g_mem'), wf['ca_w_kv'], tmm)
    hc, q_att, o_att, ao, x2 = _f_attn(x1, vec('g_ca_pre'), wf['ca_w_q'], kk, vv, wf['ca_w_o'], vec('g_ca_post'),
                                tmb, tpb, mlen)
    hf, gu, act = _f_ffn_up(x2, vec('g_ffn_pre'), wf['ffn_w_gu'], tm)
    dn, d3, loss_cols = _f_ffn_down(act, x2, tgt, wf['ffn_w_down'], vec('g_ffn_post'), tmb)

    gsm = {}
    gbig = {}
    ddn, dgu, gsm['g_ffn_post'] = _b_ffn_down(d3, dn, gu, vec('g_ffn_post'), wf['ffn_w_down'], tm)
    dx2, gsm['g_ffn_pre'] = _b_ffn_up(dgu, d3, x2, wf['ffn_w_gu'], vec('g_ffn_pre'), tmb)
    gbig['ffn_w_down'] = _mm_tn("dw_ffn_down", act, ddn)
    gbig['ffn_w_gu'] = _mm_tn("dw_ffn_gu", dgu, hf)
    dao, dq, dx1, dk, dv, gsm['g_ca_post'], gsm['g_ca_pre'] = _b_attn(
        dx2, ao, x1, q_att, vec('g_ca_post'), wf['ca_w_o'], wf['ca_w_q'], vec('g_ca_pre'), kk, vv, tmb, tpb, mlen)
    gbig['ca_w_o'] = _mm_tn("dw_ca_o", o_att, dao)
    gbig['ca_w_q'] = _mm_tn("dw_ca_q", hc, dq)
    dkv, gsm['g_mem'] = _b_mem(dk, dv, memf, wf['ca_w_kv'], vec('g_mem'), tmm)
    gbig['ca_w_kv'] = _mm_tn("dw_ca_kv", dkv, mem_n)
    dmo, da_br, db_br, dzg, dygm, dgate, dypre, gsm['g_mix_post'] = _b_mix(
        dx1, mo, a_br, b_br, zg, ypre, vec('g_mix_post'), wf['w_mix_out'], wf['w_br_gm'], wf['w_br_s5'],
        wf['s5_w_glu'], tm)
    gbig['w_mix_out'] = _mm_tn("dw_mix_out", merged, dmo)
    gbig['w_br_gm'] = _mm_tn("dw_br_gm", ygm, da_br)
    gbig['w_br_s5'] = _mm_tn("dw_br_s5", db_br, ys5)
    gbig['s5_w_glu'] = _mm_tn("dw_s5_glu", yg, dgate)
    slab_of = lambda k: gbig[k].reshape(4, -1, D)
    red_rest = _reduce_big("rest", [slab_of(k) for k in rest])
    (dzs5, dbbc, dccm_t, davec, ddvec), got = _b_s5(zs5, dypre, states, bbc, ccm, avec, dvec, nb, s5_t,
                                                    next(red_rest))
    dccm = dccm_t.transpose(0, 2, 1)
    (dzgm, gsm['gm_w_s'], dbs3, gsm['gm_ln_g'], gsm['gm_ln_b']), got = _b_gmlp(
        zgm, dygm, vec('gm_ln_g'), vec('gm_ln_b'), w['gm_w_s'], bs3, tmb, red_rest.send(got))
    gsm['gm_b_s'] = dbs3
    grad_x, gsm['g_mix_pre'] = _b_in(dzgm, dzs5, dzg, dx1, xf, wf['w_in'], vec('g_mix_pre'), tmb, None)
    dw_in_gm, got = _mm_tn("dw_in_gm", dzgm, h, comm=red_rest.send(got))
    dw_in_s5, got = _mm_tn("dw_in_s5", dzs5, h, comm=red_rest.send(got))
    greds = dict(zip(rest, _finish(red_rest, got)))

    unblk = lambda m_: _block_diag_take(m_, 16, 64).transpose(2, 0, 1, 3).reshape(16, -1)
    d_bb_re, d_bb_im = unblk(dbbc[:, :, :512]), unblk(dbbc[:, :, 512:])
    cunblk = lambda m_: _block_diag_take(m_, 64, 16).transpose(0, 1, 3, 2).reshape(32, 16, 64)
    gsm['s5_c_re'] = cunblk(dccm[:, :512, :])
    gsm['s5_c_im'] = -cunblk(dccm[:, 512:, :])
    d_ab_re, d_ab_im = davec[:, :, :512].reshape(1, -1), davec[:, :, 512:].reshape(1, -1)
    g_lr, g_li, g_ls, g_br, g_bi = _s5_disc_bwd(lr_l, li_l, ls_l, br_t, bi_t, (d_ab_re, d_ab_im, d_bb_re, d_bb_im))
    gsm['s5_lam_re'], gsm['s5_lam_im'] = g_lr.reshape(32, 64), g_li.reshape(32, 64)
    gsm['s5_log_step'] = g_ls.reshape(32, 64).sum(axis=1)
    from_t = lambda t: t.reshape(16, 32, 64).transpose(1, 2, 0)
    gsm['s5_b_re'], gsm['s5_b_im'] = from_t(g_br), from_t(g_bi)
    gsm['s5_d'] = ddvec.reshape(32, 16)

    small_shapes = [w[k].shape for k in SMALL]
    spack = _pack([gsm[k].reshape(w[k].shape) for k in SMALL] + [loss_cols], 8)
    me_slot = 4 * lax.axis_index("x") + 2 * lax.axis_index("y") + lax.axis_index("c")
    dw_in_g, (sall,) = _mm_tn("dw_in_g", dzg, h, comm=_gather_all(spack))
    gbig['w_in'] = jnp.concatenate([dw_in_gm, dw_in_s5, dw_in_g], axis=0)
    sall = lax.dynamic_update_index_in_dim(sall, spack, me_slot, 0)
    ssum = _sum_slots("sum_small", sall)
    small_red = _unpack(ssum, small_shapes + [(1, D)])
    loss = 0.5 * jnp.sum(small_red[-1]) / D

    red_in = _reduce_big("w_in", [slab_of('w_in')])
    got = _run_comm("reduce_pair_w_in", next(red_in))
    for phase in ("reduce_step1_w_in", "reduce_step2_w_in"):
        got = _run_comm(phase, red_in.send(got))
    got = _run_comm("share_pair_w_in", red_in.send(got))
    greds['w_in'] = _finish(red_in, got)[0]

    res_big = {}
    for k in big_names:
        r_, c_ = local_shapes[k]
        g = greds[k].reshape(r_, c_) if BIG[k] == 0 else greds[k].reshape(c_, r_).T
        dl, mn, vn = _adamw("adamw_" + k, w[k], g, a['m_' + k][0], a['v_' + k][0], _tile_rows(g.shape[0], 256))
        res_big[k] = (g, dl, mn, vn)
    as2d = lambda t: t.reshape(1, -1) if t.ndim == 1 else t
    dl_s, mn_s, vn_s = _adamw_many("adamw_small", [as2d(w[k]) for k in SMALL], [as2d(g) for g in small_red[:-1]],
                                   [as2d(a['m_' + k][0]) for k in SMALL], [as2d(a['v_' + k][0]) for k in SMALL])
    res_small = {k: (g, dl.reshape(w[k].shape), mn.reshape(w[k].shape), vn.reshape(w[k].shape))
                 for k, g, dl, mn, vn in zip(SMALL, small_red[:-1], dl_s, mn_s, vn_s)}

    res = {**res_big, **res_small}
    outs = [loss, grad_x.reshape(nb, seq, D)]
    for i in range(4):
        outs += [res[k][i][None] for k in WNAMES]
    return tuple(outs)


def _finish(gen, got):
    try:
        gen.send(got)
    except StopIteration as done:
        return done.value
    raise AssertionError("the generator has more exchanges")


def _tile_rows(rows, cap=384):
    best = rows
    for t in range(16, min(rows, cap) + 1, 16):
        if rows % t == 0:
            best = t
    return best
```

```python
import math
from typing import Any, NamedTuple

import jax
import jax.numpy as jnp
from jax import lax
from jax.experimental import pallas as pl
from jax.experimental.pallas import tpu as pltpu

F32 = jnp.float32
BF16 = jnp.bfloat16
EPS = 1e-6
D = 1024
GM_CHUNK = 128
GM_GROUPS = 8
S5_W = 512
S5_SUPER = 4
S5_T = 512
HEADS = 4
HEAD_DIM = 256
FFN_H = 2816
LANE = 128
VMEM_LIMIT = 56 * 1024 * 1024
MESH = pl.DeviceIdType.MESH

ADAM_LR, ADAM_B1, ADAM_B2, ADAM_EPS, ADAM_WD, ADAM_STEP = 0.001, 0.9, 0.999, 1e-08, 0.01, 10

WNAMES = ['g_mix_pre', 'w_in', 'gm_ln_g', 'gm_ln_b', 'gm_w_s', 'gm_b_s', 's5_lam_re', 's5_lam_im', 's5_log_step',
          's5_b_re', 's5_b_im', 's5_c_re', 's5_c_im', 's5_d', 's5_w_glu', 'w_br_gm', 'w_br_s5', 'w_mix_out',
          'g_mix_post', 'g_ca_pre', 'g_mem', 'ca_w_q', 'ca_w_kv', 'ca_w_o', 'g_ca_post', 'g_ffn_pre', 'ffn_w_gu',
          'ffn_w_down', 'g_ffn_post']
BIG = {'w_in': 1, 's5_w_glu': 0, 'w_br_gm': 0, 'w_br_s5': 1, 'w_mix_out': 0, 'ca_w_q': 0, 'ca_w_kv': 1,
       'ca_w_o': 0, 'ffn_w_gu': 1, 'ffn_w_down': 0}
SMALL = [n for n in WNAMES if n not in BIG]


def _dot(a, b):
    return jnp.dot(a, b, preferred_element_type=F32)


def _dot_nt(a, b):
    return lax.dot_general(a, b, (((1,), (1,)), ((), ())), preferred_element_type=F32)


def _dot_tn(a, b):
    return lax.dot_general(a, b, (((0,), (0,)), ((), ())), preferred_element_type=F32)


def _rms_fwd(x, g):
    r = lax.rsqrt(jnp.mean(x * x, axis=-1, keepdims=True) + EPS)
    return x * r * g


def _rms_bwd(x, g, dy):
    r = lax.rsqrt(jnp.mean(x * x, axis=-1, keepdims=True) + EPS)
    xh = x * r
    gdy = dy * g
    dx = r * (gdy - xh * jnp.mean(gdy * xh, axis=-1, keepdims=True))
    dg = jnp.sum(dy * xh, axis=0, keepdims=True)
    return dx, dg


_GC = math.sqrt(2.0 / math.pi)


def _gelu(x):
    return 0.5 * x * (1.0 + jnp.tanh(_GC * (x + 0.044715 * x * x * x)))


def _gelu_grad(x):
    t = jnp.tanh(_GC * (x + 0.044715 * x * x * x))
    return 0.5 * (1.0 + t) + 0.5 * x * (1.0 - t * t) * _GC * (1.0 + 3 * 0.044715 * x * x)


def _sig(x):
    return 1.0 / (1.0 + jnp.exp(-x))


def _cscan(br, bi, ar, ai, reverse):
    t = br.shape[0]
    row = lax.broadcasted_iota(jnp.int32, br.shape, 0)
    pr, pi = ar, ai
    sh = 1
    while sh < t:
        if reverse:
            keep = row < t - sh
            rr, ri = pltpu.roll(br, t - sh, 0), pltpu.roll(bi, t - sh, 0)
        else:
            keep = row >= sh
            rr, ri = pltpu.roll(br, sh, 0), pltpu.roll(bi, sh, 0)
        rr = jnp.where(keep, rr, 0.0)
        ri = jnp.where(keep, ri, 0.0)
        br, bi = br + pr * rr - pi * ri, bi + pr * ri + pi * rr
        pr, pi = pr * pr - pi * pi, 2.0 * pr * pi
        sh *= 2
    return br, bi


def _s5_disc(lr, li, ls, br, bi):
    step = jnp.exp(ls)
    mag = jnp.exp(lr * step)
    ab_re = mag * jnp.cos(li * step)
    ab_im = mag * jnp.sin(li * step)
    den = lr * lr + li * li
    nr = ab_re - 1.0
    co_re = (nr * lr + ab_im * li) / den
    co_im = (ab_im * lr - nr * li) / den
    return ab_re, ab_im, co_re * br - co_im * bi, co_re * bi + co_im * br


def _params(sem=None):
    return pltpu.CompilerParams(dimension_semantics=sem, vmem_limit_bytes=VMEM_LIMIT)


def _rowcall(name, body, n_rows, tm, ins, outs, scratch=(), comm=None):
    def spec(kind, shape):
        if kind == 'row':
            return pl.BlockSpec((tm, shape[1]), lambda i: (i, 0))
        if kind == 'full':
            nd = len(shape)
            return pl.BlockSpec(tuple(shape), lambda i: (0,) * nd, pipeline_mode=pl.Buffered(1))
        if kind == 'acc':
            nd = len(shape)
            return pl.BlockSpec(tuple(shape), lambda i: (0,) * nd)
        return kind

    in_specs = [spec(k, a.shape) for a, k in ins]
    out_shape, out_specs = [], []
    for k, s, dt in outs:
        shape = (n_rows, s) if k == 'row' else tuple(s)
        out_shape.append(jax.ShapeDtypeStruct(shape, dt))
        out_specs.append(spec(k, shape))
    args = [a for a, _ in ins]
    if comm is not None:
        return _hosting_call(body, name, (n_rows // tm,), in_specs, out_specs, out_shape, list(scratch),
                             ("arbitrary",), args, comm)
    return pl.pallas_call(
        body, name=name, grid=(n_rows // tm,), in_specs=in_specs, out_specs=out_specs, out_shape=out_shape,
        scratch_shapes=list(scratch), compiler_params=_params(("arbitrary",)),
    )(*args)


def _tile(n, cap):
    if n <= cap:
        return n
    best = LANE
    for k in range(1, n // LANE + 1):
        t = k * LANE
        if n % t == 0 and t <= cap:
            best = t
    return best


def _mm_tn(name, a, b, comm=None):
    n, k = a.shape
    m = b.shape[1]
    tk, tn, tr = _tile(k, 1536), _tile(m, 1536), min(n, 1024)

    def body(a_ref, b_ref, o_ref):
        @pl.when(pl.program_id(2) == 0)
        def _():
            o_ref[...] = jnp.zeros_like(o_ref)
        o_ref[...] += _dot_tn(a_ref[...].astype(BF16), b_ref[...].astype(BF16))

    out_spec = pl.BlockSpec((tk, tn), lambda i, j, r: (i, j))
    out_shape = (k, m)
    grid = (k // tk, m // tn, n // tr)
    in_specs = [pl.BlockSpec((tr, tk), lambda i, j, r: (r, i)), pl.BlockSpec((tr, tn), lambda i, j, r: (r, j))]
    if comm is not None:
        (res,), extra = _hosting_call(body, name, grid, in_specs, [out_spec], [jax.ShapeDtypeStruct(out_shape, F32)],
                                      [], ("arbitrary", "arbitrary", "arbitrary"), (a, b), comm)
        return res, extra
    return pl.pallas_call(
        body, name=name, grid=grid, in_specs=in_specs, out_specs=out_spec,
        out_shape=jax.ShapeDtypeStruct(out_shape, F32),
        compiler_params=_params(("parallel", "parallel", "arbitrary")),
    )(a, b)


def _comm_call(name, srcs, outs, nsem, body_fn, aliases=None):
    ns, no = len(srcs), len(outs)

    def body(*refs):
        body_fn(refs[:ns], refs[ns:ns + no], refs[ns + no], refs[ns + no + 1])

    hbm = pl.BlockSpec(memory_space=pltpu.HBM)
    return pl.pallas_call(
        body, name=name, in_specs=[hbm] * ns, out_specs=[hbm] * no,
        out_shape=[jax.ShapeDtypeStruct(s, d) for s, d in outs],
        scratch_shapes=[pltpu.SemaphoreType.DMA((nsem,)), pltpu.SemaphoreType.DMA((nsem,))],
        input_output_aliases=aliases or {},
    )(*srcs)


def _place():
    x, y, c = lax.axis_index("x"), lax.axis_index("y"), lax.axis_index("c")
    chips = (2 * x + y, 2 * (1 - x) + y, 2 * x + (1 - y), 2 * (1 - x) + (1 - y))
    peers = ((1 - x, y, c), (x, 1 - y, c), (x, y, 1 - c))
    return c, chips, peers


class _Comm(NamedTuple):
    srcs: Any
    outs: Any
    nsem: int
    stages: Any
    aliases: Any
    fracs: Any


class _Lazy:
    def __init__(self, make):
        self._make = make

    def start(self):
        self._make().start()

    def wait(self):
        self._make().wait()

    def wait_recv(self):
        self._make().wait_recv()

    def wait_send(self):
        self._make().wait_send()


def _run_comm(name, comm):
    def body_fn(srcs, dsts, ss, rs):
        for stage in comm.stages:
            stage(srcs, dsts, ss, rs)

    return _comm_call(name, comm.srcs, comm.outs, comm.nsem, body_fn, aliases=comm.aliases)


def _hosting_call(body, name, grid, in_specs, out_specs, out_shape, scratch, sem, args, comm):
    hbm = pl.BlockSpec(memory_space=pltpu.HBM)
    n_in, n_out, n_sc = len(in_specs), len(out_specs), len(scratch)
    ns, no = len(comm.srcs), len(comm.outs)
    total = math.prod(grid)
    steps = [min(total - 1, int(f * total)) for f in comm.fracs]

    def wrapped(*refs):
        ins, csrc = refs[:n_in], refs[n_in:n_in + ns]
        outs, cdst = refs[n_in + ns:n_in + ns + n_out], refs[n_in + ns + n_out:n_in + ns + n_out + no]
        rest = refs[n_in + ns + n_out + no:]
        lin = 0
        for d, size in enumerate(grid):
            lin = lin * size + pl.program_id(d)
        for stage, at in zip(comm.stages, steps):
            @pl.when(lin == at)
            def _(stage=stage):
                stage(csrc, cdst, rest[n_sc], rest[n_sc + 1])
        body(*ins, *outs, *rest[:n_sc])

    res = pl.pallas_call(
        wrapped, name=name, grid=grid, in_specs=list(in_specs) + [hbm] * ns, out_specs=list(out_specs) + [hbm] * no,
        out_shape=list(out_shape) + [jax.ShapeDtypeStruct(s, d) for s, d in comm.outs],
        scratch_shapes=list(scratch) + [pltpu.SemaphoreType.DMA((comm.nsem,)), pltpu.SemaphoreType.DMA((comm.nsem,))],
        input_output_aliases={n_in + i: n_out + o for i, o in comm.aliases.items()},
        compiler_params=_params(sem),
    )(*args, *comm.srcs)
    return res[:n_out], res[n_out:]


def _gather_weights(name, shards, axes):
    nw = len(shards)
    shapes = [s.shape for s in shards]
    outs = [((4,) + s if ax == 0 else (s[0], 4 * s[1]), BF16) for s, ax in zip(shapes, axes)]

    def win(ref, i, chip, start, rows):
        r, cols = shapes[i]
        if axes[i] == 0:
            return ref.at[chip, pl.ds(start, rows), :]
        return ref.at[pl.ds(start, rows), pl.ds(chip * cols, cols)]

    def place_body(*refs):
        srcs, dsts, bufs, sem = refs[:nw], refs[nw:2 * nw], refs[2 * nw:3 * nw], refs[3 * nw]
        me = 2 * lax.axis_index("x") + lax.axis_index("y")
        cps = []
        for i in range(nw):
            bufs[i][...] = srcs[i][...].astype(BF16)
            cps.append(pltpu.make_async_copy(bufs[i], win(dsts[i], i, me, 0, shapes[i][0]), sem.at[i]))
            cps[-1].start()
        for cp in cps:
            cp.wait()

    placed = pl.pallas_call(
        place_body, name=name, out_shape=[jax.ShapeDtypeStruct(s, d) for s, d in outs],
        in_specs=[pl.BlockSpec(memory_space=pltpu.VMEM)] * nw, out_specs=[pl.BlockSpec(memory_space=pltpu.HBM)] * nw,
        scratch_shapes=[pltpu.VMEM(s, BF16) for s in shapes] + [pltpu.SemaphoreType.DMA((nw,))],
        compiler_params=_params(),
    )(*shards)

    def copies(dsts, ss, rs):
        c, (me, xn, yn, dg), (px, py, psib) = _place()

        def rdma(k, window, peer):
            return _Lazy(lambda: pltpu.make_async_remote_copy(
                src_ref=window, dst_ref=window, send_sem=ss.at[k], recv_sem=rs.at[k], device_id=peer,
                device_id_type=MESH))

        first, later, swaps = [], [], []
        for i in range(nw):
            qr, hr, k0 = shapes[i][0] // 4, shapes[i][0] // 2, 6 * i
            q0, q1 = 2 * c * qr, (2 * c + 1) * qr
            w0, w1 = win(dsts[i], i, me, q0, qr), win(dsts[i], i, me, q1, qr)
            first.append([rdma(k0, w0, px), rdma(k0 + 1, w1, py), rdma(k0 + 2, w0, py), rdma(k0 + 3, w1, px)])
            later.append([rdma(k0 + 4, win(dsts[i], i, xn, q0, qr), py), rdma(k0 + 5, win(dsts[i], i, yn, q1, qr), px)])
            swaps.append([rdma(6 * nw + 3 * i + j, win(dsts[i], i, chip, c * hr, hr), psib)
                          for j, chip in enumerate((xn, yn, dg))])
        return first, later, swaps

    def send(srcs, dsts, ss, rs):
        for cps in copies(dsts, ss, rs)[0]:
            for cp in cps:
                cp.start()

    def forward(srcs, dsts, ss, rs):
        first, later, _ = copies(dsts, ss, rs)
        for i in range(nw):
            first[i][0].wait_recv()
            later[i][0].start()
            first[i][1].wait_recv()
            later[i][1].start()

    def swap(srcs, dsts, ss, rs):
        first, later, swaps = copies(dsts, ss, rs)
        for i in range(nw):
            for cp in first[i][2:] + later[i]:
                cp.wait_recv()
            for sw in swaps[i]:
                sw.start()

    def finish(srcs, dsts, ss, rs):
        first, later, swaps = copies(dsts, ss, rs)
        for i in range(nw):
            for sw in swaps[i]:
                sw.wait()
            for cp in first[i] + later[i]:
                cp.wait_send()

    return _Comm(placed, outs, 9 * nw, [send, forward, swap, finish], {i: i for i in range(nw)},
                 (0.0, 0.55, 0.85, 1.0))


def _gather_all(src):
    def copies(srcs, dsts, ss, rs):
        x, y, c = lax.axis_index("x"), lax.axis_index("y"), lax.axis_index("c")
        me, sib = (x, y, c), (x, y, 1 - c)
        chips = [(1 - x, y), (x, 1 - y), (1 - x, 1 - y)]

        def rows(px, py, pc):
            return dsts[0].at[4 * px + 2 * py + pc]

        def copy(k, block, to, src=None):
            return _Lazy(lambda: pltpu.make_async_remote_copy(
                src_ref=rows(*block) if src is None else src, dst_ref=rows(*block), send_sem=ss.at[k],
                recv_sem=rs.at[k], device_id=to, device_id_type=MESH))

        first = [copy(0, me, sib, src=srcs[0])] + [copy(1 + j, me, (*chip, c), src=srcs[0])
                                                   for j, chip in enumerate(chips)]
        passed = [copy(4 + j, (*chip, c), sib) for j, chip in enumerate(chips)]
        landed = [copy(0, sib, me)] + [copy(1 + j, (*chip, c), me) for j, chip in enumerate(chips)]
        landed += [copy(4 + j, (*chip, 1 - c), me) for j, chip in enumerate(chips)]
        return first, passed, landed

    def send(*refs):
        for cp in copies(*refs)[0]:
            cp.start()

    def forward(*refs):
        _, passed, landed = copies(*refs)
        for j in range(3):
            landed[1 + j].wait_recv()
            passed[j].start()

    def finish(*refs):
        first, passed, landed = copies(*refs)
        landed[0].wait_recv()
        for cp in landed[4:]:
            cp.wait_recv()
        for cp in first + passed:
            cp.wait_send()

    return _Comm([src], [((8,) + src.shape, src.dtype)], 7, [send, forward, finish], {}, (0.0, 0.9, 1.0))


def _f_in(x, g, w_in, tm):
    n = x.shape[0]

    def body(x_ref, g_ref, w_ref, h_ref, zgm_ref, zs5_ref, zg_ref):
        h = _rms_fwd(x_ref[...], g_ref[...]).astype(BF16)
        h_ref[...] = h
        zgm_ref[...] = _dot(h, w_ref[:, 0:2 * D])
        zs5_ref[...] = _dot(h, w_ref[:, 2 * D:2 * D + S5_W])
        zg_ref[...] = _dot(h, w_ref[:, 2 * D + S5_W:])

    return _rowcall("f_in", body, n, tm, [(x, 'row'), (g, 'full'), (w_in, 'full')],
                    [('row', D, BF16), ('row', 2 * D, F32), ('row', S5_W, F32), ('row', 2 * D, F32)])


def _tril():
    r = lax.broadcasted_iota(jnp.int32, (GM_CHUNK, GM_CHUNK), 0)
    c = lax.broadcasted_iota(jnp.int32, (GM_CHUNK, GM_CHUNK), 1)
    return r >= c


def _ln_stats(v):
    mu = jnp.mean(v, axis=-1, keepdims=True)
    vc = v - mu
    r = lax.rsqrt(jnp.mean(vc * vc, axis=-1, keepdims=True) + EPS)
    return vc * r, r


def _f_gmlp(zgm, ln_g, ln_b, w_s, b_s, tm):
    n = zgm.shape[0]

    def body(z_ref, lg_ref, lb_ref, ws_ref, bs_ref, y_ref):
        zg = _gelu(z_ref[...].astype(BF16))
        u = zg[:, :D]
        vh, _ = _ln_stats(zg[:, D:].astype(F32))
        vn = (vh * lg_ref[...] + lb_ref[...]).astype(BF16)
        keep = _tril()
        for g in range(GM_GROUPS):
            w = jnp.where(keep, ws_ref[g], 0.0).astype(BF16)
            cs = slice(g * LANE, (g + 1) * LANE)
            for c in range(tm // GM_CHUNK):
                rs = slice(c * GM_CHUNK, (c + 1) * GM_CHUNK)
                sv = _dot(w, vn[rs, cs]) + bs_ref[g]
                y_ref[rs, cs] = u[rs, cs] * sv.astype(BF16)

    return _rowcall("f_gmlp", body, n, tm,
                    [(zgm, 'row'), (ln_g, 'full'), (ln_b, 'full'), (w_s, 'full'), (b_s, 'full')],
                    [('row', D, BF16)])[0]


def _s5_specs(nb, nc, t, rev):
    def cc(c):
        return nc - 1 - c if rev else c
    slab = pl.BlockSpec((t, LANE), lambda j, b, c: (b * nc + cc(c), j))
    bb = pl.BlockSpec((None, LANE, 1024), lambda j, b, c: (j, 0, 0))
    cm = pl.BlockSpec((None, 1024, LANE), lambda j, b, c: (j, 0, 0))
    av = pl.BlockSpec((None, 1, 1024), lambda j, b, c: (j, 0, 0))
    dv = pl.BlockSpec((None, 1, LANE), lambda j, b, c: (j, 0, 0))
    st = pl.BlockSpec((None, None, None, 1, 1024), lambda j, b, c: (j, b, cc(c), 0, 0))
    return slab, bb, cm, av, dv, st


def _seg_load(ref, seg):
    return jnp.concatenate([ref[pl.ds(k, 8, stride=seg), :] for k in range(seg)], axis=0)


def _seg_store(ref, val, seg):
    for k in range(seg):
        ref[pl.ds(k, 8, stride=seg), :] = val[8 * k:8 * k + 8, :]


def _seg_scan(s_ref, ar, ai, seg, reverse):
    def step(i, carry):
        k = seg - 1 - i if reverse else i
        sr, si = carry
        nr = ar * sr - ai * si + s_ref[k, :, :512]
        ni = ar * si + ai * sr + s_ref[k, :, 512:]
        s_ref[k, :, :512] = nr
        s_ref[k, :, 512:] = ni
        return nr, ni

    zero = jnp.zeros((8, 512), F32)
    return lax.fori_loop(0, seg, step, (zero, zero), unroll=4)


def _seg_entries(tr, ti, pr, pi, cin, reverse):
    row = lax.broadcasted_iota(jnp.int32, (8, 512), 0)
    edge = row == (7 if reverse else 0)
    cr, ci = cin[:, :512], cin[:, 512:]
    xr = tr + jnp.where(edge, pr * cr - pi * ci, 0.0)
    xi = ti + jnp.where(edge, pr * ci + pi * cr, 0.0)
    ir, ii = _cscan(xr, xi, pr, pi, reverse)
    shift = 7 if reverse else 1
    er = jnp.where(edge, cr, pltpu.roll(ir, shift, 0))
    ei = jnp.where(edge, ci, pltpu.roll(ii, shift, 0))
    far = row == (0 if reverse else 7)
    out = jnp.concatenate([jnp.sum(jnp.where(far, ir, 0.0), axis=0, keepdims=True),
                           jnp.sum(jnp.where(far, ii, 0.0), axis=0, keepdims=True)], axis=1)
    return er, ei, out


def _seg_power(ar, ai, seg):
    pr, pi = ar, ai
    for _ in range(seg.bit_length() - 1):
        pr, pi = pr * pr - pi * pi, 2.0 * pr * pi
    return pr, pi


def _f_s5(zs5, bbc, ccm, avec, dvec, nb, t, comm):
    n = zs5.shape[0]
    nc = n // nb // t
    seg = t // 8
    slab, bb, cm, av, dv, st = _s5_specs(nb, nc, t, False)

    def body(u_ref, bb_ref, cc_ref, a_ref, d_ref, y_ref, st_ref, carry, s_ref):
        @pl.when(pl.program_id(2) == 0)
        def _():
            carry[...] = jnp.zeros_like(carry)
        cin = carry[...]
        st_ref[...] = cin
        up = _seg_load(u_ref, seg)
        s_ref[...] = _dot(up.astype(BF16), bb_ref[...]).reshape(seg, 8, 1024)
        a1r, a1i = a_ref[:, :512], a_ref[:, 512:]
        ar, ai = jnp.broadcast_to(a1r, (8, 512)), jnp.broadcast_to(a1i, (8, 512))
        tr, ti = _seg_scan(s_ref, ar, ai, seg, False)
        er, ei, cout = _seg_entries(tr, ti, *_seg_power(a1r, a1i, seg), cin, False)
        carry[...] = cout

        def apply(k, pe):
            pr, pi = pe
            pr, pi = ar * pr - ai * pi, ar * pi + ai * pr
            s_ref[k, :, :512] = s_ref[k, :, :512] + pr
            s_ref[k, :, 512:] = s_ref[k, :, 512:] + pi
            return pr, pi

        lax.fori_loop(0, seg, apply, (er, ei), unroll=4)
        s = s_ref[...].reshape(t, 1024).astype(BF16)
        _seg_store(y_ref, _dot(s, cc_ref[...]) + d_ref[...] * up, seg)

    return _hosting_call(
        body, "f_s5", (S5_SUPER, nb, nc), [slab, bb, cm, av, dv], [slab, st],
        [jax.ShapeDtypeStruct((n, S5_W), F32), jax.ShapeDtypeStruct((S5_SUPER, nb, nc, 1, 1024), F32)],
        [pltpu.VMEM((1, 1024), F32), pltpu.VMEM((seg, 8, 1024), F32)], ("arbitrary", "arbitrary", "arbitrary"),
        (zs5, bbc, ccm, avec, dvec), comm)


def _f_mix(ygm, ypre, zg, x, w_glu, w_br_gm, w_br_s5, w_mix_out, g_post, tm):
    n = x.shape[0]

    def body(ygm_ref, ypre_ref, zg_ref, x_ref, wglu_ref, wgm_ref, ws5_ref, wout_ref, g_ref,
             yg_ref, ys5_ref, a_ref, b_ref, mg_ref, mo_ref, x1_ref):
        yg = _gelu(ypre_ref[...])
        ygb = yg.astype(BF16)
        yg_ref[...] = ygb
        ys5 = (yg * _sig(_dot(ygb, wglu_ref[...]))).astype(BF16)
        ys5_ref[...] = ys5
        a = _dot(ygm_ref[...], wgm_ref[...])
        b = _dot(ys5, ws5_ref[...])
        a_ref[...] = a.astype(BF16)
        b_ref[...] = b.astype(BF16)
        zg = zg_ref[...]
        merged = (_sig(zg[:, :D]) * a + _sig(zg[:, D:]) * b).astype(BF16)
        mg_ref[...] = merged
        mo = _dot(merged, wout_ref[...])
        mo_ref[...] = mo
        x1_ref[...] = x_ref[...] + _rms_fwd(mo, g_ref[...])

    return _rowcall("f_mix", body, n, tm,
                    [(ygm, 'row'), (ypre, 'row'), (zg, 'row'), (x, 'row'), (w_glu, 'full'), (w_br_gm, 'full'),
                     (w_br_s5, 'full'), (w_mix_out, 'full'), (g_post, 'full')],
                    [('row', S5_W, BF16), ('row', S5_W, BF16), ('row', D, BF16), ('row', D, BF16),
                     ('row', D, BF16), ('row', D, F32), ('row', D, F32)])


def _f_mem(mem, g_mem, w_kv, tm):
    n = mem.shape[0]

    def body(m_ref, g_ref, w_ref, mn_ref, k_ref, v_ref):
        mn = _rms_fwd(m_ref[...], g_ref[...]).astype(BF16)
        mn_ref[...] = mn
        k_ref[...] = _dot(mn, w_ref[:, :D]).astype(BF16)
        v_ref[...] = _dot(mn, w_ref[:, D:]).astype(BF16)

    return _rowcall("f_mem", body, n, tm, [(mem, 'row'), (g_mem, 'full'), (w_kv, 'full')],
                    [('row', D, BF16), ('row', D, BF16), ('row', D, BF16)])


def _softmax(s):
    m = jnp.max(s, axis=-1, keepdims=True)
    e = jnp.exp(s - m)
    return e / jnp.sum(e, axis=-1, keepdims=True)


def _f_attn(x1, g_pre, w_q, k, v, w_o, g_post, tm, tpb, mlen):
    n = x1.shape[0]
    kv_spec = pl.BlockSpec((mlen, D), lambda i: (i // tpb, 0))
    scale = HEAD_DIM ** -0.5

    def body(x_ref, gp_ref, wq_ref, k_ref, v_ref, wo_ref, go_ref, hc_ref, q_ref, o_ref, ao_ref, x2_ref):
        x1v = x_ref[...]
        hc = _rms_fwd(x1v, gp_ref[...]).astype(BF16)
        hc_ref[...] = hc
        q_ref[...] = _dot(hc, wq_ref[...]).astype(BF16)
        for h in range(HEADS):
            hs = slice(h * HEAD_DIM, (h + 1) * HEAD_DIM)
            p = _softmax(_dot_nt(q_ref[:, hs], k_ref[:, hs]) * scale)
            o_ref[:, hs] = _dot(p.astype(BF16), v_ref[:, hs]).astype(BF16)
        ao = _dot(o_ref[...], wo_ref[...])
        ao_ref[...] = ao
        x2_ref[...] = x1v + _rms_fwd(ao, go_ref[...])

    return _rowcall("f_attn", body, n, tm,
                    [(x1, 'row'), (g_pre, 'full'), (w_q, 'full'), (k, kv_spec), (v, kv_spec), (w_o, 'full'),
                     (g_post, 'full')],
                    [('row', D, BF16), ('row', D, BF16), ('row', D, BF16), ('row', D, F32), ('row', D, F32)])


def _f_ffn_up(x2, g_pre, w_gu, tm):
    n = x2.shape[0]

    def body(x_ref, g_ref, w_ref, hf_ref, gu_ref, act_ref):
        hf = _rms_fwd(x_ref[...], g_ref[...]).astype(BF16)
        hf_ref[...] = hf
        gg = _dot(hf, w_ref[:, :FFN_H])
        uu = _dot(hf, w_ref[:, FFN_H:])
        gu_ref[:, :FFN_H] = gg.astype(BF16)
        gu_ref[:, FFN_H:] = uu.astype(BF16)
        act_ref[...] = (gg * _sig(gg) * uu).astype(BF16)

    return _rowcall("f_ffn_up", body, n, tm, [(x2, 'row'), (g_pre, 'full'), (w_gu, 'full')],
                    [('row', D, BF16), ('row', 2 * FFN_H, BF16), ('row', FFN_H, BF16)])


def _f_ffn_down(act, x2, tgt, w_down, g_post, tm):
    n = x2.shape[0]

    def body(a_ref, x_ref, t_ref, w_ref, g_ref, dn_ref, d3_ref, loss_ref):
        @pl.when(pl.program_id(0) == 0)
        def _():
            loss_ref[...] = jnp.zeros_like(loss_ref)
        dn = _dot(a_ref[...], w_ref[...])
        dn_ref[...] = dn
        err = x_ref[...] + _rms_fwd(dn, g_ref[...]) - t_ref[...]
        d3_ref[...] = err * (1.0 / D)
        loss_ref[...] += jnp.sum(err * err, axis=0, keepdims=True)

    return _rowcall("f_ffn_down", body, n, tm,
                    [(act, 'row'), (x2, 'row'), (tgt, 'row'), (w_down, 'full'), (g_post, 'full')],
                    [('row', D, F32), ('row', D, F32), ('acc', (1, D), F32)])


def _b_ffn_down(d3, dn, gu, g_post, w_down, tm):
    n = d3.shape[0]

    def body(d3_ref, dn_ref, gu_ref, g_ref, w_ref, ddn_ref, dgu_ref, dg_ref):
        @pl.when(pl.program_id(0) == 0)
        def _():
            dg_ref[...] = jnp.zeros_like(dg_ref)
        ddn, dg = _rms_bwd(dn_ref[...], g_ref[...], d3_ref[...])
        dg_ref[...] += dg
        ddn = ddn.astype(BF16)
        ddn_ref[...] = ddn
        dact = _dot_nt(ddn, w_ref[...]).astype(BF16)
        gg = gu_ref[:, :FFN_H]
        uu = gu_ref[:, FFN_H:]
        sg = _sig(gg)
        silu = gg * sg
        dgu_ref[:, :FFN_H] = dact * uu * (sg + silu * (1.0 - sg))
        dgu_ref[:, FFN_H:] = dact * silu

    return _rowcall("b_ffn_down", body, n, tm,
                    [(d3, 'row'), (dn, 'row'), (gu, 'row'), (g_post, 'full'), (w_down, 'full')],
                    [('row', D, BF16), ('row', 2 * FFN_H, BF16), ('acc', (1, D), F32)])


def _b_ffn_up(dgu, d3, x2, w_gu, g_pre, tm):
    n = d3.shape[0]

    def body(dgu_ref, d3_ref, x_ref, w_ref, g_ref, dx_ref, dg_ref):
        @pl.when(pl.program_id(0) == 0)
        def _():
            dg_ref[...] = jnp.zeros_like(dg_ref)
        dhf = _dot_nt(dgu_ref[...], w_ref[...])
        dx, dg = _rms_bwd(x_ref[...], g_ref[...], dhf)
        dg_ref[...] += dg
        dx_ref[...] = d3_ref[...] + dx

    return _rowcall("b_ffn_up", body, n, tm,
                    [(dgu, 'row'), (d3, 'row'), (x2, 'row'), (w_gu, 'full'), (g_pre, 'full')],
                    [('row', D, F32), ('acc', (1, D), F32)])


def _b_attn(dx2, ao, x1, q, g_post, w_o, w_q, g_pre, k, v, tm, tpb, mlen):
    n = dx2.shape[0]
    nb = n // (tm * tpb)
    kv_spec = pl.BlockSpec((mlen, D), lambda i: (i // tpb, 0))
    scale = HEAD_DIM ** -0.5

    def body(dx2_ref, ao_ref, x1_ref, q_ref, go_ref, wo_ref, wq_ref, gp_ref, k_ref, v_ref,
             dao_ref, dq_ref, dx1_ref, dk_ref, dv_ref, dgo_ref, dgp_ref):
        i = pl.program_id(0)

        @pl.when(i == 0)
        def _():
            dgo_ref[...] = jnp.zeros_like(dgo_ref)
            dgp_ref[...] = jnp.zeros_like(dgp_ref)

        @pl.when(i % tpb == 0)
        def _():
            dk_ref[...] = jnp.zeros_like(dk_ref)
            dv_ref[...] = jnp.zeros_like(dv_ref)

        dx2v = dx2_ref[...]
        dao, dgo = _rms_bwd(ao_ref[...], go_ref[...], dx2v)
        dgo_ref[...] += dgo
        dao = dao.astype(BF16)
        dao_ref[...] = dao
        do = _dot_nt(dao, wo_ref[...])
        for h in range(HEADS):
            hs = slice(h * HEAD_DIM, (h + 1) * HEAD_DIM)
            qh = q_ref[:, hs]
            kh = k_ref[:, hs]
            p = _softmax(_dot_nt(qh, kh) * scale)
            doh = do[:, hs].astype(BF16)
            dp = _dot_nt(doh, v_ref[:, hs])
            ds = (p * (dp - jnp.sum(dp * p, axis=-1, keepdims=True)) * scale).astype(BF16)
            dq_ref[:, hs] = _dot(ds, kh).astype(BF16)
            dk_ref[:, hs] += _dot_tn(ds, qh)
            dv_ref[:, hs] += _dot_tn(p.astype(BF16), doh)
        dhc = _dot_nt(dq_ref[...], wq_ref[...])
        dx, dgp = _rms_bwd(x1_ref[...], gp_ref[...], dhc)
        dgp_ref[...] += dgp
        dx1_ref[...] = dx2v + dx

    return _rowcall("b_attn", body, n, tm,
                    [(dx2, 'row'), (ao, 'row'), (x1, 'row'), (q, 'row'), (g_post, 'full'), (w_o, 'full'),
                     (w_q, 'full'), (g_pre, 'full'), (k, kv_spec), (v, kv_spec)],
                    [('row', D, BF16), ('row', D, BF16), ('row', D, F32),
                     (kv_spec, (nb * mlen, D), F32), (kv_spec, (nb * mlen, D), F32),
                     ('acc', (1, D), F32), ('acc', (1, D), F32)])


def _b_mem(dk, dv, mem, w_kv, g_mem, tm):
    n = mem.shape[0]

    def body(dk_ref, dv_ref, m_ref, w_ref, g_ref, dkv_ref, dg_ref):
        @pl.when(pl.program_id(0) == 0)
        def _():
            dg_ref[...] = jnp.zeros_like(dg_ref)
        dkb = dk_ref[...].astype(BF16)
        dvb = dv_ref[...].astype(BF16)
        dkv_ref[:, :D] = dkb
        dkv_ref[:, D:] = dvb
        dmn = _dot_nt(dkb, w_ref[:, :D]) + _dot_nt(dvb, w_ref[:, D:])
        _, dg = _rms_bwd(m_ref[...], g_ref[...], dmn)
        dg_ref[...] += dg

    return _rowcall("b_mem", body, n, tm, [(dk, 'row'), (dv, 'row'), (mem, 'row'), (w_kv, 'full'), (g_mem, 'full')],
                    [('row', 2 * D, BF16), ('acc', (1, D), F32)])


def _b_mix(dx1, mo, a, b, zg, ypre, g_post, w_mix_out, w_br_gm, w_br_s5, w_glu, tm):
    n = dx1.shape[0]

    def body(dx1_ref, mo_ref, a_ref, b_ref, zg_ref, ypre_ref, g_ref, wout_ref, wgm_ref, ws5_ref, wglu_ref,
             dmo_ref, da_ref, db_ref, dzg_ref, dygm_ref, dgate_ref, dypre_ref, dg_ref):
        @pl.when(pl.program_id(0) == 0)
        def _():
            dg_ref[...] = jnp.zeros_like(dg_ref)
        dmo, dg = _rms_bwd(mo_ref[...], g_ref[...], dx1_ref[...])
        dg_ref[...] += dg
        dmo = dmo.astype(BF16)
        dmo_ref[...] = dmo
        dmg = _dot_nt(dmo, wout_ref[...]).astype(BF16)
        zg = zg_ref[...].astype(BF16)
        sa, sb = _sig(zg[:, :D]), _sig(zg[:, D:])
        da = dmg * sa
        db = dmg * sb
        da_ref[...] = da
        db_ref[...] = db
        dzg_ref[:, :D] = dmg * a_ref[...] * (sa * (1.0 - sa))
        dzg_ref[:, D:] = dmg * b_ref[...] * (sb * (1.0 - sb))
        dygm_ref[...] = _dot_nt(da, wgm_ref[...])
        dys5 = _dot_nt(db, ws5_ref[...])
        ypre = ypre_ref[...]
        yg = _gelu(ypre)
        sgt = _sig(_dot(yg.astype(BF16), wglu_ref[...]))
        dgate = (dys5 * yg * sgt * (1.0 - sgt)).astype(BF16)
        dgate_ref[...] = dgate
        dyg = dys5 * sgt + _dot_nt(dgate, wglu_ref[...])
        dypre_ref[...] = dyg * _gelu_grad(ypre)

    return _rowcall("b_mix", body, n, tm,
                    [(dx1, 'row'), (mo, 'row'), (a, 'row'), (b, 'row'), (zg, 'row'), (ypre, 'row'), (g_post, 'full'),
                     (w_mix_out, 'full'), (w_br_gm, 'full'), (w_br_s5, 'full'), (w_glu, 'full')],
                    [('row', D, BF16), ('row', D, BF16), ('row', D, BF16), ('row', 2 * D, BF16), ('row', D, F32),
                     ('row', S5_W, BF16), ('row', S5_W, F32), ('acc', (1, D), F32)])


def _b_s5(zs5, dypre, states, bbc, ccm, avec, dvec, nb, t, comm):
    n = zs5.shape[0]
    nc = n // nb // t
    seg = t // 8
    slab, bb, cm, av, dv, st = _s5_specs(nb, nc, t, True)

    def body(u_ref, dy_ref, st_ref, bb_ref, cc_ref, a_ref, d_ref,
             du_ref, dbb_ref, dcc_ref, da_ref, dd_ref, lcarry, s_ref, l_ref):
        first = jnp.logical_and(pl.program_id(1) == 0, pl.program_id(2) == 0)

        @pl.when(first)
        def _():
            dbb_ref[...] = jnp.zeros_like(dbb_ref)
            dcc_ref[...] = jnp.zeros_like(dcc_ref)
            da_ref[...] = jnp.zeros_like(da_ref)
            dd_ref[...] = jnp.zeros_like(dd_ref)

        @pl.when(pl.program_id(2) == 0)
        def _():
            lcarry[...] = jnp.zeros_like(lcarry)

        up, dyp = _seg_load(u_ref, seg), _seg_load(dy_ref, seg)
        ub, dyb = up.astype(BF16), dyp.astype(BF16)
        a1r, a1i = a_ref[:, :512], a_ref[:, 512:]
        ar, ai = jnp.broadcast_to(a1r, (8, 512)), jnp.broadcast_to(a1i, (8, 512))
        pr, pi = _seg_power(a1r, a1i, seg)

        s_ref[...] = _dot(ub, bb_ref[...]).reshape(seg, 8, 1024)
        tr, ti = _seg_scan(s_ref, ar, ai, seg, False)
        er, ei, _ = _seg_entries(tr, ti, pr, pi, st_ref[...], False)

        def apply(k, pe):
            qr, qi = pe
            qr, qi = ar * qr - ai * qi, ar * qi + ai * qr
            s_ref[k, :, :512] = s_ref[k, :, :512] + qr
            s_ref[k, :, 512:] = s_ref[k, :, 512:] + qi
            return qr, qi

        lax.fori_loop(0, seg, apply, (er, ei), unroll=4)

        l_ref[...] = _dot_nt(dyb, cc_ref[...]).reshape(seg, 8, 1024)
        tr, ti = _seg_scan(l_ref, ar, -ai, seg, True)
        fr, fi, lout = _seg_entries(tr, ti, pr, -pi, lcarry[...], True)
        lcarry[...] = lout

        def apply_back(i, carry):
            qr, qi, accr, acci = carry
            k = seg - 1 - i
            qr, qi = ar * qr + ai * qi, ar * qi - ai * qr
            lr = l_ref[k, :, :512] + qr
            li = l_ref[k, :, 512:] + qi
            l_ref[k, :, :512] = lr
            l_ref[k, :, 512:] = li
            kp = jnp.maximum(k - 1, 0)
            sr = jnp.where(k == 0, er, s_ref[kp, :, :512])
            si = jnp.where(k == 0, ei, s_ref[kp, :, 512:])
            return qr, qi, accr + lr * sr + li * si, acci + li * sr - lr * si

        zero = jnp.zeros((8, 512), F32)
        _, _, accr, acci = lax.fori_loop(0, seg, apply_back, (fr, fi, zero, zero), unroll=4)
        da_ref[:, :512] += jnp.sum(accr, axis=0, keepdims=True)
        da_ref[:, 512:] += jnp.sum(acci, axis=0, keepdims=True)

        s = s_ref[...].reshape(t, 1024).astype(BF16)
        lam = l_ref[...].reshape(t, 1024).astype(BF16)
        dcc_ref[...] += _dot_tn(dyb, s)
        dbb_ref[...] += _dot_tn(ub, lam)
        _seg_store(du_ref, _dot_nt(lam, bb_ref[...]) + d_ref[...] * dyp, seg)
        dd_ref[...] += jnp.sum(dyp * up, axis=0, keepdims=True)

    return _hosting_call(
        body, "b_s5", (S5_SUPER, nb, nc), [slab, slab, st, bb, cm, av, dv], [slab, bb, bb, av, dv],
        [jax.ShapeDtypeStruct((n, S5_W), F32), jax.ShapeDtypeStruct(bbc.shape, F32),
         jax.ShapeDtypeStruct(bbc.shape, F32), jax.ShapeDtypeStruct(avec.shape, F32),
         jax.ShapeDtypeStruct(dvec.shape, F32)],
        [pltpu.VMEM((1, 1024), F32), pltpu.VMEM((seg, 8, 1024), F32), pltpu.VMEM((seg, 8, 1024), F32)],
        ("arbitrary", "arbitrary", "arbitrary"), (zs5, dypre, states, bbc, ccm, avec, dvec), comm)


def _b_gmlp(zgm, dygm, ln_g, ln_b, w_s, b_s, tm, comm):
    n = zgm.shape[0]

    def body(z_ref, dy_ref, lg_ref, lb_ref, ws_ref, bs_ref, dz_ref, dws_ref, dbs_ref, dlg_ref, dlb_ref,
             du_s, dvn_s):
        @pl.when(pl.program_id(0) == 0)
        def _():
            dws_ref[...] = jnp.zeros_like(dws_ref)
            dbs_ref[...] = jnp.zeros_like(dbs_ref)
            dlg_ref[...] = jnp.zeros_like(dlg_ref)
            dlb_ref[...] = jnp.zeros_like(dlb_ref)
        z = z_ref[...].astype(BF16)
        zg = _gelu(z)
        u = zg[:, :D]
        vh, r = _ln_stats(zg[:, D:].astype(F32))
        vn = (vh * lg_ref[...] + lb_ref[...]).astype(BF16)
        dy = dy_ref[...]
        keep = _tril()
        for g in range(GM_GROUPS):
            w = jnp.where(keep, ws_ref[g], 0.0).astype(BF16)
            cs = slice(g * LANE, (g + 1) * LANE)
            for c in range(tm // GM_CHUNK):
                rs = slice(c * GM_CHUNK, (c + 1) * GM_CHUNK)
                vb = vn[rs, cs]
                sv = _dot(w, vb) + bs_ref[g]
                dyb = dy[rs, cs]
                du_s[rs, cs] = dyb * sv
                dsv = dyb * u[rs, cs]
                dsvb = dsv.astype(BF16)
                dvn_s[rs, cs] = _dot_tn(w, dsvb)
                dws_ref[g] += jnp.where(keep, _dot_nt(dsvb, vb), 0.0)
                dbs_ref[g] += jnp.sum(dsv, axis=1, keepdims=True)
        dvn = dvn_s[...]
        dlg_ref[...] += jnp.sum(dvn * vh, axis=0, keepdims=True)
        dlb_ref[...] += jnp.sum(dvn, axis=0, keepdims=True)
        dvh = dvn * lg_ref[...]
        dv = r * (dvh - jnp.mean(dvh, axis=-1, keepdims=True) - vh * jnp.mean(dvh * vh, axis=-1, keepdims=True))
        dz_ref[:, :D] = du_s[...].astype(BF16) * _gelu_grad(z[:, :D])
        dz_ref[:, D:] = dv.astype(BF16) * _gelu_grad(z[:, D:])

    return _rowcall("b_gmlp", body, n, tm,
                    [(zgm, 'row'), (dygm, 'row'), (ln_g, 'full'), (ln_b, 'full'), (w_s, 'full'), (b_s, 'full')],
                    [('row', 2 * D, BF16), ('acc', w_s.shape, F32), ('acc', b_s.shape, F32), ('acc', (1, D), F32),
                     ('acc', (1, D), F32)],
                    scratch=[pltpu.VMEM((tm, D), F32), pltpu.VMEM((tm, D), F32)], comm=comm)


def _b_in(dzgm, dzs5, dzg, dx1, x, w_in, g_pre, tm, comm):
    n = x.shape[0]

    def body(d1_ref, d2_ref, d3_ref, dx1_ref, x_ref, w_ref, g_ref, gx_ref, dg_ref):
        @pl.when(pl.program_id(0) == 0)
        def _():
            dg_ref[...] = jnp.zeros_like(dg_ref)
        dh = (_dot_nt(d1_ref[...], w_ref[:, 0:2 * D]) + _dot_nt(d2_ref[...].astype(BF16), w_ref[:, 2 * D:2 * D + S5_W])
              + _dot_nt(d3_ref[...], w_ref[:, 2 * D + S5_W:]))
        dx, dg = _rms_bwd(x_ref[...], g_ref[...], dh)
        dg_ref[...] += dg
        gx_ref[...] = dx1_ref[...] + dx

    return _rowcall("b_in", body, n, tm,
                    [(dzgm, 'row'), (dzs5, 'row'), (dzg, 'row'), (dx1, 'row'), (x, 'row'), (w_in, 'full'),
                     (g_pre, 'full')],
                    [('row', D, F32), ('acc', (1, D), F32)], comm=comm)


def _whole(name, body, ins, outs):
    return pl.pallas_call(body, name=name, out_shape=[jax.ShapeDtypeStruct(s, dt) for s, dt in outs],
                          compiler_params=_params())(*ins)


def _s5_disc_fwd(lr, li, ls, br, bi):
    def body(lr_ref, li_ref, ls_ref, br_ref, bi_ref, o1, o2, o3, o4):
        outs = _s5_disc(lr_ref[...], li_ref[...], ls_ref[...], br_ref[...], bi_ref[...])
        for o, val in zip((o1, o2, o3, o4), outs):
            o[...] = val

    return _whole("s5_disc_fwd", body, [lr, li, ls, br, bi],
                  [(lr.shape, F32), (lr.shape, F32), (br.shape, F32), (br.shape, F32)])


def _s5_disc_bwd(lr, li, ls, br, bi, cts):
    def body(lr_ref, li_ref, ls_ref, br_ref, bi_ref, c1, c2, c3, c4, o1, o2, o3, o4, o5):
        _, vjp = jax.vjp(_s5_disc, lr_ref[...], li_ref[...], ls_ref[...], br_ref[...], bi_ref[...])
        grads = vjp((c1[...], c2[...], c3[...], c4[...]))
        for o, val in zip((o1, o2, o3, o4, o5), grads):
            o[...] = val

    return _whole("s5_disc_bwd", body, [lr, li, ls, br, bi, *cts],
                  [(lr.shape, F32), (lr.shape, F32), (lr.shape, F32), (br.shape, F32), (br.shape, F32)])


def _sum_slots(name, slots):
    def body(s_ref, o_ref):
        acc = s_ref[0]
        for k in range(1, slots.shape[0]):
            acc = acc + s_ref[k]
        o_ref[...] = acc

    return _whole(name, body, [slots], [(slots.shape[1:], F32)])[0]


def _adamw_math(w, g, m, v):
    c1 = 1.0 - ADAM_B1 ** ADAM_STEP
    c2 = 1.0 - ADAM_B2 ** ADAM_STEP
    mn = ADAM_B1 * m + (1.0 - ADAM_B1) * g
    vn = ADAM_B2 * v + (1.0 - ADAM_B2) * (g * g)
    return -ADAM_LR * ((mn / c1) / (jnp.sqrt(vn / c2) + ADAM_EPS) + ADAM_WD * w), mn, vn


def _adamw_many(name, ws, gs, ms, vs):
    k = len(ws)

    def body(*refs):
        for i in range(k):
            dl, mn, vn = _adamw_math(refs[i][...], refs[k + i][...], refs[2 * k + i][...], refs[3 * k + i][...])
            refs[4 * k + i][...] = dl
            refs[5 * k + i][...] = mn
            refs[6 * k + i][...] = vn

    res = _whole(name, body, [*ws, *gs, *ms, *vs], [(t.shape, F32) for t in ws] * 3)
    return res[:k], res[k:2 * k], res[2 * k:]


def _adamw(name, w, g, m, v, tm):
    def body(w_ref, g_ref, m_ref, v_ref, d_ref, mo_ref, vo_ref):
        d_ref[...], mo_ref[...], vo_ref[...] = _adamw_math(w_ref[...], g_ref[...], m_ref[...], v_ref[...])

    cols = w.shape[1]
    return _rowcall(name, body, w.shape[0], tm, [(w, 'row'), (g, 'row'), (m, 'row'), (v, 'row')],
                    [('row', cols, F32), ('row', cols, F32), ('row', cols, F32)])


def _picked_rowcall(name, body, grid, in_specs, out_cols, tm):
    return pl.pallas_call(
        lambda p_ref, *refs: body(*refs), name=name,
        grid_spec=pltpu.PrefetchScalarGridSpec(
            num_scalar_prefetch=1, grid=(grid,), in_specs=in_specs,
            out_specs=[pl.BlockSpec((tm, D), lambda i, p: (i, 0)) for _ in out_cols]),
        out_shape=[jax.ShapeDtypeStruct((grid * tm, D), dt) for dt in out_cols],
        compiler_params=_params(("arbitrary",)),
    )


def _exchange(copies):
    def build(src_refs, out_refs, ss, rs):
        def rdma(k, src, dst, peer):
            return _Lazy(lambda: pltpu.make_async_remote_copy(
                src_ref=src, dst_ref=dst, send_sem=ss.at[k], recv_sem=rs.at[k], device_id=peer, device_id_type=MESH))
        return copies(src_refs, out_refs, rdma)

    def start(*refs):
        for cp in build(*refs):
            cp.start()

    def finish(*refs):
        for cp in build(*refs):
            cp.wait()

    return start, finish


def _reduce_big(tag, slabs):
    rows = [s.shape[1] for s in slabs]
    blk = sum(rows) // 4
    tm = _tile_rows(blk, 512)
    per = blk // tm
    gpack = jnp.concatenate([s.reshape(4, 2, 2, r // 4, D) for s, r in zip(slabs, rows)], axis=3)

    mx, my, mc = lax.axis_index("x"), lax.axis_index("y"), lax.axis_index("c")
    chip_me, chip_xn, chip_yn = 2 * mx + my, 2 * (1 - mx) + my, 2 * mx + (1 - my)

    def pair_copies(srcs, dsts, rdma):
        c, _, (_, _, psib) = _place()
        return [rdma(0, srcs[0].at[:, 1 - c], dsts[0], psib)]

    from_sib = (yield _Comm([gpack], [((4, 2, blk, D), F32)], 1, _exchange(pair_copies), {}, (0.0, 1.0)))[0]

    def sum_both(a_ref, b_ref, o32_ref, o16_ref):
        s = a_ref[...].astype(F32) + b_ref[...].astype(F32)
        o32_ref[...] = s
        o16_ref[...] = s.astype(BF16)

    p32, p16 = _picked_rowcall(
        "sum_pair_" + tag, sum_both, 8 * per,
        [pl.BlockSpec((None, None, None, tm, D), lambda i, p: (i // (2 * per), p[0], (i // per) % 2, i % per, 0)),
         pl.BlockSpec((tm, D), lambda i, p: (i, 0))],
        [F32, BF16], tm)(jnp.stack([mc]).astype(jnp.int32), gpack, from_sib.reshape(8 * blk, D))
    p32, p16 = p32.reshape(4, 2, blk, D), p16.reshape(4, 2, blk, D)

    def step1_copies(srcs, dsts, rdma):
        _, (me, xn, yn, dg), (px, py, _) = _place()
        s16 = srcs[0]
        return [rdma(0, s16.at[yn, 0], dsts[0].at[0], py), rdma(1, s16.at[dg, 0], dsts[0].at[1], py),
                rdma(2, s16.at[xn, 1], dsts[0].at[2], px), rdma(3, s16.at[dg, 1], dsts[0].at[3], px)]

    recv1 = (yield _Comm([p16], [((4, blk, D), BF16)], 4, _exchange(step1_copies), {}, (0.0, 1.0)))[0]

    s32, s16 = _picked_rowcall(
        "sum_step1_" + tag, sum_both, 4 * per,
        [pl.BlockSpec((None, None, tm, D), lambda i, p: (p[i // per], i // (2 * per), i % per, 0)),
         pl.BlockSpec((tm, D), lambda i, p: (i, 0))],
        [F32, BF16], tm)(jnp.stack([chip_me, chip_xn, chip_me, chip_yn]).astype(jnp.int32), p32,
                         recv1.reshape(4 * blk, D))

    def step2_copies(srcs, dsts, rdma):
        _, _, (px, py, _) = _place()
        return [rdma(0, srcs[0].at[1], dsts[0].at[0], px), rdma(1, srcs[0].at[3], dsts[0].at[1], py)]

    recv2 = (yield _Comm([s16.reshape(4, blk, D)], [((2, blk, D), BF16)], 2, _exchange(step2_copies), {},
                         (0.0, 1.0)))[0]

    def sum_step2(a_ref, b_ref, o_ref):
        o_ref[...] = a_ref[...] + b_ref[...].astype(F32)

    red = _rowcall("sum_step2_" + tag, sum_step2, 2 * blk, tm,
                   [(s32.reshape(4, blk, D), pl.BlockSpec((None, tm, D), lambda i: (2 * (i // per), i % per, 0))),
                    (recv2.reshape(2 * blk, D), 'row')],
                   [('row', D, F32)])[0].reshape(2, blk, D)

    def share_copies(srcs, dsts, rdma):
        _, _, (_, _, psib) = _place()
        return [rdma(0, srcs[0], dsts[0], psib)]

    other = (yield _Comm([red], [((2, blk, D), F32)], 1, _exchange(share_copies), {}, (0.0, 1.0)))[0]
    lo = jnp.where(mc == 0, red, other)
    hi = jnp.where(mc == 0, other, red)
    out, off = [], 0
    for r in rows:
        q = r // 4
        out.append(jnp.concatenate([lo[0, off:off + q], lo[1, off:off + q], hi[0, off:off + q], hi[1, off:off + q]]))
        off += q
    return out


def _rows1024(a):
    flat = a.reshape(-1)
    pad = (-flat.shape[0]) % D
    if pad:
        flat = jnp.concatenate([flat, jnp.zeros((pad,), flat.dtype)])
    return flat.reshape(-1, D)


def _pack(arrs, pad_rows_to=8):
    parts = [_rows1024(a) for a in arrs]
    rows = sum(p.shape[0] for p in parts)
    pad = (-rows) % pad_rows_to
    if pad:
        parts.append(jnp.zeros((pad, D), parts[0].dtype))
    return jnp.concatenate(parts, axis=0)


def _unpack(packed, shapes):
    out, r = [], 0
    for s in shapes:
        size = math.prod(s)
        nr = -(-size // D)
        out.append(packed[r:r + nr].reshape(-1)[:size].reshape(s))
        r += nr
    return out


def _block_diag(t):
    eye = jnp.eye(8, dtype=t.dtype)
    j, g, a, b = t.shape
    return (t[:, :, :, None, :] * eye[None, :, None, :, None]).reshape(j, g * a, g * b)


def _block_diag_take(m, a, b):
    eye = jnp.eye(8, dtype=m.dtype)
    return (m.reshape(4, 8, a, 8, b) * eye[None, :, None, :, None]).sum(axis=3)


def kernel(x, mem, g_mix_pre, w_in, gm_ln_g, gm_ln_b, gm_w_s, gm_b_s, s5_lam_re, s5_lam_im, s5_log_step, s5_b_re, s5_b_im, s5_c_re, s5_c_im, s5_d, s5_w_glu, w_br_gm, w_br_s5, w_mix_out, g_mix_post, g_ca_pre, g_mem, ca_w_q, ca_w_kv, ca_w_o, g_ca_post, g_ffn_pre, ffn_w_gu, ffn_w_down, g_ffn_post, loss_target, m_g_mix_pre, m_w_in, m_gm_ln_g, m_gm_ln_b, m_gm_w_s, m_gm_b_s, m_s5_lam_re, m_s5_lam_im, m_s5_log_step, m_s5_b_re, m_s5_b_im, m_s5_c_re, m_s5_c_im, m_s5_d, m_s5_w_glu, m_w_br_gm, m_w_br_s5, m_w_mix_out, m_g_mix_post, m_g_ca_pre, m_g_mem, m_ca_w_q, m_ca_w_kv, m_ca_w_o, m_g_ca_post, m_g_ffn_pre, m_ffn_w_gu, m_ffn_w_down, m_g_ffn_post, v_g_mix_pre, v_w_in, v_gm_ln_g, v_gm_ln_b, v_gm_w_s, v_gm_b_s, v_s5_lam_re, v_s5_lam_im, v_s5_log_step, v_s5_b_re, v_s5_b_im, v_s5_c_re, v_s5_c_im, v_s5_d, v_s5_w_glu, v_w_br_gm, v_w_br_s5, v_w_mix_out, v_g_mix_post, v_g_ca_pre, v_g_mem, v_ca_w_q, v_ca_w_kv, v_ca_w_o, v_g_ca_post, v_g_ffn_pre, v_ffn_w_gu, v_ffn_w_down, v_g_ffn_post):
    a = dict(locals())
    w = {n: a[n][0] for n in WNAMES}
    nb, seq, _ = x.shape
    n = nb * seq
    mlen = mem.shape[1]
    tm = min(256, seq)
    tmb = min(512, seq)
    tpb = seq // tmb
    xf = x.reshape(n, D)
    tgt = loss_target.reshape(n, D)
    memf = mem.reshape(nb * mlen, D)
    tmm = min(256, mlen)

    big_names = list(BIG)
    local_shapes = {k: w[k].shape for k in big_names}
    w_in_full = _run_comm("gather_w_in", _gather_weights("cast_w_in", [w['w_in']], [BIG['w_in']]))[0]
    rest = [k for k in big_names if k != 'w_in']
    gather_rest = _gather_weights("cast_w_rest", [w[k] for k in rest], [BIG[k] for k in rest])

    vec = lambda name: w[name].reshape(1, D)

    to_lane = lambda p: p.reshape(1, -1)
    b_t = lambda p: p.transpose(2, 0, 1).reshape(16, -1)
    lr_l, li_l = to_lane(w['s5_lam_re']), to_lane(w['s5_lam_im'])
    ls_l = jnp.repeat(w['s5_log_step'], 64).reshape(1, -1)
    br_t, bi_t = b_t(w['s5_b_re']), b_t(w['s5_b_im'])
    ab_re, ab_im, bb_re, bb_im = _s5_disc_fwd(lr_l, li_l, ls_l, br_t, bi_t)
    blk = lambda t: _block_diag(t.reshape(16, 4, 8, 64).transpose(1, 2, 0, 3))
    bbc = jnp.concatenate([blk(bb_re), blk(bb_im)], axis=2).astype(BF16)
    cblk = lambda c: _block_diag(c.reshape(4, 8, 16, 64).transpose(0, 1, 3, 2))
    ccm = jnp.concatenate([cblk(w['s5_c_re']), -cblk(w['s5_c_im'])], axis=1).astype(BF16)
    avec = jnp.concatenate([ab_re.reshape(4, 1, 512), ab_im.reshape(4, 1, 512)], axis=2)
    dvec = w['s5_d'].reshape(4, 1, LANE)

    bs3 = w['gm_b_s'].reshape(GM_GROUPS, GM_CHUNK, 1)
    wf = {'w_in': w_in_full}
    h, zgm, zs5, zg = _f_in(xf, vec('g_mix_pre'), wf['w_in'], tmb)
    ygm = _f_gmlp(zgm, vec('gm_ln_g'), vec('gm_ln_b'), w['gm_w_s'], bs3, tmb)
    s5_t = min(S5_T, seq)
    (ypre, states), gathered = _f_s5(zs5, bbc, ccm, avec, dvec, nb, s5_t, gather_rest)
    for k, g in zip(rest, gathered):
        wf[k] = g.reshape(-1, g.shape[-1]) if BIG[k] == 0 else g
    yg, ys5, a_br, b_br, merged, mo, x1 = _f_mix(ygm, ypre, zg, xf, wf['s5_w_glu'], wf['w_br_gm'], wf['w_br_s5'],
                                                 wf['w_mix_out'], vec('g_mix_post'), tmb)
    mem_n, kk, vv = _f_mem(memf, vec('g_mem'), wf['ca_w_kv'], tmm)
    hc, q_att, o_att, ao, x2 = _f_attn(x1, vec('g_ca_pre'), wf['ca_w_q'], kk, vv, wf['ca_w_o'], vec('g_ca_post'),
                                tmb, tpb, mlen)
    hf, gu, act = _f_ffn_up(x2, vec('g_ffn_pre'), wf['ffn_w_gu'], tmb)
    dn, d3, loss_cols = _f_ffn_down(act, x2, tgt, wf['ffn_w_down'], vec('g_ffn_post'), tmb)

    gsm = {}
    gbig = {}
    ddn, dgu, gsm['g_ffn_post'] = _b_ffn_down(d3, dn, gu, vec('g_ffn_post'), wf['ffn_w_down'], tmb)
    dx2, gsm['g_ffn_pre'] = _b_ffn_up(dgu, d3, x2, wf['ffn_w_gu'], vec('g_ffn_pre'), tmb)
    gbig['ffn_w_down'] = _mm_tn("dw_ffn_down", act, ddn)
    gbig['ffn_w_gu'] = _mm_tn("dw_ffn_gu", dgu, hf)
    dao, dq, dx1, dk, dv, gsm['g_ca_post'], gsm['g_ca_pre'] = _b_attn(
        dx2, ao, x1, q_att, vec('g_ca_post'), wf['ca_w_o'], wf['ca_w_q'], vec('g_ca_pre'), kk, vv, tmb, tpb, mlen)
    gbig['ca_w_o'] = _mm_tn("dw_ca_o", o_att, dao)
    gbig['ca_w_q'] = _mm_tn("dw_ca_q", hc, dq)
    dkv, gsm['g_mem'] = _b_mem(dk, dv, memf, wf['ca_w_kv'], vec('g_mem'), tmm)
    gbig['ca_w_kv'] = _mm_tn("dw_ca_kv", dkv, mem_n)
    dmo, da_br, db_br, dzg, dygm, dgate, dypre, gsm['g_mix_post'] = _b_mix(
        dx1, mo, a_br, b_br, zg, ypre, vec('g_mix_post'), wf['w_mix_out'], wf['w_br_gm'], wf['w_br_s5'],
        wf['s5_w_glu'], tmb)
    gbig['w_mix_out'] = _mm_tn("dw_mix_out", merged, dmo)
    gbig['w_br_gm'] = _mm_tn("dw_br_gm", ygm, da_br)
    gbig['w_br_s5'] = _mm_tn("dw_br_s5", db_br, ys5)
    gbig['s5_w_glu'] = _mm_tn("dw_s5_glu", yg, dgate)
    slab_of = lambda k: gbig[k].reshape(4, -1, D)
    red_rest = _reduce_big("rest", [slab_of(k) for k in rest])
    (dzs5, dbbc, dccm_t, davec, ddvec), got = _b_s5(zs5, dypre, states, bbc, ccm, avec, dvec, nb, s5_t,
                                                    next(red_rest))
    dccm = dccm_t.transpose(0, 2, 1)
    (dzgm, gsm['gm_w_s'], dbs3, gsm['gm_ln_g'], gsm['gm_ln_b']), got = _b_gmlp(
        zgm, dygm, vec('gm_ln_g'), vec('gm_ln_b'), w['gm_w_s'], bs3, tmb, red_rest.send(got))
    gsm['gm_b_s'] = dbs3
    grad_x, gsm['g_mix_pre'] = _b_in(dzgm, dzs5, dzg, dx1, xf, wf['w_in'], vec('g_mix_pre'), tmb, None)
    dw_in_gm, got = _mm_tn("dw_in_gm", dzgm, h, comm=red_rest.send(got))
    dw_in_s5, got = _mm_tn("dw_in_s5", dzs5, h, comm=red_rest.send(got))
    greds = dict(zip(rest, _finish(red_rest, got)))

    unblk = lambda m_: _block_diag_take(m_, 16, 64).transpose(2, 0, 1, 3).reshape(16, -1)
    d_bb_re, d_bb_im = unblk(dbbc[:, :, :512]), unblk(dbbc[:, :, 512:])
    cunblk = lambda m_: _block_diag_take(m_, 64, 16).transpose(0, 1, 3, 2).reshape(32, 16, 64)
    gsm['s5_c_re'] = cunblk(dccm[:, :512, :])
    gsm['s5_c_im'] = -cunblk(dccm[:, 512:, :])
    d_ab_re, d_ab_im = davec[:, :, :512].reshape(1, -1), davec[:, :, 512:].reshape(1, -1)
    g_lr, g_li, g_ls, g_br, g_bi = _s5_disc_bwd(lr_l, li_l, ls_l, br_t, bi_t, (d_ab_re, d_ab_im, d_bb_re, d_bb_im))
    gsm['s5_lam_re'], gsm['s5_lam_im'] = g_lr.reshape(32, 64), g_li.reshape(32, 64)
    gsm['s5_log_step'] = g_ls.reshape(32, 64).sum(axis=1)
    from_t = lambda t: t.reshape(16, 32, 64).transpose(1, 2, 0)
    gsm['s5_b_re'], gsm['s5_b_im'] = from_t(g_br), from_t(g_bi)
    gsm['s5_d'] = ddvec.reshape(32, 16)

    small_shapes = [w[k].shape for k in SMALL]
    spack = _pack([gsm[k].reshape(w[k].shape) for k in SMALL] + [loss_cols], 8)
    me_slot = 4 * lax.axis_index("x") + 2 * lax.axis_index("y") + lax.axis_index("c")
    dw_in_g, (sall,) = _mm_tn("dw_in_g", dzg, h, comm=_gather_all(spack))
    gbig['w_in'] = jnp.concatenate([dw_in_gm, dw_in_s5, dw_in_g], axis=0)
    sall = lax.dynamic_update_index_in_dim(sall, spack, me_slot, 0)
    ssum = _sum_slots("sum_small", sall)
    small_red = _unpack(ssum, small_shapes + [(1, D)])
    loss = 0.5 * jnp.sum(small_red[-1]) / D

    red_in = _reduce_big("w_in", [slab_of('w_in')])
    got = _run_comm("reduce_pair_w_in", next(red_in))
    for phase in ("reduce_step1_w_in", "reduce_step2_w_in"):
        got = _run_comm(phase, red_in.send(got))
    got = _run_comm("share_pair_w_in", red_in.send(got))
    greds['w_in'] = _finish(red_in, got)[0]

    res_big = {}
    for k in big_names:
        r_, c_ = local_shapes[k]
        g = greds[k].reshape(r_, c_) if BIG[k] == 0 else greds[k].reshape(c_, r_).T
        dl, mn, vn = _adamw("adamw_" + k, w[k], g, a['m_' + k][0], a['v_' + k][0], _tile_rows(g.shape[0], 256))
        res_big[k] = (g, dl, mn, vn)
    as2d = lambda t: t.reshape(1, -1) if t.ndim == 1 else t
    dl_s, mn_s, vn_s = _adamw_many("adamw_small", [as2d(w[k]) for k in SMALL], [as2d(g) for g in small_red[:-1]],
                                   [as2d(a['m_' + k][0]) for k in SMALL], [as2d(a['v_' + k][0]) for k in SMALL])
    res_small = {k: (g, dl.reshape(w[k].shape), mn.reshape(w[k].shape), vn.reshape(w[k].shape))
                 for k, g, dl, mn, vn in zip(SMALL, small_red[:-1], dl_s, mn_s, vn_s)}

    res = {**res_big, **res_small}
    outs = [loss, grad_x.reshape(nb, seq, D)]
    for i in range(4):
        outs += [res[k][i][None] for k in WNAMES]
    return tuple(outs)


def _finish(gen, got):
    try:
        gen.send(got)
    except StopIteration as done:
        return done.value
    raise AssertionError("the generator has more exchanges")


def _tile_rows(rows, cap=384):
    best = rows
    for t in range(16, min(rows, cap) + 1, 16):
        if rows % t == 0:
            best = t
    return best
```

```python
import math
from typing import Any, NamedTuple

import jax
import jax.numpy as jnp
from jax import lax
from jax.experimental import pallas as pl
from jax.experimental.pallas import tpu as pltpu

F32 = jnp.float32
BF16 = jnp.bfloat16
EPS = 1e-6
D = 1024
GM_CHUNK = 128
GM_GROUPS = 8
S5_W = 512
S5_SUPER = 4
S5_T = 512
S5_TB = 1024
HEADS = 4
HEAD_DIM = 256
FFN_H = 2816
LANE = 128
VMEM_LIMIT = 56 * 1024 * 1024
MESH = pl.DeviceIdType.MESH

ADAM_LR, ADAM_B1, ADAM_B2, ADAM_EPS, ADAM_WD, ADAM_STEP = 0.001, 0.9, 0.999, 1e-08, 0.01, 10

WNAMES = ['g_mix_pre', 'w_in', 'gm_ln_g', 'gm_ln_b', 'gm_w_s', 'gm_b_s', 's5_lam_re', 's5_lam_im', 's5_log_step',
          's5_b_re', 's5_b_im', 's5_c_re', 's5_c_im', 's5_d', 's5_w_glu', 'w_br_gm', 'w_br_s5', 'w_mix_out',
          'g_mix_post', 'g_ca_pre', 'g_mem', 'ca_w_q', 'ca_w_kv', 'ca_w_o', 'g_ca_post', 'g_ffn_pre', 'ffn_w_gu',
          'ffn_w_down', 'g_ffn_post']
BIG = {'w_in': 1, 's5_w_glu': 0, 'w_br_gm': 0, 'w_br_s5': 1, 'w_mix_out': 0, 'ca_w_q': 0, 'ca_w_kv': 1,
       'ca_w_o': 0, 'ffn_w_gu': 1, 'ffn_w_down': 0}
SMALL = [n for n in WNAMES if n not in BIG]


def _dot(a, b):
    return jnp.dot(a, b, preferred_element_type=F32)


def _dot_nt(a, b):
    return lax.dot_general(a, b, (((1,), (1,)), ((), ())), preferred_element_type=F32)


def _dot_tn(a, b):
    return lax.dot_general(a, b, (((0,), (0,)), ((), ())), preferred_element_type=F32)


def _rms_fwd(x, g):
    r = lax.rsqrt(jnp.mean(x * x, axis=-1, keepdims=True) + EPS)
    return x * r * g


def _rms_bwd(x, g, dy):
    r = lax.rsqrt(jnp.mean(x * x, axis=-1, keepdims=True) + EPS)
    xh = x * r
    gdy = dy * g
    dx = r * (gdy - xh * jnp.mean(gdy * xh, axis=-1, keepdims=True))
    dg = jnp.sum(dy * xh, axis=0, keepdims=True)
    return dx, dg


_GC = math.sqrt(2.0 / math.pi)


def _gelu(x):
    return 0.5 * x * (1.0 + jnp.tanh(_GC * (x + 0.044715 * x * x * x)))


def _gelu_grad(x):
    t = jnp.tanh(_GC * (x + 0.044715 * x * x * x))
    return 0.5 * (1.0 + t) + 0.5 * x * (1.0 - t * t) * _GC * (1.0 + 3 * 0.044715 * x * x)


def _sig(x):
    return 1.0 / (1.0 + jnp.exp(-x))


def _cscan(br, bi, ar, ai, reverse):
    t = br.shape[0]
    row = lax.broadcasted_iota(jnp.int32, br.shape, 0)
    pr, pi = ar, ai
    sh = 1
    while sh < t:
        if reverse:
            keep = row < t - sh
            rr, ri = pltpu.roll(br, t - sh, 0), pltpu.roll(bi, t - sh, 0)
        else:
            keep = row >= sh
            rr, ri = pltpu.roll(br, sh, 0), pltpu.roll(bi, sh, 0)
        rr = jnp.where(keep, rr, 0.0)
        ri = jnp.where(keep, ri, 0.0)
        br, bi = br + pr * rr - pi * ri, bi + pr * ri + pi * rr
        pr, pi = pr * pr - pi * pi, 2.0 * pr * pi
        sh *= 2
    return br, bi


def _s5_disc(lr, li, ls, br, bi):
    step = jnp.exp(ls)
    mag = jnp.exp(lr * step)
    ab_re = mag * jnp.cos(li * step)
    ab_im = mag * jnp.sin(li * step)
    den = lr * lr + li * li
    nr = ab_re - 1.0
    co_re = (nr * lr + ab_im * li) / den
    co_im = (ab_im * lr - nr * li) / den
    return ab_re, ab_im, co_re * br - co_im * bi, co_re * bi + co_im * br


def _params(sem=None):
    return pltpu.CompilerParams(dimension_semantics=sem, vmem_limit_bytes=VMEM_LIMIT)


def _rowcall(name, body, n_rows, tm, ins, outs, scratch=(), comm=None):
    def spec(kind, shape):
        if kind == 'row':
            return pl.BlockSpec((tm, shape[1]), lambda i: (i, 0))
        if kind == 'full':
            nd = len(shape)
            return pl.BlockSpec(tuple(shape), lambda i: (0,) * nd, pipeline_mode=pl.Buffered(1))
        if kind == 'acc':
            nd = len(shape)
            return pl.BlockSpec(tuple(shape), lambda i: (0,) * nd)
        return kind

    in_specs = [spec(k, a.shape) for a, k in ins]
    out_shape, out_specs = [], []
    for k, s, dt in outs:
        shape = (n_rows, s) if k == 'row' else tuple(s)
        out_shape.append(jax.ShapeDtypeStruct(shape, dt))
        out_specs.append(spec(k, shape))
    args = [a for a, _ in ins]
    if comm is not None:
        return _hosting_call(body, name, (n_rows // tm,), in_specs, out_specs, out_shape, list(scratch),
                             ("arbitrary",), args, comm)
    return pl.pallas_call(
        body, name=name, grid=(n_rows // tm,), in_specs=in_specs, out_specs=out_specs, out_shape=out_shape,
        scratch_shapes=list(scratch), compiler_params=_params(("arbitrary",)),
    )(*args)


def _tile(n, cap):
    if n <= cap:
        return n
    best = LANE
    for k in range(1, n // LANE + 1):
        t = k * LANE
        if n % t == 0 and t <= cap:
            best = t
    return best


def _mm_tn(name, a, b, comm=None):
    n, k = a.shape
    m = b.shape[1]
    tk, tn, tr = _tile(k, 1536), _tile(m, 1536), min(n, 1024)

    def body(a_ref, b_ref, o_ref):
        @pl.when(pl.program_id(2) == 0)
        def _():
            o_ref[...] = jnp.zeros_like(o_ref)
        o_ref[...] += _dot_tn(a_ref[...].astype(BF16), b_ref[...].astype(BF16))

    out_spec = pl.BlockSpec((tk, tn), lambda i, j, r: (i, j))
    out_shape = (k, m)
    grid = (k // tk, m // tn, n // tr)
    in_specs = [pl.BlockSpec((tr, tk), lambda i, j, r: (r, i)), pl.BlockSpec((tr, tn), lambda i, j, r: (r, j))]
    if comm is not None:
        (res,), extra = _hosting_call(body, name, grid, in_specs, [out_spec], [jax.ShapeDtypeStruct(out_shape, F32)],
                                      [], ("arbitrary", "arbitrary", "arbitrary"), (a, b), comm)
        return res, extra
    return pl.pallas_call(
        body, name=name, grid=grid, in_specs=in_specs, out_specs=out_spec,
        out_shape=jax.ShapeDtypeStruct(out_shape, F32),
        compiler_params=_params(("parallel", "parallel", "arbitrary")),
    )(a, b)


def _comm_call(name, srcs, outs, nsem, body_fn, aliases=None):
    ns, no = len(srcs), len(outs)

    def body(*refs):
        body_fn(refs[:ns], refs[ns:ns + no], refs[ns + no], refs[ns + no + 1])

    hbm = pl.BlockSpec(memory_space=pltpu.HBM)
    return pl.pallas_call(
        body, name=name, in_specs=[hbm] * ns, out_specs=[hbm] * no,
        out_shape=[jax.ShapeDtypeStruct(s, d) for s, d in outs],
        scratch_shapes=[pltpu.SemaphoreType.DMA((nsem,)), pltpu.SemaphoreType.DMA((nsem,))],
        input_output_aliases=aliases or {},
    )(*srcs)


def _place():
    x, y, c = lax.axis_index("x"), lax.axis_index("y"), lax.axis_index("c")
    chips = (2 * x + y, 2 * (1 - x) + y, 2 * x + (1 - y), 2 * (1 - x) + (1 - y))
    peers = ((1 - x, y, c), (x, 1 - y, c), (x, y, 1 - c))
    return c, chips, peers


class _Comm(NamedTuple):
    srcs: Any
    outs: Any
    nsem: int
    stages: Any
    aliases: Any
    fracs: Any


class _Lazy:
    def __init__(self, make):
        self._make = make

    def start(self):
        self._make().start()

    def wait(self):
        self._make().wait()

    def wait_recv(self):
        self._make().wait_recv()

    def wait_send(self):
        self._make().wait_send()


def _run_comm(name, comm):
    def body_fn(srcs, dsts, ss, rs):
        for stage in comm.stages:
            stage(srcs, dsts, ss, rs)

    return _comm_call(name, comm.srcs, comm.outs, comm.nsem, body_fn, aliases=comm.aliases)


def _hosting_call(body, name, grid, in_specs, out_specs, out_shape, scratch, sem, args, comm):
    hbm = pl.BlockSpec(memory_space=pltpu.HBM)
    n_in, n_out, n_sc = len(in_specs), len(out_specs), len(scratch)
    ns, no = len(comm.srcs), len(comm.outs)
    total = math.prod(grid)
    steps = [min(total - 1, int(f * total)) for f in comm.fracs]

    def wrapped(*refs):
        ins, csrc = refs[:n_in], refs[n_in:n_in + ns]
        outs, cdst = refs[n_in + ns:n_in + ns + n_out], refs[n_in + ns + n_out:n_in + ns + n_out + no]
        rest = refs[n_in + ns + n_out + no:]
        lin = 0
        for d, size in enumerate(grid):
            lin = lin * size + pl.program_id(d)
        for stage, at in zip(comm.stages, steps):
            @pl.when(lin == at)
            def _(stage=stage):
                stage(csrc, cdst, rest[n_sc], rest[n_sc + 1])
        body(*ins, *outs, *rest[:n_sc])

    res = pl.pallas_call(
        wrapped, name=name, grid=grid, in_specs=list(in_specs) + [hbm] * ns, out_specs=list(out_specs) + [hbm] * no,
        out_shape=list(out_shape) + [jax.ShapeDtypeStruct(s, d) for s, d in comm.outs],
        scratch_shapes=list(scratch) + [pltpu.SemaphoreType.DMA((comm.nsem,)), pltpu.SemaphoreType.DMA((comm.nsem,))],
        input_output_aliases={n_in + i: n_out + o for i, o in comm.aliases.items()},
        compiler_params=_params(sem),
    )(*args, *comm.srcs)
    return res[:n_out], res[n_out:]


def _gather_weights(name, shards, axes):
    nw = len(shards)
    shapes = [s.shape for s in shards]
    outs = [((4,) + s if ax == 0 else (s[0], 4 * s[1]), BF16) for s, ax in zip(shapes, axes)]

    def win(ref, i, chip, start, rows):
        r, cols = shapes[i]
        if axes[i] == 0:
            return ref.at[chip, pl.ds(start, rows), :]
        return ref.at[pl.ds(start, rows), pl.ds(chip * cols, cols)]

    def place_body(*refs):
        srcs, dsts, bufs, sem = refs[:nw], refs[nw:2 * nw], refs[2 * nw:3 * nw], refs[3 * nw]
        me = 2 * lax.axis_index("x") + lax.axis_index("y")
        cps = []
        for i in range(nw):
            bufs[i][...] = srcs[i][...].astype(BF16)
            cps.append(pltpu.make_async_copy(bufs[i], win(dsts[i], i, me, 0, shapes[i][0]), sem.at[i]))
            cps[-1].start()
        for cp in cps:
            cp.wait()

    placed = pl.pallas_call(
        place_body, name=name, out_shape=[jax.ShapeDtypeStruct(s, d) for s, d in outs],
        in_specs=[pl.BlockSpec(memory_space=pltpu.VMEM)] * nw, out_specs=[pl.BlockSpec(memory_space=pltpu.HBM)] * nw,
        scratch_shapes=[pltpu.VMEM(s, BF16) for s in shapes] + [pltpu.SemaphoreType.DMA((nw,))],
        compiler_params=_params(),
    )(*shards)

    def copies(dsts, ss, rs):
        c, (me, xn, yn, dg), (px, py, psib) = _place()

        def rdma(k, window, peer):
            return _Lazy(lambda: pltpu.make_async_remote_copy(
                src_ref=window, dst_ref=window, send_sem=ss.at[k], recv_sem=rs.at[k], device_id=peer,
                device_id_type=MESH))

        first, later, swaps = [], [], []
        for i in range(nw):
            qr, hr, k0 = shapes[i][0] // 4, shapes[i][0] // 2, 6 * i
            q0, q1 = 2 * c * qr, (2 * c + 1) * qr
            w0, w1 = win(dsts[i], i, me, q0, qr), win(dsts[i], i, me, q1, qr)
            first.append([rdma(k0, w0, px), rdma(k0 + 1, w1, py), rdma(k0 + 2, w0, py), rdma(k0 + 3, w1, px)])
            later.append([rdma(k0 + 4, win(dsts[i], i, xn, q0, qr), py), rdma(k0 + 5, win(dsts[i], i, yn, q1, qr), px)])
            swaps.append([rdma(6 * nw + 3 * i + j, win(dsts[i], i, chip, c * hr, hr), psib)
                          for j, chip in enumerate((xn, yn, dg))])
        return first, later, swaps

    def send(srcs, dsts, ss, rs):
        for cps in copies(dsts, ss, rs)[0]:
            for cp in cps:
                cp.start()

    def forward(srcs, dsts, ss, rs):
        first, later, _ = copies(dsts, ss, rs)
        for i in range(nw):
            first[i][0].wait_recv()
            later[i][0].start()
            first[i][1].wait_recv()
            later[i][1].start()

    def swap(srcs, dsts, ss, rs):
        first, later, swaps = copies(dsts, ss, rs)
        for i in range(nw):
            for cp in first[i][2:] + later[i]:
                cp.wait_recv()
            for sw in swaps[i]:
                sw.start()

    def finish(srcs, dsts, ss, rs):
        first, later, swaps = copies(dsts, ss, rs)
        for i in range(nw):
            for sw in swaps[i]:
                sw.wait()
            for cp in first[i] + later[i]:
                cp.wait_send()

    return _Comm(placed, outs, 9 * nw, [send, forward, swap, finish], {i: i for i in range(nw)},
                 (0.0, 0.55, 0.85, 1.0))


def _gather_all(src):
    def copies(srcs, dsts, ss, rs):
        x, y, c = lax.axis_index("x"), lax.axis_index("y"), lax.axis_index("c")
        me, sib = (x, y, c), (x, y, 1 - c)
        chips = [(1 - x, y), (x, 1 - y), (1 - x, 1 - y)]

        def rows(px, py, pc):
            return dsts[0].at[4 * px + 2 * py + pc]

        def copy(k, block, to, src=None):
            return _Lazy(lambda: pltpu.make_async_remote_copy(
                src_ref=rows(*block) if src is None else src, dst_ref=rows(*block), send_sem=ss.at[k],
                recv_sem=rs.at[k], device_id=to, device_id_type=MESH))

        first = [copy(0, me, sib, src=srcs[0])] + [copy(1 + j, me, (*chip, c), src=srcs[0])
                                                   for j, chip in enumerate(chips)]
        passed = [copy(4 + j, (*chip, c), sib) for j, chip in enumerate(chips)]
        landed = [copy(0, sib, me)] + [copy(1 + j, (*chip, c), me) for j, chip in enumerate(chips)]
        landed += [copy(4 + j, (*chip, 1 - c), me) for j, chip in enumerate(chips)]
        return first, passed, landed

    def send(*refs):
        for cp in copies(*refs)[0]:
            cp.start()

    def forward(*refs):
        _, passed, landed = copies(*refs)
        for j in range(3):
            landed[1 + j].wait_recv()
            passed[j].start()

    def finish(*refs):
        first, passed, landed = copies(*refs)
        landed[0].wait_recv()
        for cp in landed[4:]:
            cp.wait_recv()
        for cp in first + passed:
            cp.wait_send()

    return _Comm([src], [((8,) + src.shape, src.dtype)], 7, [send, forward, finish], {}, (0.0, 0.9, 1.0))


def _f_in(x, g, w_in, tm):
    n = x.shape[0]

    def body(x_ref, g_ref, w_ref, h_ref, zgm_ref, zs5_ref, zg_ref):
        h = _rms_fwd(x_ref[...], g_ref[...]).astype(BF16)
        h_ref[...] = h
        zgm_ref[...] = _dot(h, w_ref[:, 0:2 * D])
        zs5_ref[...] = _dot(h, w_ref[:, 2 * D:2 * D + S5_W])
        zg_ref[...] = _dot(h, w_ref[:, 2 * D + S5_W:])

    return _rowcall("f_in", body, n, tm, [(x, 'row'), (g, 'full'), (w_in, 'full')],
                    [('row', D, BF16), ('row', 2 * D, F32), ('row', S5_W, F32), ('row', 2 * D, F32)])


def _tril():
    r = lax.broadcasted_iota(jnp.int32, (GM_CHUNK, GM_CHUNK), 0)
    c = lax.broadcasted_iota(jnp.int32, (GM_CHUNK, GM_CHUNK), 1)
    return r >= c


def _ln_stats(v):
    mu = jnp.mean(v, axis=-1, keepdims=True)
    vc = v - mu
    r = lax.rsqrt(jnp.mean(vc * vc, axis=-1, keepdims=True) + EPS)
    return vc * r, r


def _f_gmlp(zgm, ln_g, ln_b, w_s, b_s, tm):
    n = zgm.shape[0]

    def body(z_ref, lg_ref, lb_ref, ws_ref, bs_ref, y_ref):
        zg = _gelu(z_ref[...].astype(BF16))
        u = zg[:, :D]
        vh, _ = _ln_stats(zg[:, D:].astype(F32))
        vn = (vh * lg_ref[...] + lb_ref[...]).astype(BF16)
        keep = _tril()
        for g in range(GM_GROUPS):
            w = jnp.where(keep, ws_ref[g], 0.0).astype(BF16)
            cs = slice(g * LANE, (g + 1) * LANE)
            for c in range(tm // GM_CHUNK):
                rs = slice(c * GM_CHUNK, (c + 1) * GM_CHUNK)
                sv = _dot(w, vn[rs, cs]) + bs_ref[g]
                y_ref[rs, cs] = u[rs, cs] * sv.astype(BF16)

    return _rowcall("f_gmlp", body, n, tm,
                    [(zgm, 'row'), (ln_g, 'full'), (ln_b, 'full'), (w_s, 'full'), (b_s, 'full')],
                    [('row', D, BF16)])[0]


def _s5_specs(nb, nc, t, rev):
    def cc(c):
        return nc - 1 - c if rev else c
    slab = pl.BlockSpec((t, LANE), lambda j, b, c: (b * nc + cc(c), j))
    bb = pl.BlockSpec((None, LANE, 1024), lambda j, b, c: (j, 0, 0))
    cm = pl.BlockSpec((None, 1024, LANE), lambda j, b, c: (j, 0, 0))
    av = pl.BlockSpec((None, 1, 1024), lambda j, b, c: (j, 0, 0))
    dv = pl.BlockSpec((None, 1, LANE), lambda j, b, c: (j, 0, 0))
    st = pl.BlockSpec((None, None, None, 1, 1024), lambda j, b, c: (j, b, cc(c), 0, 0))
    return slab, bb, cm, av, dv, st


def _seg_load(ref, seg):
    return jnp.concatenate([ref[pl.ds(k, 8, stride=seg), :] for k in range(seg)], axis=0)


def _seg_store(ref, val, seg):
    for k in range(seg):
        ref[pl.ds(k, 8, stride=seg), :] = val[8 * k:8 * k + 8, :]


def _seg_scan(s_ref, ar, ai, seg, reverse):
    def step(i, carry):
        k = seg - 1 - i if reverse else i
        sr, si = carry
        nr = ar * sr - ai * si + s_ref[k, :, :512]
        ni = ar * si + ai * sr + s_ref[k, :, 512:]
        s_ref[k, :, :512] = nr
        s_ref[k, :, 512:] = ni
        return nr, ni

    zero = jnp.zeros((8, 512), F32)
    return lax.fori_loop(0, seg, step, (zero, zero), unroll=4)


def _seg_entries(tr, ti, pr, pi, cin, reverse):
    row = lax.broadcasted_iota(jnp.int32, (8, 512), 0)
    edge = row == (7 if reverse else 0)
    cr, ci = cin[:, :512], cin[:, 512:]
    xr = tr + jnp.where(edge, pr * cr - pi * ci, 0.0)
    xi = ti + jnp.where(edge, pr * ci + pi * cr, 0.0)
    ir, ii = _cscan(xr, xi, pr, pi, reverse)
    shift = 7 if reverse else 1
    er = jnp.where(edge, cr, pltpu.roll(ir, shift, 0))
    ei = jnp.where(edge, ci, pltpu.roll(ii, shift, 0))
    far = row == (0 if reverse else 7)
    out = jnp.concatenate([jnp.sum(jnp.where(far, ir, 0.0), axis=0, keepdims=True),
                           jnp.sum(jnp.where(far, ii, 0.0), axis=0, keepdims=True)], axis=1)
    return er, ei, out


def _seg_power(ar, ai, seg):
    pr, pi = ar, ai
    for _ in range(seg.bit_length() - 1):
        pr, pi = pr * pr - pi * pi, 2.0 * pr * pi
    return pr, pi


def _f_s5(zs5, bbc, ccm, avec, dvec, nb, t, comm):
    n = zs5.shape[0]
    nc = n // nb // t
    seg = t // 8
    slab, bb, cm, av, dv, st = _s5_specs(nb, nc, t, False)

    def body(u_ref, bb_ref, cc_ref, a_ref, d_ref, y_ref, st_ref, carry, s_ref):
        @pl.when(pl.program_id(2) == 0)
        def _():
            carry[...] = jnp.zeros_like(carry)
        cin = carry[...]
        st_ref[...] = cin
        up = _seg_load(u_ref, seg)
        s_ref[...] = _dot(up.astype(BF16), bb_ref[...]).reshape(seg, 8, 1024)
        a1r, a1i = a_ref[:, :512], a_ref[:, 512:]
        ar, ai = jnp.broadcast_to(a1r, (8, 512)), jnp.broadcast_to(a1i, (8, 512))
        tr, ti = _seg_scan(s_ref, ar, ai, seg, False)
        er, ei, cout = _seg_entries(tr, ti, *_seg_power(a1r, a1i, seg), cin, False)
        carry[...] = cout

        def apply(k, pe):
            pr, pi = pe
            pr, pi = ar * pr - ai * pi, ar * pi + ai * pr
            s_ref[k, :, :512] = s_ref[k, :, :512] + pr
            s_ref[k, :, 512:] = s_ref[k, :, 512:] + pi
            return pr, pi

        lax.fori_loop(0, seg, apply, (er, ei), unroll=4)
        s = s_ref[...].reshape(t, 1024).astype(BF16)
        _seg_store(y_ref, _dot(s, cc_ref[...]) + d_ref[...] * up, seg)

    return _hosting_call(
        body, "f_s5", (S5_SUPER, nb, nc), [slab, bb, cm, av, dv], [slab, st],
        [jax.ShapeDtypeStruct((n, S5_W), F32), jax.ShapeDtypeStruct((S5_SUPER, nb, nc, 1, 1024), F32)],
        [pltpu.VMEM((1, 1024), F32), pltpu.VMEM((seg, 8, 1024), F32)], ("arbitrary", "arbitrary", "arbitrary"),
        (zs5, bbc, ccm, avec, dvec), comm)


def _f_mix(ygm, ypre, zg, x, w_glu, w_br_gm, w_br_s5, w_mix_out, g_post, tm):
    n = x.shape[0]

    def body(ygm_ref, ypre_ref, zg_ref, x_ref, wglu_ref, wgm_ref, ws5_ref, wout_ref, g_ref,
             yg_ref, ys5_ref, a_ref, b_ref, mg_ref, mo_ref, x1_ref):
        yg = _gelu(ypre_ref[...])
        ygb = yg.astype(BF16)
        yg_ref[...] = ygb
        ys5 = (yg * _sig(_dot(ygb, wglu_ref[...]))).astype(BF16)
        ys5_ref[...] = ys5
        a = _dot(ygm_ref[...], wgm_ref[...])
        b = _dot(ys5, ws5_ref[...])
        a_ref[...] = a.astype(BF16)
        b_ref[...] = b.astype(BF16)
        zg = zg_ref[...]
        merged = (_sig(zg[:, :D]) * a + _sig(zg[:, D:]) * b).astype(BF16)
        mg_ref[...] = merged
        mo = _dot(merged, wout_ref[...])
        mo_ref[...] = mo
        x1_ref[...] = x_ref[...] + _rms_fwd(mo, g_ref[...])

    return _rowcall("f_mix", body, n, tm,
                    [(ygm, 'row'), (ypre, 'row'), (zg, 'row'), (x, 'row'), (w_glu, 'full'), (w_br_gm, 'full'),
                     (w_br_s5, 'full'), (w_mix_out, 'full'), (g_post, 'full')],
                    [('row', S5_W, BF16), ('row', S5_W, BF16), ('row', D, BF16), ('row', D, BF16),
                     ('row', D, BF16), ('row', D, F32), ('row', D, F32)])


def _f_mem(mem, g_mem, w_kv, tm):
    n = mem.shape[0]

    def body(m_ref, g_ref, w_ref, mn_ref, k_ref, v_ref):
        mn = _rms_fwd(m_ref[...], g_ref[...]).astype(BF16)
        mn_ref[...] = mn
        k_ref[...] = _dot(mn, w_ref[:, :D]).astype(BF16)
        v_ref[...] = _dot(mn, w_ref[:, D:]).astype(BF16)

    return _rowcall("f_mem", body, n, tm, [(mem, 'row'), (g_mem, 'full'), (w_kv, 'full')],
                    [('row', D, BF16), ('row', D, BF16), ('row', D, BF16)])


def _softmax(s):
    m = jnp.max(s, axis=-1, keepdims=True)
    e = jnp.exp(s - m)
    return e / jnp.sum(e, axis=-1, keepdims=True)


def _f_attn(x1, g_pre, w_q, k, v, w_o, g_post, tm, tpb, mlen):
    n = x1.shape[0]
    kv_spec = pl.BlockSpec((mlen, D), lambda i: (i // tpb, 0))
    scale = HEAD_DIM ** -0.5

    def body(x_ref, gp_ref, wq_ref, k_ref, v_ref, wo_ref, go_ref, hc_ref, q_ref, o_ref, ao_ref, x2_ref):
        x1v = x_ref[...]
        hc = _rms_fwd(x1v, gp_ref[...]).astype(BF16)
        hc_ref[...] = hc
        q_ref[...] = _dot(hc, wq_ref[...]).astype(BF16)
        for h in range(HEADS):
            hs = slice(h * HEAD_DIM, (h + 1) * HEAD_DIM)
            p = _softmax(_dot_nt(q_ref[:, hs], k_ref[:, hs]) * scale)
            o_ref[:, hs] = _dot(p.astype(BF16), v_ref[:, hs]).astype(BF16)
        ao = _dot(o_ref[...], wo_ref[...])
        ao_ref[...] = ao
        x2_ref[...] = x1v + _rms_fwd(ao, go_ref[...])

    return _rowcall("f_attn", body, n, tm,
                    [(x1, 'row'), (g_pre, 'full'), (w_q, 'full'), (k, kv_spec), (v, kv_spec), (w_o, 'full'),
                     (g_post, 'full')],
                    [('row', D, BF16), ('row', D, BF16), ('row', D, BF16), ('row', D, F32), ('row', D, F32)])


def _f_ffn_up(x2, g_pre, w_gu, tm):
    n = x2.shape[0]

    def body(x_ref, g_ref, w_ref, hf_ref, gu_ref, act_ref):
        hf = _rms_fwd(x_ref[...], g_ref[...]).astype(BF16)
        hf_ref[...] = hf
        gg = _dot(hf, w_ref[:, :FFN_H])
        uu = _dot(hf, w_ref[:, FFN_H:])
        gu_ref[:, :FFN_H] = gg.astype(BF16)
        gu_ref[:, FFN_H:] = uu.astype(BF16)
        act_ref[...] = (gg * _sig(gg) * uu).astype(BF16)

    return _rowcall("f_ffn_up", body, n, tm, [(x2, 'row'), (g_pre, 'full'), (w_gu, 'full')],
                    [('row', D, BF16), ('row', 2 * FFN_H, BF16), ('row', FFN_H, BF16)])


def _f_ffn_down(act, x2, tgt, w_down, g_post, tm):
    n = x2.shape[0]

    def body(a_ref, x_ref, t_ref, w_ref, g_ref, dn_ref, d3_ref, loss_ref):
        @pl.when(pl.program_id(0) == 0)
        def _():
            loss_ref[...] = jnp.zeros_like(loss_ref)
        dn = _dot(a_ref[...], w_ref[...])
        dn_ref[...] = dn
        err = x_ref[...] + _rms_fwd(dn, g_ref[...]) - t_ref[...]
        d3_ref[...] = err * (1.0 / D)
        loss_ref[...] += jnp.sum(err * err, axis=0, keepdims=True)

    return _rowcall("f_ffn_down", body, n, tm,
                    [(act, 'row'), (x2, 'row'), (tgt, 'row'), (w_down, 'full'), (g_post, 'full')],
                    [('row', D, F32), ('row', D, F32), ('acc', (1, D), F32)])


def _b_ffn_down(d3, dn, gu, g_post, w_down, tm):
    n = d3.shape[0]

    def body(d3_ref, dn_ref, gu_ref, g_ref, w_ref, ddn_ref, dgu_ref, dg_ref):
        @pl.when(pl.program_id(0) == 0)
        def _():
            dg_ref[...] = jnp.zeros_like(dg_ref)
        ddn, dg = _rms_bwd(dn_ref[...], g_ref[...], d3_ref[...])
        dg_ref[...] += dg
        ddn = ddn.astype(BF16)
        ddn_ref[...] = ddn
        dact = _dot_nt(ddn, w_ref[...]).astype(BF16)
        gg = gu_ref[:, :FFN_H]
        uu = gu_ref[:, FFN_H:]
        sg = _sig(gg)
        silu = gg * sg
        dgu_ref[:, :FFN_H] = dact * uu * (sg + silu * (1.0 - sg))
        dgu_ref[:, FFN_H:] = dact * silu

    return _rowcall("b_ffn_down", body, n, tm,
                    [(d3, 'row'), (dn, 'row'), (gu, 'row'), (g_post, 'full'), (w_down, 'full')],
                    [('row', D, BF16), ('row', 2 * FFN_H, BF16), ('acc', (1, D), F32)])


def _b_ffn_up(dgu, d3, x2, w_gu, g_pre, tm):
    n = d3.shape[0]

    def body(dgu_ref, d3_ref, x_ref, w_ref, g_ref, dx_ref, dg_ref):
        @pl.when(pl.program_id(0) == 0)
        def _():
            dg_ref[...] = jnp.zeros_like(dg_ref)
        dhf = _dot_nt(dgu_ref[...], w_ref[...])
        dx, dg = _rms_bwd(x_ref[...], g_ref[...], dhf)
        dg_ref[...] += dg
        dx_ref[...] = d3_ref[...] + dx

    return _rowcall("b_ffn_up", body, n, tm,
                    [(dgu, 'row'), (d3, 'row'), (x2, 'row'), (w_gu, 'full'), (g_pre, 'full')],
                    [('row', D, F32), ('acc', (1, D), F32)])


def _b_attn(dx2, ao, x1, q, g_post, w_o, w_q, g_pre, k, v, tm, tpb, mlen):
    n = dx2.shape[0]
    nb = n // (tm * tpb)
    kv_spec = pl.BlockSpec((mlen, D), lambda i: (i // tpb, 0))
    scale = HEAD_DIM ** -0.5

    def body(dx2_ref, ao_ref, x1_ref, q_ref, go_ref, wo_ref, wq_ref, gp_ref, k_ref, v_ref,
             dao_ref, dq_ref, dx1_ref, dk_ref, dv_ref, dgo_ref, dgp_ref):
        i = pl.program_id(0)

        @pl.when(i == 0)
        def _():
            dgo_ref[...] = jnp.zeros_like(dgo_ref)
            dgp_ref[...] = jnp.zeros_like(dgp_ref)

        @pl.when(i % tpb == 0)
        def _():
            dk_ref[...] = jnp.zeros_like(dk_ref)
            dv_ref[...] = jnp.zeros_like(dv_ref)

        dx2v = dx2_ref[...]
        dao, dgo = _rms_bwd(ao_ref[...], go_ref[...], dx2v)
        dgo_ref[...] += dgo
        dao = dao.astype(BF16)
        dao_ref[...] = dao
        do = _dot_nt(dao, wo_ref[...])
        for h in range(HEADS):
            hs = slice(h * HEAD_DIM, (h + 1) * HEAD_DIM)
            qh = q_ref[:, hs]
            kh = k_ref[:, hs]
            p = _softmax(_dot_nt(qh, kh) * scale)
            doh = do[:, hs].astype(BF16)
            dp = _dot_nt(doh, v_ref[:, hs])
            ds = (p * (dp - jnp.sum(dp * p, axis=-1, keepdims=True)) * scale).astype(BF16)
            dq_ref[:, hs] = _dot(ds, kh).astype(BF16)
            dk_ref[:, hs] += _dot_tn(ds, qh)
            dv_ref[:, hs] += _dot_tn(p.astype(BF16), doh)
        dhc = _dot_nt(dq_ref[...], wq_ref[...])
        dx, dgp = _rms_bwd(x1_ref[...], gp_ref[...], dhc)
        dgp_ref[...] += dgp
        dx1_ref[...] = dx2v + dx

    return _rowcall("b_attn", body, n, tm,
                    [(dx2, 'row'), (ao, 'row'), (x1, 'row'), (q, 'row'), (g_post, 'full'), (w_o, 'full'),
                     (w_q, 'full'), (g_pre, 'full'), (k, kv_spec), (v, kv_spec)],
                    [('row', D, BF16), ('row', D, BF16), ('row', D, F32),
                     (kv_spec, (nb * mlen, D), F32), (kv_spec, (nb * mlen, D), F32),
                     ('acc', (1, D), F32), ('acc', (1, D), F32)])


def _b_mem(dk, dv, mem, w_kv, g_mem, tm):
    n = mem.shape[0]

    def body(dk_ref, dv_ref, m_ref, w_ref, g_ref, dkv_ref, dg_ref):
        @pl.when(pl.program_id(0) == 0)
        def _():
            dg_ref[...] = jnp.zeros_like(dg_ref)
        dkb = dk_ref[...].astype(BF16)
        dvb = dv_ref[...].astype(BF16)
        dkv_ref[:, :D] = dkb
        dkv_ref[:, D:] = dvb
        dmn = _dot_nt(dkb, w_ref[:, :D]) + _dot_nt(dvb, w_ref[:, D:])
        _, dg = _rms_bwd(m_ref[...], g_ref[...], dmn)
        dg_ref[...] += dg

    return _rowcall("b_mem", body, n, tm, [(dk, 'row'), (dv, 'row'), (mem, 'row'), (w_kv, 'full'), (g_mem, 'full')],
                    [('row', 2 * D, BF16), ('acc', (1, D), F32)])


def _b_mix(dx1, mo, a, b, zg, ypre, g_post, w_mix_out, w_br_gm, w_br_s5, w_glu, tm):
    n = dx1.shape[0]

    def body(dx1_ref, mo_ref, a_ref, b_ref, zg_ref, ypre_ref, g_ref, wout_ref, wgm_ref, ws5_ref, wglu_ref,
             dmo_ref, da_ref, db_ref, dzg_ref, dygm_ref, dgate_ref, dypre_ref, dg_ref):
        @pl.when(pl.program_id(0) == 0)
        def _():
            dg_ref[...] = jnp.zeros_like(dg_ref)
        dmo, dg = _rms_bwd(mo_ref[...], g_ref[...], dx1_ref[...])
        dg_ref[...] += dg
        dmo = dmo.astype(BF16)
        dmo_ref[...] = dmo
        dmg = _dot_nt(dmo, wout_ref[...]).astype(BF16)
        zg = zg_ref[...].astype(BF16)
        sa, sb = _sig(zg[:, :D]), _sig(zg[:, D:])
        da = dmg * sa
        db = dmg * sb
        da_ref[...] = da
        db_ref[...] = db
        dzg_ref[:, :D] = dmg * a_ref[...] * (sa * (1.0 - sa))
        dzg_ref[:, D:] = dmg * b_ref[...] * (sb * (1.0 - sb))
        dygm_ref[...] = _dot_nt(da, wgm_ref[...])
        dys5 = _dot_nt(db, ws5_ref[...])
        ypre = ypre_ref[...]
        yg = _gelu(ypre)
        sgt = _sig(_dot(yg.astype(BF16), wglu_ref[...]))
        dgate = (dys5 * yg * sgt * (1.0 - sgt)).astype(BF16)
        dgate_ref[...] = dgate
        dyg = dys5 * sgt + _dot_nt(dgate, wglu_ref[...])
        dypre_ref[...] = dyg * _gelu_grad(ypre)

    return _rowcall("b_mix", body, n, tm,
                    [(dx1, 'row'), (mo, 'row'), (a, 'row'), (b, 'row'), (zg, 'row'), (ypre, 'row'), (g_post, 'full'),
                     (w_mix_out, 'full'), (w_br_gm, 'full'), (w_br_s5, 'full'), (w_glu, 'full')],
                    [('row', D, BF16), ('row', D, BF16), ('row', D, BF16), ('row', 2 * D, BF16), ('row', D, F32),
                     ('row', S5_W, BF16), ('row', S5_W, F32), ('acc', (1, D), F32)])


def _b_s5(zs5, dypre, states, bbc, ccm, avec, dvec, nb, t, comm):
    n = zs5.shape[0]
    nc = n // nb // t
    seg = t // 8
    slab, bb, cm, av, dv, st = _s5_specs(nb, nc, t, True)

    def body(u_ref, dy_ref, st_ref, bb_ref, cc_ref, a_ref, d_ref,
             du_ref, dbb_ref, dcc_ref, da_ref, dd_ref, lcarry, s_ref, l_ref):
        first = jnp.logical_and(pl.program_id(1) == 0, pl.program_id(2) == 0)

        @pl.when(first)
        def _():
            dbb_ref[...] = jnp.zeros_like(dbb_ref)
            dcc_ref[...] = jnp.zeros_like(dcc_ref)
            da_ref[...] = jnp.zeros_like(da_ref)
            dd_ref[...] = jnp.zeros_like(dd_ref)

        @pl.when(pl.program_id(2) == 0)
        def _():
            lcarry[...] = jnp.zeros_like(lcarry)

        up, dyp = _seg_load(u_ref, seg), _seg_load(dy_ref, seg)
        ub, dyb = up.astype(BF16), dyp.astype(BF16)
        a1r, a1i = a_ref[:, :512], a_ref[:, 512:]
        ar, ai = jnp.broadcast_to(a1r, (8, 512)), jnp.broadcast_to(a1i, (8, 512))
        pr, pi = _seg_power(a1r, a1i, seg)

        s_ref[...] = _dot(ub, bb_ref[...]).reshape(seg, 8, 1024)
        tr, ti = _seg_scan(s_ref, ar, ai, seg, False)
        er, ei, _ = _seg_entries(tr, ti, pr, pi, st_ref[...], False)

        def apply(k, pe):
            qr, qi = pe
            qr, qi = ar * qr - ai * qi, ar * qi + ai * qr
            s_ref[k, :, :512] = s_ref[k, :, :512] + qr
            s_ref[k, :, 512:] = s_ref[k, :, 512:] + qi
            return qr, qi

        lax.fori_loop(0, seg, apply, (er, ei), unroll=4)

        l_ref[...] = _dot_nt(dyb, cc_ref[...]).reshape(seg, 8, 1024)
        tr, ti = _seg_scan(l_ref, ar, -ai, seg, True)
        fr, fi, lout = _seg_entries(tr, ti, pr, -pi, lcarry[...], True)
        lcarry[...] = lout

        def apply_back(i, carry):
            qr, qi, accr, acci = carry
            k = seg - 1 - i
            qr, qi = ar * qr + ai * qi, ar * qi - ai * qr
            lr = l_ref[k, :, :512] + qr
            li = l_ref[k, :, 512:] + qi
            l_ref[k, :, :512] = lr
            l_ref[k, :, 512:] = li
            kp = jnp.maximum(k - 1, 0)
            sr = jnp.where(k == 0, er, s_ref[kp, :, :512])
            si = jnp.where(k == 0, ei, s_ref[kp, :, 512:])
            return qr, qi, accr + lr * sr + li * si, acci + li * sr - lr * si

        zero = jnp.zeros((8, 512), F32)
        _, _, accr, acci = lax.fori_loop(0, seg, apply_back, (fr, fi, zero, zero), unroll=4)
        da_ref[:, :512] += jnp.sum(accr, axis=0, keepdims=True)
        da_ref[:, 512:] += jnp.sum(acci, axis=0, keepdims=True)

        s = s_ref[...].reshape(t, 1024).astype(BF16)
        lam = l_ref[...].reshape(t, 1024).astype(BF16)
        dcc_ref[...] += _dot_tn(dyb, s)
        dbb_ref[...] += _dot_tn(ub, lam)
        _seg_store(du_ref, _dot_nt(lam, bb_ref[...]) + d_ref[...] * dyp, seg)
        dd_ref[...] += jnp.sum(dyp * up, axis=0, keepdims=True)

    return _hosting_call(
        body, "b_s5", (S5_SUPER, nb, nc), [slab, slab, st, bb, cm, av, dv], [slab, bb, bb, av, dv],
        [jax.ShapeDtypeStruct((n, S5_W), F32), jax.ShapeDtypeStruct(bbc.shape, F32),
         jax.ShapeDtypeStruct(bbc.shape, F32), jax.ShapeDtypeStruct(avec.shape, F32),
         jax.ShapeDtypeStruct(dvec.shape, F32)],
        [pltpu.VMEM((1, 1024), F32), pltpu.VMEM((seg, 8, 1024), F32), pltpu.VMEM((seg, 8, 1024), F32)],
        ("arbitrary", "arbitrary", "arbitrary"), (zs5, dypre, states, bbc, ccm, avec, dvec), comm)


def _b_gmlp(zgm, dygm, ln_g, ln_b, w_s, b_s, tm, comm):
    n = zgm.shape[0]

    def body(z_ref, dy_ref, lg_ref, lb_ref, ws_ref, bs_ref, dz_ref, dws_ref, dbs_ref, dlg_ref, dlb_ref,
             du_s, dvn_s):
        @pl.when(pl.program_id(0) == 0)
        def _():
            dws_ref[...] = jnp.zeros_like(dws_ref)
            dbs_ref[...] = jnp.zeros_like(dbs_ref)
            dlg_ref[...] = jnp.zeros_like(dlg_ref)
            dlb_ref[...] = jnp.zeros_like(dlb_ref)
        z = z_ref[...].astype(BF16)
        zg = _gelu(z)
        u = zg[:, :D]
        vh, r = _ln_stats(zg[:, D:].astype(F32))
        vn = (vh * lg_ref[...] + lb_ref[...]).astype(BF16)
        dy = dy_ref[...]
        keep = _tril()
        for g in range(GM_GROUPS):
            w = jnp.where(keep, ws_ref[g], 0.0).astype(BF16)
            cs = slice(g * LANE, (g + 1) * LANE)
            for c in range(tm // GM_CHUNK):
                rs = slice(c * GM_CHUNK, (c + 1) * GM_CHUNK)
                vb = vn[rs, cs]
                sv = _dot(w, vb) + bs_ref[g]
                dyb = dy[rs, cs]
                du_s[rs, cs] = dyb * sv
                dsv = dyb * u[rs, cs]
                dsvb = dsv.astype(BF16)
                dvn_s[rs, cs] = _dot_tn(w, dsvb)
                dws_ref[g] += jnp.where(keep, _dot_nt(dsvb, vb), 0.0)
                dbs_ref[g] += jnp.sum(dsv, axis=1, keepdims=True)
        dvn = dvn_s[...]
        dlg_ref[...] += jnp.sum(dvn * vh, axis=0, keepdims=True)
        dlb_ref[...] += jnp.sum(dvn, axis=0, keepdims=True)
        dvh = dvn * lg_ref[...]
        dv = r * (dvh - jnp.mean(dvh, axis=-1, keepdims=True) - vh * jnp.mean(dvh * vh, axis=-1, keepdims=True))
        dz_ref[:, :D] = du_s[...].astype(BF16) * _gelu_grad(z[:, :D])
        dz_ref[:, D:] = dv.astype(BF16) * _gelu_grad(z[:, D:])

    return _rowcall("b_gmlp", body, n, tm,
                    [(zgm, 'row'), (dygm, 'row'), (ln_g, 'full'), (ln_b, 'full'), (w_s, 'full'), (b_s, 'full')],
                    [('row', 2 * D, BF16), ('acc', w_s.shape, F32), ('acc', b_s.shape, F32), ('acc', (1, D), F32),
                     ('acc', (1, D), F32)],
                    scratch=[pltpu.VMEM((tm, D), F32), pltpu.VMEM((tm, D), F32)], comm=comm)


def _b_in(dzgm, dzs5, dzg, dx1, x, w_in, g_pre, tm, comm):
    n = x.shape[0]

    def body(d1_ref, d2_ref, d3_ref, dx1_ref, x_ref, w_ref, g_ref, gx_ref, dg_ref):
        @pl.when(pl.program_id(0) == 0)
        def _():
            dg_ref[...] = jnp.zeros_like(dg_ref)
        dh = (_dot_nt(d1_ref[...], w_ref[:, 0:2 * D]) + _dot_nt(d2_ref[...].astype(BF16), w_ref[:, 2 * D:2 * D + S5_W])
              + _dot_nt(d3_ref[...], w_ref[:, 2 * D + S5_W:]))
        dx, dg = _rms_bwd(x_ref[...], g_ref[...], dh)
        dg_ref[...] += dg
        gx_ref[...] = dx1_ref[...] + dx

    return _rowcall("b_in", body, n, tm,
                    [(dzgm, 'row'), (dzs5, 'row'), (dzg, 'row'), (dx1, 'row'), (x, 'row'), (w_in, 'full'),
                     (g_pre, 'full')],
                    [('row', D, F32), ('acc', (1, D), F32)], comm=comm)


def _whole(name, body, ins, outs):
    return pl.pallas_call(body, name=name, out_shape=[jax.ShapeDtypeStruct(s, dt) for s, dt in outs],
                          compiler_params=_params())(*ins)


def _s5_disc_fwd(lr, li, ls, br, bi):
    def body(lr_ref, li_ref, ls_ref, br_ref, bi_ref, o1, o2, o3, o4):
        outs = _s5_disc(lr_ref[...], li_ref[...], ls_ref[...], br_ref[...], bi_ref[...])
        for o, val in zip((o1, o2, o3, o4), outs):
            o[...] = val

    return _whole("s5_disc_fwd", body, [lr, li, ls, br, bi],
                  [(lr.shape, F32), (lr.shape, F32), (br.shape, F32), (br.shape, F32)])


def _s5_disc_bwd(lr, li, ls, br, bi, cts):
    def body(lr_ref, li_ref, ls_ref, br_ref, bi_ref, c1, c2, c3, c4, o1, o2, o3, o4, o5):
        _, vjp = jax.vjp(_s5_disc, lr_ref[...], li_ref[...], ls_ref[...], br_ref[...], bi_ref[...])
        grads = vjp((c1[...], c2[...], c3[...], c4[...]))
        for o, val in zip((o1, o2, o3, o4, o5), grads):
            o[...] = val

    return _whole("s5_disc_bwd", body, [lr, li, ls, br, bi, *cts],
                  [(lr.shape, F32), (lr.shape, F32), (lr.shape, F32), (br.shape, F32), (br.shape, F32)])


def _sum_slots(name, slots):
    def body(s_ref, o_ref):
        acc = s_ref[0]
        for k in range(1, slots.shape[0]):
            acc = acc + s_ref[k]
        o_ref[...] = acc

    return _whole(name, body, [slots], [(slots.shape[1:], F32)])[0]


def _adamw_math(w, g, m, v):
    c1 = 1.0 - ADAM_B1 ** ADAM_STEP
    c2 = 1.0 - ADAM_B2 ** ADAM_STEP
    mn = ADAM_B1 * m + (1.0 - ADAM_B1) * g
    vn = ADAM_B2 * v + (1.0 - ADAM_B2) * (g * g)
    return -ADAM_LR * ((mn / c1) / (jnp.sqrt(vn / c2) + ADAM_EPS) + ADAM_WD * w), mn, vn


def _adamw_many(name, ws, gs, ms, vs):
    k = len(ws)

    def body(*refs):
        for i in range(k):
            dl, mn, vn = _adamw_math(refs[i][...], refs[k + i][...], refs[2 * k + i][...], refs[3 * k + i][...])
            refs[4 * k + i][...] = dl
            refs[5 * k + i][...] = mn
            refs[6 * k + i][...] = vn

    res = _whole(name, body, [*ws, *gs, *ms, *vs], [(t.shape, F32) for t in ws] * 3)
    return res[:k], res[k:2 * k], res[2 * k:]


def _adamw(name, w, g, m, v, tm):
    def body(w_ref, g_ref, m_ref, v_ref, d_ref, mo_ref, vo_ref):
        d_ref[...], mo_ref[...], vo_ref[...] = _adamw_math(w_ref[...], g_ref[...], m_ref[...], v_ref[...])

    cols = w.shape[1]
    return _rowcall(name, body, w.shape[0], tm, [(w, 'row'), (g, 'row'), (m, 'row'), (v, 'row')],
                    [('row', cols, F32), ('row', cols, F32), ('row', cols, F32)])


def _picked_rowcall(name, body, grid, in_specs, out_cols, tm):
    return pl.pallas_call(
        lambda p_ref, *refs: body(*refs), name=name,
        grid_spec=pltpu.PrefetchScalarGridSpec(
            num_scalar_prefetch=1, grid=(grid,), in_specs=in_specs,
            out_specs=[pl.BlockSpec((tm, D), lambda i, p: (i, 0)) for _ in out_cols]),
        out_shape=[jax.ShapeDtypeStruct((grid * tm, D), dt) for dt in out_cols],
        compiler_params=_params(("arbitrary",)),
    )


def _exchange(copies):
    def build(src_refs, out_refs, ss, rs):
        def rdma(k, src, dst, peer):
            return _Lazy(lambda: pltpu.make_async_remote_copy(
                src_ref=src, dst_ref=dst, send_sem=ss.at[k], recv_sem=rs.at[k], device_id=peer, device_id_type=MESH))
        return copies(src_refs, out_refs, rdma)

    def start(*refs):
        for cp in build(*refs):
            cp.start()

    def finish(*refs):
        for cp in build(*refs):
            cp.wait()

    return start, finish


def _reduce_big(tag, slabs):
    rows = [s.shape[1] for s in slabs]
    blk = sum(rows) // 4
    tm = _tile_rows(blk, 512)
    per = blk // tm
    gpack = jnp.concatenate([s.reshape(4, 2, 2, r // 4, D) for s, r in zip(slabs, rows)], axis=3)

    mx, my, mc = lax.axis_index("x"), lax.axis_index("y"), lax.axis_index("c")
    chip_me, chip_xn, chip_yn = 2 * mx + my, 2 * (1 - mx) + my, 2 * mx + (1 - my)

    def pair_copies(srcs, dsts, rdma):
        c, _, (_, _, psib) = _place()
        return [rdma(0, srcs[0].at[:, 1 - c], dsts[0], psib)]

    from_sib = (yield _Comm([gpack], [((4, 2, blk, D), F32)], 1, _exchange(pair_copies), {}, (0.0, 1.0)))[0]

    def sum_both(a_ref, b_ref, o32_ref, o16_ref):
        s = a_ref[...].astype(F32) + b_ref[...].astype(F32)
        o32_ref[...] = s
        o16_ref[...] = s.astype(BF16)

    p32, p16 = _picked_rowcall(
        "sum_pair_" + tag, sum_both, 8 * per,
        [pl.BlockSpec((None, None, None, tm, D), lambda i, p: (i // (2 * per), p[0], (i // per) % 2, i % per, 0)),
         pl.BlockSpec((tm, D), lambda i, p: (i, 0))],
        [F32, BF16], tm)(jnp.stack([mc]).astype(jnp.int32), gpack, from_sib.reshape(8 * blk, D))
    p32, p16 = p32.reshape(4, 2, blk, D), p16.reshape(4, 2, blk, D)

    def step1_copies(srcs, dsts, rdma):
        _, (me, xn, yn, dg), (px, py, _) = _place()
        s16 = srcs[0]
        return [rdma(0, s16.at[yn, 0], dsts[0].at[0], py), rdma(1, s16.at[dg, 0], dsts[0].at[1], py),
                rdma(2, s16.at[xn, 1], dsts[0].at[2], px), rdma(3, s16.at[dg, 1], dsts[0].at[3], px)]

    recv1 = (yield _Comm([p16], [((4, blk, D), BF16)], 4, _exchange(step1_copies), {}, (0.0, 1.0)))[0]

    s32, s16 = _picked_rowcall(
        "sum_step1_" + tag, sum_both, 4 * per,
        [pl.BlockSpec((None, None, tm, D), lambda i, p: (p[i // per], i // (2 * per), i % per, 0)),
         pl.BlockSpec((tm, D), lambda i, p: (i, 0))],
        [F32, BF16], tm)(jnp.stack([chip_me, chip_xn, chip_me, chip_yn]).astype(jnp.int32), p32,
                         recv1.reshape(4 * blk, D))

    def step2_copies(srcs, dsts, rdma):
        _, _, (px, py, _) = _place()
        return [rdma(0, srcs[0].at[1], dsts[0].at[0], px), rdma(1, srcs[0].at[3], dsts[0].at[1], py)]

    recv2 = (yield _Comm([s16.reshape(4, blk, D)], [((2, blk, D), BF16)], 2, _exchange(step2_copies), {},
                         (0.0, 1.0)))[0]

    def sum_step2(a_ref, b_ref, o_ref):
        o_ref[...] = a_ref[...] + b_ref[...].astype(F32)

    red = _rowcall("sum_step2_" + tag, sum_step2, 2 * blk, tm,
                   [(s32.reshape(4, blk, D), pl.BlockSpec((None, tm, D), lambda i: (2 * (i // per), i % per, 0))),
                    (recv2.reshape(2 * blk, D), 'row')],
                   [('row', D, F32)])[0].reshape(2, blk, D)

    def share_copies(srcs, dsts, rdma):
        _, _, (_, _, psib) = _place()
        return [rdma(0, srcs[0], dsts[0], psib)]

    other = (yield _Comm([red], [((2, blk, D), F32)], 1, _exchange(share_copies), {}, (0.0, 1.0)))[0]
    lo = jnp.where(mc == 0, red, other)
    hi = jnp.where(mc == 0, other, red)
    out, off = [], 0
    for r in rows:
        q = r // 4
        out.append(jnp.concatenate([lo[0, off:off + q], lo[1, off:off + q], hi[0, off:off + q], hi[1, off:off + q]]))
        off += q
    return out


def _rows1024(a):
    flat = a.reshape(-1)
    pad = (-flat.shape[0]) % D
    if pad:
        flat = jnp.concatenate([flat, jnp.zeros((pad,), flat.dtype)])
    return flat.reshape(-1, D)


def _pack(arrs, pad_rows_to=8):
    parts = [_rows1024(a) for a in arrs]
    rows = sum(p.shape[0] for p in parts)
    pad = (-rows) % pad_rows_to
    if pad:
        parts.append(jnp.zeros((pad, D), parts[0].dtype))
    return jnp.concatenate(parts, axis=0)


def _unpack(packed, shapes):
    out, r = [], 0
    for s in shapes:
        size = math.prod(s)
        nr = -(-size // D)
        out.append(packed[r:r + nr].reshape(-1)[:size].reshape(s))
        r += nr
    return out


def _block_diag(t):
    eye = jnp.eye(8, dtype=t.dtype)
    j, g, a, b = t.shape
    return (t[:, :, :, None, :] * eye[None, :, None, :, None]).reshape(j, g * a, g * b)


def _block_diag_take(m, a, b):
    eye = jnp.eye(8, dtype=m.dtype)
    return (m.reshape(4, 8, a, 8, b) * eye[None, :, None, :, None]).sum(axis=3)


def kernel(x, mem, g_mix_pre, w_in, gm_ln_g, gm_ln_b, gm_w_s, gm_b_s, s5_lam_re, s5_lam_im, s5_log_step, s5_b_re, s5_b_im, s5_c_re, s5_c_im, s5_d, s5_w_glu, w_br_gm, w_br_s5, w_mix_out, g_mix_post, g_ca_pre, g_mem, ca_w_q, ca_w_kv, ca_w_o, g_ca_post, g_ffn_pre, ffn_w_gu, ffn_w_down, g_ffn_post, loss_target, m_g_mix_pre, m_w_in, m_gm_ln_g, m_gm_ln_b, m_gm_w_s, m_gm_b_s, m_s5_lam_re, m_s5_lam_im, m_s5_log_step, m_s5_b_re, m_s5_b_im, m_s5_c_re, m_s5_c_im, m_s5_d, m_s5_w_glu, m_w_br_gm, m_w_br_s5, m_w_mix_out, m_g_mix_post, m_g_ca_pre, m_g_mem, m_ca_w_q, m_ca_w_kv, m_ca_w_o, m_g_ca_post, m_g_ffn_pre, m_ffn_w_gu, m_ffn_w_down, m_g_ffn_post, v_g_mix_pre, v_w_in, v_gm_ln_g, v_gm_ln_b, v_gm_w_s, v_gm_b_s, v_s5_lam_re, v_s5_lam_im, v_s5_log_step, v_s5_b_re, v_s5_b_im, v_s5_c_re, v_s5_c_im, v_s5_d, v_s5_w_glu, v_w_br_gm, v_w_br_s5, v_w_mix_out, v_g_mix_post, v_g_ca_pre, v_g_mem, v_ca_w_q, v_ca_w_kv, v_ca_w_o, v_g_ca_post, v_g_ffn_pre, v_ffn_w_gu, v_ffn_w_down, v_g_ffn_post):
    a = dict(locals())
    w = {n: a[n][0] for n in WNAMES}
    nb, seq, _ = x.shape
    n = nb * seq
    mlen = mem.shape[1]
    tm = min(256, seq)
    tmb = min(512, seq)
    tpb = seq // tmb
    xf = x.reshape(n, D)
    tgt = loss_target.reshape(n, D)
    memf = mem.reshape(nb * mlen, D)
    tmm = min(256, mlen)

    big_names = list(BIG)
    local_shapes = {k: w[k].shape for k in big_names}
    w_in_full = _run_comm("gather_w_in", _gather_weights("cast_w_in", [w['w_in']], [BIG['w_in']]))[0]
    rest = [k for k in big_names if k != 'w_in']
    gather_rest = _gather_weights("cast_w_rest", [w[k] for k in rest], [BIG[k] for k in rest])

    vec = lambda name: w[name].reshape(1, D)

    to_lane = lambda p: p.reshape(1, -1)
    b_t = lambda p: p.transpose(2, 0, 1).reshape(16, -1)
    lr_l, li_l = to_lane(w['s5_lam_re']), to_lane(w['s5_lam_im'])
    ls_l = jnp.repeat(w['s5_log_step'], 64).reshape(1, -1)
    br_t, bi_t = b_t(w['s5_b_re']), b_t(w['s5_b_im'])
    ab_re, ab_im, bb_re, bb_im = _s5_disc_fwd(lr_l, li_l, ls_l, br_t, bi_t)
    blk = lambda t: _block_diag(t.reshape(16, 4, 8, 64).transpose(1, 2, 0, 3))
    bbc = jnp.concatenate([blk(bb_re), blk(bb_im)], axis=2).astype(BF16)
    cblk = lambda c: _block_diag(c.reshape(4, 8, 16, 64).transpose(0, 1, 3, 2))
    ccm = jnp.concatenate([cblk(w['s5_c_re']), -cblk(w['s5_c_im'])], axis=1).astype(BF16)
    avec = jnp.concatenate([ab_re.reshape(4, 1, 512), ab_im.reshape(4, 1, 512)], axis=2)
    dvec = w['s5_d'].reshape(4, 1, LANE)

    bs3 = w['gm_b_s'].reshape(GM_GROUPS, GM_CHUNK, 1)
    wf = {'w_in': w_in_full}
    h, zgm, zs5, zg = _f_in(xf, vec('g_mix_pre'), wf['w_in'], tmb)
    ygm = _f_gmlp(zgm, vec('gm_ln_g'), vec('gm_ln_b'), w['gm_w_s'], bs3, tmb)
    s5_t = min(S5_T, seq)
    (ypre, states), gathered = _f_s5(zs5, bbc, ccm, avec, dvec, nb, s5_t, gather_rest)
    for k, g in zip(rest, gathered):
        wf[k] = g.reshape(-1, g.shape[-1]) if BIG[k] == 0 else g
    yg, ys5, a_br, b_br, merged, mo, x1 = _f_mix(ygm, ypre, zg, xf, wf['s5_w_glu'], wf['w_br_gm'], wf['w_br_s5'],
                                                 wf['w_mix_out'], vec('g_mix_post'), tmb)
    mem_n, kk, vv = _f_mem(memf, vec('g_mem'), wf['ca_w_kv'], tmm)
    hc, q_att, o_att, ao, x2 = _f_attn(x1, vec('g_ca_pre'), wf['ca_w_q'], kk, vv, wf['ca_w_o'], vec('g_ca_post'),
                                tmb, tpb, mlen)
    hf, gu, act = _f_ffn_up(x2, vec('g_ffn_pre'), wf['ffn_w_gu'], tmb)
    dn, d3, loss_cols = _f_ffn_down(act, x2, tgt, wf['ffn_w_down'], vec('g_ffn_post'), tmb)

    gsm = {}
    gbig = {}
    ddn, dgu, gsm['g_ffn_post'] = _b_ffn_down(d3, dn, gu, vec('g_ffn_post'), wf['ffn_w_down'], tmb)
    dx2, gsm['g_ffn_pre'] = _b_ffn_up(dgu, d3, x2, wf['ffn_w_gu'], vec('g_ffn_pre'), tmb)
    gbig['ffn_w_down'] = _mm_tn("dw_ffn_down", act, ddn)
    gbig['ffn_w_gu'] = _mm_tn("dw_ffn_gu", dgu, hf)
    dao, dq, dx1, dk, dv, gsm['g_ca_post'], gsm['g_ca_pre'] = _b_attn(
        dx2, ao, x1, q_att, vec('g_ca_post'), wf['ca_w_o'], wf['ca_w_q'], vec('g_ca_pre'), kk, vv, tmb, tpb, mlen)
    gbig['ca_w_o'] = _mm_tn("dw_ca_o", o_att, dao)
    gbig['ca_w_q'] = _mm_tn("dw_ca_q", hc, dq)
    dkv, gsm['g_mem'] = _b_mem(dk, dv, memf, wf['ca_w_kv'], vec('g_mem'), tmm)
    gbig['ca_w_kv'] = _mm_tn("dw_ca_kv", dkv, mem_n)
    dmo, da_br, db_br, dzg, dygm, dgate, dypre, gsm['g_mix_post'] = _b_mix(
        dx1, mo, a_br, b_br, zg, ypre, vec('g_mix_post'), wf['w_mix_out'], wf['w_br_gm'], wf['w_br_s5'],
        wf['s5_w_glu'], tmb)
    gbig['w_mix_out'] = _mm_tn("dw_mix_out", merged, dmo)
    gbig['w_br_gm'] = _mm_tn("dw_br_gm", ygm, da_br)
    gbig['w_br_s5'] = _mm_tn("dw_br_s5", db_br, ys5)
    gbig['s5_w_glu'] = _mm_tn("dw_s5_glu", yg, dgate)
    slab_of = lambda k: gbig[k].reshape(4, -1, D)
    red_rest = _reduce_big("rest", [slab_of(k) for k in rest])
    s5_tb = max(s5_t, min(S5_TB, seq))
    (dzs5, dbbc, dccm_t, davec, ddvec), got = _b_s5(zs5, dypre, states[:, :, ::s5_tb // s5_t], bbc, ccm, avec, dvec,
                                                    nb, s5_tb, next(red_rest))
    dccm = dccm_t.transpose(0, 2, 1)
    (dzgm, gsm['gm_w_s'], dbs3, gsm['gm_ln_g'], gsm['gm_ln_b']), got = _b_gmlp(
        zgm, dygm, vec('gm_ln_g'), vec('gm_ln_b'), w['gm_w_s'], bs3, tmb, red_rest.send(got))
    gsm['gm_b_s'] = dbs3
    grad_x, gsm['g_mix_pre'] = _b_in(dzgm, dzs5, dzg, dx1, xf, wf['w_in'], vec('g_mix_pre'), tmb, None)
    dw_in_gm, got = _mm_tn("dw_in_gm", dzgm, h, comm=red_rest.send(got))
    dw_in_s5, got = _mm_tn("dw_in_s5", dzs5, h, comm=red_rest.send(got))
    greds = dict(zip(rest, _finish(red_rest, got)))

    unblk = lambda m_: _block_diag_take(m_, 16, 64).transpose(2, 0, 1, 3).reshape(16, -1)
    d_bb_re, d_bb_im = unblk(dbbc[:, :, :512]), unblk(dbbc[:, :, 512:])
    cunblk = lambda m_: _block_diag_take(m_, 64, 16).transpose(0, 1, 3, 2).reshape(32, 16, 64)
    gsm['s5_c_re'] = cunblk(dccm[:, :512, :])
    gsm['s5_c_im'] = -cunblk(dccm[:, 512:, :])
    d_ab_re, d_ab_im = davec[:, :, :512].reshape(1, -1), davec[:, :, 512:].reshape(1, -1)
    g_lr, g_li, g_ls, g_br, g_bi = _s5_disc_bwd(lr_l, li_l, ls_l, br_t, bi_t, (d_ab_re, d_ab_im, d_bb_re, d_bb_im))
    gsm['s5_lam_re'], gsm['s5_lam_im'] = g_lr.reshape(32, 64), g_li.reshape(32, 64)
    gsm['s5_log_step'] = g_ls.reshape(32, 64).sum(axis=1)
    from_t = lambda t: t.reshape(16, 32, 64).transpose(1, 2, 0)
    gsm['s5_b_re'], gsm['s5_b_im'] = from_t(g_br), from_t(g_bi)
    gsm['s5_d'] = ddvec.reshape(32, 16)

    small_shapes = [w[k].shape for k in SMALL]
    spack = _pack([gsm[k].reshape(w[k].shape) for k in SMALL] + [loss_cols], 8)
    me_slot = 4 * lax.axis_index("x") + 2 * lax.axis_index("y") + lax.axis_index("c")
    dw_in_g, (sall,) = _mm_tn("dw_in_g", dzg, h, comm=_gather_all(spack))
    gbig['w_in'] = jnp.concatenate([dw_in_gm, dw_in_s5, dw_in_g], axis=0)
    sall = lax.dynamic_update_index_in_dim(sall, spack, me_slot, 0)
    ssum = _sum_slots("sum_small", sall)
    small_red = _unpack(ssum, small_shapes + [(1, D)])
    loss = 0.5 * jnp.sum(small_red[-1]) / D

    red_in = _reduce_big("w_in", [slab_of('w_in')])
    got = _run_comm("reduce_pair_w_in", next(red_in))
    for phase in ("reduce_step1_w_in", "reduce_step2_w_in"):
        got = _run_comm(phase, red_in.send(got))
    got = _run_comm("share_pair_w_in", red_in.send(got))
    greds['w_in'] = _finish(red_in, got)[0]

    res_big = {}
    for k in big_names:
        r_, c_ = local_shapes[k]
        g = greds[k].reshape(r_, c_) if BIG[k] == 0 else greds[k].reshape(c_, r_).T
        dl, mn, vn = _adamw("adamw_" + k, w[k], g, a['m_' + k][0], a['v_' + k][0], _tile_rows(g.shape[0], 256))
        res_big[k] = (g, dl, mn, vn)
    as2d = lambda t: t.reshape(1, -1) if t.ndim == 1 else t
    dl_s, mn_s, vn_s = _adamw_many("adamw_small", [as2d(w[k]) for k in SMALL], [as2d(g) for g in small_red[:-1]],
                                   [as2d(a['m_' + k][0]) for k in SMALL], [as2d(a['v_' + k][0]) for k in SMALL])
    res_small = {k: (g, dl.reshape(w[k].shape), mn.reshape(w[k].shape), vn.reshape(w[k].shape))
                 for k, g, dl, mn, vn in zip(SMALL, small_red[:-1], dl_s, mn_s, vn_s)}

    res = {**res_big, **res_small}
    outs = [loss, grad_x.reshape(nb, seq, D)]
    for i in range(4):
        outs += [res[k][i][None] for k in WNAMES]
    return tuple(outs)


def _finish(gen, got):
    try:
        gen.send(got)
    except StopIteration as done:
        return done.value
    raise AssertionError("the generator has more exchanges")


def _tile_rows(rows, cap=384):
    best = rows
    for t in range(16, min(rows, cap) + 1, 16):
        if rows % t == 0:
            best = t
    return best
```

```python
import math
from typing import Any, NamedTuple

import jax
import jax.numpy as jnp
from jax import lax
from jax.experimental import pallas as pl
from jax.experimental.pallas import tpu as pltpu

F32 = jnp.float32
BF16 = jnp.bfloat16
EPS = 1e-6
D = 1024
GM_CHUNK = 128
GM_GROUPS = 8
S5_W = 512
S5_SUPER = 4
S5_T = 512
S5_TB = 1024
HEADS = 4
HEAD_DIM = 256
FFN_H = 2816
LANE = 128
VMEM_LIMIT = 56 * 1024 * 1024
MESH = pl.DeviceIdType.MESH

ADAM_LR, ADAM_B1, ADAM_B2, ADAM_EPS, ADAM_WD, ADAM_STEP = 0.001, 0.9, 0.999, 1e-08, 0.01, 10

WNAMES = ['g_mix_pre', 'w_in', 'gm_ln_g', 'gm_ln_b', 'gm_w_s', 'gm_b_s', 's5_lam_re', 's5_lam_im', 's5_log_step',
          's5_b_re', 's5_b_im', 's5_c_re', 's5_c_im', 's5_d', 's5_w_glu', 'w_br_gm', 'w_br_s5', 'w_mix_out',
          'g_mix_post', 'g_ca_pre', 'g_mem', 'ca_w_q', 'ca_w_kv', 'ca_w_o', 'g_ca_post', 'g_ffn_pre', 'ffn_w_gu',
          'ffn_w_down', 'g_ffn_post']
BIG = {'w_in': 1, 's5_w_glu': 0, 'w_br_gm': 0, 'w_br_s5': 1, 'w_mix_out': 0, 'ca_w_q': 0, 'ca_w_kv': 1,
       'ca_w_o': 0, 'ffn_w_gu': 1, 'ffn_w_down': 0}
SMALL = [n for n in WNAMES if n not in BIG]


def _dot(a, b):
    return jnp.dot(a, b, preferred_element_type=F32)


def _dot_nt(a, b):
    return lax.dot_general(a, b, (((1,), (1,)), ((), ())), preferred_element_type=F32)


def _dot_tn(a, b):
    return lax.dot_general(a, b, (((0,), (0,)), ((), ())), preferred_element_type=F32)


def _rms_fwd(x, g):
    r = lax.rsqrt(jnp.mean(x * x, axis=-1, keepdims=True) + EPS)
    return x * r * g


def _rms_bwd(x, g, dy):
    r = lax.rsqrt(jnp.mean(x * x, axis=-1, keepdims=True) + EPS)
    xh = x * r
    gdy = dy * g
    dx = r * (gdy - xh * jnp.mean(gdy * xh, axis=-1, keepdims=True))
    dg = jnp.sum(dy * xh, axis=0, keepdims=True)
    return dx, dg


_GC = math.sqrt(2.0 / math.pi)


def _gelu(x):
    return 0.5 * x * (1.0 + jnp.tanh(_GC * (x + 0.044715 * x * x * x)))


def _gelu_grad(x):
    t = jnp.tanh(_GC * (x + 0.044715 * x * x * x))
    return 0.5 * (1.0 + t) + 0.5 * x * (1.0 - t * t) * _GC * (1.0 + 3 * 0.044715 * x * x)


def _sig(x):
    return 1.0 / (1.0 + jnp.exp(-x))


def _cscan(br, bi, ar, ai, reverse):
    t = br.shape[0]
    row = lax.broadcasted_iota(jnp.int32, br.shape, 0)
    pr, pi = ar, ai
    sh = 1
    while sh < t:
        if reverse:
            keep = row < t - sh
            rr, ri = pltpu.roll(br, t - sh, 0), pltpu.roll(bi, t - sh, 0)
        else:
            keep = row >= sh
            rr, ri = pltpu.roll(br, sh, 0), pltpu.roll(bi, sh, 0)
        rr = jnp.where(keep, rr, 0.0)
        ri = jnp.where(keep, ri, 0.0)
        br, bi = br + pr * rr - pi * ri, bi + pr * ri + pi * rr
        pr, pi = pr * pr - pi * pi, 2.0 * pr * pi
        sh *= 2
    return br, bi


def _s5_disc(lr, li, ls, br, bi):
    step = jnp.exp(ls)
    mag = jnp.exp(lr * step)
    ab_re = mag * jnp.cos(li * step)
    ab_im = mag * jnp.sin(li * step)
    den = lr * lr + li * li
    nr = ab_re - 1.0
    co_re = (nr * lr + ab_im * li) / den
    co_im = (ab_im * lr - nr * li) / den
    return ab_re, ab_im, co_re * br - co_im * bi, co_re * bi + co_im * br


def _params(sem=None):
    return pltpu.CompilerParams(dimension_semantics=sem, vmem_limit_bytes=VMEM_LIMIT)


def _rowcall(name, body, n_rows, tm, ins, outs, scratch=(), comm=None):
    def spec(kind, shape):
        if kind == 'row':
            return pl.BlockSpec((tm, shape[1]), lambda i: (i, 0))
        if kind == 'full':
            nd = len(shape)
            return pl.BlockSpec(tuple(shape), lambda i: (0,) * nd, pipeline_mode=pl.Buffered(1))
        if kind == 'acc':
            nd = len(shape)
            return pl.BlockSpec(tuple(shape), lambda i: (0,) * nd)
        return kind

    in_specs = [spec(k, a.shape) for a, k in ins]
    out_shape, out_specs = [], []
    for k, s, dt in outs:
        shape = (n_rows, s) if k == 'row' else tuple(s)
        out_shape.append(jax.ShapeDtypeStruct(shape, dt))
        out_specs.append(spec(k, shape))
    args = [a for a, _ in ins]
    if comm is not None:
        return _hosting_call(body, name, (n_rows // tm,), in_specs, out_specs, out_shape, list(scratch),
                             ("arbitrary",), args, comm)
    return pl.pallas_call(
        body, name=name, grid=(n_rows // tm,), in_specs=in_specs, out_specs=out_specs, out_shape=out_shape,
        scratch_shapes=list(scratch), compiler_params=_params(("arbitrary",)),
    )(*args)


def _tile(n, cap):
    if n <= cap:
        return n
    best = LANE
    for k in range(1, n // LANE + 1):
        t = k * LANE
        if n % t == 0 and t <= cap:
            best = t
    return best


def _mm_tn(name, a, b, comm=None):
    n, k = a.shape
    m = b.shape[1]
    tk, tn, tr = _tile(k, 1536), _tile(m, 1536), min(n, 2048)

    def body(a_ref, b_ref, o_ref):
        @pl.when(pl.program_id(2) == 0)
        def _():
            o_ref[...] = jnp.zeros_like(o_ref)
        o_ref[...] += _dot_tn(a_ref[...].astype(BF16), b_ref[...].astype(BF16))

    out_spec = pl.BlockSpec((tk, tn), lambda i, j, r: (i, j))
    out_shape = (k, m)
    grid = (k // tk, m // tn, n // tr)
    in_specs = [pl.BlockSpec((tr, tk), lambda i, j, r: (r, i)), pl.BlockSpec((tr, tn), lambda i, j, r: (r, j))]
    if comm is not None:
        (res,), extra = _hosting_call(body, name, grid, in_specs, [out_spec], [jax.ShapeDtypeStruct(out_shape, F32)],
                                      [], ("arbitrary", "arbitrary", "arbitrary"), (a, b), comm)
        return res, extra
    return pl.pallas_call(
        body, name=name, grid=grid, in_specs=in_specs, out_specs=out_spec,
        out_shape=jax.ShapeDtypeStruct(out_shape, F32),
        compiler_params=_params(("parallel", "parallel", "arbitrary")),
    )(a, b)


def _comm_call(name, srcs, outs, nsem, body_fn, aliases=None):
    ns, no = len(srcs), len(outs)

    def body(*refs):
        body_fn(refs[:ns], refs[ns:ns + no], refs[ns + no], refs[ns + no + 1])

    hbm = pl.BlockSpec(memory_space=pltpu.HBM)
    return pl.pallas_call(
        body, name=name, in_specs=[hbm] * ns, out_specs=[hbm] * no,
        out_shape=[jax.ShapeDtypeStruct(s, d) for s, d in outs],
        scratch_shapes=[pltpu.SemaphoreType.DMA((nsem,)), pltpu.SemaphoreType.DMA((nsem,))],
        input_output_aliases=aliases or {},
    )(*srcs)


def _place():
    x, y, c = lax.axis_index("x"), lax.axis_index("y"), lax.axis_index("c")
    chips = (2 * x + y, 2 * (1 - x) + y, 2 * x + (1 - y), 2 * (1 - x) + (1 - y))
    peers = ((1 - x, y, c), (x, 1 - y, c), (x, y, 1 - c))
    return c, chips, peers


class _Comm(NamedTuple):
    srcs: Any
    outs: Any
    nsem: int
    stages: Any
    aliases: Any
    fracs: Any


class _Lazy:
    def __init__(self, make):
        self._make = make

    def start(self):
        self._make().start()

    def wait(self):
        self._make().wait()

    def wait_recv(self):
        self._make().wait_recv()

    def wait_send(self):
        self._make().wait_send()


def _run_comm(name, comm):
    def body_fn(srcs, dsts, ss, rs):
        for stage in comm.stages:
            stage(srcs, dsts, ss, rs)

    return _comm_call(name, comm.srcs, comm.outs, comm.nsem, body_fn, aliases=comm.aliases)


def _hosting_call(body, name, grid, in_specs, out_specs, out_shape, scratch, sem, args, comm):
    hbm = pl.BlockSpec(memory_space=pltpu.HBM)
    n_in, n_out, n_sc = len(in_specs), len(out_specs), len(scratch)
    ns, no = len(comm.srcs), len(comm.outs)
    total = math.prod(grid)
    steps = [min(total - 1, int(f * total)) for f in comm.fracs]

    def wrapped(*refs):
        ins, csrc = refs[:n_in], refs[n_in:n_in + ns]
        outs, cdst = refs[n_in + ns:n_in + ns + n_out], refs[n_in + ns + n_out:n_in + ns + n_out + no]
        rest = refs[n_in + ns + n_out + no:]
        lin = 0
        for d, size in enumerate(grid):
            lin = lin * size + pl.program_id(d)
        for stage, at in zip(comm.stages, steps):
            @pl.when(lin == at)
            def _(stage=stage):
                stage(csrc, cdst, rest[n_sc], rest[n_sc + 1])
        body(*ins, *outs, *rest[:n_sc])

    res = pl.pallas_call(
        wrapped, name=name, grid=grid, in_specs=list(in_specs) + [hbm] * ns, out_specs=list(out_specs) + [hbm] * no,
        out_shape=list(out_shape) + [jax.ShapeDtypeStruct(s, d) for s, d in comm.outs],
        scratch_shapes=list(scratch) + [pltpu.SemaphoreType.DMA((comm.nsem,)), pltpu.SemaphoreType.DMA((comm.nsem,))],
        input_output_aliases={n_in + i: n_out + o for i, o in comm.aliases.items()},
        compiler_params=_params(sem),
    )(*args, *comm.srcs)
    return res[:n_out], res[n_out:]


def _gather_weights(name, shards, axes):
    nw = len(shards)
    shapes = [s.shape for s in shards]
    outs = [((4,) + s if ax == 0 else (s[0], 4 * s[1]), BF16) for s, ax in zip(shapes, axes)]

    def win(ref, i, chip, start, rows):
        r, cols = shapes[i]
        if axes[i] == 0:
            return ref.at[chip, pl.ds(start, rows), :]
        return ref.at[pl.ds(start, rows), pl.ds(chip * cols, cols)]

    def place_body(*refs):
        srcs, dsts, bufs, sem = refs[:nw], refs[nw:2 * nw], refs[2 * nw:3 * nw], refs[3 * nw]
        me = 2 * lax.axis_index("x") + lax.axis_index("y")
        cps = []
        for i in range(nw):
            bufs[i][...] = srcs[i][...].astype(BF16)
            cps.append(pltpu.make_async_copy(bufs[i], win(dsts[i], i, me, 0, shapes[i][0]), sem.at[i]))
            cps[-1].start()
        for cp in cps:
            cp.wait()

    placed = pl.pallas_call(
        place_body, name=name, out_shape=[jax.ShapeDtypeStruct(s, d) for s, d in outs],
        in_specs=[pl.BlockSpec(memory_space=pltpu.VMEM)] * nw, out_specs=[pl.BlockSpec(memory_space=pltpu.HBM)] * nw,
        scratch_shapes=[pltpu.VMEM(s, BF16) for s in shapes] + [pltpu.SemaphoreType.DMA((nw,))],
        compiler_params=_params(),
    )(*shards)

    def copies(dsts, ss, rs):
        c, (me, xn, yn, dg), (px, py, psib) = _place()

        def rdma(k, window, peer):
            return _Lazy(lambda: pltpu.make_async_remote_copy(
                src_ref=window, dst_ref=window, send_sem=ss.at[k], recv_sem=rs.at[k], device_id=peer,
                device_id_type=MESH))

        first, later, swaps = [], [], []
        for i in range(nw):
            qr, hr, k0 = shapes[i][0] // 4, shapes[i][0] // 2, 6 * i
            q0, q1 = 2 * c * qr, (2 * c + 1) * qr
            w0, w1 = win(dsts[i], i, me, q0, qr), win(dsts[i], i, me, q1, qr)
            first.append([rdma(k0, w0, px), rdma(k0 + 1, w1, py), rdma(k0 + 2, w0, py), rdma(k0 + 3, w1, px)])
            later.append([rdma(k0 + 4, win(dsts[i], i, xn, q0, qr), py), rdma(k0 + 5, win(dsts[i], i, yn, q1, qr), px)])
            swaps.append([rdma(6 * nw + 3 * i + j, win(dsts[i], i, chip, c * hr, hr), psib)
                          for j, chip in enumerate((xn, yn, dg))])
        return first, later, swaps

    def send(srcs, dsts, ss, rs):
        for cps in copies(dsts, ss, rs)[0]:
            for cp in cps:
                cp.start()

    def forward(srcs, dsts, ss, rs):
        first, later, _ = copies(dsts, ss, rs)
        for i in range(nw):
            first[i][0].wait_recv()
            later[i][0].start()
            first[i][1].wait_recv()
            later[i][1].start()

    def swap(srcs, dsts, ss, rs):
        first, later, swaps = copies(dsts, ss, rs)
        for i in range(nw):
            for cp in first[i][2:] + later[i]:
                cp.wait_recv()
            for sw in swaps[i]:
                sw.start()

    def finish(srcs, dsts, ss, rs):
        first, later, swaps = copies(dsts, ss, rs)
        for i in range(nw):
            for sw in swaps[i]:
                sw.wait()
            for cp in first[i] + later[i]:
                cp.wait_send()

    return _Comm(placed, outs, 9 * nw, [send, forward, swap, finish], {i: i for i in range(nw)},
                 (0.0, 0.55, 0.85, 1.0))


def _gather_all(src):
    def copies(srcs, dsts, ss, rs):
        x, y, c = lax.axis_index("x"), lax.axis_index("y"), lax.axis_index("c")
        me, sib = (x, y, c), (x, y, 1 - c)
        chips = [(1 - x, y), (x, 1 - y), (1 - x, 1 - y)]

        def rows(px, py, pc):
            return dsts[0].at[4 * px + 2 * py + pc]

        def copy(k, block, to, src=None):
            return _Lazy(lambda: pltpu.make_async_remote_copy(
                src_ref=rows(*block) if src is None else src, dst_ref=rows(*block), send_sem=ss.at[k],
                recv_sem=rs.at[k], device_id=to, device_id_type=MESH))

        first = [copy(0, me, sib, src=srcs[0])] + [copy(1 + j, me, (*chip, c), src=srcs[0])
                                                   for j, chip in enumerate(chips)]
        passed = [copy(4 + j, (*chip, c), sib) for j, chip in enumerate(chips)]
        landed = [copy(0, sib, me)] + [copy(1 + j, (*chip, c), me) for j, chip in enumerate(chips)]
        landed += [copy(4 + j, (*chip, 1 - c), me) for j, chip in enumerate(chips)]
        return first, passed, landed

    def send(*refs):
        for cp in copies(*refs)[0]:
            cp.start()

    def forward(*refs):
        _, passed, landed = copies(*refs)
        for j in range(3):
            landed[1 + j].wait_recv()
            passed[j].start()

    def finish(*refs):
        first, passed, landed = copies(*refs)
        landed[0].wait_recv()
        for cp in landed[4:]:
            cp.wait_recv()
        for cp in first + passed:
            cp.wait_send()

    return _Comm([src], [((8,) + src.shape, src.dtype)], 7, [send, forward, finish], {}, (0.0, 0.9, 1.0))


def _f_in(x, g, w_in, tm):
    n = x.shape[0]

    def body(x_ref, g_ref, w_ref, h_ref, zgm_ref, zs5_ref, zg_ref):
        h = _rms_fwd(x_ref[...], g_ref[...]).astype(BF16)
        h_ref[...] = h
        zgm_ref[...] = _dot(h, w_ref[:, 0:2 * D])
        zs5_ref[...] = _dot(h, w_ref[:, 2 * D:2 * D + S5_W])
        zg_ref[...] = _dot(h, w_ref[:, 2 * D + S5_W:])

    return _rowcall("f_in", body, n, tm, [(x, 'row'), (g, 'full'), (w_in, 'full')],
                    [('row', D, BF16), ('row', 2 * D, F32), ('row', S5_W, F32), ('row', 2 * D, F32)])


def _tril():
    r = lax.broadcasted_iota(jnp.int32, (GM_CHUNK, GM_CHUNK), 0)
    c = lax.broadcasted_iota(jnp.int32, (GM_CHUNK, GM_CHUNK), 1)
    return r >= c


def _ln_stats(v):
    mu = jnp.mean(v, axis=-1, keepdims=True)
    vc = v - mu
    r = lax.rsqrt(jnp.mean(vc * vc, axis=-1, keepdims=True) + EPS)
    return vc * r, r


def _f_gmlp(zgm, ln_g, ln_b, w_s, b_s, tm):
    n = zgm.shape[0]

    def body(z_ref, lg_ref, lb_ref, ws_ref, bs_ref, y_ref):
        zg = _gelu(z_ref[...].astype(BF16))
        u = zg[:, :D]
        vh, _ = _ln_stats(zg[:, D:].astype(F32))
        vn = (vh * lg_ref[...] + lb_ref[...]).astype(BF16)
        keep = _tril()
        for g in range(GM_GROUPS):
            w = jnp.where(keep, ws_ref[g], 0.0).astype(BF16)
            cs = slice(g * LANE, (g + 1) * LANE)
            for c in range(tm // GM_CHUNK):
                rs = slice(c * GM_CHUNK, (c + 1) * GM_CHUNK)
                sv = _dot(w, vn[rs, cs]) + bs_ref[g]
                y_ref[rs, cs] = u[rs, cs] * sv.astype(BF16)

    return _rowcall("f_gmlp", body, n, tm,
                    [(zgm, 'row'), (ln_g, 'full'), (ln_b, 'full'), (w_s, 'full'), (b_s, 'full')],
                    [('row', D, BF16)])[0]


def _s5_specs(nb, nc, t, rev):
    def cc(c):
        return nc - 1 - c if rev else c
    slab = pl.BlockSpec((t, LANE), lambda j, b, c: (b * nc + cc(c), j))
    bb = pl.BlockSpec((None, LANE, 1024), lambda j, b, c: (j, 0, 0))
    cm = pl.BlockSpec((None, 1024, LANE), lambda j, b, c: (j, 0, 0))
    av = pl.BlockSpec((None, 1, 1024), lambda j, b, c: (j, 0, 0))
    dv = pl.BlockSpec((None, 1, LANE), lambda j, b, c: (j, 0, 0))
    st = pl.BlockSpec((None, None, None, 1, 1024), lambda j, b, c: (j, b, cc(c), 0, 0))
    return slab, bb, cm, av, dv, st


def _seg_load(ref, seg):
    return jnp.concatenate([ref[pl.ds(k, 8, stride=seg), :] for k in range(seg)], axis=0)


def _seg_store(ref, val, seg):
    for k in range(seg):
        ref[pl.ds(k, 8, stride=seg), :] = val[8 * k:8 * k + 8, :]


def _seg_scan(s_ref, ar, ai, seg, reverse):
    def step(i, carry):
        k = seg - 1 - i if reverse else i
        sr, si = carry
        nr = ar * sr - ai * si + s_ref[k, :, :512]
        ni = ar * si + ai * sr + s_ref[k, :, 512:]
        s_ref[k, :, :512] = nr
        s_ref[k, :, 512:] = ni
        return nr, ni

    zero = jnp.zeros((8, 512), F32)
    return lax.fori_loop(0, seg, step, (zero, zero), unroll=4)


def _seg_entries(tr, ti, pr, pi, cin, reverse):
    row = lax.broadcasted_iota(jnp.int32, (8, 512), 0)
    edge = row == (7 if reverse else 0)
    cr, ci = cin[:, :512], cin[:, 512:]
    xr = tr + jnp.where(edge, pr * cr - pi * ci, 0.0)
    xi = ti + jnp.where(edge, pr * ci + pi * cr, 0.0)
    ir, ii = _cscan(xr, xi, pr, pi, reverse)
    shift = 7 if reverse else 1
    er = jnp.where(edge, cr, pltpu.roll(ir, shift, 0))
    ei = jnp.where(edge, ci, pltpu.roll(ii, shift, 0))
    far = row == (0 if reverse else 7)
    out = jnp.concatenate([jnp.sum(jnp.where(far, ir, 0.0), axis=0, keepdims=True),
                           jnp.sum(jnp.where(far, ii, 0.0), axis=0, keepdims=True)], axis=1)
    return er, ei, out


def _seg_power(ar, ai, seg):
    pr, pi = ar, ai
    for _ in range(seg.bit_length() - 1):
        pr, pi = pr * pr - pi * pi, 2.0 * pr * pi
    return pr, pi


def _f_s5(zs5, bbc, ccm, avec, dvec, nb, t, comm):
    n = zs5.shape[0]
    nc = n // nb // t
    seg = t // 8
    slab, bb, cm, av, dv, st = _s5_specs(nb, nc, t, False)

    def body(u_ref, bb_ref, cc_ref, a_ref, d_ref, y_ref, st_ref, carry, s_ref):
        @pl.when(pl.program_id(2) == 0)
        def _():
            carry[...] = jnp.zeros_like(carry)
        cin = carry[...]
        st_ref[...] = cin
        up = _seg_load(u_ref, seg)
        s_ref[...] = _dot(up.astype(BF16), bb_ref[...]).reshape(seg, 8, 1024)
        a1r, a1i = a_ref[:, :512], a_ref[:, 512:]
        ar, ai = jnp.broadcast_to(a1r, (8, 512)), jnp.broadcast_to(a1i, (8, 512))
        tr, ti = _seg_scan(s_ref, ar, ai, seg, False)
        er, ei, cout = _seg_entries(tr, ti, *_seg_power(a1r, a1i, seg), cin, False)
        carry[...] = cout

        def apply(k, pe):
            pr, pi = pe
            pr, pi = ar * pr - ai * pi, ar * pi + ai * pr
            s_ref[k, :, :512] = s_ref[k, :, :512] + pr
            s_ref[k, :, 512:] = s_ref[k, :, 512:] + pi
            return pr, pi

        lax.fori_loop(0, seg, apply, (er, ei), unroll=4)
        s = s_ref[...].reshape(t, 1024).astype(BF16)
        _seg_store(y_ref, _dot(s, cc_ref[...]) + d_ref[...] * up, seg)

    return _hosting_call(
        body, "f_s5", (S5_SUPER, nb, nc), [slab, bb, cm, av, dv], [slab, st],
        [jax.ShapeDtypeStruct((n, S5_W), F32), jax.ShapeDtypeStruct((S5_SUPER, nb, nc, 1, 1024), F32)],
        [pltpu.VMEM((1, 1024), F32), pltpu.VMEM((seg, 8, 1024), F32)], ("arbitrary", "arbitrary", "arbitrary"),
        (zs5, bbc, ccm, avec, dvec), comm)


def _f_mix(ygm, ypre, zg, x, w_glu, w_br_gm, w_br_s5, w_mix_out, g_post, tm):
    n = x.shape[0]

    def body(ygm_ref, ypre_ref, zg_ref, x_ref, wglu_ref, wgm_ref, ws5_ref, wout_ref, g_ref,
             yg_ref, ys5_ref, a_ref, b_ref, mg_ref, mo_ref, x1_ref):
        yg = _gelu(ypre_ref[...])
        ygb = yg.astype(BF16)
        yg_ref[...] = ygb
        ys5 = (yg * _sig(_dot(ygb, wglu_ref[...]))).astype(BF16)
        ys5_ref[...] = ys5
        a = _dot(ygm_ref[...], wgm_ref[...])
        b = _dot(ys5, ws5_ref[...])
        a_ref[...] = a.astype(BF16)
        b_ref[...] = b.astype(BF16)
        zg = zg_ref[...]
        merged = (_sig(zg[:, :D]) * a + _sig(zg[:, D:]) * b).astype(BF16)
        mg_ref[...] = merged
        mo = _dot(merged, wout_ref[...])
        mo_ref[...] = mo
        x1_ref[...] = x_ref[...] + _rms_fwd(mo, g_ref[...])

    return _rowcall("f_mix", body, n, tm,
                    [(ygm, 'row'), (ypre, 'row'), (zg, 'row'), (x, 'row'), (w_glu, 'full'), (w_br_gm, 'full'),
                     (w_br_s5, 'full'), (w_mix_out, 'full'), (g_post, 'full')],
                    [('row', S5_W, BF16), ('row', S5_W, BF16), ('row', D, BF16), ('row', D, BF16),
                     ('row', D, BF16), ('row', D, F32), ('row', D, F32)])


def _f_mem(mem, g_mem, w_kv, tm):
    n = mem.shape[0]

    def body(m_ref, g_ref, w_ref, mn_ref, k_ref, v_ref):
        mn = _rms_fwd(m_ref[...], g_ref[...]).astype(BF16)
        mn_ref[...] = mn
        k_ref[...] = _dot(mn, w_ref[:, :D]).astype(BF16)
        v_ref[...] = _dot(mn, w_ref[:, D:]).astype(BF16)

    return _rowcall("f_mem", body, n, tm, [(mem, 'row'), (g_mem, 'full'), (w_kv, 'full')],
                    [('row', D, BF16), ('row', D, BF16), ('row', D, BF16)])


def _softmax(s):
    m = jnp.max(s, axis=-1, keepdims=True)
    e = jnp.exp(s - m)
    return e / jnp.sum(e, axis=-1, keepdims=True)


def _f_attn(x1, g_pre, w_q, k, v, w_o, g_post, tm, tpb, mlen):
    n = x1.shape[0]
    kv_spec = pl.BlockSpec((mlen, D), lambda i: (i // tpb, 0))
    scale = HEAD_DIM ** -0.5

    def body(x_ref, gp_ref, wq_ref, k_ref, v_ref, wo_ref, go_ref, hc_ref, q_ref, o_ref, ao_ref, x2_ref):
        x1v = x_ref[...]
        hc = _rms_fwd(x1v, gp_ref[...]).astype(BF16)
        hc_ref[...] = hc
        q_ref[...] = _dot(hc, wq_ref[...]).astype(BF16)
        for h in range(HEADS):
            hs = slice(h * HEAD_DIM, (h + 1) * HEAD_DIM)
            p = _softmax(_dot_nt(q_ref[:, hs], k_ref[:, hs]) * scale)
            o_ref[:, hs] = _dot(p.astype(BF16), v_ref[:, hs]).astype(BF16)
        ao = _dot(o_ref[...], wo_ref[...])
        ao_ref[...] = ao
        x2_ref[...] = x1v + _rms_fwd(ao, go_ref[...])

    return _rowcall("f_attn", body, n, tm,
                    [(x1, 'row'), (g_pre, 'full'), (w_q, 'full'), (k, kv_spec), (v, kv_spec), (w_o, 'full'),
                     (g_post, 'full')],
                    [('row', D, BF16), ('row', D, BF16), ('row', D, BF16), ('row', D, F32), ('row', D, F32)])


def _f_ffn_up(x2, g_pre, w_gu, tm):
    n = x2.shape[0]

    def body(x_ref, g_ref, w_ref, hf_ref, gu_ref, act_ref):
        hf = _rms_fwd(x_ref[...], g_ref[...]).astype(BF16)
        hf_ref[...] = hf
        gg = _dot(hf, w_ref[:, :FFN_H])
        uu = _dot(hf, w_ref[:, FFN_H:])
        gu_ref[:, :FFN_H] = gg.astype(BF16)
        gu_ref[:, FFN_H:] = uu.astype(BF16)
        act_ref[...] = (gg * _sig(gg) * uu).astype(BF16)

    return _rowcall("f_ffn_up", body, n, tm, [(x2, 'row'), (g_pre, 'full'), (w_gu, 'full')],
                    [('row', D, BF16), ('row', 2 * FFN_H, BF16), ('row', FFN_H, BF16)])


def _f_ffn_down(act, x2, tgt, w_down, g_post, tm):
    n = x2.shape[0]

    def body(a_ref, x_ref, t_ref, w_ref, g_ref, dn_ref, d3_ref, loss_ref):
        @pl.when(pl.program_id(0) == 0)
        def _():
            loss_ref[...] = jnp.zeros_like(loss_ref)
        dn = _dot(a_ref[...], w_ref[...])
        dn_ref[...] = dn
        err = x_ref[...] + _rms_fwd(dn, g_ref[...]) - t_ref[...]
        d3_ref[...] = err * (1.0 / D)
        loss_ref[...] += jnp.sum(err * err, axis=0, keepdims=True)

    return _rowcall("f_ffn_down", body, n, tm,
                    [(act, 'row'), (x2, 'row'), (tgt, 'row'), (w_down, 'full'), (g_post, 'full')],
                    [('row', D, F32), ('row', D, F32), ('acc', (1, D), F32)])


def _b_ffn_down(d3, dn, gu, g_post, w_down, tm):
    n = d3.shape[0]

    def body(d3_ref, dn_ref, gu_ref, g_ref, w_ref, ddn_ref, dgu_ref, dg_ref):
        @pl.when(pl.program_id(0) == 0)
        def _():
            dg_ref[...] = jnp.zeros_like(dg_ref)
        ddn, dg = _rms_bwd(dn_ref[...], g_ref[...], d3_ref[...])
        dg_ref[...] += dg
        ddn = ddn.astype(BF16)
        ddn_ref[...] = ddn
        dact = _dot_nt(ddn, w_ref[...]).astype(BF16)
        gg = gu_ref[:, :FFN_H]
        uu = gu_ref[:, FFN_H:]
        sg = _sig(gg)
        silu = gg * sg
        dgu_ref[:, :FFN_H] = dact * uu * (sg + silu * (1.0 - sg))
        dgu_ref[:, FFN_H:] = dact * silu

    return _rowcall("b_ffn_down", body, n, tm,
                    [(d3, 'row'), (dn, 'row'), (gu, 'row'), (g_post, 'full'), (w_down, 'full')],
                    [('row', D, BF16), ('row', 2 * FFN_H, BF16), ('acc', (1, D), F32)])


def _b_ffn_up(dgu, d3, x2, w_gu, g_pre, tm):
    n = d3.shape[0]

    def body(dgu_ref, d3_ref, x_ref, w_ref, g_ref, dx_ref, dg_ref):
        @pl.when(pl.program_id(0) == 0)
        def _():
            dg_ref[...] = jnp.zeros_like(dg_ref)
        dhf = _dot_nt(dgu_ref[...], w_ref[...])
        dx, dg = _rms_bwd(x_ref[...], g_ref[...], dhf)
        dg_ref[...] += dg
        dx_ref[...] = d3_ref[...] + dx

    return _rowcall("b_ffn_up", body, n, tm,
                    [(dgu, 'row'), (d3, 'row'), (x2, 'row'), (w_gu, 'full'), (g_pre, 'full')],
                    [('row', D, F32), ('acc', (1, D), F32)])


def _b_attn(dx2, ao, x1, q, g_post, w_o, w_q, g_pre, k, v, tm, tpb, mlen):
    n = dx2.shape[0]
    nb = n // (tm * tpb)
    kv_spec = pl.BlockSpec((mlen, D), lambda i: (i // tpb, 0))
    scale = HEAD_DIM ** -0.5

    def body(dx2_ref, ao_ref, x1_ref, q_ref, go_ref, wo_ref, wq_ref, gp_ref, k_ref, v_ref,
             dao_ref, dq_ref, dx1_ref, dk_ref, dv_ref, dgo_ref, dgp_ref):
        i = pl.program_id(0)

        @pl.when(i == 0)
        def _():
            dgo_ref[...] = jnp.zeros_like(dgo_ref)
            dgp_ref[...] = jnp.zeros_like(dgp_ref)

        @pl.when(i % tpb == 0)
        def _():
            dk_ref[...] = jnp.zeros_like(dk_ref)
            dv_ref[...] = jnp.zeros_like(dv_ref)

        dx2v = dx2_ref[...]
        dao, dgo = _rms_bwd(ao_ref[...], go_ref[...], dx2v)
        dgo_ref[...] += dgo
        dao = dao.astype(BF16)
        dao_ref[...] = dao
        do = _dot_nt(dao, wo_ref[...])
        for h in range(HEADS):
            hs = slice(h * HEAD_DIM, (h + 1) * HEAD_DIM)
            qh = q_ref[:, hs]
            kh = k_ref[:, hs]
            p = _softmax(_dot_nt(qh, kh) * scale)
            doh = do[:, hs].astype(BF16)
            dp = _dot_nt(doh, v_ref[:, hs])
            ds = (p * (dp - jnp.sum(dp * p, axis=-1, keepdims=True)) * scale).astype(BF16)
            dq_ref[:, hs] = _dot(ds, kh).astype(BF16)
            dk_ref[:, hs] += _dot_tn(ds, qh)
            dv_ref[:, hs] += _dot_tn(p.astype(BF16), doh)
        dhc = _dot_nt(dq_ref[...], wq_ref[...])
        dx, dgp = _rms_bwd(x1_ref[...], gp_ref[...], dhc)
        dgp_ref[...] += dgp
        dx1_ref[...] = dx2v + dx

    return _rowcall("b_attn", body, n, tm,
                    [(dx2, 'row'), (ao, 'row'), (x1, 'row'), (q, 'row'), (g_post, 'full'), (w_o, 'full'),
                     (w_q, 'full'), (g_pre, 'full'), (k, kv_spec), (v, kv_spec)],
                    [('row', D, BF16), ('row', D, BF16), ('row', D, F32),
                     (kv_spec, (nb * mlen, D), F32), (kv_spec, (nb * mlen, D), F32),
                     ('acc', (1, D), F32), ('acc', (1, D), F32)])


def _b_mem(dk, dv, mem, w_kv, g_mem, tm):
    n = mem.shape[0]

    def body(dk_ref, dv_ref, m_ref, w_ref, g_ref, dkv_ref, dg_ref):
        @pl.when(pl.program_id(0) == 0)
        def _():
            dg_ref[...] = jnp.zeros_like(dg_ref)
        dkb = dk_ref[...].astype(BF16)
        dvb = dv_ref[...].astype(BF16)
        dkv_ref[:, :D] = dkb
        dkv_ref[:, D:] = dvb
        dmn = _dot_nt(dkb, w_ref[:, :D]) + _dot_nt(dvb, w_ref[:, D:])
        _, dg = _rms_bwd(m_ref[...], g_ref[...], dmn)
        dg_ref[...] += dg

    return _rowcall("b_mem", body, n, tm, [(dk, 'row'), (dv, 'row'), (mem, 'row'), (w_kv, 'full'), (g_mem, 'full')],
                    [('row', 2 * D, BF16), ('acc', (1, D), F32)])


def _b_mix(dx1, mo, a, b, zg, ypre, g_post, w_mix_out, w_br_gm, w_br_s5, w_glu, tm):
    n = dx1.shape[0]

    def body(dx1_ref, mo_ref, a_ref, b_ref, zg_ref, ypre_ref, g_ref, wout_ref, wgm_ref, ws5_ref, wglu_ref,
             dmo_ref, da_ref, db_ref, dzg_ref, dygm_ref, dgate_ref, dypre_ref, dg_ref):
        @pl.when(pl.program_id(0) == 0)
        def _():
            dg_ref[...] = jnp.zeros_like(dg_ref)
        dmo, dg = _rms_bwd(mo_ref[...], g_ref[...], dx1_ref[...])
        dg_ref[...] += dg
        dmo = dmo.astype(BF16)
        dmo_ref[...] = dmo
        dmg = _dot_nt(dmo, wout_ref[...]).astype(BF16)
        zg = zg_ref[...].astype(BF16)
        sa, sb = _sig(zg[:, :D]), _sig(zg[:, D:])
        da = dmg * sa
        db = dmg * sb
        da_ref[...] = da
        db_ref[...] = db
        dzg_ref[:, :D] = dmg * a_ref[...] * (sa * (1.0 - sa))
        dzg_ref[:, D:] = dmg * b_ref[...] * (sb * (1.0 - sb))
        dygm_ref[...] = _dot_nt(da, wgm_ref[...])
        dys5 = _dot_nt(db, ws5_ref[...])
        ypre = ypre_ref[...]
        yg = _gelu(ypre)
        sgt = _sig(_dot(yg.astype(BF16), wglu_ref[...]))
        dgate = (dys5 * yg * sgt * (1.0 - sgt)).astype(BF16)
        dgate_ref[...] = dgate
        dyg = dys5 * sgt + _dot_nt(dgate, wglu_ref[...])
        dypre_ref[...] = dyg * _gelu_grad(ypre)

    return _rowcall("b_mix", body, n, tm,
                    [(dx1, 'row'), (mo, 'row'), (a, 'row'), (b, 'row'), (zg, 'row'), (ypre, 'row'), (g_post, 'full'),
                     (w_mix_out, 'full'), (w_br_gm, 'full'), (w_br_s5, 'full'), (w_glu, 'full')],
                    [('row', D, BF16), ('row', D, BF16), ('row', D, BF16), ('row', 2 * D, BF16), ('row', D, F32),
                     ('row', S5_W, BF16), ('row', S5_W, F32), ('acc', (1, D), F32)])


def _b_s5(zs5, dypre, states, bbc, ccm, avec, dvec, nb, t, comm):
    n = zs5.shape[0]
    nc = n // nb // t
    seg = t // 8
    slab, bb, cm, av, dv, st = _s5_specs(nb, nc, t, True)

    def body(u_ref, dy_ref, st_ref, bb_ref, cc_ref, a_ref, d_ref,
             du_ref, dbb_ref, dcc_ref, da_ref, dd_ref, lcarry, s_ref, l_ref):
        first = jnp.logical_and(pl.program_id(1) == 0, pl.program_id(2) == 0)

        @pl.when(first)
        def _():
            dbb_ref[...] = jnp.zeros_like(dbb_ref)
            dcc_ref[...] = jnp.zeros_like(dcc_ref)
            da_ref[...] = jnp.zeros_like(da_ref)
            dd_ref[...] = jnp.zeros_like(dd_ref)

        @pl.when(pl.program_id(2) == 0)
        def _():
            lcarry[...] = jnp.zeros_like(lcarry)

        up, dyp = _seg_load(u_ref, seg), _seg_load(dy_ref, seg)
        ub, dyb = up.astype(BF16), dyp.astype(BF16)
        a1r, a1i = a_ref[:, :512], a_ref[:, 512:]
        ar, ai = jnp.broadcast_to(a1r, (8, 512)), jnp.broadcast_to(a1i, (8, 512))
        pr, pi = _seg_power(a1r, a1i, seg)

        s_ref[...] = _dot(ub, bb_ref[...]).reshape(seg, 8, 1024)
        tr, ti = _seg_scan(s_ref, ar, ai, seg, False)
        er, ei, _ = _seg_entries(tr, ti, pr, pi, st_ref[...], False)

        def apply(k, pe):
            qr, qi = pe
            qr, qi = ar * qr - ai * qi, ar * qi + ai * qr
            s_ref[k, :, :512] = s_ref[k, :, :512] + qr
            s_ref[k, :, 512:] = s_ref[k, :, 512:] + qi
            return qr, qi

        lax.fori_loop(0, seg, apply, (er, ei), unroll=4)

        l_ref[...] = _dot_nt(dyb, cc_ref[...]).reshape(seg, 8, 1024)
        tr, ti = _seg_scan(l_ref, ar, -ai, seg, True)
        fr, fi, lout = _seg_entries(tr, ti, pr, -pi, lcarry[...], True)
        lcarry[...] = lout

        def apply_back(i, carry):
            qr, qi, accr, acci = carry
            k = seg - 1 - i
            qr, qi = ar * qr + ai * qi, ar * qi - ai * qr
            lr = l_ref[k, :, :512] + qr
            li = l_ref[k, :, 512:] + qi
            l_ref[k, :, :512] = lr
            l_ref[k, :, 512:] = li
            kp = jnp.maximum(k - 1, 0)
            sr = jnp.where(k == 0, er, s_ref[kp, :, :512])
            si = jnp.where(k == 0, ei, s_ref[kp, :, 512:])
            return qr, qi, accr + lr * sr + li * si, acci + li * sr - lr * si

        zero = jnp.zeros((8, 512), F32)
        _, _, accr, acci = lax.fori_loop(0, seg, apply_back, (fr, fi, zero, zero), unroll=4)
        da_ref[:, :512] += jnp.sum(accr, axis=0, keepdims=True)
        da_ref[:, 512:] += jnp.sum(acci, axis=0, keepdims=True)

        s = s_ref[...].reshape(t, 1024).astype(BF16)
        lam = l_ref[...].reshape(t, 1024).astype(BF16)
        dcc_ref[...] += _dot_tn(dyb, s)
        dbb_ref[...] += _dot_tn(ub, lam)
        _seg_store(du_ref, _dot_nt(lam, bb_ref[...]) + d_ref[...] * dyp, seg)
        dd_ref[...] += jnp.sum(dyp * up, axis=0, keepdims=True)

    return _hosting_call(
        body, "b_s5", (S5_SUPER, nb, nc), [slab, slab, st, bb, cm, av, dv], [slab, bb, bb, av, dv],
        [jax.ShapeDtypeStruct((n, S5_W), F32), jax.ShapeDtypeStruct(bbc.shape, F32),
         jax.ShapeDtypeStruct(bbc.shape, F32), jax.ShapeDtypeStruct(avec.shape, F32),
         jax.ShapeDtypeStruct(dvec.shape, F32)],
        [pltpu.VMEM((1, 1024), F32), pltpu.VMEM((seg, 8, 1024), F32), pltpu.VMEM((seg, 8, 1024), F32)],
        ("arbitrary", "arbitrary", "arbitrary"), (zs5, dypre, states, bbc, ccm, avec, dvec), comm)


def _b_gmlp(zgm, dygm, ln_g, ln_b, w_s, b_s, tm, comm):
    n = zgm.shape[0]

    def body(z_ref, dy_ref, lg_ref, lb_ref, ws_ref, bs_ref, dz_ref, dws_ref, dbs_ref, dlg_ref, dlb_ref,
             du_s, dvn_s):
        @pl.when(pl.program_id(0) == 0)
        def _():
            dws_ref[...] = jnp.zeros_like(dws_ref)
            dbs_ref[...] = jnp.zeros_like(dbs_ref)
            dlg_ref[...] = jnp.zeros_like(dlg_ref)
            dlb_ref[...] = jnp.zeros_like(dlb_ref)
        z = z_ref[...].astype(BF16)
        zg = _gelu(z)
        u = zg[:, :D]
        vh, r = _ln_stats(zg[:, D:].astype(F32))
        vn = (vh * lg_ref[...] + lb_ref[...]).astype(BF16)
        dy = dy_ref[...]
        keep = _tril()
        for g in range(GM_GROUPS):
            w = jnp.where(keep, ws_ref[g], 0.0).astype(BF16)
            cs = slice(g * LANE, (g + 1) * LANE)
            for c in range(tm // GM_CHUNK):
                rs = slice(c * GM_CHUNK, (c + 1) * GM_CHUNK)
                vb = vn[rs, cs]
                sv = _dot(w, vb) + bs_ref[g]
                dyb = dy[rs, cs]
                du_s[rs, cs] = dyb * sv
                dsv = dyb * u[rs, cs]
                dsvb = dsv.astype(BF16)
                dvn_s[rs, cs] = _dot_tn(w, dsvb)
                dws_ref[g] += jnp.where(keep, _dot_nt(dsvb, vb), 0.0)
                dbs_ref[g] += jnp.sum(dsv, axis=1, keepdims=True)
        dvn = dvn_s[...]
        dlg_ref[...] += jnp.sum(dvn * vh, axis=0, keepdims=True)
        dlb_ref[...] += jnp.sum(dvn, axis=0, keepdims=True)
        dvh = dvn * lg_ref[...]
        dv = r * (dvh - jnp.mean(dvh, axis=-1, keepdims=True) - vh * jnp.mean(dvh * vh, axis=-1, keepdims=True))
        dz_ref[:, :D] = du_s[...].astype(BF16) * _gelu_grad(z[:, :D])
        dz_ref[:, D:] = dv.astype(BF16) * _gelu_grad(z[:, D:])

    return _rowcall("b_gmlp", body, n, tm,
                    [(zgm, 'row'), (dygm, 'row'), (ln_g, 'full'), (ln_b, 'full'), (w_s, 'full'), (b_s, 'full')],
                    [('row', 2 * D, BF16), ('acc', w_s.shape, F32), ('acc', b_s.shape, F32), ('acc', (1, D), F32),
                     ('acc', (1, D), F32)],
                    scratch=[pltpu.VMEM((tm, D), F32), pltpu.VMEM((tm, D), F32)], comm=comm)


def _b_in(dzgm, dzs5, dzg, dx1, x, w_in, g_pre, tm, comm):
    n = x.shape[0]

    def body(d1_ref, d2_ref, d3_ref, dx1_ref, x_ref, w_ref, g_ref, gx_ref, dg_ref):
        @pl.when(pl.program_id(0) == 0)
        def _():
            dg_ref[...] = jnp.zeros_like(dg_ref)
        dh = (_dot_nt(d1_ref[...], w_ref[:, 0:2 * D]) + _dot_nt(d2_ref[...].astype(BF16), w_ref[:, 2 * D:2 * D + S5_W])
              + _dot_nt(d3_ref[...], w_ref[:, 2 * D + S5_W:]))
        dx, dg = _rms_bwd(x_ref[...], g_ref[...], dh)
        dg_ref[...] += dg
        gx_ref[...] = dx1_ref[...] + dx

    return _rowcall("b_in", body, n, tm,
                    [(dzgm, 'row'), (dzs5, 'row'), (dzg, 'row'), (dx1, 'row'), (x, 'row'), (w_in, 'full'),
                     (g_pre, 'full')],
                    [('row', D, F32), ('acc', (1, D), F32)], comm=comm)


def _whole(name, body, ins, outs):
    return pl.pallas_call(body, name=name, out_shape=[jax.ShapeDtypeStruct(s, dt) for s, dt in outs],
                          compiler_params=_params())(*ins)


def _s5_disc_fwd(lr, li, ls, br, bi):
    def body(lr_ref, li_ref, ls_ref, br_ref, bi_ref, o1, o2, o3, o4):
        outs = _s5_disc(lr_ref[...], li_ref[...], ls_ref[...], br_ref[...], bi_ref[...])
        for o, val in zip((o1, o2, o3, o4), outs):
            o[...] = val

    return _whole("s5_disc_fwd", body, [lr, li, ls, br, bi],
                  [(lr.shape, F32), (lr.shape, F32), (br.shape, F32), (br.shape, F32)])


def _s5_disc_bwd(lr, li, ls, br, bi, cts):
    def body(lr_ref, li_ref, ls_ref, br_ref, bi_ref, c1, c2, c3, c4, o1, o2, o3, o4, o5):
        _, vjp = jax.vjp(_s5_disc, lr_ref[...], li_ref[...], ls_ref[...], br_ref[...], bi_ref[...])
        grads = vjp((c1[...], c2[...], c3[...], c4[...]))
        for o, val in zip((o1, o2, o3, o4, o5), grads):
            o[...] = val

    return _whole("s5_disc_bwd", body, [lr, li, ls, br, bi, *cts],
                  [(lr.shape, F32), (lr.shape, F32), (lr.shape, F32), (br.shape, F32), (br.shape, F32)])


def _sum_slots(name, slots):
    def body(s_ref, o_ref):
        acc = s_ref[0]
        for k in range(1, slots.shape[0]):
            acc = acc + s_ref[k]
        o_ref[...] = acc

    return _whole(name, body, [slots], [(slots.shape[1:], F32)])[0]


def _adamw_math(w, g, m, v):
    c1 = 1.0 - ADAM_B1 ** ADAM_STEP
    c2 = 1.0 - ADAM_B2 ** ADAM_STEP
    mn = ADAM_B1 * m + (1.0 - ADAM_B1) * g
    vn = ADAM_B2 * v + (1.0 - ADAM_B2) * (g * g)
    return -ADAM_LR * ((mn / c1) / (jnp.sqrt(vn / c2) + ADAM_EPS) + ADAM_WD * w), mn, vn


def _adamw_many(name, ws, gs, ms, vs):
    k = len(ws)

    def body(*refs):
        for i in range(k):
            dl, mn, vn = _adamw_math(refs[i][...], refs[k + i][...], refs[2 * k + i][...], refs[3 * k + i][...])
            refs[4 * k + i][...] = dl
            refs[5 * k + i][...] = mn
            refs[6 * k + i][...] = vn

    res = _whole(name, body, [*ws, *gs, *ms, *vs], [(t.shape, F32) for t in ws] * 3)
    return res[:k], res[k:2 * k], res[2 * k:]


def _adamw(name, w, g, m, v, tm):
    def body(w_ref, g_ref, m_ref, v_ref, d_ref, mo_ref, vo_ref):
        d_ref[...], mo_ref[...], vo_ref[...] = _adamw_math(w_ref[...], g_ref[...], m_ref[...], v_ref[...])

    cols = w.shape[1]
    return _rowcall(name, body, w.shape[0], tm, [(w, 'row'), (g, 'row'), (m, 'row'), (v, 'row')],
                    [('row', cols, F32), ('row', cols, F32), ('row', cols, F32)])


def _picked_rowcall(name, body, grid, in_specs, out_cols, tm):
    return pl.pallas_call(
        lambda p_ref, *refs: body(*refs), name=name,
        grid_spec=pltpu.PrefetchScalarGridSpec(
            num_scalar_prefetch=1, grid=(grid,), in_specs=in_specs,
            out_specs=[pl.BlockSpec((tm, D), lambda i, p: (i, 0)) for _ in out_cols]),
        out_shape=[jax.ShapeDtypeStruct((grid * tm, D), dt) for dt in out_cols],
        compiler_params=_params(("arbitrary",)),
    )


def _exchange(copies):
    def build(src_refs, out_refs, ss, rs):
        def rdma(k, src, dst, peer):
            return _Lazy(lambda: pltpu.make_async_remote_copy(
                src_ref=src, dst_ref=dst, send_sem=ss.at[k], recv_sem=rs.at[k], device_id=peer, device_id_type=MESH))
        return copies(src_refs, out_refs, rdma)

    def start(*refs):
        for cp in build(*refs):
            cp.start()

    def finish(*refs):
        for cp in build(*refs):
            cp.wait()

    return start, finish


def _reduce_big(tag, slabs):
    rows = [s.shape[1] for s in slabs]
    blk = sum(rows) // 4
    tm = _tile_rows(blk, 512)
    per = blk // tm
    gpack = jnp.concatenate([s.reshape(4, 2, 2, r // 4, D) for s, r in zip(slabs, rows)], axis=3)

    mx, my, mc = lax.axis_index("x"), lax.axis_index("y"), lax.axis_index("c")
    chip_me, chip_xn, chip_yn = 2 * mx + my, 2 * (1 - mx) + my, 2 * mx + (1 - my)

    def pair_copies(srcs, dsts, rdma):
        c, _, (_, _, psib) = _place()
        return [rdma(0, srcs[0].at[:, 1 - c], dsts[0], psib)]

    from_sib = (yield _Comm([gpack], [((4, 2, blk, D), F32)], 1, _exchange(pair_copies), {}, (0.0, 1.0)))[0]

    def sum_both(a_ref, b_ref, o32_ref, o16_ref):
        s = a_ref[...].astype(F32) + b_ref[...].astype(F32)
        o32_ref[...] = s
        o16_ref[...] = s.astype(BF16)

    p32, p16 = _picked_rowcall(
        "sum_pair_" + tag, sum_both, 8 * per,
        [pl.BlockSpec((None, None, None, tm, D), lambda i, p: (i // (2 * per), p[0], (i // per) % 2, i % per, 0)),
         pl.BlockSpec((tm, D), lambda i, p: (i, 0))],
        [F32, BF16], tm)(jnp.stack([mc]).astype(jnp.int32), gpack, from_sib.reshape(8 * blk, D))
    p32, p16 = p32.reshape(4, 2, blk, D), p16.reshape(4, 2, blk, D)

    def step1_copies(srcs, dsts, rdma):
        _, (me, xn, yn, dg), (px, py, _) = _place()
        s16 = srcs[0]
        return [rdma(0, s16.at[yn, 0], dsts[0].at[0], py), rdma(1, s16.at[dg, 0], dsts[0].at[1], py),
                rdma(2, s16.at[xn, 1], dsts[0].at[2], px), rdma(3, s16.at[dg, 1], dsts[0].at[3], px)]

    recv1 = (yield _Comm([p16], [((4, blk, D), BF16)], 4, _exchange(step1_copies), {}, (0.0, 1.0)))[0]

    s32, s16 = _picked_rowcall(
        "sum_step1_" + tag, sum_both, 4 * per,
        [pl.BlockSpec((None, None, tm, D), lambda i, p: (p[i // per], i // (2 * per), i % per, 0)),
         pl.BlockSpec((tm, D), lambda i, p: (i, 0))],
        [F32, BF16], tm)(jnp.stack([chip_me, chip_xn, chip_me, chip_yn]).astype(jnp.int32), p32,
                         recv1.reshape(4 * blk, D))

    def step2_copies(srcs, dsts, rdma):
        _, _, (px, py, _) = _place()
        return [rdma(0, srcs[0].at[1], dsts[0].at[0], px), rdma(1, srcs[0].at[3], dsts[0].at[1], py)]

    recv2 = (yield _Comm([s16.reshape(4, blk, D)], [((2, blk, D), BF16)], 2, _exchange(step2_copies), {},
                         (0.0, 1.0)))[0]

    def sum_step2(a_ref, b_ref, o_ref):
        o_ref[...] = a_ref[...] + b_ref[...].astype(F32)

    red = _rowcall("sum_step2_" + tag, sum_step2, 2 * blk, tm,
                   [(s32.reshape(4, blk, D), pl.BlockSpec((None, tm, D), lambda i: (2 * (i // per), i % per, 0))),
                    (recv2.reshape(2 * blk, D), 'row')],
                   [('row', D, F32)])[0].reshape(2, blk, D)

    def share_copies(srcs, dsts, rdma):
        _, _, (_, _, psib) = _place()
        return [rdma(0, srcs[0], dsts[0], psib)]

    other = (yield _Comm([red], [((2, blk, D), F32)], 1, _exchange(share_copies), {}, (0.0, 1.0)))[0]
    lo = jnp.where(mc == 0, red, other)
    hi = jnp.where(mc == 0, other, red)
    out, off = [], 0
    for r in rows:
        q = r // 4
        out.append(jnp.concatenate([lo[0, off:off + q], lo[1, off:off + q], hi[0, off:off + q], hi[1, off:off + q]]))
        off += q
    return out


def _rows1024(a):
    flat = a.reshape(-1)
    pad = (-flat.shape[0]) % D
    if pad:
        flat = jnp.concatenate([flat, jnp.zeros((pad,), flat.dtype)])
    return flat.reshape(-1, D)


def _pack(arrs, pad_rows_to=8):
    parts = [_rows1024(a) for a in arrs]
    rows = sum(p.shape[0] for p in parts)
    pad = (-rows) % pad_rows_to
    if pad:
        parts.append(jnp.zeros((pad, D), parts[0].dtype))
    return jnp.concatenate(parts, axis=0)


def _unpack(packed, shapes):
    out, r = [], 0
    for s in shapes:
        size = math.prod(s)
        nr = -(-size // D)
        out.append(packed[r:r + nr].reshape(-1)[:size].reshape(s))
        r += nr
    return out


def _block_diag(t):
    eye = jnp.eye(8, dtype=t.dtype)
    j, g, a, b = t.shape
    return (t[:, :, :, None, :] * eye[None, :, None, :, None]).reshape(j, g * a, g * b)


def _block_diag_take(m, a, b):
    eye = jnp.eye(8, dtype=m.dtype)
    return (m.reshape(4, 8, a, 8, b) * eye[None, :, None, :, None]).sum(axis=3)


def kernel(x, mem, g_mix_pre, w_in, gm_ln_g, gm_ln_b, gm_w_s, gm_b_s, s5_lam_re, s5_lam_im, s5_log_step, s5_b_re, s5_b_im, s5_c_re, s5_c_im, s5_d, s5_w_glu, w_br_gm, w_br_s5, w_mix_out, g_mix_post, g_ca_pre, g_mem, ca_w_q, ca_w_kv, ca_w_o, g_ca_post, g_ffn_pre, ffn_w_gu, ffn_w_down, g_ffn_post, loss_target, m_g_mix_pre, m_w_in, m_gm_ln_g, m_gm_ln_b, m_gm_w_s, m_gm_b_s, m_s5_lam_re, m_s5_lam_im, m_s5_log_step, m_s5_b_re, m_s5_b_im, m_s5_c_re, m_s5_c_im, m_s5_d, m_s5_w_glu, m_w_br_gm, m_w_br_s5, m_w_mix_out, m_g_mix_post, m_g_ca_pre, m_g_mem, m_ca_w_q, m_ca_w_kv, m_ca_w_o, m_g_ca_post, m_g_ffn_pre, m_ffn_w_gu, m_ffn_w_down, m_g_ffn_post, v_g_mix_pre, v_w_in, v_gm_ln_g, v_gm_ln_b, v_gm_w_s, v_gm_b_s, v_s5_lam_re, v_s5_lam_im, v_s5_log_step, v_s5_b_re, v_s5_b_im, v_s5_c_re, v_s5_c_im, v_s5_d, v_s5_w_glu, v_w_br_gm, v_w_br_s5, v_w_mix_out, v_g_mix_post, v_g_ca_pre, v_g_mem, v_ca_w_q, v_ca_w_kv, v_ca_w_o, v_g_ca_post, v_g_ffn_pre, v_ffn_w_gu, v_ffn_w_down, v_g_ffn_post):
    a = dict(locals())
    w = {n: a[n][0] for n in WNAMES}
    nb, seq, _ = x.shape
    n = nb * seq
    mlen = mem.shape[1]
    tm = min(256, seq)
    tmb = min(512, seq)
    tpb = seq // tmb
    xf = x.reshape(n, D)
    tgt = loss_target.reshape(n, D)
    memf = mem.reshape(nb * mlen, D)
    tmm = min(256, mlen)

    big_names = list(BIG)
    local_shapes = {k: w[k].shape for k in big_names}
    w_in_full = _run_comm("gather_w_in", _gather_weights("cast_w_in", [w['w_in']], [BIG['w_in']]))[0]
    rest = [k for k in big_names if k != 'w_in']
    gather_rest = _gather_weights("cast_w_rest", [w[k] for k in rest], [BIG[k] for k in rest])

    vec = lambda name: w[name].reshape(1, D)

    to_lane = lambda p: p.reshape(1, -1)
    b_t = lambda p: p.transpose(2, 0, 1).reshape(16, -1)
    lr_l, li_l = to_lane(w['s5_lam_re']), to_lane(w['s5_lam_im'])
    ls_l = jnp.repeat(w['s5_log_step'], 64).reshape(1, -1)
    br_t, bi_t = b_t(w['s5_b_re']), b_t(w['s5_b_im'])
    ab_re, ab_im, bb_re, bb_im = _s5_disc_fwd(lr_l, li_l, ls_l, br_t, bi_t)
    blk = lambda t: _block_diag(t.reshape(16, 4, 8, 64).transpose(1, 2, 0, 3))
    bbc = jnp.concatenate([blk(bb_re), blk(bb_im)], axis=2).astype(BF16)
    cblk = lambda c: _block_diag(c.reshape(4, 8, 16, 64).transpose(0, 1, 3, 2))
    ccm = jnp.concatenate([cblk(w['s5_c_re']), -cblk(w['s5_c_im'])], axis=1).astype(BF16)
    avec = jnp.concatenate([ab_re.reshape(4, 1, 512), ab_im.reshape(4, 1, 512)], axis=2)
    dvec = w['s5_d'].reshape(4, 1, LANE)

    bs3 = w['gm_b_s'].reshape(GM_GROUPS, GM_CHUNK, 1)
    wf = {'w_in': w_in_full}
    h, zgm, zs5, zg = _f_in(xf, vec('g_mix_pre'), wf['w_in'], tmb)
    ygm = _f_gmlp(zgm, vec('gm_ln_g'), vec('gm_ln_b'), w['gm_w_s'], bs3, tmb)
    s5_t = min(S5_T, seq)
    (ypre, states), gathered = _f_s5(zs5, bbc, ccm, avec, dvec, nb, s5_t, gather_rest)
    for k, g in zip(rest, gathered):
        wf[k] = g.reshape(-1, g.shape[-1]) if BIG[k] == 0 else g
    yg, ys5, a_br, b_br, merged, mo, x1 = _f_mix(ygm, ypre, zg, xf, wf['s5_w_glu'], wf['w_br_gm'], wf['w_br_s5'],
                                                 wf['w_mix_out'], vec('g_mix_post'), tmb)
    mem_n, kk, vv = _f_mem(memf, vec('g_mem'), wf['ca_w_kv'], tmm)
    hc, q_att, o_att, ao, x2 = _f_attn(x1, vec('g_ca_pre'), wf['ca_w_q'], kk, vv, wf['ca_w_o'], vec('g_ca_post'),
                                tmb, tpb, mlen)
    hf, gu, act = _f_ffn_up(x2, vec('g_ffn_pre'), wf['ffn_w_gu'], tmb)
    dn, d3, loss_cols = _f_ffn_down(act, x2, tgt, wf['ffn_w_down'], vec('g_ffn_post'), tmb)

    gsm = {}
    gbig = {}
    ddn, dgu, gsm['g_ffn_post'] = _b_ffn_down(d3, dn, gu, vec('g_ffn_post'), wf['ffn_w_down'], tmb)
    dx2, gsm['g_ffn_pre'] = _b_ffn_up(dgu, d3, x2, wf['ffn_w_gu'], vec('g_ffn_pre'), tmb)
    gbig['ffn_w_down'] = _mm_tn("dw_ffn_down", act, ddn)
    gbig['ffn_w_gu'] = _mm_tn("dw_ffn_gu", dgu, hf)
    dao, dq, dx1, dk, dv, gsm['g_ca_post'], gsm['g_ca_pre'] = _b_attn(
        dx2, ao, x1, q_att, vec('g_ca_post'), wf['ca_w_o'], wf['ca_w_q'], vec('g_ca_pre'), kk, vv, tmb, tpb, mlen)
    gbig['ca_w_o'] = _mm_tn("dw_ca_o", o_att, dao)
    gbig['ca_w_q'] = _mm_tn("dw_ca_q", hc, dq)
    dkv, gsm['g_mem'] = _b_mem(dk, dv, memf, wf['ca_w_kv'], vec('g_mem'), tmm)
    gbig['ca_w_kv'] = _mm_tn("dw_ca_kv", dkv, mem_n)
    dmo, da_br, db_br, dzg, dygm, dgate, dypre, gsm['g_mix_post'] = _b_mix(
        dx1, mo, a_br, b_br, zg, ypre, vec('g_mix_post'), wf['w_mix_out'], wf['w_br_gm'], wf['w_br_s5'],
        wf['s5_w_glu'], tmb)
    gbig['w_mix_out'] = _mm_tn("dw_mix_out", merged, dmo)
    gbig['w_br_gm'] = _mm_tn("dw_br_gm", ygm, da_br)
    gbig['w_br_s5'] = _mm_tn("dw_br_s5", db_br, ys5)
    gbig['s5_w_glu'] = _mm_tn("dw_s5_glu", yg, dgate)
    slab_of = lambda k: gbig[k].reshape(4, -1, D)
    red_rest = _reduce_big("rest", [slab_of(k) for k in rest])
    s5_tb = max(s5_t, min(S5_TB, seq))
    (dzs5, dbbc, dccm_t, davec, ddvec), got = _b_s5(zs5, dypre, states[:, :, ::s5_tb // s5_t], bbc, ccm, avec, dvec,
                                                    nb, s5_tb, next(red_rest))
    dccm = dccm_t.transpose(0, 2, 1)
    (dzgm, gsm['gm_w_s'], dbs3, gsm['gm_ln_g'], gsm['gm_ln_b']), got = _b_gmlp(
        zgm, dygm, vec('gm_ln_g'), vec('gm_ln_b'), w['gm_w_s'], bs3, tmb, red_rest.send(got))
    gsm['gm_b_s'] = dbs3
    grad_x, gsm['g_mix_pre'] = _b_in(dzgm, dzs5, dzg, dx1, xf, wf['w_in'], vec('g_mix_pre'), tmb, None)
    dw_in_gm, got = _mm_tn("dw_in_gm", dzgm, h, comm=red_rest.send(got))
    dw_in_s5, got = _mm_tn("dw_in_s5", dzs5, h, comm=red_rest.send(got))
    greds = dict(zip(rest, _finish(red_rest, got)))

    unblk = lambda m_: _block_diag_take(m_, 16, 64).transpose(2, 0, 1, 3).reshape(16, -1)
    d_bb_re, d_bb_im = unblk(dbbc[:, :, :512]), unblk(dbbc[:, :, 512:])
    cunblk = lambda m_: _block_diag_take(m_, 64, 16).transpose(0, 1, 3, 2).reshape(32, 16, 64)
    gsm['s5_c_re'] = cunblk(dccm[:, :512, :])
    gsm['s5_c_im'] = -cunblk(dccm[:, 512:, :])
    d_ab_re, d_ab_im = davec[:, :, :512].reshape(1, -1), davec[:, :, 512:].reshape(1, -1)
    g_lr, g_li, g_ls, g_br, g_bi = _s5_disc_bwd(lr_l, li_l, ls_l, br_t, bi_t, (d_ab_re, d_ab_im, d_bb_re, d_bb_im))
    gsm['s5_lam_re'], gsm['s5_lam_im'] = g_lr.reshape(32, 64), g_li.reshape(32, 64)
    gsm['s5_log_step'] = g_ls.reshape(32, 64).sum(axis=1)
    from_t = lambda t: t.reshape(16, 32, 64).transpose(1, 2, 0)
    gsm['s5_b_re'], gsm['s5_b_im'] = from_t(g_br), from_t(g_bi)
    gsm['s5_d'] = ddvec.reshape(32, 16)

    small_shapes = [w[k].shape for k in SMALL]
    spack = _pack([gsm[k].reshape(w[k].shape) for k in SMALL] + [loss_cols], 8)
    me_slot = 4 * lax.axis_index("x") + 2 * lax.axis_index("y") + lax.axis_index("c")
    dw_in_g, (sall,) = _mm_tn("dw_in_g", dzg, h, comm=_gather_all(spack))
    gbig['w_in'] = jnp.concatenate([dw_in_gm, dw_in_s5, dw_in_g], axis=0)
    sall = lax.dynamic_update_index_in_dim(sall, spack, me_slot, 0)
    ssum = _sum_slots("sum_small", sall)
    small_red = _unpack(ssum, small_shapes + [(1, D)])
    loss = 0.5 * jnp.sum(small_red[-1]) / D

    red_in = _reduce_big("w_in", [slab_of('w_in')])
    got = _run_comm("reduce_pair_w_in", next(red_in))
    for phase in ("reduce_step1_w_in", "reduce_step2_w_in"):
        got = _run_comm(phase, red_in.send(got))
    got = _run_comm("share_pair_w_in", red_in.send(got))
    greds['w_in'] = _finish(red_in, got)[0]

    res_big = {}
    for k in big_names:
        r_, c_ = local_shapes[k]
        g = greds[k].reshape(r_, c_) if BIG[k] == 0 else greds[k].reshape(c_, r_).T
        dl, mn, vn = _adamw("adamw_" + k, w[k], g, a['m_' + k][0], a['v_' + k][0], _tile_rows(g.shape[0], 256))
        res_big[k] = (g, dl, mn, vn)
    as2d = lambda t: t.reshape(1, -1) if t.ndim == 1 else t
    dl_s, mn_s, vn_s = _adamw_many("adamw_small", [as2d(w[k]) for k in SMALL], [as2d(g) for g in small_red[:-1]],
                                   [as2d(a['m_' + k][0]) for k in SMALL], [as2d(a['v_' + k][0]) for k in SMALL])
    res_small = {k: (g, dl.reshape(w[k].shape), mn.reshape(w[k].shape), vn.reshape(w[k].shape))
                 for k, g, dl, mn, vn in zip(SMALL, small_red[:-1], dl_s, mn_s, vn_s)}

    res = {**res_big, **res_small}
    outs = [loss, grad_x.reshape(nb, seq, D)]
    for i in range(4):
        outs += [res[k][i][None] for k in WNAMES]
    return tuple(outs)


def _finish(gen, got):
    try:
        gen.send(got)
    except StopIteration as done:
        return done.value
    raise AssertionError("the generator has more exchanges")


def _tile_rows(rows, cap=384):
    best = rows
    for t in range(16, min(rows, cap) + 1, 16):
        if rows % t == 0:
            best = t
    return best
```

```python
import math
from typing import Any, NamedTuple

import jax
import jax.numpy as jnp
from jax import lax
from jax.experimental import pallas as pl
from jax.experimental.pallas import tpu as pltpu

F32 = jnp.float32
BF16 = jnp.bfloat16
EPS = 1e-6
D = 1024
GM_CHUNK = 128
GM_GROUPS = 8
S5_W = 512
S5_SUPER = 4
S5_T = 512
S5_TB = 1024
HEADS = 4
HEAD_DIM = 256
FFN_H = 2816
LANE = 128
VMEM_LIMIT = 56 * 1024 * 1024
MESH = pl.DeviceIdType.MESH

ADAM_LR, ADAM_B1, ADAM_B2, ADAM_EPS, ADAM_WD, ADAM_STEP = 0.001, 0.9, 0.999, 1e-08, 0.01, 10

WNAMES = ['g_mix_pre', 'w_in', 'gm_ln_g', 'gm_ln_b', 'gm_w_s', 'gm_b_s', 's5_lam_re', 's5_lam_im', 's5_log_step',
          's5_b_re', 's5_b_im', 's5_c_re', 's5_c_im', 's5_d', 's5_w_glu', 'w_br_gm', 'w_br_s5', 'w_mix_out',
          'g_mix_post', 'g_ca_pre', 'g_mem', 'ca_w_q', 'ca_w_kv', 'ca_w_o', 'g_ca_post', 'g_ffn_pre', 'ffn_w_gu',
          'ffn_w_down', 'g_ffn_post']
BIG = {'w_in': 1, 's5_w_glu': 0, 'w_br_gm': 0, 'w_br_s5': 1, 'w_mix_out': 0, 'ca_w_q': 0, 'ca_w_kv': 1,
       'ca_w_o': 0, 'ffn_w_gu': 1, 'ffn_w_down': 0}
SMALL = [n for n in WNAMES if n not in BIG]


def _dot(a, b):
    return jnp.dot(a, b, preferred_element_type=F32)


def _dot_nt(a, b):
    return lax.dot_general(a, b, (((1,), (1,)), ((), ())), preferred_element_type=F32)


def _dot_tn(a, b):
    return lax.dot_general(a, b, (((0,), (0,)), ((), ())), preferred_element_type=F32)


def _rms_fwd(x, g):
    r = lax.rsqrt(jnp.mean(x * x, axis=-1, keepdims=True) + EPS)
    return x * r * g


def _rms_bwd(x, g, dy):
    r = lax.rsqrt(jnp.mean(x * x, axis=-1, keepdims=True) + EPS)
    xh = x * r
    gdy = dy * g
    dx = r * (gdy - xh * jnp.mean(gdy * xh, axis=-1, keepdims=True))
    dg = jnp.sum(dy * xh, axis=0, keepdims=True)
    return dx, dg


_GC = math.sqrt(2.0 / math.pi)


def _gelu(x):
    return 0.5 * x * (1.0 + jnp.tanh(_GC * (x + 0.044715 * x * x * x)))


def _gelu_grad(x):
    t = jnp.tanh(_GC * (x + 0.044715 * x * x * x))
    return 0.5 * (1.0 + t) + 0.5 * x * (1.0 - t * t) * _GC * (1.0 + 3 * 0.044715 * x * x)


def _sig(x):
    return 1.0 / (1.0 + jnp.exp(-x))


def _cscan(br, bi, ar, ai, reverse):
    t = br.shape[0]
    row = lax.broadcasted_iota(jnp.int32, br.shape, 0)
    pr, pi = ar, ai
    sh = 1
    while sh < t:
        if reverse:
            keep = row < t - sh
            rr, ri = pltpu.roll(br, t - sh, 0), pltpu.roll(bi, t - sh, 0)
        else:
            keep = row >= sh
            rr, ri = pltpu.roll(br, sh, 0), pltpu.roll(bi, sh, 0)
        rr = jnp.where(keep, rr, 0.0)
        ri = jnp.where(keep, ri, 0.0)
        br, bi = br + pr * rr - pi * ri, bi + pr * ri + pi * rr
        pr, pi = pr * pr - pi * pi, 2.0 * pr * pi
        sh *= 2
    return br, bi


def _s5_disc(lr, li, ls, br, bi):
    step = jnp.exp(ls)
    mag = jnp.exp(lr * step)
    ab_re = mag * jnp.cos(li * step)
    ab_im = mag * jnp.sin(li * step)
    den = lr * lr + li * li
    nr = ab_re - 1.0
    co_re = (nr * lr + ab_im * li) / den
    co_im = (ab_im * lr - nr * li) / den
    return ab_re, ab_im, co_re * br - co_im * bi, co_re * bi + co_im * br


def _params(sem=None):
    return pltpu.CompilerParams(dimension_semantics=sem, vmem_limit_bytes=VMEM_LIMIT)


def _rowcall(name, body, n_rows, tm, ins, outs, scratch=(), comm=None):
    def spec(kind, shape):
        if kind == 'row':
            return pl.BlockSpec((tm, shape[1]), lambda i: (i, 0))
        if kind == 'full':
            nd = len(shape)
            return pl.BlockSpec(tuple(shape), lambda i: (0,) * nd, pipeline_mode=pl.Buffered(1))
        if kind == 'acc':
            nd = len(shape)
            return pl.BlockSpec(tuple(shape), lambda i: (0,) * nd)
        return kind

    in_specs = [spec(k, a.shape) for a, k in ins]
    out_shape, out_specs = [], []
    for k, s, dt in outs:
        shape = (n_rows, s) if k == 'row' else tuple(s)
        out_shape.append(jax.ShapeDtypeStruct(shape, dt))
        out_specs.append(spec(k, shape))
    args = [a for a, _ in ins]
    if comm is not None:
        return _hosting_call(body, name, (n_rows // tm,), in_specs, out_specs, out_shape, list(scratch),
                             ("arbitrary",), args, comm)
    return pl.pallas_call(
        body, name=name, grid=(n_rows // tm,), in_specs=in_specs, out_specs=out_specs, out_shape=out_shape,
        scratch_shapes=list(scratch), compiler_params=_params(("arbitrary",)),
    )(*args)


def _tile(n, cap):
    if n <= cap:
        return n
    best = LANE
    for k in range(1, n // LANE + 1):
        t = k * LANE
        if n % t == 0 and t <= cap:
            best = t
    return best


def _mm_tn(name, a, b, comm=None):
    n, k = a.shape
    m = b.shape[1]
    tk, tn, tr = _tile(k, 1536), _tile(m, 1536), min(n, 1024 if comm is not None else 2048)

    def body(a_ref, b_ref, o_ref):
        @pl.when(pl.program_id(2) == 0)
        def _():
            o_ref[...] = jnp.zeros_like(o_ref)
        o_ref[...] += _dot_tn(a_ref[...].astype(BF16), b_ref[...].astype(BF16))

    out_spec = pl.BlockSpec((tk, tn), lambda i, j, r: (i, j))
    out_shape = (k, m)
    grid = (k // tk, m // tn, n // tr)
    in_specs = [pl.BlockSpec((tr, tk), lambda i, j, r: (r, i)), pl.BlockSpec((tr, tn), lambda i, j, r: (r, j))]
    if comm is not None:
        (res,), extra = _hosting_call(body, name, grid, in_specs, [out_spec], [jax.ShapeDtypeStruct(out_shape, F32)],
                                      [], ("arbitrary", "arbitrary", "arbitrary"), (a, b), comm)
        return res, extra
    return pl.pallas_call(
        body, name=name, grid=grid, in_specs=in_specs, out_specs=out_spec,
        out_shape=jax.ShapeDtypeStruct(out_shape, F32),
        compiler_params=_params(("parallel", "parallel", "arbitrary")),
    )(a, b)


def _comm_call(name, srcs, outs, nsem, body_fn, aliases=None):
    ns, no = len(srcs), len(outs)

    def body(*refs):
        body_fn(refs[:ns], refs[ns:ns + no], refs[ns + no], refs[ns + no + 1])

    hbm = pl.BlockSpec(memory_space=pltpu.HBM)
    return pl.pallas_call(
        body, name=name, in_specs=[hbm] * ns, out_specs=[hbm] * no,
        out_shape=[jax.ShapeDtypeStruct(s, d) for s, d in outs],
        scratch_shapes=[pltpu.SemaphoreType.DMA((nsem,)), pltpu.SemaphoreType.DMA((nsem,))],
        input_output_aliases=aliases or {},
    )(*srcs)


def _place():
    x, y, c = lax.axis_index("x"), lax.axis_index("y"), lax.axis_index("c")
    chips = (2 * x + y, 2 * (1 - x) + y, 2 * x + (1 - y), 2 * (1 - x) + (1 - y))
    peers = ((1 - x, y, c), (x, 1 - y, c), (x, y, 1 - c))
    return c, chips, peers


class _Comm(NamedTuple):
    srcs: Any
    outs: Any
    nsem: int
    stages: Any
    aliases: Any
    fracs: Any


class _Lazy:
    def __init__(self, make):
        self._make = make

    def start(self):
        self._make().start()

    def wait(self):
        self._make().wait()

    def wait_recv(self):
        self._make().wait_recv()

    def wait_send(self):
        self._make().wait_send()


def _run_comm(name, comm):
    def body_fn(srcs, dsts, ss, rs):
        for stage in comm.stages:
            stage(srcs, dsts, ss, rs)

    return _comm_call(name, comm.srcs, comm.outs, comm.nsem, body_fn, aliases=comm.aliases)


def _hosting_call(body, name, grid, in_specs, out_specs, out_shape, scratch, sem, args, comm):
    hbm = pl.BlockSpec(memory_space=pltpu.HBM)
    n_in, n_out, n_sc = len(in_specs), len(out_specs), len(scratch)
    ns, no = len(comm.srcs), len(comm.outs)
    total = math.prod(grid)
    steps = [min(total - 1, int(f * total)) for f in comm.fracs]

    def wrapped(*refs):
        ins, csrc = refs[:n_in], refs[n_in:n_in + ns]
        outs, cdst = refs[n_in + ns:n_in + ns + n_out], refs[n_in + ns + n_out:n_in + ns + n_out + no]
        rest = refs[n_in + ns + n_out + no:]
        lin = 0
        for d, size in enumerate(grid):
            lin = lin * size + pl.program_id(d)
        for stage, at in zip(comm.stages, steps):
            @pl.when(lin == at)
            def _(stage=stage):
                stage(csrc, cdst, rest[n_sc], rest[n_sc + 1])
        body(*ins, *outs, *rest[:n_sc])

    res = pl.pallas_call(
        wrapped, name=name, grid=grid, in_specs=list(in_specs) + [hbm] * ns, out_specs=list(out_specs) + [hbm] * no,
        out_shape=list(out_shape) + [jax.ShapeDtypeStruct(s, d) for s, d in comm.outs],
        scratch_shapes=list(scratch) + [pltpu.SemaphoreType.DMA((comm.nsem,)), pltpu.SemaphoreType.DMA((comm.nsem,))],
        input_output_aliases={n_in + i: n_out + o for i, o in comm.aliases.items()},
        compiler_params=_params(sem),
    )(*args, *comm.srcs)
    return res[:n_out], res[n_out:]


def _gather_weights(name, shards, axes):
    nw = len(shards)
    shapes = [s.shape for s in shards]
    outs = [((4,) + s if ax == 0 else (s[0], 4 * s[1]), BF16) for s, ax in zip(shapes, axes)]

    def win(ref, i, chip, start, rows):
        r, cols = shapes[i]
        if axes[i] == 0:
            return ref.at[chip, pl.ds(start, rows), :]
        return ref.at[pl.ds(start, rows), pl.ds(chip * cols, cols)]

    def place_body(*refs):
        srcs, dsts, bufs, sem = refs[:nw], refs[nw:2 * nw], refs[2 * nw:3 * nw], refs[3 * nw]
        me = 2 * lax.axis_index("x") + lax.axis_index("y")
        cps = []
        for i in range(nw):
            bufs[i][...] = srcs[i][...].astype(BF16)
            cps.append(pltpu.make_async_copy(bufs[i], win(dsts[i], i, me, 0, shapes[i][0]), sem.at[i]))
            cps[-1].start()
        for cp in cps:
            cp.wait()

    placed = pl.pallas_call(
        place_body, name=name, out_shape=[jax.ShapeDtypeStruct(s, d) for s, d in outs],
        in_specs=[pl.BlockSpec(memory_space=pltpu.VMEM)] * nw, out_specs=[pl.BlockSpec(memory_space=pltpu.HBM)] * nw,
        scratch_shapes=[pltpu.VMEM(s, BF16) for s in shapes] + [pltpu.SemaphoreType.DMA((nw,))],
        compiler_params=_params(),
    )(*shards)

    def copies(dsts, ss, rs):
        c, (me, xn, yn, dg), (px, py, psib) = _place()

        def rdma(k, window, peer):
            return _Lazy(lambda: pltpu.make_async_remote_copy(
                src_ref=window, dst_ref=window, send_sem=ss.at[k], recv_sem=rs.at[k], device_id=peer,
                device_id_type=MESH))

        first, later, swaps = [], [], []
        for i in range(nw):
            qr, hr, k0 = shapes[i][0] // 4, shapes[i][0] // 2, 6 * i
            q0, q1 = 2 * c * qr, (2 * c + 1) * qr
            w0, w1 = win(dsts[i], i, me, q0, qr), win(dsts[i], i, me, q1, qr)
            first.append([rdma(k0, w0, px), rdma(k0 + 1, w1, py), rdma(k0 + 2, w0, py), rdma(k0 + 3, w1, px)])
            later.append([rdma(k0 + 4, win(dsts[i], i, xn, q0, qr), py), rdma(k0 + 5, win(dsts[i], i, yn, q1, qr), px)])
            swaps.append([rdma(6 * nw + 3 * i + j, win(dsts[i], i, chip, c * hr, hr), psib)
                          for j, chip in enumerate((xn, yn, dg))])
        return first, later, swaps

    def send(srcs, dsts, ss, rs):
        for cps in copies(dsts, ss, rs)[0]:
            for cp in cps:
                cp.start()

    def forward(srcs, dsts, ss, rs):
        first, later, _ = copies(dsts, ss, rs)
        for i in range(nw):
            first[i][0].wait_recv()
            later[i][0].start()
            first[i][1].wait_recv()
            later[i][1].start()

    def swap(srcs, dsts, ss, rs):
        first, later, swaps = copies(dsts, ss, rs)
        for i in range(nw):
            for cp in first[i][2:] + later[i]:
                cp.wait_recv()
            for sw in swaps[i]:
                sw.start()

    def finish(srcs, dsts, ss, rs):
        first, later, swaps = copies(dsts, ss, rs)
        for i in range(nw):
            for sw in swaps[i]:
                sw.wait()
            for cp in first[i] + later[i]:
                cp.wait_send()

    return _Comm(placed, outs, 9 * nw, [send, forward, swap, finish], {i: i for i in range(nw)},
                 (0.0, 0.55, 0.85, 1.0))


def _gather_all(src):
    def copies(srcs, dsts, ss, rs):
        x, y, c = lax.axis_index("x"), lax.axis_index("y"), lax.axis_index("c")
        me, sib = (x, y, c), (x, y, 1 - c)
        chips = [(1 - x, y), (x, 1 - y), (1 - x, 1 - y)]

        def rows(px, py, pc):
            return dsts[0].at[4 * px + 2 * py + pc]

        def copy(k, block, to, src=None):
            return _Lazy(lambda: pltpu.make_async_remote_copy(
                src_ref=rows(*block) if src is None else src, dst_ref=rows(*block), send_sem=ss.at[k],
                recv_sem=rs.at[k], device_id=to, device_id_type=MESH))

        first = [copy(0, me, sib, src=srcs[0])] + [copy(1 + j, me, (*chip, c), src=srcs[0])
                                                   for j, chip in enumerate(chips)]
        passed = [copy(4 + j, (*chip, c), sib) for j, chip in enumerate(chips)]
        landed = [copy(0, sib, me)] + [copy(1 + j, (*chip, c), me) for j, chip in enumerate(chips)]
        landed += [copy(4 + j, (*chip, 1 - c), me) for j, chip in enumerate(chips)]
        return first, passed, landed

    def send(*refs):
        for cp in copies(*refs)[0]:
            cp.start()

    def forward(*refs):
        _, passed, landed = copies(*refs)
        for j in range(3):
            landed[1 + j].wait_recv()
            passed[j].start()

    def finish(*refs):
        first, passed, landed = copies(*refs)
        landed[0].wait_recv()
        for cp in landed[4:]:
            cp.wait_recv()
        for cp in first + passed:
            cp.wait_send()

    return _Comm([src], [((8,) + src.shape, src.dtype)], 7, [send, forward, finish], {}, (0.0, 0.9, 1.0))


def _f_in(x, g, w_in, tm):
    n = x.shape[0]

    def body(x_ref, g_ref, w_ref, h_ref, zgm_ref, zs5_ref, zg_ref):
        h = _rms_fwd(x_ref[...], g_ref[...]).astype(BF16)
        h_ref[...] = h
        zgm_ref[...] = _dot(h, w_ref[:, 0:2 * D])
        zs5_ref[...] = _dot(h, w_ref[:, 2 * D:2 * D + S5_W])
        zg_ref[...] = _dot(h, w_ref[:, 2 * D + S5_W:])

    return _rowcall("f_in", body, n, tm, [(x, 'row'), (g, 'full'), (w_in, 'full')],
                    [('row', D, BF16), ('row', 2 * D, F32), ('row', S5_W, F32), ('row', 2 * D, F32)])


def _tril():
    r = lax.broadcasted_iota(jnp.int32, (GM_CHUNK, GM_CHUNK), 0)
    c = lax.broadcasted_iota(jnp.int32, (GM_CHUNK, GM_CHUNK), 1)
    return r >= c


def _ln_stats(v):
    mu = jnp.mean(v, axis=-1, keepdims=True)
    vc = v - mu
    r = lax.rsqrt(jnp.mean(vc * vc, axis=-1, keepdims=True) + EPS)
    return vc * r, r


def _f_gmlp(zgm, ln_g, ln_b, w_s, b_s, tm):
    n = zgm.shape[0]

    def body(z_ref, lg_ref, lb_ref, ws_ref, bs_ref, y_ref):
        zg = _gelu(z_ref[...].astype(BF16))
        u = zg[:, :D]
        vh, _ = _ln_stats(zg[:, D:].astype(F32))
        vn = (vh * lg_ref[...] + lb_ref[...]).astype(BF16)
        keep = _tril()
        for g in range(GM_GROUPS):
            w = jnp.where(keep, ws_ref[g], 0.0).astype(BF16)
            cs = slice(g * LANE, (g + 1) * LANE)
            for c in range(tm // GM_CHUNK):
                rs = slice(c * GM_CHUNK, (c + 1) * GM_CHUNK)
                sv = _dot(w, vn[rs, cs]) + bs_ref[g]
                y_ref[rs, cs] = u[rs, cs] * sv.astype(BF16)

    return _rowcall("f_gmlp", body, n, tm,
                    [(zgm, 'row'), (ln_g, 'full'), (ln_b, 'full'), (w_s, 'full'), (b_s, 'full')],
                    [('row', D, BF16)])[0]


def _s5_specs(nb, nc, t, rev):
    def cc(c):
        return nc - 1 - c if rev else c
    slab = pl.BlockSpec((t, LANE), lambda j, b, c: (b * nc + cc(c), j))
    bb = pl.BlockSpec((None, LANE, 1024), lambda j, b, c: (j, 0, 0))
    cm = pl.BlockSpec((None, 1024, LANE), lambda j, b, c: (j, 0, 0))
    av = pl.BlockSpec((None, 1, 1024), lambda j, b, c: (j, 0, 0))
    dv = pl.BlockSpec((None, 1, LANE), lambda j, b, c: (j, 0, 0))
    st = pl.BlockSpec((None, None, None, 1, 1024), lambda j, b, c: (j, b, cc(c), 0, 0))
    return slab, bb, cm, av, dv, st


def _seg_load(ref, seg):
    return jnp.concatenate([ref[pl.ds(k, 8, stride=seg), :] for k in range(seg)], axis=0)


def _seg_store(ref, val, seg):
    for k in range(seg):
        ref[pl.ds(k, 8, stride=seg), :] = val[8 * k:8 * k + 8, :]


def _seg_scan(s_ref, ar, ai, seg, reverse):
    def step(i, carry):
        k = seg - 1 - i if reverse else i
        sr, si = carry
        nr = ar * sr - ai * si + s_ref[k, :, :512]
        ni = ar * si + ai * sr + s_ref[k, :, 512:]
        s_ref[k, :, :512] = nr
        s_ref[k, :, 512:] = ni
        return nr, ni

    zero = jnp.zeros((8, 512), F32)
    return lax.fori_loop(0, seg, step, (zero, zero), unroll=4)


def _seg_entries(tr, ti, pr, pi, cin, reverse):
    row = lax.broadcasted_iota(jnp.int32, (8, 512), 0)
    edge = row == (7 if reverse else 0)
    cr, ci = cin[:, :512], cin[:, 512:]
    xr = tr + jnp.where(edge, pr * cr - pi * ci, 0.0)
    xi = ti + jnp.where(edge, pr * ci + pi * cr, 0.0)
    ir, ii = _cscan(xr, xi, pr, pi, reverse)
    shift = 7 if reverse else 1
    er = jnp.where(edge, cr, pltpu.roll(ir, shift, 0))
    ei = jnp.where(edge, ci, pltpu.roll(ii, shift, 0))
    far = row == (0 if reverse else 7)
    out = jnp.concatenate([jnp.sum(jnp.where(far, ir, 0.0), axis=0, keepdims=True),
                           jnp.sum(jnp.where(far, ii, 0.0), axis=0, keepdims=True)], axis=1)
    return er, ei, out


def _seg_power(ar, ai, seg):
    pr, pi = ar, ai
    for _ in range(seg.bit_length() - 1):
        pr, pi = pr * pr - pi * pi, 2.0 * pr * pi
    return pr, pi


def _f_s5(zs5, bbc, ccm, avec, dvec, nb, t, comm):
    n = zs5.shape[0]
    nc = n // nb // t
    seg = t // 8
    slab, bb, cm, av, dv, st = _s5_specs(nb, nc, t, False)

    def body(u_ref, bb_ref, cc_ref, a_ref, d_ref, y_ref, st_ref, carry, s_ref):
        @pl.when(pl.program_id(2) == 0)
        def _():
            carry[...] = jnp.zeros_like(carry)
        cin = carry[...]
        st_ref[...] = cin
        up = _seg_load(u_ref, seg)
        s_ref[...] = _dot(up.astype(BF16), bb_ref[...]).reshape(seg, 8, 1024)
        a1r, a1i = a_ref[:, :512], a_ref[:, 512:]
        ar, ai = jnp.broadcast_to(a1r, (8, 512)), jnp.broadcast_to(a1i, (8, 512))
        tr, ti = _seg_scan(s_ref, ar, ai, seg, False)
        er, ei, cout = _seg_entries(tr, ti, *_seg_power(a1r, a1i, seg), cin, False)
        carry[...] = cout

        def apply(k, pe):
            pr, pi = pe
            pr, pi = ar * pr - ai * pi, ar * pi + ai * pr
            s_ref[k, :, :512] = s_ref[k, :, :512] + pr
            s_ref[k, :, 512:] = s_ref[k, :, 512:] + pi
            return pr, pi

        lax.fori_loop(0, seg, apply, (er, ei), unroll=4)
        s = s_ref[...].reshape(t, 1024).astype(BF16)
        _seg_store(y_ref, _dot(s, cc_ref[...]) + d_ref[...] * up, seg)

    return _hosting_call(
        body, "f_s5", (S5_SUPER, nb, nc), [slab, bb, cm, av, dv], [slab, st],
        [jax.ShapeDtypeStruct((n, S5_W), F32), jax.ShapeDtypeStruct((S5_SUPER, nb, nc, 1, 1024), F32)],
        [pltpu.VMEM((1, 1024), F32), pltpu.VMEM((seg, 8, 1024), F32)], ("arbitrary", "arbitrary", "arbitrary"),
        (zs5, bbc, ccm, avec, dvec), comm)


def _f_mix(ygm, ypre, zg, x, w_glu, w_br_gm, w_br_s5, w_mix_out, g_post, tm):
    n = x.shape[0]

    def body(ygm_ref, ypre_ref, zg_ref, x_ref, wglu_ref, wgm_ref, ws5_ref, wout_ref, g_ref,
             yg_ref, ys5_ref, a_ref, b_ref, mg_ref, mo_ref, x1_ref):
        yg = _gelu(ypre_ref[...])
        ygb = yg.astype(BF16)
        yg_ref[...] = ygb
        ys5 = (yg * _sig(_dot(ygb, wglu_ref[...]))).astype(BF16)
        ys5_ref[...] = ys5
        a = _dot(ygm_ref[...], wgm_ref[...])
        b = _dot(ys5, ws5_ref[...])
        a_ref[...] = a.astype(BF16)
        b_ref[...] = b.astype(BF16)
        zg = zg_ref[...]
        merged = (_sig(zg[:, :D]) * a + _sig(zg[:, D:]) * b).astype(BF16)
        mg_ref[...] = merged
        mo = _dot(merged, wout_ref[...])
        mo_ref[...] = mo
        x1_ref[...] = x_ref[...] + _rms_fwd(mo, g_ref[...])

    return _rowcall("f_mix", body, n, tm,
                    [(ygm, 'row'), (ypre, 'row'), (zg, 'row'), (x, 'row'), (w_glu, 'full'), (w_br_gm, 'full'),
                     (w_br_s5, 'full'), (w_mix_out, 'full'), (g_post, 'full')],
                    [('row', S5_W, BF16), ('row', S5_W, BF16), ('row', D, BF16), ('row', D, BF16),
                     ('row', D, BF16), ('row', D, F32), ('row', D, F32)])


def _f_mem(mem, g_mem, w_kv, tm):
    n = mem.shape[0]

    def body(m_ref, g_ref, w_ref, mn_ref, k_ref, v_ref):
        mn = _rms_fwd(m_ref[...], g_ref[...]).astype(BF16)
        mn_ref[...] = mn
        k_ref[...] = _dot(mn, w_ref[:, :D]).astype(BF16)
        v_ref[...] = _dot(mn, w_ref[:, D:]).astype(BF16)

    return _rowcall("f_mem", body, n, tm, [(mem, 'row'), (g_mem, 'full'), (w_kv, 'full')],
                    [('row', D, BF16), ('row', D, BF16), ('row', D, BF16)])


def _softmax(s):
    m = jnp.max(s, axis=-1, keepdims=True)
    e = jnp.exp(s - m)
    return e / jnp.sum(e, axis=-1, keepdims=True)


def _f_attn(x1, g_pre, w_q, k, v, w_o, g_post, tm, tpb, mlen):
    n = x1.shape[0]
    kv_spec = pl.BlockSpec((mlen, D), lambda i: (i // tpb, 0))
    scale = HEAD_DIM ** -0.5

    def body(x_ref, gp_ref, wq_ref, k_ref, v_ref, wo_ref, go_ref, hc_ref, q_ref, o_ref, ao_ref, x2_ref):
        x1v = x_ref[...]
        hc = _rms_fwd(x1v, gp_ref[...]).astype(BF16)
        hc_ref[...] = hc
        q_ref[...] = _dot(hc, wq_ref[...]).astype(BF16)
        for h in range(HEADS):
            hs = slice(h * HEAD_DIM, (h + 1) * HEAD_DIM)
            p = _softmax(_dot_nt(q_ref[:, hs], k_ref[:, hs]) * scale)
            o_ref[:, hs] = _dot(p.astype(BF16), v_ref[:, hs]).astype(BF16)
        ao = _dot(o_ref[...], wo_ref[...])
        ao_ref[...] = ao
        x2_ref[...] = x1v + _rms_fwd(ao, go_ref[...])

    return _rowcall("f_attn", body, n, tm,
                    [(x1, 'row'), (g_pre, 'full'), (w_q, 'full'), (k, kv_spec), (v, kv_spec), (w_o, 'full'),
                     (g_post, 'full')],
                    [('row', D, BF16), ('row', D, BF16), ('row', D, BF16), ('row', D, F32), ('row', D, F32)])


def _f_ffn_up(x2, g_pre, w_gu, tm):
    n = x2.shape[0]

    def body(x_ref, g_ref, w_ref, hf_ref, gu_ref, act_ref):
        hf = _rms_fwd(x_ref[...], g_ref[...]).astype(BF16)
        hf_ref[...] = hf
        gg = _dot(hf, w_ref[:, :FFN_H])
        uu = _dot(hf, w_ref[:, FFN_H:])
        gu_ref[:, :FFN_H] = gg.astype(BF16)
        gu_ref[:, FFN_H:] = uu.astype(BF16)
        act_ref[...] = (gg * _sig(gg) * uu).astype(BF16)

    return _rowcall("f_ffn_up", body, n, tm, [(x2, 'row'), (g_pre, 'full'), (w_gu, 'full')],
                    [('row', D, BF16), ('row', 2 * FFN_H, BF16), ('row', FFN_H, BF16)])


def _f_ffn_down(act, x2, tgt, w_down, g_post, tm):
    n = x2.shape[0]

    def body(a_ref, x_ref, t_ref, w_ref, g_ref, dn_ref, d3_ref, loss_ref):
        @pl.when(pl.program_id(0) == 0)
        def _():
            loss_ref[...] = jnp.zeros_like(loss_ref)
        dn = _dot(a_ref[...], w_ref[...])
        dn_ref[...] = dn
        err = x_ref[...] + _rms_fwd(dn, g_ref[...]) - t_ref[...]
        d3_ref[...] = err * (1.0 / D)
        loss_ref[...] += jnp.sum(err * err, axis=0, keepdims=True)

    return _rowcall("f_ffn_down", body, n, tm,
                    [(act, 'row'), (x2, 'row'), (tgt, 'row'), (w_down, 'full'), (g_post, 'full')],
                    [('row', D, F32), ('row', D, F32), ('acc', (1, D), F32)])


def _b_ffn_down(d3, dn, gu, g_post, w_down, tm):
    n = d3.shape[0]

    def body(d3_ref, dn_ref, gu_ref, g_ref, w_ref, ddn_ref, dgu_ref, dg_ref):
        @pl.when(pl.program_id(0) == 0)
        def _():
            dg_ref[...] = jnp.zeros_like(dg_ref)
        ddn, dg = _rms_bwd(dn_ref[...], g_ref[...], d3_ref[...])
        dg_ref[...] += dg
        ddn = ddn.astype(BF16)
        ddn_ref[...] = ddn
        dact = _dot_nt(ddn, w_ref[...]).astype(BF16)
        gg = gu_ref[:, :FFN_H]
        uu = gu_ref[:, FFN_H:]
        sg = _sig(gg)
        silu = gg * sg
        dgu_ref[:, :FFN_H] = dact * uu * (sg + silu * (1.0 - sg))
        dgu_ref[:, FFN_H:] = dact * silu

    return _rowcall("b_ffn_down", body, n, tm,
                    [(d3, 'row'), (dn, 'row'), (gu, 'row'), (g_post, 'full'), (w_down, 'full')],
                    [('row', D, BF16), ('row', 2 * FFN_H, BF16), ('acc', (1, D), F32)])


def _b_ffn_up(dgu, d3, x2, w_gu, g_pre, tm):
    n = d3.shape[0]

    def body(dgu_ref, d3_ref, x_ref, w_ref, g_ref, dx_ref, dg_ref):
        @pl.when(pl.program_id(0) == 0)
        def _():
            dg_ref[...] = jnp.zeros_like(dg_ref)
        dhf = _dot_nt(dgu_ref[...], w_ref[...])
        dx, dg = _rms_bwd(x_ref[...], g_ref[...], dhf)
        dg_ref[...] += dg
        dx_ref[...] = d3_ref[...] + dx

    return _rowcall("b_ffn_up", body, n, tm,
                    [(dgu, 'row'), (d3, 'row'), (x2, 'row'), (w_gu, 'full'), (g_pre, 'full')],
                    [('row', D, F32), ('acc', (1, D), F32)])


def _b_attn(dx2, ao, x1, q, g_post, w_o, w_q, g_pre, k, v, tm, tpb, mlen):
    n = dx2.shape[0]
    nb = n // (tm * tpb)
    kv_spec = pl.BlockSpec((mlen, D), lambda i: (i // tpb, 0))
    scale = HEAD_DIM ** -0.5

    def body(dx2_ref, ao_ref, x1_ref, q_ref, go_ref, wo_ref, wq_ref, gp_ref, k_ref, v_ref,
             dao_ref, dq_ref, dx1_ref, dk_ref, dv_ref, dgo_ref, dgp_ref):
        i = pl.program_id(0)

        @pl.when(i == 0)
        def _():
            dgo_ref[...] = jnp.zeros_like(dgo_ref)
            dgp_ref[...] = jnp.zeros_like(dgp_ref)

        @pl.when(i % tpb == 0)
        def _():
            dk_ref[...] = jnp.zeros_like(dk_ref)
            dv_ref[...] = jnp.zeros_like(dv_ref)

        dx2v = dx2_ref[...]
        dao, dgo = _rms_bwd(ao_ref[...], go_ref[...], dx2v)
        dgo_ref[...] += dgo
        dao = dao.astype(BF16)
        dao_ref[...] = dao
        do = _dot_nt(dao, wo_ref[...])
        for h in range(HEADS):
            hs = slice(h * HEAD_DIM, (h + 1) * HEAD_DIM)
            qh = q_ref[:, hs]
            kh = k_ref[:, hs]
            p = _softmax(_dot_nt(qh, kh) * scale)
            doh = do[:, hs].astype(BF16)
            dp = _dot_nt(doh, v_ref[:, hs])
            ds = (p * (dp - jnp.sum(dp * p, axis=-1, keepdims=True)) * scale).astype(BF16)
            dq_ref[:, hs] = _dot(ds, kh).astype(BF16)
            dk_ref[:, hs] += _dot_tn(ds, qh)
            dv_ref[:, hs] += _dot_tn(p.astype(BF16), doh)
        dhc = _dot_nt(dq_ref[...], wq_ref[...])
        dx, dgp = _rms_bwd(x1_ref[...], gp_ref[...], dhc)
        dgp_ref[...] += dgp
        dx1_ref[...] = dx2v + dx

    return _rowcall("b_attn", body, n, tm,
                    [(dx2, 'row'), (ao, 'row'), (x1, 'row'), (q, 'row'), (g_post, 'full'), (w_o, 'full'),
                     (w_q, 'full'), (g_pre, 'full'), (k, kv_spec), (v, kv_spec)],
                    [('row', D, BF16), ('row', D, BF16), ('row', D, F32),
                     (kv_spec, (nb * mlen, D), F32), (kv_spec, (nb * mlen, D), F32),
                     ('acc', (1, D), F32), ('acc', (1, D), F32)])


def _b_mem(dk, dv, mem, w_kv, g_mem, tm):
    n = mem.shape[0]

    def body(dk_ref, dv_ref, m_ref, w_ref, g_ref, dkv_ref, dg_ref):
        @pl.when(pl.program_id(0) == 0)
        def _():
            dg_ref[...] = jnp.zeros_like(dg_ref)
        dkb = dk_ref[...].astype(BF16)
        dvb = dv_ref[...].astype(BF16)
        dkv_ref[:, :D] = dkb
        dkv_ref[:, D:] = dvb
        dmn = _dot_nt(dkb, w_ref[:, :D]) + _dot_nt(dvb, w_ref[:, D:])
        _, dg = _rms_bwd(m_ref[...], g_ref[...], dmn)
        dg_ref[...] += dg

    return _rowcall("b_mem", body, n, tm, [(dk, 'row'), (dv, 'row'), (mem, 'row'), (w_kv, 'full'), (g_mem, 'full')],
                    [('row', 2 * D, BF16), ('acc', (1, D), F32)])


def _b_mix(dx1, mo, a, b, zg, ypre, g_post, w_mix_out, w_br_gm, w_br_s5, w_glu, tm):
    n = dx1.shape[0]

    def body(dx1_ref, mo_ref, a_ref, b_ref, zg_ref, ypre_ref, g_ref, wout_ref, wgm_ref, ws5_ref, wglu_ref,
             dmo_ref, da_ref, db_ref, dzg_ref, dygm_ref, dgate_ref, dypre_ref, dg_ref):
        @pl.when(pl.program_id(0) == 0)
        def _():
            dg_ref[...] = jnp.zeros_like(dg_ref)
        dmo, dg = _rms_bwd(mo_ref[...], g_ref[...], dx1_ref[...])
        dg_ref[...] += dg
        dmo = dmo.astype(BF16)
        dmo_ref[...] = dmo
        dmg = _dot_nt(dmo, wout_ref[...]).astype(BF16)
        zg = zg_ref[...].astype(BF16)
        sa, sb = _sig(zg[:, :D]), _sig(zg[:, D:])
        da = dmg * sa
        db = dmg * sb
        da_ref[...] = da
        db_ref[...] = db
        dzg_ref[:, :D] = dmg * a_ref[...] * (sa * (1.0 - sa))
        dzg_ref[:, D:] = dmg * b_ref[...] * (sb * (1.0 - sb))
        dygm_ref[...] = _dot_nt(da, wgm_ref[...])
        dys5 = _dot_nt(db, ws5_ref[...])
        ypre = ypre_ref[...]
        yg = _gelu(ypre)
        sgt = _sig(_dot(yg.astype(BF16), wglu_ref[...]))
        dgate = (dys5 * yg * sgt * (1.0 - sgt)).astype(BF16)
        dgate_ref[...] = dgate
        dyg = dys5 * sgt + _dot_nt(dgate, wglu_ref[...])
        dypre_ref[...] = dyg * _gelu_grad(ypre)

    return _rowcall("b_mix", body, n, tm,
                    [(dx1, 'row'), (mo, 'row'), (a, 'row'), (b, 'row'), (zg, 'row'), (ypre, 'row'), (g_post, 'full'),
                     (w_mix_out, 'full'), (w_br_gm, 'full'), (w_br_s5, 'full'), (w_glu, 'full')],
                    [('row', D, BF16), ('row', D, BF16), ('row', D, BF16), ('row', 2 * D, BF16), ('row', D, F32),
                     ('row', S5_W, BF16), ('row', S5_W, F32), ('acc', (1, D), F32)])


def _b_s5(zs5, dypre, states, bbc, ccm, avec, dvec, nb, t, comm):
    n = zs5.shape[0]
    nc = n // nb // t
    seg = t // 8
    slab, bb, cm, av, dv, st = _s5_specs(nb, nc, t, True)

    def body(u_ref, dy_ref, st_ref, bb_ref, cc_ref, a_ref, d_ref,
             du_ref, dbb_ref, dcc_ref, da_ref, dd_ref, lcarry, s_ref, l_ref):
        first = jnp.logical_and(pl.program_id(1) == 0, pl.program_id(2) == 0)

        @pl.when(first)
        def _():
            dbb_ref[...] = jnp.zeros_like(dbb_ref)
            dcc_ref[...] = jnp.zeros_like(dcc_ref)
            da_ref[...] = jnp.zeros_like(da_ref)
            dd_ref[...] = jnp.zeros_like(dd_ref)

        @pl.when(pl.program_id(2) == 0)
        def _():
            lcarry[...] = jnp.zeros_like(lcarry)

        up, dyp = _seg_load(u_ref, seg), _seg_load(dy_ref, seg)
        ub, dyb = up.astype(BF16), dyp.astype(BF16)
        a1r, a1i = a_ref[:, :512], a_ref[:, 512:]
        ar, ai = jnp.broadcast_to(a1r, (8, 512)), jnp.broadcast_to(a1i, (8, 512))
        pr, pi = _seg_power(a1r, a1i, seg)

        s_ref[...] = _dot(ub, bb_ref[...]).reshape(seg, 8, 1024)
        tr, ti = _seg_scan(s_ref, ar, ai, seg, False)
        er, ei, _ = _seg_entries(tr, ti, pr, pi, st_ref[...], False)

        def apply(k, pe):
            qr, qi = pe
            qr, qi = ar * qr - ai * qi, ar * qi + ai * qr
            s_ref[k, :, :512] = s_ref[k, :, :512] + qr
            s_ref[k, :, 512:] = s_ref[k, :, 512:] + qi
            return qr, qi

        lax.fori_loop(0, seg, apply, (er, ei), unroll=4)

        l_ref[...] = _dot_nt(dyb, cc_ref[...]).reshape(seg, 8, 1024)
        tr, ti = _seg_scan(l_ref, ar, -ai, seg, True)
        fr, fi, lout = _seg_entries(tr, ti, pr, -pi, lcarry[...], True)
        lcarry[...] = lout

        def apply_back(i, carry):
            qr, qi, accr, acci = carry
            k = seg - 1 - i
            qr, qi = ar * qr + ai * qi, ar * qi - ai * qr
            lr = l_ref[k, :, :512] + qr
            li = l_ref[k, :, 512:] + qi
            l_ref[k, :, :512] = lr
            l_ref[k, :, 512:] = li
            kp = jnp.maximum(k - 1, 0)
            sr = jnp.where(k == 0, er, s_ref[kp, :, :512])
            si = jnp.where(k == 0, ei, s_ref[kp, :, 512:])
            return qr, qi, accr + lr * sr + li * si, acci + li * sr - lr * si

        zero = jnp.zeros((8, 512), F32)
        _, _, accr, acci = lax.fori_loop(0, seg, apply_back, (fr, fi, zero, zero), unroll=4)
        da_ref[:, :512] += jnp.sum(accr, axis=0, keepdims=True)
        da_ref[:, 512:] += jnp.sum(acci, axis=0, keepdims=True)

        s = s_ref[...].reshape(t, 1024).astype(BF16)
        lam = l_ref[...].reshape(t, 1024).astype(BF16)
        dcc_ref[...] += _dot_tn(dyb, s)
        dbb_ref[...] += _dot_tn(ub, lam)
        _seg_store(du_ref, _dot_nt(lam, bb_ref[...]) + d_ref[...] * dyp, seg)
        dd_ref[...] += jnp.sum(dyp * up, axis=0, keepdims=True)

    return _hosting_call(
        body, "b_s5", (S5_SUPER, nb, nc), [slab, slab, st, bb, cm, av, dv], [slab, bb, bb, av, dv],
        [jax.ShapeDtypeStruct((n, S5_W), F32), jax.ShapeDtypeStruct(bbc.shape, F32),
         jax.ShapeDtypeStruct(bbc.shape, F32), jax.ShapeDtypeStruct(avec.shape, F32),
         jax.ShapeDtypeStruct(dvec.shape, F32)],
        [pltpu.VMEM((1, 1024), F32), pltpu.VMEM((seg, 8, 1024), F32), pltpu.VMEM((seg, 8, 1024), F32)],
        ("arbitrary", "arbitrary", "arbitrary"), (zs5, dypre, states, bbc, ccm, avec, dvec), comm)


def _b_gmlp(zgm, dygm, ln_g, ln_b, w_s, b_s, tm, comm):
    n = zgm.shape[0]

    def body(z_ref, dy_ref, lg_ref, lb_ref, ws_ref, bs_ref, dz_ref, dws_ref, dbs_ref, dlg_ref, dlb_ref,
             du_s, dvn_s):
        @pl.when(pl.program_id(0) == 0)
        def _():
            dws_ref[...] = jnp.zeros_like(dws_ref)
            dbs_ref[...] = jnp.zeros_like(dbs_ref)
            dlg_ref[...] = jnp.zeros_like(dlg_ref)
            dlb_ref[...] = jnp.zeros_like(dlb_ref)
        z = z_ref[...].astype(BF16)
        zg = _gelu(z)
        u = zg[:, :D]
        vh, r = _ln_stats(zg[:, D:].astype(F32))
        vn = (vh * lg_ref[...] + lb_ref[...]).astype(BF16)
        dy = dy_ref[...]
        keep = _tril()
        for g in range(GM_GROUPS):
            w = jnp.where(keep, ws_ref[g], 0.0).astype(BF16)
            cs = slice(g * LANE, (g + 1) * LANE)
            for c in range(tm // GM_CHUNK):
                rs = slice(c * GM_CHUNK, (c + 1) * GM_CHUNK)
                vb = vn[rs, cs]
                sv = _dot(w, vb) + bs_ref[g]
                dyb = dy[rs, cs]
                du_s[rs, cs] = dyb * sv
                dsv = dyb * u[rs, cs]
                dsvb = dsv.astype(BF16)
                dvn_s[rs, cs] = _dot_tn(w, dsvb)
                dws_ref[g] += jnp.where(keep, _dot_nt(dsvb, vb), 0.0)
                dbs_ref[g] += jnp.sum(dsv, axis=1, keepdims=True)
        dvn = dvn_s[...]
        dlg_ref[...] += jnp.sum(dvn * vh, axis=0, keepdims=True)
        dlb_ref[...] += jnp.sum(dvn, axis=0, keepdims=True)
        dvh = dvn * lg_ref[...]
        dv = r * (dvh - jnp.mean(dvh, axis=-1, keepdims=True) - vh * jnp.mean(dvh * vh, axis=-1, keepdims=True))
        dz_ref[:, :D] = du_s[...].astype(BF16) * _gelu_grad(z[:, :D])
        dz_ref[:, D:] = dv.astype(BF16) * _gelu_grad(z[:, D:])

    return _rowcall("b_gmlp", body, n, tm,
                    [(zgm, 'row'), (dygm, 'row'), (ln_g, 'full'), (ln_b, 'full'), (w_s, 'full'), (b_s, 'full')],
                    [('row', 2 * D, BF16), ('acc', w_s.shape, F32), ('acc', b_s.shape, F32), ('acc', (1, D), F32),
                     ('acc', (1, D), F32)],
                    scratch=[pltpu.VMEM((tm, D), F32), pltpu.VMEM((tm, D), F32)], comm=comm)


def _b_in(dzgm, dzs5, dzg, dx1, x, w_in, g_pre, tm, comm):
    n = x.shape[0]

    def body(d1_ref, d2_ref, d3_ref, dx1_ref, x_ref, w_ref, g_ref, gx_ref, dg_ref):
        @pl.when(pl.program_id(0) == 0)
        def _():
            dg_ref[...] = jnp.zeros_like(dg_ref)
        dh = (_dot_nt(d1_ref[...], w_ref[:, 0:2 * D]) + _dot_nt(d2_ref[...].astype(BF16), w_ref[:, 2 * D:2 * D + S5_W])
              + _dot_nt(d3_ref[...], w_ref[:, 2 * D + S5_W:]))
        dx, dg = _rms_bwd(x_ref[...], g_ref[...], dh)
        dg_ref[...] += dg
        gx_ref[...] = dx1_ref[...] + dx

    return _rowcall("b_in", body, n, tm,
                    [(dzgm, 'row'), (dzs5, 'row'), (dzg, 'row'), (dx1, 'row'), (x, 'row'), (w_in, 'full'),
                     (g_pre, 'full')],
                    [('row', D, F32), ('acc', (1, D), F32)], comm=comm)


def _whole(name, body, ins, outs):
    return pl.pallas_call(body, name=name, out_shape=[jax.ShapeDtypeStruct(s, dt) for s, dt in outs],
                          compiler_params=_params())(*ins)


def _s5_disc_fwd(lr, li, ls, br, bi):
    def body(lr_ref, li_ref, ls_ref, br_ref, bi_ref, o1, o2, o3, o4):
        outs = _s5_disc(lr_ref[...], li_ref[...], ls_ref[...], br_ref[...], bi_ref[...])
        for o, val in zip((o1, o2, o3, o4), outs):
            o[...] = val

    return _whole("s5_disc_fwd", body, [lr, li, ls, br, bi],
                  [(lr.shape, F32), (lr.shape, F32), (br.shape, F32), (br.shape, F32)])


def _s5_disc_bwd(lr, li, ls, br, bi, cts):
    def body(lr_ref, li_ref, ls_ref, br_ref, bi_ref, c1, c2, c3, c4, o1, o2, o3, o4, o5):
        _, vjp = jax.vjp(_s5_disc, lr_ref[...], li_ref[...], ls_ref[...], br_ref[...], bi_ref[...])
        grads = vjp((c1[...], c2[...], c3[...], c4[...]))
        for o, val in zip((o1, o2, o3, o4, o5), grads):
            o[...] = val

    return _whole("s5_disc_bwd", body, [lr, li, ls, br, bi, *cts],
                  [(lr.shape, F32), (lr.shape, F32), (lr.shape, F32), (br.shape, F32), (br.shape, F32)])


def _sum_slots(name, slots):
    def body(s_ref, o_ref):
        acc = s_ref[0]
        for k in range(1, slots.shape[0]):
            acc = acc + s_ref[k]
        o_ref[...] = acc

    return _whole(name, body, [slots], [(slots.shape[1:], F32)])[0]


def _adamw_math(w, g, m, v):
    c1 = 1.0 - ADAM_B1 ** ADAM_STEP
    c2 = 1.0 - ADAM_B2 ** ADAM_STEP
    mn = ADAM_B1 * m + (1.0 - ADAM_B1) * g
    vn = ADAM_B2 * v + (1.0 - ADAM_B2) * (g * g)
    return -ADAM_LR * ((mn / c1) / (jnp.sqrt(vn / c2) + ADAM_EPS) + ADAM_WD * w), mn, vn


def _adamw_many(name, ws, gs, ms, vs):
    k = len(ws)

    def body(*refs):
        for i in range(k):
            dl, mn, vn = _adamw_math(refs[i][...], refs[k + i][...], refs[2 * k + i][...], refs[3 * k + i][...])
            refs[4 * k + i][...] = dl
            refs[5 * k + i][...] = mn
            refs[6 * k + i][...] = vn

    res = _whole(name, body, [*ws, *gs, *ms, *vs], [(t.shape, F32) for t in ws] * 3)
    return res[:k], res[k:2 * k], res[2 * k:]


def _adamw(name, w, g, m, v, tm):
    def body(w_ref, g_ref, m_ref, v_ref, d_ref, mo_ref, vo_ref):
        d_ref[...], mo_ref[...], vo_ref[...] = _adamw_math(w_ref[...], g_ref[...], m_ref[...], v_ref[...])

    cols = w.shape[1]
    return _rowcall(name, body, w.shape[0], tm, [(w, 'row'), (g, 'row'), (m, 'row'), (v, 'row')],
                    [('row', cols, F32), ('row', cols, F32), ('row', cols, F32)])


def _picked_rowcall(name, body, grid, in_specs, out_cols, tm):
    return pl.pallas_call(
        lambda p_ref, *refs: body(*refs), name=name,
        grid_spec=pltpu.PrefetchScalarGridSpec(
            num_scalar_prefetch=1, grid=(grid,), in_specs=in_specs,
            out_specs=[pl.BlockSpec((tm, D), lambda i, p: (i, 0)) for _ in out_cols]),
        out_shape=[jax.ShapeDtypeStruct((grid * tm, D), dt) for dt in out_cols],
        compiler_params=_params(("arbitrary",)),
    )


def _exchange(copies):
    def build(src_refs, out_refs, ss, rs):
        def rdma(k, src, dst, peer):
            return _Lazy(lambda: pltpu.make_async_remote_copy(
                src_ref=src, dst_ref=dst, send_sem=ss.at[k], recv_sem=rs.at[k], device_id=peer, device_id_type=MESH))
        return copies(src_refs, out_refs, rdma)

    def start(*refs):
        for cp in build(*refs):
            cp.start()

    def finish(*refs):
        for cp in build(*refs):
            cp.wait()

    return start, finish


def _reduce_big(tag, slabs):
    rows = [s.shape[1] for s in slabs]
    blk = sum(rows) // 4
    tm = _tile_rows(blk, 512)
    per = blk // tm
    gpack = jnp.concatenate([s.reshape(4, 2, 2, r // 4, D) for s, r in zip(slabs, rows)], axis=3)

    mx, my, mc = lax.axis_index("x"), lax.axis_index("y"), lax.axis_index("c")
    chip_me, chip_xn, chip_yn = 2 * mx + my, 2 * (1 - mx) + my, 2 * mx + (1 - my)

    def pair_copies(srcs, dsts, rdma):
        c, _, (_, _, psib) = _place()
        return [rdma(0, srcs[0].at[:, 1 - c], dsts[0], psib)]

    from_sib = (yield _Comm([gpack], [((4, 2, blk, D), F32)], 1, _exchange(pair_copies), {}, (0.0, 1.0)))[0]

    def sum_both(a_ref, b_ref, o32_ref, o16_ref):
        s = a_ref[...].astype(F32) + b_ref[...].astype(F32)
        o32_ref[...] = s
        o16_ref[...] = s.astype(BF16)

    p32, p16 = _picked_rowcall(
        "sum_pair_" + tag, sum_both, 8 * per,
        [pl.BlockSpec((None, None, None, tm, D), lambda i, p: (i // (2 * per), p[0], (i // per) % 2, i % per, 0)),
         pl.BlockSpec((tm, D), lambda i, p: (i, 0))],
        [F32, BF16], tm)(jnp.stack([mc]).astype(jnp.int32), gpack, from_sib.reshape(8 * blk, D))
    p32, p16 = p32.reshape(4, 2, blk, D), p16.reshape(4, 2, blk, D)

    def step1_copies(srcs, dsts, rdma):
        _, (me, xn, yn, dg), (px, py, _) = _place()
        s16 = srcs[0]
        return [rdma(0, s16.at[yn, 0], dsts[0].at[0], py), rdma(1, s16.at[dg, 0], dsts[0].at[1], py),
                rdma(2, s16.at[xn, 1], dsts[0].at[2], px), rdma(3, s16.at[dg, 1], dsts[0].at[3], px)]

    recv1 = (yield _Comm([p16], [((4, blk, D), BF16)], 4, _exchange(step1_copies), {}, (0.0, 1.0)))[0]

    s32, s16 = _picked_rowcall(
        "sum_step1_" + tag, sum_both, 4 * per,
        [pl.BlockSpec((None, None, tm, D), lambda i, p: (p[i // per], i // (2 * per), i % per, 0)),
         pl.BlockSpec((tm, D), lambda i, p: (i, 0))],
        [F32, BF16], tm)(jnp.stack([chip_me, chip_xn, chip_me, chip_yn]).astype(jnp.int32), p32,
                         recv1.reshape(4 * blk, D))

    def step2_copies(srcs, dsts, rdma):
        _, _, (px, py, _) = _place()
        return [rdma(0, srcs[0].at[1], dsts[0].at[0], px), rdma(1, srcs[0].at[3], dsts[0].at[1], py)]

    recv2 = (yield _Comm([s16.reshape(4, blk, D)], [((2, blk, D), BF16)], 2, _exchange(step2_copies), {},
                         (0.0, 1.0)))[0]

    def sum_step2(a_ref, b_ref, o_ref):
        o_ref[...] = a_ref[...] + b_ref[...].astype(F32)

    red = _rowcall("sum_step2_" + tag, sum_step2, 2 * blk, tm,
                   [(s32.reshape(4, blk, D), pl.BlockSpec((None, tm, D), lambda i: (2 * (i // per), i % per, 0))),
                    (recv2.reshape(2 * blk, D), 'row')],
                   [('row', D, F32)])[0].reshape(2, blk, D)

    def share_copies(srcs, dsts, rdma):
        _, _, (_, _, psib) = _place()
        return [rdma(0, srcs[0], dsts[0], psib)]

    other = (yield _Comm([red], [((2, blk, D), F32)], 1, _exchange(share_copies), {}, (0.0, 1.0)))[0]
    lo = jnp.where(mc == 0, red, other)
    hi = jnp.where(mc == 0, other, red)
    out, off = [], 0
    for r in rows:
        q = r // 4
        out.append(jnp.concatenate([lo[0, off:off + q], lo[1, off:off + q], hi[0, off:off + q], hi[1, off:off + q]]))
        off += q
    return out


def _rows1024(a):
    flat = a.reshape(-1)
    pad = (-flat.shape[0]) % D
    if pad:
        flat = jnp.concatenate([flat, jnp.zeros((pad,), flat.dtype)])
    return flat.reshape(-1, D)


def _pack(arrs, pad_rows_to=8):
    parts = [_rows1024(a) for a in arrs]
    rows = sum(p.shape[0] for p in parts)
    pad = (-rows) % pad_rows_to
    if pad:
        parts.append(jnp.zeros((pad, D), parts[0].dtype))
    return jnp.concatenate(parts, axis=0)


def _unpack(packed, shapes):
    out, r = [], 0
    for s in shapes:
        size = math.prod(s)
        nr = -(-size // D)
        out.append(packed[r:r + nr].reshape(-1)[:size].reshape(s))
        r += nr
    return out


def _block_diag(t):
    eye = jnp.eye(8, dtype=t.dtype)
    j, g, a, b = t.shape
    return (t[:, :, :, None, :] * eye[None, :, None, :, None]).reshape(j, g * a, g * b)


def _block_diag_take(m, a, b):
    eye = jnp.eye(8, dtype=m.dtype)
    return (m.reshape(4, 8, a, 8, b) * eye[None, :, None, :, None]).sum(axis=3)


def kernel(x, mem, g_mix_pre, w_in, gm_ln_g, gm_ln_b, gm_w_s, gm_b_s, s5_lam_re, s5_lam_im, s5_log_step, s5_b_re, s5_b_im, s5_c_re, s5_c_im, s5_d, s5_w_glu, w_br_gm, w_br_s5, w_mix_out, g_mix_post, g_ca_pre, g_mem, ca_w_q, ca_w_kv, ca_w_o, g_ca_post, g_ffn_pre, ffn_w_gu, ffn_w_down, g_ffn_post, loss_target, m_g_mix_pre, m_w_in, m_gm_ln_g, m_gm_ln_b, m_gm_w_s, m_gm_b_s, m_s5_lam_re, m_s5_lam_im, m_s5_log_step, m_s5_b_re, m_s5_b_im, m_s5_c_re, m_s5_c_im, m_s5_d, m_s5_w_glu, m_w_br_gm, m_w_br_s5, m_w_mix_out, m_g_mix_post, m_g_ca_pre, m_g_mem, m_ca_w_q, m_ca_w_kv, m_ca_w_o, m_g_ca_post, m_g_ffn_pre, m_ffn_w_gu, m_ffn_w_down, m_g_ffn_post, v_g_mix_pre, v_w_in, v_gm_ln_g, v_gm_ln_b, v_gm_w_s, v_gm_b_s, v_s5_lam_re, v_s5_lam_im, v_s5_log_step, v_s5_b_re, v_s5_b_im, v_s5_c_re, v_s5_c_im, v_s5_d, v_s5_w_glu, v_w_br_gm, v_w_br_s5, v_w_mix_out, v_g_mix_post, v_g_ca_pre, v_g_mem, v_ca_w_q, v_ca_w_kv, v_ca_w_o, v_g_ca_post, v_g_ffn_pre, v_ffn_w_gu, v_ffn_w_down, v_g_ffn_post):
    a = dict(locals())
    w = {n: a[n][0] for n in WNAMES}
    nb, seq, _ = x.shape
    n = nb * seq
    mlen = mem.shape[1]
    tm = min(256, seq)
    tmb = min(512, seq)
    tpb = seq // tmb
    xf = x.reshape(n, D)
    tgt = loss_target.reshape(n, D)
    memf = mem.reshape(nb * mlen, D)
    tmm = min(256, mlen)

    big_names = list(BIG)
    local_shapes = {k: w[k].shape for k in big_names}
    w_in_full = _run_comm("gather_w_in", _gather_weights("cast_w_in", [w['w_in']], [BIG['w_in']]))[0]
    rest = [k for k in big_names if k != 'w_in']
    gather_rest = _gather_weights("cast_w_rest", [w[k] for k in rest], [BIG[k] for k in rest])

    vec = lambda name: w[name].reshape(1, D)

    to_lane = lambda p: p.reshape(1, -1)
    b_t = lambda p: p.transpose(2, 0, 1).reshape(16, -1)
    lr_l, li_l = to_lane(w['s5_lam_re']), to_lane(w['s5_lam_im'])
    ls_l = jnp.repeat(w['s5_log_step'], 64).reshape(1, -1)
    br_t, bi_t = b_t(w['s5_b_re']), b_t(w['s5_b_im'])
    ab_re, ab_im, bb_re, bb_im = _s5_disc_fwd(lr_l, li_l, ls_l, br_t, bi_t)
    blk = lambda t: _block_diag(t.reshape(16, 4, 8, 64).transpose(1, 2, 0, 3))
    bbc = jnp.concatenate([blk(bb_re), blk(bb_im)], axis=2).astype(BF16)
    cblk = lambda c: _block_diag(c.reshape(4, 8, 16, 64).transpose(0, 1, 3, 2))
    ccm = jnp.concatenate([cblk(w['s5_c_re']), -cblk(w['s5_c_im'])], axis=1).astype(BF16)
    avec = jnp.concatenate([ab_re.reshape(4, 1, 512), ab_im.reshape(4, 1, 512)], axis=2)
    dvec = w['s5_d'].reshape(4, 1, LANE)

    bs3 = w['gm_b_s'].reshape(GM_GROUPS, GM_CHUNK, 1)
    wf = {'w_in': w_in_full}
    h, zgm, zs5, zg = _f_in(xf, vec('g_mix_pre'), wf['w_in'], tmb)
    ygm = _f_gmlp(zgm, vec('gm_ln_g'), vec('gm_ln_b'), w['gm_w_s'], bs3, tmb)
    s5_t = min(S5_T, seq)
    (ypre, states), gathered = _f_s5(zs5, bbc, ccm, avec, dvec, nb, s5_t, gather_rest)
    for k, g in zip(rest, gathered):
        wf[k] = g.reshape(-1, g.shape[-1]) if BIG[k] == 0 else g
    yg, ys5, a_br, b_br, merged, mo, x1 = _f_mix(ygm, ypre, zg, xf, wf['s5_w_glu'], wf['w_br_gm'], wf['w_br_s5'],
                                                 wf['w_mix_out'], vec('g_mix_post'), tmb)
    mem_n, kk, vv = _f_mem(memf, vec('g_mem'), wf['ca_w_kv'], tmm)
    hc, q_att, o_att, ao, x2 = _f_attn(x1, vec('g_ca_pre'), wf['ca_w_q'], kk, vv, wf['ca_w_o'], vec('g_ca_post'),
                                tmb, tpb, mlen)
    hf, gu, act = _f_ffn_up(x2, vec('g_ffn_pre'), wf['ffn_w_gu'], tmb)
    dn, d3, loss_cols = _f_ffn_down(act, x2, tgt, wf['ffn_w_down'], vec('g_ffn_post'), tmb)

    gsm = {}
    gbig = {}
    ddn, dgu, gsm['g_ffn_post'] = _b_ffn_down(d3, dn, gu, vec('g_ffn_post'), wf['ffn_w_down'], tmb)
    dx2, gsm['g_ffn_pre'] = _b_ffn_up(dgu, d3, x2, wf['ffn_w_gu'], vec('g_ffn_pre'), tmb)
    gbig['ffn_w_down'] = _mm_tn("dw_ffn_down", act, ddn)
    gbig['ffn_w_gu'] = _mm_tn("dw_ffn_gu", dgu, hf)
    dao, dq, dx1, dk, dv, gsm['g_ca_post'], gsm['g_ca_pre'] = _b_attn(
        dx2, ao, x1, q_att, vec('g_ca_post'), wf['ca_w_o'], wf['ca_w_q'], vec('g_ca_pre'), kk, vv, tmb, tpb, mlen)
    gbig['ca_w_o'] = _mm_tn("dw_ca_o", o_att, dao)
    gbig['ca_w_q'] = _mm_tn("dw_ca_q", hc, dq)
    dkv, gsm['g_mem'] = _b_mem(dk, dv, memf, wf['ca_w_kv'], vec('g_mem'), tmm)
    gbig['ca_w_kv'] = _mm_tn("dw_ca_kv", dkv, mem_n)
    dmo, da_br, db_br, dzg, dygm, dgate, dypre, gsm['g_mix_post'] = _b_mix(
        dx1, mo, a_br, b_br, zg, ypre, vec('g_mix_post'), wf['w_mix_out'], wf['w_br_gm'], wf['w_br_s5'],
        wf['s5_w_glu'], tmb)
    gbig['w_mix_out'] = _mm_tn("dw_mix_out", merged, dmo)
    gbig['w_br_gm'] = _mm_tn("dw_br_gm", ygm, da_br)
    gbig['w_br_s5'] = _mm_tn("dw_br_s5", db_br, ys5)
    gbig['s5_w_glu'] = _mm_tn("dw_s5_glu", yg, dgate)
    slab_of = lambda k: gbig[k].reshape(4, -1, D)
    red_rest = _reduce_big("rest", [slab_of(k) for k in rest])
    s5_tb = max(s5_t, min(S5_TB, seq))
    (dzs5, dbbc, dccm_t, davec, ddvec), got = _b_s5(zs5, dypre, states[:, :, ::s5_tb // s5_t], bbc, ccm, avec, dvec,
                                                    nb, s5_tb, next(red_rest))
    dccm = dccm_t.transpose(0, 2, 1)
    (dzgm, gsm['gm_w_s'], dbs3, gsm['gm_ln_g'], gsm['gm_ln_b']), got = _b_gmlp(
        zgm, dygm, vec('gm_ln_g'), vec('gm_ln_b'), w['gm_w_s'], bs3, tmb, red_rest.send(got))
    gsm['gm_b_s'] = dbs3
    grad_x, gsm['g_mix_pre'] = _b_in(dzgm, dzs5, dzg, dx1, xf, wf['w_in'], vec('g_mix_pre'), tmb, None)
    dw_in_gm, got = _mm_tn("dw_in_gm", dzgm, h, comm=red_rest.send(got))
    dw_in_s5, got = _mm_tn("dw_in_s5", dzs5, h, comm=red_rest.send(got))
    greds = dict(zip(rest, _finish(red_rest, got)))

    unblk = lambda m_: _block_diag_take(m_, 16, 64).transpose(2, 0, 1, 3).reshape(16, -1)
    d_bb_re, d_bb_im = unblk(dbbc[:, :, :512]), unblk(dbbc[:, :, 512:])
    cunblk = lambda m_: _block_diag_take(m_, 64, 16).transpose(0, 1, 3, 2).reshape(32, 16, 64)
    gsm['s5_c_re'] = cunblk(dccm[:, :512, :])
    gsm['s5_c_im'] = -cunblk(dccm[:, 512:, :])
    d_ab_re, d_ab_im = davec[:, :, :512].reshape(1, -1), davec[:, :, 512:].reshape(1, -1)
    g_lr, g_li, g_ls, g_br, g_bi = _s5_disc_bwd(lr_l, li_l, ls_l, br_t, bi_t, (d_ab_re, d_ab_im, d_bb_re, d_bb_im))
    gsm['s5_lam_re'], gsm['s5_lam_im'] = g_lr.reshape(32, 64), g_li.reshape(32, 64)
    gsm['s5_log_step'] = g_ls.reshape(32, 64).sum(axis=1)
    from_t = lambda t: t.reshape(16, 32, 64).transpose(1, 2, 0)
    gsm['s5_b_re'], gsm['s5_b_im'] = from_t(g_br), from_t(g_bi)
    gsm['s5_d'] = ddvec.reshape(32, 16)

    small_shapes = [w[k].shape for k in SMALL]
    spack = _pack([gsm[k].reshape(w[k].shape) for k in SMALL] + [loss_cols], 8)
    me_slot = 4 * lax.axis_index("x") + 2 * lax.axis_index("y") + lax.axis_index("c")
    dw_in_g, (sall,) = _mm_tn("dw_in_g", dzg, h, comm=_gather_all(spack))
    gbig['w_in'] = jnp.concatenate([dw_in_gm, dw_in_s5, dw_in_g], axis=0)
    sall = lax.dynamic_update_index_in_dim(sall, spack, me_slot, 0)
    ssum = _sum_slots("sum_small", sall)
    small_red = _unpack(ssum, small_shapes + [(1, D)])
    loss = 0.5 * jnp.sum(small_red[-1]) / D

    red_in = _reduce_big("w_in", [slab_of('w_in')])
    got = _run_comm("reduce_pair_w_in", next(red_in))
    for phase in ("reduce_step1_w_in", "reduce_step2_w_in"):
        got = _run_comm(phase, red_in.send(got))
    got = _run_comm("share_pair_w_in", red_in.send(got))
    greds['w_in'] = _finish(red_in, got)[0]

    res_big = {}
    for k in big_names:
        r_, c_ = local_shapes[k]
        g = greds[k].reshape(r_, c_) if BIG[k] == 0 else greds[k].reshape(c_, r_).T
        dl, mn, vn = _adamw("adamw_" + k, w[k], g, a['m_' + k][0], a['v_' + k][0], _tile_rows(g.shape[0], 256))
        res_big[k] = (g, dl, mn, vn)
    as2d = lambda t: t.reshape(1, -1) if t.ndim == 1 else t
    dl_s, mn_s, vn_s = _adamw_many("adamw_small", [as2d(w[k]) for k in SMALL], [as2d(g) for g in small_red[:-1]],
                                   [as2d(a['m_' + k][0]) for k in SMALL], [as2d(a['v_' + k][0]) for k in SMALL])
    res_small = {k: (g, dl.reshape(w[k].shape), mn.reshape(w[k].shape), vn.reshape(w[k].shape))
                 for k, g, dl, mn, vn in zip(SMALL, small_red[:-1], dl_s, mn_s, vn_s)}

    res = {**res_big, **res_small}
    outs = [loss, grad_x.reshape(nb, seq, D)]
    for i in range(4):
        outs += [res[k][i][None] for k in WNAMES]
    return tuple(outs)


def _finish(gen, got):
    try:
        gen.send(got)
    except StopIteration as done:
        return done.value
    raise AssertionError("the generator has more exchanges")


def _tile_rows(rows, cap=384):
    best = rows
    for t in range(16, min(rows, cap) + 1, 16):
        if rows % t == 0:
            best = t
    return best
```

```python
import math
from typing import Any, NamedTuple

import jax
import jax.numpy as jnp
from jax import lax
from jax.experimental import pallas as pl
from jax.experimental.pallas import tpu as pltpu

F32 = jnp.float32
BF16 = jnp.bfloat16
EPS = 1e-6
D = 1024
GM_CHUNK = 128
GM_GROUPS = 8
S5_W = 512
S5_SUPER = 4
S5_T = 512
S5_TB = 1024
HEADS = 4
HEAD_DIM = 256
FFN_H = 2816
LANE = 128
VMEM_LIMIT = 56 * 1024 * 1024
MESH = pl.DeviceIdType.MESH

ADAM_LR, ADAM_B1, ADAM_B2, ADAM_EPS, ADAM_WD, ADAM_STEP = 0.001, 0.9, 0.999, 1e-08, 0.01, 10

WNAMES = ['g_mix_pre', 'w_in', 'gm_ln_g', 'gm_ln_b', 'gm_w_s', 'gm_b_s', 's5_lam_re', 's5_lam_im', 's5_log_step',
          's5_b_re', 's5_b_im', 's5_c_re', 's5_c_im', 's5_d', 's5_w_glu', 'w_br_gm', 'w_br_s5', 'w_mix_out',
          'g_mix_post', 'g_ca_pre', 'g_mem', 'ca_w_q', 'ca_w_kv', 'ca_w_o', 'g_ca_post', 'g_ffn_pre', 'ffn_w_gu',
          'ffn_w_down', 'g_ffn_post']
BIG = {'w_in': 1, 's5_w_glu': 0, 'w_br_gm': 0, 'w_br_s5': 1, 'w_mix_out': 0, 'ca_w_q': 0, 'ca_w_kv': 1,
       'ca_w_o': 0, 'ffn_w_gu': 1, 'ffn_w_down': 0}
SMALL = [n for n in WNAMES if n not in BIG]


def _dot(a, b):
    return jnp.dot(a, b, preferred_element_type=F32)


def _dot_nt(a, b):
    return lax.dot_general(a, b, (((1,), (1,)), ((), ())), preferred_element_type=F32)


def _dot_tn(a, b):
    return lax.dot_general(a, b, (((0,), (0,)), ((), ())), preferred_element_type=F32)


def _rms_fwd(x, g):
    r = lax.rsqrt(jnp.mean(x * x, axis=-1, keepdims=True) + EPS)
    return x * r * g


def _rms_bwd(x, g, dy):
    r = lax.rsqrt(jnp.mean(x * x, axis=-1, keepdims=True) + EPS)
    xh = x * r
    gdy = dy * g
    dx = r * (gdy - xh * jnp.mean(gdy * xh, axis=-1, keepdims=True))
    dg = jnp.sum(dy * xh, axis=0, keepdims=True)
    return dx, dg


_GC = math.sqrt(2.0 / math.pi)


def _gelu(x):
    return 0.5 * x * (1.0 + jnp.tanh(_GC * (x + 0.044715 * x * x * x)))


def _gelu_grad(x):
    t = jnp.tanh(_GC * (x + 0.044715 * x * x * x))
    return 0.5 * (1.0 + t) + 0.5 * x * (1.0 - t * t) * _GC * (1.0 + 3 * 0.044715 * x * x)


def _sig(x):
    return 1.0 / (1.0 + jnp.exp(-x))


def _cscan(br, bi, ar, ai, reverse):
    t = br.shape[0]
    row = lax.broadcasted_iota(jnp.int32, br.shape, 0)
    pr, pi = ar, ai
    sh = 1
    while sh < t:
        if reverse:
            keep = row < t - sh
            rr, ri = pltpu.roll(br, t - sh, 0), pltpu.roll(bi, t - sh, 0)
        else:
            keep = row >= sh
            rr, ri = pltpu.roll(br, sh, 0), pltpu.roll(bi, sh, 0)
        rr = jnp.where(keep, rr, 0.0)
        ri = jnp.where(keep, ri, 0.0)
        br, bi = br + pr * rr - pi * ri, bi + pr * ri + pi * rr
        pr, pi = pr * pr - pi * pi, 2.0 * pr * pi
        sh *= 2
    return br, bi


def _s5_disc(lr, li, ls, br, bi):
    step = jnp.exp(ls)
    mag = jnp.exp(lr * step)
    ab_re = mag * jnp.cos(li * step)
    ab_im = mag * jnp.sin(li * step)
    den = lr * lr + li * li
    nr = ab_re - 1.0
    co_re = (nr * lr + ab_im * li) / den
    co_im = (ab_im * lr - nr * li) / den
    return ab_re, ab_im, co_re * br - co_im * bi, co_re * bi + co_im * br


def _params(sem=None):
    return pltpu.CompilerParams(dimension_semantics=sem, vmem_limit_bytes=VMEM_LIMIT)


def _rowcall(name, body, n_rows, tm, ins, outs, scratch=(), comm=None):
    def spec(kind, shape):
        if kind == 'row':
            return pl.BlockSpec((tm, shape[1]), lambda i: (i, 0))
        if kind == 'full':
            nd = len(shape)
            return pl.BlockSpec(tuple(shape), lambda i: (0,) * nd, pipeline_mode=pl.Buffered(1))
        if kind == 'acc':
            nd = len(shape)
            return pl.BlockSpec(tuple(shape), lambda i: (0,) * nd)
        return kind

    in_specs = [spec(k, a.shape) for a, k in ins]
    out_shape, out_specs = [], []
    for k, s, dt in outs:
        shape = (n_rows, s) if k == 'row' else tuple(s)
        out_shape.append(jax.ShapeDtypeStruct(shape, dt))
        out_specs.append(spec(k, shape))
    args = [a for a, _ in ins]
    if comm is not None:
        return _hosting_call(body, name, (n_rows // tm,), in_specs, out_specs, out_shape, list(scratch),
                             ("arbitrary",), args, comm)
    return pl.pallas_call(
        body, name=name, grid=(n_rows // tm,), in_specs=in_specs, out_specs=out_specs, out_shape=out_shape,
        scratch_shapes=list(scratch), compiler_params=_params(("arbitrary",)),
    )(*args)


def _tile(n, cap):
    if n <= cap:
        return n
    best = LANE
    for k in range(1, n // LANE + 1):
        t = k * LANE
        if n % t == 0 and t <= cap:
            best = t
    return best


def _mm_tn(name, a, b, comm=None):
    n, k = a.shape
    m = b.shape[1]
    tk, tn = _tile(k, 1536 if comm is not None else 2816), _tile(m, 1536)
    tr = min(n, 1024 if comm is not None or tk > 1536 else 2048)

    def body(a_ref, b_ref, o_ref):
        @pl.when(pl.program_id(2) == 0)
        def _():
            o_ref[...] = jnp.zeros_like(o_ref)
        o_ref[...] += _dot_tn(a_ref[...].astype(BF16), b_ref[...].astype(BF16))

    out_spec = pl.BlockSpec((tk, tn), lambda i, j, r: (i, j))
    out_shape = (k, m)
    grid = (k // tk, m // tn, n // tr)
    in_specs = [pl.BlockSpec((tr, tk), lambda i, j, r: (r, i)), pl.BlockSpec((tr, tn), lambda i, j, r: (r, j))]
    if comm is not None:
        (res,), extra = _hosting_call(body, name, grid, in_specs, [out_spec], [jax.ShapeDtypeStruct(out_shape, F32)],
                                      [], ("arbitrary", "arbitrary", "arbitrary"), (a, b), comm)
        return res, extra
    return pl.pallas_call(
        body, name=name, grid=grid, in_specs=in_specs, out_specs=out_spec,
        out_shape=jax.ShapeDtypeStruct(out_shape, F32),
        compiler_params=_params(("parallel", "parallel", "arbitrary")),
    )(a, b)


def _comm_call(name, srcs, outs, nsem, body_fn, aliases=None):
    ns, no = len(srcs), len(outs)

    def body(*refs):
        body_fn(refs[:ns], refs[ns:ns + no], refs[ns + no], refs[ns + no + 1])

    hbm = pl.BlockSpec(memory_space=pltpu.HBM)
    return pl.pallas_call(
        body, name=name, in_specs=[hbm] * ns, out_specs=[hbm] * no,
        out_shape=[jax.ShapeDtypeStruct(s, d) for s, d in outs],
        scratch_shapes=[pltpu.SemaphoreType.DMA((nsem,)), pltpu.SemaphoreType.DMA((nsem,))],
        input_output_aliases=aliases or {},
    )(*srcs)


def _place():
    x, y, c = lax.axis_index("x"), lax.axis_index("y"), lax.axis_index("c")
    chips = (2 * x + y, 2 * (1 - x) + y, 2 * x + (1 - y), 2 * (1 - x) + (1 - y))
    peers = ((1 - x, y, c), (x, 1 - y, c), (x, y, 1 - c))
    return c, chips, peers


class _Comm(NamedTuple):
    srcs: Any
    outs: Any
    nsem: int
    stages: Any
    aliases: Any
    fracs: Any


class _Lazy:
    def __init__(self, make):
        self._make = make

    def start(self):
        self._make().start()

    def wait(self):
        self._make().wait()

    def wait_recv(self):
        self._make().wait_recv()

    def wait_send(self):
        self._make().wait_send()


def _run_comm(name, comm):
    def body_fn(srcs, dsts, ss, rs):
        for stage in comm.stages:
            stage(srcs, dsts, ss, rs)

    return _comm_call(name, comm.srcs, comm.outs, comm.nsem, body_fn, aliases=comm.aliases)


def _hosting_call(body, name, grid, in_specs, out_specs, out_shape, scratch, sem, args, comm):
    hbm = pl.BlockSpec(memory_space=pltpu.HBM)
    n_in, n_out, n_sc = len(in_specs), len(out_specs), len(scratch)
    ns, no = len(comm.srcs), len(comm.outs)
    total = math.prod(grid)
    steps = [min(total - 1, int(f * total)) for f in comm.fracs]

    def wrapped(*refs):
        ins, csrc = refs[:n_in], refs[n_in:n_in + ns]
        outs, cdst = refs[n_in + ns:n_in + ns + n_out], refs[n_in + ns + n_out:n_in + ns + n_out + no]
        rest = refs[n_in + ns + n_out + no:]
        lin = 0
        for d, size in enumerate(grid):
            lin = lin * size + pl.program_id(d)
        for stage, at in zip(comm.stages, steps):
            @pl.when(lin == at)
            def _(stage=stage):
                stage(csrc, cdst, rest[n_sc], rest[n_sc + 1])
        body(*ins, *outs, *rest[:n_sc])

    res = pl.pallas_call(
        wrapped, name=name, grid=grid, in_specs=list(in_specs) + [hbm] * ns, out_specs=list(out_specs) + [hbm] * no,
        out_shape=list(out_shape) + [jax.ShapeDtypeStruct(s, d) for s, d in comm.outs],
        scratch_shapes=list(scratch) + [pltpu.SemaphoreType.DMA((comm.nsem,)), pltpu.SemaphoreType.DMA((comm.nsem,))],
        input_output_aliases={n_in + i: n_out + o for i, o in comm.aliases.items()},
        compiler_params=_params(sem),
    )(*args, *comm.srcs)
    return res[:n_out], res[n_out:]


def _gather_weights(name, shards, axes):
    nw = len(shards)
    shapes = [s.shape for s in shards]
    outs = [((4,) + s if ax == 0 else (s[0], 4 * s[1]), BF16) for s, ax in zip(shapes, axes)]

    def win(ref, i, chip, start, rows):
        r, cols = shapes[i]
        if axes[i] == 0:
            return ref.at[chip, pl.ds(start, rows), :]
        return ref.at[pl.ds(start, rows), pl.ds(chip * cols, cols)]

    def place_body(*refs):
        srcs, dsts, bufs, sem = refs[:nw], refs[nw:2 * nw], refs[2 * nw:3 * nw], refs[3 * nw]
        me = 2 * lax.axis_index("x") + lax.axis_index("y")
        cps = []
        for i in range(nw):
            bufs[i][...] = srcs[i][...].astype(BF16)
            cps.append(pltpu.make_async_copy(bufs[i], win(dsts[i], i, me, 0, shapes[i][0]), sem.at[i]))
            cps[-1].start()
        for cp in cps:
            cp.wait()

    placed = pl.pallas_call(
        place_body, name=name, out_shape=[jax.ShapeDtypeStruct(s, d) for s, d in outs],
        in_specs=[pl.BlockSpec(memory_space=pltpu.VMEM)] * nw, out_specs=[pl.BlockSpec(memory_space=pltpu.HBM)] * nw,
        scratch_shapes=[pltpu.VMEM(s, BF16) for s in shapes] + [pltpu.SemaphoreType.DMA((nw,))],
        compiler_params=_params(),
    )(*shards)

    def copies(dsts, ss, rs):
        c, (me, xn, yn, dg), (px, py, psib) = _place()

        def rdma(k, window, peer):
            return _Lazy(lambda: pltpu.make_async_remote_copy(
                src_ref=window, dst_ref=window, send_sem=ss.at[k], recv_sem=rs.at[k], device_id=peer,
                device_id_type=MESH))

        first, later, swaps = [], [], []
        for i in range(nw):
            qr, hr, k0 = shapes[i][0] // 4, shapes[i][0] // 2, 6 * i
            q0, q1 = 2 * c * qr, (2 * c + 1) * qr
            w0, w1 = win(dsts[i], i, me, q0, qr), win(dsts[i], i, me, q1, qr)
            first.append([rdma(k0, w0, px), rdma(k0 + 1, w1, py), rdma(k0 + 2, w0, py), rdma(k0 + 3, w1, px)])
            later.append([rdma(k0 + 4, win(dsts[i], i, xn, q0, qr), py), rdma(k0 + 5, win(dsts[i], i, yn, q1, qr), px)])
            swaps.append([rdma(6 * nw + 3 * i + j, win(dsts[i], i, chip, c * hr, hr), psib)
                          for j, chip in enumerate((xn, yn, dg))])
        return first, later, swaps

    def send(srcs, dsts, ss, rs):
        for cps in copies(dsts, ss, rs)[0]:
            for cp in cps:
                cp.start()

    def forward(srcs, dsts, ss, rs):
        first, later, _ = copies(dsts, ss, rs)
        for i in range(nw):
            first[i][0].wait_recv()
            later[i][0].start()
            first[i][1].wait_recv()
            later[i][1].start()

    def swap(srcs, dsts, ss, rs):
        first, later, swaps = copies(dsts, ss, rs)
        for i in range(nw):
            for cp in first[i][2:] + later[i]:
                cp.wait_recv()
            for sw in swaps[i]:
                sw.start()

    def finish(srcs, dsts, ss, rs):
        first, later, swaps = copies(dsts, ss, rs)
        for i in range(nw):
            for sw in swaps[i]:
                sw.wait()
            for cp in first[i] + later[i]:
                cp.wait_send()

    return _Comm(placed, outs, 9 * nw, [send, forward, swap, finish], {i: i for i in range(nw)},
                 (0.0, 0.55, 0.85, 1.0))


def _gather_all(src):
    def copies(srcs, dsts, ss, rs):
        x, y, c = lax.axis_index("x"), lax.axis_index("y"), lax.axis_index("c")
        me, sib = (x, y, c), (x, y, 1 - c)
        chips = [(1 - x, y), (x, 1 - y), (1 - x, 1 - y)]

        def rows(px, py, pc):
            return dsts[0].at[4 * px + 2 * py + pc]

        def copy(k, block, to, src=None):
            return _Lazy(lambda: pltpu.make_async_remote_copy(
                src_ref=rows(*block) if src is None else src, dst_ref=rows(*block), send_sem=ss.at[k],
                recv_sem=rs.at[k], device_id=to, device_id_type=MESH))

        first = [copy(0, me, sib, src=srcs[0])] + [copy(1 + j, me, (*chip, c), src=srcs[0])
                                                   for j, chip in enumerate(chips)]
        passed = [copy(4 + j, (*chip, c), sib) for j, chip in enumerate(chips)]
        landed = [copy(0, sib, me)] + [copy(1 + j, (*chip, c), me) for j, chip in enumerate(chips)]
        landed += [copy(4 + j, (*chip, 1 - c), me) for j, chip in enumerate(chips)]
        return first, passed, landed

    def send(*refs):
        for cp in copies(*refs)[0]:
            cp.start()

    def forward(*refs):
        _, passed, landed = copies(*refs)
        for j in range(3):
            landed[1 + j].wait_recv()
            passed[j].start()

    def finish(*refs):
        first, passed, landed = copies(*refs)
        landed[0].wait_recv()
        for cp in landed[4:]:
            cp.wait_recv()
        for cp in first + passed:
            cp.wait_send()

    return _Comm([src], [((8,) + src.shape, src.dtype)], 7, [send, forward, finish], {}, (0.0, 0.9, 1.0))


def _f_in(x, g, w_in, tm):
    n = x.shape[0]

    def body(x_ref, g_ref, w_ref, h_ref, zgm_ref, zs5_ref, zg_ref):
        h = _rms_fwd(x_ref[...], g_ref[...]).astype(BF16)
        h_ref[...] = h
        zgm_ref[...] = _dot(h, w_ref[:, 0:2 * D])
        zs5_ref[...] = _dot(h, w_ref[:, 2 * D:2 * D + S5_W])
        zg_ref[...] = _dot(h, w_ref[:, 2 * D + S5_W:])

    return _rowcall("f_in", body, n, tm, [(x, 'row'), (g, 'full'), (w_in, 'full')],
                    [('row', D, BF16), ('row', 2 * D, F32), ('row', S5_W, F32), ('row', 2 * D, F32)])


def _tril():
    r = lax.broadcasted_iota(jnp.int32, (GM_CHUNK, GM_CHUNK), 0)
    c = lax.broadcasted_iota(jnp.int32, (GM_CHUNK, GM_CHUNK), 1)
    return r >= c


def _ln_stats(v):
    mu = jnp.mean(v, axis=-1, keepdims=True)
    vc = v - mu
    r = lax.rsqrt(jnp.mean(vc * vc, axis=-1, keepdims=True) + EPS)
    return vc * r, r


def _f_gmlp(zgm, ln_g, ln_b, w_s, b_s, tm):
    n = zgm.shape[0]

    def body(z_ref, lg_ref, lb_ref, ws_ref, bs_ref, y_ref):
        zg = _gelu(z_ref[...].astype(BF16))
        u = zg[:, :D]
        vh, _ = _ln_stats(zg[:, D:].astype(F32))
        vn = (vh * lg_ref[...] + lb_ref[...]).astype(BF16)
        keep = _tril()
        for g in range(GM_GROUPS):
            w = jnp.where(keep, ws_ref[g], 0.0).astype(BF16)
            cs = slice(g * LANE, (g + 1) * LANE)
            for c in range(tm // GM_CHUNK):
                rs = slice(c * GM_CHUNK, (c + 1) * GM_CHUNK)
                sv = _dot(w, vn[rs, cs]) + bs_ref[g]
                y_ref[rs, cs] = u[rs, cs] * sv.astype(BF16)

    return _rowcall("f_gmlp", body, n, tm,
                    [(zgm, 'row'), (ln_g, 'full'), (ln_b, 'full'), (w_s, 'full'), (b_s, 'full')],
                    [('row', D, BF16)])[0]


def _s5_specs(nb, nc, t, rev):
    def cc(c):
        return nc - 1 - c if rev else c
    slab = pl.BlockSpec((t, LANE), lambda j, b, c: (b * nc + cc(c), j))
    bb = pl.BlockSpec((None, LANE, 1024), lambda j, b, c: (j, 0, 0))
    cm = pl.BlockSpec((None, 1024, LANE), lambda j, b, c: (j, 0, 0))
    av = pl.BlockSpec((None, 1, 1024), lambda j, b, c: (j, 0, 0))
    dv = pl.BlockSpec((None, 1, LANE), lambda j, b, c: (j, 0, 0))
    st = pl.BlockSpec((None, None, None, 1, 1024), lambda j, b, c: (j, b, cc(c), 0, 0))
    return slab, bb, cm, av, dv, st


def _seg_load(ref, seg):
    return jnp.concatenate([ref[pl.ds(k, 8, stride=seg), :] for k in range(seg)], axis=0)


def _seg_store(ref, val, seg):
    for k in range(seg):
        ref[pl.ds(k, 8, stride=seg), :] = val[8 * k:8 * k + 8, :]


def _seg_scan(s_ref, ar, ai, seg, reverse):
    def step(i, carry):
        k = seg - 1 - i if reverse else i
        sr, si = carry
        nr = ar * sr - ai * si + s_ref[k, :, :512]
        ni = ar * si + ai * sr + s_ref[k, :, 512:]
        s_ref[k, :, :512] = nr
        s_ref[k, :, 512:] = ni
        return nr, ni

    zero = jnp.zeros((8, 512), F32)
    return lax.fori_loop(0, seg, step, (zero, zero), unroll=4)


def _seg_entries(tr, ti, pr, pi, cin, reverse):
    row = lax.broadcasted_iota(jnp.int32, (8, 512), 0)
    edge = row == (7 if reverse else 0)
    cr, ci = cin[:, :512], cin[:, 512:]
    xr = tr + jnp.where(edge, pr * cr - pi * ci, 0.0)
    xi = ti + jnp.where(edge, pr * ci + pi * cr, 0.0)
    ir, ii = _cscan(xr, xi, pr, pi, reverse)
    shift = 7 if reverse else 1
    er = jnp.where(edge, cr, pltpu.roll(ir, shift, 0))
    ei = jnp.where(edge, ci, pltpu.roll(ii, shift, 0))
    far = row == (0 if reverse else 7)
    out = jnp.concatenate([jnp.sum(jnp.where(far, ir, 0.0), axis=0, keepdims=True),
                           jnp.sum(jnp.where(far, ii, 0.0), axis=0, keepdims=True)], axis=1)
    return er, ei, out


def _seg_power(ar, ai, seg):
    pr, pi = ar, ai
    for _ in range(seg.bit_length() - 1):
        pr, pi = pr * pr - pi * pi, 2.0 * pr * pi
    return pr, pi


def _f_s5(zs5, bbc, ccm, avec, dvec, nb, t, comm):
    n = zs5.shape[0]
    nc = n // nb // t
    seg = t // 8
    slab, bb, cm, av, dv, st = _s5_specs(nb, nc, t, False)

    def body(u_ref, bb_ref, cc_ref, a_ref, d_ref, y_ref, st_ref, carry, s_ref):
        @pl.when(pl.program_id(2) == 0)
        def _():
            carry[...] = jnp.zeros_like(carry)
        cin = carry[...]
        st_ref[...] = cin
        up = _seg_load(u_ref, seg)
        s_ref[...] = _dot(up.astype(BF16), bb_ref[...]).reshape(seg, 8, 1024)
        a1r, a1i = a_ref[:, :512], a_ref[:, 512:]
        ar, ai = jnp.broadcast_to(a1r, (8, 512)), jnp.broadcast_to(a1i, (8, 512))
        tr, ti = _seg_scan(s_ref, ar, ai, seg, False)
        er, ei, cout = _seg_entries(tr, ti, *_seg_power(a1r, a1i, seg), cin, False)
        carry[...] = cout

        def apply(k, pe):
            pr, pi = pe
            pr, pi = ar * pr - ai * pi, ar * pi + ai * pr
            s_ref[k, :, :512] = s_ref[k, :, :512] + pr
            s_ref[k, :, 512:] = s_ref[k, :, 512:] + pi
            return pr, pi

        lax.fori_loop(0, seg, apply, (er, ei), unroll=4)
        s = s_ref[...].reshape(t, 1024).astype(BF16)
        _seg_store(y_ref, _dot(s, cc_ref[...]) + d_ref[...] * up, seg)

    return _hosting_call(
        body, "f_s5", (S5_SUPER, nb, nc), [slab, bb, cm, av, dv], [slab, st],
        [jax.ShapeDtypeStruct((n, S5_W), F32), jax.ShapeDtypeStruct((S5_SUPER, nb, nc, 1, 1024), F32)],
        [pltpu.VMEM((1, 1024), F32), pltpu.VMEM((seg, 8, 1024), F32)], ("arbitrary", "arbitrary", "arbitrary"),
        (zs5, bbc, ccm, avec, dvec), comm)


def _f_mix(ygm, ypre, zg, x, w_glu, w_br_gm, w_br_s5, w_mix_out, g_post, tm):
    n = x.shape[0]

    def body(ygm_ref, ypre_ref, zg_ref, x_ref, wglu_ref, wgm_ref, ws5_ref, wout_ref, g_ref,
             yg_ref, ys5_ref, a_ref, b_ref, mg_ref, mo_ref, x1_ref):
        yg = _gelu(ypre_ref[...])
        ygb = yg.astype(BF16)
        yg_ref[...] = ygb
        ys5 = (yg * _sig(_dot(ygb, wglu_ref[...]))).astype(BF16)
        ys5_ref[...] = ys5
        a = _dot(ygm_ref[...], wgm_ref[...])
        b = _dot(ys5, ws5_ref[...])
        a_ref[...] = a.astype(BF16)
        b_ref[...] = b.astype(BF16)
        zg = zg_ref[...]
        merged = (_sig(zg[:, :D]) * a + _sig(zg[:, D:]) * b).astype(BF16)
        mg_ref[...] = merged
        mo = _dot(merged, wout_ref[...])
        mo_ref[...] = mo
        x1_ref[...] = x_ref[...] + _rms_fwd(mo, g_ref[...])

    return _rowcall("f_mix", body, n, tm,
                    [(ygm, 'row'), (ypre, 'row'), (zg, 'row'), (x, 'row'), (w_glu, 'full'), (w_br_gm, 'full'),
                     (w_br_s5, 'full'), (w_mix_out, 'full'), (g_post, 'full')],
                    [('row', S5_W, BF16), ('row', S5_W, BF16), ('row', D, BF16), ('row', D, BF16),
                     ('row', D, BF16), ('row', D, F32), ('row', D, F32)])


def _f_mem(mem, g_mem, w_kv, tm):
    n = mem.shape[0]

    def body(m_ref, g_ref, w_ref, mn_ref, k_ref, v_ref):
        mn = _rms_fwd(m_ref[...], g_ref[...]).astype(BF16)
        mn_ref[...] = mn
        k_ref[...] = _dot(mn, w_ref[:, :D]).astype(BF16)
        v_ref[...] = _dot(mn, w_ref[:, D:]).astype(BF16)

    return _rowcall("f_mem", body, n, tm, [(mem, 'row'), (g_mem, 'full'), (w_kv, 'full')],
                    [('row', D, BF16), ('row', D, BF16), ('row', D, BF16)])


def _softmax(s):
    m = jnp.max(s, axis=-1, keepdims=True)
    e = jnp.exp(s - m)
    return e / jnp.sum(e, axis=-1, keepdims=True)


def _f_attn(x1, g_pre, w_q, k, v, w_o, g_post, tm, tpb, mlen):
    n = x1.shape[0]
    kv_spec = pl.BlockSpec((mlen, D), lambda i: (i // tpb, 0))
    scale = HEAD_DIM ** -0.5

    def body(x_ref, gp_ref, wq_ref, k_ref, v_ref, wo_ref, go_ref, hc_ref, q_ref, o_ref, ao_ref, x2_ref):
        x1v = x_ref[...]
        hc = _rms_fwd(x1v, gp_ref[...]).astype(BF16)
        hc_ref[...] = hc
        q_ref[...] = _dot(hc, wq_ref[...]).astype(BF16)
        for h in range(HEADS):
            hs = slice(h * HEAD_DIM, (h + 1) * HEAD_DIM)
            p = _softmax(_dot_nt(q_ref[:, hs], k_ref[:, hs]) * scale)
            o_ref[:, hs] = _dot(p.astype(BF16), v_ref[:, hs]).astype(BF16)
        ao = _dot(o_ref[...], wo_ref[...])
        ao_ref[...] = ao
        x2_ref[...] = x1v + _rms_fwd(ao, go_ref[...])

    return _rowcall("f_attn", body, n, tm,
                    [(x1, 'row'), (g_pre, 'full'), (w_q, 'full'), (k, kv_spec), (v, kv_spec), (w_o, 'full'),
                     (g_post, 'full')],
                    [('row', D, BF16), ('row', D, BF16), ('row', D, BF16), ('row', D, F32), ('row', D, F32)])


def _f_ffn_up(x2, g_pre, w_gu, tm):
    n = x2.shape[0]

    def body(x_ref, g_ref, w_ref, hf_ref, gu_ref, act_ref):
        hf = _rms_fwd(x_ref[...], g_ref[...]).astype(BF16)
        hf_ref[...] = hf
        gg = _dot(hf, w_ref[:, :FFN_H])
        uu = _dot(hf, w_ref[:, FFN_H:])
        gu_ref[:, :FFN_H] = gg.astype(BF16)
        gu_ref[:, FFN_H:] = uu.astype(BF16)
        act_ref[...] = (gg * _sig(gg) * uu).astype(BF16)

    return _rowcall("f_ffn_up", body, n, tm, [(x2, 'row'), (g_pre, 'full'), (w_gu, 'full')],
                    [('row', D, BF16), ('row', 2 * FFN_H, BF16), ('row', FFN_H, BF16)])


def _f_ffn_down(act, x2, tgt, w_down, g_post, tm):
    n = x2.shape[0]

    def body(a_ref, x_ref, t_ref, w_ref, g_ref, dn_ref, d3_ref, loss_ref):
        @pl.when(pl.program_id(0) == 0)
        def _():
            loss_ref[...] = jnp.zeros_like(loss_ref)
        dn = _dot(a_ref[...], w_ref[...])
        dn_ref[...] = dn
        err = x_ref[...] + _rms_fwd(dn, g_ref[...]) - t_ref[...]
        d3_ref[...] = err * (1.0 / D)
        loss_ref[...] += jnp.sum(err * err, axis=0, keepdims=True)

    return _rowcall("f_ffn_down", body, n, tm,
                    [(act, 'row'), (x2, 'row'), (tgt, 'row'), (w_down, 'full'), (g_post, 'full')],
                    [('row', D, F32), ('row', D, F32), ('acc', (1, D), F32)])


def _b_ffn_down(d3, dn, gu, g_post, w_down, tm):
    n = d3.shape[0]

    def body(d3_ref, dn_ref, gu_ref, g_ref, w_ref, ddn_ref, dgu_ref, dg_ref):
        @pl.when(pl.program_id(0) == 0)
        def _():
            dg_ref[...] = jnp.zeros_like(dg_ref)
        ddn, dg = _rms_bwd(dn_ref[...], g_ref[...], d3_ref[...])
        dg_ref[...] += dg
        ddn = ddn.astype(BF16)
        ddn_ref[...] = ddn
        dact = _dot_nt(ddn, w_ref[...]).astype(BF16)
        gg = gu_ref[:, :FFN_H]
        uu = gu_ref[:, FFN_H:]
        sg = _sig(gg)
        silu = gg * sg
        dgu_ref[:, :FFN_H] = dact * uu * (sg + silu * (1.0 - sg))
        dgu_ref[:, FFN_H:] = dact * silu

    return _rowcall("b_ffn_down", body, n, tm,
                    [(d3, 'row'), (dn, 'row'), (gu, 'row'), (g_post, 'full'), (w_down, 'full')],
                    [('row', D, BF16), ('row', 2 * FFN_H, BF16), ('acc', (1, D), F32)])


def _b_ffn_up(dgu, d3, x2, w_gu, g_pre, tm):
    n = d3.shape[0]

    def body(dgu_ref, d3_ref, x_ref, w_ref, g_ref, dx_ref, dg_ref):
        @pl.when(pl.program_id(0) == 0)
        def _():
            dg_ref[...] = jnp.zeros_like(dg_ref)
        dhf = _dot_nt(dgu_ref[...], w_ref[...])
        dx, dg = _rms_bwd(x_ref[...], g_ref[...], dhf)
        dg_ref[...] += dg
        dx_ref[...] = d3_ref[...] + dx

    return _rowcall("b_ffn_up", body, n, tm,
                    [(dgu, 'row'), (d3, 'row'), (x2, 'row'), (w_gu, 'full'), (g_pre, 'full')],
                    [('row', D, F32), ('acc', (1, D), F32)])


def _b_attn(dx2, ao, x1, q, g_post, w_o, w_q, g_pre, k, v, tm, tpb, mlen):
    n = dx2.shape[0]
    nb = n // (tm * tpb)
    kv_spec = pl.BlockSpec((mlen, D), lambda i: (i // tpb, 0))
    scale = HEAD_DIM ** -0.5

    def body(dx2_ref, ao_ref, x1_ref, q_ref, go_ref, wo_ref, wq_ref, gp_ref, k_ref, v_ref,
             dao_ref, dq_ref, dx1_ref, dk_ref, dv_ref, dgo_ref, dgp_ref):
        i = pl.program_id(0)

        @pl.when(i == 0)
        def _():
            dgo_ref[...] = jnp.zeros_like(dgo_ref)
            dgp_ref[...] = jnp.zeros_like(dgp_ref)

        @pl.when(i % tpb == 0)
        def _():
            dk_ref[...] = jnp.zeros_like(dk_ref)
            dv_ref[...] = jnp.zeros_like(dv_ref)

        dx2v = dx2_ref[...]
        dao, dgo = _rms_bwd(ao_ref[...], go_ref[...], dx2v)
        dgo_ref[...] += dgo
        dao = dao.astype(BF16)
        dao_ref[...] = dao
        do = _dot_nt(dao, wo_ref[...])
        for h in range(HEADS):
            hs = slice(h * HEAD_DIM, (h + 1) * HEAD_DIM)
            qh = q_ref[:, hs]
            kh = k_ref[:, hs]
            p = _softmax(_dot_nt(qh, kh) * scale)
            doh = do[:, hs].astype(BF16)
            dp = _dot_nt(doh, v_ref[:, hs])
            ds = (p * (dp - jnp.sum(dp * p, axis=-1, keepdims=True)) * scale).astype(BF16)
            dq_ref[:, hs] = _dot(ds, kh).astype(BF16)
            dk_ref[:, hs] += _dot_tn(ds, qh)
            dv_ref[:, hs] += _dot_tn(p.astype(BF16), doh)
        dhc = _dot_nt(dq_ref[...], wq_ref[...])
        dx, dgp = _rms_bwd(x1_ref[...], gp_ref[...], dhc)
        dgp_ref[...] += dgp
        dx1_ref[...] = dx2v + dx

    return _rowcall("b_attn", body, n, tm,
                    [(dx2, 'row'), (ao, 'row'), (x1, 'row'), (q, 'row'), (g_post, 'full'), (w_o, 'full'),
                     (w_q, 'full'), (g_pre, 'full'), (k, kv_spec), (v, kv_spec)],
                    [('row', D, BF16), ('row', D, BF16), ('row', D, F32),
                     (kv_spec, (nb * mlen, D), F32), (kv_spec, (nb * mlen, D), F32),
                     ('acc', (1, D), F32), ('acc', (1, D), F32)])


def _b_mem(dk, dv, mem, w_kv, g_mem, tm):
    n = mem.shape[0]

    def body(dk_ref, dv_ref, m_ref, w_ref, g_ref, dkv_ref, dg_ref):
        @pl.when(pl.program_id(0) == 0)
        def _():
            dg_ref[...] = jnp.zeros_like(dg_ref)
        dkb = dk_ref[...].astype(BF16)
        dvb = dv_ref[...].astype(BF16)
        dkv_ref[:, :D] = dkb
        dkv_ref[:, D:] = dvb
        dmn = _dot_nt(dkb, w_ref[:, :D]) + _dot_nt(dvb, w_ref[:, D:])
        _, dg = _rms_bwd(m_ref[...], g_ref[...], dmn)
        dg_ref[...] += dg

    return _rowcall("b_mem", body, n, tm, [(dk, 'row'), (dv, 'row'), (mem, 'row'), (w_kv, 'full'), (g_mem, 'full')],
                    [('row', 2 * D, BF16), ('acc', (1, D), F32)])


def _b_mix(dx1, mo, a, b, zg, ypre, g_post, w_mix_out, w_br_gm, w_br_s5, w_glu, tm):
    n = dx1.shape[0]

    def body(dx1_ref, mo_ref, a_ref, b_ref, zg_ref, ypre_ref, g_ref, wout_ref, wgm_ref, ws5_ref, wglu_ref,
             dmo_ref, da_ref, db_ref, dzg_ref, dygm_ref, dgate_ref, dypre_ref, dg_ref):
        @pl.when(pl.program_id(0) == 0)
        def _():
            dg_ref[...] = jnp.zeros_like(dg_ref)
        dmo, dg = _rms_bwd(mo_ref[...], g_ref[...], dx1_ref[...])
        dg_ref[...] += dg
        dmo = dmo.astype(BF16)
        dmo_ref[...] = dmo
        dmg = _dot_nt(dmo, wout_ref[...]).astype(BF16)
        zg = zg_ref[...].astype(BF16)
        sa, sb = _sig(zg[:, :D]), _sig(zg[:, D:])
        da = dmg * sa
        db = dmg * sb
        da_ref[...] = da
        db_ref[...] = db
        dzg_ref[:, :D] = dmg * a_ref[...] * (sa * (1.0 - sa))
        dzg_ref[:, D:] = dmg * b_ref[...] * (sb * (1.0 - sb))
        dygm_ref[...] = _dot_nt(da, wgm_ref[...])
        dys5 = _dot_nt(db, ws5_ref[...])
        ypre = ypre_ref[...]
        yg = _gelu(ypre)
        sgt = _sig(_dot(yg.astype(BF16), wglu_ref[...]))
        dgate = (dys5 * yg * sgt * (1.0 - sgt)).astype(BF16)
        dgate_ref[...] = dgate
        dyg = dys5 * sgt + _dot_nt(dgate, wglu_ref[...])
        dypre_ref[...] = dyg * _gelu_grad(ypre)

    return _rowcall("b_mix", body, n, tm,
                    [(dx1, 'row'), (mo, 'row'), (a, 'row'), (b, 'row'), (zg, 'row'), (ypre, 'row'), (g_post, 'full'),
                     (w_mix_out, 'full'), (w_br_gm, 'full'), (w_br_s5, 'full'), (w_glu, 'full')],
                    [('row', D, BF16), ('row', D, BF16), ('row', D, BF16), ('row', 2 * D, BF16), ('row', D, F32),
                     ('row', S5_W, BF16), ('row', S5_W, F32), ('acc', (1, D), F32)])


def _b_s5(zs5, dypre, states, bbc, ccm, avec, dvec, nb, t, comm):
    n = zs5.shape[0]
    nc = n // nb // t
    seg = t // 8
    slab, bb, cm, av, dv, st = _s5_specs(nb, nc, t, True)

    def body(u_ref, dy_ref, st_ref, bb_ref, cc_ref, a_ref, d_ref,
             du_ref, dbb_ref, dcc_ref, da_ref, dd_ref, lcarry, s_ref, l_ref):
        first = jnp.logical_and(pl.program_id(1) == 0, pl.program_id(2) == 0)

        @pl.when(first)
        def _():
            dbb_ref[...] = jnp.zeros_like(dbb_ref)
            dcc_ref[...] = jnp.zeros_like(dcc_ref)
            da_ref[...] = jnp.zeros_like(da_ref)
            dd_ref[...] = jnp.zeros_like(dd_ref)

        @pl.when(pl.program_id(2) == 0)
        def _():
            lcarry[...] = jnp.zeros_like(lcarry)

        up, dyp = _seg_load(u_ref, seg), _seg_load(dy_ref, seg)
        ub, dyb = up.astype(BF16), dyp.astype(BF16)
        a1r, a1i = a_ref[:, :512], a_ref[:, 512:]
        ar, ai = jnp.broadcast_to(a1r, (8, 512)), jnp.broadcast_to(a1i, (8, 512))
        pr, pi = _seg_power(a1r, a1i, seg)

        s_ref[...] = _dot(ub, bb_ref[...]).reshape(seg, 8, 1024)
        tr, ti = _seg_scan(s_ref, ar, ai, seg, False)
        er, ei, _ = _seg_entries(tr, ti, pr, pi, st_ref[...], False)

        def apply(k, pe):
            qr, qi = pe
            qr, qi = ar * qr - ai * qi, ar * qi + ai * qr
            s_ref[k, :, :512] = s_ref[k, :, :512] + qr
            s_ref[k, :, 512:] = s_ref[k, :, 512:] + qi
            return qr, qi

        lax.fori_loop(0, seg, apply, (er, ei), unroll=4)

        l_ref[...] = _dot_nt(dyb, cc_ref[...]).reshape(seg, 8, 1024)
        tr, ti = _seg_scan(l_ref, ar, -ai, seg, True)
        fr, fi, lout = _seg_entries(tr, ti, pr, -pi, lcarry[...], True)
        lcarry[...] = lout

        def apply_back(i, carry):
            qr, qi, accr, acci = carry
            k = seg - 1 - i
            qr, qi = ar * qr + ai * qi, ar * qi - ai * qr
            lr = l_ref[k, :, :512] + qr
            li = l_ref[k, :, 512:] + qi
            l_ref[k, :, :512] = lr
            l_ref[k, :, 512:] = li
            kp = jnp.maximum(k - 1, 0)
            sr = jnp.where(k == 0, er, s_ref[kp, :, :512])
            si = jnp.where(k == 0, ei, s_ref[kp, :, 512:])
            return qr, qi, accr + lr * sr + li * si, acci + li * sr - lr * si

        zero = jnp.zeros((8, 512), F32)
        _, _, accr, acci = lax.fori_loop(0, seg, apply_back, (fr, fi, zero, zero), unroll=4)
        da_ref[:, :512] += jnp.sum(accr, axis=0, keepdims=True)
        da_ref[:, 512:] += jnp.sum(acci, axis=0, keepdims=True)

        s = s_ref[...].reshape(t, 1024).astype(BF16)
        lam = l_ref[...].reshape(t, 1024).astype(BF16)
        dcc_ref[...] += _dot_tn(dyb, s)
        dbb_ref[...] += _dot_tn(ub, lam)
        _seg_store(du_ref, _dot_nt(lam, bb_ref[...]) + d_ref[...] * dyp, seg)
        dd_ref[...] += jnp.sum(dyp * up, axis=0, keepdims=True)

    return _hosting_call(
        body, "b_s5", (S5_SUPER, nb, nc), [slab, slab, st, bb, cm, av, dv], [slab, bb, bb, av, dv],
        [jax.ShapeDtypeStruct((n, S5_W), F32), jax.ShapeDtypeStruct(bbc.shape, F32),
         jax.ShapeDtypeStruct(bbc.shape, F32), jax.ShapeDtypeStruct(avec.shape, F32),
         jax.ShapeDtypeStruct(dvec.shape, F32)],
        [pltpu.VMEM((1, 1024), F32), pltpu.VMEM((seg, 8, 1024), F32), pltpu.VMEM((seg, 8, 1024), F32)],
        ("arbitrary", "arbitrary", "arbitrary"), (zs5, dypre, states, bbc, ccm, avec, dvec), comm)


def _b_gmlp(zgm, dygm, ln_g, ln_b, w_s, b_s, tm, comm):
    n = zgm.shape[0]

    def body(z_ref, dy_ref, lg_ref, lb_ref, ws_ref, bs_ref, dz_ref, dws_ref, dbs_ref, dlg_ref, dlb_ref,
             du_s, dvn_s):
        @pl.when(pl.program_id(0) == 0)
        def _():
            dws_ref[...] = jnp.zeros_like(dws_ref)
            dbs_ref[...] = jnp.zeros_like(dbs_ref)
            dlg_ref[...] = jnp.zeros_like(dlg_ref)
            dlb_ref[...] = jnp.zeros_like(dlb_ref)
        z = z_ref[...].astype(BF16)
        zg = _gelu(z)
        u = zg[:, :D]
        vh, r = _ln_stats(zg[:, D:].astype(F32))
        vn = (vh * lg_ref[...] + lb_ref[...]).astype(BF16)
        dy = dy_ref[...]
        keep = _tril()
        for g in range(GM_GROUPS):
            w = jnp.where(keep, ws_ref[g], 0.0).astype(BF16)
            cs = slice(g * LANE, (g + 1) * LANE)
            for c in range(tm // GM_CHUNK):
                rs = slice(c * GM_CHUNK, (c + 1) * GM_CHUNK)
                vb = vn[rs, cs]
                sv = _dot(w, vb) + bs_ref[g]
                dyb = dy[rs, cs]
                du_s[rs, cs] = dyb * sv
                dsv = dyb * u[rs, cs]
                dsvb = dsv.astype(BF16)
                dvn_s[rs, cs] = _dot_tn(w, dsvb)
                dws_ref[g] += jnp.where(keep, _dot_nt(dsvb, vb), 0.0)
                dbs_ref[g] += jnp.sum(dsv, axis=1, keepdims=True)
        dvn = dvn_s[...]
        dlg_ref[...] += jnp.sum(dvn * vh, axis=0, keepdims=True)
        dlb_ref[...] += jnp.sum(dvn, axis=0, keepdims=True)
        dvh = dvn * lg_ref[...]
        dv = r * (dvh - jnp.mean(dvh, axis=-1, keepdims=True) - vh * jnp.mean(dvh * vh, axis=-1, keepdims=True))
        dz_ref[:, :D] = du_s[...].astype(BF16) * _gelu_grad(z[:, :D])
        dz_ref[:, D:] = dv.astype(BF16) * _gelu_grad(z[:, D:])

    return _rowcall("b_gmlp", body, n, tm,
                    [(zgm, 'row'), (dygm, 'row'), (ln_g, 'full'), (ln_b, 'full'), (w_s, 'full'), (b_s, 'full')],
                    [('row', 2 * D, BF16), ('acc', w_s.shape, F32), ('acc', b_s.shape, F32), ('acc', (1, D), F32),
                     ('acc', (1, D), F32)],
                    scratch=[pltpu.VMEM((tm, D), F32), pltpu.VMEM((tm, D), F32)], comm=comm)


def _b_in(dzgm, dzs5, dzg, dx1, x, w_in, g_pre, tm, comm):
    n = x.shape[0]

    def body(d1_ref, d2_ref, d3_ref, dx1_ref, x_ref, w_ref, g_ref, gx_ref, dg_ref):
        @pl.when(pl.program_id(0) == 0)
        def _():
            dg_ref[...] = jnp.zeros_like(dg_ref)
        dh = (_dot_nt(d1_ref[...], w_ref[:, 0:2 * D]) + _dot_nt(d2_ref[...].astype(BF16), w_ref[:, 2 * D:2 * D + S5_W])
              + _dot_nt(d3_ref[...], w_ref[:, 2 * D + S5_W:]))
        dx, dg = _rms_bwd(x_ref[...], g_ref[...], dh)
        dg_ref[...] += dg
        gx_ref[...] = dx1_ref[...] + dx

    return _rowcall("b_in", body, n, tm,
                    [(dzgm, 'row'), (dzs5, 'row'), (dzg, 'row'), (dx1, 'row'), (x, 'row'), (w_in, 'full'),
                     (g_pre, 'full')],
                    [('row', D, F32), ('acc', (1, D), F32)], comm=comm)


def _whole(name, body, ins, outs):
    return pl.pallas_call(body, name=name, out_shape=[jax.ShapeDtypeStruct(s, dt) for s, dt in outs],
                          compiler_params=_params())(*ins)


def _s5_disc_fwd(lr, li, ls, br, bi):
    def body(lr_ref, li_ref, ls_ref, br_ref, bi_ref, o1, o2, o3, o4):
        outs = _s5_disc(lr_ref[...], li_ref[...], ls_ref[...], br_ref[...], bi_ref[...])
        for o, val in zip((o1, o2, o3, o4), outs):
            o[...] = val

    return _whole("s5_disc_fwd", body, [lr, li, ls, br, bi],
                  [(lr.shape, F32), (lr.shape, F32), (br.shape, F32), (br.shape, F32)])


def _s5_disc_bwd(lr, li, ls, br, bi, cts):
    def body(lr_ref, li_ref, ls_ref, br_ref, bi_ref, c1, c2, c3, c4, o1, o2, o3, o4, o5):
        _, vjp = jax.vjp(_s5_disc, lr_ref[...], li_ref[...], ls_ref[...], br_ref[...], bi_ref[...])
        grads = vjp((c1[...], c2[...], c3[...], c4[...]))
        for o, val in zip((o1, o2, o3, o4, o5), grads):
            o[...] = val

    return _whole("s5_disc_bwd", body, [lr, li, ls, br, bi, *cts],
                  [(lr.shape, F32), (lr.shape, F32), (lr.shape, F32), (br.shape, F32), (br.shape, F32)])


def _sum_slots(name, slots):
    def body(s_ref, o_ref):
        acc = s_ref[0]
        for k in range(1, slots.shape[0]):
            acc = acc + s_ref[k]
        o_ref[...] = acc

    return _whole(name, body, [slots], [(slots.shape[1:], F32)])[0]


def _adamw_math(w, g, m, v):
    c1 = 1.0 - ADAM_B1 ** ADAM_STEP
    c2 = 1.0 - ADAM_B2 ** ADAM_STEP
    mn = ADAM_B1 * m + (1.0 - ADAM_B1) * g
    vn = ADAM_B2 * v + (1.0 - ADAM_B2) * (g * g)
    return -ADAM_LR * ((mn / c1) / (jnp.sqrt(vn / c2) + ADAM_EPS) + ADAM_WD * w), mn, vn


def _adamw_many(name, ws, gs, ms, vs):
    k = len(ws)

    def body(*refs):
        for i in range(k):
            dl, mn, vn = _adamw_math(refs[i][...], refs[k + i][...], refs[2 * k + i][...], refs[3 * k + i][...])
            refs[4 * k + i][...] = dl
            refs[5 * k + i][...] = mn
            refs[6 * k + i][...] = vn

    res = _whole(name, body, [*ws, *gs, *ms, *vs], [(t.shape, F32) for t in ws] * 3)
    return res[:k], res[k:2 * k], res[2 * k:]


def _adamw(name, w, g, m, v, tm):
    def body(w_ref, g_ref, m_ref, v_ref, d_ref, mo_ref, vo_ref):
        d_ref[...], mo_ref[...], vo_ref[...] = _adamw_math(w_ref[...], g_ref[...], m_ref[...], v_ref[...])

    cols = w.shape[1]
    return _rowcall(name, body, w.shape[0], tm, [(w, 'row'), (g, 'row'), (m, 'row'), (v, 'row')],
                    [('row', cols, F32), ('row', cols, F32), ('row', cols, F32)])


def _picked_rowcall(name, body, grid, in_specs, out_cols, tm):
    return pl.pallas_call(
        lambda p_ref, *refs: body(*refs), name=name,
        grid_spec=pltpu.PrefetchScalarGridSpec(
            num_scalar_prefetch=1, grid=(grid,), in_specs=in_specs,
            out_specs=[pl.BlockSpec((tm, D), lambda i, p: (i, 0)) for _ in out_cols]),
        out_shape=[jax.ShapeDtypeStruct((grid * tm, D), dt) for dt in out_cols],
        compiler_params=_params(("arbitrary",)),
    )


def _exchange(copies):
    def build(src_refs, out_refs, ss, rs):
        def rdma(k, src, dst, peer):
            return _Lazy(lambda: pltpu.make_async_remote_copy(
                src_ref=src, dst_ref=dst, send_sem=ss.at[k], recv_sem=rs.at[k], device_id=peer, device_id_type=MESH))
        return copies(src_refs, out_refs, rdma)

    def start(*refs):
        for cp in build(*refs):
            cp.start()

    def finish(*refs):
        for cp in build(*refs):
            cp.wait()

    return start, finish


def _reduce_big(tag, slabs):
    rows = [s.shape[1] for s in slabs]
    blk = sum(rows) // 4
    tm = _tile_rows(blk, 512)
    per = blk // tm
    gpack = jnp.concatenate([s.reshape(4, 2, 2, r // 4, D) for s, r in zip(slabs, rows)], axis=3)

    mx, my, mc = lax.axis_index("x"), lax.axis_index("y"), lax.axis_index("c")
    chip_me, chip_xn, chip_yn = 2 * mx + my, 2 * (1 - mx) + my, 2 * mx + (1 - my)

    def pair_copies(srcs, dsts, rdma):
        c, _, (_, _, psib) = _place()
        return [rdma(0, srcs[0].at[:, 1 - c], dsts[0], psib)]

    from_sib = (yield _Comm([gpack], [((4, 2, blk, D), F32)], 1, _exchange(pair_copies), {}, (0.0, 1.0)))[0]

    def sum_both(a_ref, b_ref, o32_ref, o16_ref):
        s = a_ref[...].astype(F32) + b_ref[...].astype(F32)
        o32_ref[...] = s
        o16_ref[...] = s.astype(BF16)

    p32, p16 = _picked_rowcall(
        "sum_pair_" + tag, sum_both, 8 * per,
        [pl.BlockSpec((None, None, None, tm, D), lambda i, p: (i // (2 * per), p[0], (i // per) % 2, i % per, 0)),
         pl.BlockSpec((tm, D), lambda i, p: (i, 0))],
        [F32, BF16], tm)(jnp.stack([mc]).astype(jnp.int32), gpack, from_sib.reshape(8 * blk, D))
    p32, p16 = p32.reshape(4, 2, blk, D), p16.reshape(4, 2, blk, D)

    def step1_copies(srcs, dsts, rdma):
        _, (me, xn, yn, dg), (px, py, _) = _place()
        s16 = srcs[0]
        return [rdma(0, s16.at[yn, 0], dsts[0].at[0], py), rdma(1, s16.at[dg, 0], dsts[0].at[1], py),
                rdma(2, s16.at[xn, 1], dsts[0].at[2], px), rdma(3, s16.at[dg, 1], dsts[0].at[3], px)]

    recv1 = (yield _Comm([p16], [((4, blk, D), BF16)], 4, _exchange(step1_copies), {}, (0.0, 1.0)))[0]

    s32, s16 = _picked_rowcall(
        "sum_step1_" + tag, sum_both, 4 * per,
        [pl.BlockSpec((None, None, tm, D), lambda i, p: (p[i // per], i // (2 * per), i % per, 0)),
         pl.BlockSpec((tm, D), lambda i, p: (i, 0))],
        [F32, BF16], tm)(jnp.stack([chip_me, chip_xn, chip_me, chip_yn]).astype(jnp.int32), p32,
                         recv1.reshape(4 * blk, D))

    def step2_copies(srcs, dsts, rdma):
        _, _, (px, py, _) = _place()
        return [rdma(0, srcs[0].at[1], dsts[0].at[0], px), rdma(1, srcs[0].at[3], dsts[0].at[1], py)]

    recv2 = (yield _Comm([s16.reshape(4, blk, D)], [((2, blk, D), BF16)], 2, _exchange(step2_copies), {},
                         (0.0, 1.0)))[0]

    def sum_step2(a_ref, b_ref, o_ref):
        o_ref[...] = a_ref[...] + b_ref[...].astype(F32)

    red = _rowcall("sum_step2_" + tag, sum_step2, 2 * blk, tm,
                   [(s32.reshape(4, blk, D), pl.BlockSpec((None, tm, D), lambda i: (2 * (i // per), i % per, 0))),
                    (recv2.reshape(2 * blk, D), 'row')],
                   [('row', D, F32)])[0].reshape(2, blk, D)

    def share_copies(srcs, dsts, rdma):
        _, _, (_, _, psib) = _place()
        return [rdma(0, srcs[0], dsts[0], psib)]

    other = (yield _Comm([red], [((2, blk, D), F32)], 1, _exchange(share_copies), {}, (0.0, 1.0)))[0]
    lo = jnp.where(mc == 0, red, other)
    hi = jnp.where(mc == 0, other, red)
    out, off = [], 0
    for r in rows:
        q = r // 4
        out.append(jnp.concatenate([lo[0, off:off + q], lo[1, off:off + q], hi[0, off:off + q], hi[1, off:off + q]]))
        off += q
    return out


def _rows1024(a):
    flat = a.reshape(-1)
    pad = (-flat.shape[0]) % D
    if pad:
        flat = jnp.concatenate([flat, jnp.zeros((pad,), flat.dtype)])
    return flat.reshape(-1, D)


def _pack(arrs, pad_rows_to=8):
    parts = [_rows1024(a) for a in arrs]
    rows = sum(p.shape[0] for p in parts)
    pad = (-rows) % pad_rows_to
    if pad:
        parts.append(jnp.zeros((pad, D), parts[0].dtype))
    return jnp.concatenate(parts, axis=0)


def _unpack(packed, shapes):
    out, r = [], 0
    for s in shapes:
        size = math.prod(s)
        nr = -(-size // D)
        out.append(packed[r:r + nr].reshape(-1)[:size].reshape(s))
        r += nr
    return out


def _block_diag(t):
    eye = jnp.eye(8, dtype=t.dtype)
    j, g, a, b = t.shape
    return (t[:, :, :, None, :] * eye[None, :, None, :, None]).reshape(j, g * a, g * b)


def _block_diag_take(m, a, b):
    eye = jnp.eye(8, dtype=m.dtype)
    return (m.reshape(4, 8, a, 8, b) * eye[None, :, None, :, None]).sum(axis=3)


def kernel(x, mem, g_mix_pre, w_in, gm_ln_g, gm_ln_b, gm_w_s, gm_b_s, s5_lam_re, s5_lam_im, s5_log_step, s5_b_re, s5_b_im, s5_c_re, s5_c_im, s5_d, s5_w_glu, w_br_gm, w_br_s5, w_mix_out, g_mix_post, g_ca_pre, g_mem, ca_w_q, ca_w_kv, ca_w_o, g_ca_post, g_ffn_pre, ffn_w_gu, ffn_w_down, g_ffn_post, loss_target, m_g_mix_pre, m_w_in, m_gm_ln_g, m_gm_ln_b, m_gm_w_s, m_gm_b_s, m_s5_lam_re, m_s5_lam_im, m_s5_log_step, m_s5_b_re, m_s5_b_im, m_s5_c_re, m_s5_c_im, m_s5_d, m_s5_w_glu, m_w_br_gm, m_w_br_s5, m_w_mix_out, m_g_mix_post, m_g_ca_pre, m_g_mem, m_ca_w_q, m_ca_w_kv, m_ca_w_o, m_g_ca_post, m_g_ffn_pre, m_ffn_w_gu, m_ffn_w_down, m_g_ffn_post, v_g_mix_pre, v_w_in, v_gm_ln_g, v_gm_ln_b, v_gm_w_s, v_gm_b_s, v_s5_lam_re, v_s5_lam_im, v_s5_log_step, v_s5_b_re, v_s5_b_im, v_s5_c_re, v_s5_c_im, v_s5_d, v_s5_w_glu, v_w_br_gm, v_w_br_s5, v_w_mix_out, v_g_mix_post, v_g_ca_pre, v_g_mem, v_ca_w_q, v_ca_w_kv, v_ca_w_o, v_g_ca_post, v_g_ffn_pre, v_ffn_w_gu, v_ffn_w_down, v_g_ffn_post):
    a = dict(locals())
    w = {n: a[n][0] for n in WNAMES}
    nb, seq, _ = x.shape
    n = nb * seq
    mlen = mem.shape[1]
    tm = min(256, seq)
    tmb = min(512, seq)
    tpb = seq // tmb
    xf = x.reshape(n, D)
    tgt = loss_target.reshape(n, D)
    memf = mem.reshape(nb * mlen, D)
    tmm = min(256, mlen)

    big_names = list(BIG)
    local_shapes = {k: w[k].shape for k in big_names}
    w_in_full = _run_comm("gather_w_in", _gather_weights("cast_w_in", [w['w_in']], [BIG['w_in']]))[0]
    rest = [k for k in big_names if k != 'w_in']
    gather_rest = _gather_weights("cast_w_rest", [w[k] for k in rest], [BIG[k] for k in rest])

    vec = lambda name: w[name].reshape(1, D)

    to_lane = lambda p: p.reshape(1, -1)
    b_t = lambda p: p.transpose(2, 0, 1).reshape(16, -1)
    lr_l, li_l = to_lane(w['s5_lam_re']), to_lane(w['s5_lam_im'])
    ls_l = jnp.repeat(w['s5_log_step'], 64).reshape(1, -1)
    br_t, bi_t = b_t(w['s5_b_re']), b_t(w['s5_b_im'])
    ab_re, ab_im, bb_re, bb_im = _s5_disc_fwd(lr_l, li_l, ls_l, br_t, bi_t)
    blk = lambda t: _block_diag(t.reshape(16, 4, 8, 64).transpose(1, 2, 0, 3))
    bbc = jnp.concatenate([blk(bb_re), blk(bb_im)], axis=2).astype(BF16)
    cblk = lambda c: _block_diag(c.reshape(4, 8, 16, 64).transpose(0, 1, 3, 2))
    ccm = jnp.concatenate([cblk(w['s5_c_re']), -cblk(w['s5_c_im'])], axis=1).astype(BF16)
    avec = jnp.concatenate([ab_re.reshape(4, 1, 512), ab_im.reshape(4, 1, 512)], axis=2)
    dvec = w['s5_d'].reshape(4, 1, LANE)

    bs3 = w['gm_b_s'].reshape(GM_GROUPS, GM_CHUNK, 1)
    wf = {'w_in': w_in_full}
    h, zgm, zs5, zg = _f_in(xf, vec('g_mix_pre'), wf['w_in'], tmb)
    ygm = _f_gmlp(zgm, vec('gm_ln_g'), vec('gm_ln_b'), w['gm_w_s'], bs3, tmb)
    s5_t = min(S5_T, seq)
    (ypre, states), gathered = _f_s5(zs5, bbc, ccm, avec, dvec, nb, s5_t, gather_rest)
    for k, g in zip(rest, gathered):
        wf[k] = g.reshape(-1, g.shape[-1]) if BIG[k] == 0 else g
    yg, ys5, a_br, b_br, merged, mo, x1 = _f_mix(ygm, ypre, zg, xf, wf['s5_w_glu'], wf['w_br_gm'], wf['w_br_s5'],
                                                 wf['w_mix_out'], vec('g_mix_post'), tmb)
    mem_n, kk, vv = _f_mem(memf, vec('g_mem'), wf['ca_w_kv'], tmm)
    hc, q_att, o_att, ao, x2 = _f_attn(x1, vec('g_ca_pre'), wf['ca_w_q'], kk, vv, wf['ca_w_o'], vec('g_ca_post'),
                                tmb, tpb, mlen)
    hf, gu, act = _f_ffn_up(x2, vec('g_ffn_pre'), wf['ffn_w_gu'], tmb)
    dn, d3, loss_cols = _f_ffn_down(act, x2, tgt, wf['ffn_w_down'], vec('g_ffn_post'), tmb)

    gsm = {}
    gbig = {}
    ddn, dgu, gsm['g_ffn_post'] = _b_ffn_down(d3, dn, gu, vec('g_ffn_post'), wf['ffn_w_down'], tmb)
    dx2, gsm['g_ffn_pre'] = _b_ffn_up(dgu, d3, x2, wf['ffn_w_gu'], vec('g_ffn_pre'), tmb)
    gbig['ffn_w_down'] = _mm_tn("dw_ffn_down", act, ddn)
    gbig['ffn_w_gu'] = _mm_tn("dw_ffn_gu", dgu, hf)
    dao, dq, dx1, dk, dv, gsm['g_ca_post'], gsm['g_ca_pre'] = _b_attn(
        dx2, ao, x1, q_att, vec('g_ca_post'), wf['ca_w_o'], wf['ca_w_q'], vec('g_ca_pre'), kk, vv, tmb, tpb, mlen)
    gbig['ca_w_o'] = _mm_tn("dw_ca_o", o_att, dao)
    gbig['ca_w_q'] = _mm_tn("dw_ca_q", hc, dq)
    dkv, gsm['g_mem'] = _b_mem(dk, dv, memf, wf['ca_w_kv'], vec('g_mem'), tmm)
    gbig['ca_w_kv'] = _mm_tn("dw_ca_kv", dkv, mem_n)
    dmo, da_br, db_br, dzg, dygm, dgate, dypre, gsm['g_mix_post'] = _b_mix(
        dx1, mo, a_br, b_br, zg, ypre, vec('g_mix_post'), wf['w_mix_out'], wf['w_br_gm'], wf['w_br_s5'],
        wf['s5_w_glu'], tmb)
    gbig['w_mix_out'] = _mm_tn("dw_mix_out", merged, dmo)
    gbig['w_br_gm'] = _mm_tn("dw_br_gm", ygm, da_br)
    gbig['w_br_s5'] = _mm_tn("dw_br_s5", db_br, ys5)
    gbig['s5_w_glu'] = _mm_tn("dw_s5_glu", yg, dgate)
    slab_of = lambda k: gbig[k].reshape(4, -1, D)
    red_rest = _reduce_big("rest", [slab_of(k) for k in rest])
    s5_tb = max(s5_t, min(S5_TB, seq))
    (dzs5, dbbc, dccm_t, davec, ddvec), got = _b_s5(zs5, dypre, states[:, :, ::s5_tb // s5_t], bbc, ccm, avec, dvec,
                                                    nb, s5_tb, next(red_rest))
    dccm = dccm_t.transpose(0, 2, 1)
    (dzgm, gsm['gm_w_s'], dbs3, gsm['gm_ln_g'], gsm['gm_ln_b']), got = _b_gmlp(
        zgm, dygm, vec('gm_ln_g'), vec('gm_ln_b'), w['gm_w_s'], bs3, tmb, red_rest.send(got))
    gsm['gm_b_s'] = dbs3
    grad_x, gsm['g_mix_pre'] = _b_in(dzgm, dzs5, dzg, dx1, xf, wf['w_in'], vec('g_mix_pre'), tmb, None)
    dw_in_gm, got = _mm_tn("dw_in_gm", dzgm, h, comm=red_rest.send(got))
    dw_in_s5, got = _mm_tn("dw_in_s5", dzs5, h, comm=red_rest.send(got))
    greds = dict(zip(rest, _finish(red_rest, got)))

    unblk = lambda m_: _block_diag_take(m_, 16, 64).transpose(2, 0, 1, 3).reshape(16, -1)
    d_bb_re, d_bb_im = unblk(dbbc[:, :, :512]), unblk(dbbc[:, :, 512:])
    cunblk = lambda m_: _block_diag_take(m_, 64, 16).transpose(0, 1, 3, 2).reshape(32, 16, 64)
    gsm['s5_c_re'] = cunblk(dccm[:, :512, :])
    gsm['s5_c_im'] = -cunblk(dccm[:, 512:, :])
    d_ab_re, d_ab_im = davec[:, :, :512].reshape(1, -1), davec[:, :, 512:].reshape(1, -1)
    g_lr, g_li, g_ls, g_br, g_bi = _s5_disc_bwd(lr_l, li_l, ls_l, br_t, bi_t, (d_ab_re, d_ab_im, d_bb_re, d_bb_im))
    gsm['s5_lam_re'], gsm['s5_lam_im'] = g_lr.reshape(32, 64), g_li.reshape(32, 64)
    gsm['s5_log_step'] = g_ls.reshape(32, 64).sum(axis=1)
    from_t = lambda t: t.reshape(16, 32, 64).transpose(1, 2, 0)
    gsm['s5_b_re'], gsm['s5_b_im'] = from_t(g_br), from_t(g_bi)
    gsm['s5_d'] = ddvec.reshape(32, 16)

    small_shapes = [w[k].shape for k in SMALL]
    spack = _pack([gsm[k].reshape(w[k].shape) for k in SMALL] + [loss_cols], 8)
    me_slot = 4 * lax.axis_index("x") + 2 * lax.axis_index("y") + lax.axis_index("c")
    dw_in_g, (sall,) = _mm_tn("dw_in_g", dzg, h, comm=_gather_all(spack))
    gbig['w_in'] = jnp.concatenate([dw_in_gm, dw_in_s5, dw_in_g], axis=0)
    sall = lax.dynamic_update_index_in_dim(sall, spack, me_slot, 0)
    ssum = _sum_slots("sum_small", sall)
    small_red = _unpack(ssum, small_shapes + [(1, D)])
    loss = 0.5 * jnp.sum(small_red[-1]) / D

    red_in = _reduce_big("w_in", [slab_of('w_in')])
    got = _run_comm("reduce_pair_w_in", next(red_in))
    for phase in ("reduce_step1_w_in", "reduce_step2_w_in"):
        got = _run_comm(phase, red_in.send(got))
    got = _run_comm("share_pair_w_in", red_in.send(got))
    greds['w_in'] = _finish(red_in, got)[0]

    res_big = {}
    for k in big_names:
        r_, c_ = local_shapes[k]
        g = greds[k].reshape(r_, c_) if BIG[k] == 0 else greds[k].reshape(c_, r_).T
        dl, mn, vn = _adamw("adamw_" + k, w[k], g, a['m_' + k][0], a['v_' + k][0], _tile_rows(g.shape[0], 256))
        res_big[k] = (g, dl, mn, vn)
    as2d = lambda t: t.reshape(1, -1) if t.ndim == 1 else t
    dl_s, mn_s, vn_s = _adamw_many("adamw_small", [as2d(w[k]) for k in SMALL], [as2d(g) for g in small_red[:-1]],
                                   [as2d(a['m_' + k][0]) for k in SMALL], [as2d(a['v_' + k][0]) for k in SMALL])
    res_small = {k: (g, dl.reshape(w[k].shape), mn.reshape(w[k].shape), vn.reshape(w[k].shape))
                 for k, g, dl, mn, vn in zip(SMALL, small_red[:-1], dl_s, mn_s, vn_s)}

    res = {**res_big, **res_small}
    outs = [loss, grad_x.reshape(nb, seq, D)]
    for i in range(4):
        outs += [res[k][i][None] for k in WNAMES]
    return tuple(outs)


def _finish(gen, got):
    try:
        gen.send(got)
    except StopIteration as done:
        return done.value
    raise AssertionError("the generator has more exchanges")


def _tile_rows(rows, cap=384):
    best = rows
    for t in range(16, min(rows, cap) + 1, 16):
        if rows % t == 0:
            best = t
    return best
```
